```python
import math
import jax, jax.numpy as jnp
from jax import lax
import numpy as np

D_MODEL = 1024
BATCH = 8
SEQ = 8192
DEPTH = 2

MLA_HEADS = 8
MLA_NOPE_DIM = 128
MLA_ROPE_DIM = 64
MLA_V_DIM = 128
MLA_Q_RANK = 384
MLA_KV_RANK = 256
ROPE_THETA = 10000.0
SWA_Q_HEADS = 16
SWA_KV_HEADS = 4
SWA_HEAD_DIM = 64
WINDOW = 128
REL_BUCKETS = 32
REL_MAX_DIST = 128
D_FF = 4 * D_MODEL
BLOCK = 128
LN_EPS = 1e-5
RMS_EPS = 1e-6

kernel_name = 'yoco_mla_swa_sink_t5_deepnorm'


def _layernorm(x, g, b):
    xf = x.astype(jnp.float32)
    mu = xf.mean(-1, keepdims=True)
    var = jnp.square(xf - mu).mean(-1, keepdims=True)
    return ((xf - mu) * lax.rsqrt(var + LN_EPS) * g + b).astype(x.dtype)


def _rmsnorm(x, g):
    xf = x.astype(jnp.float32)
    return (xf * lax.rsqrt(jnp.mean(xf * xf, -1, keepdims=True) + RMS_EPS) * g).astype(x.dtype)


def _rope(x, pos):
    half = x.shape[-1] // 2
    inv = ROPE_THETA ** (-jnp.arange(half, dtype=jnp.float32) / half)
    ang = pos.astype(jnp.float32)[:, None] * inv[None, :]
    ang = ang.reshape(ang.shape[:1] + (1,) * (x.ndim - 3) + ang.shape[1:])
    cos, sin = jnp.cos(ang), jnp.sin(ang)
    xf = x.astype(jnp.float32)
    x1, x2 = xf[..., :half], xf[..., half:]
    return jnp.concatenate([x1 * cos - x2 * sin, x1 * sin + x2 * cos], -1).astype(x.dtype)


def _t5_bucket(dist):
    n = jnp.maximum(dist, 0)
    max_exact = REL_BUCKETS // 2
    nf = jnp.maximum(n, 1).astype(jnp.float32)
    large = max_exact + (jnp.log(nf / max_exact) / math.log(REL_MAX_DIST / max_exact)
                         * (REL_BUCKETS - max_exact)).astype(jnp.int32)
    large = jnp.minimum(large, REL_BUCKETS - 1)
    return jnp.where(n < max_exact, n, large)


def _to_blocks(t, nb):
    return t.reshape((t.shape[0], nb, BLOCK) + t.shape[2:]).swapaxes(0, 1)


def _mla(x, w_in, g_q, g_kv, w_uq, w_uk, w_uv, w_o, pos):
    B, S, _ = x.shape
    nb = S // BLOCK
    h = x @ w_in
    c_q = _rmsnorm(h[..., :MLA_Q_RANK], g_q)
    c_kv = _rmsnorm(h[..., MLA_Q_RANK:MLA_Q_RANK + MLA_KV_RANK], g_kv)
    k_r = _rope(h[..., MLA_Q_RANK + MLA_KV_RANK:], pos)
    q = jnp.einsum('bsr,rhd->bshd', c_q, w_uq)
    q_n = q[..., :MLA_NOPE_DIM]
    q_r = _rope(q[..., MLA_NOPE_DIM:], pos)
    q_lat = jnp.einsum('bshn,chn->bshc', q_n, w_uk)
    scale = (MLA_NOPE_DIM + MLA_ROPE_DIM) ** -0.5
    kpos = jnp.arange(S)

    def block(args):
        ql, qr, i = args
        s = (jnp.einsum('bqhc,bkc->bhqk', ql, c_kv)
             + jnp.einsum('bqhr,bkr->bhqk', qr, k_r)).astype(jnp.float32) * scale
        qpos = i * BLOCK + jnp.arange(BLOCK)
        s = jnp.where(kpos[None, :] <= qpos[:, None], s, -jnp.inf)
        p = jax.nn.softmax(s, axis=-1).astype(x.dtype)
        return jnp.einsum('bhqk,bkc->bqhc', p, c_kv)

    o_lat = lax.map(block, (_to_blocks(q_lat, nb), _to_blocks(q_r, nb), jnp.arange(nb)))
    o_lat = o_lat.swapaxes(0, 1).reshape(B, S, MLA_HEADS, MLA_KV_RANK)
    o = jnp.einsum('bshc,chv->bshv', o_lat, w_uv).reshape(B, S, MLA_HEADS * MLA_V_DIM)
    return o @ w_o


def _shared_kv(x, w_kv):
    B, S, _ = x.shape
    nb = S // BLOCK
    kv = (x @ w_kv).reshape(B, nb, BLOCK, 2, SWA_KV_HEADS, SWA_HEAD_DIM)
    k, v = kv[:, :, :, 0], kv[:, :, :, 1]

    def band(t):
        prev = jnp.pad(t, ((0, 0), (1, 0), (0, 0), (0, 0), (0, 0)))[:, :-1]
        return jnp.concatenate([prev, t], axis=2)

    return band(k), band(v)


def _swa(x, w_q, sinks, w_o, k_band, v_band, rel_bias):
    B, S, _ = x.shape
    nb = S // BLOCK
    G = SWA_Q_HEADS // SWA_KV_HEADS
    q = (x @ w_q).reshape(B, nb, BLOCK, SWA_KV_HEADS, G, SWA_HEAD_DIM)
    s = jnp.einsum('bnqkgd,bnjkd->bnkgqj', q, k_band).astype(jnp.float32) * SWA_HEAD_DIM ** -0.5
    i = jnp.arange(BLOCK)
    j = jnp.arange(2 * BLOCK)
    dist = i[:, None] + BLOCK - j[None, :]
    bias = rel_bias[_t5_bucket(dist)].astype(jnp.float32)
    bias = bias.transpose(2, 0, 1).reshape(SWA_KV_HEADS, G, BLOCK, 2 * BLOCK)
    kpos = jnp.arange(nb)[:, None] * BLOCK - BLOCK + j[None, :]
    valid = (dist >= 0) & (dist < WINDOW)
    mask = valid[None] & (kpos >= 0)[:, None, :]
    s = jnp.where(mask[None, :, None, None], s + bias, -jnp.inf)
    sink = sinks.astype(jnp.float32).reshape(SWA_KV_HEADS, G)[..., None]
    m = jnp.maximum(s.max(-1), sink)
    p = jnp.exp(s - m[..., None])
    denom = p.sum(-1) + jnp.exp(sink - m)
    p = (p / denom[..., None]).astype(x.dtype)
    o = jnp.einsum('bnkgqj,bnjkd->bnqkgd', p, v_band).reshape(B, S, SWA_Q_HEADS * SWA_HEAD_DIM)
    return o @ w_o


def _mlp(x, w_up, w_down):
    h = jax.nn.relu(x @ w_up)
    return (h * h) @ w_down


def _fwd_setup_inputs(seed: int = 0) -> dict:
    key = jax.random.key(seed)
    ks = jax.random.split(key, 20)
    n_a = DEPTH // 2
    n_b = DEPTH - n_a
    beta = (8 * DEPTH) ** -0.25
    f32 = jnp.float32

    def nrm(k, shape, fan_in, gain=1.0):
        return jax.random.normal(k, shape, f32) * (gain * fan_in ** -0.5)

    in_w = MLA_Q_RANK + MLA_KV_RANK + MLA_ROPE_DIM
    kv_k = nrm(ks[8], (D_MODEL, 1, SWA_KV_HEADS * SWA_HEAD_DIM), D_MODEL)
    kv_v = nrm(ks[9], (D_MODEL, 1, SWA_KV_HEADS * SWA_HEAD_DIM), D_MODEL, beta)
    return {
        'x': jax.random.normal(ks[0], (BATCH, SEQ, D_MODEL), f32),
        'mla_w_in': nrm(ks[1], (n_a, D_MODEL, in_w), D_MODEL),
        'mla_g_q': 1.0 + 0.05 * jax.random.normal(ks[2], (n_a, MLA_Q_RANK), f32),
        'mla_g_kv': 1.0 + 0.05 * jax.random.normal(ks[3], (n_a, MLA_KV_RANK), f32),
        'mla_w_uq': nrm(ks[4], (n_a, MLA_Q_RANK, MLA_HEADS, MLA_NOPE_DIM + MLA_ROPE_DIM), MLA_Q_RANK),
        'mla_w_uk': nrm(ks[5], (n_a, MLA_KV_RANK, MLA_HEADS, MLA_NOPE_DIM), MLA_KV_RANK),
        'mla_w_uv': nrm(ks[6], (n_a, MLA_KV_RANK, MLA_HEADS, MLA_V_DIM), MLA_KV_RANK, beta),
        'mla_w_o': nrm(ks[7], (n_a, MLA_HEADS * MLA_V_DIM, D_MODEL), MLA_HEADS * MLA_V_DIM, beta),
        'kv_w_shared': jnp.concatenate([kv_k, kv_v], axis=1).reshape(D_MODEL, 2 * SWA_KV_HEADS * SWA_HEAD_DIM),
        'swa_w_q': nrm(ks[10], (n_b, D_MODEL, SWA_Q_HEADS * SWA_HEAD_DIM), D_MODEL),
        'swa_sinks': 0.5 * jax.random.normal(ks[11], (n_b, SWA_Q_HEADS), f32),
        'swa_w_o': nrm(ks[12], (n_b, SWA_Q_HEADS * SWA_HEAD_DIM, D_MODEL), SWA_Q_HEADS * SWA_HEAD_DIM, beta),
        'rel_bias': 0.5 * jax.random.normal(ks[13], (REL_BUCKETS, SWA_Q_HEADS), f32),
        'mlp_w_up': nrm(ks[14], (DEPTH, D_MODEL, D_FF), D_MODEL),
        'mlp_w_down': nrm(ks[15], (DEPTH, D_FF, D_MODEL), D_FF, beta),
        'ln_mix_g': 1.0 + 0.05 * jax.random.normal(ks[16], (DEPTH, D_MODEL), f32),
        'ln_mix_b': 0.02 * jax.random.normal(ks[17], (DEPTH, D_MODEL), f32),
        'ln_mlp_g': 1.0 + 0.05 * jax.random.normal(ks[18], (DEPTH, D_MODEL), f32),
        'ln_mlp_b': 0.02 * jax.random.normal(ks[19], (DEPTH, D_MODEL), f32),
    }


def _fwd_reference(x, mla_w_in, mla_g_q, mla_g_kv, mla_w_uq, mla_w_uk, mla_w_uv, mla_w_o,
              kv_w_shared, swa_w_q, swa_sinks, swa_w_o, rel_bias,
              mlp_w_up, mlp_w_down, ln_mix_g, ln_mix_b, ln_mlp_g, ln_mlp_b):
    alpha = (2 * DEPTH) ** 0.25
    n_a = DEPTH // 2
    pos = jnp.arange(x.shape[1])
    k_band = v_band = None
    for l in range(DEPTH):
        if l < n_a:
            y = _mla(x, mla_w_in[l], mla_g_q[l], mla_g_kv[l], mla_w_uq[l], mla_w_uk[l],
                     mla_w_uv[l], mla_w_o[l], pos)
        else:
            if l == n_a:
                k_band, v_band = _shared_kv(x, kv_w_shared)
            b = l - n_a
            y = _swa(x, swa_w_q[b], swa_sinks[b], swa_w_o[b], k_band, v_band, rel_bias)
        x = _layernorm(alpha * x + y, ln_mix_g[l], ln_mix_b[l])
        x = _layernorm(alpha * x + _mlp(x, mlp_w_up[l], mlp_w_down[l]), ln_mlp_g[l], ln_mlp_b[l])
    return x


import jax as _jax
import jax.numpy as _jnp

TWIN_FORMAT = 'train_step'
FWD_PARAMS = ['x', 'mla_w_in', 'mla_g_q', 'mla_g_kv', 'mla_w_uq', 'mla_w_uk', 'mla_w_uv', 'mla_w_o', 'kv_w_shared', 'swa_w_q', 'swa_sinks', 'swa_w_o', 'rel_bias', 'mlp_w_up', 'mlp_w_down', 'ln_mix_g', 'ln_mix_b', 'ln_mlp_g', 'ln_mlp_b']
TWIN_WEIGHTS = ['mla_w_in', 'mla_g_q', 'mla_g_kv', 'mla_w_uq', 'mla_w_uk', 'mla_w_uv', 'mla_w_o', 'kv_w_shared', 'swa_w_q', 'swa_sinks', 'swa_w_o', 'rel_bias', 'mlp_w_up', 'mlp_w_down', 'ln_mix_g', 'ln_mix_b', 'ln_mlp_g', 'ln_mlp_b']
TWIN_DIFF_INPUT = 'x'
TWIN_INPUTS = ['x', 'mla_w_in', 'mla_g_q', 'mla_g_kv', 'mla_w_uq', 'mla_w_uk', 'mla_w_uv', 'mla_w_o', 'kv_w_shared', 'swa_w_q', 'swa_sinks', 'swa_w_o', 'rel_bias', 'mlp_w_up', 'mlp_w_down', 'ln_mix_g', 'ln_mix_b', 'ln_mlp_g', 'ln_mlp_b', 'loss_target', 'm_mla_w_in', 'm_mla_g_q', 'm_mla_g_kv', 'm_mla_w_uq', 'm_mla_w_uk', 'm_mla_w_uv', 'm_mla_w_o', 'm_kv_w_shared', 'm_swa_w_q', 'm_swa_sinks', 'm_swa_w_o', 'm_rel_bias', 'm_mlp_w_up', 'm_mlp_w_down', 'm_ln_mix_g', 'm_ln_mix_b', 'm_ln_mlp_g', 'm_ln_mlp_b', 'v_mla_w_in', 'v_mla_g_q', 'v_mla_g_kv', 'v_mla_w_uq', 'v_mla_w_uk', 'v_mla_w_uv', 'v_mla_w_o', 'v_kv_w_shared', 'v_swa_w_q', 'v_swa_sinks', 'v_swa_w_o', 'v_rel_bias', 'v_mlp_w_up', 'v_mlp_w_down', 'v_ln_mix_g', 'v_ln_mix_b', 'v_ln_mlp_g', 'v_ln_mlp_b']
TWIN_OUTPUTS = ['loss', 'grad_x', 'grad_mla_w_in', 'grad_mla_g_q', 'grad_mla_g_kv', 'grad_mla_w_uq', 'grad_mla_w_uk', 'grad_mla_w_uv', 'grad_mla_w_o', 'grad_kv_w_shared', 'grad_swa_w_q', 'grad_swa_sinks', 'grad_swa_w_o', 'grad_rel_bias', 'grad_mlp_w_up', 'grad_mlp_w_down', 'grad_ln_mix_g', 'grad_ln_mix_b', 'grad_ln_mlp_g', 'grad_ln_mlp_b', 'delta_mla_w_in', 'delta_mla_g_q', 'delta_mla_g_kv', 'delta_mla_w_uq', 'delta_mla_w_uk', 'delta_mla_w_uv', 'delta_mla_w_o', 'delta_kv_w_shared', 'delta_swa_w_q', 'delta_swa_sinks', 'delta_swa_w_o', 'delta_rel_bias', 'delta_mlp_w_up', 'delta_mlp_w_down', 'delta_ln_mix_g', 'delta_ln_mix_b', 'delta_ln_mlp_g', 'delta_ln_mlp_b', 'new_m_mla_w_in', 'new_m_mla_g_q', 'new_m_mla_g_kv', 'new_m_mla_w_uq', 'new_m_mla_w_uk', 'new_m_mla_w_uv', 'new_m_mla_w_o', 'new_m_kv_w_shared', 'new_m_swa_w_q', 'new_m_swa_sinks', 'new_m_swa_w_o', 'new_m_rel_bias', 'new_m_mlp_w_up', 'new_m_mlp_w_down', 'new_m_ln_mix_g', 'new_m_ln_mix_b', 'new_m_ln_mlp_g', 'new_m_ln_mlp_b', 'new_v_mla_w_in', 'new_v_mla_g_q', 'new_v_mla_g_kv', 'new_v_mla_w_uq', 'new_v_mla_w_uk', 'new_v_mla_w_uv', 'new_v_mla_w_o', 'new_v_kv_w_shared', 'new_v_swa_w_q', 'new_v_swa_sinks', 'new_v_swa_w_o', 'new_v_rel_bias', 'new_v_mlp_w_up', 'new_v_mlp_w_down', 'new_v_ln_mix_g', 'new_v_ln_mix_b', 'new_v_ln_mlp_g', 'new_v_ln_mlp_b']
TWIN_LEAF_KINDS = {'loss': 'loss', 'grad_x': 'grad_x', 'grad_mla_w_in': 'grad_w', 'grad_mla_g_q': 'grad_w', 'grad_mla_g_kv': 'grad_w', 'grad_mla_w_uq': 'grad_w', 'grad_mla_w_uk': 'grad_w', 'grad_mla_w_uv': 'grad_w', 'grad_mla_w_o': 'grad_w', 'grad_kv_w_shared': 'grad_w', 'grad_swa_w_q': 'grad_w', 'grad_swa_sinks': 'grad_w', 'grad_swa_w_o': 'grad_w', 'grad_rel_bias': 'grad_w', 'grad_mlp_w_up': 'grad_w', 'grad_mlp_w_down': 'grad_w', 'grad_ln_mix_g': 'grad_w', 'grad_ln_mix_b': 'grad_w', 'grad_ln_mlp_g': 'grad_w', 'grad_ln_mlp_b': 'grad_w', 'delta_mla_w_in': 'delta_w', 'delta_mla_g_q': 'delta_w', 'delta_mla_g_kv': 'delta_w', 'delta_mla_w_uq': 'delta_w', 'delta_mla_w_uk': 'delta_w', 'delta_mla_w_uv': 'delta_w', 'delta_mla_w_o': 'delta_w', 'delta_kv_w_shared': 'delta_w', 'delta_swa_w_q': 'delta_w', 'delta_swa_sinks': 'delta_w', 'delta_swa_w_o': 'delta_w', 'delta_rel_bias': 'delta_w', 'delta_mlp_w_up': 'delta_w', 'delta_mlp_w_down': 'delta_w', 'delta_ln_mix_g': 'delta_w', 'delta_ln_mix_b': 'delta_w', 'delta_ln_mlp_g': 'delta_w', 'delta_ln_mlp_b': 'delta_w', 'new_m_mla_w_in': 'new_m', 'new_m_mla_g_q': 'new_m', 'new_m_mla_g_kv': 'new_m', 'new_m_mla_w_uq': 'new_m', 'new_m_mla_w_uk': 'new_m', 'new_m_mla_w_uv': 'new_m', 'new_m_mla_w_o': 'new_m', 'new_m_kv_w_shared': 'new_m', 'new_m_swa_w_q': 'new_m', 'new_m_swa_sinks': 'new_m', 'new_m_swa_w_o': 'new_m', 'new_m_rel_bias': 'new_m', 'new_m_mlp_w_up': 'new_m', 'new_m_mlp_w_down': 'new_m', 'new_m_ln_mix_g': 'new_m', 'new_m_ln_mix_b': 'new_m', 'new_m_ln_mlp_g': 'new_m', 'new_m_ln_mlp_b': 'new_m', 'new_v_mla_w_in': 'new_v', 'new_v_mla_g_q': 'new_v', 'new_v_mla_g_kv': 'new_v', 'new_v_mla_w_uq': 'new_v', 'new_v_mla_w_uk': 'new_v', 'new_v_mla_w_uv': 'new_v', 'new_v_mla_w_o': 'new_v', 'new_v_kv_w_shared': 'new_v', 'new_v_swa_w_q': 'new_v', 'new_v_swa_sinks': 'new_v', 'new_v_swa_w_o': 'new_v', 'new_v_rel_bias': 'new_v', 'new_v_mlp_w_up': 'new_v', 'new_v_mlp_w_down': 'new_v', 'new_v_ln_mix_g': 'new_v', 'new_v_ln_mix_b': 'new_v', 'new_v_ln_mlp_g': 'new_v', 'new_v_ln_mlp_b': 'new_v'}


def _forward(args):
    return _fwd_reference(*[args[k] for k in FWD_PARAMS])


def _output_shape():
    def fwd():
        inp = _fwd_setup_inputs(0)
        return _fwd_reference(*[inp[k] for k in FWD_PARAMS])
    out = _jax.eval_shape(fwd)
    return out.shape, out.dtype

N_MICROBATCH = 1
ADAM_LR = 0.001
ADAM_B1 = 0.9
ADAM_B2 = 0.999
ADAM_EPS = 1e-08
ADAM_WD = 0.01
ADAM_STEP = 10
PER_EXAMPLE_BATCH_AXIS = {'x': 0, 'loss_target': 0}
SHARED_INPUTS = []
_WEIGHT_DTYPES = {'mla_w_in': _jnp.float32, 'mla_g_q': _jnp.float32, 'mla_g_kv': _jnp.float32, 'mla_w_uq': _jnp.float32, 'mla_w_uk': _jnp.float32, 'mla_w_uv': _jnp.float32, 'mla_w_o': _jnp.float32, 'kv_w_shared': _jnp.float32, 'swa_w_q': _jnp.float32, 'swa_sinks': _jnp.float32, 'swa_w_o': _jnp.float32, 'rel_bias': _jnp.float32, 'mlp_w_up': _jnp.float32, 'mlp_w_down': _jnp.float32, 'ln_mix_g': _jnp.float32, 'ln_mix_b': _jnp.float32, 'ln_mlp_g': _jnp.float32, 'ln_mlp_b': _jnp.float32}
MOMENT_SCALE = {'mla_w_in': 1.895305e-02, 'mla_g_q': 1.450087e-02, 'mla_g_kv': 2.714606e-02, 'mla_w_uq': 6.956967e-03, 'mla_w_uk': 7.137665e-03, 'mla_w_uv': 2.111865e-02, 'mla_w_o': 2.103699e-02, 'kv_w_shared': 9.725129e-02, 'swa_w_q': 8.315028e-03, 'swa_sinks': 7.950159e-03, 'swa_w_o': 7.796726e-02, 'rel_bias': 1.135764e-02, 'mlp_w_up': 6.260068e-02, 'mlp_w_down': 4.634569e-01, 'ln_mix_g': 5.112976e+00, 'ln_mix_b': 1.483586e+00, 'ln_mlp_g': 4.598349e+01, 'ln_mlp_b': 1.030832e+01}


def _to_microbatches(a, axis):
    t = _jnp.moveaxis(a, axis, 0)
    t = t.reshape((N_MICROBATCH, t.shape[0] // N_MICROBATCH) + t.shape[1:])
    return _jnp.moveaxis(t, 1, axis + 1)


def setup_inputs(seed: int = 0) -> dict:
    inp = _fwd_setup_inputs(seed)
    key = _jax.random.fold_in(_jax.random.key(seed), 7919)
    shape, _ = _output_shape()
    out = dict(inp)
    out["loss_target"] = _jax.random.normal(_jax.random.fold_in(key, 0), shape, _jnp.float32)
    for i, name in enumerate(TWIN_WEIGHTS):
        w = inp[name].astype(_jnp.float32)
        if MOMENT_SCALE is None:
            s = _jnp.sqrt(_jnp.mean(_jnp.square(w)) + 1e-30)
        else:
            s = MOMENT_SCALE[name]
        km, kv = _jax.random.split(_jax.random.fold_in(key, i + 1))
        out[name] = w
        out["m_" + name] = s * _jax.random.normal(km, w.shape, _jnp.float32)
        out["v_" + name] = (s * s) * _jax.random.uniform(kv, w.shape, _jnp.float32, 0.5, 1.5)
    if N_MICROBATCH > 1:
        for name, axis in PER_EXAMPLE_BATCH_AXIS.items():
            out[name] = _to_microbatches(out[name], axis)
    return {'x': out['x'], 'mla_w_in': out['mla_w_in'], 'mla_g_q': out['mla_g_q'], 'mla_g_kv': out['mla_g_kv'], 'mla_w_uq': out['mla_w_uq'], 'mla_w_uk': out['mla_w_uk'], 'mla_w_uv': out['mla_w_uv'], 'mla_w_o': out['mla_w_o'], 'kv_w_shared': out['kv_w_shared'], 'swa_w_q': out['swa_w_q'], 'swa_sinks': out['swa_sinks'], 'swa_w_o': out['swa_w_o'], 'rel_bias': out['rel_bias'], 'mlp_w_up': out['mlp_w_up'], 'mlp_w_down': out['mlp_w_down'], 'ln_mix_g': out['ln_mix_g'], 'ln_mix_b': out['ln_mix_b'], 'ln_mlp_g': out['ln_mlp_g'], 'ln_mlp_b': out['ln_mlp_b'], 'loss_target': out['loss_target'], 'm_mla_w_in': out['m_mla_w_in'], 'm_mla_g_q': out['m_mla_g_q'], 'm_mla_g_kv': out['m_mla_g_kv'], 'm_mla_w_uq': out['m_mla_w_uq'], 'm_mla_w_uk': out['m_mla_w_uk'], 'm_mla_w_uv': out['m_mla_w_uv'], 'm_mla_w_o': out['m_mla_w_o'], 'm_kv_w_shared': out['m_kv_w_shared'], 'm_swa_w_q': out['m_swa_w_q'], 'm_swa_sinks': out['m_swa_sinks'], 'm_swa_w_o': out['m_swa_w_o'], 'm_rel_bias': out['m_rel_bias'], 'm_mlp_w_up': out['m_mlp_w_up'], 'm_mlp_w_down': out['m_mlp_w_down'], 'm_ln_mix_g': out['m_ln_mix_g'], 'm_ln_mix_b': out['m_ln_mix_b'], 'm_ln_mlp_g': out['m_ln_mlp_g'], 'm_ln_mlp_b': out['m_ln_mlp_b'], 'v_mla_w_in': out['v_mla_w_in'], 'v_mla_g_q': out['v_mla_g_q'], 'v_mla_g_kv': out['v_mla_g_kv'], 'v_mla_w_uq': out['v_mla_w_uq'], 'v_mla_w_uk': out['v_mla_w_uk'], 'v_mla_w_uv': out['v_mla_w_uv'], 'v_mla_w_o': out['v_mla_w_o'], 'v_kv_w_shared': out['v_kv_w_shared'], 'v_swa_w_q': out['v_swa_w_q'], 'v_swa_sinks': out['v_swa_sinks'], 'v_swa_w_o': out['v_swa_w_o'], 'v_rel_bias': out['v_rel_bias'], 'v_mlp_w_up': out['v_mlp_w_up'], 'v_mlp_w_down': out['v_mlp_w_down'], 'v_ln_mix_g': out['v_ln_mix_g'], 'v_ln_mix_b': out['v_ln_mix_b'], 'v_ln_mlp_g': out['v_ln_mlp_g'], 'v_ln_mlp_b': out['v_ln_mlp_b']}


def _loss(weights, diff, rest, loss_target):
    with _jax.named_scope("forward"):
        args = {**rest, TWIN_DIFF_INPUT: diff, **{k: w.astype(_WEIGHT_DTYPES[k]) for k, w in weights.items()}}
        y = _forward(args)
    with _jax.named_scope("loss_head"):
        err = _jnp.square(y.astype(_jnp.float32) - loss_target)
        return 0.5 * _jnp.sum(_jnp.mean(err, axis=-1)) if err.ndim else 0.5 * err


def _adamw(w, g, m, v):
    m = ADAM_B1 * m + (1.0 - ADAM_B1) * g
    v = ADAM_B2 * v + (1.0 - ADAM_B2) * _jnp.square(g)
    m_hat = m / (1.0 - ADAM_B1 ** ADAM_STEP)
    v_hat = v / (1.0 - ADAM_B2 ** ADAM_STEP)
    delta = -ADAM_LR * (m_hat / (_jnp.sqrt(v_hat) + ADAM_EPS) + ADAM_WD * w)
    return delta, m, v


def reference(x, mla_w_in, mla_g_q, mla_g_kv, mla_w_uq, mla_w_uk, mla_w_uv, mla_w_o, kv_w_shared, swa_w_q, swa_sinks, swa_w_o, rel_bias, mlp_w_up, mlp_w_down, ln_mix_g, ln_mix_b, ln_mlp_g, ln_mlp_b, loss_target, m_mla_w_in, m_mla_g_q, m_mla_g_kv, m_mla_w_uq, m_mla_w_uk, m_mla_w_uv, m_mla_w_o, m_kv_w_shared, m_swa_w_q, m_swa_sinks, m_swa_w_o, m_rel_bias, m_mlp_w_up, m_mlp_w_down, m_ln_mix_g, m_ln_mix_b, m_ln_mlp_g, m_ln_mlp_b, v_mla_w_in, v_mla_g_q, v_mla_g_kv, v_mla_w_uq, v_mla_w_uk, v_mla_w_uv, v_mla_w_o, v_kv_w_shared, v_swa_w_q, v_swa_sinks, v_swa_w_o, v_rel_bias, v_mlp_w_up, v_mlp_w_down, v_ln_mix_g, v_ln_mix_b, v_ln_mlp_g, v_ln_mlp_b):
    given = dict(x=x, mla_w_in=mla_w_in, mla_g_q=mla_g_q, mla_g_kv=mla_g_kv, mla_w_uq=mla_w_uq, mla_w_uk=mla_w_uk, mla_w_uv=mla_w_uv, mla_w_o=mla_w_o, kv_w_shared=kv_w_shared, swa_w_q=swa_w_q, swa_sinks=swa_sinks, swa_w_o=swa_w_o, rel_bias=rel_bias, mlp_w_up=mlp_w_up, mlp_w_down=mlp_w_down, ln_mix_g=ln_mix_g, ln_mix_b=ln_mix_b, ln_mlp_g=ln_mlp_g, ln_mlp_b=ln_mlp_b, loss_target=loss_target, m_mla_w_in=m_mla_w_in, m_mla_g_q=m_mla_g_q, m_mla_g_kv=m_mla_g_kv, m_mla_w_uq=m_mla_w_uq, m_mla_w_uk=m_mla_w_uk, m_mla_w_uv=m_mla_w_uv, m_mla_w_o=m_mla_w_o, m_kv_w_shared=m_kv_w_shared, m_swa_w_q=m_swa_w_q, m_swa_sinks=m_swa_sinks, m_swa_w_o=m_swa_w_o, m_rel_bias=m_rel_bias, m_mlp_w_up=m_mlp_w_up, m_mlp_w_down=m_mlp_w_down, m_ln_mix_g=m_ln_mix_g, m_ln_mix_b=m_ln_mix_b, m_ln_mlp_g=m_ln_mlp_g, m_ln_mlp_b=m_ln_mlp_b, v_mla_w_in=v_mla_w_in, v_mla_g_q=v_mla_g_q, v_mla_g_kv=v_mla_g_kv, v_mla_w_uq=v_mla_w_uq, v_mla_w_uk=v_mla_w_uk, v_mla_w_uv=v_mla_w_uv, v_mla_w_o=v_mla_w_o, v_kv_w_shared=v_kv_w_shared, v_swa_w_q=v_swa_w_q, v_swa_sinks=v_swa_sinks, v_swa_w_o=v_swa_w_o, v_rel_bias=v_rel_bias, v_mlp_w_up=v_mlp_w_up, v_mlp_w_down=v_mlp_w_down, v_ln_mix_g=v_ln_mix_g, v_ln_mix_b=v_ln_mix_b, v_ln_mlp_g=v_ln_mlp_g, v_ln_mlp_b=v_ln_mlp_b)
    weights = {n: given[n] for n in TWIN_WEIGHTS}
    shared = {n: given[n] for n in SHARED_INPUTS}
    per_example = {n: given[n] for n in ['x']}
    grad_fn = _jax.value_and_grad(_loss, argnums=(0, 1))

    def one_microbatch(ex, loss_target):
        ex = dict(ex)
        diff = ex.pop(TWIN_DIFF_INPUT)
        return grad_fn(weights, diff, {**shared, **ex}, loss_target)

    if N_MICROBATCH == 1:
        loss, (grad_w, grad_x) = one_microbatch(per_example, given["loss_target"])
    else:
        def body(carry, xs):
            loss_sum, grad_sum = carry
            l_k, (gw_k, gx_k) = one_microbatch(xs[0], xs[1])
            with _jax.named_scope("update"):
                return (loss_sum + l_k, _jax.tree.map(_jnp.add, grad_sum, gw_k)), gx_k

        init = (_jnp.zeros((), _jnp.float32), _jax.tree.map(_jnp.zeros_like, weights))
        (loss, grad_w), grad_x = _jax.lax.scan(body, init, (per_example, given["loss_target"]))
    with _jax.named_scope("update"):
        delta_w, new_m, new_v = {}, {}, {}
        for n in TWIN_WEIGHTS:
            delta_w[n], new_m[n], new_v[n] = _adamw(weights[n], grad_w[n], given["m_" + n], given["v_" + n])
    return (loss, grad_x, *[grad_w[n] for n in TWIN_WEIGHTS], *[delta_w[n] for n in TWIN_WEIGHTS],
            *[new_m[n] for n in TWIN_WEIGHTS], *[new_v[n] for n in TWIN_WEIGHTS])
```

```python
import functools
import math

import numpy as np
import jax
import jax.numpy as jnp
from jax import lax
from jax.experimental import pallas as pl
from jax.experimental.pallas import tpu as pltpu

F32 = jnp.float32
_MXU_DTYPE = jnp.bfloat16

D_MODEL = 1024
DEPTH = 2
MLA_HEADS = 8
MLA_NOPE = 128
MLA_ROPE = 64
MLA_V = 128
MLA_QR = 384
MLA_C = 256
MLA_DK = 384
ROPE_THETA = 10000.0
SWA_QH = 16
SWA_KVH = 4
SWA_D = 64
SWA_BLOCK = 128
REL_BUCKETS = 32
REL_MAX_DIST = 128
D_FF = 4096
LN_EPS = 1e-5
RMS_EPS = 1e-6
ALPHA = (2 * DEPTH) ** 0.25
ADAM_LR, ADAM_B1, ADAM_B2, ADAM_EPS, ADAM_WD, ADAM_STEP = 0.001, 0.9, 0.999, 1e-08, 0.01, 10

N_DEV = 8
AXES = ("x", "y", "c")
V7X_VMEM_BYTES = 64 * 1024 * 1024
VMEM_LIMIT = V7X_VMEM_BYTES - 8 * 1024 * 1024
LANES = 1024
ATT_TQ = 256
ATT_TK = 512

NT = (((1,), (1,)), ((), ()))
TN = (((0,), (0,)), ((), ()))
S = jax.ShapeDtypeStruct


def _params(*sem, vmem=VMEM_LIMIT):
    return pltpu.CompilerParams(dimension_semantics=sem, vmem_limit_bytes=vmem)


def _dot(a, b, dims=None, precision=None):
    if dims is None:
        return jnp.dot(a, b, preferred_element_type=F32, precision=precision)
    return lax.dot_general(a, b, dims, preferred_element_type=F32, precision=precision)


def _mx(v):
    return v.astype(_MXU_DTYPE)


def _swap_halves_64(v):
    return jnp.concatenate([v[:, 32:], v[:, :32]], axis=-1)


def _swap_halves_groups(v):
    n = v.shape[-1]
    lane = lax.broadcasted_iota(jnp.int32, v.shape, 1)
    return jnp.where(lane % 64 < 32, pltpu.roll(v, n - 32, 1), pltpu.roll(v, 32, 1))


def _mm(a, b, *, name, ta=False, tb=False, add=None, out_dtype=F32, tm=512, tn=512, tk=512, precision=None):
    M, K = (a.shape[1], a.shape[0]) if ta else a.shape
    N = b.shape[0] if tb else b.shape[1]
    tm, tn, tk = min(tm, M), min(tn, N), min(tk, K)
    assert M % tm == 0 and N % tn == 0 and K % tk == 0, (M, N, K, tm, tn, tk)
    nk = K // tk
    dims = (((0 if ta else 1,), (1 if tb else 0,)), ((), ()))
    has_add = add is not None

    def body(*refs):
        if has_add:
            a_ref, b_ref, add_ref, o_ref, acc = refs
        else:
            a_ref, b_ref, o_ref, acc = refs
        k = pl.program_id(2)

        @pl.when(k == 0)
        def _():
            acc[...] = jnp.zeros_like(acc)

        av, bv = a_ref[...], b_ref[...]
        if precision is None:
            av, bv = _mx(av), _mx(bv)
        acc[...] += _dot(av, bv, dims, precision)

        @pl.when(k == nk - 1)
        def _():
            r = acc[...]
            if has_add:
                r = r + add_ref[...]
            o_ref[...] = r.astype(out_dtype)

    a_spec = pl.BlockSpec((tk, tm), lambda i, j, k: (k, i)) if ta else pl.BlockSpec((tm, tk), lambda i, j, k: (i, k))
    b_spec = pl.BlockSpec((tn, tk), lambda i, j, k: (j, k)) if tb else pl.BlockSpec((tk, tn), lambda i, j, k: (k, j))
    in_specs = [a_spec, b_spec]
    args = [a, b]
    if has_add:
        in_specs.append(pl.BlockSpec((tm, tn), lambda i, j, k: (i, j)))
        args.append(add)
    return pl.pallas_call(
        body, name=name, grid=(M // tm, N // tn, nk), in_specs=in_specs,
        out_specs=pl.BlockSpec((tm, tn), lambda i, j, k: (i, j)), out_shape=S((M, N), out_dtype),
        scratch_shapes=[pltpu.VMEM((tm, tn), F32)],
        compiler_params=_params("parallel", "parallel", "arbitrary"))(*args)


def _ln_fwd_math(z, g, b):
    mu = jnp.mean(z, axis=-1, keepdims=True)
    zc = z - mu
    var = jnp.mean(zc * zc, axis=-1, keepdims=True)
    rstd = lax.rsqrt(var + LN_EPS)
    xhat = zc * rstd
    return xhat * g + b, xhat, rstd


def _ln_bwd_math(dxo, xhat, rstd, g):
    dxh = dxo * g
    m1 = jnp.mean(dxh, axis=-1, keepdims=True)
    m2 = jnp.mean(dxh * xhat, axis=-1, keepdims=True)
    dz = rstd * (dxh - m1 - xhat * m2)
    dg = jnp.sum(dxo * xhat, axis=0, keepdims=True)
    db = jnp.sum(dxo, axis=0, keepdims=True)
    return dz, dg, db


def _rms_fwd_math(xr, g):
    r = lax.rsqrt(jnp.mean(xr * xr, axis=-1, keepdims=True) + RMS_EPS)
    return xr * r * g


def _rms_bwd_math(dy, xr, g):
    r = lax.rsqrt(jnp.mean(xr * xr, axis=-1, keepdims=True) + RMS_EPS)
    gy = dy * g
    dx = r * gy - xr * (r * r * r) * jnp.mean(gy * xr, axis=-1, keepdims=True)
    dg = jnp.sum(dy * xr * r, axis=0, keepdims=True)
    return dx, dg


def _mla_pre_fwd(x, w_in, g_q, g_kv, w_uq_n, w_uq_r, w_uk_t, cos, sin):
    T = x.shape[0]
    tm = min(ATT_TQ, T)
    nq = T // tm
    H = MLA_HEADS

    def body(x_ref, win_ref, gq_ref, gkv_ref, wn_ref, wr_ref, wuk_ref, cos_ref, sin_ref,
             h_ref, kc_ref, qs_ref, cq_ref, qn_ref):
        h = _dot(_mx(x_ref[...]), win_ref[...])
        h_ref[...] = h
        cos_v, sin_v = cos_ref[...], sin_ref[...]
        cq = _mx(_rms_fwd_math(h[:, :MLA_QR], gq_ref[...]))
        ckv = _rms_fwd_math(h[:, MLA_QR:MLA_QR + MLA_C], gkv_ref[...])
        krr = h[:, MLA_QR + MLA_C:]
        kr = krr * cos_v[:, :MLA_ROPE] + _swap_halves_64(krr) * sin_v[:, :MLA_ROPE]
        kc_ref[:, 0:MLA_C] = _mx(ckv)
        kc_ref[:, MLA_C:MLA_C + MLA_ROPE] = _mx(kr)
        kc_ref[:, MLA_C + MLA_ROPE:] = jnp.zeros((tm, MLA_DK - MLA_C - MLA_ROPE), _MXU_DTYPE)
        cq_ref[...] = cq
        qnb = _mx(_dot(cq, wn_ref[...]))
        qn_ref[...] = qnb
        qr = _dot(cq, wr_ref[...])
        qrr = qr * cos_v + _swap_halves_groups(qr) * sin_v
        for hd in range(H):
            ql = _dot(qnb[:, MLA_NOPE * hd:MLA_NOPE * (hd + 1)], wuk_ref[hd])
            qs_ref[0, hd, :, 0:MLA_C] = _mx(ql)
            qs_ref[0, hd, :, MLA_C:MLA_C + MLA_ROPE] = _mx(qrr[:, MLA_ROPE * hd:MLA_ROPE * (hd + 1)])
            qs_ref[0, hd, :, MLA_C + MLA_ROPE:] = jnp.zeros((tm, MLA_DK - MLA_C - MLA_ROPE), _MXU_DTYPE)

    full = lambda shp: pl.BlockSpec(shp, lambda i: (0,) * len(shp))
    rows = lambda n: pl.BlockSpec((tm, n), lambda i: (i, 0))
    n_in = w_in.shape[1]
    return pl.pallas_call(
        body, name="mla_pre_fwd", grid=(nq,),
        in_specs=[rows(D_MODEL), full(w_in.shape), full(g_q.shape), full(g_kv.shape), full(w_uq_n.shape),
                  full(w_uq_r.shape), full(w_uk_t.shape), rows(H * MLA_ROPE), rows(H * MLA_ROPE)],
        out_specs=[rows(n_in), rows(MLA_DK), pl.BlockSpec((1, H, tm, MLA_DK), lambda i: (i, 0, 0, 0)),
                   rows(MLA_QR), rows(H * MLA_NOPE)],
        out_shape=[S((T, n_in), F32), S((T, MLA_DK), _MXU_DTYPE), S((nq, H, tm, MLA_DK), _MXU_DTYPE),
                   S((T, MLA_QR), _MXU_DTYPE), S((T, H * MLA_NOPE), _MXU_DTYPE)],
        compiler_params=_params("parallel"))(x, w_in, g_q, g_kv, w_uq_n, w_uq_r, w_uk_t, cos, sin)


def _att_steps(T, tq, tk):
    qi, kj = [], []
    for i in range(T // tq):
        for j in range((i * tq + tq - 1) // tk + 1):
            qi.append(i)
            kj.append(j)
    return jnp.asarray(np.array(qi, np.int32)), jnp.asarray(np.array(kj, np.int32))


def _mla_attn_fwd(qs, kc):
    nq, H, tq, DK = qs.shape
    T = kc.shape[0]
    tk = min(ATT_TK, T)
    scale = (MLA_NOPE + MLA_ROPE) ** -0.5
    qi, kj = _att_steps(T, tq, tk)

    def body(qi_ref, kj_ref, q_ref, k_ref, o_ref, lse_ref, m_sc, l_sc, acc_sc):
        st = pl.program_id(0)
        i, j = qi_ref[st], kj_ref[st]
        j_last = (i * tq + tq - 1) // tk

        @pl.when(j == 0)
        def _():
            m_sc[...] = jnp.full_like(m_sc, -jnp.inf)
            l_sc[...] = jnp.zeros_like(l_sc)
            acc_sc[...] = jnp.zeros_like(acc_sc)

        k = k_ref[pl.ds(pl.multiple_of(j * tk, tk), tk), :]
        v = k[:, :MLA_C]
        row = lax.broadcasted_iota(jnp.int32, (tq, tk), 0) + i * tq
        col = lax.broadcasted_iota(jnp.int32, (tq, tk), 1) + j * tk
        causal = col <= row

        def head(hd, carry):
            s = _dot(q_ref[0, hd], k, NT) * scale
            s = jnp.where(causal, s, -jnp.inf)
            m_prev = m_sc[hd]
            m_new = jnp.maximum(m_prev, jnp.max(s, axis=1, keepdims=True))
            a = jnp.exp(m_prev - m_new)
            p = jnp.exp(s - m_new)
            l_sc[hd] = a * l_sc[hd] + jnp.sum(p, axis=1, keepdims=True)
            acc_sc[hd] = a * acc_sc[hd] + _dot(_mx(p), v)
            m_sc[hd] = m_new
            return carry

        lax.fori_loop(0, H, head, 0)

        @pl.when(j == j_last)
        def _():
            o_ref[0] = _mx(acc_sc[...] / l_sc[...])
            lse_ref[0] = m_sc[...] + jnp.log(l_sc[...])

    gs = pltpu.PrefetchScalarGridSpec(
        num_scalar_prefetch=2, grid=(int(qi.shape[0]),),
        in_specs=[pl.BlockSpec((1, H, tq, DK), lambda s, qi, kj: (qi[s], 0, 0, 0)),
                  pl.BlockSpec(memory_space=pltpu.VMEM)],
        out_specs=[pl.BlockSpec((1, H, tq, MLA_C), lambda s, qi, kj: (qi[s], 0, 0, 0)),
                   pl.BlockSpec((1, H, tq, 1), lambda s, qi, kj: (qi[s], 0, 0, 0))],
        scratch_shapes=[pltpu.VMEM((H, tq, 1), F32), pltpu.VMEM((H, tq, 1), F32), pltpu.VMEM((H, tq, MLA_C), F32)])
    return pl.pallas_call(
        body, name="mla_attn_fwd", grid_spec=gs,
        out_shape=[S((nq, H, tq, MLA_C), _MXU_DTYPE), S((nq, H, tq, 1), F32)],
        compiler_params=_params("arbitrary"))(qi, kj, qs, kc)


def _mla_attn_bwd(qs, kc, dol, lse, delta):
    nq, H, tq, DK = qs.shape
    T = kc.shape[0]
    tk = min(ATT_TK, T)
    scale = (MLA_NOPE + MLA_ROPE) ** -0.5
    qi, kj = _att_steps(T, tq, tk)
    n_steps = int(qi.shape[0])

    def body(qi_ref, kj_ref, q_ref, k_ref, do_ref, lse_ref, dl_ref, dq_ref, dk_ref, dv_ref, dk_acc, dv_acc, sem):
        st = pl.program_id(0)
        i, j = qi_ref[st], kj_ref[st]

        @pl.when(st == 0)
        def _():
            dk_acc[...] = jnp.zeros_like(dk_acc)
            dv_acc[...] = jnp.zeros_like(dv_acc)

        @pl.when(j == 0)
        def _():
            dq_ref[...] = jnp.zeros_like(dq_ref)

        koff = pl.multiple_of(j * tk, tk)
        k = k_ref[pl.ds(koff, tk), :]
        v = k[:, :MLA_C]
        row = lax.broadcasted_iota(jnp.int32, (tq, tk), 0) + i * tq
        col = lax.broadcasted_iota(jnp.int32, (tq, tk), 1) + j * tk
        causal = col <= row

        def head(hd, carry):
            dk_c, dv_c = carry
            q = q_ref[0, hd]
            do = do_ref[0, hd]
            s = _dot(q, k, NT) * scale
            p = jnp.where(causal, jnp.exp(s - lse_ref[0, hd]), 0.0)
            dp = _dot(do, v, NT)
            dsb = _mx(p * (dp - dl_ref[0, hd]) * scale)
            pb = _mx(p)
            dq_ref[0, hd] += _dot(dsb, k)
            return dk_c + _dot(dsb, q, TN), dv_c + _dot(pb, do, TN)

        dk_c, dv_c = lax.fori_loop(0, H, head, (jnp.zeros((tk, DK), F32), jnp.zeros((tk, MLA_C), F32)))
        dk_acc[pl.ds(koff, tk), :] += dk_c
        dv_acc[pl.ds(koff, tk), :] += dv_c

        @pl.when(st == n_steps - 1)
        def _():
            c1 = pltpu.make_async_copy(dk_acc, dk_ref, sem.at[0])
            c2 = pltpu.make_async_copy(dv_acc, dv_ref, sem.at[1])
            c1.start()
            c2.start()
            c1.wait()
            c2.wait()

    blk = lambda n: pl.BlockSpec((1, H, tq, n), lambda s, qi, kj: (qi[s], 0, 0, 0))
    gs = pltpu.PrefetchScalarGridSpec(
        num_scalar_prefetch=2, grid=(n_steps,),
        in_specs=[blk(DK), pl.BlockSpec(memory_space=pltpu.VMEM), blk(MLA_C), blk(1), blk(1)],
        out_specs=[blk(DK), pl.BlockSpec(memory_space=pl.ANY), pl.BlockSpec(memory_space=pl.ANY)],
        scratch_shapes=[pltpu.VMEM((T, DK), F32), pltpu.VMEM((T, MLA_C), F32), pltpu.SemaphoreType.DMA((2,))])
    return pl.pallas_call(
        body, name="mla_attn_bwd", grid_spec=gs,
        out_shape=[S((nq, H, tq, DK), F32), S((T, DK), F32), S((T, MLA_C), F32)],
        compiler_params=_params("arbitrary"))(qi, kj, qs, kc, dol, lse, delta)


def _mla_uv_fwd(olat, w_uv):
    nq, H, tq, C = olat.shape
    T = nq * tq

    def body(ol_ref, wuv_ref, o_ref):
        for hd in range(H):
            o_ref[:, MLA_V * hd:MLA_V * (hd + 1)] = _mx(_dot(ol_ref[0, hd], wuv_ref[hd]))

    return pl.pallas_call(
        body, name="mla_uv_fwd", grid=(nq,),
        in_specs=[pl.BlockSpec((1, H, tq, C), lambda i: (i, 0, 0, 0)), pl.BlockSpec(w_uv.shape, lambda i: (0, 0, 0))],
        out_specs=pl.BlockSpec((tq, H * MLA_V), lambda i: (i, 0)), out_shape=S((T, H * MLA_V), _MXU_DTYPE),
        compiler_params=_params("parallel"))(olat, w_uv)


def _mla_uv_bwd(do, olat, w_uv):
    nq, H, tq, C = olat.shape

    def body(do_ref, ol_ref, wuv_ref, dol_ref, dl_ref, dw_ref):
        @pl.when(pl.program_id(0) == 0)
        def _():
            dw_ref[...] = jnp.zeros_like(dw_ref)

        dov = do_ref[...]
        for hd in range(H):
            doh = _mx(dov[:, MLA_V * hd:MLA_V * (hd + 1)])
            ol = ol_ref[0, hd]
            dol = _dot(doh, wuv_ref[hd], NT)
            dol_ref[0, hd] = _mx(dol)
            dl_ref[0, hd] = jnp.sum(dol * ol.astype(F32), axis=1, keepdims=True)
            dw_ref[hd] += _dot(ol, doh, TN)

    blk = lambda n: pl.BlockSpec((1, H, tq, n), lambda i: (i, 0, 0, 0))
    return pl.pallas_call(
        body, name="mla_uv_bwd", grid=(nq,),
        in_specs=[pl.BlockSpec((tq, H * MLA_V), lambda i: (i, 0)), blk(C), pl.BlockSpec(w_uv.shape, lambda i: (0, 0, 0))],
        out_specs=[blk(C), blk(1), pl.BlockSpec(w_uv.shape, lambda i: (0, 0, 0))],
        out_shape=[S(olat.shape, _MXU_DTYPE), S((nq, H, tq, 1), F32), S(w_uv.shape, F32)],
        compiler_params=_params("arbitrary"))(do, olat, w_uv)


def _mla_pre_bwd(dqs, dkc, dv, h, x, dres, cq, qn, cos, sin, w_in, g_q, g_kv, w_uq_n, w_uq_r, w_uk):
    nq, H, tm, DK = dqs.shape
    T = nq * tm
    n_in = w_in.shape[1]

    def body(dqs_ref, dkc_ref, dv_ref, h_ref, x_ref, dres_ref, cq_ref, qn_ref, cos_ref, sin_ref,
             win_ref, gq_ref, gkv_ref, wn_ref, wr_ref, wuk_ref,
             gx_ref, dwin_ref, dwn_ref, dwr_ref, dwuk_ref, dgq_ref, dgkv_ref, dqn_sc, dqr_sc, dh_sc):
        @pl.when(pl.program_id(0) == 0)
        def _():
            for r in (dwin_ref, dwn_ref, dwr_ref, dwuk_ref, dgq_ref, dgkv_ref):
                r[...] = jnp.zeros_like(r)

        cos_v, sin_v = cos_ref[...], sin_ref[...]
        qnb = qn_ref[...]
        for hd in range(H):
            dqh = dqs_ref[0, hd]
            dql = _mx(dqh[:, :MLA_C])
            dqn_sc[:, MLA_NOPE * hd:MLA_NOPE * (hd + 1)] = _dot(dql, wuk_ref[hd])
            dwuk_ref[hd] += _dot(dql, qnb[:, MLA_NOPE * hd:MLA_NOPE * (hd + 1)], TN)
            dqr_sc[:, MLA_ROPE * hd:MLA_ROPE * (hd + 1)] = dqh[:, MLA_C:MLA_C + MLA_ROPE]
        dqr = dqr_sc[...]
        dqrb = _mx(dqr * cos_v + _swap_halves_groups(dqr * sin_v))
        dqnb = _mx(dqn_sc[...])
        cq = cq_ref[...]
        dwn_ref[...] += _dot(cq, dqnb, TN)
        dwr_ref[...] += _dot(cq, dqrb, TN)
        dcq = _dot(dqnb, wn_ref[...], NT) + _dot(dqrb, wr_ref[...], NT)
        hv = h_ref[...]
        dxq, dgq = _rms_bwd_math(dcq, hv[:, :MLA_QR], gq_ref[...])
        dgq_ref[...] += dgq
        dkcv = dkc_ref[...]
        dckv = dkcv[:, :MLA_C] + dv_ref[...]
        dxkv, dgkv = _rms_bwd_math(dckv, hv[:, MLA_QR:MLA_QR + MLA_C], gkv_ref[...])
        dgkv_ref[...] += dgkv
        dkr = dkcv[:, MLA_C:MLA_C + MLA_ROPE]
        dkr_raw = dkr * cos_v[:, :MLA_ROPE] + _swap_halves_64(dkr * sin_v[:, :MLA_ROPE])
        dh_sc[:, 0:MLA_QR] = dxq
        dh_sc[:, MLA_QR:MLA_QR + MLA_C] = dxkv
        dh_sc[:, MLA_QR + MLA_C:] = dkr_raw
        dhb = _mx(dh_sc[...])
        gx_ref[...] = dres_ref[...] + _dot(dhb, win_ref[...], NT)
        dwin_ref[...] += _dot(_mx(x_ref[...]), dhb, TN)

    full = lambda shp: pl.BlockSpec(shp, lambda i: (0,) * len(shp))
    rows = lambda n: pl.BlockSpec((tm, n), lambda i: (i, 0))
    return pl.pallas_call(
        body, name="mla_pre_bwd", grid=(nq,),
        in_specs=[pl.BlockSpec((1, H, tm, DK), lambda i: (i, 0, 0, 0)), rows(DK), rows(MLA_C), rows(n_in),
                  rows(D_MODEL), rows(D_MODEL), rows(MLA_QR), rows(H * MLA_NOPE), rows(H * MLA_ROPE), rows(H * MLA_ROPE),
                  full(w_in.shape), full(g_q.shape), full(g_kv.shape), full(w_uq_n.shape), full(w_uq_r.shape),
                  full(w_uk.shape)],
        out_specs=[rows(D_MODEL), full(w_in.shape), full(w_uq_n.shape), full(w_uq_r.shape), full(w_uk.shape),
                   full(g_q.shape), full(g_kv.shape)],
        out_shape=[S((T, D_MODEL), F32), S(w_in.shape, F32), S(w_uq_n.shape, F32), S(w_uq_r.shape, F32),
                   S(w_uk.shape, F32), S(g_q.shape, F32), S(g_kv.shape, F32)],
        scratch_shapes=[pltpu.VMEM((tm, H * MLA_NOPE), F32), pltpu.VMEM((tm, H * MLA_ROPE), F32),
                        pltpu.VMEM((tm, n_in), F32)],
        compiler_params=_params("arbitrary"))(dqs, dkc, dv, h, x, dres, cq, qn, cos, sin, w_in, g_q, g_kv,
                                              w_uq_n, w_uq_r, w_uk)


def _proj_ln_fwd(a, w, xres, g, b, *, name, tm=512):
    T, K = a.shape
    tm = min(tm, T)

    def body(a_ref, w_ref, x_ref, g_ref, b_ref, xo_ref, xob_ref, xh_ref, rs_ref):
        z = ALPHA * x_ref[...] + _dot(a_ref[...], w_ref[...])
        xo, xhat, rstd = _ln_fwd_math(z, g_ref[...], b_ref[...])
        xo_ref[...] = xo
        xob_ref[...] = _mx(xo)
        xh_ref[...] = xhat
        rs_ref[...] = rstd

    rows = lambda n: pl.BlockSpec((tm, n), lambda i: (i, 0))
    full = lambda shp: pl.BlockSpec(shp, lambda i: (0,) * len(shp))
    return pl.pallas_call(
        body, name=name, grid=(T // tm,),
        in_specs=[rows(K), full(w.shape), rows(D_MODEL), full(g.shape), full(b.shape)],
        out_specs=[rows(D_MODEL), rows(D_MODEL), rows(D_MODEL), rows(1)],
        out_shape=[S((T, D_MODEL), F32), S((T, D_MODEL), _MXU_DTYPE), S((T, D_MODEL), F32), S((T, 1), F32)],
        compiler_params=_params("parallel"))(a, w, xres, g, b)


def _proj_ln_bwd(dxo, xhat, rstd, g, a, w, *, name, tm=512):
    T, K = a.shape
    tm = min(tm, T)

    def body(dxo_ref, xh_ref, rs_ref, g_ref, a_ref, w_ref, dres_ref, da_ref, dw_ref, dg_ref, db_ref):
        @pl.when(pl.program_id(0) == 0)
        def _():
            for r in (dw_ref, dg_ref, db_ref):
                r[...] = jnp.zeros_like(r)

        dz, dg, db = _ln_bwd_math(dxo_ref[...], xh_ref[...], rs_ref[...], g_ref[...])
        dg_ref[...] += dg
        db_ref[...] += db
        dres_ref[...] = ALPHA * dz
        dzb = _mx(dz)
        da_ref[...] = _dot(dzb, w_ref[...], NT)
        dw_ref[...] += _dot(a_ref[...], dzb, TN)

    rows = lambda n: pl.BlockSpec((tm, n), lambda i: (i, 0))
    full = lambda shp: pl.BlockSpec(shp, lambda i: (0,) * len(shp))
    return pl.pallas_call(
        body, name=name, grid=(T // tm,),
        in_specs=[rows(D_MODEL), rows(D_MODEL), rows(1), full(g.shape), rows(K), full(w.shape)],
        out_specs=[rows(D_MODEL), rows(K), full(w.shape), full(g.shape), full(g.shape)],
        out_shape=[S((T, D_MODEL), F32), S((T, K), F32), S(w.shape, F32), S(g.shape, F32), S(g.shape, F32)],
        compiler_params=_params("arbitrary"))(dxo, xhat, rstd, g, a, w)


def _mlp_fwd(xb, xres, w_up, w_dn, layer, g, b, *, tm=1024):
    T = xb.shape[0]
    tm = min(tm, T)
    nj, _, _, fc = w_up.shape

    def body(xb_ref, x_ref, wu_ref, wd_ref, g_ref, b_ref, u_ref, xo_ref, xob_ref, xh_ref, rs_ref, acc):
        j = pl.program_id(1)

        @pl.when(j == 0)
        def _():
            acc[...] = ALPHA * x_ref[...]

        u = _dot(xb_ref[...], wu_ref[...])
        u_ref[...] = _mx(u)
        r = jnp.maximum(u, 0.0)
        acc[...] += _dot(_mx(r * r), wd_ref[...])

        @pl.when(j == nj - 1)
        def _():
            xo, xhat, rstd = _ln_fwd_math(acc[...], g_ref[...], b_ref[...])
            xo_ref[...] = xo
            xob_ref[...] = _mx(xo)
            xh_ref[...] = xhat
            rs_ref[...] = rstd

    rows = lambda n: pl.BlockSpec((tm, n), lambda i, j: (i, 0))
    full = lambda shp: pl.BlockSpec(shp, lambda i, j: (0,) * len(shp))
    return pl.pallas_call(
        body, name=f"mlp_fwd_{layer}", grid=(T // tm, nj),
        in_specs=[rows(D_MODEL), rows(D_MODEL),
                  pl.BlockSpec((None, None, D_MODEL, fc), lambda i, j: (j, layer, 0, 0)),
                  pl.BlockSpec((None, None, fc, D_MODEL), lambda i, j: (j, layer, 0, 0)),
                  full(g.shape), full(b.shape)],
        out_specs=[pl.BlockSpec((tm, fc), lambda i, j: (i, j)), rows(D_MODEL), rows(D_MODEL), rows(D_MODEL), rows(1)],
        out_shape=[S((T, nj * fc), _MXU_DTYPE), S((T, D_MODEL), F32), S((T, D_MODEL), _MXU_DTYPE),
                   S((T, D_MODEL), F32), S((T, 1), F32)],
        scratch_shapes=[pltpu.VMEM((tm, D_MODEL), F32)],
        compiler_params=_params("parallel", "arbitrary"))(xb, xres, w_up, w_dn, g, b)


def _mlp_bwd_dx(dxo, xhat, rstd, g, u, w_up, w_dn, layer, *, tm=1024):
    T = dxo.shape[0]
    tm = min(tm, T)
    nj, _, _, fc = w_up.shape

    def body(dxo_ref, xh_ref, rs_ref, g_ref, u_ref, wu_ref, wd_ref, dx_ref, du_ref, dyb_ref, dg_ref, db_ref, acc, dy_sc):
        i, j = pl.program_id(0), pl.program_id(1)

        @pl.when((i == 0) & (j == 0))
        def _():
            dg_ref[...] = jnp.zeros_like(dg_ref)
            db_ref[...] = jnp.zeros_like(db_ref)

        @pl.when(j == 0)
        def _():
            dz, dg, db = _ln_bwd_math(dxo_ref[...], xh_ref[...], rs_ref[...], g_ref[...])
            dg_ref[...] += dg
            db_ref[...] += db
            acc[...] = ALPHA * dz
            dy_sc[...] = _mx(dz)
            dyb_ref[...] = _mx(dz)

        r = jnp.maximum(u_ref[...].astype(F32), 0.0)
        da = _dot(dy_sc[...], wd_ref[...], NT)
        dub = _mx(da * (2.0 * r))
        du_ref[...] = dub
        acc[...] += _dot(dub, wu_ref[...], NT)

        @pl.when(j == nj - 1)
        def _():
            dx_ref[...] = acc[...]

    rows = lambda n: pl.BlockSpec((tm, n), lambda i, j: (i, 0))
    full = lambda shp: pl.BlockSpec(shp, lambda i, j: (0,) * len(shp))
    return pl.pallas_call(
        body, name=f"mlp_bwd_dx_{layer}", grid=(T // tm, nj),
        in_specs=[rows(D_MODEL), rows(D_MODEL), rows(1), full(g.shape), pl.BlockSpec((tm, fc), lambda i, j: (i, j)),
                  pl.BlockSpec((None, None, D_MODEL, fc), lambda i, j: (j, layer, 0, 0)),
                  pl.BlockSpec((None, None, fc, D_MODEL), lambda i, j: (j, layer, 0, 0))],
        out_specs=[rows(D_MODEL), pl.BlockSpec((tm, fc), lambda i, j: (i, j)), rows(D_MODEL), full(g.shape), full(g.shape)],
        out_shape=[S((T, D_MODEL), F32), S((T, nj * fc), _MXU_DTYPE), S((T, D_MODEL), _MXU_DTYPE),
                   S(g.shape, F32), S(g.shape, F32)],
        scratch_shapes=[pltpu.VMEM((tm, D_MODEL), F32), pltpu.VMEM((tm, D_MODEL), _MXU_DTYPE)],
        compiler_params=_params("arbitrary", "arbitrary"))(dxo, xhat, rstd, g, u, w_up, w_dn)


def _mlp_bwd_dw(u, dyb, xinb, du, layer, *, nj, tm=1024):
    T = u.shape[0]
    tm = min(tm, T)
    fc = u.shape[1] // nj

    def body(u_ref, dy_ref, x_ref, du_ref, gd_ref, gu_ref):
        @pl.when(pl.program_id(1) == 0)
        def _():
            gd_ref[...] = jnp.zeros_like(gd_ref)
            gu_ref[...] = jnp.zeros_like(gu_ref)

        r = jnp.maximum(u_ref[...].astype(F32), 0.0)
        gd_ref[...] += _dot(_mx(r * r), dy_ref[...], TN)
        gu_ref[...] += _dot(x_ref[...], du_ref[...], TN)

    return pl.pallas_call(
        body, name=f"mlp_bwd_dw_{layer}", grid=(nj, T // tm),
        in_specs=[pl.BlockSpec((tm, fc), lambda j, i: (i, j)), pl.BlockSpec((tm, D_MODEL), lambda j, i: (i, 0)),
                  pl.BlockSpec((tm, D_MODEL), lambda j, i: (i, 0)), pl.BlockSpec((tm, fc), lambda j, i: (i, j))],
        out_specs=[pl.BlockSpec((None, fc, D_MODEL), lambda j, i: (j, 0, 0)),
                   pl.BlockSpec((None, D_MODEL, fc), lambda j, i: (j, 0, 0))],
        out_shape=[S((nj, fc, D_MODEL), F32), S((nj, D_MODEL, fc), F32)],
        compiler_params=_params("parallel", "arbitrary"))(u, dyb, xinb, du)


def _swa_scores(q, kp, kc, bias_h, sink, n, hd):
    blk = SWA_BLOCK
    kh = hd // (SWA_QH // SWA_KVH)
    qh = q[:, SWA_D * hd:SWA_D * (hd + 1)]
    kph = kp[:, SWA_D * kh:SWA_D * (kh + 1)]
    kch = kc[:, SWA_D * kh:SWA_D * (kh + 1)]
    scale = SWA_D ** -0.5
    row = lax.broadcasted_iota(jnp.int32, (blk, blk), 0)
    col = lax.broadcasted_iota(jnp.int32, (blk, blk), 1)
    sp = _dot(qh, kph, NT) * scale + bias_h[:, :blk]
    sc = _dot(qh, kch, NT) * scale + bias_h[:, blk:]
    sp = jnp.where((col > row) & (n > 0), sp, -jnp.inf)
    sc = jnp.where(col <= row, sc, -jnp.inf)
    m = jnp.maximum(jnp.maximum(jnp.max(sp, axis=1, keepdims=True), jnp.max(sc, axis=1, keepdims=True)), sink)
    pp, pc, ps = jnp.exp(sp - m), jnp.exp(sc - m), jnp.exp(sink - m)
    inv = 1.0 / (jnp.sum(pp, axis=1, keepdims=True) + jnp.sum(pc, axis=1, keepdims=True) + ps)
    return qh, kph, kch, pp * inv, pc * inv, ps * inv


def _swa_attn_fwd(qkv, bias, sinks):
    T = qkv.shape[0]
    blk = SWA_BLOCK
    nb = T // blk
    dq, dkv = SWA_QH * SWA_D, SWA_KVH * SWA_D

    def body(q_ref, kvp_ref, kvc_ref, bias_ref, sink_ref, o_ref):
        n = pl.program_id(0)
        q, kvp, kvc = q_ref[...], kvp_ref[...], kvc_ref[...]
        sinks_v = sink_ref[...]
        for hd in range(SWA_QH):
            kh = hd // (SWA_QH // SWA_KVH)
            _, _, _, pp, pc, _ = _swa_scores(q, kvp[:, :dkv], kvc[:, :dkv], bias_ref[hd], sinks_v[:, hd:hd + 1], n, hd)
            vp = kvp[:, dkv + SWA_D * kh:dkv + SWA_D * (kh + 1)]
            vc = kvc[:, dkv + SWA_D * kh:dkv + SWA_D * (kh + 1)]
            o_ref[:, SWA_D * hd:SWA_D * (hd + 1)] = _mx(_dot(_mx(pp), vp) + _dot(_mx(pc), vc))

    return pl.pallas_call(
        body, name="swa_attn_fwd", grid=(nb,),
        in_specs=[pl.BlockSpec((blk, dq), lambda n: (n, 0)),
                  pl.BlockSpec((blk, 2 * dkv), lambda n: (jnp.maximum(n - 1, 0), dq // (2 * dkv))),
                  pl.BlockSpec((blk, 2 * dkv), lambda n: (n, dq // (2 * dkv))),
                  pl.BlockSpec(bias.shape, lambda n: (0, 0, 0)), pl.BlockSpec(sinks.shape, lambda n: (0, 0))],
        out_specs=pl.BlockSpec((blk, dq), lambda n: (n, 0)), out_shape=S((T, dq), _MXU_DTYPE),
        compiler_params=_params("parallel"))(qkv, qkv, qkv, bias, sinks)


def _swa_attn_bwd(qkv, ob, do, bias, sinks):
    T = qkv.shape[0]
    blk = SWA_BLOCK
    nb = T // blk
    dq, dkv = SWA_QH * SWA_D, SWA_KVH * SWA_D
    grp = SWA_QH // SWA_KVH

    def body(q_ref, kvp_ref, kvc_ref, o_ref, do_ref, bias_ref, sink_ref, dqkv_ref, dbias_ref, dsink_ref, carry):
        st = pl.program_id(0)
        n = nb - 1 - st

        @pl.when(st == 0)
        def _():
            carry[...] = jnp.zeros_like(carry)
            dbias_ref[...] = jnp.zeros_like(dbias_ref)
            dsink_ref[...] = jnp.zeros_like(dsink_ref)

        q, kvp, kvc = q_ref[...], kvp_ref[...], kvc_ref[...]
        ov, dov = o_ref[...], do_ref[...]
        sinks_v = sink_ref[...]
        for kh in range(SWA_KVH):
            vp = kvp[:, dkv + SWA_D * kh:dkv + SWA_D * (kh + 1)]
            vc = kvc[:, dkv + SWA_D * kh:dkv + SWA_D * (kh + 1)]
            dkp = dkc = dvp = dvc = jnp.zeros((blk, SWA_D), F32)
            for gi in range(grp):
                hd = kh * grp + gi
                qh, kph, kch, pp, pc, ps = _swa_scores(q, kvp[:, :dkv], kvc[:, :dkv], bias_ref[hd],
                                                       sinks_v[:, hd:hd + 1], n, hd)
                doh = dov[:, SWA_D * hd:SWA_D * (hd + 1)]
                dl = jnp.sum(doh * ov[:, SWA_D * hd:SWA_D * (hd + 1)].astype(F32), axis=1, keepdims=True)
                dohb = _mx(doh)
                dsp = pp * (_dot(dohb, vp, NT) - dl)
                dsc = pc * (_dot(dohb, vc, NT) - dl)
                dbias_ref[hd, :, 0:blk] += dsp
                dbias_ref[hd, :, blk:] += dsc
                dsk = jnp.sum(-ps * dl, axis=0, keepdims=True)
                dsink_ref[hd:hd + 1, :] += jnp.broadcast_to(dsk, (1, dsink_ref.shape[1]))
                dspb, dscb = _mx(dsp * (SWA_D ** -0.5)), _mx(dsc * (SWA_D ** -0.5))
                dqkv_ref[:, SWA_D * hd:SWA_D * (hd + 1)] = _mx(_dot(dspb, kph) + _dot(dscb, kch))
                dkp = dkp + _dot(dspb, qh, TN)
                dkc = dkc + _dot(dscb, qh, TN)
                dvp = dvp + _dot(_mx(pp), dohb, TN)
                dvc = dvc + _dot(_mx(pc), dohb, TN)
            ko, vo = SWA_D * kh, dkv + SWA_D * kh
            dqkv_ref[:, dq + ko:dq + ko + SWA_D] = _mx(dkc + carry[:, ko:ko + SWA_D])
            dqkv_ref[:, dq + vo:dq + vo + SWA_D] = _mx(dvc + carry[:, vo:vo + SWA_D])
            carry[:, ko:ko + SWA_D] = dkp
            carry[:, vo:vo + SWA_D] = dvp

    rev = lambda s: nb - 1 - s
    return pl.pallas_call(
        body, name="swa_attn_bwd", grid=(nb,),
        in_specs=[pl.BlockSpec((blk, dq), lambda s: (rev(s), 0)),
                  pl.BlockSpec((blk, 2 * dkv), lambda s: (jnp.maximum(rev(s) - 1, 0), dq // (2 * dkv))),
                  pl.BlockSpec((blk, 2 * dkv), lambda s: (rev(s), dq // (2 * dkv))),
                  pl.BlockSpec((blk, dq), lambda s: (rev(s), 0)), pl.BlockSpec((blk, dq), lambda s: (rev(s), 0)),
                  pl.BlockSpec(bias.shape, lambda s: (0, 0, 0)), pl.BlockSpec(sinks.shape, lambda s: (0, 0))],
        out_specs=[pl.BlockSpec((blk, dq + 2 * dkv), lambda s: (rev(s), 0)),
                   pl.BlockSpec(bias.shape, lambda s: (0, 0, 0)), pl.BlockSpec((SWA_QH, 128), lambda s: (0, 0))],
        out_shape=[S((T, dq + 2 * dkv), _MXU_DTYPE), S(bias.shape, F32), S((SWA_QH, 128), F32)],
        scratch_shapes=[pltpu.VMEM((blk, 2 * dkv), F32)],
        compiler_params=_params("arbitrary"))(qkv, qkv, qkv, ob, do, bias, sinks)


def _t5_onehot():
    i = jnp.arange(SWA_BLOCK)
    j = jnp.arange(2 * SWA_BLOCK)
    n = jnp.maximum(i[:, None] + SWA_BLOCK - j[None, :], 0)
    max_exact = REL_BUCKETS // 2
    nf = jnp.maximum(n, 1).astype(F32)
    large = max_exact + (jnp.log(nf / max_exact) / math.log(REL_MAX_DIST / max_exact)
                         * (REL_BUCKETS - max_exact)).astype(jnp.int32)
    large = jnp.minimum(large, REL_BUCKETS - 1)
    bucket = jnp.where(n < max_exact, n, large).reshape(-1)
    return (bucket[None, :] == jnp.arange(REL_BUCKETS)[:, None]).astype(F32)


def _loss_head(y, target, *, tm=1024):
    T, D = y.shape
    tm = min(tm, T)

    def body(y_ref, t_ref, loss_ref, dy_ref):
        @pl.when(pl.program_id(0) == 0)
        def _():
            loss_ref[...] = jnp.zeros_like(loss_ref)

        d = y_ref[...] - t_ref[...]
        dy_ref[...] = d * (1.0 / D)
        rs = jnp.sum(d * d, axis=1, keepdims=True)
        loss_ref[...] += (0.5 / D) * jnp.sum(rs, axis=0, keepdims=True)

    rows = pl.BlockSpec((tm, D), lambda i: (i, 0))
    return pl.pallas_call(
        body, name="loss_head", grid=(T // tm,), in_specs=[rows, rows],
        out_specs=[pl.BlockSpec((1, 1), lambda i: (0, 0)), rows], out_shape=[S((1, 1), F32), S((T, D), F32)],
        compiler_params=_params("arbitrary"))(y, target)


def _exchange(gather, scatter, *, name):
    n_g, n_s = len(gather), len(scatter)
    n_arr = n_g + n_s

    def body(*refs):
        ins, outs = refs[:n_arr], refs[n_arr:2 * n_arr]
        send_sems, recv_sems, loc_sems = refs[2 * n_arr:]
        mx, my, mc = lax.axis_index("x"), lax.axis_index("y"), lax.axis_index("c")
        me = 4 * mx + 2 * my + mc
        local, remote = [], []
        for a in range(n_arr):
            src = ins[a] if a < n_g else ins[a].at[me]
            cp = pltpu.make_async_copy(src, outs[a].at[me], loc_sems.at[a])
            cp.start()
            local.append(cp)
        for k in range(1, N_DEV):
            px, py, pc = mx ^ ((k >> 2) & 1), my ^ ((k >> 1) & 1), mc ^ (k & 1)
            peer = 4 * px + 2 * py + pc
            for a in range(n_arr):
                src = ins[a] if a < n_g else ins[a].at[peer]
                cp = pltpu.make_async_remote_copy(
                    src_ref=src, dst_ref=outs[a].at[me], send_sem=send_sems.at[a, k - 1],
                    recv_sem=recv_sems.at[a, k - 1], device_id=(px, py, pc), device_id_type=pl.DeviceIdType.MESH)
                cp.start()
                remote.append(cp)
        for cp in remote:
            cp.wait()
        for cp in local:
            cp.wait()

    hbm = pl.BlockSpec(memory_space=pl.ANY)
    out_shape = [S((N_DEV,) + g.shape, g.dtype) for g in gather] + [S(s.shape, s.dtype) for s in scatter]
    return pl.pallas_call(
        body, name=name, in_specs=[hbm] * n_arr, out_specs=[hbm] * n_arr, out_shape=out_shape,
        scratch_shapes=[pltpu.SemaphoreType.DMA((n_arr, N_DEV - 1)), pltpu.SemaphoreType.DMA((n_arr, N_DEV - 1)),
                        pltpu.SemaphoreType.DMA((n_arr,))])(*gather, *scatter)


def _adamw(parts, w, m, v, *, name, tr=256):
    R, C = w.shape
    tr = min(tr, R)
    assert R % tr == 0

    def body(p_ref, w_ref, m_ref, v_ref, g_ref, d_ref, nm_ref, nv_ref):
        g = p_ref[0]
        for k in range(1, N_DEV):
            g = g + p_ref[k]
        g_ref[...] = g
        m_new = ADAM_B1 * m_ref[...] + (1.0 - ADAM_B1) * g
        v_new = ADAM_B2 * v_ref[...] + (1.0 - ADAM_B2) * (g * g)
        m_hat = m_new / (1.0 - ADAM_B1 ** ADAM_STEP)
        v_hat = v_new / (1.0 - ADAM_B2 ** ADAM_STEP)
        d_ref[...] = -ADAM_LR * (m_hat / (jnp.sqrt(v_hat) + ADAM_EPS) + ADAM_WD * w_ref[...])
        nm_ref[...] = m_new
        nv_ref[...] = v_new

    rows = pl.BlockSpec((tr, C), lambda i: (i, 0))
    return pl.pallas_call(
        body, name=name, grid=(R // tr,),
        in_specs=[pl.BlockSpec((N_DEV, tr, C), lambda i: (0, i, 0)), rows, rows, rows],
        out_specs=[rows] * 4, out_shape=[S((R, C), F32)] * 4,
        compiler_params=_params("parallel"))(parts, w, m, v)


def _rows_of(n):
    return -(-n // LANES)


def _pack(pieces, total_rows, dtype, lead=()):
    out = []
    for p in pieces:
        flat = p.reshape(lead + (-1,)).astype(dtype)
        n = flat.shape[-1]
        pad = _rows_of(n) * LANES - n
        if pad:
            flat = jnp.pad(flat, [(0, 0)] * len(lead) + [(0, pad)])
        out.append(flat.reshape(lead + (-1, LANES)))
    used = sum(o.shape[-2] for o in out)
    if total_rows > used:
        out.append(jnp.zeros(lead + (total_rows - used, LANES), dtype))
    return jnp.concatenate(out, axis=len(lead))


def _unpack(buf, shapes, lead=()):
    res, r0 = [], 0
    for shp in shapes:
        n = int(np.prod(shp))
        nr = _rows_of(n)
        piece = lax.slice_in_dim(buf, r0, r0 + nr, axis=len(lead)).reshape(lead + (nr * LANES,))
        res.append(lax.slice_in_dim(piece, 0, n, axis=len(lead)).reshape(lead + tuple(shp)))
        r0 += nr
    return res


def _round_up(n, m):
    return -(-n // m) * m


BIG = ["mla_w_in", "mla_w_uq", "mla_w_uk", "mla_w_uv", "mla_w_o", "kv_w_shared", "swa_w_q", "swa_w_o",
       "mlp_w_up", "mlp_w_down"]
GAINS = ["mla_g_q", "mla_g_kv"]
SHARDED = BIG + GAINS
REPL = ["swa_sinks", "rel_bias", "ln_mix_g", "ln_mix_b", "ln_mlp_g", "ln_mlp_b"]
WEIGHTS = ["mla_w_in", "mla_g_q", "mla_g_kv", "mla_w_uq", "mla_w_uk", "mla_w_uv", "mla_w_o", "kv_w_shared",
           "swa_w_q", "swa_sinks", "swa_w_o", "rel_bias", "mlp_w_up", "mlp_w_down", "ln_mix_g", "ln_mix_b",
           "ln_mlp_g", "ln_mlp_b"]


def kernel(x, mla_w_in, mla_g_q, mla_g_kv, mla_w_uq, mla_w_uk, mla_w_uv, mla_w_o, kv_w_shared, swa_w_q, swa_sinks, swa_w_o, rel_bias, mlp_w_up, mlp_w_down, ln_mix_g, ln_mix_b, ln_mlp_g, ln_mlp_b, loss_target, m_mla_w_in, m_mla_g_q, m_mla_g_kv, m_mla_w_uq, m_mla_w_uk, m_mla_w_uv, m_mla_w_o, m_kv_w_shared, m_swa_w_q, m_swa_sinks, m_swa_w_o, m_rel_bias, m_mlp_w_up, m_mlp_w_down, m_ln_mix_g, m_ln_mix_b, m_ln_mlp_g, m_ln_mlp_b, v_mla_w_in, v_mla_g_q, v_mla_g_kv, v_mla_w_uq, v_mla_w_uk, v_mla_w_uv, v_mla_w_o, v_kv_w_shared, v_swa_w_q, v_swa_sinks, v_swa_w_o, v_rel_bias, v_mlp_w_up, v_mlp_w_down, v_ln_mix_g, v_ln_mix_b, v_ln_mlp_g, v_ln_mlp_b):
    args = dict(locals())
    W = {n: args[n] for n in WEIGHTS}
    M = {n: args["m_" + n] for n in WEIGHTS}
    V = {n: args["v_" + n] for n in WEIGHTS}
    T = x.shape[1]
    x2d = x.reshape(T, D_MODEL)
    tgt = loss_target.reshape(T, D_MODEL)
    H = MLA_HEADS

    big_rows = _round_up(sum(_rows_of(W[n].size) for n in BIG), 16)
    wb_local = _pack([W[n] for n in BIG], big_rows, _MXU_DTYPE)
    gains_local = _pack([W[n] for n in GAINS], 8, F32)
    wb_all, gains_all = _exchange([wb_local, gains_local], [], name="gather_weights")
    w_in_s, w_uq_s, w_uk_s, w_uv_s, w_o_s, w_kv_s, w_q_s, w_o2_s, w_up, w_dn = _unpack(
        wb_all, [W[n].shape for n in BIG], lead=(N_DEV,))
    g_q_s, g_kv_s = _unpack(gains_all, [W[n].shape for n in GAINS], lead=(N_DEV,))
    w_in = w_in_s.reshape(D_MODEL, -1)
    g_q = g_q_s.reshape(1, MLA_QR)
    g_kv = g_kv_s.reshape(1, MLA_C)
    w_uq = w_uq_s.reshape(MLA_QR, H, MLA_NOPE + MLA_ROPE)
    w_uq_n = w_uq[:, :, :MLA_NOPE].reshape(MLA_QR, H * MLA_NOPE)
    w_uq_r = w_uq[:, :, MLA_NOPE:].reshape(MLA_QR, H * MLA_ROPE)
    w_uk = w_uk_s.reshape(MLA_C, H, MLA_NOPE).transpose(1, 0, 2)
    w_uk_t = w_uk.transpose(0, 2, 1)
    w_uv = w_uv_s.reshape(MLA_C, H, MLA_V).transpose(1, 0, 2)
    w_o = w_o_s.reshape(H * MLA_V, D_MODEL)
    w_qkv = jnp.concatenate([w_q_s.reshape(D_MODEL, -1), w_kv_s.reshape(D_MODEL, -1)], axis=1)
    w_o2 = w_o2_s.reshape(SWA_QH * SWA_D, D_MODEL)
    ln = lambda a, l: a[l].reshape(1, D_MODEL)

    half = MLA_ROPE // 2
    inv = ROPE_THETA ** (-jnp.arange(half, dtype=F32) / half)
    ang = jnp.arange(T, dtype=F32)[:, None] * inv[None, :]
    cos = jnp.tile(jnp.concatenate([jnp.cos(ang), jnp.cos(ang)], -1), (1, H))
    sin = jnp.tile(jnp.concatenate([-jnp.sin(ang), jnp.sin(ang)], -1), (1, H))

    h, kc, qs, cq, qn = _mla_pre_fwd(x2d, w_in, g_q, g_kv, w_uq_n, w_uq_r, w_uk_t, cos, sin)
    olat, lse = _mla_attn_fwd(qs, kc)
    o_mla = _mla_uv_fwd(olat, w_uv)
    x1, x1b, xh1, rs1 = _proj_ln_fwd(o_mla, w_o, x2d, ln(ln_mix_g, 0), ln(ln_mix_b, 0), name="mla_out_ln_fwd")
    u0, x2, x2b, xh2, rs2 = _mlp_fwd(x1b, x1, w_up, w_dn, 0, ln(ln_mlp_g, 0), ln(ln_mlp_b, 0))
    onehot = _t5_onehot()
    bias = _mm(rel_bias.T, onehot, name="rel_bias_expand", precision=lax.Precision.HIGHEST, tn=8192).reshape(
        SWA_QH, SWA_BLOCK, 2 * SWA_BLOCK)
    qkv = _mm(x2b, w_qkv, name="swa_qkv_fwd", out_dtype=_MXU_DTYPE, tm=1024, tn=512, tk=1024)
    o_swa = _swa_attn_fwd(qkv, bias, swa_sinks)
    x3, x3b, xh3, rs3 = _proj_ln_fwd(o_swa, w_o2, x2, ln(ln_mix_g, 1), ln(ln_mix_b, 1), name="swa_out_ln_fwd")
    u1, x4, _, xh4, rs4 = _mlp_fwd(x3b, x3, w_up, w_dn, 1, ln(ln_mlp_g, 1), ln(ln_mlp_b, 1))
    loss_part, dx4 = _loss_head(x4, tgt)
    loss = lax.psum(loss_part[0, 0], AXES)

    nj = w_up.shape[0]
    dx3, du1, dy4b, dg_mlp1, db_mlp1 = _mlp_bwd_dx(dx4, xh4, rs4, ln(ln_mlp_g, 1), u1, w_up, w_dn, 1)
    g_dn1, g_up1 = _mlp_bwd_dw(u1, dy4b, x3b, du1, 1, nj=nj)
    dres3, do_swa, g_o2, dg_mix1, db_mix1 = _proj_ln_bwd(dx3, xh3, rs3, ln(ln_mix_g, 1), o_swa, w_o2,
                                                         name="swa_out_ln_bwd")
    dqkv, dbias, dsink = _swa_attn_bwd(qkv, o_swa, do_swa, bias, swa_sinks)
    g_rel = _mm(onehot, dbias.reshape(SWA_QH, -1), name="rel_bias_grad", tb=True, precision=lax.Precision.HIGHEST,
                tk=8192)
    g_sinks = dsink[:, 0].reshape(1, SWA_QH)
    dx2 = _mm(dqkv, w_qkv, name="swa_qkv_bwd_dx", tb=True, add=dres3, tm=1024, tn=1024, tk=512)
    g_qkv = _mm(x2b, dqkv, name="swa_qkv_bwd_dw", ta=True, tm=1024, tn=512, tk=1024)
    dx1, du0, dy2b, dg_mlp0, db_mlp0 = _mlp_bwd_dx(dx2, xh2, rs2, ln(ln_mlp_g, 0), u0, w_up, w_dn, 0)
    g_dn0, g_up0 = _mlp_bwd_dw(u0, dy2b, x1b, du0, 0, nj=nj)
    dres1, do_mla, g_o, dg_mix0, db_mix0 = _proj_ln_bwd(dx1, xh1, rs1, ln(ln_mix_g, 0), o_mla, w_o,
                                                        name="mla_out_ln_bwd")
    dol, delta, g_uv = _mla_uv_bwd(do_mla, olat, w_uv)
    dqs, dkc, dv = _mla_attn_bwd(qs, kc, dol, lse, delta)
    grad_x, g_in, g_uq_n, g_uq_r, g_uk, g_gq, g_gkv = _mla_pre_bwd(
        dqs, dkc, dv, h, x2d, dres1, cq, qn, cos, sin, w_in, g_q, g_kv, w_uq_n, w_uq_r, w_uk)

    dq_cols = SWA_QH * SWA_D
    full_grads = {
        "mla_w_in": g_in,
        "mla_w_uq": jnp.concatenate([g_uq_n.reshape(MLA_QR, H, MLA_NOPE), g_uq_r.reshape(MLA_QR, H, MLA_ROPE)], -1),
        "mla_w_uk": g_uk.transpose(1, 0, 2),
        "mla_w_uv": g_uv.transpose(1, 0, 2),
        "mla_w_o": g_o,
        "kv_w_shared": g_qkv[:, dq_cols:],
        "swa_w_q": g_qkv[:, :dq_cols],
        "swa_w_o": g_o2,
        "mlp_w_up": jnp.stack([g_up0, g_up1], axis=1),
        "mlp_w_down": jnp.stack([g_dn0, g_dn1], axis=1),
        "mla_g_q": g_gq,
        "mla_g_kv": g_gkv,
    }
    shard_rows = _round_up(sum(_rows_of(W[n].size) for n in SHARDED), 256)
    g_parts = _pack([full_grads[n] for n in SHARDED], shard_rows, F32, lead=(N_DEV,))
    repl_grads = {
        "swa_sinks": g_sinks, "rel_bias": g_rel,
        "ln_mix_g": jnp.concatenate([dg_mix0, dg_mix1], 0), "ln_mix_b": jnp.concatenate([db_mix0, db_mix1], 0),
        "ln_mlp_g": jnp.concatenate([dg_mlp0, dg_mlp1], 0), "ln_mlp_b": jnp.concatenate([db_mlp0, db_mlp1], 0),
    }
    repl_rows = _round_up(sum(_rows_of(W[n].size) for n in REPL), 8)
    r_part = _pack([repl_grads[n] for n in REPL], repl_rows, F32)

    r_all, g_recv = _exchange([r_part], [g_parts], name="exchange_grads")
    pk = lambda d, names, rows: _pack([d[n] for n in names], rows, F32)
    sh = _adamw(g_recv, pk(W, SHARDED, shard_rows), pk(M, SHARDED, shard_rows), pk(V, SHARDED, shard_rows),
                name="adamw_sharded")
    rp = _adamw(r_all, pk(W, REPL, repl_rows), pk(M, REPL, repl_rows), pk(V, REPL, repl_rows), name="adamw_replicated")
    outs = []
    for k in range(4):
        d = dict(zip(SHARDED, _unpack(sh[k], [W[n].shape for n in SHARDED])))
        d.update(zip(REPL, _unpack(rp[k], [W[n].shape for n in REPL])))
        outs.append(d)
    return (loss, grad_x.reshape(x.shape), *[outs[0][n] for n in WEIGHTS], *[outs[1][n] for n in WEIGHTS],
            *[outs[2][n] for n in WEIGHTS], *[outs[3][n] for n in WEIGHTS])
```

```python
import functools
import math

import numpy as np
import jax
import jax.numpy as jnp
from jax import lax
from jax.experimental import pallas as pl
from jax.experimental.pallas import tpu as pltpu

F32 = jnp.float32
_MXU_DTYPE = jnp.bfloat16

D_MODEL = 1024
DEPTH = 2
MLA_HEADS = 8
MLA_NOPE = 128
MLA_ROPE = 64
MLA_V = 128
MLA_QR = 384
MLA_C = 256
MLA_DK = 384
ROPE_THETA = 10000.0
SWA_QH = 16
SWA_KVH = 4
SWA_D = 64
SWA_BLOCK = 128
REL_BUCKETS = 32
REL_MAX_DIST = 128
D_FF = 4096
LN_EPS = 1e-5
RMS_EPS = 1e-6
ALPHA = (2 * DEPTH) ** 0.25
ADAM_LR, ADAM_B1, ADAM_B2, ADAM_EPS, ADAM_WD, ADAM_STEP = 0.001, 0.9, 0.999, 1e-08, 0.01, 10

N_DEV = 8
AXES = ("x", "y", "c")
V7X_VMEM_BYTES = 64 * 1024 * 1024
VMEM_LIMIT = V7X_VMEM_BYTES - 8 * 1024 * 1024
LANES = 1024
ATT_TQ = 256
ATT_TK = 512
ATT_HEAD_GROUP = 2

NT = (((1,), (1,)), ((), ()))
TN = (((0,), (0,)), ((), ()))
S = jax.ShapeDtypeStruct


def _params(*sem, vmem=VMEM_LIMIT):
    return pltpu.CompilerParams(dimension_semantics=sem, vmem_limit_bytes=vmem)


def _dot(a, b, dims=None, precision=None):
    if dims is None:
        return jnp.dot(a, b, preferred_element_type=F32, precision=precision)
    return lax.dot_general(a, b, dims, preferred_element_type=F32, precision=precision)


def _mx(v):
    return v.astype(_MXU_DTYPE)


def _swap_halves_64(v):
    return jnp.concatenate([v[:, 32:], v[:, :32]], axis=-1)


def _swap_halves_groups(v):
    n = v.shape[-1]
    lane = lax.broadcasted_iota(jnp.int32, v.shape, 1)
    return jnp.where(lane % 64 < 32, pltpu.roll(v, n - 32, 1), pltpu.roll(v, 32, 1))


def _mm(a, b, *, name, ta=False, tb=False, add=None, out_dtype=F32, tm=512, tn=512, tk=512, precision=None):
    M, K = (a.shape[1], a.shape[0]) if ta else a.shape
    N = b.shape[0] if tb else b.shape[1]
    tm, tn, tk = min(tm, M), min(tn, N), min(tk, K)
    assert M % tm == 0 and N % tn == 0 and K % tk == 0, (M, N, K, tm, tn, tk)
    nk = K // tk
    dims = (((0 if ta else 1,), (1 if tb else 0,)), ((), ()))
    has_add = add is not None

    def body(*refs):
        if has_add:
            a_ref, b_ref, add_ref, o_ref, acc = refs
        else:
            a_ref, b_ref, o_ref, acc = refs
        k = pl.program_id(2)

        @pl.when(k == 0)
        def _():
            acc[...] = jnp.zeros_like(acc)

        av, bv = a_ref[...], b_ref[...]
        if precision is None:
            av, bv = _mx(av), _mx(bv)
        acc[...] += _dot(av, bv, dims, precision)

        @pl.when(k == nk - 1)
        def _():
            r = acc[...]
            if has_add:
                r = r + add_ref[...]
            o_ref[...] = r.astype(out_dtype)

    a_spec = pl.BlockSpec((tk, tm), lambda i, j, k: (k, i)) if ta else pl.BlockSpec((tm, tk), lambda i, j, k: (i, k))
    b_spec = pl.BlockSpec((tn, tk), lambda i, j, k: (j, k)) if tb else pl.BlockSpec((tk, tn), lambda i, j, k: (k, j))
    in_specs = [a_spec, b_spec]
    args = [a, b]
    if has_add:
        in_specs.append(pl.BlockSpec((tm, tn), lambda i, j, k: (i, j)))
        args.append(add)
    return pl.pallas_call(
        body, name=name, grid=(M // tm, N // tn, nk), in_specs=in_specs,
        out_specs=pl.BlockSpec((tm, tn), lambda i, j, k: (i, j)), out_shape=S((M, N), out_dtype),
        scratch_shapes=[pltpu.VMEM((tm, tn), F32)],
        compiler_params=_params("parallel", "parallel", "arbitrary"))(*args)


def _ln_fwd_math(z, g, b):
    mu = jnp.mean(z, axis=-1, keepdims=True)
    zc = z - mu
    var = jnp.mean(zc * zc, axis=-1, keepdims=True)
    rstd = lax.rsqrt(var + LN_EPS)
    xhat = zc * rstd
    return xhat * g + b, xhat, rstd


def _ln_bwd_math(dxo, xhat, rstd, g):
    dxh = dxo * g
    m1 = jnp.mean(dxh, axis=-1, keepdims=True)
    m2 = jnp.mean(dxh * xhat, axis=-1, keepdims=True)
    dz = rstd * (dxh - m1 - xhat * m2)
    dg = jnp.sum(dxo * xhat, axis=0, keepdims=True)
    db = jnp.sum(dxo, axis=0, keepdims=True)
    return dz, dg, db


def _rms_fwd_math(xr, g):
    r = lax.rsqrt(jnp.mean(xr * xr, axis=-1, keepdims=True) + RMS_EPS)
    return xr * r * g


def _rms_bwd_math(dy, xr, g):
    r = lax.rsqrt(jnp.mean(xr * xr, axis=-1, keepdims=True) + RMS_EPS)
    gy = dy * g
    dx = r * gy - xr * (r * r * r) * jnp.mean(gy * xr, axis=-1, keepdims=True)
    dg = jnp.sum(dy * xr * r, axis=0, keepdims=True)
    return dx, dg


def _mla_pre_fwd(x, w_in, g_q, g_kv, w_uq_n, w_uq_r, w_uk_t, cos, sin):
    T = x.shape[0]
    tm = min(ATT_TQ, T)
    nq = T // tm
    H = MLA_HEADS

    def body(x_ref, win_ref, gq_ref, gkv_ref, wn_ref, wr_ref, wuk_ref, cos_ref, sin_ref,
             h_ref, kc_ref, qs_ref, cq_ref, qn_ref):
        h = _dot(_mx(x_ref[...]), win_ref[...])
        h_ref[...] = h
        cos_v, sin_v = cos_ref[...], sin_ref[...]
        cq = _mx(_rms_fwd_math(h[:, :MLA_QR], gq_ref[...]))
        ckv = _rms_fwd_math(h[:, MLA_QR:MLA_QR + MLA_C], gkv_ref[...])
        krr = h[:, MLA_QR + MLA_C:]
        kr = krr * cos_v[:, :MLA_ROPE] + _swap_halves_64(krr) * sin_v[:, :MLA_ROPE]
        kc_ref[:, 0:MLA_C] = _mx(ckv)
        kc_ref[:, MLA_C:MLA_C + MLA_ROPE] = _mx(kr)
        kc_ref[:, MLA_C + MLA_ROPE:] = jnp.zeros((tm, MLA_DK - MLA_C - MLA_ROPE), _MXU_DTYPE)
        cq_ref[...] = cq
        qnb = _mx(_dot(cq, wn_ref[...]))
        qn_ref[...] = qnb
        qr = _dot(cq, wr_ref[...])
        qrr = qr * cos_v + _swap_halves_groups(qr) * sin_v
        for hd in range(H):
            ql = _dot(qnb[:, MLA_NOPE * hd:MLA_NOPE * (hd + 1)], wuk_ref[hd])
            qs_ref[0, hd, :, 0:MLA_C] = _mx(ql)
            qs_ref[0, hd, :, MLA_C:MLA_C + MLA_ROPE] = _mx(qrr[:, MLA_ROPE * hd:MLA_ROPE * (hd + 1)])
            qs_ref[0, hd, :, MLA_C + MLA_ROPE:] = jnp.zeros((tm, MLA_DK - MLA_C - MLA_ROPE), _MXU_DTYPE)

    full = lambda shp: pl.BlockSpec(shp, lambda i: (0,) * len(shp))
    rows = lambda n: pl.BlockSpec((tm, n), lambda i: (i, 0))
    n_in = w_in.shape[1]
    return pl.pallas_call(
        body, name="mla_pre_fwd", grid=(nq,),
        in_specs=[rows(D_MODEL), full(w_in.shape), full(g_q.shape), full(g_kv.shape), full(w_uq_n.shape),
                  full(w_uq_r.shape), full(w_uk_t.shape), rows(H * MLA_ROPE), rows(H * MLA_ROPE)],
        out_specs=[rows(n_in), rows(MLA_DK), pl.BlockSpec((1, H, tm, MLA_DK), lambda i: (i, 0, 0, 0)),
                   rows(MLA_QR), rows(H * MLA_NOPE)],
        out_shape=[S((T, n_in), F32), S((T, MLA_DK), _MXU_DTYPE), S((nq, H, tm, MLA_DK), _MXU_DTYPE),
                   S((T, MLA_QR), _MXU_DTYPE), S((T, H * MLA_NOPE), _MXU_DTYPE)],
        compiler_params=_params("parallel"))(x, w_in, g_q, g_kv, w_uq_n, w_uq_r, w_uk_t, cos, sin)


def _att_steps(T, tq, tk):
    qi, kj = [], []
    for i in range(T // tq):
        for j in range((i * tq + tq - 1) // tk + 1):
            qi.append(i)
            kj.append(j)
    return jnp.asarray(np.array(qi, np.int32)), jnp.asarray(np.array(kj, np.int32))


def _mla_attn_fwd(qs, kc):
    nq, H, tq, DK = qs.shape
    T = kc.shape[0]
    tk = min(ATT_TK, T)
    scale = (MLA_NOPE + MLA_ROPE) ** -0.5
    c2 = scale * math.log2(math.e)
    qi, kj = _att_steps(T, tq, tk)
    hg = ATT_HEAD_GROUP
    R = hg * tq

    def body(qi_ref, kj_ref, q_ref, k_ref, o_ref, lse_ref, m_sc, l_sc, acc_sc):
        st = pl.program_id(0)
        i, j = qi_ref[st], kj_ref[st]
        j_last = (i * tq + tq - 1) // tk

        @pl.when(j == 0)
        def _():
            m_sc[...] = jnp.full_like(m_sc, -jnp.inf)
            l_sc[...] = jnp.zeros_like(l_sc)
            acc_sc[...] = jnp.zeros_like(acc_sc)

        def step(masked):
            k = k_ref[pl.ds(pl.multiple_of(j * tk, tk), tk), :]
            v = k[:, :MLA_C]
            if masked:
                row = lax.broadcasted_iota(jnp.int32, (R, tk), 0) % tq + i * tq
                col = lax.broadcasted_iota(jnp.int32, (R, tk), 1) + j * tk
                causal = col <= row
            n_g = H // hg
            qk = lambda g: _dot(q_ref[0, g * hg:(g + 1) * hg].reshape(R, DK), k, NT)
            s_next = qk(0)
            for g in range(n_g):
                rs = slice(g * R, (g + 1) * R)
                s = s_next
                if g + 1 < n_g:
                    s_next = qk(g + 1)
                if masked:
                    s = jnp.where(causal, s, -jnp.inf)
                m_prev = m_sc[rs]
                m_new = jnp.maximum(m_prev, jnp.max(s, axis=1, keepdims=True))
                a = jnp.exp2((m_prev - m_new) * c2)
                p = jnp.exp2((s - m_new) * c2)
                l_sc[rs] = a * l_sc[rs] + jnp.sum(p, axis=1, keepdims=True)
                acc_sc[rs] = a * acc_sc[rs] + _dot(_mx(p), v)
                m_sc[rs] = m_new

        pl.when(j == j_last)(lambda: step(True))
        pl.when(j != j_last)(lambda: step(False))

        @pl.when(j == j_last)
        def _():
            o_ref[0] = _mx(acc_sc[...] / l_sc[...]).reshape(H, tq, MLA_C)
            lse_ref[0] = (m_sc[...] * scale + jnp.log(l_sc[...])).reshape(H, tq, 1)

    gs = pltpu.PrefetchScalarGridSpec(
        num_scalar_prefetch=2, grid=(int(qi.shape[0]),),
        in_specs=[pl.BlockSpec((1, H, tq, DK), lambda s, qi, kj: (qi[s], 0, 0, 0)),
                  pl.BlockSpec(memory_space=pltpu.VMEM)],
        out_specs=[pl.BlockSpec((1, H, tq, MLA_C), lambda s, qi, kj: (qi[s], 0, 0, 0)),
                   pl.BlockSpec((1, H, tq, 1), lambda s, qi, kj: (qi[s], 0, 0, 0))],
        scratch_shapes=[pltpu.VMEM((H * tq, 1), F32), pltpu.VMEM((H * tq, 1), F32), pltpu.VMEM((H * tq, MLA_C), F32)])
    return pl.pallas_call(
        body, name="mla_attn_fwd", grid_spec=gs,
        out_shape=[S((nq, H, tq, MLA_C), _MXU_DTYPE), S((nq, H, tq, 1), F32)],
        compiler_params=_params("arbitrary"))(qi, kj, qs, kc)


def _mla_attn_bwd(qs, kc, dol, lse, delta):
    nq, H, tq, DK = qs.shape
    T = kc.shape[0]
    tk = min(ATT_TK, T)
    scale = (MLA_NOPE + MLA_ROPE) ** -0.5
    log2e = math.log2(math.e)
    qi, kj = _att_steps(T, tq, tk)
    n_steps = int(qi.shape[0])
    hg = ATT_HEAD_GROUP
    R = hg * tq

    def body(qi_ref, kj_ref, q_ref, k_ref, do_ref, lse_ref, dl_ref, dq_ref, dk_ref, dv_ref, dk_acc, dv_acc, sem):
        st = pl.program_id(0)
        i, j = qi_ref[st], kj_ref[st]
        j_last = (i * tq + tq - 1) // tk

        @pl.when(st == 0)
        def _():
            dk_acc[...] = jnp.zeros_like(dk_acc)
            dv_acc[...] = jnp.zeros_like(dv_acc)

        @pl.when(j == 0)
        def _():
            dq_ref[...] = jnp.zeros_like(dq_ref)

        def step(masked):
            koff = pl.multiple_of(j * tk, tk)
            k = k_ref[pl.ds(koff, tk), :]
            v = k[:, :MLA_C]
            if masked:
                row = lax.broadcasted_iota(jnp.int32, (R, tk), 0) % tq + i * tq
                col = lax.broadcasted_iota(jnp.int32, (R, tk), 1) + j * tk
                causal = col <= row
            dk_c = jnp.zeros((tk, DK), F32)
            dv_c = jnp.zeros((tk, MLA_C), F32)
            n_g = H // hg

            def scores(g):
                hs = slice(g * hg, (g + 1) * hg)
                q = q_ref[0, hs].reshape(R, DK)
                do = do_ref[0, hs].reshape(R, MLA_C)
                return q, do, _dot(q, k, NT), _dot(do, v, NT)

            nxt = scores(0)
            for g in range(n_g):
                hs = slice(g * hg, (g + 1) * hg)
                q, do, s, dp = nxt
                if g + 1 < n_g:
                    nxt = scores(g + 1)
                p = jnp.exp2(s * (scale * log2e) - lse_ref[0, hs].reshape(R, 1) * log2e)
                if masked:
                    p = jnp.where(causal, p, 0.0)
                dsb = _mx(p * (dp - dl_ref[0, hs].reshape(R, 1)))
                pb = _mx(p)
                dq_ref[0, hs] += _dot(dsb, k).reshape(hg, tq, DK)
                dk_c = dk_c + _dot(dsb, q, TN)
                dv_c = dv_c + _dot(pb, do, TN)
            dk_acc[pl.ds(koff, tk), :] += dk_c * scale
            dv_acc[pl.ds(koff, tk), :] += dv_c

        pl.when(j == j_last)(lambda: step(True))
        pl.when(j != j_last)(lambda: step(False))

        @pl.when(j == j_last)
        def _():
            dq_ref[...] = dq_ref[...] * scale

        @pl.when(st == n_steps - 1)
        def _():
            c1 = pltpu.make_async_copy(dk_acc, dk_ref, sem.at[0])
            c2 = pltpu.make_async_copy(dv_acc, dv_ref, sem.at[1])
            c1.start()
            c2.start()
            c1.wait()
            c2.wait()

    blk = lambda n: pl.BlockSpec((1, H, tq, n), lambda s, qi, kj: (qi[s], 0, 0, 0))
    gs = pltpu.PrefetchScalarGridSpec(
        num_scalar_prefetch=2, grid=(n_steps,),
        in_specs=[blk(DK), pl.BlockSpec(memory_space=pltpu.VMEM), blk(MLA_C), blk(1), blk(1)],
        out_specs=[blk(DK), pl.BlockSpec(memory_space=pl.ANY), pl.BlockSpec(memory_space=pl.ANY)],
        scratch_shapes=[pltpu.VMEM((T, DK), F32), pltpu.VMEM((T, MLA_C), F32), pltpu.SemaphoreType.DMA((2,))])
    return pl.pallas_call(
        body, name="mla_attn_bwd", grid_spec=gs,
        out_shape=[S((nq, H, tq, DK), F32), S((T, DK), F32), S((T, MLA_C), F32)],
        compiler_params=_params("arbitrary"))(qi, kj, qs, kc, dol, lse, delta)


def _mla_uv_fwd(olat, w_uv):
    nq, H, tq, C = olat.shape
    T = nq * tq

    def body(ol_ref, wuv_ref, o_ref):
        for hd in range(H):
            o_ref[:, MLA_V * hd:MLA_V * (hd + 1)] = _mx(_dot(ol_ref[0, hd], wuv_ref[hd]))

    return pl.pallas_call(
        body, name="mla_uv_fwd", grid=(nq,),
        in_specs=[pl.BlockSpec((1, H, tq, C), lambda i: (i, 0, 0, 0)), pl.BlockSpec(w_uv.shape, lambda i: (0, 0, 0))],
        out_specs=pl.BlockSpec((tq, H * MLA_V), lambda i: (i, 0)), out_shape=S((T, H * MLA_V), _MXU_DTYPE),
        compiler_params=_params("parallel"))(olat, w_uv)


def _mla_uv_bwd(do, olat, w_uv):
    nq, H, tq, C = olat.shape

    def body(do_ref, ol_ref, wuv_ref, dol_ref, dl_ref, dw_ref):
        @pl.when(pl.program_id(0) == 0)
        def _():
            dw_ref[...] = jnp.zeros_like(dw_ref)

        dov = do_ref[...]
        for hd in range(H):
            doh = _mx(dov[:, MLA_V * hd:MLA_V * (hd + 1)])
            ol = ol_ref[0, hd]
            dol = _dot(doh, wuv_ref[hd], NT)
            dol_ref[0, hd] = _mx(dol)
            dl_ref[0, hd] = jnp.sum(dol * ol.astype(F32), axis=1, keepdims=True)
            dw_ref[hd] += _dot(ol, doh, TN)

    blk = lambda n: pl.BlockSpec((1, H, tq, n), lambda i: (i, 0, 0, 0))
    return pl.pallas_call(
        body, name="mla_uv_bwd", grid=(nq,),
        in_specs=[pl.BlockSpec((tq, H * MLA_V), lambda i: (i, 0)), blk(C), pl.BlockSpec(w_uv.shape, lambda i: (0, 0, 0))],
        out_specs=[blk(C), blk(1), pl.BlockSpec(w_uv.shape, lambda i: (0, 0, 0))],
        out_shape=[S(olat.shape, _MXU_DTYPE), S((nq, H, tq, 1), F32), S(w_uv.shape, F32)],
        compiler_params=_params("arbitrary"))(do, olat, w_uv)


def _mla_pre_bwd(dqs, dkc, dv, h, x, dres, cq, qn, cos, sin, w_in, g_q, g_kv, w_uq_n, w_uq_r, w_uk):
    nq, H, tm, DK = dqs.shape
    T = nq * tm
    n_in = w_in.shape[1]

    def body(dqs_ref, dkc_ref, dv_ref, h_ref, x_ref, dres_ref, cq_ref, qn_ref, cos_ref, sin_ref,
             win_ref, gq_ref, gkv_ref, wn_ref, wr_ref, wuk_ref,
             gx_ref, dwin_ref, dwn_ref, dwr_ref, dwuk_ref, dgq_ref, dgkv_ref, dqn_sc, dqr_sc, dh_sc):
        @pl.when(pl.program_id(0) == 0)
        def _():
            for r in (dwin_ref, dwn_ref, dwr_ref, dwuk_ref, dgq_ref, dgkv_ref):
                r[...] = jnp.zeros_like(r)

        cos_v, sin_v = cos_ref[...], sin_ref[...]
        qnb = qn_ref[...]
        for hd in range(H):
            dqh = dqs_ref[0, hd]
            dql = _mx(dqh[:, :MLA_C])
            dqn_sc[:, MLA_NOPE * hd:MLA_NOPE * (hd + 1)] = _dot(dql, wuk_ref[hd])
            dwuk_ref[hd] += _dot(dql, qnb[:, MLA_NOPE * hd:MLA_NOPE * (hd + 1)], TN)
            dqr_sc[:, MLA_ROPE * hd:MLA_ROPE * (hd + 1)] = dqh[:, MLA_C:MLA_C + MLA_ROPE]
        dqr = dqr_sc[...]
        dqrb = _mx(dqr * cos_v + _swap_halves_groups(dqr * sin_v))
        dqnb = _mx(dqn_sc[...])
        cq = cq_ref[...]
        dwn_ref[...] += _dot(cq, dqnb, TN)
        dwr_ref[...] += _dot(cq, dqrb, TN)
        dcq = _dot(dqnb, wn_ref[...], NT) + _dot(dqrb, wr_ref[...], NT)
        hv = h_ref[...]
        dxq, dgq = _rms_bwd_math(dcq, hv[:, :MLA_QR], gq_ref[...])
        dgq_ref[...] += dgq
        dkcv = dkc_ref[...]
        dckv = dkcv[:, :MLA_C] + dv_ref[...]
        dxkv, dgkv = _rms_bwd_math(dckv, hv[:, MLA_QR:MLA_QR + MLA_C], gkv_ref[...])
        dgkv_ref[...] += dgkv
        dkr = dkcv[:, MLA_C:MLA_C + MLA_ROPE]
        dkr_raw = dkr * cos_v[:, :MLA_ROPE] + _swap_halves_64(dkr * sin_v[:, :MLA_ROPE])
        dh_sc[:, 0:MLA_QR] = dxq
        dh_sc[:, MLA_QR:MLA_QR + MLA_C] = dxkv
        dh_sc[:, MLA_QR + MLA_C:] = dkr_raw
        dhb = _mx(dh_sc[...])
        gx_ref[...] = dres_ref[...] + _dot(dhb, win_ref[...], NT)
        dwin_ref[...] += _dot(_mx(x_ref[...]), dhb, TN)

    full = lambda shp: pl.BlockSpec(shp, lambda i: (0,) * len(shp))
    rows = lambda n: pl.BlockSpec((tm, n), lambda i: (i, 0))
    return pl.pallas_call(
        body, name="mla_pre_bwd", grid=(nq,),
        in_specs=[pl.BlockSpec((1, H, tm, DK), lambda i: (i, 0, 0, 0)), rows(DK), rows(MLA_C), rows(n_in),
                  rows(D_MODEL), rows(D_MODEL), rows(MLA_QR), rows(H * MLA_NOPE), rows(H * MLA_ROPE), rows(H * MLA_ROPE),
                  full(w_in.shape), full(g_q.shape), full(g_kv.shape), full(w_uq_n.shape), full(w_uq_r.shape),
                  full(w_uk.shape)],
        out_specs=[rows(D_MODEL), full(w_in.shape), full(w_uq_n.shape), full(w_uq_r.shape), full(w_uk.shape),
                   full(g_q.shape), full(g_kv.shape)],
        out_shape=[S((T, D_MODEL), F32), S(w_in.shape, F32), S(w_uq_n.shape, F32), S(w_uq_r.shape, F32),
                   S(w_uk.shape, F32), S(g_q.shape, F32), S(g_kv.shape, F32)],
        scratch_shapes=[pltpu.VMEM((tm, H * MLA_NOPE), F32), pltpu.VMEM((tm, H * MLA_ROPE), F32),
                        pltpu.VMEM((tm, n_in), F32)],
        compiler_params=_params("arbitrary"))(dqs, dkc, dv, h, x, dres, cq, qn, cos, sin, w_in, g_q, g_kv,
                                              w_uq_n, w_uq_r, w_uk)


def _proj_ln_fwd(a, w, xres, g, b, *, name, tm=512):
    T, K = a.shape
    tm = min(tm, T)

    def body(a_ref, w_ref, x_ref, g_ref, b_ref, xo_ref, xob_ref, xh_ref, rs_ref):
        z = ALPHA * x_ref[...] + _dot(a_ref[...], w_ref[...])
        xo, xhat, rstd = _ln_fwd_math(z, g_ref[...], b_ref[...])
        xo_ref[...] = xo
        xob_ref[...] = _mx(xo)
        xh_ref[...] = xhat
        rs_ref[...] = rstd

    rows = lambda n: pl.BlockSpec((tm, n), lambda i: (i, 0))
    full = lambda shp: pl.BlockSpec(shp, lambda i: (0,) * len(shp))
    return pl.pallas_call(
        body, name=name, grid=(T // tm,),
        in_specs=[rows(K), full(w.shape), rows(D_MODEL), full(g.shape), full(b.shape)],
        out_specs=[rows(D_MODEL), rows(D_MODEL), rows(D_MODEL), rows(1)],
        out_shape=[S((T, D_MODEL), F32), S((T, D_MODEL), _MXU_DTYPE), S((T, D_MODEL), F32), S((T, 1), F32)],
        compiler_params=_params("parallel"))(a, w, xres, g, b)


def _proj_ln_bwd(dxo, xhat, rstd, g, a, w, *, name, tm=512):
    T, K = a.shape
    tm = min(tm, T)

    def body(dxo_ref, xh_ref, rs_ref, g_ref, a_ref, w_ref, dres_ref, da_ref, dw_ref, dg_ref, db_ref):
        @pl.when(pl.program_id(0) == 0)
        def _():
            for r in (dw_ref, dg_ref, db_ref):
                r[...] = jnp.zeros_like(r)

        dz, dg, db = _ln_bwd_math(dxo_ref[...], xh_ref[...], rs_ref[...], g_ref[...])
        dg_ref[...] += dg
        db_ref[...] += db
        dres_ref[...] = ALPHA * dz
        dzb = _mx(dz)
        da_ref[...] = _dot(dzb, w_ref[...], NT)
        dw_ref[...] += _dot(a_ref[...], dzb, TN)

    rows = lambda n: pl.BlockSpec((tm, n), lambda i: (i, 0))
    full = lambda shp: pl.BlockSpec(shp, lambda i: (0,) * len(shp))
    return pl.pallas_call(
        body, name=name, grid=(T // tm,),
        in_specs=[rows(D_MODEL), rows(D_MODEL), rows(1), full(g.shape), rows(K), full(w.shape)],
        out_specs=[rows(D_MODEL), rows(K), full(w.shape), full(g.shape), full(g.shape)],
        out_shape=[S((T, D_MODEL), F32), S((T, K), F32), S(w.shape, F32), S(g.shape, F32), S(g.shape, F32)],
        compiler_params=_params("arbitrary"))(dxo, xhat, rstd, g, a, w)


def _mlp_fwd(xb, xres, w_up, w_dn, layer, g, b, *, tm=1024):
    T = xb.shape[0]
    tm = min(tm, T)
    nj, _, _, fc = w_up.shape

    def body(xb_ref, x_ref, wu_ref, wd_ref, g_ref, b_ref, u_ref, xo_ref, xob_ref, xh_ref, rs_ref, acc):
        j = pl.program_id(1)

        @pl.when(j == 0)
        def _():
            acc[...] = ALPHA * x_ref[...]

        u = _dot(xb_ref[...], wu_ref[...])
        u_ref[...] = _mx(u)
        r = jnp.maximum(u, 0.0)
        acc[...] += _dot(_mx(r * r), wd_ref[...])

        @pl.when(j == nj - 1)
        def _():
            xo, xhat, rstd = _ln_fwd_math(acc[...], g_ref[...], b_ref[...])
            xo_ref[...] = xo
            xob_ref[...] = _mx(xo)
            xh_ref[...] = xhat
            rs_ref[...] = rstd

    rows = lambda n: pl.BlockSpec((tm, n), lambda i, j: (i, 0))
    full = lambda shp: pl.BlockSpec(shp, lambda i, j: (0,) * len(shp))
    return pl.pallas_call(
        body, name=f"mlp_fwd_{layer}", grid=(T // tm, nj),
        in_specs=[rows(D_MODEL), rows(D_MODEL),
                  pl.BlockSpec((None, None, D_MODEL, fc), lambda i, j: (j, layer, 0, 0)),
                  pl.BlockSpec((None, None, fc, D_MODEL), lambda i, j: (j, layer, 0, 0)),
                  full(g.shape), full(b.shape)],
        out_specs=[pl.BlockSpec((tm, fc), lambda i, j: (i, j)), rows(D_MODEL), rows(D_MODEL), rows(D_MODEL), rows(1)],
        out_shape=[S((T, nj * fc), _MXU_DTYPE), S((T, D_MODEL), F32), S((T, D_MODEL), _MXU_DTYPE),
                   S((T, D_MODEL), F32), S((T, 1), F32)],
        scratch_shapes=[pltpu.VMEM((tm, D_MODEL), F32)],
        compiler_params=_params("parallel", "arbitrary"))(xb, xres, w_up, w_dn, g, b)


def _mlp_bwd_dx(dxo, xhat, rstd, g, u, w_up, w_dn, layer, *, tm=1024):
    T = dxo.shape[0]
    tm = min(tm, T)
    nj, _, _, fc = w_up.shape

    def body(dxo_ref, xh_ref, rs_ref, g_ref, u_ref, wu_ref, wd_ref, dx_ref, du_ref, dyb_ref, dg_ref, db_ref, acc, dy_sc):
        i, j = pl.program_id(0), pl.program_id(1)

        @pl.when((i == 0) & (j == 0))
        def _():
            dg_ref[...] = jnp.zeros_like(dg_ref)
            db_ref[...] = jnp.zeros_like(db_ref)

        @pl.when(j == 0)
        def _():
            dz, dg, db = _ln_bwd_math(dxo_ref[...], xh_ref[...], rs_ref[...], g_ref[...])
            dg_ref[...] += dg
            db_ref[...] += db
            acc[...] = ALPHA * dz
            dy_sc[...] = _mx(dz)
            dyb_ref[...] = _mx(dz)

        r = jnp.maximum(u_ref[...].astype(F32), 0.0)
        da = _dot(dy_sc[...], wd_ref[...], NT)
        dub = _mx(da * (2.0 * r))
        du_ref[...] = dub
        acc[...] += _dot(dub, wu_ref[...], NT)

        @pl.when(j == nj - 1)
        def _():
            dx_ref[...] = acc[...]

    rows = lambda n: pl.BlockSpec((tm, n), lambda i, j: (i, 0))
    full = lambda shp: pl.BlockSpec(shp, lambda i, j: (0,) * len(shp))
    return pl.pallas_call(
        body, name=f"mlp_bwd_dx_{layer}", grid=(T // tm, nj),
        in_specs=[rows(D_MODEL), rows(D_MODEL), rows(1), full(g.shape), pl.BlockSpec((tm, fc), lambda i, j: (i, j)),
                  pl.BlockSpec((None, None, D_MODEL, fc), lambda i, j: (j, layer, 0, 0)),
                  pl.BlockSpec((None, None, fc, D_MODEL), lambda i, j: (j, layer, 0, 0))],
        out_specs=[rows(D_MODEL), pl.BlockSpec((tm, fc), lambda i, j: (i, j)), rows(D_MODEL), full(g.shape), full(g.shape)],
        out_shape=[S((T, D_MODEL), F32), S((T, nj * fc), _MXU_DTYPE), S((T, D_MODEL), _MXU_DTYPE),
                   S(g.shape, F32), S(g.shape, F32)],
        scratch_shapes=[pltpu.VMEM((tm, D_MODEL), F32), pltpu.VMEM((tm, D_MODEL), _MXU_DTYPE)],
        compiler_params=_params("arbitrary", "arbitrary"))(dxo, xhat, rstd, g, u, w_up, w_dn)


def _mlp_bwd_dw(u, dyb, xinb, du, layer, *, nj, tm=1024):
    T = u.shape[0]
    tm = min(tm, T)
    fc = u.shape[1] // nj

    def body(u_ref, dy_ref, x_ref, du_ref, gd_ref, gu_ref):
        @pl.when(pl.program_id(1) == 0)
        def _():
            gd_ref[...] = jnp.zeros_like(gd_ref)
            gu_ref[...] = jnp.zeros_like(gu_ref)

        r = jnp.maximum(u_ref[...].astype(F32), 0.0)
        gd_ref[...] += _dot(_mx(r * r), dy_ref[...], TN)
        gu_ref[...] += _dot(x_ref[...], du_ref[...], TN)

    return pl.pallas_call(
        body, name=f"mlp_bwd_dw_{layer}", grid=(nj, T // tm),
        in_specs=[pl.BlockSpec((tm, fc), lambda j, i: (i, j)), pl.BlockSpec((tm, D_MODEL), lambda j, i: (i, 0)),
                  pl.BlockSpec((tm, D_MODEL), lambda j, i: (i, 0)), pl.BlockSpec((tm, fc), lambda j, i: (i, j))],
        out_specs=[pl.BlockSpec((None, fc, D_MODEL), lambda j, i: (j, 0, 0)),
                   pl.BlockSpec((None, D_MODEL, fc), lambda j, i: (j, 0, 0))],
        out_shape=[S((nj, fc, D_MODEL), F32), S((nj, D_MODEL, fc), F32)],
        compiler_params=_params("parallel", "arbitrary"))(u, dyb, xinb, du)


def _swa_scores(q, kp, kc, bias_h, sink, n, hd):
    blk = SWA_BLOCK
    kh = hd // (SWA_QH // SWA_KVH)
    qh = q[:, SWA_D * hd:SWA_D * (hd + 1)]
    kph = kp[:, SWA_D * kh:SWA_D * (kh + 1)]
    kch = kc[:, SWA_D * kh:SWA_D * (kh + 1)]
    scale = SWA_D ** -0.5
    row = lax.broadcasted_iota(jnp.int32, (blk, blk), 0)
    col = lax.broadcasted_iota(jnp.int32, (blk, blk), 1)
    sp = _dot(qh, kph, NT) * scale + bias_h[:, :blk]
    sc = _dot(qh, kch, NT) * scale + bias_h[:, blk:]
    sp = jnp.where((col > row) & (n > 0), sp, -jnp.inf)
    sc = jnp.where(col <= row, sc, -jnp.inf)
    m = jnp.maximum(jnp.maximum(jnp.max(sp, axis=1, keepdims=True), jnp.max(sc, axis=1, keepdims=True)), sink)
    pp, pc, ps = jnp.exp(sp - m), jnp.exp(sc - m), jnp.exp(sink - m)
    inv = 1.0 / (jnp.sum(pp, axis=1, keepdims=True) + jnp.sum(pc, axis=1, keepdims=True) + ps)
    return qh, kph, kch, pp * inv, pc * inv, ps * inv


def _swa_attn_fwd(qkv, bias, sinks):
    T = qkv.shape[0]
    blk = SWA_BLOCK
    nb = T // blk
    dq, dkv = SWA_QH * SWA_D, SWA_KVH * SWA_D

    def body(q_ref, kvp_ref, kvc_ref, bias_ref, sink_ref, o_ref):
        n = pl.program_id(0)
        q, kvp, kvc = q_ref[...], kvp_ref[...], kvc_ref[...]
        sinks_v = sink_ref[...]
        for hd in range(SWA_QH):
            kh = hd // (SWA_QH // SWA_KVH)
            _, _, _, pp, pc, _ = _swa_scores(q, kvp[:, :dkv], kvc[:, :dkv], bias_ref[hd], sinks_v[:, hd:hd + 1], n, hd)
            vp = kvp[:, dkv + SWA_D * kh:dkv + SWA_D * (kh + 1)]
            vc = kvc[:, dkv + SWA_D * kh:dkv + SWA_D * (kh + 1)]
            o_ref[:, SWA_D * hd:SWA_D * (hd + 1)] = _mx(_dot(_mx(pp), vp) + _dot(_mx(pc), vc))

    return pl.pallas_call(
        body, name="swa_attn_fwd", grid=(nb,),
        in_specs=[pl.BlockSpec((blk, dq), lambda n: (n, 0)),
                  pl.BlockSpec((blk, 2 * dkv), lambda n: (jnp.maximum(n - 1, 0), dq // (2 * dkv))),
                  pl.BlockSpec((blk, 2 * dkv), lambda n: (n, dq // (2 * dkv))),
                  pl.BlockSpec(bias.shape, lambda n: (0, 0, 0)), pl.BlockSpec(sinks.shape, lambda n: (0, 0))],
        out_specs=pl.BlockSpec((blk, dq), lambda n: (n, 0)), out_shape=S((T, dq), _MXU_DTYPE),
        compiler_params=_params("parallel"))(qkv, qkv, qkv, bias, sinks)


def _swa_attn_bwd(qkv, ob, do, bias, sinks):
    T = qkv.shape[0]
    blk = SWA_BLOCK
    nb = T // blk
    dq, dkv = SWA_QH * SWA_D, SWA_KVH * SWA_D
    grp = SWA_QH // SWA_KVH

    def body(q_ref, kvp_ref, kvc_ref, o_ref, do_ref, bias_ref, sink_ref, dqkv_ref, dbias_ref, dsink_ref, carry):
        st = pl.program_id(0)
        n = nb - 1 - st

        @pl.when(st == 0)
        def _():
            carry[...] = jnp.zeros_like(carry)
            dbias_ref[...] = jnp.zeros_like(dbias_ref)
            dsink_ref[...] = jnp.zeros_like(dsink_ref)

        q, kvp, kvc = q_ref[...], kvp_ref[...], kvc_ref[...]
        ov, dov = o_ref[...], do_ref[...]
        sinks_v = sink_ref[...]
        for kh in range(SWA_KVH):
            vp = kvp[:, dkv + SWA_D * kh:dkv + SWA_D * (kh + 1)]
            vc = kvc[:, dkv + SWA_D * kh:dkv + SWA_D * (kh + 1)]
            dkp = dkc = dvp = dvc = jnp.zeros((blk, SWA_D), F32)
            for gi in range(grp):
                hd = kh * grp + gi
                qh, kph, kch, pp, pc, ps = _swa_scores(q, kvp[:, :dkv], kvc[:, :dkv], bias_ref[hd],
                                                       sinks_v[:, hd:hd + 1], n, hd)
                doh = dov[:, SWA_D * hd:SWA_D * (hd + 1)]
                dl = jnp.sum(doh * ov[:, SWA_D * hd:SWA_D * (hd + 1)].astype(F32), axis=1, keepdims=True)
                dohb = _mx(doh)
                dsp = pp * (_dot(dohb, vp, NT) - dl)
                dsc = pc * (_dot(dohb, vc, NT) - dl)
                dbias_ref[hd, :, 0:blk] += dsp
                dbias_ref[hd, :, blk:] += dsc
                dsk = jnp.sum(-ps * dl, axis=0, keepdims=True)
                dsink_ref[hd:hd + 1, :] += jnp.broadcast_to(dsk, (1, dsink_ref.shape[1]))
                dspb, dscb = _mx(dsp * (SWA_D ** -0.5)), _mx(dsc * (SWA_D ** -0.5))
                dqkv_ref[:, SWA_D * hd:SWA_D * (hd + 1)] = _mx(_dot(dspb, kph) + _dot(dscb, kch))
                dkp = dkp + _dot(dspb, qh, TN)
                dkc = dkc + _dot(dscb, qh, TN)
                dvp = dvp + _dot(_mx(pp), dohb, TN)
                dvc = dvc + _dot(_mx(pc), dohb, TN)
            ko, vo = SWA_D * kh, dkv + SWA_D * kh
            dqkv_ref[:, dq + ko:dq + ko + SWA_D] = _mx(dkc + carry[:, ko:ko + SWA_D])
            dqkv_ref[:, dq + vo:dq + vo + SWA_D] = _mx(dvc + carry[:, vo:vo + SWA_D])
            carry[:, ko:ko + SWA_D] = dkp
            carry[:, vo:vo + SWA_D] = dvp

    rev = lambda s: nb - 1 - s
    return pl.pallas_call(
        body, name="swa_attn_bwd", grid=(nb,),
        in_specs=[pl.BlockSpec((blk, dq), lambda s: (rev(s), 0)),
                  pl.BlockSpec((blk, 2 * dkv), lambda s: (jnp.maximum(rev(s) - 1, 0), dq // (2 * dkv))),
                  pl.BlockSpec((blk, 2 * dkv), lambda s: (rev(s), dq // (2 * dkv))),
                  pl.BlockSpec((blk, dq), lambda s: (rev(s), 0)), pl.BlockSpec((blk, dq), lambda s: (rev(s), 0)),
                  pl.BlockSpec(bias.shape, lambda s: (0, 0, 0)), pl.BlockSpec(sinks.shape, lambda s: (0, 0))],
        out_specs=[pl.BlockSpec((blk, dq + 2 * dkv), lambda s: (rev(s), 0)),
                   pl.BlockSpec(bias.shape, lambda s: (0, 0, 0)), pl.BlockSpec((SWA_QH, 128), lambda s: (0, 0))],
        out_shape=[S((T, dq + 2 * dkv), _MXU_DTYPE), S(bias.shape, F32), S((SWA_QH, 128), F32)],
        scratch_shapes=[pltpu.VMEM((blk, 2 * dkv), F32)],
        compiler_params=_params("arbitrary"))(qkv, qkv, qkv, ob, do, bias, sinks)


def _t5_onehot():
    i = jnp.arange(SWA_BLOCK)
    j = jnp.arange(2 * SWA_BLOCK)
    n = jnp.maximum(i[:, None] + SWA_BLOCK - j[None, :], 0)
    max_exact = REL_BUCKETS // 2
    nf = jnp.maximum(n, 1).astype(F32)
    large = max_exact + (jnp.log(nf / max_exact) / math.log(REL_MAX_DIST / max_exact)
                         * (REL_BUCKETS - max_exact)).astype(jnp.int32)
    large = jnp.minimum(large, REL_BUCKETS - 1)
    bucket = jnp.where(n < max_exact, n, large).reshape(-1)
    return (bucket[None, :] == jnp.arange(REL_BUCKETS)[:, None]).astype(F32)


def _loss_head(y, target, *, tm=1024):
    T, D = y.shape
    tm = min(tm, T)

    def body(y_ref, t_ref, loss_ref, dy_ref):
        @pl.when(pl.program_id(0) == 0)
        def _():
            loss_ref[...] = jnp.zeros_like(loss_ref)

        d = y_ref[...] - t_ref[...]
        dy_ref[...] = d * (1.0 / D)
        rs = jnp.sum(d * d, axis=1, keepdims=True)
        loss_ref[...] += (0.5 / D) * jnp.sum(rs, axis=0, keepdims=True)

    rows = pl.BlockSpec((tm, D), lambda i: (i, 0))
    return pl.pallas_call(
        body, name="loss_head", grid=(T // tm,), in_specs=[rows, rows],
        out_specs=[pl.BlockSpec((1, 1), lambda i: (0, 0)), rows], out_shape=[S((1, 1), F32), S((T, D), F32)],
        compiler_params=_params("arbitrary"))(y, target)


def _exchange(gather, scatter, *, name):
    n_g, n_s = len(gather), len(scatter)
    n_arr = n_g + n_s

    def body(*refs):
        ins, outs = refs[:n_arr], refs[n_arr:2 * n_arr]
        send_sems, recv_sems, loc_sems = refs[2 * n_arr:]
        mx, my, mc = lax.axis_index("x"), lax.axis_index("y"), lax.axis_index("c")
        me = 4 * mx + 2 * my + mc
        local, remote = [], []
        for a in range(n_arr):
            src = ins[a] if a < n_g else ins[a].at[me]
            cp = pltpu.make_async_copy(src, outs[a].at[me], loc_sems.at[a])
            cp.start()
            local.append(cp)
        for k in range(1, N_DEV):
            px, py, pc = mx ^ ((k >> 2) & 1), my ^ ((k >> 1) & 1), mc ^ (k & 1)
            peer = 4 * px + 2 * py + pc
            for a in range(n_arr):
                src = ins[a] if a < n_g else ins[a].at[peer]
                cp = pltpu.make_async_remote_copy(
                    src_ref=src, dst_ref=outs[a].at[me], send_sem=send_sems.at[a, k - 1],
                    recv_sem=recv_sems.at[a, k - 1], device_id=(px, py, pc), device_id_type=pl.DeviceIdType.MESH)
                cp.start()
                remote.append(cp)
        for cp in remote:
            cp.wait()
        for cp in local:
            cp.wait()

    hbm = pl.BlockSpec(memory_space=pl.ANY)
    out_shape = [S((N_DEV,) + g.shape, g.dtype) for g in gather] + [S(s.shape, s.dtype) for s in scatter]
    return pl.pallas_call(
        body, name=name, in_specs=[hbm] * n_arr, out_specs=[hbm] * n_arr, out_shape=out_shape,
        scratch_shapes=[pltpu.SemaphoreType.DMA((n_arr, N_DEV - 1)), pltpu.SemaphoreType.DMA((n_arr, N_DEV - 1)),
                        pltpu.SemaphoreType.DMA((n_arr,))])(*gather, *scatter)


def _adamw(parts, w, m, v, *, name, tr=256):
    R, C = w.shape
    tr = min(tr, R)
    assert R % tr == 0

    def body(p_ref, w_ref, m_ref, v_ref, g_ref, d_ref, nm_ref, nv_ref):
        g = p_ref[0]
        for k in range(1, N_DEV):
            g = g + p_ref[k]
        g_ref[...] = g
        m_new = ADAM_B1 * m_ref[...] + (1.0 - ADAM_B1) * g
        v_new = ADAM_B2 * v_ref[...] + (1.0 - ADAM_B2) * (g * g)
        m_hat = m_new / (1.0 - ADAM_B1 ** ADAM_STEP)
        v_hat = v_new / (1.0 - ADAM_B2 ** ADAM_STEP)
        d_ref[...] = -ADAM_LR * (m_hat / (jnp.sqrt(v_hat) + ADAM_EPS) + ADAM_WD * w_ref[...])
        nm_ref[...] = m_new
        nv_ref[...] = v_new

    rows = pl.BlockSpec((tr, C), lambda i: (i, 0))
    return pl.pallas_call(
        body, name=name, grid=(R // tr,),
        in_specs=[pl.BlockSpec((N_DEV, tr, C), lambda i: (0, i, 0)), rows, rows, rows],
        out_specs=[rows] * 4, out_shape=[S((R, C), F32)] * 4,
        compiler_params=_params("parallel"))(parts, w, m, v)


def _rows_of(n):
    return -(-n // LANES)


def _pack(pieces, total_rows, dtype, lead=()):
    out = []
    for p in pieces:
        flat = p.reshape(lead + (-1,)).astype(dtype)
        n = flat.shape[-1]
        pad = _rows_of(n) * LANES - n
        if pad:
            flat = jnp.pad(flat, [(0, 0)] * len(lead) + [(0, pad)])
        out.append(flat.reshape(lead + (-1, LANES)))
    used = sum(o.shape[-2] for o in out)
    if total_rows > used:
        out.append(jnp.zeros(lead + (total_rows - used, LANES), dtype))
    return jnp.concatenate(out, axis=len(lead))


def _unpack(buf, shapes, lead=()):
    res, r0 = [], 0
    for shp in shapes:
        n = int(np.prod(shp))
        nr = _rows_of(n)
        piece = lax.slice_in_dim(buf, r0, r0 + nr, axis=len(lead)).reshape(lead + (nr * LANES,))
        res.append(lax.slice_in_dim(piece, 0, n, axis=len(lead)).reshape(lead + tuple(shp)))
        r0 += nr
    return res


def _round_up(n, m):
    return -(-n // m) * m


BIG = ["mla_w_in", "mla_w_uq", "mla_w_uk", "mla_w_uv", "mla_w_o", "kv_w_shared", "swa_w_q", "swa_w_o",
       "mlp_w_up", "mlp_w_down"]
GAINS = ["mla_g_q", "mla_g_kv"]
SHARDED = BIG + GAINS
REPL = ["swa_sinks", "rel_bias", "ln_mix_g", "ln_mix_b", "ln_mlp_g", "ln_mlp_b"]
WEIGHTS = ["mla_w_in", "mla_g_q", "mla_g_kv", "mla_w_uq", "mla_w_uk", "mla_w_uv", "mla_w_o", "kv_w_shared",
           "swa_w_q", "swa_sinks", "swa_w_o", "rel_bias", "mlp_w_up", "mlp_w_down", "ln_mix_g", "ln_mix_b",
           "ln_mlp_g", "ln_mlp_b"]


def kernel(x, mla_w_in, mla_g_q, mla_g_kv, mla_w_uq, mla_w_uk, mla_w_uv, mla_w_o, kv_w_shared, swa_w_q, swa_sinks, swa_w_o, rel_bias, mlp_w_up, mlp_w_down, ln_mix_g, ln_mix_b, ln_mlp_g, ln_mlp_b, loss_target, m_mla_w_in, m_mla_g_q, m_mla_g_kv, m_mla_w_uq, m_mla_w_uk, m_mla_w_uv, m_mla_w_o, m_kv_w_shared, m_swa_w_q, m_swa_sinks, m_swa_w_o, m_rel_bias, m_mlp_w_up, m_mlp_w_down, m_ln_mix_g, m_ln_mix_b, m_ln_mlp_g, m_ln_mlp_b, v_mla_w_in, v_mla_g_q, v_mla_g_kv, v_mla_w_uq, v_mla_w_uk, v_mla_w_uv, v_mla_w_o, v_kv_w_shared, v_swa_w_q, v_swa_sinks, v_swa_w_o, v_rel_bias, v_mlp_w_up, v_mlp_w_down, v_ln_mix_g, v_ln_mix_b, v_ln_mlp_g, v_ln_mlp_b):
    args = dict(locals())
    W = {n: args[n] for n in WEIGHTS}
    M = {n: args["m_" + n] for n in WEIGHTS}
    V = {n: args["v_" + n] for n in WEIGHTS}
    T = x.shape[1]
    x2d = x.reshape(T, D_MODEL)
    tgt = loss_target.reshape(T, D_MODEL)
    H = MLA_HEADS

    big_rows = _round_up(sum(_rows_of(W[n].size) for n in BIG), 16)
    wb_local = _pack([W[n] for n in BIG], big_rows, _MXU_DTYPE)
    gains_local = _pack([W[n] for n in GAINS], 8, F32)
    wb_all, gains_all = _exchange([wb_local, gains_local], [], name="gather_weights")
    w_in_s, w_uq_s, w_uk_s, w_uv_s, w_o_s, w_kv_s, w_q_s, w_o2_s, w_up, w_dn = _unpack(
        wb_all, [W[n].shape for n in BIG], lead=(N_DEV,))
    g_q_s, g_kv_s = _unpack(gains_all, [W[n].shape for n in GAINS], lead=(N_DEV,))
    w_in = w_in_s.reshape(D_MODEL, -1)
    g_q = g_q_s.reshape(1, MLA_QR)
    g_kv = g_kv_s.reshape(1, MLA_C)
    w_uq = w_uq_s.reshape(MLA_QR, H, MLA_NOPE + MLA_ROPE)
    w_uq_n = w_uq[:, :, :MLA_NOPE].reshape(MLA_QR, H * MLA_NOPE)
    w_uq_r = w_uq[:, :, MLA_NOPE:].reshape(MLA_QR, H * MLA_ROPE)
    w_uk = w_uk_s.reshape(MLA_C, H, MLA_NOPE).transpose(1, 0, 2)
    w_uk_t = w_uk.transpose(0, 2, 1)
    w_uv = w_uv_s.reshape(MLA_C, H, MLA_V).transpose(1, 0, 2)
    w_o = w_o_s.reshape(H * MLA_V, D_MODEL)
    w_qkv = jnp.concatenate([w_q_s.reshape(D_MODEL, -1), w_kv_s.reshape(D_MODEL, -1)], axis=1)
    w_o2 = w_o2_s.reshape(SWA_QH * SWA_D, D_MODEL)
    ln = lambda a, l: a[l].reshape(1, D_MODEL)

    half = MLA_ROPE // 2
    inv = ROPE_THETA ** (-jnp.arange(half, dtype=F32) / half)
    ang = jnp.arange(T, dtype=F32)[:, None] * inv[None, :]
    cos = jnp.tile(jnp.concatenate([jnp.cos(ang), jnp.cos(ang)], -1), (1, H))
    sin = jnp.tile(jnp.concatenate([-jnp.sin(ang), jnp.sin(ang)], -1), (1, H))

    h, kc, qs, cq, qn = _mla_pre_fwd(x2d, w_in, g_q, g_kv, w_uq_n, w_uq_r, w_uk_t, cos, sin)
    olat, lse = _mla_attn_fwd(qs, kc)
    o_mla = _mla_uv_fwd(olat, w_uv)
    x1, x1b, xh1, rs1 = _proj_ln_fwd(o_mla, w_o, x2d, ln(ln_mix_g, 0), ln(ln_mix_b, 0), name="mla_out_ln_fwd")
    u0, x2, x2b, xh2, rs2 = _mlp_fwd(x1b, x1, w_up, w_dn, 0, ln(ln_mlp_g, 0), ln(ln_mlp_b, 0))
    onehot = _t5_onehot()
    bias = _mm(rel_bias.T, onehot, name="rel_bias_expand", precision=lax.Precision.HIGHEST, tn=8192).reshape(
        SWA_QH, SWA_BLOCK, 2 * SWA_BLOCK)
    qkv = _mm(x2b, w_qkv, name="swa_qkv_fwd", out_dtype=_MXU_DTYPE, tm=1024, tn=512, tk=1024)
    o_swa = _swa_attn_fwd(qkv, bias, swa_sinks)
    x3, x3b, xh3, rs3 = _proj_ln_fwd(o_swa, w_o2, x2, ln(ln_mix_g, 1), ln(ln_mix_b, 1), name="swa_out_ln_fwd")
    u1, x4, _, xh4, rs4 = _mlp_fwd(x3b, x3, w_up, w_dn, 1, ln(ln_mlp_g, 1), ln(ln_mlp_b, 1))
    loss_part, dx4 = _loss_head(x4, tgt)
    loss = lax.psum(loss_part[0, 0], AXES)

    nj = w_up.shape[0]
    dx3, du1, dy4b, dg_mlp1, db_mlp1 = _mlp_bwd_dx(dx4, xh4, rs4, ln(ln_mlp_g, 1), u1, w_up, w_dn, 1)
    g_dn1, g_up1 = _mlp_bwd_dw(u1, dy4b, x3b, du1, 1, nj=nj)
    dres3, do_swa, g_o2, dg_mix1, db_mix1 = _proj_ln_bwd(dx3, xh3, rs3, ln(ln_mix_g, 1), o_swa, w_o2,
                                                         name="swa_out_ln_bwd")
    dqkv, dbias, dsink = _swa_attn_bwd(qkv, o_swa, do_swa, bias, swa_sinks)
    g_rel = _mm(onehot, dbias.reshape(SWA_QH, -1), name="rel_bias_grad", tb=True, precision=lax.Precision.HIGHEST,
                tk=8192)
    g_sinks = dsink[:, 0].reshape(1, SWA_QH)
    dx2 = _mm(dqkv, w_qkv, name="swa_qkv_bwd_dx", tb=True, add=dres3, tm=1024, tn=1024, tk=512)
    g_qkv = _mm(x2b, dqkv, name="swa_qkv_bwd_dw", ta=True, tm=1024, tn=512, tk=1024)
    dx1, du0, dy2b, dg_mlp0, db_mlp0 = _mlp_bwd_dx(dx2, xh2, rs2, ln(ln_mlp_g, 0), u0, w_up, w_dn, 0)
    g_dn0, g_up0 = _mlp_bwd_dw(u0, dy2b, x1b, du0, 0, nj=nj)
    dres1, do_mla, g_o, dg_mix0, db_mix0 = _proj_ln_bwd(dx1, xh1, rs1, ln(ln_mix_g, 0), o_mla, w_o,
                                                        name="mla_out_ln_bwd")
    dol, delta, g_uv = _mla_uv_bwd(do_mla, olat, w_uv)
    dqs, dkc, dv = _mla_attn_bwd(qs, kc, dol, lse, delta)
    grad_x, g_in, g_uq_n, g_uq_r, g_uk, g_gq, g_gkv = _mla_pre_bwd(
        dqs, dkc, dv, h, x2d, dres1, cq, qn, cos, sin, w_in, g_q, g_kv, w_uq_n, w_uq_r, w_uk)

    dq_cols = SWA_QH * SWA_D
    full_grads = {
        "mla_w_in": g_in,
        "mla_w_uq": jnp.concatenate([g_uq_n.reshape(MLA_QR, H, MLA_NOPE), g_uq_r.reshape(MLA_QR, H, MLA_ROPE)], -1),
        "mla_w_uk": g_uk.transpose(1, 0, 2),
        "mla_w_uv": g_uv.transpose(1, 0, 2),
        "mla_w_o": g_o,
        "kv_w_shared": g_qkv[:, dq_cols:],
        "swa_w_q": g_qkv[:, :dq_cols],
        "swa_w_o": g_o2,
        "mlp_w_up": jnp.stack([g_up0, g_up1], axis=1),
        "mlp_w_down": jnp.stack([g_dn0, g_dn1], axis=1),
        "mla_g_q": g_gq,
        "mla_g_kv": g_gkv,
    }
    shard_rows = _round_up(sum(_rows_of(W[n].size) for n in SHARDED), 256)
    g_parts = _pack([full_grads[n] for n in SHARDED], shard_rows, F32, lead=(N_DEV,))
    repl_grads = {
        "swa_sinks": g_sinks, "rel_bias": g_rel,
        "ln_mix_g": jnp.concatenate([dg_mix0, dg_mix1], 0), "ln_mix_b": jnp.concatenate([db_mix0, db_mix1], 0),
        "ln_mlp_g": jnp.concatenate([dg_mlp0, dg_mlp1], 0), "ln_mlp_b": jnp.concatenate([db_mlp0, db_mlp1], 0),
    }
    repl_rows = _round_up(sum(_rows_of(W[n].size) for n in REPL), 8)
    r_part = _pack([repl_grads[n] for n in REPL], repl_rows, F32)

    r_all, g_recv = _exchange([r_part], [g_parts], name="exchange_grads")
    pk = lambda d, names, rows: _pack([d[n] for n in names], rows, F32)
    sh = _adamw(g_recv, pk(W, SHARDED, shard_rows), pk(M, SHARDED, shard_rows), pk(V, SHARDED, shard_rows),
                name="adamw_sharded")
    rp = _adamw(r_all, pk(W, REPL, repl_rows), pk(M, REPL, repl_rows), pk(V, REPL, repl_rows), name="adamw_replicated")
    outs = []
    for k in range(4):
        d = dict(zip(SHARDED, _unpack(sh[k], [W[n].shape for n in SHARDED])))
        d.update(zip(REPL, _unpack(rp[k], [W[n].shape for n in REPL])))
        outs.append(d)
    return (loss, grad_x.reshape(x.shape), *[outs[0][n] for n in WEIGHTS], *[outs[1][n] for n in WEIGHTS],
            *[outs[2][n] for n in WEIGHTS], *[outs[3][n] for n in WEIGHTS])
```

```python
import functools
import math

import numpy as np
import jax
import jax.numpy as jnp
from jax import lax
from jax.experimental import pallas as pl
from jax.experimental.pallas import tpu as pltpu

F32 = jnp.float32
_MXU_DTYPE = jnp.bfloat16

D_MODEL = 1024
DEPTH = 2
MLA_HEADS = 8
MLA_NOPE = 128
MLA_ROPE = 64
MLA_V = 128
MLA_QR = 384
MLA_C = 256
MLA_DK = 384
ROPE_THETA = 10000.0
SWA_QH = 16
SWA_KVH = 4
SWA_D = 64
SWA_BLOCK = 128
REL_BUCKETS = 32
REL_MAX_DIST = 128
D_FF = 4096
LN_EPS = 1e-5
RMS_EPS = 1e-6
ALPHA = (2 * DEPTH) ** 0.25
ADAM_LR, ADAM_B1, ADAM_B2, ADAM_EPS, ADAM_WD, ADAM_STEP = 0.001, 0.9, 0.999, 1e-08, 0.01, 10

N_DEV = 8
AXES = ("x", "y", "c")
V7X_VMEM_BYTES = 64 * 1024 * 1024
VMEM_LIMIT = V7X_VMEM_BYTES - 8 * 1024 * 1024
LANES = 1024
ATT_TQ = 256
ATT_TK = 512
ATT_HEAD_GROUP = 2

NT = (((1,), (1,)), ((), ()))
TN = (((0,), (0,)), ((), ()))
S = jax.ShapeDtypeStruct


def _params(*sem, vmem=VMEM_LIMIT):
    return pltpu.CompilerParams(dimension_semantics=sem, vmem_limit_bytes=vmem)


def _dot(a, b, dims=None, precision=None):
    if dims is None:
        return jnp.dot(a, b, preferred_element_type=F32, precision=precision)
    return lax.dot_general(a, b, dims, preferred_element_type=F32, precision=precision)


def _mx(v):
    return v.astype(_MXU_DTYPE)


def _swap_halves_64(v):
    return jnp.concatenate([v[:, 32:], v[:, :32]], axis=-1)


def _swap_halves_groups(v):
    n = v.shape[-1]
    lane = lax.broadcasted_iota(jnp.int32, v.shape, 1)
    return jnp.where(lane % 64 < 32, pltpu.roll(v, n - 32, 1), pltpu.roll(v, 32, 1))


def _mm(a, b, *, name, ta=False, tb=False, add=None, out_dtype=F32, tm=512, tn=512, tk=512, precision=None):
    M, K = (a.shape[1], a.shape[0]) if ta else a.shape
    N = b.shape[0] if tb else b.shape[1]
    tm, tn, tk = min(tm, M), min(tn, N), min(tk, K)
    assert M % tm == 0 and N % tn == 0 and K % tk == 0, (M, N, K, tm, tn, tk)
    nk = K // tk
    dims = (((0 if ta else 1,), (1 if tb else 0,)), ((), ()))
    has_add = add is not None

    def body(*refs):
        if has_add:
            a_ref, b_ref, add_ref, o_ref, acc = refs
        else:
            a_ref, b_ref, o_ref, acc = refs
        k = pl.program_id(2)

        @pl.when(k == 0)
        def _():
            acc[...] = jnp.zeros_like(acc)

        av, bv = a_ref[...], b_ref[...]
        if precision is None:
            av, bv = _mx(av), _mx(bv)
        acc[...] += _dot(av, bv, dims, precision)

        @pl.when(k == nk - 1)
        def _():
            r = acc[...]
            if has_add:
                r = r + add_ref[...]
            o_ref[...] = r.astype(out_dtype)

    a_spec = pl.BlockSpec((tk, tm), lambda i, j, k: (k, i)) if ta else pl.BlockSpec((tm, tk), lambda i, j, k: (i, k))
    b_spec = pl.BlockSpec((tn, tk), lambda i, j, k: (j, k)) if tb else pl.BlockSpec((tk, tn), lambda i, j, k: (k, j))
    in_specs = [a_spec, b_spec]
    args = [a, b]
    if has_add:
        in_specs.append(pl.BlockSpec((tm, tn), lambda i, j, k: (i, j)))
        args.append(add)
    return pl.pallas_call(
        body, name=name, grid=(M // tm, N // tn, nk), in_specs=in_specs,
        out_specs=pl.BlockSpec((tm, tn), lambda i, j, k: (i, j)), out_shape=S((M, N), out_dtype),
        scratch_shapes=[pltpu.VMEM((tm, tn), F32)],
        compiler_params=_params("parallel", "parallel", "arbitrary"))(*args)


def _ln_fwd_math(z, g, b):
    mu = jnp.mean(z, axis=-1, keepdims=True)
    zc = z - mu
    var = jnp.mean(zc * zc, axis=-1, keepdims=True)
    rstd = lax.rsqrt(var + LN_EPS)
    xhat = zc * rstd
    return xhat * g + b, xhat, rstd


def _ln_bwd_math(dxo, xhat, rstd, g):
    dxh = dxo * g
    m1 = jnp.mean(dxh, axis=-1, keepdims=True)
    m2 = jnp.mean(dxh * xhat, axis=-1, keepdims=True)
    dz = rstd * (dxh - m1 - xhat * m2)
    dg = jnp.sum(dxo * xhat, axis=0, keepdims=True)
    db = jnp.sum(dxo, axis=0, keepdims=True)
    return dz, dg, db


def _rms_fwd_math(xr, g):
    r = lax.rsqrt(jnp.mean(xr * xr, axis=-1, keepdims=True) + RMS_EPS)
    return xr * r * g


def _rms_bwd_math(dy, xr, g):
    r = lax.rsqrt(jnp.mean(xr * xr, axis=-1, keepdims=True) + RMS_EPS)
    gy = dy * g
    dx = r * gy - xr * (r * r * r) * jnp.mean(gy * xr, axis=-1, keepdims=True)
    dg = jnp.sum(dy * xr * r, axis=0, keepdims=True)
    return dx, dg


def _mla_pre_fwd(x, w_in, g_q, g_kv, w_uq_n, w_uq_r, w_uk_t, cos, sin):
    T = x.shape[0]
    tm = min(ATT_TQ, T)
    nq = T // tm
    H = MLA_HEADS

    def body(x_ref, win_ref, gq_ref, gkv_ref, wn_ref, wr_ref, wuk_ref, cos_ref, sin_ref,
             h_ref, kc_ref, qs_ref, cq_ref, qn_ref):
        h = _dot(_mx(x_ref[...]), win_ref[...])
        h_ref[...] = h
        cos_v, sin_v = cos_ref[...], sin_ref[...]
        cq = _mx(_rms_fwd_math(h[:, :MLA_QR], gq_ref[...]))
        ckv = _rms_fwd_math(h[:, MLA_QR:MLA_QR + MLA_C], gkv_ref[...])
        krr = h[:, MLA_QR + MLA_C:]
        kr = krr * cos_v[:, :MLA_ROPE] + _swap_halves_64(krr) * sin_v[:, :MLA_ROPE]
        kc_ref[:, 0:MLA_C] = _mx(ckv)
        kc_ref[:, MLA_C:MLA_C + MLA_ROPE] = _mx(kr)
        kc_ref[:, MLA_C + MLA_ROPE:] = jnp.zeros((tm, MLA_DK - MLA_C - MLA_ROPE), _MXU_DTYPE)
        cq_ref[...] = cq
        qnb = _mx(_dot(cq, wn_ref[...]))
        qn_ref[...] = qnb
        qr = _dot(cq, wr_ref[...])
        qrr = qr * cos_v + _swap_halves_groups(qr) * sin_v
        for hd in range(H):
            ql = _dot(qnb[:, MLA_NOPE * hd:MLA_NOPE * (hd + 1)], wuk_ref[hd])
            qs_ref[0, hd, :, 0:MLA_C] = _mx(ql)
            qs_ref[0, hd, :, MLA_C:MLA_C + MLA_ROPE] = _mx(qrr[:, MLA_ROPE * hd:MLA_ROPE * (hd + 1)])
            qs_ref[0, hd, :, MLA_C + MLA_ROPE:] = jnp.zeros((tm, MLA_DK - MLA_C - MLA_ROPE), _MXU_DTYPE)

    full = lambda shp: pl.BlockSpec(shp, lambda i: (0,) * len(shp))
    rows = lambda n: pl.BlockSpec((tm, n), lambda i: (i, 0))
    n_in = w_in.shape[1]
    return pl.pallas_call(
        body, name="mla_pre_fwd", grid=(nq,),
        in_specs=[rows(D_MODEL), full(w_in.shape), full(g_q.shape), full(g_kv.shape), full(w_uq_n.shape),
                  full(w_uq_r.shape), full(w_uk_t.shape), rows(H * MLA_ROPE), rows(H * MLA_ROPE)],
        out_specs=[rows(n_in), rows(MLA_DK), pl.BlockSpec((1, H, tm, MLA_DK), lambda i: (i, 0, 0, 0)),
                   rows(MLA_QR), rows(H * MLA_NOPE)],
        out_shape=[S((T, n_in), F32), S((T, MLA_DK), _MXU_DTYPE), S((nq, H, tm, MLA_DK), _MXU_DTYPE),
                   S((T, MLA_QR), _MXU_DTYPE), S((T, H * MLA_NOPE), _MXU_DTYPE)],
        compiler_params=_params("parallel"))(x, w_in, g_q, g_kv, w_uq_n, w_uq_r, w_uk_t, cos, sin)


def _att_steps(T, tq, tk):
    qi, kj = [], []
    for i in range(T // tq):
        for j in range((i * tq + tq - 1) // tk + 1):
            qi.append(i)
            kj.append(j)
    return jnp.asarray(np.array(qi, np.int32)), jnp.asarray(np.array(kj, np.int32))


def _ride_exchange(st, n_steps, ins, outs, n_gather, sems):
    if not ins:
        return

    @pl.when(st == 0)
    def _():
        for cp in _exchange_copies(ins, outs, n_gather, *sems):
            cp.start()

    @pl.when(st == n_steps - 1)
    def _():
        for cp in _exchange_copies(ins, outs, n_gather, *sems):
            cp.wait()


def _mla_attn_fwd(qs, kc, gather=(), scatter=()):
    nq, H, tq, DK = qs.shape
    T = kc.shape[0]
    tk = min(ATT_TK, T)
    scale = (MLA_NOPE + MLA_ROPE) ** -0.5
    c2 = scale * math.log2(math.e)
    qi, kj = _att_steps(T, tq, tk)
    n_steps = int(qi.shape[0])
    hg = ATT_HEAD_GROUP
    R = hg * tq
    n_x = len(gather) + len(scatter)

    def body(qi_ref, kj_ref, q_ref, k_ref, *rest):
        x_ins, (o_ref, lse_ref), x_outs = rest[:n_x], rest[n_x:n_x + 2], rest[n_x + 2:2 * n_x + 2]
        m_sc, l_sc, acc_sc = rest[2 * n_x + 2:2 * n_x + 5]
        st = pl.program_id(0)
        _ride_exchange(st, n_steps, x_ins, x_outs, len(gather), rest[2 * n_x + 5:])
        i, j = qi_ref[st], kj_ref[st]
        j_last = (i * tq + tq - 1) // tk

        @pl.when(j == 0)
        def _():
            m_sc[...] = jnp.full_like(m_sc, -jnp.inf)
            l_sc[...] = jnp.zeros_like(l_sc)
            acc_sc[...] = jnp.zeros_like(acc_sc)

        def step(masked):
            k = k_ref[pl.ds(pl.multiple_of(j * tk, tk), tk), :]
            v = k[:, :MLA_C]
            if masked:
                row = lax.broadcasted_iota(jnp.int32, (R, tk), 0) % tq + i * tq
                col = lax.broadcasted_iota(jnp.int32, (R, tk), 1) + j * tk
                causal = col <= row
            n_g = H // hg
            qk = lambda g: _dot(q_ref[0, g * hg:(g + 1) * hg].reshape(R, DK), k, NT)
            s_next = qk(0)
            for g in range(n_g):
                rs = slice(g * R, (g + 1) * R)
                s = s_next
                if g + 1 < n_g:
                    s_next = qk(g + 1)
                if masked:
                    s = jnp.where(causal, s, -jnp.inf)
                m_prev = m_sc[rs]
                m_new = jnp.maximum(m_prev, jnp.max(s, axis=1, keepdims=True))
                a = jnp.exp2((m_prev - m_new) * c2)
                p = jnp.exp2((s - m_new) * c2)
                l_sc[rs] = a * l_sc[rs] + jnp.sum(p, axis=1, keepdims=True)
                acc_sc[rs] = a * acc_sc[rs] + _dot(_mx(p), v)
                m_sc[rs] = m_new

        pl.when(j == j_last)(lambda: step(True))
        pl.when(j != j_last)(lambda: step(False))

        @pl.when(j == j_last)
        def _():
            o_ref[0] = _mx(acc_sc[...] / l_sc[...]).reshape(H, tq, MLA_C)
            lse_ref[0] = (m_sc[...] * scale + jnp.log(l_sc[...])).reshape(H, tq, 1)

    hbm = pl.BlockSpec(memory_space=pl.ANY)
    x_shapes, x_sems = _exchange_shapes(gather, scatter) if n_x else ([], [])
    gs = pltpu.PrefetchScalarGridSpec(
        num_scalar_prefetch=2, grid=(n_steps,),
        in_specs=[pl.BlockSpec((1, H, tq, DK), lambda s, qi, kj: (qi[s], 0, 0, 0)),
                  pl.BlockSpec(memory_space=pltpu.VMEM)] + [hbm] * n_x,
        out_specs=[pl.BlockSpec((1, H, tq, MLA_C), lambda s, qi, kj: (qi[s], 0, 0, 0)),
                   pl.BlockSpec((1, H, tq, 1), lambda s, qi, kj: (qi[s], 0, 0, 0))] + [hbm] * n_x,
        scratch_shapes=[pltpu.VMEM((H * tq, 1), F32), pltpu.VMEM((H * tq, 1), F32),
                        pltpu.VMEM((H * tq, MLA_C), F32)] + x_sems)
    res = pl.pallas_call(
        body, name="mla_attn_fwd", grid_spec=gs,
        out_shape=[S((nq, H, tq, MLA_C), _MXU_DTYPE), S((nq, H, tq, 1), F32)] + x_shapes,
        compiler_params=_params("arbitrary"))(qi, kj, qs, kc, *gather, *scatter)
    return res[0], res[1], res[2:]


def _mla_attn_bwd(qs, kc, dol, lse, delta, gather=(), scatter=()):
    nq, H, tq, DK = qs.shape
    T = kc.shape[0]
    tk = min(ATT_TK, T)
    scale = (MLA_NOPE + MLA_ROPE) ** -0.5
    log2e = math.log2(math.e)
    qi, kj = _att_steps(T, tq, tk)
    n_steps = int(qi.shape[0])
    hg = ATT_HEAD_GROUP
    R = hg * tq
    n_x = len(gather) + len(scatter)

    def body(qi_ref, kj_ref, q_ref, k_ref, do_ref, lse_ref, dl_ref, *rest):
        x_ins, (dq_ref, dk_ref, dv_ref), x_outs = rest[:n_x], rest[n_x:n_x + 3], rest[n_x + 3:2 * n_x + 3]
        dk_acc, dv_acc, sem = rest[2 * n_x + 3:2 * n_x + 6]
        st = pl.program_id(0)
        _ride_exchange(st, n_steps, x_ins, x_outs, len(gather), rest[2 * n_x + 6:])
        i, j = qi_ref[st], kj_ref[st]
        j_last = (i * tq + tq - 1) // tk

        @pl.when(st == 0)
        def _():
            dk_acc[...] = jnp.zeros_like(dk_acc)
            dv_acc[...] = jnp.zeros_like(dv_acc)

        @pl.when(j == 0)
        def _():
            dq_ref[...] = jnp.zeros_like(dq_ref)

        def step(masked):
            koff = pl.multiple_of(j * tk, tk)
            k = k_ref[pl.ds(koff, tk), :]
            v = k[:, :MLA_C]
            if masked:
                row = lax.broadcasted_iota(jnp.int32, (R, tk), 0) % tq + i * tq
                col = lax.broadcasted_iota(jnp.int32, (R, tk), 1) + j * tk
                causal = col <= row
            dk_c = jnp.zeros((tk, DK), F32)
            dv_c = jnp.zeros((tk, MLA_C), F32)
            n_g = H // hg

            def scores(g):
                hs = slice(g * hg, (g + 1) * hg)
                q = q_ref[0, hs].reshape(R, DK)
                do = do_ref[0, hs].reshape(R, MLA_C)
                return q, do, _dot(q, k, NT), _dot(do, v, NT)

            nxt = scores(0)
            for g in range(n_g):
                hs = slice(g * hg, (g + 1) * hg)
                q, do, s, dp = nxt
                if g + 1 < n_g:
                    nxt = scores(g + 1)
                p = jnp.exp2(s * (scale * log2e) - lse_ref[0, hs].reshape(R, 1) * log2e)
                if masked:
                    p = jnp.where(causal, p, 0.0)
                dsb = _mx(p * (dp - dl_ref[0, hs].reshape(R, 1)))
                pb = _mx(p)
                dq_ref[0, hs] += _dot(dsb, k).reshape(hg, tq, DK)
                dk_c = dk_c + _dot(dsb, q, TN)
                dv_c = dv_c + _dot(pb, do, TN)
            dk_acc[pl.ds(koff, tk), :] += dk_c * scale
            dv_acc[pl.ds(koff, tk), :] += dv_c

        pl.when(j == j_last)(lambda: step(True))
        pl.when(j != j_last)(lambda: step(False))

        @pl.when(j == j_last)
        def _():
            dq_ref[...] = dq_ref[...] * scale

        @pl.when(st == n_steps - 1)
        def _():
            c1 = pltpu.make_async_copy(dk_acc, dk_ref, sem.at[0])
            c2 = pltpu.make_async_copy(dv_acc, dv_ref, sem.at[1])
            c1.start()
            c2.start()
            c1.wait()
            c2.wait()

    blk = lambda n: pl.BlockSpec((1, H, tq, n), lambda s, qi, kj: (qi[s], 0, 0, 0))
    hbm = pl.BlockSpec(memory_space=pl.ANY)
    x_shapes, x_sems = _exchange_shapes(gather, scatter) if n_x else ([], [])
    gs = pltpu.PrefetchScalarGridSpec(
        num_scalar_prefetch=2, grid=(n_steps,),
        in_specs=[blk(DK), pl.BlockSpec(memory_space=pltpu.VMEM), blk(MLA_C), blk(1), blk(1)] + [hbm] * n_x,
        out_specs=[blk(DK), hbm, hbm] + [hbm] * n_x,
        scratch_shapes=[pltpu.VMEM((T, DK), F32), pltpu.VMEM((T, MLA_C), F32), pltpu.SemaphoreType.DMA((2,))] + x_sems)
    res = pl.pallas_call(
        body, name="mla_attn_bwd", grid_spec=gs,
        out_shape=[S((nq, H, tq, DK), F32), S((T, DK), F32), S((T, MLA_C), F32)] + x_shapes,
        compiler_params=_params("arbitrary"))(qi, kj, qs, kc, dol, lse, delta, *gather, *scatter)
    return res[0], res[1], res[2], res[3:]


def _mla_uv_fwd(olat, w_uv):
    nq, H, tq, C = olat.shape
    T = nq * tq

    def body(ol_ref, wuv_ref, o_ref):
        for hd in range(H):
            o_ref[:, MLA_V * hd:MLA_V * (hd + 1)] = _mx(_dot(ol_ref[0, hd], wuv_ref[hd]))

    return pl.pallas_call(
        body, name="mla_uv_fwd", grid=(nq,),
        in_specs=[pl.BlockSpec((1, H, tq, C), lambda i: (i, 0, 0, 0)), pl.BlockSpec(w_uv.shape, lambda i: (0, 0, 0))],
        out_specs=pl.BlockSpec((tq, H * MLA_V), lambda i: (i, 0)), out_shape=S((T, H * MLA_V), _MXU_DTYPE),
        compiler_params=_params("parallel"))(olat, w_uv)


def _mla_uv_bwd(do, olat, w_uv):
    nq, H, tq, C = olat.shape

    def body(do_ref, ol_ref, wuv_ref, dol_ref, dl_ref, dw_ref):
        @pl.when(pl.program_id(0) == 0)
        def _():
            dw_ref[...] = jnp.zeros_like(dw_ref)

        dov = do_ref[...]
        for hd in range(H):
            doh = _mx(dov[:, MLA_V * hd:MLA_V * (hd + 1)])
            ol = ol_ref[0, hd]
            dol = _dot(doh, wuv_ref[hd], NT)
            dol_ref[0, hd] = _mx(dol)
            dl_ref[0, hd] = jnp.sum(dol * ol.astype(F32), axis=1, keepdims=True)
            dw_ref[hd] += _dot(ol, doh, TN)

    blk = lambda n: pl.BlockSpec((1, H, tq, n), lambda i: (i, 0, 0, 0))
    return pl.pallas_call(
        body, name="mla_uv_bwd", grid=(nq,),
        in_specs=[pl.BlockSpec((tq, H * MLA_V), lambda i: (i, 0)), blk(C), pl.BlockSpec(w_uv.shape, lambda i: (0, 0, 0))],
        out_specs=[blk(C), blk(1), pl.BlockSpec(w_uv.shape, lambda i: (0, 0, 0))],
        out_shape=[S(olat.shape, _MXU_DTYPE), S((nq, H, tq, 1), F32), S(w_uv.shape, F32)],
        compiler_params=_params("arbitrary"))(do, olat, w_uv)


def _mla_pre_bwd(dqs, dkc, dv, h, x, dres, cq, qn, cos, sin, w_in, g_q, g_kv, w_uq_n, w_uq_r, w_uk):
    nq, H, tm, DK = dqs.shape
    T = nq * tm
    n_in = w_in.shape[1]

    def body(dqs_ref, dkc_ref, dv_ref, h_ref, x_ref, dres_ref, cq_ref, qn_ref, cos_ref, sin_ref,
             win_ref, gq_ref, gkv_ref, wn_ref, wr_ref, wuk_ref,
             gx_ref, dwin_ref, dwn_ref, dwr_ref, dwuk_ref, dgq_ref, dgkv_ref, dqn_sc, dqr_sc, dh_sc):
        @pl.when(pl.program_id(0) == 0)
        def _():
            for r in (dwin_ref, dwn_ref, dwr_ref, dwuk_ref, dgq_ref, dgkv_ref):
                r[...] = jnp.zeros_like(r)

        cos_v, sin_v = cos_ref[...], sin_ref[...]
        qnb = qn_ref[...]
        for hd in range(H):
            dqh = dqs_ref[0, hd]
            dql = _mx(dqh[:, :MLA_C])
            dqn_sc[:, MLA_NOPE * hd:MLA_NOPE * (hd + 1)] = _dot(dql, wuk_ref[hd])
            dwuk_ref[hd] += _dot(dql, qnb[:, MLA_NOPE * hd:MLA_NOPE * (hd + 1)], TN)
            dqr_sc[:, MLA_ROPE * hd:MLA_ROPE * (hd + 1)] = dqh[:, MLA_C:MLA_C + MLA_ROPE]
        dqr = dqr_sc[...]
        dqrb = _mx(dqr * cos_v + _swap_halves_groups(dqr * sin_v))
        dqnb = _mx(dqn_sc[...])
        cq = cq_ref[...]
        dwn_ref[...] += _dot(cq, dqnb, TN)
        dwr_ref[...] += _dot(cq, dqrb, TN)
        dcq = _dot(dqnb, wn_ref[...], NT) + _dot(dqrb, wr_ref[...], NT)
        hv = h_ref[...]
        dxq, dgq = _rms_bwd_math(dcq, hv[:, :MLA_QR], gq_ref[...])
        dgq_ref[...] += dgq
        dkcv = dkc_ref[...]
        dckv = dkcv[:, :MLA_C] + dv_ref[...]
        dxkv, dgkv = _rms_bwd_math(dckv, hv[:, MLA_QR:MLA_QR + MLA_C], gkv_ref[...])
        dgkv_ref[...] += dgkv
        dkr = dkcv[:, MLA_C:MLA_C + MLA_ROPE]
        dkr_raw = dkr * cos_v[:, :MLA_ROPE] + _swap_halves_64(dkr * sin_v[:, :MLA_ROPE])
        dh_sc[:, 0:MLA_QR] = dxq
        dh_sc[:, MLA_QR:MLA_QR + MLA_C] = dxkv
        dh_sc[:, MLA_QR + MLA_C:] = dkr_raw
        dhb = _mx(dh_sc[...])
        gx_ref[...] = dres_ref[...] + _dot(dhb, win_ref[...], NT)
        dwin_ref[...] += _dot(_mx(x_ref[...]), dhb, TN)

    full = lambda shp: pl.BlockSpec(shp, lambda i: (0,) * len(shp))
    rows = lambda n: pl.BlockSpec((tm, n), lambda i: (i, 0))
    return pl.pallas_call(
        body, name="mla_pre_bwd", grid=(nq,),
        in_specs=[pl.BlockSpec((1, H, tm, DK), lambda i: (i, 0, 0, 0)), rows(DK), rows(MLA_C), rows(n_in),
                  rows(D_MODEL), rows(D_MODEL), rows(MLA_QR), rows(H * MLA_NOPE), rows(H * MLA_ROPE), rows(H * MLA_ROPE),
                  full(w_in.shape), full(g_q.shape), full(g_kv.shape), full(w_uq_n.shape), full(w_uq_r.shape),
                  full(w_uk.shape)],
        out_specs=[rows(D_MODEL), full(w_in.shape), full(w_uq_n.shape), full(w_uq_r.shape), full(w_uk.shape),
                   full(g_q.shape), full(g_kv.shape)],
        out_shape=[S((T, D_MODEL), F32), S(w_in.shape, F32), S(w_uq_n.shape, F32), S(w_uq_r.shape, F32),
                   S(w_uk.shape, F32), S(g_q.shape, F32), S(g_kv.shape, F32)],
        scratch_shapes=[pltpu.VMEM((tm, H * MLA_NOPE), F32), pltpu.VMEM((tm, H * MLA_ROPE), F32),
                        pltpu.VMEM((tm, n_in), F32)],
        compiler_params=_params("arbitrary"))(dqs, dkc, dv, h, x, dres, cq, qn, cos, sin, w_in, g_q, g_kv,
                                              w_uq_n, w_uq_r, w_uk)


def _proj_ln_fwd(a, w, xres, g, b, *, name, tm=512):
    T, K = a.shape
    tm = min(tm, T)

    def body(a_ref, w_ref, x_ref, g_ref, b_ref, xo_ref, xob_ref, xh_ref, rs_ref):
        z = ALPHA * x_ref[...] + _dot(a_ref[...], w_ref[...])
        xo, xhat, rstd = _ln_fwd_math(z, g_ref[...], b_ref[...])
        xo_ref[...] = xo
        xob_ref[...] = _mx(xo)
        xh_ref[...] = xhat
        rs_ref[...] = rstd

    rows = lambda n: pl.BlockSpec((tm, n), lambda i: (i, 0))
    full = lambda shp: pl.BlockSpec(shp, lambda i: (0,) * len(shp))
    return pl.pallas_call(
        body, name=name, grid=(T // tm,),
        in_specs=[rows(K), full(w.shape), rows(D_MODEL), full(g.shape), full(b.shape)],
        out_specs=[rows(D_MODEL), rows(D_MODEL), rows(D_MODEL), rows(1)],
        out_shape=[S((T, D_MODEL), F32), S((T, D_MODEL), _MXU_DTYPE), S((T, D_MODEL), F32), S((T, 1), F32)],
        compiler_params=_params("parallel"))(a, w, xres, g, b)


def _proj_ln_bwd(dxo, xhat, rstd, g, a, w, *, name, tm=512):
    T, K = a.shape
    tm = min(tm, T)

    def body(dxo_ref, xh_ref, rs_ref, g_ref, a_ref, w_ref, dres_ref, da_ref, dw_ref, dg_ref, db_ref):
        @pl.when(pl.program_id(0) == 0)
        def _():
            for r in (dw_ref, dg_ref, db_ref):
                r[...] = jnp.zeros_like(r)

        dz, dg, db = _ln_bwd_math(dxo_ref[...], xh_ref[...], rs_ref[...], g_ref[...])
        dg_ref[...] += dg
        db_ref[...] += db
        dres_ref[...] = ALPHA * dz
        dzb = _mx(dz)
        da_ref[...] = _dot(dzb, w_ref[...], NT)
        dw_ref[...] += _dot(a_ref[...], dzb, TN)

    rows = lambda n: pl.BlockSpec((tm, n), lambda i: (i, 0))
    full = lambda shp: pl.BlockSpec(shp, lambda i: (0,) * len(shp))
    return pl.pallas_call(
        body, name=name, grid=(T // tm,),
        in_specs=[rows(D_MODEL), rows(D_MODEL), rows(1), full(g.shape), rows(K), full(w.shape)],
        out_specs=[rows(D_MODEL), rows(K), full(w.shape), full(g.shape), full(g.shape)],
        out_shape=[S((T, D_MODEL), F32), S((T, K), F32), S(w.shape, F32), S(g.shape, F32), S(g.shape, F32)],
        compiler_params=_params("arbitrary"))(dxo, xhat, rstd, g, a, w)


def _mlp_fwd(xb, xres, w_up, w_dn, layer, g, b, *, tm=1024):
    T = xb.shape[0]
    tm = min(tm, T)
    nj, _, _, fc = w_up.shape

    def body(xb_ref, x_ref, wu_ref, wd_ref, g_ref, b_ref, u_ref, xo_ref, xob_ref, xh_ref, rs_ref, acc):
        j = pl.program_id(1)

        @pl.when(j == 0)
        def _():
            acc[...] = ALPHA * x_ref[...]

        u = _dot(xb_ref[...], wu_ref[...])
        u_ref[...] = _mx(u)
        r = jnp.maximum(u, 0.0)
        acc[...] += _dot(_mx(r * r), wd_ref[...])

        @pl.when(j == nj - 1)
        def _():
            xo, xhat, rstd = _ln_fwd_math(acc[...], g_ref[...], b_ref[...])
            xo_ref[...] = xo
            xob_ref[...] = _mx(xo)
            xh_ref[...] = xhat
            rs_ref[...] = rstd

    rows = lambda n: pl.BlockSpec((tm, n), lambda i, j: (i, 0))
    full = lambda shp: pl.BlockSpec(shp, lambda i, j: (0,) * len(shp))
    return pl.pallas_call(
        body, name=f"mlp_fwd_{layer}", grid=(T // tm, nj),
        in_specs=[rows(D_MODEL), rows(D_MODEL),
                  pl.BlockSpec((None, None, D_MODEL, fc), lambda i, j: (j, layer, 0, 0)),
                  pl.BlockSpec((None, None, fc, D_MODEL), lambda i, j: (j, layer, 0, 0)),
                  full(g.shape), full(b.shape)],
        out_specs=[pl.BlockSpec((tm, fc), lambda i, j: (i, j)), rows(D_MODEL), rows(D_MODEL), rows(D_MODEL), rows(1)],
        out_shape=[S((T, nj * fc), _MXU_DTYPE), S((T, D_MODEL), F32), S((T, D_MODEL), _MXU_DTYPE),
                   S((T, D_MODEL), F32), S((T, 1), F32)],
        scratch_shapes=[pltpu.VMEM((tm, D_MODEL), F32)],
        compiler_params=_params("parallel", "arbitrary"))(xb, xres, w_up, w_dn, g, b)


def _mlp_bwd_dx(dxo, xhat, rstd, g, u, w_up, w_dn, layer, *, tm=1024):
    T = dxo.shape[0]
    tm = min(tm, T)
    nj, _, _, fc = w_up.shape

    def body(dxo_ref, xh_ref, rs_ref, g_ref, u_ref, wu_ref, wd_ref, dx_ref, du_ref, dyb_ref, dg_ref, db_ref, acc, dy_sc):
        i, j = pl.program_id(0), pl.program_id(1)

        @pl.when((i == 0) & (j == 0))
        def _():
            dg_ref[...] = jnp.zeros_like(dg_ref)
            db_ref[...] = jnp.zeros_like(db_ref)

        @pl.when(j == 0)
        def _():
            dz, dg, db = _ln_bwd_math(dxo_ref[...], xh_ref[...], rs_ref[...], g_ref[...])
            dg_ref[...] += dg
            db_ref[...] += db
            acc[...] = ALPHA * dz
            dy_sc[...] = _mx(dz)
            dyb_ref[...] = _mx(dz)

        r = jnp.maximum(u_ref[...].astype(F32), 0.0)
        da = _dot(dy_sc[...], wd_ref[...], NT)
        dub = _mx(da * (2.0 * r))
        du_ref[...] = dub
        acc[...] += _dot(dub, wu_ref[...], NT)

        @pl.when(j == nj - 1)
        def _():
            dx_ref[...] = acc[...]

    rows = lambda n: pl.BlockSpec((tm, n), lambda i, j: (i, 0))
    full = lambda shp: pl.BlockSpec(shp, lambda i, j: (0,) * len(shp))
    return pl.pallas_call(
        body, name=f"mlp_bwd_dx_{layer}", grid=(T // tm, nj),
        in_specs=[rows(D_MODEL), rows(D_MODEL), rows(1), full(g.shape), pl.BlockSpec((tm, fc), lambda i, j: (i, j)),
                  pl.BlockSpec((None, None, D_MODEL, fc), lambda i, j: (j, layer, 0, 0)),
                  pl.BlockSpec((None, None, fc, D_MODEL), lambda i, j: (j, layer, 0, 0))],
        out_specs=[rows(D_MODEL), pl.BlockSpec((tm, fc), lambda i, j: (i, j)), rows(D_MODEL), full(g.shape), full(g.shape)],
        out_shape=[S((T, D_MODEL), F32), S((T, nj * fc), _MXU_DTYPE), S((T, D_MODEL), _MXU_DTYPE),
                   S(g.shape, F32), S(g.shape, F32)],
        scratch_shapes=[pltpu.VMEM((tm, D_MODEL), F32), pltpu.VMEM((tm, D_MODEL), _MXU_DTYPE)],
        compiler_params=_params("arbitrary", "arbitrary"))(dxo, xhat, rstd, g, u, w_up, w_dn)


def _mlp_bwd_dw(u, dyb, xinb, du, layer, *, nj, tm=1024):
    T = u.shape[0]
    tm = min(tm, T)
    fc = u.shape[1] // nj

    def body(u_ref, dy_ref, x_ref, du_ref, gd_ref, gu_ref):
        @pl.when(pl.program_id(1) == 0)
        def _():
            gd_ref[...] = jnp.zeros_like(gd_ref)
            gu_ref[...] = jnp.zeros_like(gu_ref)

        r = jnp.maximum(u_ref[...].astype(F32), 0.0)
        gd_ref[...] += _dot(_mx(r * r), dy_ref[...], TN)
        gu_ref[...] += _dot(x_ref[...], du_ref[...], TN)

    return pl.pallas_call(
        body, name=f"mlp_bwd_dw_{layer}", grid=(nj, T // tm),
        in_specs=[pl.BlockSpec((tm, fc), lambda j, i: (i, j)), pl.BlockSpec((tm, D_MODEL), lambda j, i: (i, 0)),
                  pl.BlockSpec((tm, D_MODEL), lambda j, i: (i, 0)), pl.BlockSpec((tm, fc), lambda j, i: (i, j))],
        out_specs=[pl.BlockSpec((None, fc, D_MODEL), lambda j, i: (j, 0, 0)),
                   pl.BlockSpec((None, D_MODEL, fc), lambda j, i: (j, 0, 0))],
        out_shape=[S((nj, fc, D_MODEL), F32), S((nj, D_MODEL, fc), F32)],
        compiler_params=_params("parallel", "arbitrary"))(u, dyb, xinb, du)


def _swa_scores(q, kp, kc, bias_h, sink, n, hd):
    blk = SWA_BLOCK
    kh = hd // (SWA_QH // SWA_KVH)
    qh = q[:, SWA_D * hd:SWA_D * (hd + 1)]
    kph = kp[:, SWA_D * kh:SWA_D * (kh + 1)]
    kch = kc[:, SWA_D * kh:SWA_D * (kh + 1)]
    scale = SWA_D ** -0.5
    row = lax.broadcasted_iota(jnp.int32, (blk, blk), 0)
    col = lax.broadcasted_iota(jnp.int32, (blk, blk), 1)
    sp = _dot(qh, kph, NT) * scale + bias_h[:, :blk]
    sc = _dot(qh, kch, NT) * scale + bias_h[:, blk:]
    sp = jnp.where((col > row) & (n > 0), sp, -jnp.inf)
    sc = jnp.where(col <= row, sc, -jnp.inf)
    m = jnp.maximum(jnp.maximum(jnp.max(sp, axis=1, keepdims=True), jnp.max(sc, axis=1, keepdims=True)), sink)
    pp, pc, ps = jnp.exp(sp - m), jnp.exp(sc - m), jnp.exp(sink - m)
    inv = 1.0 / (jnp.sum(pp, axis=1, keepdims=True) + jnp.sum(pc, axis=1, keepdims=True) + ps)
    return qh, kph, kch, pp * inv, pc * inv, ps * inv


def _swa_attn_fwd(qkv, bias, sinks):
    T = qkv.shape[0]
    blk = SWA_BLOCK
    nb = T // blk
    dq, dkv = SWA_QH * SWA_D, SWA_KVH * SWA_D

    def body(q_ref, kvp_ref, kvc_ref, bias_ref, sink_ref, o_ref):
        n = pl.program_id(0)
        q, kvp, kvc = q_ref[...], kvp_ref[...], kvc_ref[...]
        sinks_v = sink_ref[...]
        for hd in range(SWA_QH):
            kh = hd // (SWA_QH // SWA_KVH)
            _, _, _, pp, pc, _ = _swa_scores(q, kvp[:, :dkv], kvc[:, :dkv], bias_ref[hd], sinks_v[:, hd:hd + 1], n, hd)
            vp = kvp[:, dkv + SWA_D * kh:dkv + SWA_D * (kh + 1)]
            vc = kvc[:, dkv + SWA_D * kh:dkv + SWA_D * (kh + 1)]
            o_ref[:, SWA_D * hd:SWA_D * (hd + 1)] = _mx(_dot(_mx(pp), vp) + _dot(_mx(pc), vc))

    return pl.pallas_call(
        body, name="swa_attn_fwd", grid=(nb,),
        in_specs=[pl.BlockSpec((blk, dq), lambda n: (n, 0)),
                  pl.BlockSpec((blk, 2 * dkv), lambda n: (jnp.maximum(n - 1, 0), dq // (2 * dkv))),
                  pl.BlockSpec((blk, 2 * dkv), lambda n: (n, dq // (2 * dkv))),
                  pl.BlockSpec(bias.shape, lambda n: (0, 0, 0)), pl.BlockSpec(sinks.shape, lambda n: (0, 0))],
        out_specs=pl.BlockSpec((blk, dq), lambda n: (n, 0)), out_shape=S((T, dq), _MXU_DTYPE),
        compiler_params=_params("parallel"))(qkv, qkv, qkv, bias, sinks)


def _swa_attn_bwd(qkv, ob, do, bias, sinks):
    T = qkv.shape[0]
    blk = SWA_BLOCK
    nb = T // blk
    dq, dkv = SWA_QH * SWA_D, SWA_KVH * SWA_D
    grp = SWA_QH // SWA_KVH

    def body(q_ref, kvp_ref, kvc_ref, o_ref, do_ref, bias_ref, sink_ref, dqkv_ref, dbias_ref, dsink_ref, carry):
        st = pl.program_id(0)
        n = nb - 1 - st

        @pl.when(st == 0)
        def _():
            carry[...] = jnp.zeros_like(carry)
            dbias_ref[...] = jnp.zeros_like(dbias_ref)
            dsink_ref[...] = jnp.zeros_like(dsink_ref)

        q, kvp, kvc = q_ref[...], kvp_ref[...], kvc_ref[...]
        ov, dov = o_ref[...], do_ref[...]
        sinks_v = sink_ref[...]
        for kh in range(SWA_KVH):
            vp = kvp[:, dkv + SWA_D * kh:dkv + SWA_D * (kh + 1)]
            vc = kvc[:, dkv + SWA_D * kh:dkv + SWA_D * (kh + 1)]
            dkp = dkc = dvp = dvc = jnp.zeros((blk, SWA_D), F32)
            for gi in range(grp):
                hd = kh * grp + gi
                qh, kph, kch, pp, pc, ps = _swa_scores(q, kvp[:, :dkv], kvc[:, :dkv], bias_ref[hd],
                                                       sinks_v[:, hd:hd + 1], n, hd)
                doh = dov[:, SWA_D * hd:SWA_D * (hd + 1)]
                dl = jnp.sum(doh * ov[:, SWA_D * hd:SWA_D * (hd + 1)].astype(F32), axis=1, keepdims=True)
                dohb = _mx(doh)
                dsp = pp * (_dot(dohb, vp, NT) - dl)
                dsc = pc * (_dot(dohb, vc, NT) - dl)
                dbias_ref[hd, :, 0:blk] += dsp
                dbias_ref[hd, :, blk:] += dsc
                dsk = jnp.sum(-ps * dl, axis=0, keepdims=True)
                dsink_ref[hd:hd + 1, :] += jnp.broadcast_to(dsk, (1, dsink_ref.shape[1]))
                dspb, dscb = _mx(dsp * (SWA_D ** -0.5)), _mx(dsc * (SWA_D ** -0.5))
                dqkv_ref[:, SWA_D * hd:SWA_D * (hd + 1)] = _mx(_dot(dspb, kph) + _dot(dscb, kch))
                dkp = dkp + _dot(dspb, qh, TN)
                dkc = dkc + _dot(dscb, qh, TN)
                dvp = dvp + _dot(_mx(pp), dohb, TN)
                dvc = dvc + _dot(_mx(pc), dohb, TN)
            ko, vo = SWA_D * kh, dkv + SWA_D * kh
            dqkv_ref[:, dq + ko:dq + ko + SWA_D] = _mx(dkc + carry[:, ko:ko + SWA_D])
            dqkv_ref[:, dq + vo:dq + vo + SWA_D] = _mx(dvc + carry[:, vo:vo + SWA_D])
            carry[:, ko:ko + SWA_D] = dkp
            carry[:, vo:vo + SWA_D] = dvp

    rev = lambda s: nb - 1 - s
    return pl.pallas_call(
        body, name="swa_attn_bwd", grid=(nb,),
        in_specs=[pl.BlockSpec((blk, dq), lambda s: (rev(s), 0)),
                  pl.BlockSpec((blk, 2 * dkv), lambda s: (jnp.maximum(rev(s) - 1, 0), dq // (2 * dkv))),
                  pl.BlockSpec((blk, 2 * dkv), lambda s: (rev(s), dq // (2 * dkv))),
                  pl.BlockSpec((blk, dq), lambda s: (rev(s), 0)), pl.BlockSpec((blk, dq), lambda s: (rev(s), 0)),
                  pl.BlockSpec(bias.shape, lambda s: (0, 0, 0)), pl.BlockSpec(sinks.shape, lambda s: (0, 0))],
        out_specs=[pl.BlockSpec((blk, dq + 2 * dkv), lambda s: (rev(s), 0)),
                   pl.BlockSpec(bias.shape, lambda s: (0, 0, 0)), pl.BlockSpec((SWA_QH, 128), lambda s: (0, 0))],
        out_shape=[S((T, dq + 2 * dkv), _MXU_DTYPE), S(bias.shape, F32), S((SWA_QH, 128), F32)],
        scratch_shapes=[pltpu.VMEM((blk, 2 * dkv), F32)],
        compiler_params=_params("arbitrary"))(qkv, qkv, qkv, ob, do, bias, sinks)


def _t5_onehot():
    i = jnp.arange(SWA_BLOCK)
    j = jnp.arange(2 * SWA_BLOCK)
    n = jnp.maximum(i[:, None] + SWA_BLOCK - j[None, :], 0)
    max_exact = REL_BUCKETS // 2
    nf = jnp.maximum(n, 1).astype(F32)
    large = max_exact + (jnp.log(nf / max_exact) / math.log(REL_MAX_DIST / max_exact)
                         * (REL_BUCKETS - max_exact)).astype(jnp.int32)
    large = jnp.minimum(large, REL_BUCKETS - 1)
    bucket = jnp.where(n < max_exact, n, large).reshape(-1)
    return (bucket[None, :] == jnp.arange(REL_BUCKETS)[:, None]).astype(F32)


def _loss_head(y, target, *, tm=1024):
    T, D = y.shape
    tm = min(tm, T)

    def body(y_ref, t_ref, loss_ref, dy_ref):
        @pl.when(pl.program_id(0) == 0)
        def _():
            loss_ref[...] = jnp.zeros_like(loss_ref)

        d = y_ref[...] - t_ref[...]
        dy_ref[...] = d * (1.0 / D)
        rs = jnp.sum(d * d, axis=1, keepdims=True)
        loss_ref[...] += (0.5 / D) * jnp.sum(rs, axis=0, keepdims=True)

    rows = pl.BlockSpec((tm, D), lambda i: (i, 0))
    return pl.pallas_call(
        body, name="loss_head", grid=(T // tm,), in_specs=[rows, rows],
        out_specs=[pl.BlockSpec((1, 1), lambda i: (0, 0)), rows], out_shape=[S((1, 1), F32), S((T, D), F32)],
        compiler_params=_params("arbitrary"))(y, target)


def _exchange_copies(ins, outs, n_gather, send_sems, recv_sems, loc_sems):
    mx, my, mc = lax.axis_index("x"), lax.axis_index("y"), lax.axis_index("c")
    me = 4 * mx + 2 * my + mc
    copies = []
    for a in range(len(ins)):
        src = ins[a] if a < n_gather else ins[a].at[me]
        copies.append(pltpu.make_async_copy(src, outs[a].at[me], loc_sems.at[a]))
    for k in range(1, N_DEV):
        px, py, pc = mx ^ ((k >> 2) & 1), my ^ ((k >> 1) & 1), mc ^ (k & 1)
        peer = 4 * px + 2 * py + pc
        for a in range(len(ins)):
            src = ins[a] if a < n_gather else ins[a].at[peer]
            copies.append(pltpu.make_async_remote_copy(
                src_ref=src, dst_ref=outs[a].at[me], send_sem=send_sems.at[a, k - 1],
                recv_sem=recv_sems.at[a, k - 1], device_id=(px, py, pc), device_id_type=pl.DeviceIdType.MESH))
    return copies


def _exchange_shapes(gather, scatter):
    n_arr = len(gather) + len(scatter)
    out_shape = [S((N_DEV,) + tuple(g.shape), g.dtype) for g in gather] + [S(s.shape, s.dtype) for s in scatter]
    sems = [pltpu.SemaphoreType.DMA((n_arr, N_DEV - 1)), pltpu.SemaphoreType.DMA((n_arr, N_DEV - 1)),
            pltpu.SemaphoreType.DMA((n_arr,))]
    return out_shape, sems


def _exchange(gather, scatter, *, name):
    n_g = len(gather)
    n_arr = n_g + len(scatter)

    def body(*refs):
        copies = _exchange_copies(refs[:n_arr], refs[n_arr:2 * n_arr], n_g, *refs[2 * n_arr:])
        for cp in copies:
            cp.start()
        for cp in copies:
            cp.wait()

    hbm = pl.BlockSpec(memory_space=pl.ANY)
    out_shape, sems = _exchange_shapes(gather, scatter)
    return pl.pallas_call(
        body, name=name, in_specs=[hbm] * n_arr, out_specs=[hbm] * n_arr, out_shape=out_shape,
        scratch_shapes=sems)(*gather, *scatter)


def _adamw(parts, w, m, v, *, name, tr=256):
    R, C = w.shape
    tr = min(tr, R)
    assert R % tr == 0

    def body(p_ref, w_ref, m_ref, v_ref, g_ref, d_ref, nm_ref, nv_ref):
        g = p_ref[0]
        for k in range(1, N_DEV):
            g = g + p_ref[k]
        g_ref[...] = g
        m_new = ADAM_B1 * m_ref[...] + (1.0 - ADAM_B1) * g
        v_new = ADAM_B2 * v_ref[...] + (1.0 - ADAM_B2) * (g * g)
        m_hat = m_new / (1.0 - ADAM_B1 ** ADAM_STEP)
        v_hat = v_new / (1.0 - ADAM_B2 ** ADAM_STEP)
        d_ref[...] = -ADAM_LR * (m_hat / (jnp.sqrt(v_hat) + ADAM_EPS) + ADAM_WD * w_ref[...])
        nm_ref[...] = m_new
        nv_ref[...] = v_new

    rows = pl.BlockSpec((tr, C), lambda i: (i, 0))
    return pl.pallas_call(
        body, name=name, grid=(R // tr,),
        in_specs=[pl.BlockSpec((N_DEV, tr, C), lambda i: (0, i, 0)), rows, rows, rows],
        out_specs=[rows] * 4, out_shape=[S((R, C), F32)] * 4,
        compiler_params=_params("parallel"))(parts, w, m, v)


def _rows_of(n):
    return -(-n // LANES)


def _pack(pieces, total_rows, dtype, lead=()):
    out = []
    for p in pieces:
        flat = p.reshape(lead + (-1,)).astype(dtype)
        n = flat.shape[-1]
        pad = _rows_of(n) * LANES - n
        if pad:
            flat = jnp.pad(flat, [(0, 0)] * len(lead) + [(0, pad)])
        out.append(flat.reshape(lead + (-1, LANES)))
    used = sum(o.shape[-2] for o in out)
    if total_rows > used:
        out.append(jnp.zeros(lead + (total_rows - used, LANES), dtype))
    return jnp.concatenate(out, axis=len(lead))


def _unpack(buf, shapes, lead=()):
    res, r0 = [], 0
    for shp in shapes:
        n = int(np.prod(shp))
        nr = _rows_of(n)
        piece = lax.slice_in_dim(buf, r0, r0 + nr, axis=len(lead)).reshape(lead + (nr * LANES,))
        res.append(lax.slice_in_dim(piece, 0, n, axis=len(lead)).reshape(lead + tuple(shp)))
        r0 += nr
    return res


def _round_up(n, m):
    return -(-n // m) * m


BIG = ["mla_w_in", "mla_w_uq", "mla_w_uk", "mla_w_uv", "mla_w_o", "kv_w_shared", "swa_w_q", "swa_w_o",
       "mlp_w_up", "mlp_w_down"]
GAINS = ["mla_g_q", "mla_g_kv"]
SHARDED = BIG + GAINS
REPL = ["swa_sinks", "rel_bias", "ln_mix_g", "ln_mix_b", "ln_mlp_g", "ln_mlp_b"]
WEIGHTS = ["mla_w_in", "mla_g_q", "mla_g_kv", "mla_w_uq", "mla_w_uk", "mla_w_uv", "mla_w_o", "kv_w_shared",
           "swa_w_q", "swa_sinks", "swa_w_o", "rel_bias", "mlp_w_up", "mlp_w_down", "ln_mix_g", "ln_mix_b",
           "ln_mlp_g", "ln_mlp_b"]


def kernel(x, mla_w_in, mla_g_q, mla_g_kv, mla_w_uq, mla_w_uk, mla_w_uv, mla_w_o, kv_w_shared, swa_w_q, swa_sinks, swa_w_o, rel_bias, mlp_w_up, mlp_w_down, ln_mix_g, ln_mix_b, ln_mlp_g, ln_mlp_b, loss_target, m_mla_w_in, m_mla_g_q, m_mla_g_kv, m_mla_w_uq, m_mla_w_uk, m_mla_w_uv, m_mla_w_o, m_kv_w_shared, m_swa_w_q, m_swa_sinks, m_swa_w_o, m_rel_bias, m_mlp_w_up, m_mlp_w_down, m_ln_mix_g, m_ln_mix_b, m_ln_mlp_g, m_ln_mlp_b, v_mla_w_in, v_mla_g_q, v_mla_g_kv, v_mla_w_uq, v_mla_w_uk, v_mla_w_uv, v_mla_w_o, v_kv_w_shared, v_swa_w_q, v_swa_sinks, v_swa_w_o, v_rel_bias, v_mlp_w_up, v_mlp_w_down, v_ln_mix_g, v_ln_mix_b, v_ln_mlp_g, v_ln_mlp_b):
    args = dict(locals())
    W = {n: args[n] for n in WEIGHTS}
    M = {n: args["m_" + n] for n in WEIGHTS}
    V = {n: args["v_" + n] for n in WEIGHTS}
    T = x.shape[1]
    x2d = x.reshape(T, D_MODEL)
    tgt = loss_target.reshape(T, D_MODEL)
    H = MLA_HEADS

    SH = {"mla_w_in": (-1, mla_w_in.shape[-1]), "mla_w_uq": (-1, H * (MLA_NOPE + MLA_ROPE)),
          "mla_w_uk": (-1, H * MLA_NOPE), "mla_w_uv": (-1, H * MLA_V), "mla_w_o": (-1, D_MODEL),
          "kv_w_shared": (-1, kv_w_shared.shape[-1]), "swa_w_q": (-1, swa_w_q.shape[-1]), "swa_w_o": (-1, D_MODEL)}
    slab = lambda d, n: d[n].reshape(SH[n])
    bf = lambda a: a.astype(_MXU_DTYPE)
    gains_slab = lambda d: jnp.pad(jnp.concatenate([d["mla_g_q"], d["mla_g_kv"]], axis=1),
                                   ((0, 7), (0, 128 - d["mla_g_q"].shape[1] - d["mla_g_kv"].shape[1])))
    n_gq, n_gkv = mla_g_q.shape[1], mla_g_kv.shape[1]
    w_in_s, w_uq_s, w_uk_s, gains_all = _exchange(
        [bf(slab(W, "mla_w_in")), bf(slab(W, "mla_w_uq")), bf(slab(W, "mla_w_uk")), gains_slab(W)], [],
        name="gather_mla_in")
    later = [bf(slab(W, n)) for n in ("mla_w_uv", "mla_w_o", "kv_w_shared", "swa_w_q", "swa_w_o")]
    later += [bf(mlp_w_up), bf(mlp_w_down)]
    w_in = w_in_s.reshape(D_MODEL, -1)
    g_q = gains_all[:, 0, :n_gq].reshape(1, MLA_QR)
    g_kv = gains_all[:, 0, n_gq:n_gq + n_gkv].reshape(1, MLA_C)
    w_uq = w_uq_s.reshape(MLA_QR, H, MLA_NOPE + MLA_ROPE)
    w_uq_n = w_uq[:, :, :MLA_NOPE].reshape(MLA_QR, H * MLA_NOPE)
    w_uq_r = w_uq[:, :, MLA_NOPE:].reshape(MLA_QR, H * MLA_ROPE)
    w_uk = w_uk_s.reshape(MLA_C, H, MLA_NOPE).transpose(1, 0, 2)
    w_uk_t = w_uk.transpose(0, 2, 1)
    ln = lambda a, l: a[l].reshape(1, D_MODEL)

    half = MLA_ROPE // 2
    inv = ROPE_THETA ** (-jnp.arange(half, dtype=F32) / half)
    ang = jnp.arange(T, dtype=F32)[:, None] * inv[None, :]
    cos = jnp.tile(jnp.concatenate([jnp.cos(ang), jnp.cos(ang)], -1), (1, H))
    sin = jnp.tile(jnp.concatenate([-jnp.sin(ang), jnp.sin(ang)], -1), (1, H))

    h, kc, qs, cq, qn = _mla_pre_fwd(x2d, w_in, g_q, g_kv, w_uq_n, w_uq_r, w_uk_t, cos, sin)
    olat, lse, (w_uv_s, w_o_s, w_kv_s, w_q_s, w_o2_s, w_up, w_dn) = _mla_attn_fwd(qs, kc, gather=later)
    w_uv = w_uv_s.reshape(MLA_C, H, MLA_V).transpose(1, 0, 2)
    w_o = w_o_s.reshape(H * MLA_V, D_MODEL)
    w_qkv = jnp.concatenate([w_q_s.reshape(D_MODEL, -1), w_kv_s.reshape(D_MODEL, -1)], axis=1)
    w_o2 = w_o2_s.reshape(SWA_QH * SWA_D, D_MODEL)
    o_mla = _mla_uv_fwd(olat, w_uv)
    x1, x1b, xh1, rs1 = _proj_ln_fwd(o_mla, w_o, x2d, ln(ln_mix_g, 0), ln(ln_mix_b, 0), name="mla_out_ln_fwd")
    u0, x2, x2b, xh2, rs2 = _mlp_fwd(x1b, x1, w_up, w_dn, 0, ln(ln_mlp_g, 0), ln(ln_mlp_b, 0))
    onehot = _t5_onehot()
    bias = _mm(rel_bias.T, onehot, name="rel_bias_expand", precision=lax.Precision.HIGHEST, tn=8192).reshape(
        SWA_QH, SWA_BLOCK, 2 * SWA_BLOCK)
    qkv = _mm(x2b, w_qkv, name="swa_qkv_fwd", out_dtype=_MXU_DTYPE, tm=1024, tn=512, tk=1024)
    o_swa = _swa_attn_fwd(qkv, bias, swa_sinks)
    x3, x3b, xh3, rs3 = _proj_ln_fwd(o_swa, w_o2, x2, ln(ln_mix_g, 1), ln(ln_mix_b, 1), name="swa_out_ln_fwd")
    u1, x4, _, xh4, rs4 = _mlp_fwd(x3b, x3, w_up, w_dn, 1, ln(ln_mlp_g, 1), ln(ln_mlp_b, 1))
    loss_part, dx4 = _loss_head(x4, tgt)
    loss = lax.psum(loss_part[0, 0], AXES)

    nj = w_up.shape[0]
    dx3, du1, dy4b, dg_mlp1, db_mlp1 = _mlp_bwd_dx(dx4, xh4, rs4, ln(ln_mlp_g, 1), u1, w_up, w_dn, 1)
    g_dn1, g_up1 = _mlp_bwd_dw(u1, dy4b, x3b, du1, 1, nj=nj)
    dres3, do_swa, g_o2, dg_mix1, db_mix1 = _proj_ln_bwd(dx3, xh3, rs3, ln(ln_mix_g, 1), o_swa, w_o2,
                                                         name="swa_out_ln_bwd")
    dqkv, dbias, dsink = _swa_attn_bwd(qkv, o_swa, do_swa, bias, swa_sinks)
    g_rel = _mm(onehot, dbias.reshape(SWA_QH, -1), name="rel_bias_grad", tb=True, precision=lax.Precision.HIGHEST,
                tk=8192)
    g_sinks = dsink[:, 0].reshape(1, SWA_QH)
    dx2 = _mm(dqkv, w_qkv, name="swa_qkv_bwd_dx", tb=True, add=dres3, tm=1024, tn=1024, tk=512)
    g_qkv = _mm(x2b, dqkv, name="swa_qkv_bwd_dw", ta=True, tm=1024, tn=512, tk=1024)
    dx1, du0, dy2b, dg_mlp0, db_mlp0 = _mlp_bwd_dx(dx2, xh2, rs2, ln(ln_mlp_g, 0), u0, w_up, w_dn, 0)
    g_dn0, g_up0 = _mlp_bwd_dw(u0, dy2b, x1b, du0, 0, nj=nj)
    dres1, do_mla, g_o, dg_mix0, db_mix0 = _proj_ln_bwd(dx1, xh1, rs1, ln(ln_mix_g, 0), o_mla, w_o,
                                                        name="mla_out_ln_bwd")
    dol, delta, g_uv = _mla_uv_bwd(do_mla, olat, w_uv)
    repl_grads = {
        "swa_sinks": g_sinks, "rel_bias": g_rel,
        "ln_mix_g": jnp.concatenate([dg_mix0, dg_mix1], 0), "ln_mix_b": jnp.concatenate([db_mix0, db_mix1], 0),
        "ln_mlp_g": jnp.concatenate([dg_mlp0, dg_mlp1], 0), "ln_mlp_b": jnp.concatenate([db_mlp0, db_mlp1], 0),
    }
    repl_rows = _round_up(sum(_rows_of(W[n].size) for n in REPL), 8)
    r_part = _pack([repl_grads[n] for n in REPL], repl_rows, F32)
    by_dev = lambda g: g.reshape((N_DEV, g.shape[0] // N_DEV) + g.shape[1:])
    early = [by_dev(g_o2), by_dev(g_qkv), g_up0, g_up1, g_dn0, g_dn1, by_dev(g_o),
             by_dev(g_uv.transpose(1, 0, 2).reshape(MLA_C, H * MLA_V))]
    dqs, dkc, dv, (r_all, p_o2, p_qkv, p_up0, p_up1, p_dn0, p_dn1, p_o, p_uv) = _mla_attn_bwd(
        qs, kc, dol, lse, delta, gather=[r_part], scatter=early)
    grad_x, g_in, g_uq_n, g_uq_r, g_uk, g_gq, g_gkv = _mla_pre_bwd(
        dqs, dkc, dv, h, x2d, dres1, cq, qn, cos, sin, w_in, g_q, g_kv, w_uq_n, w_uq_r, w_uk)
    g_uq = jnp.concatenate([g_uq_n.reshape(MLA_QR, H, MLA_NOPE), g_uq_r.reshape(MLA_QR, H, MLA_ROPE)], -1)
    g_gains = jnp.pad(jnp.concatenate([g_gq.reshape(N_DEV, n_gq), g_gkv.reshape(N_DEV, n_gkv)], axis=1)[:, None, :],
                      ((0, 0), (0, 7), (0, 128 - n_gq - n_gkv)))
    p_in, p_uq, p_uk, p_gains = _exchange(
        [], [by_dev(g_in), by_dev(g_uq.reshape(MLA_QR, -1)), by_dev(g_uk.transpose(1, 0, 2).reshape(MLA_C, -1)),
             g_gains], name="exchange_mla_in_grads")

    res = {}

    def adam(name, parts, names, to_slab, from_slab):
        out = _adamw(parts, to_slab(W), to_slab(M), to_slab(V), name="adamw_" + name)
        for k in range(4):
            for n, a in zip(names, from_slab(out[k])):
                res[(k, n)] = a.reshape(W[n].shape)

    one = lambda n: (lambda d: slab(d, n))
    adam("swa_w_o", p_o2, ["swa_w_o"], one("swa_w_o"), lambda s: [s])
    dq_cols = SWA_QH * SWA_D
    adam("swa_qkv", p_qkv, ["swa_w_q", "kv_w_shared"],
         lambda d: jnp.concatenate([slab(d, "swa_w_q"), slab(d, "kv_w_shared")], axis=1),
         lambda s: [s[:, :dq_cols], s[:, dq_cols:]])
    for name, parts in (("mlp_w_up", (p_up0, p_up1)), ("mlp_w_down", (p_dn0, p_dn1))):
        per_layer = [_adamw(parts[l], W[name][l], M[name][l], V[name][l], name=f"adamw_{name}_{l}") for l in range(DEPTH)]
        for k in range(4):
            res[(k, name)] = jnp.stack([per_layer[l][k] for l in range(DEPTH)], axis=0)
    adam("mla_w_o", p_o, ["mla_w_o"], one("mla_w_o"), lambda s: [s])
    adam("mla_w_uv", p_uv, ["mla_w_uv"], one("mla_w_uv"), lambda s: [s])
    adam("mla_w_in", p_in, ["mla_w_in"], one("mla_w_in"), lambda s: [s])
    adam("mla_w_uq", p_uq, ["mla_w_uq"], one("mla_w_uq"), lambda s: [s])
    adam("mla_w_uk", p_uk, ["mla_w_uk"], one("mla_w_uk"), lambda s: [s])
    adam("mla_gains", p_gains, ["mla_g_q", "mla_g_kv"], gains_slab,
         lambda s: [s[0:1, :n_gq], s[0:1, n_gq:n_gq + n_gkv]])
    adam("replicated", r_all, REPL, lambda d: _pack([d[n] for n in REPL], repl_rows, F32),
         lambda s: _unpack(s, [W[n].shape for n in REPL]))
    return (loss, grad_x.reshape(x.shape), *[res[(k, n)] for k in range(4) for n in WEIGHTS])
```

```python
import functools
import math

import numpy as np
import jax
import jax.numpy as jnp
from jax import lax
from jax.experimental import pallas as pl
from jax.experimental.pallas import tpu as pltpu

F32 = jnp.float32
_MXU_DTYPE = jnp.bfloat16

D_MODEL = 1024
DEPTH = 2
MLA_HEADS = 8
MLA_NOPE = 128
MLA_ROPE = 64
MLA_V = 128
MLA_QR = 384
MLA_C = 256
MLA_DK = 384
ROPE_THETA = 10000.0
SWA_QH = 16
SWA_KVH = 4
SWA_D = 64
SWA_BLOCK = 128
REL_BUCKETS = 32
REL_MAX_DIST = 128
D_FF = 4096
LN_EPS = 1e-5
RMS_EPS = 1e-6
ALPHA = (2 * DEPTH) ** 0.25
ADAM_LR, ADAM_B1, ADAM_B2, ADAM_EPS, ADAM_WD, ADAM_STEP = 0.001, 0.9, 0.999, 1e-08, 0.01, 10

N_DEV = 8
AXES = ("x", "y", "c")
V7X_VMEM_BYTES = 64 * 1024 * 1024
VMEM_LIMIT = V7X_VMEM_BYTES - 8 * 1024 * 1024
LANES = 1024
ATT_TQ = 256
ATT_TK = 512
ATT_HEAD_GROUP = 2

NT = (((1,), (1,)), ((), ()))
TN = (((0,), (0,)), ((), ()))
S = jax.ShapeDtypeStruct


def _params(*sem, vmem=VMEM_LIMIT):
    return pltpu.CompilerParams(dimension_semantics=sem, vmem_limit_bytes=vmem)


def _dot(a, b, dims=None, precision=None):
    if dims is None:
        return jnp.dot(a, b, preferred_element_type=F32, precision=precision)
    return lax.dot_general(a, b, dims, preferred_element_type=F32, precision=precision)


def _mx(v):
    return v.astype(_MXU_DTYPE)


def _swap_halves_64(v):
    return jnp.concatenate([v[:, 32:], v[:, :32]], axis=-1)


def _swap_halves_groups(v):
    n = v.shape[-1]
    lane = lax.broadcasted_iota(jnp.int32, v.shape, 1)
    return jnp.where(lane % 64 < 32, pltpu.roll(v, n - 32, 1), pltpu.roll(v, 32, 1))


def _mm(a, b, *, name, ta=False, tb=False, add=None, out_dtype=F32, tm=512, tn=512, tk=512, precision=None):
    M, K = (a.shape[1], a.shape[0]) if ta else a.shape
    N = b.shape[0] if tb else b.shape[1]
    tm, tn, tk = min(tm, M), min(tn, N), min(tk, K)
    assert M % tm == 0 and N % tn == 0 and K % tk == 0, (M, N, K, tm, tn, tk)
    nk = K // tk
    dims = (((0 if ta else 1,), (1 if tb else 0,)), ((), ()))
    has_add = add is not None

    def body(*refs):
        if has_add:
            a_ref, b_ref, add_ref, o_ref, acc = refs
        else:
            a_ref, b_ref, o_ref, acc = refs
        k = pl.program_id(2)

        @pl.when(k == 0)
        def _():
            acc[...] = jnp.zeros_like(acc)

        av, bv = a_ref[...], b_ref[...]
        if precision is None:
            av, bv = _mx(av), _mx(bv)
        acc[...] += _dot(av, bv, dims, precision)

        @pl.when(k == nk - 1)
        def _():
            r = acc[...]
            if has_add:
                r = r + add_ref[...]
            o_ref[...] = r.astype(out_dtype)

    a_spec = pl.BlockSpec((tk, tm), lambda i, j, k: (k, i)) if ta else pl.BlockSpec((tm, tk), lambda i, j, k: (i, k))
    b_spec = pl.BlockSpec((tn, tk), lambda i, j, k: (j, k)) if tb else pl.BlockSpec((tk, tn), lambda i, j, k: (k, j))
    in_specs = [a_spec, b_spec]
    args = [a, b]
    if has_add:
        in_specs.append(pl.BlockSpec((tm, tn), lambda i, j, k: (i, j)))
        args.append(add)
    return pl.pallas_call(
        body, name=name, grid=(M // tm, N // tn, nk), in_specs=in_specs,
        out_specs=pl.BlockSpec((tm, tn), lambda i, j, k: (i, j)), out_shape=S((M, N), out_dtype),
        scratch_shapes=[pltpu.VMEM((tm, tn), F32)],
        compiler_params=_params("parallel", "parallel", "arbitrary"))(*args)


def _ln_fwd_math(z, g, b):
    mu = jnp.mean(z, axis=-1, keepdims=True)
    zc = z - mu
    var = jnp.mean(zc * zc, axis=-1, keepdims=True)
    rstd = lax.rsqrt(var + LN_EPS)
    xhat = zc * rstd
    return xhat * g + b, xhat, rstd


def _ln_bwd_math(dxo, xhat, rstd, g):
    dxh = dxo * g
    m1 = jnp.mean(dxh, axis=-1, keepdims=True)
    m2 = jnp.mean(dxh * xhat, axis=-1, keepdims=True)
    dz = rstd * (dxh - m1 - xhat * m2)
    dg = jnp.sum(dxo * xhat, axis=0, keepdims=True)
    db = jnp.sum(dxo, axis=0, keepdims=True)
    return dz, dg, db


def _rms_fwd_math(xr, g):
    r = lax.rsqrt(jnp.mean(xr * xr, axis=-1, keepdims=True) + RMS_EPS)
    return xr * r * g


def _rms_bwd_math(dy, xr, g):
    r = lax.rsqrt(jnp.mean(xr * xr, axis=-1, keepdims=True) + RMS_EPS)
    gy = dy * g
    dx = r * gy - xr * (r * r * r) * jnp.mean(gy * xr, axis=-1, keepdims=True)
    dg = jnp.sum(dy * xr * r, axis=0, keepdims=True)
    return dx, dg


def _mla_pre_fwd(x, w_in, g_q, g_kv, w_uq_n, w_uq_r, w_uk_t, cos, sin):
    T = x.shape[0]
    tm = min(ATT_TQ, T)
    nq = T // tm
    H = MLA_HEADS

    def body(x_ref, win_ref, gq_ref, gkv_ref, wn_ref, wr_ref, wuk_ref, cos_ref, sin_ref,
             h_ref, kc_ref, qs_ref, cq_ref, qn_ref):
        h = _dot(_mx(x_ref[...]), win_ref[...])
        h_ref[...] = h
        cos_v, sin_v = cos_ref[...], sin_ref[...]
        cq = _mx(_rms_fwd_math(h[:, :MLA_QR], gq_ref[...]))
        ckv = _rms_fwd_math(h[:, MLA_QR:MLA_QR + MLA_C], gkv_ref[...])
        krr = h[:, MLA_QR + MLA_C:]
        kr = krr * cos_v[:, :MLA_ROPE] + _swap_halves_64(krr) * sin_v[:, :MLA_ROPE]
        kc_ref[:, 0:MLA_C] = _mx(ckv)
        kc_ref[:, MLA_C:MLA_C + MLA_ROPE] = _mx(kr)
        kc_ref[:, MLA_C + MLA_ROPE:] = jnp.zeros((tm, MLA_DK - MLA_C - MLA_ROPE), _MXU_DTYPE)
        cq_ref[...] = cq
        qnb = _mx(_dot(cq, wn_ref[...]))
        qn_ref[...] = qnb
        qr = _dot(cq, wr_ref[...])
        qrr = qr * cos_v + _swap_halves_groups(qr) * sin_v
        for hd in range(H):
            ql = _dot(qnb[:, MLA_NOPE * hd:MLA_NOPE * (hd + 1)], wuk_ref[hd])
            qs_ref[0, hd, :, 0:MLA_C] = _mx(ql)
            qs_ref[0, hd, :, MLA_C:MLA_C + MLA_ROPE] = _mx(qrr[:, MLA_ROPE * hd:MLA_ROPE * (hd + 1)])
            qs_ref[0, hd, :, MLA_C + MLA_ROPE:] = jnp.zeros((tm, MLA_DK - MLA_C - MLA_ROPE), _MXU_DTYPE)

    full = lambda shp: pl.BlockSpec(shp, lambda i: (0,) * len(shp))
    rows = lambda n: pl.BlockSpec((tm, n), lambda i: (i, 0))
    n_in = w_in.shape[1]
    return pl.pallas_call(
        body, name="mla_pre_fwd", grid=(nq,),
        in_specs=[rows(D_MODEL), full(w_in.shape), full(g_q.shape), full(g_kv.shape), full(w_uq_n.shape),
                  full(w_uq_r.shape), full(w_uk_t.shape), rows(H * MLA_ROPE), rows(H * MLA_ROPE)],
        out_specs=[rows(n_in), rows(MLA_DK), pl.BlockSpec((1, H, tm, MLA_DK), lambda i: (i, 0, 0, 0)),
                   rows(MLA_QR), rows(H * MLA_NOPE)],
        out_shape=[S((T, n_in), F32), S((T, MLA_DK), _MXU_DTYPE), S((nq, H, tm, MLA_DK), _MXU_DTYPE),
                   S((T, MLA_QR), _MXU_DTYPE), S((T, H * MLA_NOPE), _MXU_DTYPE)],
        compiler_params=_params("parallel"))(x, w_in, g_q, g_kv, w_uq_n, w_uq_r, w_uk_t, cos, sin)


def _att_steps(T, tq, tk):
    qi, kj = [], []
    for i in range(T // tq):
        for j in range((i * tq + tq - 1) // tk + 1):
            qi.append(i)
            kj.append(j)
    return jnp.asarray(np.array(qi, np.int32)), jnp.asarray(np.array(kj, np.int32))


def _ride_exchange(st, n_steps, ins, outs, n_gather, sems):
    if not ins:
        return

    @pl.when(st == 0)
    def _():
        for cp in _exchange_copies(ins, outs, n_gather, *sems):
            cp.start()

    @pl.when(st == n_steps - 1)
    def _():
        for cp in _exchange_copies(ins, outs, n_gather, *sems):
            cp.wait()


def _mla_attn_fwd(qs, kc, gather=(), scatter=()):
    nq, H, tq, DK = qs.shape
    T = kc.shape[0]
    tk = min(ATT_TK, T)
    scale = (MLA_NOPE + MLA_ROPE) ** -0.5
    c2 = scale * math.log2(math.e)
    qi, kj = _att_steps(T, tq, tk)
    n_steps = int(qi.shape[0])
    hg = ATT_HEAD_GROUP
    R = hg * tq
    n_x = len(gather) + len(scatter)

    def body(qi_ref, kj_ref, q_ref, k_ref, *rest):
        x_ins, (o_ref, lse_ref), x_outs = rest[:n_x], rest[n_x:n_x + 2], rest[n_x + 2:2 * n_x + 2]
        m_sc, l_sc, acc_sc = rest[2 * n_x + 2:2 * n_x + 5]
        st = pl.program_id(0)
        _ride_exchange(st, n_steps, x_ins, x_outs, len(gather), rest[2 * n_x + 5:])
        i, j = qi_ref[st], kj_ref[st]
        j_last = (i * tq + tq - 1) // tk

        @pl.when(j == 0)
        def _():
            m_sc[...] = jnp.full_like(m_sc, -jnp.inf)
            l_sc[...] = jnp.zeros_like(l_sc)
            acc_sc[...] = jnp.zeros_like(acc_sc)

        def step(masked):
            k = k_ref[pl.ds(pl.multiple_of(j * tk, tk), tk), :]
            v = k[:, :MLA_C]
            if masked:
                row = lax.broadcasted_iota(jnp.int32, (R, tk), 0) % tq + i * tq
                col = lax.broadcasted_iota(jnp.int32, (R, tk), 1) + j * tk
                causal = col <= row
            n_g = H // hg
            qk = lambda g: _dot(q_ref[0, g * hg:(g + 1) * hg].reshape(R, DK), k, NT)
            s_next = qk(0)
            for g in range(n_g):
                rs = slice(g * R, (g + 1) * R)
                s = s_next
                if g + 1 < n_g:
                    s_next = qk(g + 1)
                if masked:
                    s = jnp.where(causal, s, -jnp.inf)
                m_prev = m_sc[rs]
                m_new = jnp.maximum(m_prev, jnp.max(s, axis=1, keepdims=True))
                a = jnp.exp2((m_prev - m_new) * c2)
                p = jnp.exp2((s - m_new) * c2)
                l_sc[rs] = a * l_sc[rs] + jnp.sum(p, axis=1, keepdims=True)
                acc_sc[rs] = a * acc_sc[rs] + _dot(_mx(p), v)
                m_sc[rs] = m_new

        pl.when(j == j_last)(lambda: step(True))
        pl.when(j != j_last)(lambda: step(False))

        @pl.when(j == j_last)
        def _():
            o_ref[0] = _mx(acc_sc[...] / l_sc[...]).reshape(H, tq, MLA_C)
            lse_ref[0] = (m_sc[...] * scale + jnp.log(l_sc[...])).reshape(H, tq, 1)

    hbm = pl.BlockSpec(memory_space=pl.ANY)
    x_shapes, x_sems = _exchange_shapes(gather, scatter) if n_x else ([], [])
    gs = pltpu.PrefetchScalarGridSpec(
        num_scalar_prefetch=2, grid=(n_steps,),
        in_specs=[pl.BlockSpec((1, H, tq, DK), lambda s, qi, kj: (qi[s], 0, 0, 0)),
                  pl.BlockSpec(memory_space=pltpu.VMEM)] + [hbm] * n_x,
        out_specs=[pl.BlockSpec((1, H, tq, MLA_C), lambda s, qi, kj: (qi[s], 0, 0, 0)),
                   pl.BlockSpec((1, H, tq, 1), lambda s, qi, kj: (qi[s], 0, 0, 0))] + [hbm] * n_x,
        scratch_shapes=[pltpu.VMEM((H * tq, 1), F32), pltpu.VMEM((H * tq, 1), F32),
                        pltpu.VMEM((H * tq, MLA_C), F32)] + x_sems)
    res = pl.pallas_call(
        body, name="mla_attn_fwd", grid_spec=gs,
        out_shape=[S((nq, H, tq, MLA_C), _MXU_DTYPE), S((nq, H, tq, 1), F32)] + x_shapes,
        compiler_params=_params("arbitrary"))(qi, kj, qs, kc, *gather, *scatter)
    return res[0], res[1], res[2:]


def _mla_attn_bwd(qs, kc, dol, lse, delta, gather=(), scatter=()):
    nq, H, tq, DK = qs.shape
    T = kc.shape[0]
    tk = min(ATT_TK, T)
    scale = (MLA_NOPE + MLA_ROPE) ** -0.5
    log2e = math.log2(math.e)
    qi, kj = _att_steps(T, tq, tk)
    n_steps = int(qi.shape[0])
    hg = ATT_HEAD_GROUP
    R = hg * tq
    n_x = len(gather) + len(scatter)

    def body(qi_ref, kj_ref, q_ref, k_ref, do_ref, lse_ref, dl_ref, *rest):
        x_ins, (dq_ref, dk_ref, dv_ref), x_outs = rest[:n_x], rest[n_x:n_x + 3], rest[n_x + 3:2 * n_x + 3]
        dk_acc, dv_acc, sem = rest[2 * n_x + 3:2 * n_x + 6]
        st = pl.program_id(0)
        _ride_exchange(st, n_steps, x_ins, x_outs, len(gather), rest[2 * n_x + 6:])
        i, j = qi_ref[st], kj_ref[st]
        j_last = (i * tq + tq - 1) // tk

        @pl.when(st == 0)
        def _():
            dk_acc[...] = jnp.zeros_like(dk_acc)
            dv_acc[...] = jnp.zeros_like(dv_acc)

        @pl.when(j == 0)
        def _():
            dq_ref[...] = jnp.zeros_like(dq_ref)

        def step(masked):
            koff = pl.multiple_of(j * tk, tk)
            k = k_ref[pl.ds(koff, tk), :]
            v = k[:, :MLA_C]
            if masked:
                row = lax.broadcasted_iota(jnp.int32, (R, tk), 0) % tq + i * tq
                col = lax.broadcasted_iota(jnp.int32, (R, tk), 1) + j * tk
                causal = col <= row
            dk_c = jnp.zeros((tk, DK), F32)
            dv_c = jnp.zeros((tk, MLA_C), F32)
            n_g = H // hg

            def scores(g):
                hs = slice(g * hg, (g + 1) * hg)
                q = q_ref[0, hs].reshape(R, DK)
                do = do_ref[0, hs].reshape(R, MLA_C)
                return q, do, _dot(q, k, NT), _dot(do, v, NT)

            nxt = scores(0)
            for g in range(n_g):
                hs = slice(g * hg, (g + 1) * hg)
                q, do, s, dp = nxt
                if g + 1 < n_g:
                    nxt = scores(g + 1)
                p = jnp.exp2(s * (scale * log2e) - lse_ref[0, hs].reshape(R, 1) * log2e)
                if masked:
                    p = jnp.where(causal, p, 0.0)
                dsb = _mx(p * (dp - dl_ref[0, hs].reshape(R, 1)))
                pb = _mx(p)
                dq_ref[0, hs] += _dot(dsb, k).reshape(hg, tq, DK)
                dk_c = dk_c + _dot(dsb, q, TN)
                dv_c = dv_c + _dot(pb, do, TN)
            dk_acc[pl.ds(koff, tk), :] += dk_c * scale
            dv_acc[pl.ds(koff, tk), :] += dv_c

        pl.when(j == j_last)(lambda: step(True))
        pl.when(j != j_last)(lambda: step(False))

        @pl.when(j == j_last)
        def _():
            dq_ref[...] = dq_ref[...] * scale

        @pl.when(st == n_steps - 1)
        def _():
            c1 = pltpu.make_async_copy(dk_acc, dk_ref, sem.at[0])
            c2 = pltpu.make_async_copy(dv_acc, dv_ref, sem.at[1])
            c1.start()
            c2.start()
            c1.wait()
            c2.wait()

    blk = lambda n: pl.BlockSpec((1, H, tq, n), lambda s, qi, kj: (qi[s], 0, 0, 0))
    hbm = pl.BlockSpec(memory_space=pl.ANY)
    x_shapes, x_sems = _exchange_shapes(gather, scatter) if n_x else ([], [])
    gs = pltpu.PrefetchScalarGridSpec(
        num_scalar_prefetch=2, grid=(n_steps,),
        in_specs=[blk(DK), pl.BlockSpec(memory_space=pltpu.VMEM), blk(MLA_C), blk(1), blk(1)] + [hbm] * n_x,
        out_specs=[blk(DK), hbm, hbm] + [hbm] * n_x,
        scratch_shapes=[pltpu.VMEM((T, DK), F32), pltpu.VMEM((T, MLA_C), F32), pltpu.SemaphoreType.DMA((2,))] + x_sems)
    res = pl.pallas_call(
        body, name="mla_attn_bwd", grid_spec=gs,
        out_shape=[S((nq, H, tq, DK), F32), S((T, DK), F32), S((T, MLA_C), F32)] + x_shapes,
        compiler_params=_params("arbitrary"))(qi, kj, qs, kc, dol, lse, delta, *gather, *scatter)
    return res[0], res[1], res[2], res[3:]


def _mla_uv_fwd(olat, w_uv):
    nq, H, tq, C = olat.shape
    T = nq * tq

    def body(ol_ref, wuv_ref, o_ref):
        for hd in range(H):
            o_ref[:, MLA_V * hd:MLA_V * (hd + 1)] = _mx(_dot(ol_ref[0, hd], wuv_ref[hd]))

    return pl.pallas_call(
        body, name="mla_uv_fwd", grid=(nq,),
        in_specs=[pl.BlockSpec((1, H, tq, C), lambda i: (i, 0, 0, 0)), pl.BlockSpec(w_uv.shape, lambda i: (0, 0, 0))],
        out_specs=pl.BlockSpec((tq, H * MLA_V), lambda i: (i, 0)), out_shape=S((T, H * MLA_V), _MXU_DTYPE),
        compiler_params=_params("parallel"))(olat, w_uv)


def _mla_uv_bwd(do, olat, w_uv):
    nq, H, tq, C = olat.shape

    def body(do_ref, ol_ref, wuv_ref, dol_ref, dl_ref, dw_ref):
        @pl.when(pl.program_id(0) == 0)
        def _():
            dw_ref[...] = jnp.zeros_like(dw_ref)

        dov = do_ref[...]
        for hd in range(H):
            doh = _mx(dov[:, MLA_V * hd:MLA_V * (hd + 1)])
            ol = ol_ref[0, hd]
            dol = _dot(doh, wuv_ref[hd], NT)
            dol_ref[0, hd] = _mx(dol)
            dl_ref[0, hd] = jnp.sum(dol * ol.astype(F32), axis=1, keepdims=True)
            dw_ref[hd] += _dot(ol, doh, TN)

    blk = lambda n: pl.BlockSpec((1, H, tq, n), lambda i: (i, 0, 0, 0))
    return pl.pallas_call(
        body, name="mla_uv_bwd", grid=(nq,),
        in_specs=[pl.BlockSpec((tq, H * MLA_V), lambda i: (i, 0)), blk(C), pl.BlockSpec(w_uv.shape, lambda i: (0, 0, 0))],
        out_specs=[blk(C), blk(1), pl.BlockSpec(w_uv.shape, lambda i: (0, 0, 0))],
        out_shape=[S(olat.shape, _MXU_DTYPE), S((nq, H, tq, 1), F32), S(w_uv.shape, F32)],
        compiler_params=_params("arbitrary"))(do, olat, w_uv)


def _mla_pre_bwd(dqs, dkc, dv, h, x, dres, cq, qn, cos, sin, w_in, g_q, g_kv, w_uq_n, w_uq_r, w_uk):
    nq, H, tm, DK = dqs.shape
    T = nq * tm
    n_in = w_in.shape[1]

    def body(dqs_ref, dkc_ref, dv_ref, h_ref, x_ref, dres_ref, cq_ref, qn_ref, cos_ref, sin_ref,
             win_ref, gq_ref, gkv_ref, wn_ref, wr_ref, wuk_ref,
             gx_ref, dwin_ref, dwn_ref, dwr_ref, dwuk_ref, dgq_ref, dgkv_ref, dqn_sc, dqr_sc, dh_sc):
        @pl.when(pl.program_id(0) == 0)
        def _():
            for r in (dwin_ref, dwn_ref, dwr_ref, dwuk_ref, dgq_ref, dgkv_ref):
                r[...] = jnp.zeros_like(r)

        cos_v, sin_v = cos_ref[...], sin_ref[...]
        qnb = qn_ref[...]
        for hd in range(H):
            dqh = dqs_ref[0, hd]
            dql = _mx(dqh[:, :MLA_C])
            dqn_sc[:, MLA_NOPE * hd:MLA_NOPE * (hd + 1)] = _dot(dql, wuk_ref[hd])
            dwuk_ref[hd] += _dot(dql, qnb[:, MLA_NOPE * hd:MLA_NOPE * (hd + 1)], TN)
            dqr_sc[:, MLA_ROPE * hd:MLA_ROPE * (hd + 1)] = dqh[:, MLA_C:MLA_C + MLA_ROPE]
        dqr = dqr_sc[...]
        dqrb = _mx(dqr * cos_v + _swap_halves_groups(dqr * sin_v))
        dqnb = _mx(dqn_sc[...])
        cq = cq_ref[...]
        dwn_ref[...] += _dot(cq, dqnb, TN)
        dwr_ref[...] += _dot(cq, dqrb, TN)
        dcq = _dot(dqnb, wn_ref[...], NT) + _dot(dqrb, wr_ref[...], NT)
        hv = h_ref[...]
        dxq, dgq = _rms_bwd_math(dcq, hv[:, :MLA_QR], gq_ref[...])
        dgq_ref[...] += dgq
        dkcv = dkc_ref[...]
        dckv = dkcv[:, :MLA_C] + dv_ref[...]
        dxkv, dgkv = _rms_bwd_math(dckv, hv[:, MLA_QR:MLA_QR + MLA_C], gkv_ref[...])
        dgkv_ref[...] += dgkv
        dkr = dkcv[:, MLA_C:MLA_C + MLA_ROPE]
        dkr_raw = dkr * cos_v[:, :MLA_ROPE] + _swap_halves_64(dkr * sin_v[:, :MLA_ROPE])
        dh_sc[:, 0:MLA_QR] = dxq
        dh_sc[:, MLA_QR:MLA_QR + MLA_C] = dxkv
        dh_sc[:, MLA_QR + MLA_C:] = dkr_raw
        dhb = _mx(dh_sc[...])
        gx_ref[...] = dres_ref[...] + _dot(dhb, win_ref[...], NT)
        dwin_ref[...] += _dot(_mx(x_ref[...]), dhb, TN)

    full = lambda shp: pl.BlockSpec(shp, lambda i: (0,) * len(shp))
    rows = lambda n: pl.BlockSpec((tm, n), lambda i: (i, 0))
    return pl.pallas_call(
        body, name="mla_pre_bwd", grid=(nq,),
        in_specs=[pl.BlockSpec((1, H, tm, DK), lambda i: (i, 0, 0, 0)), rows(DK), rows(MLA_C), rows(n_in),
                  rows(D_MODEL), rows(D_MODEL), rows(MLA_QR), rows(H * MLA_NOPE), rows(H * MLA_ROPE), rows(H * MLA_ROPE),
                  full(w_in.shape), full(g_q.shape), full(g_kv.shape), full(w_uq_n.shape), full(w_uq_r.shape),
                  full(w_uk.shape)],
        out_specs=[rows(D_MODEL), full(w_in.shape), full(w_uq_n.shape), full(w_uq_r.shape), full(w_uk.shape),
                   full(g_q.shape), full(g_kv.shape)],
        out_shape=[S((T, D_MODEL), F32), S(w_in.shape, F32), S(w_uq_n.shape, F32), S(w_uq_r.shape, F32),
                   S(w_uk.shape, F32), S(g_q.shape, F32), S(g_kv.shape, F32)],
        scratch_shapes=[pltpu.VMEM((tm, H * MLA_NOPE), F32), pltpu.VMEM((tm, H * MLA_ROPE), F32),
                        pltpu.VMEM((tm, n_in), F32)],
        compiler_params=_params("arbitrary"))(dqs, dkc, dv, h, x, dres, cq, qn, cos, sin, w_in, g_q, g_kv,
                                              w_uq_n, w_uq_r, w_uk)


def _proj_ln_fwd(a, w, xres, g, b, *, name, tm=512):
    T, K = a.shape
    tm = min(tm, T)

    def body(a_ref, w_ref, x_ref, g_ref, b_ref, xo_ref, xob_ref, xh_ref, rs_ref):
        z = ALPHA * x_ref[...] + _dot(a_ref[...], w_ref[...])
        xo, xhat, rstd = _ln_fwd_math(z, g_ref[...], b_ref[...])
        xo_ref[...] = xo
        xob_ref[...] = _mx(xo)
        xh_ref[...] = xhat
        rs_ref[...] = rstd

    rows = lambda n: pl.BlockSpec((tm, n), lambda i: (i, 0))
    full = lambda shp: pl.BlockSpec(shp, lambda i: (0,) * len(shp))
    return pl.pallas_call(
        body, name=name, grid=(T // tm,),
        in_specs=[rows(K), full(w.shape), rows(D_MODEL), full(g.shape), full(b.shape)],
        out_specs=[rows(D_MODEL), rows(D_MODEL), rows(D_MODEL), rows(1)],
        out_shape=[S((T, D_MODEL), F32), S((T, D_MODEL), _MXU_DTYPE), S((T, D_MODEL), F32), S((T, 1), F32)],
        compiler_params=_params("parallel"))(a, w, xres, g, b)


def _proj_ln_bwd(dxo, xhat, rstd, g, a, w, *, name, tm=512):
    T, K = a.shape
    tm = min(tm, T)

    def body(dxo_ref, xh_ref, rs_ref, g_ref, a_ref, w_ref, dres_ref, da_ref, dw_ref, dg_ref, db_ref):
        @pl.when(pl.program_id(0) == 0)
        def _():
            for r in (dw_ref, dg_ref, db_ref):
                r[...] = jnp.zeros_like(r)

        dz, dg, db = _ln_bwd_math(dxo_ref[...], xh_ref[...], rs_ref[...], g_ref[...])
        dg_ref[...] += dg
        db_ref[...] += db
        dres_ref[...] = ALPHA * dz
        dzb = _mx(dz)
        da_ref[...] = _dot(dzb, w_ref[...], NT)
        dw_ref[...] += _dot(a_ref[...], dzb, TN)

    rows = lambda n: pl.BlockSpec((tm, n), lambda i: (i, 0))
    full = lambda shp: pl.BlockSpec(shp, lambda i: (0,) * len(shp))
    return pl.pallas_call(
        body, name=name, grid=(T // tm,),
        in_specs=[rows(D_MODEL), rows(D_MODEL), rows(1), full(g.shape), rows(K), full(w.shape)],
        out_specs=[rows(D_MODEL), rows(K), full(w.shape), full(g.shape), full(g.shape)],
        out_shape=[S((T, D_MODEL), F32), S((T, K), F32), S(w.shape, F32), S(g.shape, F32), S(g.shape, F32)],
        compiler_params=_params("arbitrary"))(dxo, xhat, rstd, g, a, w)


def _mlp_fwd(xb, xres, w_up, w_dn, layer, g, b, *, tm=1024):
    T = xb.shape[0]
    tm = min(tm, T)
    nj, _, _, fc = w_up.shape

    def body(xb_ref, x_ref, wu_ref, wd_ref, g_ref, b_ref, u_ref, xo_ref, xob_ref, xh_ref, rs_ref, acc):
        j = pl.program_id(1)

        @pl.when(j == 0)
        def _():
            acc[...] = ALPHA * x_ref[...]

        u = _dot(xb_ref[...], wu_ref[...])
        u_ref[...] = _mx(u)
        r = jnp.maximum(u, 0.0)
        acc[...] += _dot(_mx(r * r), wd_ref[...])

        @pl.when(j == nj - 1)
        def _():
            xo, xhat, rstd = _ln_fwd_math(acc[...], g_ref[...], b_ref[...])
            xo_ref[...] = xo
            xob_ref[...] = _mx(xo)
            xh_ref[...] = xhat
            rs_ref[...] = rstd

    rows = lambda n: pl.BlockSpec((tm, n), lambda i, j: (i, 0))
    full = lambda shp: pl.BlockSpec(shp, lambda i, j: (0,) * len(shp))
    return pl.pallas_call(
        body, name=f"mlp_fwd_{layer}", grid=(T // tm, nj),
        in_specs=[rows(D_MODEL), rows(D_MODEL),
                  pl.BlockSpec((None, None, D_MODEL, fc), lambda i, j: (j, layer, 0, 0)),
                  pl.BlockSpec((None, None, fc, D_MODEL), lambda i, j: (j, layer, 0, 0)),
                  full(g.shape), full(b.shape)],
        out_specs=[pl.BlockSpec((tm, fc), lambda i, j: (i, j)), rows(D_MODEL), rows(D_MODEL), rows(D_MODEL), rows(1)],
        out_shape=[S((T, nj * fc), _MXU_DTYPE), S((T, D_MODEL), F32), S((T, D_MODEL), _MXU_DTYPE),
                   S((T, D_MODEL), F32), S((T, 1), F32)],
        scratch_shapes=[pltpu.VMEM((tm, D_MODEL), F32)],
        compiler_params=_params("parallel", "arbitrary"))(xb, xres, w_up, w_dn, g, b)


def _mlp_bwd_dx(dxo, xhat, rstd, g, u, w_up, w_dn, layer, *, tm=1024):
    T = dxo.shape[0]
    tm = min(tm, T)
    nj, _, _, fc = w_up.shape

    def body(dxo_ref, xh_ref, rs_ref, g_ref, u_ref, wu_ref, wd_ref, dx_ref, du_ref, dyb_ref, dg_ref, db_ref, acc, dy_sc):
        i, j = pl.program_id(0), pl.program_id(1)

        @pl.when((i == 0) & (j == 0))
        def _():
            dg_ref[...] = jnp.zeros_like(dg_ref)
            db_ref[...] = jnp.zeros_like(db_ref)

        @pl.when(j == 0)
        def _():
            dz, dg, db = _ln_bwd_math(dxo_ref[...], xh_ref[...], rs_ref[...], g_ref[...])
            dg_ref[...] += dg
            db_ref[...] += db
            acc[...] = ALPHA * dz
            dy_sc[...] = _mx(dz)
            dyb_ref[...] = _mx(dz)

        r = jnp.maximum(u_ref[...].astype(F32), 0.0)
        da = _dot(dy_sc[...], wd_ref[...], NT)
        dub = _mx(da * (2.0 * r))
        du_ref[...] = dub
        acc[...] += _dot(dub, wu_ref[...], NT)

        @pl.when(j == nj - 1)
        def _():
            dx_ref[...] = acc[...]

    rows = lambda n: pl.BlockSpec((tm, n), lambda i, j: (i, 0))
    full = lambda shp: pl.BlockSpec(shp, lambda i, j: (0,) * len(shp))
    return pl.pallas_call(
        body, name=f"mlp_bwd_dx_{layer}", grid=(T // tm, nj),
        in_specs=[rows(D_MODEL), rows(D_MODEL), rows(1), full(g.shape), pl.BlockSpec((tm, fc), lambda i, j: (i, j)),
                  pl.BlockSpec((None, None, D_MODEL, fc), lambda i, j: (j, layer, 0, 0)),
                  pl.BlockSpec((None, None, fc, D_MODEL), lambda i, j: (j, layer, 0, 0))],
        out_specs=[rows(D_MODEL), pl.BlockSpec((tm, fc), lambda i, j: (i, j)), rows(D_MODEL), full(g.shape), full(g.shape)],
        out_shape=[S((T, D_MODEL), F32), S((T, nj * fc), _MXU_DTYPE), S((T, D_MODEL), _MXU_DTYPE),
                   S(g.shape, F32), S(g.shape, F32)],
        scratch_shapes=[pltpu.VMEM((tm, D_MODEL), F32), pltpu.VMEM((tm, D_MODEL), _MXU_DTYPE)],
        compiler_params=_params("arbitrary", "arbitrary"))(dxo, xhat, rstd, g, u, w_up, w_dn)


def _mlp_bwd_dw(u, dyb, xinb, du, layer, *, nj, tm=1024):
    T = u.shape[0]
    tm = min(tm, T)
    fc = u.shape[1] // nj

    def body(u_ref, dy_ref, x_ref, du_ref, gd_ref, gu_ref):
        @pl.when(pl.program_id(1) == 0)
        def _():
            gd_ref[...] = jnp.zeros_like(gd_ref)
            gu_ref[...] = jnp.zeros_like(gu_ref)

        r = jnp.maximum(u_ref[...].astype(F32), 0.0)
        gd_ref[...] += _dot(_mx(r * r), dy_ref[...], TN)
        gu_ref[...] += _dot(x_ref[...], du_ref[...], TN)

    return pl.pallas_call(
        body, name=f"mlp_bwd_dw_{layer}", grid=(nj, T // tm),
        in_specs=[pl.BlockSpec((tm, fc), lambda j, i: (i, j)), pl.BlockSpec((tm, D_MODEL), lambda j, i: (i, 0)),
                  pl.BlockSpec((tm, D_MODEL), lambda j, i: (i, 0)), pl.BlockSpec((tm, fc), lambda j, i: (i, j))],
        out_specs=[pl.BlockSpec((None, fc, D_MODEL), lambda j, i: (j, 0, 0)),
                   pl.BlockSpec((None, D_MODEL, fc), lambda j, i: (j, 0, 0))],
        out_shape=[S((nj, fc, D_MODEL), F32), S((nj, D_MODEL, fc), F32)],
        compiler_params=_params("parallel", "arbitrary"))(u, dyb, xinb, du)


SWA_GROUP = SWA_QH // SWA_KVH
SWA_ROWS = SWA_GROUP * SWA_BLOCK


def _swa_heads(a, kh):
    return jnp.concatenate([a[:, SWA_D * (kh * SWA_GROUP + g):SWA_D * (kh * SWA_GROUP + g + 1)]
                            for g in range(SWA_GROUP)], axis=0)


def _swa_group(q, kvp, kvc, bias_ref, sink_ref, n, kh):
    blk, dkv = SWA_BLOCK, SWA_KVH * SWA_D
    cols = slice(kh * SWA_ROWS, (kh + 1) * SWA_ROWS)
    qg = _swa_heads(q, kh)
    kb = jnp.concatenate([kvp[:, SWA_D * kh:SWA_D * (kh + 1)], kvc[:, SWA_D * kh:SWA_D * (kh + 1)]], axis=0)
    vb = jnp.concatenate([kvp[:, dkv + SWA_D * kh:dkv + SWA_D * (kh + 1)],
                          kvc[:, dkv + SWA_D * kh:dkv + SWA_D * (kh + 1)]], axis=0)
    s = _dot(kb, qg, NT) * (SWA_D ** -0.5) + bias_ref[:, cols]
    key = lax.broadcasted_iota(jnp.int32, (2 * blk, SWA_ROWS), 0)
    qry = lax.broadcasted_iota(jnp.int32, (2 * blk, SWA_ROWS), 1) % blk
    valid = (key > qry) & (key <= qry + blk) & ((key >= blk) | (n > 0))
    s = jnp.where(valid, s, -jnp.inf)
    sink = sink_ref[:, cols]
    m = jnp.maximum(jnp.max(s, axis=0, keepdims=True), sink)
    p, ps = jnp.exp(s - m), jnp.exp(sink - m)
    inv = 1.0 / (jnp.sum(p, axis=0, keepdims=True) + ps)
    return qg, kb, vb, p * inv, ps * inv


def _swa_attn_fwd(qkv, bias, sinks):
    T = qkv.shape[0]
    blk = SWA_BLOCK
    nb = T // blk
    dq, dkv = SWA_QH * SWA_D, SWA_KVH * SWA_D

    def body(q_ref, kvp_ref, kvc_ref, bias_ref, sink_ref, o_ref):
        n = pl.program_id(0)
        q, kvp, kvc = q_ref[...], kvp_ref[...], kvc_ref[...]
        for kh in range(SWA_KVH):
            _, _, vb, p, _ = _swa_group(q, kvp, kvc, bias_ref, sink_ref, n, kh)
            og = _mx(_dot(_mx(p), vb, TN))
            for g in range(SWA_GROUP):
                hd = kh * SWA_GROUP + g
                o_ref[:, SWA_D * hd:SWA_D * (hd + 1)] = og[blk * g:blk * (g + 1), :]

    return pl.pallas_call(
        body, name="swa_attn_fwd", grid=(nb,),
        in_specs=[pl.BlockSpec((blk, dq), lambda n: (n, 0)),
                  pl.BlockSpec((blk, 2 * dkv), lambda n: (jnp.maximum(n - 1, 0), dq // (2 * dkv))),
                  pl.BlockSpec((blk, 2 * dkv), lambda n: (n, dq // (2 * dkv))),
                  pl.BlockSpec(bias.shape, lambda n: (0, 0)), pl.BlockSpec(sinks.shape, lambda n: (0, 0))],
        out_specs=pl.BlockSpec((blk, dq), lambda n: (n, 0)), out_shape=S((T, dq), _MXU_DTYPE),
        compiler_params=_params("parallel"))(qkv, qkv, qkv, bias, sinks)


def _swa_attn_bwd(qkv, ob, do, bias, sinks):
    T = qkv.shape[0]
    blk = SWA_BLOCK
    nb = T // blk
    dq, dkv = SWA_QH * SWA_D, SWA_KVH * SWA_D

    def body(q_ref, kvp_ref, kvc_ref, o_ref, do_ref, bias_ref, sink_ref, dqkv_ref, dbias_ref, dsink_ref, carry):
        st = pl.program_id(0)
        n = nb - 1 - st

        @pl.when(st == 0)
        def _():
            carry[...] = jnp.zeros_like(carry)
            dbias_ref[...] = jnp.zeros_like(dbias_ref)
            dsink_ref[...] = jnp.zeros_like(dsink_ref)

        q, kvp, kvc = q_ref[...], kvp_ref[...], kvc_ref[...]
        ov, dov = o_ref[...], do_ref[...]
        ones = jnp.ones((8, SWA_D), F32)
        for kh in range(SWA_KVH):
            cols = slice(kh * SWA_ROWS, (kh + 1) * SWA_ROWS)
            qg, kb, vb, p, ps = _swa_group(q, kvp, kvc, bias_ref, sink_ref, n, kh)
            dog = _swa_heads(dov, kh)
            dl = _dot(ones, dog * _swa_heads(ov, kh).astype(F32), NT, lax.Precision.HIGHEST)[0:1]
            dogb = _mx(dog)
            ds = p * (_dot(vb, dogb, NT) - dl)
            dbias_ref[:, cols] += ds
            dsink_ref[0:1, cols] += -ps * dl
            dsb = _mx(ds * (SWA_D ** -0.5))
            dqg = _mx(_dot(dsb, kb, TN))
            for g in range(SWA_GROUP):
                hd = kh * SWA_GROUP + g
                dqkv_ref[:, SWA_D * hd:SWA_D * (hd + 1)] = dqg[blk * g:blk * (g + 1), :]
            dkb = _dot(dsb, qg)
            dvb = _dot(_mx(p), dogb)
            ko, vo = SWA_D * kh, dkv + SWA_D * kh
            dqkv_ref[:, dq + ko:dq + ko + SWA_D] = _mx(dkb[blk:, :] + carry[:, ko:ko + SWA_D])
            dqkv_ref[:, dq + vo:dq + vo + SWA_D] = _mx(dvb[blk:, :] + carry[:, vo:vo + SWA_D])
            carry[:, ko:ko + SWA_D] = dkb[:blk, :]
            carry[:, vo:vo + SWA_D] = dvb[:blk, :]

    rev = lambda s: nb - 1 - s
    return pl.pallas_call(
        body, name="swa_attn_bwd", grid=(nb,),
        in_specs=[pl.BlockSpec((blk, dq), lambda s: (rev(s), 0)),
                  pl.BlockSpec((blk, 2 * dkv), lambda s: (jnp.maximum(rev(s) - 1, 0), dq // (2 * dkv))),
                  pl.BlockSpec((blk, 2 * dkv), lambda s: (rev(s), dq // (2 * dkv))),
                  pl.BlockSpec((blk, dq), lambda s: (rev(s), 0)), pl.BlockSpec((blk, dq), lambda s: (rev(s), 0)),
                  pl.BlockSpec(bias.shape, lambda s: (0, 0)), pl.BlockSpec(sinks.shape, lambda s: (0, 0))],
        out_specs=[pl.BlockSpec((blk, dq + 2 * dkv), lambda s: (rev(s), 0)),
                   pl.BlockSpec(bias.shape, lambda s: (0, 0)), pl.BlockSpec((8, sinks.shape[1]), lambda s: (0, 0))],
        out_shape=[S((T, dq + 2 * dkv), _MXU_DTYPE), S(bias.shape, F32), S((8, sinks.shape[1]), F32)],
        scratch_shapes=[pltpu.VMEM((blk, 2 * dkv), F32)],
        compiler_params=_params("arbitrary"))(qkv, qkv, qkv, ob, do, bias, sinks)


def _t5_onehot():
    i = jnp.arange(SWA_BLOCK)
    j = jnp.arange(2 * SWA_BLOCK)
    n = jnp.maximum(i[:, None] + SWA_BLOCK - j[None, :], 0)
    max_exact = REL_BUCKETS // 2
    nf = jnp.maximum(n, 1).astype(F32)
    large = max_exact + (jnp.log(nf / max_exact) / math.log(REL_MAX_DIST / max_exact)
                         * (REL_BUCKETS - max_exact)).astype(jnp.int32)
    large = jnp.minimum(large, REL_BUCKETS - 1)
    bucket = jnp.where(n < max_exact, n, large).reshape(-1)
    return (bucket[None, :] == jnp.arange(REL_BUCKETS)[:, None]).astype(F32)


def _loss_head(y, target, *, tm=1024):
    T, D = y.shape
    tm = min(tm, T)

    def body(y_ref, t_ref, loss_ref, dy_ref):
        @pl.when(pl.program_id(0) == 0)
        def _():
            loss_ref[...] = jnp.zeros_like(loss_ref)

        d = y_ref[...] - t_ref[...]
        dy_ref[...] = d * (1.0 / D)
        rs = jnp.sum(d * d, axis=1, keepdims=True)
        loss_ref[...] += (0.5 / D) * jnp.sum(rs, axis=0, keepdims=True)

    rows = pl.BlockSpec((tm, D), lambda i: (i, 0))
    return pl.pallas_call(
        body, name="loss_head", grid=(T // tm,), in_specs=[rows, rows],
        out_specs=[pl.BlockSpec((1, 1), lambda i: (0, 0)), rows], out_shape=[S((1, 1), F32), S((T, D), F32)],
        compiler_params=_params("arbitrary"))(y, target)


def _exchange_copies(ins, outs, n_gather, send_sems, recv_sems, loc_sems):
    mx, my, mc = lax.axis_index("x"), lax.axis_index("y"), lax.axis_index("c")
    me = 4 * mx + 2 * my + mc
    copies = []
    for a in range(len(ins)):
        src = ins[a] if a < n_gather else ins[a].at[me]
        copies.append(pltpu.make_async_copy(src, outs[a].at[me], loc_sems.at[a]))
    for k in range(1, N_DEV):
        px, py, pc = mx ^ ((k >> 2) & 1), my ^ ((k >> 1) & 1), mc ^ (k & 1)
        peer = 4 * px + 2 * py + pc
        for a in range(len(ins)):
            src = ins[a] if a < n_gather else ins[a].at[peer]
            copies.append(pltpu.make_async_remote_copy(
                src_ref=src, dst_ref=outs[a].at[me], send_sem=send_sems.at[a, k - 1],
                recv_sem=recv_sems.at[a, k - 1], device_id=(px, py, pc), device_id_type=pl.DeviceIdType.MESH))
    return copies


def _exchange_shapes(gather, scatter):
    n_arr = len(gather) + len(scatter)
    out_shape = [S((N_DEV,) + tuple(g.shape), g.dtype) for g in gather] + [S(s.shape, s.dtype) for s in scatter]
    sems = [pltpu.SemaphoreType.DMA((n_arr, N_DEV - 1)), pltpu.SemaphoreType.DMA((n_arr, N_DEV - 1)),
            pltpu.SemaphoreType.DMA((n_arr,))]
    return out_shape, sems


def _exchange(gather, scatter, *, name):
    n_g = len(gather)
    n_arr = n_g + len(scatter)

    def body(*refs):
        copies = _exchange_copies(refs[:n_arr], refs[n_arr:2 * n_arr], n_g, *refs[2 * n_arr:])
        for cp in copies:
            cp.start()
        for cp in copies:
            cp.wait()

    hbm = pl.BlockSpec(memory_space=pl.ANY)
    out_shape, sems = _exchange_shapes(gather, scatter)
    return pl.pallas_call(
        body, name=name, in_specs=[hbm] * n_arr, out_specs=[hbm] * n_arr, out_shape=out_shape,
        scratch_shapes=sems)(*gather, *scatter)


def _adamw(parts, w, m, v, *, name, tr=256):
    R, C = w.shape
    tr = min(tr, R)
    assert R % tr == 0

    def body(p_ref, w_ref, m_ref, v_ref, g_ref, d_ref, nm_ref, nv_ref):
        g = p_ref[0]
        for k in range(1, N_DEV):
            g = g + p_ref[k]
        g_ref[...] = g
        m_new = ADAM_B1 * m_ref[...] + (1.0 - ADAM_B1) * g
        v_new = ADAM_B2 * v_ref[...] + (1.0 - ADAM_B2) * (g * g)
        m_hat = m_new / (1.0 - ADAM_B1 ** ADAM_STEP)
        v_hat = v_new / (1.0 - ADAM_B2 ** ADAM_STEP)
        d_ref[...] = -ADAM_LR * (m_hat / (jnp.sqrt(v_hat) + ADAM_EPS) + ADAM_WD * w_ref[...])
        nm_ref[...] = m_new
        nv_ref[...] = v_new

    rows = pl.BlockSpec((tr, C), lambda i: (i, 0))
    return pl.pallas_call(
        body, name=name, grid=(R // tr,),
        in_specs=[pl.BlockSpec((N_DEV, tr, C), lambda i: (0, i, 0)), rows, rows, rows],
        out_specs=[rows] * 4, out_shape=[S((R, C), F32)] * 4,
        compiler_params=_params("parallel"))(parts, w, m, v)


def _rows_of(n):
    return -(-n // LANES)


def _pack(pieces, total_rows, dtype, lead=()):
    out = []
    for p in pieces:
        flat = p.reshape(lead + (-1,)).astype(dtype)
        n = flat.shape[-1]
        pad = _rows_of(n) * LANES - n
        if pad:
            flat = jnp.pad(flat, [(0, 0)] * len(lead) + [(0, pad)])
        out.append(flat.reshape(lead + (-1, LANES)))
    used = sum(o.shape[-2] for o in out)
    if total_rows > used:
        out.append(jnp.zeros(lead + (total_rows - used, LANES), dtype))
    return jnp.concatenate(out, axis=len(lead))


def _unpack(buf, shapes, lead=()):
    res, r0 = [], 0
    for shp in shapes:
        n = int(np.prod(shp))
        nr = _rows_of(n)
        piece = lax.slice_in_dim(buf, r0, r0 + nr, axis=len(lead)).reshape(lead + (nr * LANES,))
        res.append(lax.slice_in_dim(piece, 0, n, axis=len(lead)).reshape(lead + tuple(shp)))
        r0 += nr
    return res


def _round_up(n, m):
    return -(-n // m) * m


BIG = ["mla_w_in", "mla_w_uq", "mla_w_uk", "mla_w_uv", "mla_w_o", "kv_w_shared", "swa_w_q", "swa_w_o",
       "mlp_w_up", "mlp_w_down"]
GAINS = ["mla_g_q", "mla_g_kv"]
SHARDED = BIG + GAINS
REPL = ["swa_sinks", "rel_bias", "ln_mix_g", "ln_mix_b", "ln_mlp_g", "ln_mlp_b"]
WEIGHTS = ["mla_w_in", "mla_g_q", "mla_g_kv", "mla_w_uq", "mla_w_uk", "mla_w_uv", "mla_w_o", "kv_w_shared",
           "swa_w_q", "swa_sinks", "swa_w_o", "rel_bias", "mlp_w_up", "mlp_w_down", "ln_mix_g", "ln_mix_b",
           "ln_mlp_g", "ln_mlp_b"]


def kernel(x, mla_w_in, mla_g_q, mla_g_kv, mla_w_uq, mla_w_uk, mla_w_uv, mla_w_o, kv_w_shared, swa_w_q, swa_sinks, swa_w_o, rel_bias, mlp_w_up, mlp_w_down, ln_mix_g, ln_mix_b, ln_mlp_g, ln_mlp_b, loss_target, m_mla_w_in, m_mla_g_q, m_mla_g_kv, m_mla_w_uq, m_mla_w_uk, m_mla_w_uv, m_mla_w_o, m_kv_w_shared, m_swa_w_q, m_swa_sinks, m_swa_w_o, m_rel_bias, m_mlp_w_up, m_mlp_w_down, m_ln_mix_g, m_ln_mix_b, m_ln_mlp_g, m_ln_mlp_b, v_mla_w_in, v_mla_g_q, v_mla_g_kv, v_mla_w_uq, v_mla_w_uk, v_mla_w_uv, v_mla_w_o, v_kv_w_shared, v_swa_w_q, v_swa_sinks, v_swa_w_o, v_rel_bias, v_mlp_w_up, v_mlp_w_down, v_ln_mix_g, v_ln_mix_b, v_ln_mlp_g, v_ln_mlp_b):
    args = dict(locals())
    W = {n: args[n] for n in WEIGHTS}
    M = {n: args["m_" + n] for n in WEIGHTS}
    V = {n: args["v_" + n] for n in WEIGHTS}
    T = x.shape[1]
    x2d = x.reshape(T, D_MODEL)
    tgt = loss_target.reshape(T, D_MODEL)
    H = MLA_HEADS

    SH = {"mla_w_in": (-1, mla_w_in.shape[-1]), "mla_w_uq": (-1, H * (MLA_NOPE + MLA_ROPE)),
          "mla_w_uk": (-1, H * MLA_NOPE), "mla_w_uv": (-1, H * MLA_V), "mla_w_o": (-1, D_MODEL),
          "kv_w_shared": (-1, kv_w_shared.shape[-1]), "swa_w_q": (-1, swa_w_q.shape[-1]), "swa_w_o": (-1, D_MODEL)}
    slab = lambda d, n: d[n].reshape(SH[n])
    bf = lambda a: a.astype(_MXU_DTYPE)
    gains_slab = lambda d: jnp.pad(jnp.concatenate([d["mla_g_q"], d["mla_g_kv"]], axis=1),
                                   ((0, 7), (0, 128 - d["mla_g_q"].shape[1] - d["mla_g_kv"].shape[1])))
    n_gq, n_gkv = mla_g_q.shape[1], mla_g_kv.shape[1]
    w_in_s, w_uq_s, w_uk_s, gains_all = _exchange(
        [bf(slab(W, "mla_w_in")), bf(slab(W, "mla_w_uq")), bf(slab(W, "mla_w_uk")), gains_slab(W)], [],
        name="gather_mla_in")
    later = [bf(slab(W, n)) for n in ("mla_w_uv", "mla_w_o", "kv_w_shared", "swa_w_q", "swa_w_o")]
    later += [bf(mlp_w_up), bf(mlp_w_down)]
    w_in = w_in_s.reshape(D_MODEL, -1)
    g_q = gains_all[:, 0, :n_gq].reshape(1, MLA_QR)
    g_kv = gains_all[:, 0, n_gq:n_gq + n_gkv].reshape(1, MLA_C)
    w_uq = w_uq_s.reshape(MLA_QR, H, MLA_NOPE + MLA_ROPE)
    w_uq_n = w_uq[:, :, :MLA_NOPE].reshape(MLA_QR, H * MLA_NOPE)
    w_uq_r = w_uq[:, :, MLA_NOPE:].reshape(MLA_QR, H * MLA_ROPE)
    w_uk = w_uk_s.reshape(MLA_C, H, MLA_NOPE).transpose(1, 0, 2)
    w_uk_t = w_uk.transpose(0, 2, 1)
    ln = lambda a, l: a[l].reshape(1, D_MODEL)

    half = MLA_ROPE // 2
    inv = ROPE_THETA ** (-jnp.arange(half, dtype=F32) / half)
    ang = jnp.arange(T, dtype=F32)[:, None] * inv[None, :]
    cos = jnp.tile(jnp.concatenate([jnp.cos(ang), jnp.cos(ang)], -1), (1, H))
    sin = jnp.tile(jnp.concatenate([-jnp.sin(ang), jnp.sin(ang)], -1), (1, H))

    h, kc, qs, cq, qn = _mla_pre_fwd(x2d, w_in, g_q, g_kv, w_uq_n, w_uq_r, w_uk_t, cos, sin)
    olat, lse, (w_uv_s, w_o_s, w_kv_s, w_q_s, w_o2_s, w_up, w_dn) = _mla_attn_fwd(qs, kc, gather=later)
    w_uv = w_uv_s.reshape(MLA_C, H, MLA_V).transpose(1, 0, 2)
    w_o = w_o_s.reshape(H * MLA_V, D_MODEL)
    w_qkv = jnp.concatenate([w_q_s.reshape(D_MODEL, -1), w_kv_s.reshape(D_MODEL, -1)], axis=1)
    w_o2 = w_o2_s.reshape(SWA_QH * SWA_D, D_MODEL)
    o_mla = _mla_uv_fwd(olat, w_uv)
    x1, x1b, xh1, rs1 = _proj_ln_fwd(o_mla, w_o, x2d, ln(ln_mix_g, 0), ln(ln_mix_b, 0), name="mla_out_ln_fwd")
    u0, x2, x2b, xh2, rs2 = _mlp_fwd(x1b, x1, w_up, w_dn, 0, ln(ln_mlp_g, 0), ln(ln_mlp_b, 0))
    onehot = _t5_onehot()
    bias = _mm(rel_bias.T, onehot, name="rel_bias_expand", precision=lax.Precision.HIGHEST, tn=8192).reshape(
        SWA_QH * SWA_BLOCK, 2 * SWA_BLOCK).T
    sink_rows = jnp.repeat(swa_sinks.reshape(SWA_QH), SWA_BLOCK).reshape(1, SWA_QH * SWA_BLOCK)
    qkv = _mm(x2b, w_qkv, name="swa_qkv_fwd", out_dtype=_MXU_DTYPE, tm=1024, tn=512, tk=1024)
    o_swa = _swa_attn_fwd(qkv, bias, sink_rows)
    x3, x3b, xh3, rs3 = _proj_ln_fwd(o_swa, w_o2, x2, ln(ln_mix_g, 1), ln(ln_mix_b, 1), name="swa_out_ln_fwd")
    u1, x4, _, xh4, rs4 = _mlp_fwd(x3b, x3, w_up, w_dn, 1, ln(ln_mlp_g, 1), ln(ln_mlp_b, 1))
    loss_part, dx4 = _loss_head(x4, tgt)
    loss = lax.psum(loss_part[0, 0], AXES)

    nj = w_up.shape[0]
    dx3, du1, dy4b, dg_mlp1, db_mlp1 = _mlp_bwd_dx(dx4, xh4, rs4, ln(ln_mlp_g, 1), u1, w_up, w_dn, 1)
    g_dn1, g_up1 = _mlp_bwd_dw(u1, dy4b, x3b, du1, 1, nj=nj)
    dres3, do_swa, g_o2, dg_mix1, db_mix1 = _proj_ln_bwd(dx3, xh3, rs3, ln(ln_mix_g, 1), o_swa, w_o2,
                                                         name="swa_out_ln_bwd")
    dqkv, dbias, dsink = _swa_attn_bwd(qkv, o_swa, do_swa, bias, sink_rows)
    g_rel = _mm(onehot, dbias.T.reshape(SWA_QH, -1), name="rel_bias_grad", tb=True, precision=lax.Precision.HIGHEST,
                tk=8192)
    head_of_row = (jnp.arange(SWA_QH * SWA_BLOCK)[:, None] // SWA_BLOCK == jnp.arange(SWA_QH)[None, :]).astype(F32)
    g_sinks = _mm(dsink, head_of_row, name="sinks_grad", precision=lax.Precision.HIGHEST, tk=2048)[0:1]
    dx2 = _mm(dqkv, w_qkv, name="swa_qkv_bwd_dx", tb=True, add=dres3, tm=1024, tn=1024, tk=512)
    g_qkv = _mm(x2b, dqkv, name="swa_qkv_bwd_dw", ta=True, tm=1024, tn=512, tk=1024)
    dx1, du0, dy2b, dg_mlp0, db_mlp0 = _mlp_bwd_dx(dx2, xh2, rs2, ln(ln_mlp_g, 0), u0, w_up, w_dn, 0)
    g_dn0, g_up0 = _mlp_bwd_dw(u0, dy2b, x1b, du0, 0, nj=nj)
    dres1, do_mla, g_o, dg_mix0, db_mix0 = _proj_ln_bwd(dx1, xh1, rs1, ln(ln_mix_g, 0), o_mla, w_o,
                                                        name="mla_out_ln_bwd")
    dol, delta, g_uv = _mla_uv_bwd(do_mla, olat, w_uv)
    repl_grads = {
        "swa_sinks": g_sinks, "rel_bias": g_rel,
        "ln_mix_g": jnp.concatenate([dg_mix0, dg_mix1], 0), "ln_mix_b": jnp.concatenate([db_mix0, db_mix1], 0),
        "ln_mlp_g": jnp.concatenate([dg_mlp0, dg_mlp1], 0), "ln_mlp_b": jnp.concatenate([db_mlp0, db_mlp1], 0),
    }
    repl_rows = _round_up(sum(_rows_of(W[n].size) for n in REPL), 8)
    r_part = _pack([repl_grads[n] for n in REPL], repl_rows, F32)
    by_dev = lambda g: g.reshape((N_DEV, g.shape[0] // N_DEV) + g.shape[1:])
    early = [by_dev(g_o2), by_dev(g_qkv), g_up0, g_up1, g_dn0, g_dn1, by_dev(g_o),
             by_dev(g_uv.transpose(1, 0, 2).reshape(MLA_C, H * MLA_V))]
    dqs, dkc, dv, (r_all, p_o2, p_qkv, p_up0, p_up1, p_dn0, p_dn1, p_o, p_uv) = _mla_attn_bwd(
        qs, kc, dol, lse, delta, gather=[r_part], scatter=early)
    grad_x, g_in, g_uq_n, g_uq_r, g_uk, g_gq, g_gkv = _mla_pre_bwd(
        dqs, dkc, dv, h, x2d, dres1, cq, qn, cos, sin, w_in, g_q, g_kv, w_uq_n, w_uq_r, w_uk)
    g_uq = jnp.concatenate([g_uq_n.reshape(MLA_QR, H, MLA_NOPE), g_uq_r.reshape(MLA_QR, H, MLA_ROPE)], -1)
    g_gains = jnp.pad(jnp.concatenate([g_gq.reshape(N_DEV, n_gq), g_gkv.reshape(N_DEV, n_gkv)], axis=1)[:, None, :],
                      ((0, 0), (0, 7), (0, 128 - n_gq - n_gkv)))
    p_in, p_uq, p_uk, p_gains = _exchange(
        [], [by_dev(g_in), by_dev(g_uq.reshape(MLA_QR, -1)), by_dev(g_uk.transpose(1, 0, 2).reshape(MLA_C, -1)),
             g_gains], name="exchange_mla_in_grads")

    res = {}

    def adam(name, parts, names, to_slab, from_slab):
        out = _adamw(parts, to_slab(W), to_slab(M), to_slab(V), name="adamw_" + name)
        for k in range(4):
            for n, a in zip(names, from_slab(out[k])):
                res[(k, n)] = a.reshape(W[n].shape)

    one = lambda n: (lambda d: slab(d, n))
    adam("swa_w_o", p_o2, ["swa_w_o"], one("swa_w_o"), lambda s: [s])
    dq_cols = SWA_QH * SWA_D
    adam("swa_qkv", p_qkv, ["swa_w_q", "kv_w_shared"],
         lambda d: jnp.concatenate([slab(d, "swa_w_q"), slab(d, "kv_w_shared")], axis=1),
         lambda s: [s[:, :dq_cols], s[:, dq_cols:]])
    for name, parts in (("mlp_w_up", (p_up0, p_up1)), ("mlp_w_down", (p_dn0, p_dn1))):
        per_layer = [_adamw(parts[l], W[name][l], M[name][l], V[name][l], name=f"adamw_{name}_{l}") for l in range(DEPTH)]
        for k in range(4):
            res[(k, name)] = jnp.stack([per_layer[l][k] for l in range(DEPTH)], axis=0)
    adam("mla_w_o", p_o, ["mla_w_o"], one("mla_w_o"), lambda s: [s])
    adam("mla_w_uv", p_uv, ["mla_w_uv"], one("mla_w_uv"), lambda s: [s])
    adam("mla_w_in", p_in, ["mla_w_in"], one("mla_w_in"), lambda s: [s])
    adam("mla_w_uq", p_uq, ["mla_w_uq"], one("mla_w_uq"), lambda s: [s])
    adam("mla_w_uk", p_uk, ["mla_w_uk"], one("mla_w_uk"), lambda s: [s])
    adam("mla_gains", p_gains, ["mla_g_q", "mla_g_kv"], gains_slab,
         lambda s: [s[0:1, :n_gq], s[0:1, n_gq:n_gq + n_gkv]])
    adam("replicated", r_all, REPL, lambda d: _pack([d[n] for n in REPL], repl_rows, F32),
         lambda s: _unpack(s, [W[n].shape for n in REPL]))
    return (loss, grad_x.reshape(x.shape), *[res[(k, n)] for k in range(4) for n in WEIGHTS])
```

```python
import functools
import math

import numpy as np
import jax
import jax.numpy as jnp
from jax import lax
from jax.experimental import pallas as pl
from jax.experimental.pallas import tpu as pltpu

F32 = jnp.float32
_MXU_DTYPE = jnp.bfloat16

D_MODEL = 1024
DEPTH = 2
MLA_HEADS = 8
MLA_NOPE = 128
MLA_ROPE = 64
MLA_V = 128
MLA_QR = 384
MLA_C = 256
MLA_DK = 384
ROPE_THETA = 10000.0
SWA_QH = 16
SWA_KVH = 4
SWA_D = 64
SWA_BLOCK = 128
REL_BUCKETS = 32
REL_MAX_DIST = 128
D_FF = 4096
LN_EPS = 1e-5
RMS_EPS = 1e-6
ALPHA = (2 * DEPTH) ** 0.25
ADAM_LR, ADAM_B1, ADAM_B2, ADAM_EPS, ADAM_WD, ADAM_STEP = 0.001, 0.9, 0.999, 1e-08, 0.01, 10

N_DEV = 8
AXES = ("x", "y", "c")
V7X_VMEM_BYTES = 64 * 1024 * 1024
VMEM_LIMIT = V7X_VMEM_BYTES - 8 * 1024 * 1024
LANES = 1024
ATT_TQ = 256
ATT_TK = 512
ATT_HEAD_GROUP = 2

NT = (((1,), (1,)), ((), ()))
TN = (((0,), (0,)), ((), ()))
S = jax.ShapeDtypeStruct


def _params(*sem, vmem=VMEM_LIMIT):
    return pltpu.CompilerParams(dimension_semantics=sem, vmem_limit_bytes=vmem)


def _dot(a, b, dims=None, precision=None):
    if dims is None:
        return jnp.dot(a, b, preferred_element_type=F32, precision=precision)
    return lax.dot_general(a, b, dims, preferred_element_type=F32, precision=precision)


def _mx(v):
    return v.astype(_MXU_DTYPE)


def _swap_halves_64(v):
    return jnp.concatenate([v[:, 32:], v[:, :32]], axis=-1)


def _swap_halves_groups(v):
    n = v.shape[-1]
    lane = lax.broadcasted_iota(jnp.int32, v.shape, 1)
    return jnp.where(lane % 64 < 32, pltpu.roll(v, n - 32, 1), pltpu.roll(v, 32, 1))


def _mm(a, b, *, name, ta=False, tb=False, add=None, out_dtype=F32, tm=512, tn=512, tk=512, precision=None):
    M, K = (a.shape[1], a.shape[0]) if ta else a.shape
    N = b.shape[0] if tb else b.shape[1]
    tm, tn, tk = min(tm, M), min(tn, N), min(tk, K)
    assert M % tm == 0 and N % tn == 0 and K % tk == 0, (M, N, K, tm, tn, tk)
    nk = K // tk
    dims = (((0 if ta else 1,), (1 if tb else 0,)), ((), ()))
    has_add = add is not None

    def body(*refs):
        if has_add:
            a_ref, b_ref, add_ref, o_ref, acc = refs
        else:
            a_ref, b_ref, o_ref, acc = refs
        k = pl.program_id(2)

        @pl.when(k == 0)
        def _():
            acc[...] = jnp.zeros_like(acc)

        av, bv = a_ref[...], b_ref[...]
        if precision is None:
            av, bv = _mx(av), _mx(bv)
        acc[...] += _dot(av, bv, dims, precision)

        @pl.when(k == nk - 1)
        def _():
            r = acc[...]
            if has_add:
                r = r + add_ref[...]
            o_ref[...] = r.astype(out_dtype)

    a_spec = pl.BlockSpec((tk, tm), lambda i, j, k: (k, i)) if ta else pl.BlockSpec((tm, tk), lambda i, j, k: (i, k))
    b_spec = pl.BlockSpec((tn, tk), lambda i, j, k: (j, k)) if tb else pl.BlockSpec((tk, tn), lambda i, j, k: (k, j))
    in_specs = [a_spec, b_spec]
    args = [a, b]
    if has_add:
        in_specs.append(pl.BlockSpec((tm, tn), lambda i, j, k: (i, j)))
        args.append(add)
    return pl.pallas_call(
        body, name=name, grid=(M // tm, N // tn, nk), in_specs=in_specs,
        out_specs=pl.BlockSpec((tm, tn), lambda i, j, k: (i, j)), out_shape=S((M, N), out_dtype),
        scratch_shapes=[pltpu.VMEM((tm, tn), F32)],
        compiler_params=_params("parallel", "parallel", "arbitrary"))(*args)


def _ln_fwd_math(z, g, b):
    mu = jnp.mean(z, axis=-1, keepdims=True)
    zc = z - mu
    var = jnp.mean(zc * zc, axis=-1, keepdims=True)
    rstd = lax.rsqrt(var + LN_EPS)
    xhat = zc * rstd
    return xhat * g + b, xhat, rstd


def _ln_bwd_math(dxo, xhat, rstd, g):
    dxh = dxo * g
    m1 = jnp.mean(dxh, axis=-1, keepdims=True)
    m2 = jnp.mean(dxh * xhat, axis=-1, keepdims=True)
    dz = rstd * (dxh - m1 - xhat * m2)
    dg = jnp.sum(dxo * xhat, axis=0, keepdims=True)
    db = jnp.sum(dxo, axis=0, keepdims=True)
    return dz, dg, db


def _rms_fwd_math(xr, g):
    r = lax.rsqrt(jnp.mean(xr * xr, axis=-1, keepdims=True) + RMS_EPS)
    return xr * r * g


def _rms_bwd_math(dy, xr, g):
    r = lax.rsqrt(jnp.mean(xr * xr, axis=-1, keepdims=True) + RMS_EPS)
    gy = dy * g
    dx = r * gy - xr * (r * r * r) * jnp.mean(gy * xr, axis=-1, keepdims=True)
    dg = jnp.sum(dy * xr * r, axis=0, keepdims=True)
    return dx, dg


def _mla_pre_fwd(x, w_in, g_q, g_kv, w_uq_n, w_uq_r, w_uk_t, cos, sin):
    T = x.shape[0]
    tm = min(ATT_TQ, T)
    nq = T // tm
    H = MLA_HEADS

    tk = min(ATT_TK, T)

    def body(x_ref, win_ref, gq_ref, gkv_ref, wn_ref, wr_ref, wuk_ref, cos_ref, sin_ref,
             h_ref, kc_ref, kct_ref, qs_ref, cq_ref, qn_ref):
        h = _dot(_mx(x_ref[...]), win_ref[...])
        h_ref[...] = h
        cos_v, sin_v = cos_ref[...], sin_ref[...]
        cq = _mx(_rms_fwd_math(h[:, :MLA_QR], gq_ref[...]))
        ckv = _rms_fwd_math(h[:, MLA_QR:MLA_QR + MLA_C], gkv_ref[...])
        krr = h[:, MLA_QR + MLA_C:]
        kr = krr * cos_v[:, :MLA_ROPE] + _swap_halves_64(krr) * sin_v[:, :MLA_ROPE]
        kr_pad = jnp.concatenate([kr, jnp.zeros((tm, MLA_DK - MLA_C - MLA_ROPE), F32)], axis=1)
        kc_ref[:, 0:MLA_C] = _mx(ckv)
        kc_ref[:, MLA_C:] = _mx(kr_pad)
        kct_ref[0:MLA_C, :] = _mx(ckv.T)
        kct_ref[MLA_C:, :] = _mx(kr_pad.T)
        cq_ref[...] = cq
        qnb = _mx(_dot(cq, wn_ref[...]))
        qn_ref[...] = qnb
        qr = _dot(cq, wr_ref[...])
        qrr = qr * cos_v + _swap_halves_groups(qr) * sin_v
        for hd in range(H):
            ql = _dot(qnb[:, MLA_NOPE * hd:MLA_NOPE * (hd + 1)], wuk_ref[hd])
            qs_ref[0, hd, :, 0:MLA_C] = _mx(ql)
            qs_ref[0, hd, :, MLA_C:MLA_C + MLA_ROPE] = _mx(qrr[:, MLA_ROPE * hd:MLA_ROPE * (hd + 1)])
            qs_ref[0, hd, :, MLA_C + MLA_ROPE:] = jnp.zeros((tm, MLA_DK - MLA_C - MLA_ROPE), _MXU_DTYPE)

    full = lambda shp: pl.BlockSpec(shp, lambda i: (0,) * len(shp))
    rows = lambda n: pl.BlockSpec((tm, n), lambda i: (i, 0))
    n_in = w_in.shape[1]
    return pl.pallas_call(
        body, name="mla_pre_fwd", grid=(nq,),
        in_specs=[rows(D_MODEL), full(w_in.shape), full(g_q.shape), full(g_kv.shape), full(w_uq_n.shape),
                  full(w_uq_r.shape), full(w_uk_t.shape), rows(H * MLA_ROPE), rows(H * MLA_ROPE)],
        out_specs=[rows(n_in), rows(MLA_DK),
                   pl.BlockSpec((None, MLA_DK, tm), lambda i: (i * tm // tk, 0, i % (tk // tm))),
                   pl.BlockSpec((1, H, tm, MLA_DK), lambda i: (i, 0, 0, 0)), rows(MLA_QR), rows(H * MLA_NOPE)],
        out_shape=[S((T, n_in), F32), S((T, MLA_DK), _MXU_DTYPE), S((T // tk, MLA_DK, tk), _MXU_DTYPE),
                   S((nq, H, tm, MLA_DK), _MXU_DTYPE), S((T, MLA_QR), _MXU_DTYPE), S((T, H * MLA_NOPE), _MXU_DTYPE)],
        compiler_params=_params("parallel"))(x, w_in, g_q, g_kv, w_uq_n, w_uq_r, w_uk_t, cos, sin)


def _att_steps(T, tq, tk):
    qi, kj = [], []
    for i in range(T // tq):
        for j in range((i * tq + tq - 1) // tk + 1):
            qi.append(i)
            kj.append(j)
    return jnp.asarray(np.array(qi, np.int32)), jnp.asarray(np.array(kj, np.int32))


def _ride_exchange(st, n_steps, ins, outs, n_gather, sems):
    if not ins:
        return

    @pl.when(st == 0)
    def _():
        for cp in _exchange_copies(ins, outs, n_gather, *sems):
            cp.start()

    @pl.when(st == n_steps - 1)
    def _():
        for cp in _exchange_copies(ins, outs, n_gather, *sems):
            cp.wait()


def _mla_attn_fwd(qs, kc, kct, gather=(), scatter=()):
    nq, H, tq, DK = qs.shape
    T = kc.shape[0]
    tk = min(ATT_TK, T)
    scale = (MLA_NOPE + MLA_ROPE) ** -0.5
    c2 = scale * math.log2(math.e)
    qi, kj = _att_steps(T, tq, tk)
    n_steps = int(qi.shape[0])
    hg = ATT_HEAD_GROUP
    R = hg * tq
    n_x = len(gather) + len(scatter)

    def body(qi_ref, kj_ref, q_ref, k_ref, kt_ref, *rest):
        x_ins, (o_ref, lse_ref), x_outs = rest[:n_x], rest[n_x:n_x + 2], rest[n_x + 2:2 * n_x + 2]
        m_sc, l_sc, acc_sc = rest[2 * n_x + 2:2 * n_x + 5]
        st = pl.program_id(0)
        _ride_exchange(st, n_steps, x_ins, x_outs, len(gather), rest[2 * n_x + 5:])
        i, j = qi_ref[st], kj_ref[st]
        j_last = (i * tq + tq - 1) // tk

        @pl.when(j == 0)
        def _():
            m_sc[...] = jnp.full_like(m_sc, -jnp.inf)
            l_sc[...] = jnp.zeros_like(l_sc)
            acc_sc[...] = jnp.zeros_like(acc_sc)

        def step(masked):
            k = k_ref[...]
            vt = kt_ref[0:MLA_C, :]
            if masked:
                key = lax.broadcasted_iota(jnp.int32, (tk, R), 0) + j * tk
                qry = lax.broadcasted_iota(jnp.int32, (tk, R), 1) % tq + i * tq
                causal = key <= qry
            n_g = H // hg
            qk = lambda g: _dot(k, q_ref[0, g * hg:(g + 1) * hg].reshape(R, DK), NT)
            s_next = qk(0)
            for g in range(n_g):
                cs = slice(g * R, (g + 1) * R)
                s = s_next
                if g + 1 < n_g:
                    s_next = qk(g + 1)
                if masked:
                    s = jnp.where(causal, s, -jnp.inf)
                m_prev = m_sc[:, cs]
                m_new = jnp.maximum(m_prev, jnp.max(s, axis=0, keepdims=True))
                a = jnp.exp2((m_prev - m_new) * c2)
                p = jnp.exp2((s - m_new) * c2)
                l_sc[:, cs] = a * l_sc[:, cs] + jnp.sum(p, axis=0, keepdims=True)
                acc_sc[:, cs] = a * acc_sc[:, cs] + _dot(vt, _mx(p))
                m_sc[:, cs] = m_new

        pl.when(j == j_last)(lambda: step(True))
        pl.when(j != j_last)(lambda: step(False))

        @pl.when(j == j_last)
        def _():
            o_ref[0] = _mx(acc_sc[...] / l_sc[...])
            lse_ref[0] = m_sc[...] * scale + jnp.log(l_sc[...])

    hbm = pl.BlockSpec(memory_space=pl.ANY)
    x_shapes, x_sems = _exchange_shapes(gather, scatter) if n_x else ([], [])
    gs = pltpu.PrefetchScalarGridSpec(
        num_scalar_prefetch=2, grid=(n_steps,),
        in_specs=[pl.BlockSpec((1, H, tq, DK), lambda s, qi, kj: (qi[s], 0, 0, 0)),
                  pl.BlockSpec((tk, DK), lambda s, qi, kj: (kj[s], 0)),
                  pl.BlockSpec((None, DK, tk), lambda s, qi, kj: (kj[s], 0, 0))] + [hbm] * n_x,
        out_specs=[pl.BlockSpec((1, MLA_C, H * tq), lambda s, qi, kj: (qi[s], 0, 0)),
                   pl.BlockSpec((1, 1, H * tq), lambda s, qi, kj: (qi[s], 0, 0))] + [hbm] * n_x,
        scratch_shapes=[pltpu.VMEM((1, H * tq), F32), pltpu.VMEM((1, H * tq), F32),
                        pltpu.VMEM((MLA_C, H * tq), F32)] + x_sems)
    res = pl.pallas_call(
        body, name="mla_attn_fwd", grid_spec=gs,
        out_shape=[S((nq, MLA_C, H * tq), _MXU_DTYPE), S((nq, 1, H * tq), F32)] + x_shapes,
        compiler_params=_params("arbitrary"))(qi, kj, qs, kc, kct, *gather, *scatter)
    return res[0], res[1], res[2:]


def _mla_attn_bwd(qs, kc, kct, dol, lse, delta, gather=(), scatter=()):
    nq, H, tq, DK = qs.shape
    T = kc.shape[0]
    tk = min(ATT_TK, T)
    scale = (MLA_NOPE + MLA_ROPE) ** -0.5
    log2e = math.log2(math.e)
    qi, kj = _att_steps(T, tq, tk)
    n_steps = int(qi.shape[0])
    hg = ATT_HEAD_GROUP
    R = hg * tq
    n_x = len(gather) + len(scatter)

    def body(qi_ref, kj_ref, q_ref, k_ref, kt_ref, do_ref, lse_ref, dl_ref, *rest):
        x_ins, (dq_ref, dk_ref, dv_ref), x_outs = rest[:n_x], rest[n_x:n_x + 3], rest[n_x + 3:2 * n_x + 3]
        dk_acc, dv_acc, sem = rest[2 * n_x + 3:2 * n_x + 6]
        st = pl.program_id(0)
        _ride_exchange(st, n_steps, x_ins, x_outs, len(gather), rest[2 * n_x + 6:])
        i, j = qi_ref[st], kj_ref[st]
        j_last = (i * tq + tq - 1) // tk

        @pl.when(st == 0)
        def _():
            dk_acc[...] = jnp.zeros_like(dk_acc)
            dv_acc[...] = jnp.zeros_like(dv_acc)

        @pl.when(j == 0)
        def _():
            dq_ref[...] = jnp.zeros_like(dq_ref)

        def step(masked):
            k, kt = k_ref[...], kt_ref[...]
            v = k[:, :MLA_C]
            if masked:
                key = lax.broadcasted_iota(jnp.int32, (tk, R), 0) + j * tk
                qry = lax.broadcasted_iota(jnp.int32, (tk, R), 1) % tq + i * tq
                causal = key <= qry
            dk_c = jnp.zeros((tk, DK), F32)
            dvt_c = jnp.zeros((MLA_C, tk), F32)
            n_g = H // hg

            def scores(g):
                q = q_ref[0, g * hg:(g + 1) * hg].reshape(R, DK)
                dot = do_ref[0, :, g * R:(g + 1) * R]
                return q, dot, _dot(k, q, NT), _dot(v, dot)

            nxt = scores(0)
            for g in range(n_g):
                cs = slice(g * R, (g + 1) * R)
                q, dot, s, dp = nxt
                if g + 1 < n_g:
                    nxt = scores(g + 1)
                p = jnp.exp2(s * (scale * log2e) - lse_ref[0, :, cs] * log2e)
                if masked:
                    p = jnp.where(causal, p, 0.0)
                dsb = _mx(p * (dp - dl_ref[0, :, cs]))
                dq_ref[0, :, cs] += _dot(kt, dsb)
                dk_c = dk_c + _dot(dsb, q)
                dvt_c = dvt_c + _dot(dot, _mx(p), NT)
            dk_acc[pl.ds(pl.multiple_of(j * tk, tk), tk), :] += dk_c * scale
            dv_acc[j] += dvt_c

        pl.when(j == j_last)(lambda: step(True))
        pl.when(j != j_last)(lambda: step(False))

        @pl.when(j == j_last)
        def _():
            dq_ref[...] = dq_ref[...] * scale

        @pl.when(st == n_steps - 1)
        def _():
            c1 = pltpu.make_async_copy(dk_acc, dk_ref, sem.at[0])
            c2 = pltpu.make_async_copy(dv_acc, dv_ref, sem.at[1])
            c1.start()
            c2.start()
            c1.wait()
            c2.wait()

    cols = lambda n: pl.BlockSpec((1, n, H * tq), lambda s, qi, kj: (qi[s], 0, 0))
    hbm = pl.BlockSpec(memory_space=pl.ANY)
    x_shapes, x_sems = _exchange_shapes(gather, scatter) if n_x else ([], [])
    gs = pltpu.PrefetchScalarGridSpec(
        num_scalar_prefetch=2, grid=(n_steps,),
        in_specs=[pl.BlockSpec((1, H, tq, DK), lambda s, qi, kj: (qi[s], 0, 0, 0)),
                  pl.BlockSpec((tk, DK), lambda s, qi, kj: (kj[s], 0)),
                  pl.BlockSpec((None, DK, tk), lambda s, qi, kj: (kj[s], 0, 0)),
                  cols(MLA_C), cols(1), cols(1)] + [hbm] * n_x,
        out_specs=[cols(DK), hbm, hbm] + [hbm] * n_x,
        scratch_shapes=[pltpu.VMEM((T, DK), F32), pltpu.VMEM((T // tk, MLA_C, tk), F32),
                        pltpu.SemaphoreType.DMA((2,))] + x_sems)
    res = pl.pallas_call(
        body, name="mla_attn_bwd", grid_spec=gs,
        out_shape=[S((nq, DK, H * tq), F32), S((T, DK), F32), S((T // tk, MLA_C, tk), F32)] + x_shapes,
        compiler_params=_params("arbitrary"))(qi, kj, qs, kc, kct, dol, lse, delta, *gather, *scatter)
    return res[0], res[1], res[2], res[3:]


def _mla_uv_fwd(olat, w_uv):
    nq, C, cols = olat.shape
    H = w_uv.shape[0]
    tq = cols // H

    def body(ol_ref, wuv_ref, o_ref):
        for hd in range(H):
            o_ref[:, MLA_V * hd:MLA_V * (hd + 1)] = _mx(_dot(ol_ref[0, :, tq * hd:tq * (hd + 1)], wuv_ref[hd], TN))

    return pl.pallas_call(
        body, name="mla_uv_fwd", grid=(nq,),
        in_specs=[pl.BlockSpec((1, C, cols), lambda i: (i, 0, 0)), pl.BlockSpec(w_uv.shape, lambda i: (0, 0, 0))],
        out_specs=pl.BlockSpec((tq, H * MLA_V), lambda i: (i, 0)), out_shape=S((nq * tq, H * MLA_V), _MXU_DTYPE),
        compiler_params=_params("parallel"))(olat, w_uv)


def _mla_uv_bwd(do, olat, w_uv):
    nq, C, cols = olat.shape
    H = w_uv.shape[0]
    tq = cols // H

    def body(do_ref, ol_ref, wuv_ref, dol_ref, dl_ref, dw_ref):
        @pl.when(pl.program_id(0) == 0)
        def _():
            dw_ref[...] = jnp.zeros_like(dw_ref)

        dov = do_ref[...]
        for hd in range(H):
            cs = slice(tq * hd, tq * (hd + 1))
            doh = _mx(dov[:, MLA_V * hd:MLA_V * (hd + 1)])
            ol = ol_ref[0, :, cs]
            dol = _dot(wuv_ref[hd], doh, NT)
            dol_ref[0, :, cs] = _mx(dol)
            dl_ref[0, :, cs] = jnp.sum(dol * ol.astype(F32), axis=0, keepdims=True)
            dw_ref[hd] += _dot(ol, doh)

    blk = lambda n: pl.BlockSpec((1, n, cols), lambda i: (i, 0, 0))
    return pl.pallas_call(
        body, name="mla_uv_bwd", grid=(nq,),
        in_specs=[pl.BlockSpec((tq, H * MLA_V), lambda i: (i, 0)), blk(C), pl.BlockSpec(w_uv.shape, lambda i: (0, 0, 0))],
        out_specs=[blk(C), blk(1), pl.BlockSpec(w_uv.shape, lambda i: (0, 0, 0))],
        out_shape=[S(olat.shape, _MXU_DTYPE), S((nq, 1, cols), F32), S(w_uv.shape, F32)],
        compiler_params=_params("arbitrary"))(do, olat, w_uv)


def _mla_pre_bwd(dqs, dkc, dv, h, x, dres, cq, qn, cos, sin, w_in, g_q, g_kv, w_uq_n, w_uq_r, w_uk):
    nq, DK, cols = dqs.shape
    H = w_uk.shape[0]
    tm = cols // H
    T = nq * tm
    tk = dv.shape[2]
    n_in = w_in.shape[1]

    def body(dqs_ref, dkc_ref, dv_ref, h_ref, x_ref, dres_ref, cq_ref, qn_ref, cos_ref, sin_ref,
             win_ref, gq_ref, gkv_ref, wn_ref, wr_ref, wuk_ref,
             gx_ref, dwin_ref, dwn_ref, dwr_ref, dwuk_ref, dgq_ref, dgkv_ref, dqn_sc, dqr_sc, dh_sc):
        @pl.when(pl.program_id(0) == 0)
        def _():
            for r in (dwin_ref, dwn_ref, dwr_ref, dwuk_ref, dgq_ref, dgkv_ref):
                r[...] = jnp.zeros_like(r)

        cos_v, sin_v = cos_ref[...], sin_ref[...]
        qnb = qn_ref[...]
        for hd in range(H):
            cs = slice(tm * hd, tm * (hd + 1))
            dql = _mx(dqs_ref[0, 0:MLA_C, cs])
            dqn_sc[:, MLA_NOPE * hd:MLA_NOPE * (hd + 1)] = _dot(dql, wuk_ref[hd], TN)
            dwuk_ref[hd] += _dot(dql, qnb[:, MLA_NOPE * hd:MLA_NOPE * (hd + 1)])
            dqr_sc[MLA_ROPE * hd:MLA_ROPE * (hd + 1), :] = dqs_ref[0, MLA_C:MLA_C + MLA_ROPE, cs]
        dqr = dqr_sc[...].T
        dqrb = _mx(dqr * cos_v + _swap_halves_groups(dqr * sin_v))
        dqnb = _mx(dqn_sc[...])
        cq = cq_ref[...]
        dwn_ref[...] += _dot(cq, dqnb, TN)
        dwr_ref[...] += _dot(cq, dqrb, TN)
        dcq = _dot(dqnb, wn_ref[...], NT) + _dot(dqrb, wr_ref[...], NT)
        hv = h_ref[...]
        dxq, dgq = _rms_bwd_math(dcq, hv[:, :MLA_QR], gq_ref[...])
        dgq_ref[...] += dgq
        dkcv = dkc_ref[...]
        dckv = dkcv[:, :MLA_C] + dv_ref[...].T
        dxkv, dgkv = _rms_bwd_math(dckv, hv[:, MLA_QR:MLA_QR + MLA_C], gkv_ref[...])
        dgkv_ref[...] += dgkv
        dkr = dkcv[:, MLA_C:MLA_C + MLA_ROPE]
        dkr_raw = dkr * cos_v[:, :MLA_ROPE] + _swap_halves_64(dkr * sin_v[:, :MLA_ROPE])
        dh_sc[:, 0:MLA_QR] = dxq
        dh_sc[:, MLA_QR:MLA_QR + MLA_C] = dxkv
        dh_sc[:, MLA_QR + MLA_C:] = dkr_raw
        dhb = _mx(dh_sc[...])
        gx_ref[...] = dres_ref[...] + _dot(dhb, win_ref[...], NT)
        dwin_ref[...] += _dot(_mx(x_ref[...]), dhb, TN)

    full = lambda shp: pl.BlockSpec(shp, lambda i: (0,) * len(shp))
    rows = lambda n: pl.BlockSpec((tm, n), lambda i: (i, 0))
    return pl.pallas_call(
        body, name="mla_pre_bwd", grid=(nq,),
        in_specs=[pl.BlockSpec((1, DK, cols), lambda i: (i, 0, 0)), rows(DK),
                  pl.BlockSpec((None, MLA_C, tm), lambda i: (i * tm // tk, 0, i % (tk // tm))), rows(n_in),
                  rows(D_MODEL), rows(D_MODEL), rows(MLA_QR), rows(H * MLA_NOPE), rows(H * MLA_ROPE), rows(H * MLA_ROPE),
                  full(w_in.shape), full(g_q.shape), full(g_kv.shape), full(w_uq_n.shape), full(w_uq_r.shape),
                  full(w_uk.shape)],
        out_specs=[rows(D_MODEL), full(w_in.shape), full(w_uq_n.shape), full(w_uq_r.shape), full(w_uk.shape),
                   full(g_q.shape), full(g_kv.shape)],
        out_shape=[S((T, D_MODEL), F32), S(w_in.shape, F32), S(w_uq_n.shape, F32), S(w_uq_r.shape, F32),
                   S(w_uk.shape, F32), S(g_q.shape, F32), S(g_kv.shape, F32)],
        scratch_shapes=[pltpu.VMEM((tm, H * MLA_NOPE), F32), pltpu.VMEM((H * MLA_ROPE, tm), F32),
                        pltpu.VMEM((tm, n_in), F32)],
        compiler_params=_params("arbitrary"))(dqs, dkc, dv, h, x, dres, cq, qn, cos, sin, w_in, g_q, g_kv,
                                              w_uq_n, w_uq_r, w_uk)


def _proj_ln_fwd(a, w, xres, g, b, *, name, tm=512):
    T, K = a.shape
    tm = min(tm, T)

    def body(a_ref, w_ref, x_ref, g_ref, b_ref, xo_ref, xob_ref, xh_ref, rs_ref):
        z = ALPHA * x_ref[...] + _dot(a_ref[...], w_ref[...])
        xo, xhat, rstd = _ln_fwd_math(z, g_ref[...], b_ref[...])
        xo_ref[...] = xo
        xob_ref[...] = _mx(xo)
        xh_ref[...] = xhat
        rs_ref[...] = rstd

    rows = lambda n: pl.BlockSpec((tm, n), lambda i: (i, 0))
    full = lambda shp: pl.BlockSpec(shp, lambda i: (0,) * len(shp))
    return pl.pallas_call(
        body, name=name, grid=(T // tm,),
        in_specs=[rows(K), full(w.shape), rows(D_MODEL), full(g.shape), full(b.shape)],
        out_specs=[rows(D_MODEL), rows(D_MODEL), rows(D_MODEL), rows(1)],
        out_shape=[S((T, D_MODEL), F32), S((T, D_MODEL), _MXU_DTYPE), S((T, D_MODEL), F32), S((T, 1), F32)],
        compiler_params=_params("parallel"))(a, w, xres, g, b)


def _proj_ln_bwd(dxo, xhat, rstd, g, a, w, *, name, tm=512):
    T, K = a.shape
    tm = min(tm, T)

    def body(dxo_ref, xh_ref, rs_ref, g_ref, a_ref, w_ref, dres_ref, da_ref, dw_ref, dg_ref, db_ref):
        @pl.when(pl.program_id(0) == 0)
        def _():
            for r in (dw_ref, dg_ref, db_ref):
                r[...] = jnp.zeros_like(r)

        dz, dg, db = _ln_bwd_math(dxo_ref[...], xh_ref[...], rs_ref[...], g_ref[...])
        dg_ref[...] += dg
        db_ref[...] += db
        dres_ref[...] = ALPHA * dz
        dzb = _mx(dz)
        da_ref[...] = _dot(dzb, w_ref[...], NT)
        dw_ref[...] += _dot(a_ref[...], dzb, TN)

    rows = lambda n: pl.BlockSpec((tm, n), lambda i: (i, 0))
    full = lambda shp: pl.BlockSpec(shp, lambda i: (0,) * len(shp))
    return pl.pallas_call(
        body, name=name, grid=(T // tm,),
        in_specs=[rows(D_MODEL), rows(D_MODEL), rows(1), full(g.shape), rows(K), full(w.shape)],
        out_specs=[rows(D_MODEL), rows(K), full(w.shape), full(g.shape), full(g.shape)],
        out_shape=[S((T, D_MODEL), F32), S((T, K), F32), S(w.shape, F32), S(g.shape, F32), S(g.shape, F32)],
        compiler_params=_params("arbitrary"))(dxo, xhat, rstd, g, a, w)


def _mlp_fwd(xb, xres, w_up, w_dn, layer, g, b, *, tm=1024):
    T = xb.shape[0]
    tm = min(tm, T)
    nj, _, _, fc = w_up.shape

    def body(xb_ref, x_ref, wu_ref, wd_ref, g_ref, b_ref, u_ref, xo_ref, xob_ref, xh_ref, rs_ref, acc):
        j = pl.program_id(1)

        @pl.when(j == 0)
        def _():
            acc[...] = ALPHA * x_ref[...]

        u = _dot(xb_ref[...], wu_ref[...])
        u_ref[...] = _mx(u)
        r = jnp.maximum(u, 0.0)
        acc[...] += _dot(_mx(r * r), wd_ref[...])

        @pl.when(j == nj - 1)
        def _():
            xo, xhat, rstd = _ln_fwd_math(acc[...], g_ref[...], b_ref[...])
            xo_ref[...] = xo
            xob_ref[...] = _mx(xo)
            xh_ref[...] = xhat
            rs_ref[...] = rstd

    rows = lambda n: pl.BlockSpec((tm, n), lambda i, j: (i, 0))
    full = lambda shp: pl.BlockSpec(shp, lambda i, j: (0,) * len(shp))
    return pl.pallas_call(
        body, name=f"mlp_fwd_{layer}", grid=(T // tm, nj),
        in_specs=[rows(D_MODEL), rows(D_MODEL),
                  pl.BlockSpec((None, None, D_MODEL, fc), lambda i, j: (j, layer, 0, 0)),
                  pl.BlockSpec((None, None, fc, D_MODEL), lambda i, j: (j, layer, 0, 0)),
                  full(g.shape), full(b.shape)],
        out_specs=[pl.BlockSpec((tm, fc), lambda i, j: (i, j)), rows(D_MODEL), rows(D_MODEL), rows(D_MODEL), rows(1)],
        out_shape=[S((T, nj * fc), _MXU_DTYPE), S((T, D_MODEL), F32), S((T, D_MODEL), _MXU_DTYPE),
                   S((T, D_MODEL), F32), S((T, 1), F32)],
        scratch_shapes=[pltpu.VMEM((tm, D_MODEL), F32)],
        compiler_params=_params("parallel", "arbitrary"))(xb, xres, w_up, w_dn, g, b)


def _mlp_bwd_dx(dxo, xhat, rstd, g, u, w_up, w_dn, layer, *, tm=1024):
    T = dxo.shape[0]
    tm = min(tm, T)
    nj, _, _, fc = w_up.shape

    def body(dxo_ref, xh_ref, rs_ref, g_ref, u_ref, wu_ref, wd_ref, dx_ref, du_ref, dyb_ref, dg_ref, db_ref, acc, dy_sc):
        i, j = pl.program_id(0), pl.program_id(1)

        @pl.when((i == 0) & (j == 0))
        def _():
            dg_ref[...] = jnp.zeros_like(dg_ref)
            db_ref[...] = jnp.zeros_like(db_ref)

        @pl.when(j == 0)
        def _():
            dz, dg, db = _ln_bwd_math(dxo_ref[...], xh_ref[...], rs_ref[...], g_ref[...])
            dg_ref[...] += dg
            db_ref[...] += db
            acc[...] = ALPHA * dz
            dy_sc[...] = _mx(dz)
            dyb_ref[...] = _mx(dz)

        r = jnp.maximum(u_ref[...].astype(F32), 0.0)
        da = _dot(dy_sc[...], wd_ref[...], NT)
        dub = _mx(da * (2.0 * r))
        du_ref[...] = dub
        acc[...] += _dot(dub, wu_ref[...], NT)

        @pl.when(j == nj - 1)
        def _():
            dx_ref[...] = acc[...]

    rows = lambda n: pl.BlockSpec((tm, n), lambda i, j: (i, 0))
    full = lambda shp: pl.BlockSpec(shp, lambda i, j: (0,) * len(shp))
    return pl.pallas_call(
        body, name=f"mlp_bwd_dx_{layer}", grid=(T // tm, nj),
        in_specs=[rows(D_MODEL), rows(D_MODEL), rows(1), full(g.shape), pl.BlockSpec((tm, fc), lambda i, j: (i, j)),
                  pl.BlockSpec((None, None, D_MODEL, fc), lambda i, j: (j, layer, 0, 0)),
                  pl.BlockSpec((None, None, fc, D_MODEL), lambda i, j: (j, layer, 0, 0))],
        out_specs=[rows(D_MODEL), pl.BlockSpec((tm, fc), lambda i, j: (i, j)), rows(D_MODEL), full(g.shape), full(g.shape)],
        out_shape=[S((T, D_MODEL), F32), S((T, nj * fc), _MXU_DTYPE), S((T, D_MODEL), _MXU_DTYPE),
                   S(g.shape, F32), S(g.shape, F32)],
        scratch_shapes=[pltpu.VMEM((tm, D_MODEL), F32), pltpu.VMEM((tm, D_MODEL), _MXU_DTYPE)],
        compiler_params=_params("arbitrary", "arbitrary"))(dxo, xhat, rstd, g, u, w_up, w_dn)


def _mlp_bwd_dw(u, dyb, xinb, du, layer, *, nj, tm=1024):
    T = u.shape[0]
    tm = min(tm, T)
    fc = u.shape[1] // nj

    def body(u_ref, dy_ref, x_ref, du_ref, gd_ref, gu_ref):
        @pl.when(pl.program_id(1) == 0)
        def _():
            gd_ref[...] = jnp.zeros_like(gd_ref)
            gu_ref[...] = jnp.zeros_like(gu_ref)

        r = jnp.maximum(u_ref[...].astype(F32), 0.0)
        gd_ref[...] += _dot(_mx(r * r), dy_ref[...], TN)
        gu_ref[...] += _dot(x_ref[...], du_ref[...], TN)

    return pl.pallas_call(
        body, name=f"mlp_bwd_dw_{layer}", grid=(nj, T // tm),
        in_specs=[pl.BlockSpec((tm, fc), lambda j, i: (i, j)), pl.BlockSpec((tm, D_MODEL), lambda j, i: (i, 0)),
                  pl.BlockSpec((tm, D_MODEL), lambda j, i: (i, 0)), pl.BlockSpec((tm, fc), lambda j, i: (i, j))],
        out_specs=[pl.BlockSpec((None, fc, D_MODEL), lambda j, i: (j, 0, 0)),
                   pl.BlockSpec((None, D_MODEL, fc), lambda j, i: (j, 0, 0))],
        out_shape=[S((nj, fc, D_MODEL), F32), S((nj, D_MODEL, fc), F32)],
        compiler_params=_params("parallel", "arbitrary"))(u, dyb, xinb, du)


SWA_GROUP = SWA_QH // SWA_KVH
SWA_ROWS = SWA_GROUP * SWA_BLOCK


def _swa_heads(a, kh):
    return jnp.concatenate([a[:, SWA_D * (kh * SWA_GROUP + g):SWA_D * (kh * SWA_GROUP + g + 1)]
                            for g in range(SWA_GROUP)], axis=0)


def _swa_group(q, kvp, kvc, bias_ref, sink_ref, n, kh):
    blk, dkv = SWA_BLOCK, SWA_KVH * SWA_D
    cols = slice(kh * SWA_ROWS, (kh + 1) * SWA_ROWS)
    qg = _swa_heads(q, kh)
    kb = jnp.concatenate([kvp[:, SWA_D * kh:SWA_D * (kh + 1)], kvc[:, SWA_D * kh:SWA_D * (kh + 1)]], axis=0)
    vb = jnp.concatenate([kvp[:, dkv + SWA_D * kh:dkv + SWA_D * (kh + 1)],
                          kvc[:, dkv + SWA_D * kh:dkv + SWA_D * (kh + 1)]], axis=0)
    s = _dot(kb, qg, NT) * (SWA_D ** -0.5) + bias_ref[:, cols]
    key = lax.broadcasted_iota(jnp.int32, (2 * blk, SWA_ROWS), 0)
    qry = lax.broadcasted_iota(jnp.int32, (2 * blk, SWA_ROWS), 1) % blk
    valid = (key > qry) & (key <= qry + blk) & ((key >= blk) | (n > 0))
    s = jnp.where(valid, s, -jnp.inf)
    sink = sink_ref[:, cols]
    m = jnp.maximum(jnp.max(s, axis=0, keepdims=True), sink)
    p, ps = jnp.exp(s - m), jnp.exp(sink - m)
    inv = 1.0 / (jnp.sum(p, axis=0, keepdims=True) + ps)
    return qg, kb, vb, p * inv, ps * inv


def _swa_attn_fwd(qkv, bias, sinks):
    T = qkv.shape[0]
    blk = SWA_BLOCK
    nb = T // blk
    dq, dkv = SWA_QH * SWA_D, SWA_KVH * SWA_D

    def body(q_ref, kvp_ref, kvc_ref, bias_ref, sink_ref, o_ref):
        n = pl.program_id(0)
        q, kvp, kvc = q_ref[...], kvp_ref[...], kvc_ref[...]
        for kh in range(SWA_KVH):
            _, _, vb, p, _ = _swa_group(q, kvp, kvc, bias_ref, sink_ref, n, kh)
            og = _mx(_dot(_mx(p), vb, TN))
            for g in range(SWA_GROUP):
                hd = kh * SWA_GROUP + g
                o_ref[:, SWA_D * hd:SWA_D * (hd + 1)] = og[blk * g:blk * (g + 1), :]

    return pl.pallas_call(
        body, name="swa_attn_fwd", grid=(nb,),
        in_specs=[pl.BlockSpec((blk, dq), lambda n: (n, 0)),
                  pl.BlockSpec((blk, 2 * dkv), lambda n: (jnp.maximum(n - 1, 0), dq // (2 * dkv))),
                  pl.BlockSpec((blk, 2 * dkv), lambda n: (n, dq // (2 * dkv))),
                  pl.BlockSpec(bias.shape, lambda n: (0, 0)), pl.BlockSpec(sinks.shape, lambda n: (0, 0))],
        out_specs=pl.BlockSpec((blk, dq), lambda n: (n, 0)), out_shape=S((T, dq), _MXU_DTYPE),
        compiler_params=_params("parallel"))(qkv, qkv, qkv, bias, sinks)


def _swa_attn_bwd(qkv, ob, do, bias, sinks):
    T = qkv.shape[0]
    blk = SWA_BLOCK
    nb = T // blk
    dq, dkv = SWA_QH * SWA_D, SWA_KVH * SWA_D

    def body(q_ref, kvp_ref, kvc_ref, o_ref, do_ref, bias_ref, sink_ref, dqkv_ref, dbias_ref, dsink_ref, carry):
        st = pl.program_id(0)
        n = nb - 1 - st

        @pl.when(st == 0)
        def _():
            carry[...] = jnp.zeros_like(carry)
            dbias_ref[...] = jnp.zeros_like(dbias_ref)
            dsink_ref[...] = jnp.zeros_like(dsink_ref)

        q, kvp, kvc = q_ref[...], kvp_ref[...], kvc_ref[...]
        ov, dov = o_ref[...], do_ref[...]
        ones = jnp.ones((8, SWA_D), F32)
        for kh in range(SWA_KVH):
            cols = slice(kh * SWA_ROWS, (kh + 1) * SWA_ROWS)
            qg, kb, vb, p, ps = _swa_group(q, kvp, kvc, bias_ref, sink_ref, n, kh)
            dog = _swa_heads(dov, kh)
            dl = _dot(ones, dog * _swa_heads(ov, kh).astype(F32), NT, lax.Precision.HIGHEST)[0:1]
            dogb = _mx(dog)
            ds = p * (_dot(vb, dogb, NT) - dl)
            dbias_ref[:, cols] += ds
            dsink_ref[0:1, cols] += -ps * dl
            dsb = _mx(ds * (SWA_D ** -0.5))
            dqg = _mx(_dot(dsb, kb, TN))
            for g in range(SWA_GROUP):
                hd = kh * SWA_GROUP + g
                dqkv_ref[:, SWA_D * hd:SWA_D * (hd + 1)] = dqg[blk * g:blk * (g + 1), :]
            dkb = _dot(dsb, qg)
            dvb = _dot(_mx(p), dogb)
            ko, vo = SWA_D * kh, dkv + SWA_D * kh
            dqkv_ref[:, dq + ko:dq + ko + SWA_D] = _mx(dkb[blk:, :] + carry[:, ko:ko + SWA_D])
            dqkv_ref[:, dq + vo:dq + vo + SWA_D] = _mx(dvb[blk:, :] + carry[:, vo:vo + SWA_D])
            carry[:, ko:ko + SWA_D] = dkb[:blk, :]
            carry[:, vo:vo + SWA_D] = dvb[:blk, :]

    rev = lambda s: nb - 1 - s
    return pl.pallas_call(
        body, name="swa_attn_bwd", grid=(nb,),
        in_specs=[pl.BlockSpec((blk, dq), lambda s: (rev(s), 0)),
                  pl.BlockSpec((blk, 2 * dkv), lambda s: (jnp.maximum(rev(s) - 1, 0), dq // (2 * dkv))),
                  pl.BlockSpec((blk, 2 * dkv), lambda s: (rev(s), dq // (2 * dkv))),
                  pl.BlockSpec((blk, dq), lambda s: (rev(s), 0)), pl.BlockSpec((blk, dq), lambda s: (rev(s), 0)),
                  pl.BlockSpec(bias.shape, lambda s: (0, 0)), pl.BlockSpec(sinks.shape, lambda s: (0, 0))],
        out_specs=[pl.BlockSpec((blk, dq + 2 * dkv), lambda s: (rev(s), 0)),
                   pl.BlockSpec(bias.shape, lambda s: (0, 0)), pl.BlockSpec((8, sinks.shape[1]), lambda s: (0, 0))],
        out_shape=[S((T, dq + 2 * dkv), _MXU_DTYPE), S(bias.shape, F32), S((8, sinks.shape[1]), F32)],
        scratch_shapes=[pltpu.VMEM((blk, 2 * dkv), F32)],
        compiler_params=_params("arbitrary"))(qkv, qkv, qkv, ob, do, bias, sinks)


def _t5_onehot():
    i = jnp.arange(SWA_BLOCK)
    j = jnp.arange(2 * SWA_BLOCK)
    n = jnp.maximum(i[:, None] + SWA_BLOCK - j[None, :], 0)
    max_exact = REL_BUCKETS // 2
    nf = jnp.maximum(n, 1).astype(F32)
    large = max_exact + (jnp.log(nf / max_exact) / math.log(REL_MAX_DIST / max_exact)
                         * (REL_BUCKETS - max_exact)).astype(jnp.int32)
    large = jnp.minimum(large, REL_BUCKETS - 1)
    bucket = jnp.where(n < max_exact, n, large).reshape(-1)
    return (bucket[None, :] == jnp.arange(REL_BUCKETS)[:, None]).astype(F32)


def _loss_head(y, target, *, tm=1024):
    T, D = y.shape
    tm = min(tm, T)

    def body(y_ref, t_ref, loss_ref, dy_ref):
        @pl.when(pl.program_id(0) == 0)
        def _():
            loss_ref[...] = jnp.zeros_like(loss_ref)

        d = y_ref[...] - t_ref[...]
        dy_ref[...] = d * (1.0 / D)
        rs = jnp.sum(d * d, axis=1, keepdims=True)
        loss_ref[...] += (0.5 / D) * jnp.sum(rs, axis=0, keepdims=True)

    rows = pl.BlockSpec((tm, D), lambda i: (i, 0))
    return pl.pallas_call(
        body, name="loss_head", grid=(T // tm,), in_specs=[rows, rows],
        out_specs=[pl.BlockSpec((1, 1), lambda i: (0, 0)), rows], out_shape=[S((1, 1), F32), S((T, D), F32)],
        compiler_params=_params("arbitrary"))(y, target)


def _exchange_copies(ins, outs, n_gather, send_sems, recv_sems, loc_sems):
    mx, my, mc = lax.axis_index("x"), lax.axis_index("y"), lax.axis_index("c")
    me = 4 * mx + 2 * my + mc
    copies = []
    for a in range(len(ins)):
        src = ins[a] if a < n_gather else ins[a].at[me]
        copies.append(pltpu.make_async_copy(src, outs[a].at[me], loc_sems.at[a]))
    for k in range(1, N_DEV):
        px, py, pc = mx ^ ((k >> 2) & 1), my ^ ((k >> 1) & 1), mc ^ (k & 1)
        peer = 4 * px + 2 * py + pc
        for a in range(len(ins)):
            src = ins[a] if a < n_gather else ins[a].at[peer]
            copies.append(pltpu.make_async_remote_copy(
                src_ref=src, dst_ref=outs[a].at[me], send_sem=send_sems.at[a, k - 1],
                recv_sem=recv_sems.at[a, k - 1], device_id=(px, py, pc), device_id_type=pl.DeviceIdType.MESH))
    return copies


def _exchange_shapes(gather, scatter):
    n_arr = len(gather) + len(scatter)
    out_shape = [S((N_DEV,) + tuple(g.shape), g.dtype) for g in gather] + [S(s.shape, s.dtype) for s in scatter]
    sems = [pltpu.SemaphoreType.DMA((n_arr, N_DEV - 1)), pltpu.SemaphoreType.DMA((n_arr, N_DEV - 1)),
            pltpu.SemaphoreType.DMA((n_arr,))]
    return out_shape, sems


def _exchange(gather, scatter, *, name):
    n_g = len(gather)
    n_arr = n_g + len(scatter)

    def body(*refs):
        copies = _exchange_copies(refs[:n_arr], refs[n_arr:2 * n_arr], n_g, *refs[2 * n_arr:])
        for cp in copies:
            cp.start()
        for cp in copies:
            cp.wait()

    hbm = pl.BlockSpec(memory_space=pl.ANY)
    out_shape, sems = _exchange_shapes(gather, scatter)
    return pl.pallas_call(
        body, name=name, in_specs=[hbm] * n_arr, out_specs=[hbm] * n_arr, out_shape=out_shape,
        scratch_shapes=sems)(*gather, *scatter)


def _adamw(parts, w, m, v, *, name, tr=256):
    R, C = w.shape
    tr = min(tr, R)
    assert R % tr == 0

    def body(p_ref, w_ref, m_ref, v_ref, g_ref, d_ref, nm_ref, nv_ref):
        g = p_ref[0]
        for k in range(1, N_DEV):
            g = g + p_ref[k]
        g_ref[...] = g
        m_new = ADAM_B1 * m_ref[...] + (1.0 - ADAM_B1) * g
        v_new = ADAM_B2 * v_ref[...] + (1.0 - ADAM_B2) * (g * g)
        m_hat = m_new / (1.0 - ADAM_B1 ** ADAM_STEP)
        v_hat = v_new / (1.0 - ADAM_B2 ** ADAM_STEP)
        d_ref[...] = -ADAM_LR * (m_hat / (jnp.sqrt(v_hat) + ADAM_EPS) + ADAM_WD * w_ref[...])
        nm_ref[...] = m_new
        nv_ref[...] = v_new

    rows = pl.BlockSpec((tr, C), lambda i: (i, 0))
    return pl.pallas_call(
        body, name=name, grid=(R // tr,),
        in_specs=[pl.BlockSpec((N_DEV, tr, C), lambda i: (0, i, 0)), rows, rows, rows],
        out_specs=[rows] * 4, out_shape=[S((R, C), F32)] * 4,
        compiler_params=_params("parallel"))(parts, w, m, v)


def _rows_of(n):
    return -(-n // LANES)


def _pack(pieces, total_rows, dtype, lead=()):
    out = []
    for p in pieces:
        flat = p.reshape(lead + (-1,)).astype(dtype)
        n = flat.shape[-1]
        pad = _rows_of(n) * LANES - n
        if pad:
            flat = jnp.pad(flat, [(0, 0)] * len(lead) + [(0, pad)])
        out.append(flat.reshape(lead + (-1, LANES)))
    used = sum(o.shape[-2] for o in out)
    if total_rows > used:
        out.append(jnp.zeros(lead + (total_rows - used, LANES), dtype))
    return jnp.concatenate(out, axis=len(lead))


def _unpack(buf, shapes, lead=()):
    res, r0 = [], 0
    for shp in shapes:
        n = int(np.prod(shp))
        nr = _rows_of(n)
        piece = lax.slice_in_dim(buf, r0, r0 + nr, axis=len(lead)).reshape(lead + (nr * LANES,))
        res.append(lax.slice_in_dim(piece, 0, n, axis=len(lead)).reshape(lead + tuple(shp)))
        r0 += nr
    return res


def _round_up(n, m):
    return -(-n // m) * m


BIG = ["mla_w_in", "mla_w_uq", "mla_w_uk", "mla_w_uv", "mla_w_o", "kv_w_shared", "swa_w_q", "swa_w_o",
       "mlp_w_up", "mlp_w_down"]
GAINS = ["mla_g_q", "mla_g_kv"]
SHARDED = BIG + GAINS
REPL = ["swa_sinks", "rel_bias", "ln_mix_g", "ln_mix_b", "ln_mlp_g", "ln_mlp_b"]
WEIGHTS = ["mla_w_in", "mla_g_q", "mla_g_kv", "mla_w_uq", "mla_w_uk", "mla_w_uv", "mla_w_o", "kv_w_shared",
           "swa_w_q", "swa_sinks", "swa_w_o", "rel_bias", "mlp_w_up", "mlp_w_down", "ln_mix_g", "ln_mix_b",
           "ln_mlp_g", "ln_mlp_b"]


def kernel(x, mla_w_in, mla_g_q, mla_g_kv, mla_w_uq, mla_w_uk, mla_w_uv, mla_w_o, kv_w_shared, swa_w_q, swa_sinks, swa_w_o, rel_bias, mlp_w_up, mlp_w_down, ln_mix_g, ln_mix_b, ln_mlp_g, ln_mlp_b, loss_target, m_mla_w_in, m_mla_g_q, m_mla_g_kv, m_mla_w_uq, m_mla_w_uk, m_mla_w_uv, m_mla_w_o, m_kv_w_shared, m_swa_w_q, m_swa_sinks, m_swa_w_o, m_rel_bias, m_mlp_w_up, m_mlp_w_down, m_ln_mix_g, m_ln_mix_b, m_ln_mlp_g, m_ln_mlp_b, v_mla_w_in, v_mla_g_q, v_mla_g_kv, v_mla_w_uq, v_mla_w_uk, v_mla_w_uv, v_mla_w_o, v_kv_w_shared, v_swa_w_q, v_swa_sinks, v_swa_w_o, v_rel_bias, v_mlp_w_up, v_mlp_w_down, v_ln_mix_g, v_ln_mix_b, v_ln_mlp_g, v_ln_mlp_b):
    args = dict(locals())
    W = {n: args[n] for n in WEIGHTS}
    M = {n: args["m_" + n] for n in WEIGHTS}
    V = {n: args["v_" + n] for n in WEIGHTS}
    T = x.shape[1]
    x2d = x.reshape(T, D_MODEL)
    tgt = loss_target.reshape(T, D_MODEL)
    H = MLA_HEADS

    SH = {"mla_w_in": (-1, mla_w_in.shape[-1]), "mla_w_uq": (-1, H * (MLA_NOPE + MLA_ROPE)),
          "mla_w_uk": (-1, H * MLA_NOPE), "mla_w_uv": (-1, H * MLA_V), "mla_w_o": (-1, D_MODEL),
          "kv_w_shared": (-1, kv_w_shared.shape[-1]), "swa_w_q": (-1, swa_w_q.shape[-1]), "swa_w_o": (-1, D_MODEL)}
    slab = lambda d, n: d[n].reshape(SH[n])
    bf = lambda a: a.astype(_MXU_DTYPE)
    gains_slab = lambda d: jnp.pad(jnp.concatenate([d["mla_g_q"], d["mla_g_kv"]], axis=1),
                                   ((0, 7), (0, 128 - d["mla_g_q"].shape[1] - d["mla_g_kv"].shape[1])))
    n_gq, n_gkv = mla_g_q.shape[1], mla_g_kv.shape[1]
    w_in_s, w_uq_s, w_uk_s, gains_all = _exchange(
        [bf(slab(W, "mla_w_in")), bf(slab(W, "mla_w_uq")), bf(slab(W, "mla_w_uk")), gains_slab(W)], [],
        name="gather_mla_in")
    later = [bf(slab(W, n)) for n in ("mla_w_uv", "mla_w_o", "kv_w_shared", "swa_w_q", "swa_w_o")]
    later += [bf(mlp_w_up), bf(mlp_w_down)]
    w_in = w_in_s.reshape(D_MODEL, -1)
    g_q = gains_all[:, 0, :n_gq].reshape(1, MLA_QR)
    g_kv = gains_all[:, 0, n_gq:n_gq + n_gkv].reshape(1, MLA_C)
    w_uq = w_uq_s.reshape(MLA_QR, H, MLA_NOPE + MLA_ROPE)
    w_uq_n = w_uq[:, :, :MLA_NOPE].reshape(MLA_QR, H * MLA_NOPE)
    w_uq_r = w_uq[:, :, MLA_NOPE:].reshape(MLA_QR, H * MLA_ROPE)
    w_uk = w_uk_s.reshape(MLA_C, H, MLA_NOPE).transpose(1, 0, 2)
    w_uk_t = w_uk.transpose(0, 2, 1)
    ln = lambda a, l: a[l].reshape(1, D_MODEL)

    half = MLA_ROPE // 2
    inv = ROPE_THETA ** (-jnp.arange(half, dtype=F32) / half)
    ang = jnp.arange(T, dtype=F32)[:, None] * inv[None, :]
    cos = jnp.tile(jnp.concatenate([jnp.cos(ang), jnp.cos(ang)], -1), (1, H))
    sin = jnp.tile(jnp.concatenate([-jnp.sin(ang), jnp.sin(ang)], -1), (1, H))

    h, kc, kct, qs, cq, qn = _mla_pre_fwd(x2d, w_in, g_q, g_kv, w_uq_n, w_uq_r, w_uk_t, cos, sin)
    olat, lse, (w_uv_s, w_o_s, w_kv_s, w_q_s, w_o2_s, w_up, w_dn) = _mla_attn_fwd(qs, kc, kct, gather=later)
    w_uv = w_uv_s.reshape(MLA_C, H, MLA_V).transpose(1, 0, 2)
    w_o = w_o_s.reshape(H * MLA_V, D_MODEL)
    w_qkv = jnp.concatenate([w_q_s.reshape(D_MODEL, -1), w_kv_s.reshape(D_MODEL, -1)], axis=1)
    w_o2 = w_o2_s.reshape(SWA_QH * SWA_D, D_MODEL)
    o_mla = _mla_uv_fwd(olat, w_uv)
    x1, x1b, xh1, rs1 = _proj_ln_fwd(o_mla, w_o, x2d, ln(ln_mix_g, 0), ln(ln_mix_b, 0), name="mla_out_ln_fwd")
    u0, x2, x2b, xh2, rs2 = _mlp_fwd(x1b, x1, w_up, w_dn, 0, ln(ln_mlp_g, 0), ln(ln_mlp_b, 0))
    onehot = _t5_onehot()
    bias = _mm(rel_bias.T, onehot, name="rel_bias_expand", precision=lax.Precision.HIGHEST, tn=8192).reshape(
        SWA_QH * SWA_BLOCK, 2 * SWA_BLOCK).T
    sink_rows = jnp.repeat(swa_sinks.reshape(SWA_QH), SWA_BLOCK).reshape(1, SWA_QH * SWA_BLOCK)
    qkv = _mm(x2b, w_qkv, name="swa_qkv_fwd", out_dtype=_MXU_DTYPE, tm=1024, tn=512, tk=1024)
    o_swa = _swa_attn_fwd(qkv, bias, sink_rows)
    x3, x3b, xh3, rs3 = _proj_ln_fwd(o_swa, w_o2, x2, ln(ln_mix_g, 1), ln(ln_mix_b, 1), name="swa_out_ln_fwd")
    u1, x4, _, xh4, rs4 = _mlp_fwd(x3b, x3, w_up, w_dn, 1, ln(ln_mlp_g, 1), ln(ln_mlp_b, 1))
    loss_part, dx4 = _loss_head(x4, tgt)
    loss = lax.psum(loss_part[0, 0], AXES)

    nj = w_up.shape[0]
    dx3, du1, dy4b, dg_mlp1, db_mlp1 = _mlp_bwd_dx(dx4, xh4, rs4, ln(ln_mlp_g, 1), u1, w_up, w_dn, 1)
    g_dn1, g_up1 = _mlp_bwd_dw(u1, dy4b, x3b, du1, 1, nj=nj)
    dres3, do_swa, g_o2, dg_mix1, db_mix1 = _proj_ln_bwd(dx3, xh3, rs3, ln(ln_mix_g, 1), o_swa, w_o2,
                                                         name="swa_out_ln_bwd")
    dqkv, dbias, dsink = _swa_attn_bwd(qkv, o_swa, do_swa, bias, sink_rows)
    g_rel = _mm(onehot, dbias.T.reshape(SWA_QH, -1), name="rel_bias_grad", tb=True, precision=lax.Precision.HIGHEST,
                tk=8192)
    head_of_row = (jnp.arange(SWA_QH * SWA_BLOCK)[:, None] // SWA_BLOCK == jnp.arange(SWA_QH)[None, :]).astype(F32)
    g_sinks = _mm(dsink, head_of_row, name="sinks_grad", precision=lax.Precision.HIGHEST, tk=2048)[0:1]
    dx2 = _mm(dqkv, w_qkv, name="swa_qkv_bwd_dx", tb=True, add=dres3, tm=1024, tn=1024, tk=512)
    g_qkv = _mm(x2b, dqkv, name="swa_qkv_bwd_dw", ta=True, tm=1024, tn=512, tk=1024)
    dx1, du0, dy2b, dg_mlp0, db_mlp0 = _mlp_bwd_dx(dx2, xh2, rs2, ln(ln_mlp_g, 0), u0, w_up, w_dn, 0)
    g_dn0, g_up0 = _mlp_bwd_dw(u0, dy2b, x1b, du0, 0, nj=nj)
    dres1, do_mla, g_o, dg_mix0, db_mix0 = _proj_ln_bwd(dx1, xh1, rs1, ln(ln_mix_g, 0), o_mla, w_o,
                                                        name="mla_out_ln_bwd")
    dol, delta, g_uv = _mla_uv_bwd(do_mla, olat, w_uv)
    repl_grads = {
        "swa_sinks": g_sinks, "rel_bias": g_rel,
        "ln_mix_g": jnp.concatenate([dg_mix0, dg_mix1], 0), "ln_mix_b": jnp.concatenate([db_mix0, db_mix1], 0),
        "ln_mlp_g": jnp.concatenate([dg_mlp0, dg_mlp1], 0), "ln_mlp_b": jnp.concatenate([db_mlp0, db_mlp1], 0),
    }
    repl_rows = _round_up(sum(_rows_of(W[n].size) for n in REPL), 8)
    r_part = _pack([repl_grads[n] for n in REPL], repl_rows, F32)
    by_dev = lambda g: g.reshape((N_DEV, g.shape[0] // N_DEV) + g.shape[1:])
    early = [by_dev(g_o2), by_dev(g_qkv), g_up0, g_up1, g_dn0, g_dn1, by_dev(g_o),
             by_dev(g_uv.transpose(1, 0, 2).reshape(MLA_C, H * MLA_V))]
    dqs, dkc, dv, (r_all, p_o2, p_qkv, p_up0, p_up1, p_dn0, p_dn1, p_o, p_uv) = _mla_attn_bwd(
        qs, kc, kct, dol, lse, delta, gather=[r_part], scatter=early)
    grad_x, g_in, g_uq_n, g_uq_r, g_uk, g_gq, g_gkv = _mla_pre_bwd(
        dqs, dkc, dv, h, x2d, dres1, cq, qn, cos, sin, w_in, g_q, g_kv, w_uq_n, w_uq_r, w_uk)
    g_uq = jnp.concatenate([g_uq_n.reshape(MLA_QR, H, MLA_NOPE), g_uq_r.reshape(MLA_QR, H, MLA_ROPE)], -1)
    g_gains = jnp.pad(jnp.concatenate([g_gq.reshape(N_DEV, n_gq), g_gkv.reshape(N_DEV, n_gkv)], axis=1)[:, None, :],
                      ((0, 0), (0, 7), (0, 128 - n_gq - n_gkv)))
    p_in, p_uq, p_uk, p_gains = _exchange(
        [], [by_dev(g_in), by_dev(g_uq.reshape(MLA_QR, -1)), by_dev(g_uk.transpose(1, 0, 2).reshape(MLA_C, -1)),
             g_gains], name="exchange_mla_in_grads")

    res = {}

    def adam(name, parts, names, to_slab, from_slab):
        out = _adamw(parts, to_slab(W), to_slab(M), to_slab(V), name="adamw_" + name)
        for k in range(4):
            for n, a in zip(names, from_slab(out[k])):
                res[(k, n)] = a.reshape(W[n].shape)

    one = lambda n: (lambda d: slab(d, n))
    adam("swa_w_o", p_o2, ["swa_w_o"], one("swa_w_o"), lambda s: [s])
    dq_cols = SWA_QH * SWA_D
    adam("swa_qkv", p_qkv, ["swa_w_q", "kv_w_shared"],
         lambda d: jnp.concatenate([slab(d, "swa_w_q"), slab(d, "kv_w_shared")], axis=1),
         lambda s: [s[:, :dq_cols], s[:, dq_cols:]])
    for name, parts in (("mlp_w_up", (p_up0, p_up1)), ("mlp_w_down", (p_dn0, p_dn1))):
        per_layer = [_adamw(parts[l], W[name][l], M[name][l], V[name][l], name=f"adamw_{name}_{l}") for l in range(DEPTH)]
        for k in range(4):
            res[(k, name)] = jnp.stack([per_layer[l][k] for l in range(DEPTH)], axis=0)
    adam("mla_w_o", p_o, ["mla_w_o"], one("mla_w_o"), lambda s: [s])
    adam("mla_w_uv", p_uv, ["mla_w_uv"], one("mla_w_uv"), lambda s: [s])
    adam("mla_w_in", p_in, ["mla_w_in"], one("mla_w_in"), lambda s: [s])
    adam("mla_w_uq", p_uq, ["mla_w_uq"], one("mla_w_uq"), lambda s: [s])
    adam("mla_w_uk", p_uk, ["mla_w_uk"], one("mla_w_uk"), lambda s: [s])
    adam("mla_gains", p_gains, ["mla_g_q", "mla_g_kv"], gains_slab,
         lambda s: [s[0:1, :n_gq], s[0:1, n_gq:n_gq + n_gkv]])
    adam("replicated", r_all, REPL, lambda d: _pack([d[n] for n in REPL], repl_rows, F32),
         lambda s: _unpack(s, [W[n].shape for n in REPL]))
    return (loss, grad_x.reshape(x.shape), *[res[(k, n)] for k in range(4) for n in WEIGHTS])
```

```python
import functools
import math

import numpy as np
import jax
import jax.numpy as jnp
from jax import lax
from jax.experimental import pallas as pl
from jax.experimental.pallas import tpu as pltpu

F32 = jnp.float32
_MXU_DTYPE = jnp.bfloat16

D_MODEL = 1024
DEPTH = 2
MLA_HEADS = 8
MLA_NOPE = 128
MLA_ROPE = 64
MLA_V = 128
MLA_QR = 384
MLA_C = 256
MLA_DK = 384
MLA_DT = MLA_C + MLA_ROPE
ROPE_THETA = 10000.0
SWA_QH = 16
SWA_KVH = 4
SWA_D = 64
SWA_BLOCK = 128
REL_BUCKETS = 32
REL_MAX_DIST = 128
D_FF = 4096
LN_EPS = 1e-5
RMS_EPS = 1e-6
ALPHA = (2 * DEPTH) ** 0.25
ADAM_LR, ADAM_B1, ADAM_B2, ADAM_EPS, ADAM_WD, ADAM_STEP = 0.001, 0.9, 0.999, 1e-08, 0.01, 10

N_DEV = 8
AXES = ("x", "y", "c")
V7X_VMEM_BYTES = 64 * 1024 * 1024
VMEM_LIMIT = V7X_VMEM_BYTES - 8 * 1024 * 1024
LANES = 1024
ATT_TQ = 512
ATT_TK = 512
ATT_HEAD_GROUP = 1

NT = (((1,), (1,)), ((), ()))
TN = (((0,), (0,)), ((), ()))
S = jax.ShapeDtypeStruct


def _params(*sem, vmem=VMEM_LIMIT):
    return pltpu.CompilerParams(dimension_semantics=sem, vmem_limit_bytes=vmem)


def _dot(a, b, dims=None, precision=None):
    if dims is None:
        return jnp.dot(a, b, preferred_element_type=F32, precision=precision)
    return lax.dot_general(a, b, dims, preferred_element_type=F32, precision=precision)


def _mx(v):
    return v.astype(_MXU_DTYPE)


def _swap_halves_64(v):
    return jnp.concatenate([v[:, 32:], v[:, :32]], axis=-1)


def _swap_halves_groups(v):
    n = v.shape[-1]
    lane = lax.broadcasted_iota(jnp.int32, v.shape, 1)
    return jnp.where(lane % 64 < 32, pltpu.roll(v, n - 32, 1), pltpu.roll(v, 32, 1))


def _mm(a, b, *, name, ta=False, tb=False, add=None, out_dtype=F32, tm=512, tn=512, tk=512, precision=None):
    M, K = (a.shape[1], a.shape[0]) if ta else a.shape
    N = b.shape[0] if tb else b.shape[1]
    tm, tn, tk = min(tm, M), min(tn, N), min(tk, K)
    assert M % tm == 0 and N % tn == 0 and K % tk == 0, (M, N, K, tm, tn, tk)
    nk = K // tk
    dims = (((0 if ta else 1,), (1 if tb else 0,)), ((), ()))
    has_add = add is not None

    def body(*refs):
        if has_add:
            a_ref, b_ref, add_ref, o_ref, acc = refs
        else:
            a_ref, b_ref, o_ref, acc = refs
        k = pl.program_id(2)

        @pl.when(k == 0)
        def _():
            acc[...] = jnp.zeros_like(acc)

        av, bv = a_ref[...], b_ref[...]
        if precision is None:
            av, bv = _mx(av), _mx(bv)
        acc[...] += _dot(av, bv, dims, precision)

        @pl.when(k == nk - 1)
        def _():
            r = acc[...]
            if has_add:
                r = r + add_ref[...]
            o_ref[...] = r.astype(out_dtype)

    a_spec = pl.BlockSpec((tk, tm), lambda i, j, k: (k, i)) if ta else pl.BlockSpec((tm, tk), lambda i, j, k: (i, k))
    b_spec = pl.BlockSpec((tn, tk), lambda i, j, k: (j, k)) if tb else pl.BlockSpec((tk, tn), lambda i, j, k: (k, j))
    in_specs = [a_spec, b_spec]
    args = [a, b]
    if has_add:
        in_specs.append(pl.BlockSpec((tm, tn), lambda i, j, k: (i, j)))
        args.append(add)
    return pl.pallas_call(
        body, name=name, grid=(M // tm, N // tn, nk), in_specs=in_specs,
        out_specs=pl.BlockSpec((tm, tn), lambda i, j, k: (i, j)), out_shape=S((M, N), out_dtype),
        scratch_shapes=[pltpu.VMEM((tm, tn), F32)],
        compiler_params=_params("parallel", "parallel", "arbitrary"))(*args)


def _ln_fwd_math(z, g, b):
    mu = jnp.mean(z, axis=-1, keepdims=True)
    zc = z - mu
    var = jnp.mean(zc * zc, axis=-1, keepdims=True)
    rstd = lax.rsqrt(var + LN_EPS)
    xhat = zc * rstd
    return xhat * g + b, xhat, rstd


def _ln_bwd_math(dxo, xhat, rstd, g):
    dxh = dxo * g
    m1 = jnp.mean(dxh, axis=-1, keepdims=True)
    m2 = jnp.mean(dxh * xhat, axis=-1, keepdims=True)
    dz = rstd * (dxh - m1 - xhat * m2)
    dg = jnp.sum(dxo * xhat, axis=0, keepdims=True)
    db = jnp.sum(dxo, axis=0, keepdims=True)
    return dz, dg, db


def _rms_fwd_math(xr, g):
    r = lax.rsqrt(jnp.mean(xr * xr, axis=-1, keepdims=True) + RMS_EPS)
    return xr * r * g


def _rms_bwd_math(dy, xr, g):
    r = lax.rsqrt(jnp.mean(xr * xr, axis=-1, keepdims=True) + RMS_EPS)
    gy = dy * g
    dx = r * gy - xr * (r * r * r) * jnp.mean(gy * xr, axis=-1, keepdims=True)
    dg = jnp.sum(dy * xr * r, axis=0, keepdims=True)
    return dx, dg


def _mla_pre_fwd(x, w_in, g_q, g_kv, w_uq_n, w_uq_r, w_uk_t, cos, sin):
    T = x.shape[0]
    tm = min(ATT_TQ, T)
    nq = T // tm
    H = MLA_HEADS

    tk = min(ATT_TK, T)

    def body(x_ref, win_ref, gq_ref, gkv_ref, wn_ref, wr_ref, wuk_ref, cos_ref, sin_ref,
             h_ref, kc_ref, kct_ref, qs_ref, qst_ref, cq_ref, qn_ref):
        h = _dot(_mx(x_ref[...]), win_ref[...])
        h_ref[...] = h
        cos_v, sin_v = cos_ref[...], sin_ref[...]
        cq = _mx(_rms_fwd_math(h[:, :MLA_QR], gq_ref[...]))
        ckv = _rms_fwd_math(h[:, MLA_QR:MLA_QR + MLA_C], gkv_ref[...])
        krr = h[:, MLA_QR + MLA_C:]
        kr = krr * cos_v[:, :MLA_ROPE] + _swap_halves_64(krr) * sin_v[:, :MLA_ROPE]
        kr_pad = jnp.concatenate([kr, jnp.zeros((tm, MLA_DK - MLA_C - MLA_ROPE), F32)], axis=1)
        kc_ref[:, 0:MLA_C] = _mx(ckv)
        kc_ref[:, MLA_C:] = _mx(kr_pad)
        kct_ref[0:MLA_C, :] = _mx(ckv.T)
        kct_ref[MLA_C:, :] = _mx(kr_pad.T[0:MLA_ROPE, :])
        cq_ref[...] = cq
        qnb = _mx(_dot(cq, wn_ref[...]))
        qn_ref[...] = qnb
        qr = _dot(cq, wr_ref[...])
        qrr = qr * cos_v + _swap_halves_groups(qr) * sin_v
        qrr_t = qrr.T
        for hd in range(H):
            ql = _dot(qnb[:, MLA_NOPE * hd:MLA_NOPE * (hd + 1)], wuk_ref[hd])
            qst_ref[0, 0:MLA_C, tm * hd:tm * (hd + 1)] = _mx(ql.T)
            qst_ref[0, MLA_C:, tm * hd:tm * (hd + 1)] = _mx(qrr_t[MLA_ROPE * hd:MLA_ROPE * (hd + 1), :])
            qs_ref[0, hd, :, 0:MLA_C] = _mx(ql)
            qs_ref[0, hd, :, MLA_C:MLA_C + MLA_ROPE] = _mx(qrr[:, MLA_ROPE * hd:MLA_ROPE * (hd + 1)])
            qs_ref[0, hd, :, MLA_C + MLA_ROPE:] = jnp.zeros((tm, MLA_DK - MLA_C - MLA_ROPE), _MXU_DTYPE)

    full = lambda shp: pl.BlockSpec(shp, lambda i: (0,) * len(shp))
    rows = lambda n: pl.BlockSpec((tm, n), lambda i: (i, 0))
    n_in = w_in.shape[1]
    return pl.pallas_call(
        body, name="mla_pre_fwd", grid=(nq,),
        in_specs=[rows(D_MODEL), full(w_in.shape), full(g_q.shape), full(g_kv.shape), full(w_uq_n.shape),
                  full(w_uq_r.shape), full(w_uk_t.shape), rows(H * MLA_ROPE), rows(H * MLA_ROPE)],
        out_specs=[rows(n_in), rows(MLA_DK),
                   pl.BlockSpec((None, MLA_DT, tm), lambda i: (i * tm // tk, 0, i % (tk // tm))),
                   pl.BlockSpec((1, H, tm, MLA_DK), lambda i: (i, 0, 0, 0)),
                   pl.BlockSpec((1, MLA_DT, H * tm), lambda i: (i, 0, 0)), rows(MLA_QR), rows(H * MLA_NOPE)],
        out_shape=[S((T, n_in), F32), S((T, MLA_DK), _MXU_DTYPE), S((T // tk, MLA_DT, tk), _MXU_DTYPE),
                   S((nq, H, tm, MLA_DK), _MXU_DTYPE), S((nq, MLA_DT, H * tm), _MXU_DTYPE),
                   S((T, MLA_QR), _MXU_DTYPE), S((T, H * MLA_NOPE), _MXU_DTYPE)],
        compiler_params=_params("parallel"))(x, w_in, g_q, g_kv, w_uq_n, w_uq_r, w_uk_t, cos, sin)


def _att_steps(T, tq, tk):
    qi, kj = [], []
    for i in range(T // tq):
        for j in range((i * tq + tq - 1) // tk + 1):
            qi.append(i)
            kj.append(j)
    return jnp.asarray(np.array(qi, np.int32)), jnp.asarray(np.array(kj, np.int32))


def _ride_exchange(st, n_steps, ins, outs, n_gather, sems):
    if not ins:
        return

    @pl.when(st == 0)
    def _():
        for cp in _exchange_copies(ins, outs, n_gather, *sems):
            cp.start()

    @pl.when(st == n_steps - 1)
    def _():
        for cp in _exchange_copies(ins, outs, n_gather, *sems):
            cp.wait()


def _mla_attn_fwd(qs, kc, kct, gather=(), scatter=()):
    nq, H, tq, DK = qs.shape
    T = kc.shape[0]
    tk = min(ATT_TK, T)
    scale = (MLA_NOPE + MLA_ROPE) ** -0.5
    c2 = scale * math.log2(math.e)
    qi, kj = _att_steps(T, tq, tk)
    n_steps = int(qi.shape[0])
    hg = ATT_HEAD_GROUP
    R = hg * tq
    n_x = len(gather) + len(scatter)

    def body(qi_ref, kj_ref, q_ref, k_ref, kt_ref, *rest):
        x_ins, (o_ref, lse_ref), x_outs = rest[:n_x], rest[n_x:n_x + 2], rest[n_x + 2:2 * n_x + 2]
        m_sc, l_sc, acc_sc = rest[2 * n_x + 2:2 * n_x + 5]
        st = pl.program_id(0)
        _ride_exchange(st, n_steps, x_ins, x_outs, len(gather), rest[2 * n_x + 5:])
        i, j = qi_ref[st], kj_ref[st]
        j_last = (i * tq + tq - 1) // tk

        @pl.when(j == 0)
        def _():
            m_sc[...] = jnp.full_like(m_sc, -jnp.inf)
            l_sc[...] = jnp.zeros_like(l_sc)
            acc_sc[...] = jnp.zeros_like(acc_sc)

        def step(masked):
            k = k_ref[...]
            vt = kt_ref[0:MLA_C, :]
            if masked:
                key = lax.broadcasted_iota(jnp.int32, (tk, R), 0) + j * tk
                qry = lax.broadcasted_iota(jnp.int32, (tk, R), 1) % tq + i * tq
                causal = key <= qry
            n_g = H // hg
            qk = lambda g: _dot(k, q_ref[0, g * hg:(g + 1) * hg].reshape(R, DK), NT)
            def accumulate(g, a, pb):
                cs = slice(g * R, (g + 1) * R)
                acc_sc[:, cs] = a * acc_sc[:, cs] + _dot(vt, pb)

            s_next = qk(0)
            pending = None
            for g in range(n_g):
                cs = slice(g * R, (g + 1) * R)
                s = s_next
                if g + 1 < n_g:
                    s_next = qk(g + 1)
                if pending is not None:
                    accumulate(*pending)
                if masked:
                    s = jnp.where(causal, s, -jnp.inf)
                m_prev = m_sc[:, cs]
                m_new = jnp.maximum(m_prev, jnp.max(s, axis=0, keepdims=True))
                a = jnp.exp2((m_prev - m_new) * c2)
                p = jnp.exp2((s - m_new) * c2)
                l_sc[:, cs] = a * l_sc[:, cs] + jnp.sum(p, axis=0, keepdims=True)
                m_sc[:, cs] = m_new
                pending = (g, a, _mx(p))
            accumulate(*pending)

        pl.when(j == j_last)(lambda: step(True))
        pl.when(j != j_last)(lambda: step(False))

        @pl.when(j == j_last)
        def _():
            o_ref[0] = _mx(acc_sc[...] / l_sc[...])
            lse_ref[0] = m_sc[...] * scale + jnp.log(l_sc[...])

    hbm = pl.BlockSpec(memory_space=pl.ANY)
    x_shapes, x_sems = _exchange_shapes(gather, scatter) if n_x else ([], [])
    gs = pltpu.PrefetchScalarGridSpec(
        num_scalar_prefetch=2, grid=(n_steps,),
        in_specs=[pl.BlockSpec((1, H, tq, DK), lambda s, qi, kj: (qi[s], 0, 0, 0)),
                  pl.BlockSpec((tk, DK), lambda s, qi, kj: (kj[s], 0)),
                  pl.BlockSpec((None, MLA_DT, tk), lambda s, qi, kj: (kj[s], 0, 0))] + [hbm] * n_x,
        out_specs=[pl.BlockSpec((1, MLA_C, H * tq), lambda s, qi, kj: (qi[s], 0, 0)),
                   pl.BlockSpec((1, 1, H * tq), lambda s, qi, kj: (qi[s], 0, 0))] + [hbm] * n_x,
        scratch_shapes=[pltpu.VMEM((1, H * tq), F32), pltpu.VMEM((1, H * tq), F32),
                        pltpu.VMEM((MLA_C, H * tq), F32)] + x_sems)
    res = pl.pallas_call(
        body, name="mla_attn_fwd", grid_spec=gs,
        out_shape=[S((nq, MLA_C, H * tq), _MXU_DTYPE), S((nq, 1, H * tq), F32)] + x_shapes,
        compiler_params=_params("arbitrary"))(qi, kj, qs, kc, kct, *gather, *scatter)
    return res[0], res[1], res[2:]


def _mla_attn_bwd(qs, qst, kc, kct, dol, lse, delta, gather=(), scatter=()):
    nq, H, tq, DK = qs.shape
    T = kc.shape[0]
    tk = min(ATT_TK, T)
    scale = (MLA_NOPE + MLA_ROPE) ** -0.5
    log2e = math.log2(math.e)
    qi, kj = _att_steps(T, tq, tk)
    n_steps = int(qi.shape[0])
    hg = ATT_HEAD_GROUP
    R = hg * tq
    n_x = len(gather) + len(scatter)

    def body(qi_ref, kj_ref, q_ref, qt_ref, k_ref, kt_ref, do_ref, lse_ref, dl_ref, *rest):
        x_ins, (dq_ref, dk_ref, dv_ref), x_outs = rest[:n_x], rest[n_x:n_x + 3], rest[n_x + 3:2 * n_x + 3]
        dk_acc, dv_acc, sem = rest[2 * n_x + 3:2 * n_x + 6]
        st = pl.program_id(0)
        _ride_exchange(st, n_steps, x_ins, x_outs, len(gather), rest[2 * n_x + 6:])
        i, j = qi_ref[st], kj_ref[st]
        j_last = (i * tq + tq - 1) // tk

        @pl.when(st == 0)
        def _():
            dk_acc[...] = jnp.zeros_like(dk_acc)
            dv_acc[...] = jnp.zeros_like(dv_acc)

        @pl.when(j == 0)
        def _():
            dq_ref[...] = jnp.zeros_like(dq_ref)

        def step(masked):
            k, kt = k_ref[...], kt_ref[...]
            v = k[:, :MLA_C]
            if masked:
                key = lax.broadcasted_iota(jnp.int32, (tk, R), 0) + j * tk
                qry = lax.broadcasted_iota(jnp.int32, (tk, R), 1) % tq + i * tq
                causal = key <= qry
            dkt_c = jnp.zeros((MLA_DT, tk), F32)
            dvt_c = jnp.zeros((MLA_C, tk), F32)
            n_g = H // hg

            def scores(g):
                q = q_ref[0, g * hg:(g + 1) * hg].reshape(R, DK)
                dot = do_ref[0, :, g * R:(g + 1) * R]
                return dot, _dot(k, q, NT), _dot(v, dot)

            nxt = scores(0)
            for g in range(n_g):
                cs = slice(g * R, (g + 1) * R)
                dot, s, dp = nxt
                if g + 1 < n_g:
                    nxt = scores(g + 1)
                p = jnp.exp2(s * (scale * log2e) - lse_ref[0, :, cs] * log2e)
                if masked:
                    p = jnp.where(causal, p, 0.0)
                dsb = _mx(p * (dp - dl_ref[0, :, cs]))
                dq_ref[0, :, cs] += _dot(kt, dsb)
                dkt_c = dkt_c + _dot(qt_ref[0, :, cs], dsb, NT)
                dvt_c = dvt_c + _dot(dot, _mx(p), NT)
            dk_acc[j] += dkt_c * scale
            dv_acc[j] += dvt_c

        pl.when(j == j_last)(lambda: step(True))
        pl.when(j != j_last)(lambda: step(False))

        @pl.when(j == j_last)
        def _():
            dq_ref[...] = dq_ref[...] * scale

        @pl.when(st == n_steps - 1)
        def _():
            c1 = pltpu.make_async_copy(dk_acc, dk_ref, sem.at[0])
            c2 = pltpu.make_async_copy(dv_acc, dv_ref, sem.at[1])
            c1.start()
            c2.start()
            c1.wait()
            c2.wait()

    cols = lambda n: pl.BlockSpec((1, n, H * tq), lambda s, qi, kj: (qi[s], 0, 0))
    hbm = pl.BlockSpec(memory_space=pl.ANY)
    x_shapes, x_sems = _exchange_shapes(gather, scatter) if n_x else ([], [])
    gs = pltpu.PrefetchScalarGridSpec(
        num_scalar_prefetch=2, grid=(n_steps,),
        in_specs=[pl.BlockSpec((1, H, tq, DK), lambda s, qi, kj: (qi[s], 0, 0, 0)), cols(MLA_DT),
                  pl.BlockSpec((tk, DK), lambda s, qi, kj: (kj[s], 0)),
                  pl.BlockSpec((None, MLA_DT, tk), lambda s, qi, kj: (kj[s], 0, 0)),
                  cols(MLA_C), cols(1), cols(1)] + [hbm] * n_x,
        out_specs=[cols(MLA_DT), hbm, hbm] + [hbm] * n_x,
        scratch_shapes=[pltpu.VMEM((T // tk, MLA_DT, tk), F32), pltpu.VMEM((T // tk, MLA_C, tk), F32),
                        pltpu.SemaphoreType.DMA((2,))] + x_sems)
    res = pl.pallas_call(
        body, name="mla_attn_bwd", grid_spec=gs,
        out_shape=[S((nq, MLA_DT, H * tq), F32), S((T // tk, MLA_DT, tk), F32),
                   S((T // tk, MLA_C, tk), F32)] + x_shapes,
        compiler_params=_params("arbitrary"))(qi, kj, qs, qst, kc, kct, dol, lse, delta, *gather, *scatter)
    return res[0], res[1], res[2], res[3:]


def _mla_uv_fwd(olat, w_uv):
    nq, C, cols = olat.shape
    H = w_uv.shape[0]
    tq = cols // H

    def body(ol_ref, wuv_ref, o_ref):
        for hd in range(H):
            o_ref[:, MLA_V * hd:MLA_V * (hd + 1)] = _mx(_dot(ol_ref[0, :, tq * hd:tq * (hd + 1)], wuv_ref[hd], TN))

    return pl.pallas_call(
        body, name="mla_uv_fwd", grid=(nq,),
        in_specs=[pl.BlockSpec((1, C, cols), lambda i: (i, 0, 0)), pl.BlockSpec(w_uv.shape, lambda i: (0, 0, 0))],
        out_specs=pl.BlockSpec((tq, H * MLA_V), lambda i: (i, 0)), out_shape=S((nq * tq, H * MLA_V), _MXU_DTYPE),
        compiler_params=_params("parallel"))(olat, w_uv)


def _mla_uv_bwd(do, olat, w_uv):
    nq, C, cols = olat.shape
    H = w_uv.shape[0]
    tq = cols // H

    def body(do_ref, ol_ref, wuv_ref, dol_ref, dl_ref, dw_ref):
        @pl.when(pl.program_id(0) == 0)
        def _():
            dw_ref[...] = jnp.zeros_like(dw_ref)

        dov = do_ref[...]
        for hd in range(H):
            cs = slice(tq * hd, tq * (hd + 1))
            doh = _mx(dov[:, MLA_V * hd:MLA_V * (hd + 1)])
            ol = ol_ref[0, :, cs]
            dol = _dot(wuv_ref[hd], doh, NT)
            dol_ref[0, :, cs] = _mx(dol)
            dl_ref[0, :, cs] = jnp.sum(dol * ol.astype(F32), axis=0, keepdims=True)
            dw_ref[hd] += _dot(ol, doh)

    blk = lambda n: pl.BlockSpec((1, n, cols), lambda i: (i, 0, 0))
    return pl.pallas_call(
        body, name="mla_uv_bwd", grid=(nq,),
        in_specs=[pl.BlockSpec((tq, H * MLA_V), lambda i: (i, 0)), blk(C), pl.BlockSpec(w_uv.shape, lambda i: (0, 0, 0))],
        out_specs=[blk(C), blk(1), pl.BlockSpec(w_uv.shape, lambda i: (0, 0, 0))],
        out_shape=[S(olat.shape, _MXU_DTYPE), S((nq, 1, cols), F32), S(w_uv.shape, F32)],
        compiler_params=_params("arbitrary"))(do, olat, w_uv)


def _mla_pre_bwd(dqs, dkc, dv, h, x, dres, cq, qn, cos, sin, w_in, g_q, g_kv, w_uq_n, w_uq_r, w_uk):
    nq, DK, cols = dqs.shape
    H = w_uk.shape[0]
    tm = cols // H
    T = nq * tm
    tk = dv.shape[2]
    n_in = w_in.shape[1]

    def body(dqs_ref, dkc_ref, dv_ref, h_ref, x_ref, dres_ref, cq_ref, qn_ref, cos_ref, sin_ref,
             win_ref, gq_ref, gkv_ref, wn_ref, wr_ref, wuk_ref,
             gx_ref, dwin_ref, dwn_ref, dwr_ref, dwuk_ref, dgq_ref, dgkv_ref, dqn_sc, dqr_sc, dh_sc):
        @pl.when(pl.program_id(0) == 0)
        def _():
            for r in (dwin_ref, dwn_ref, dwr_ref, dwuk_ref, dgq_ref, dgkv_ref):
                r[...] = jnp.zeros_like(r)

        cos_v, sin_v = cos_ref[...], sin_ref[...]
        qnb = qn_ref[...]
        for hd in range(H):
            cs = slice(tm * hd, tm * (hd + 1))
            dql = _mx(dqs_ref[0, 0:MLA_C, cs])
            dqn_sc[:, MLA_NOPE * hd:MLA_NOPE * (hd + 1)] = _dot(dql, wuk_ref[hd], TN)
            dwuk_ref[hd] += _dot(dql, qnb[:, MLA_NOPE * hd:MLA_NOPE * (hd + 1)])
            dqr_sc[MLA_ROPE * hd:MLA_ROPE * (hd + 1), :] = dqs_ref[0, MLA_C:MLA_C + MLA_ROPE, cs]
        dqr = dqr_sc[...].T
        dqrb = _mx(dqr * cos_v + _swap_halves_groups(dqr * sin_v))
        dqnb = _mx(dqn_sc[...])
        cq = cq_ref[...]
        dwn_ref[...] += _dot(cq, dqnb, TN)
        dwr_ref[...] += _dot(cq, dqrb, TN)
        dcq = _dot(dqnb, wn_ref[...], NT) + _dot(dqrb, wr_ref[...], NT)
        hv = h_ref[...]
        dxq, dgq = _rms_bwd_math(dcq, hv[:, :MLA_QR], gq_ref[...])
        dgq_ref[...] += dgq
        dckv = (dkc_ref[0:MLA_C, :] + dv_ref[...]).T
        dxkv, dgkv = _rms_bwd_math(dckv, hv[:, MLA_QR:MLA_QR + MLA_C], gkv_ref[...])
        dgkv_ref[...] += dgkv
        dkr = jnp.concatenate([dkc_ref[MLA_C:, :], jnp.zeros((128 - MLA_ROPE, tm), F32)], axis=0).T[:, :MLA_ROPE]
        dkr_raw = dkr * cos_v[:, :MLA_ROPE] + _swap_halves_64(dkr * sin_v[:, :MLA_ROPE])
        dh_sc[:, 0:MLA_QR] = dxq
        dh_sc[:, MLA_QR:MLA_QR + MLA_C] = dxkv
        dh_sc[:, MLA_QR + MLA_C:] = dkr_raw
        dhb = _mx(dh_sc[...])
        gx_ref[...] = dres_ref[...] + _dot(dhb, win_ref[...], NT)
        dwin_ref[...] += _dot(_mx(x_ref[...]), dhb, TN)

    full = lambda shp: pl.BlockSpec(shp, lambda i: (0,) * len(shp))
    rows = lambda n: pl.BlockSpec((tm, n), lambda i: (i, 0))
    return pl.pallas_call(
        body, name="mla_pre_bwd", grid=(nq,),
        in_specs=[pl.BlockSpec((1, DK, cols), lambda i: (i, 0, 0)),
                  pl.BlockSpec((None, DK, tm), lambda i: (i * tm // tk, 0, i % (tk // tm))),
                  pl.BlockSpec((None, MLA_C, tm), lambda i: (i * tm // tk, 0, i % (tk // tm))), rows(n_in),
                  rows(D_MODEL), rows(D_MODEL), rows(MLA_QR), rows(H * MLA_NOPE), rows(H * MLA_ROPE), rows(H * MLA_ROPE),
                  full(w_in.shape), full(g_q.shape), full(g_kv.shape), full(w_uq_n.shape), full(w_uq_r.shape),
                  full(w_uk.shape)],
        out_specs=[rows(D_MODEL), full(w_in.shape), full(w_uq_n.shape), full(w_uq_r.shape), full(w_uk.shape),
                   full(g_q.shape), full(g_kv.shape)],
        out_shape=[S((T, D_MODEL), F32), S(w_in.shape, F32), S(w_uq_n.shape, F32), S(w_uq_r.shape, F32),
                   S(w_uk.shape, F32), S(g_q.shape, F32), S(g_kv.shape, F32)],
        scratch_shapes=[pltpu.VMEM((tm, H * MLA_NOPE), F32), pltpu.VMEM((H * MLA_ROPE, tm), F32),
                        pltpu.VMEM((tm, n_in), F32)],
        compiler_params=_params("arbitrary"))(dqs, dkc, dv, h, x, dres, cq, qn, cos, sin, w_in, g_q, g_kv,
                                              w_uq_n, w_uq_r, w_uk)


def _proj_ln_fwd(a, w, xres, g, b, *, name, tm=512):
    T, K = a.shape
    tm = min(tm, T)

    def body(a_ref, w_ref, x_ref, g_ref, b_ref, xo_ref, xob_ref, xh_ref, rs_ref):
        z = ALPHA * x_ref[...] + _dot(a_ref[...], w_ref[...])
        xo, xhat, rstd = _ln_fwd_math(z, g_ref[...], b_ref[...])
        xo_ref[...] = xo
        xob_ref[...] = _mx(xo)
        xh_ref[...] = xhat
        rs_ref[...] = rstd

    rows = lambda n: pl.BlockSpec((tm, n), lambda i: (i, 0))
    full = lambda shp: pl.BlockSpec(shp, lambda i: (0,) * len(shp))
    return pl.pallas_call(
        body, name=name, grid=(T // tm,),
        in_specs=[rows(K), full(w.shape), rows(D_MODEL), full(g.shape), full(b.shape)],
        out_specs=[rows(D_MODEL), rows(D_MODEL), rows(D_MODEL), rows(1)],
        out_shape=[S((T, D_MODEL), F32), S((T, D_MODEL), _MXU_DTYPE), S((T, D_MODEL), F32), S((T, 1), F32)],
        compiler_params=_params("parallel"))(a, w, xres, g, b)


def _proj_ln_bwd(dxo, xhat, rstd, g, a, w, *, name, tm=512):
    T, K = a.shape
    tm = min(tm, T)

    def body(dxo_ref, xh_ref, rs_ref, g_ref, a_ref, w_ref, dres_ref, da_ref, dw_ref, dg_ref, db_ref):
        @pl.when(pl.program_id(0) == 0)
        def _():
            for r in (dw_ref, dg_ref, db_ref):
                r[...] = jnp.zeros_like(r)

        dz, dg, db = _ln_bwd_math(dxo_ref[...], xh_ref[...], rs_ref[...], g_ref[...])
        dg_ref[...] += dg
        db_ref[...] += db
        dres_ref[...] = ALPHA * dz
        dzb = _mx(dz)
        da_ref[...] = _dot(dzb, w_ref[...], NT)
        dw_ref[...] += _dot(a_ref[...], dzb, TN)

    rows = lambda n: pl.BlockSpec((tm, n), lambda i: (i, 0))
    full = lambda shp: pl.BlockSpec(shp, lambda i: (0,) * len(shp))
    return pl.pallas_call(
        body, name=name, grid=(T // tm,),
        in_specs=[rows(D_MODEL), rows(D_MODEL), rows(1), full(g.shape), rows(K), full(w.shape)],
        out_specs=[rows(D_MODEL), rows(K), full(w.shape), full(g.shape), full(g.shape)],
        out_shape=[S((T, D_MODEL), F32), S((T, K), F32), S(w.shape, F32), S(g.shape, F32), S(g.shape, F32)],
        compiler_params=_params("arbitrary"))(dxo, xhat, rstd, g, a, w)


def _mlp_fwd(xb, xres, w_up, w_dn, layer, g, b, *, tm=1024):
    T = xb.shape[0]
    tm = min(tm, T)
    nj, _, _, fc = w_up.shape

    def body(xb_ref, x_ref, wu_ref, wd_ref, g_ref, b_ref, u_ref, xo_ref, xob_ref, xh_ref, rs_ref, acc):
        j = pl.program_id(1)

        @pl.when(j == 0)
        def _():
            acc[...] = ALPHA * x_ref[...]

        u = _dot(xb_ref[...], wu_ref[...])
        u_ref[...] = _mx(u)
        r = jnp.maximum(u, 0.0)
        acc[...] += _dot(_mx(r * r), wd_ref[...])

        @pl.when(j == nj - 1)
        def _():
            xo, xhat, rstd = _ln_fwd_math(acc[...], g_ref[...], b_ref[...])
            xo_ref[...] = xo
            xob_ref[...] = _mx(xo)
            xh_ref[...] = xhat
            rs_ref[...] = rstd

    rows = lambda n: pl.BlockSpec((tm, n), lambda i, j: (i, 0))
    full = lambda shp: pl.BlockSpec(shp, lambda i, j: (0,) * len(shp))
    return pl.pallas_call(
        body, name=f"mlp_fwd_{layer}", grid=(T // tm, nj),
        in_specs=[rows(D_MODEL), rows(D_MODEL),
                  pl.BlockSpec((None, None, D_MODEL, fc), lambda i, j: (j, layer, 0, 0)),
                  pl.BlockSpec((None, None, fc, D_MODEL), lambda i, j: (j, layer, 0, 0)),
                  full(g.shape), full(b.shape)],
        out_specs=[pl.BlockSpec((tm, fc), lambda i, j: (i, j)), rows(D_MODEL), rows(D_MODEL), rows(D_MODEL), rows(1)],
        out_shape=[S((T, nj * fc), _MXU_DTYPE), S((T, D_MODEL), F32), S((T, D_MODEL), _MXU_DTYPE),
                   S((T, D_MODEL), F32), S((T, 1), F32)],
        scratch_shapes=[pltpu.VMEM((tm, D_MODEL), F32)],
        compiler_params=_params("parallel", "arbitrary"))(xb, xres, w_up, w_dn, g, b)


def _mlp_bwd_dx(dxo, xhat, rstd, g, u, w_up, w_dn, layer, *, tm=1024):
    T = dxo.shape[0]
    tm = min(tm, T)
    nj, _, _, fc = w_up.shape

    def body(dxo_ref, xh_ref, rs_ref, g_ref, u_ref, wu_ref, wd_ref, dx_ref, du_ref, dyb_ref, dg_ref, db_ref, acc, dy_sc):
        i, j = pl.program_id(0), pl.program_id(1)

        @pl.when((i == 0) & (j == 0))
        def _():
            dg_ref[...] = jnp.zeros_like(dg_ref)
            db_ref[...] = jnp.zeros_like(db_ref)

        @pl.when(j == 0)
        def _():
            dz, dg, db = _ln_bwd_math(dxo_ref[...], xh_ref[...], rs_ref[...], g_ref[...])
            dg_ref[...] += dg
            db_ref[...] += db
            acc[...] = ALPHA * dz
            dy_sc[...] = _mx(dz)
            dyb_ref[...] = _mx(dz)

        r = jnp.maximum(u_ref[...].astype(F32), 0.0)
        da = _dot(dy_sc[...], wd_ref[...], NT)
        dub = _mx(da * (2.0 * r))
        du_ref[...] = dub
        acc[...] += _dot(dub, wu_ref[...], NT)

        @pl.when(j == nj - 1)
        def _():
            dx_ref[...] = acc[...]

    rows = lambda n: pl.BlockSpec((tm, n), lambda i, j: (i, 0))
    full = lambda shp: pl.BlockSpec(shp, lambda i, j: (0,) * len(shp))
    return pl.pallas_call(
        body, name=f"mlp_bwd_dx_{layer}", grid=(T // tm, nj),
        in_specs=[rows(D_MODEL), rows(D_MODEL), rows(1), full(g.shape), pl.BlockSpec((tm, fc), lambda i, j: (i, j)),
                  pl.BlockSpec((None, None, D_MODEL, fc), lambda i, j: (j, layer, 0, 0)),
                  pl.BlockSpec((None, None, fc, D_MODEL), lambda i, j: (j, layer, 0, 0))],
        out_specs=[rows(D_MODEL), pl.BlockSpec((tm, fc), lambda i, j: (i, j)), rows(D_MODEL), full(g.shape), full(g.shape)],
        out_shape=[S((T, D_MODEL), F32), S((T, nj * fc), _MXU_DTYPE), S((T, D_MODEL), _MXU_DTYPE),
                   S(g.shape, F32), S(g.shape, F32)],
        scratch_shapes=[pltpu.VMEM((tm, D_MODEL), F32), pltpu.VMEM((tm, D_MODEL), _MXU_DTYPE)],
        compiler_params=_params("arbitrary", "arbitrary"))(dxo, xhat, rstd, g, u, w_up, w_dn)


def _mlp_bwd_dw(u, dyb, xinb, du, layer, *, nj, tm=1024):
    T = u.shape[0]
    tm = min(tm, T)
    fc = u.shape[1] // nj

    def body(u_ref, dy_ref, x_ref, du_ref, gd_ref, gu_ref):
        @pl.when(pl.program_id(1) == 0)
        def _():
            gd_ref[...] = jnp.zeros_like(gd_ref)
            gu_ref[...] = jnp.zeros_like(gu_ref)

        r = jnp.maximum(u_ref[...].astype(F32), 0.0)
        gd_ref[...] += _dot(_mx(r * r), dy_ref[...], TN)
        gu_ref[...] += _dot(x_ref[...], du_ref[...], TN)

    return pl.pallas_call(
        body, name=f"mlp_bwd_dw_{layer}", grid=(nj, T // tm),
        in_specs=[pl.BlockSpec((tm, fc), lambda j, i: (i, j)), pl.BlockSpec((tm, D_MODEL), lambda j, i: (i, 0)),
                  pl.BlockSpec((tm, D_MODEL), lambda j, i: (i, 0)), pl.BlockSpec((tm, fc), lambda j, i: (i, j))],
        out_specs=[pl.BlockSpec((None, fc, D_MODEL), lambda j, i: (j, 0, 0)),
                   pl.BlockSpec((None, D_MODEL, fc), lambda j, i: (j, 0, 0))],
        out_shape=[S((nj, fc, D_MODEL), F32), S((nj, D_MODEL, fc), F32)],
        compiler_params=_params("parallel", "arbitrary"))(u, dyb, xinb, du)


SWA_GROUP = SWA_QH // SWA_KVH
SWA_ROWS = SWA_GROUP * SWA_BLOCK


def _swa_heads(a, kh):
    return jnp.concatenate([a[:, SWA_D * (kh * SWA_GROUP + g):SWA_D * (kh * SWA_GROUP + g + 1)]
                            for g in range(SWA_GROUP)], axis=0)


def _swa_group(q, kvp, kvc, bias_ref, sink_ref, n, kh):
    blk, dkv = SWA_BLOCK, SWA_KVH * SWA_D
    cols = slice(kh * SWA_ROWS, (kh + 1) * SWA_ROWS)
    qg = _swa_heads(q, kh)
    kb = jnp.concatenate([kvp[:, SWA_D * kh:SWA_D * (kh + 1)], kvc[:, SWA_D * kh:SWA_D * (kh + 1)]], axis=0)
    vb = jnp.concatenate([kvp[:, dkv + SWA_D * kh:dkv + SWA_D * (kh + 1)],
                          kvc[:, dkv + SWA_D * kh:dkv + SWA_D * (kh + 1)]], axis=0)
    s = _dot(kb, qg, NT) * (SWA_D ** -0.5) + bias_ref[:, cols]
    key = lax.broadcasted_iota(jnp.int32, (2 * blk, SWA_ROWS), 0)
    qry = lax.broadcasted_iota(jnp.int32, (2 * blk, SWA_ROWS), 1) % blk
    valid = (key > qry) & (key <= qry + blk) & ((key >= blk) | (n > 0))
    s = jnp.where(valid, s, -jnp.inf)
    sink = sink_ref[:, cols]
    m = jnp.maximum(jnp.max(s, axis=0, keepdims=True), sink)
    p, ps = jnp.exp(s - m), jnp.exp(sink - m)
    inv = 1.0 / (jnp.sum(p, axis=0, keepdims=True) + ps)
    return qg, kb, vb, p * inv, ps * inv


def _swa_attn_fwd(qkv, bias, sinks):
    T = qkv.shape[0]
    blk = SWA_BLOCK
    nb = T // blk
    dq, dkv = SWA_QH * SWA_D, SWA_KVH * SWA_D

    def body(q_ref, kvp_ref, kvc_ref, bias_ref, sink_ref, o_ref):
        n = pl.program_id(0)
        q, kvp, kvc = q_ref[...], kvp_ref[...], kvc_ref[...]
        for kh in range(SWA_KVH):
            _, _, vb, p, _ = _swa_group(q, kvp, kvc, bias_ref, sink_ref, n, kh)
            og = _mx(_dot(_mx(p), vb, TN))
            for g in range(SWA_GROUP):
                hd = kh * SWA_GROUP + g
                o_ref[:, SWA_D * hd:SWA_D * (hd + 1)] = og[blk * g:blk * (g + 1), :]

    return pl.pallas_call(
        body, name="swa_attn_fwd", grid=(nb,),
        in_specs=[pl.BlockSpec((blk, dq), lambda n: (n, 0)),
                  pl.BlockSpec((blk, 2 * dkv), lambda n: (jnp.maximum(n - 1, 0), dq // (2 * dkv))),
                  pl.BlockSpec((blk, 2 * dkv), lambda n: (n, dq // (2 * dkv))),
                  pl.BlockSpec(bias.shape, lambda n: (0, 0)), pl.BlockSpec(sinks.shape, lambda n: (0, 0))],
        out_specs=pl.BlockSpec((blk, dq), lambda n: (n, 0)), out_shape=S((T, dq), _MXU_DTYPE),
        compiler_params=_params("parallel"))(qkv, qkv, qkv, bias, sinks)


def _swa_attn_bwd(qkv, ob, do, bias, sinks):
    T = qkv.shape[0]
    blk = SWA_BLOCK
    nb = T // blk
    dq, dkv = SWA_QH * SWA_D, SWA_KVH * SWA_D

    def body(q_ref, kvp_ref, kvc_ref, o_ref, do_ref, bias_ref, sink_ref, dqkv_ref, dbias_ref, dsink_ref, carry):
        st = pl.program_id(0)
        n = nb - 1 - st

        @pl.when(st == 0)
        def _():
            carry[...] = jnp.zeros_like(carry)
            dbias_ref[...] = jnp.zeros_like(dbias_ref)
            dsink_ref[...] = jnp.zeros_like(dsink_ref)

        q, kvp, kvc = q_ref[...], kvp_ref[...], kvc_ref[...]
        ov, dov = o_ref[...], do_ref[...]
        ones = jnp.ones((8, SWA_D), F32)
        for kh in range(SWA_KVH):
            cols = slice(kh * SWA_ROWS, (kh + 1) * SWA_ROWS)
            qg, kb, vb, p, ps = _swa_group(q, kvp, kvc, bias_ref, sink_ref, n, kh)
            dog = _swa_heads(dov, kh)
            dl = _dot(ones, dog * _swa_heads(ov, kh).astype(F32), NT, lax.Precision.HIGHEST)[0:1]
            dogb = _mx(dog)
            ds = p * (_dot(vb, dogb, NT) - dl)
            dbias_ref[:, cols] += ds
            dsink_ref[0:1, cols] += -ps * dl
            dsb = _mx(ds * (SWA_D ** -0.5))
            dqg = _mx(_dot(dsb, kb, TN))
            for g in range(SWA_GROUP):
                hd = kh * SWA_GROUP + g
                dqkv_ref[:, SWA_D * hd:SWA_D * (hd + 1)] = dqg[blk * g:blk * (g + 1), :]
            dkb = _dot(dsb, qg)
            dvb = _dot(_mx(p), dogb)
            ko, vo = SWA_D * kh, dkv + SWA_D * kh
            dqkv_ref[:, dq + ko:dq + ko + SWA_D] = _mx(dkb[blk:, :] + carry[:, ko:ko + SWA_D])
            dqkv_ref[:, dq + vo:dq + vo + SWA_D] = _mx(dvb[blk:, :] + carry[:, vo:vo + SWA_D])
            carry[:, ko:ko + SWA_D] = dkb[:blk, :]
            carry[:, vo:vo + SWA_D] = dvb[:blk, :]

    rev = lambda s: nb - 1 - s
    return pl.pallas_call(
        body, name="swa_attn_bwd", grid=(nb,),
        in_specs=[pl.BlockSpec((blk, dq), lambda s: (rev(s), 0)),
                  pl.BlockSpec((blk, 2 * dkv), lambda s: (jnp.maximum(rev(s) - 1, 0), dq // (2 * dkv))),
                  pl.BlockSpec((blk, 2 * dkv), lambda s: (rev(s), dq // (2 * dkv))),
                  pl.BlockSpec((blk, dq), lambda s: (rev(s), 0)), pl.BlockSpec((blk, dq), lambda s: (rev(s), 0)),
                  pl.BlockSpec(bias.shape, lambda s: (0, 0)), pl.BlockSpec(sinks.shape, lambda s: (0, 0))],
        out_specs=[pl.BlockSpec((blk, dq + 2 * dkv), lambda s: (rev(s), 0)),
                   pl.BlockSpec(bias.shape, lambda s: (0, 0)), pl.BlockSpec((8, sinks.shape[1]), lambda s: (0, 0))],
        out_shape=[S((T, dq + 2 * dkv), _MXU_DTYPE), S(bias.shape, F32), S((8, sinks.shape[1]), F32)],
        scratch_shapes=[pltpu.VMEM((blk, 2 * dkv), F32)],
        compiler_params=_params("arbitrary"))(qkv, qkv, qkv, ob, do, bias, sinks)


def _t5_onehot():
    i = jnp.arange(SWA_BLOCK)
    j = jnp.arange(2 * SWA_BLOCK)
    n = jnp.maximum(i[:, None] + SWA_BLOCK - j[None, :], 0)
    max_exact = REL_BUCKETS // 2
    nf = jnp.maximum(n, 1).astype(F32)
    large = max_exact + (jnp.log(nf / max_exact) / math.log(REL_MAX_DIST / max_exact)
                         * (REL_BUCKETS - max_exact)).astype(jnp.int32)
    large = jnp.minimum(large, REL_BUCKETS - 1)
    bucket = jnp.where(n < max_exact, n, large).reshape(-1)
    return (bucket[None, :] == jnp.arange(REL_BUCKETS)[:, None]).astype(F32)


def _loss_head(y, target, *, tm=1024):
    T, D = y.shape
    tm = min(tm, T)

    def body(y_ref, t_ref, loss_ref, dy_ref):
        @pl.when(pl.program_id(0) == 0)
        def _():
            loss_ref[...] = jnp.zeros_like(loss_ref)

        d = y_ref[...] - t_ref[...]
        dy_ref[...] = d * (1.0 / D)
        rs = jnp.sum(d * d, axis=1, keepdims=True)
        loss_ref[...] += (0.5 / D) * jnp.sum(rs, axis=0, keepdims=True)

    rows = pl.BlockSpec((tm, D), lambda i: (i, 0))
    return pl.pallas_call(
        body, name="loss_head", grid=(T // tm,), in_specs=[rows, rows],
        out_specs=[pl.BlockSpec((1, 1), lambda i: (0, 0)), rows], out_shape=[S((1, 1), F32), S((T, D), F32)],
        compiler_params=_params("arbitrary"))(y, target)


def _exchange_copies(ins, outs, n_gather, send_sems, recv_sems, loc_sems):
    mx, my, mc = lax.axis_index("x"), lax.axis_index("y"), lax.axis_index("c")
    me = 4 * mx + 2 * my + mc
    copies = []
    for a in range(len(ins)):
        src = ins[a] if a < n_gather else ins[a].at[me]
        copies.append(pltpu.make_async_copy(src, outs[a].at[me], loc_sems.at[a]))
    for k in range(1, N_DEV):
        px, py, pc = mx ^ ((k >> 2) & 1), my ^ ((k >> 1) & 1), mc ^ (k & 1)
        peer = 4 * px + 2 * py + pc
        for a in range(len(ins)):
            src = ins[a] if a < n_gather else ins[a].at[peer]
            copies.append(pltpu.make_async_remote_copy(
                src_ref=src, dst_ref=outs[a].at[me], send_sem=send_sems.at[a, k - 1],
                recv_sem=recv_sems.at[a, k - 1], device_id=(px, py, pc), device_id_type=pl.DeviceIdType.MESH))
    return copies


def _exchange_shapes(gather, scatter):
    n_arr = len(gather) + len(scatter)
    out_shape = [S((N_DEV,) + tuple(g.shape), g.dtype) for g in gather] + [S(s.shape, s.dtype) for s in scatter]
    sems = [pltpu.SemaphoreType.DMA((n_arr, N_DEV - 1)), pltpu.SemaphoreType.DMA((n_arr, N_DEV - 1)),
            pltpu.SemaphoreType.DMA((n_arr,))]
    return out_shape, sems


def _exchange(gather, scatter, *, name):
    n_g = len(gather)
    n_arr = n_g + len(scatter)

    def body(*refs):
        copies = _exchange_copies(refs[:n_arr], refs[n_arr:2 * n_arr], n_g, *refs[2 * n_arr:])
        for cp in copies:
            cp.start()
        for cp in copies:
            cp.wait()

    hbm = pl.BlockSpec(memory_space=pl.ANY)
    out_shape, sems = _exchange_shapes(gather, scatter)
    return pl.pallas_call(
        body, name=name, in_specs=[hbm] * n_arr, out_specs=[hbm] * n_arr, out_shape=out_shape,
        scratch_shapes=sems)(*gather, *scatter)


def _adamw(parts, w, m, v, *, name, tr=256):
    R, C = w.shape
    tr = min(tr, R)
    assert R % tr == 0

    def body(p_ref, w_ref, m_ref, v_ref, g_ref, d_ref, nm_ref, nv_ref):
        g = p_ref[0]
        for k in range(1, N_DEV):
            g = g + p_ref[k]
        g_ref[...] = g
        m_new = ADAM_B1 * m_ref[...] + (1.0 - ADAM_B1) * g
        v_new = ADAM_B2 * v_ref[...] + (1.0 - ADAM_B2) * (g * g)
        m_hat = m_new / (1.0 - ADAM_B1 ** ADAM_STEP)
        v_hat = v_new / (1.0 - ADAM_B2 ** ADAM_STEP)
        d_ref[...] = -ADAM_LR * (m_hat / (jnp.sqrt(v_hat) + ADAM_EPS) + ADAM_WD * w_ref[...])
        nm_ref[...] = m_new
        nv_ref[...] = v_new

    rows = pl.BlockSpec((tr, C), lambda i: (i, 0))
    return pl.pallas_call(
        body, name=name, grid=(R // tr,),
        in_specs=[pl.BlockSpec((N_DEV, tr, C), lambda i: (0, i, 0)), rows, rows, rows],
        out_specs=[rows] * 4, out_shape=[S((R, C), F32)] * 4,
        compiler_params=_params("parallel"))(parts, w, m, v)


def _rows_of(n):
    return -(-n // LANES)


def _pack(pieces, total_rows, dtype, lead=()):
    out = []
    for p in pieces:
        flat = p.reshape(lead + (-1,)).astype(dtype)
        n = flat.shape[-1]
        pad = _rows_of(n) * LANES - n
        if pad:
            flat = jnp.pad(flat, [(0, 0)] * len(lead) + [(0, pad)])
        out.append(flat.reshape(lead + (-1, LANES)))
    used = sum(o.shape[-2] for o in out)
    if total_rows > used:
        out.append(jnp.zeros(lead + (total_rows - used, LANES), dtype))
    return jnp.concatenate(out, axis=len(lead))


def _unpack(buf, shapes, lead=()):
    res, r0 = [], 0
    for shp in shapes:
        n = int(np.prod(shp))
        nr = _rows_of(n)
        piece = lax.slice_in_dim(buf, r0, r0 + nr, axis=len(lead)).reshape(lead + (nr * LANES,))
        res.append(lax.slice_in_dim(piece, 0, n, axis=len(lead)).reshape(lead + tuple(shp)))
        r0 += nr
    return res


def _round_up(n, m):
    return -(-n // m) * m


BIG = ["mla_w_in", "mla_w_uq", "mla_w_uk", "mla_w_uv", "mla_w_o", "kv_w_shared", "swa_w_q", "swa_w_o",
       "mlp_w_up", "mlp_w_down"]
GAINS = ["mla_g_q", "mla_g_kv"]
SHARDED = BIG + GAINS
REPL = ["swa_sinks", "rel_bias", "ln_mix_g", "ln_mix_b", "ln_mlp_g", "ln_mlp_b"]
WEIGHTS = ["mla_w_in", "mla_g_q", "mla_g_kv", "mla_w_uq", "mla_w_uk", "mla_w_uv", "mla_w_o", "kv_w_shared",
           "swa_w_q", "swa_sinks", "swa_w_o", "rel_bias", "mlp_w_up", "mlp_w_down", "ln_mix_g", "ln_mix_b",
           "ln_mlp_g", "ln_mlp_b"]


def kernel(x, mla_w_in, mla_g_q, mla_g_kv, mla_w_uq, mla_w_uk, mla_w_uv, mla_w_o, kv_w_shared, swa_w_q, swa_sinks, swa_w_o, rel_bias, mlp_w_up, mlp_w_down, ln_mix_g, ln_mix_b, ln_mlp_g, ln_mlp_b, loss_target, m_mla_w_in, m_mla_g_q, m_mla_g_kv, m_mla_w_uq, m_mla_w_uk, m_mla_w_uv, m_mla_w_o, m_kv_w_shared, m_swa_w_q, m_swa_sinks, m_swa_w_o, m_rel_bias, m_mlp_w_up, m_mlp_w_down, m_ln_mix_g, m_ln_mix_b, m_ln_mlp_g, m_ln_mlp_b, v_mla_w_in, v_mla_g_q, v_mla_g_kv, v_mla_w_uq, v_mla_w_uk, v_mla_w_uv, v_mla_w_o, v_kv_w_shared, v_swa_w_q, v_swa_sinks, v_swa_w_o, v_rel_bias, v_mlp_w_up, v_mlp_w_down, v_ln_mix_g, v_ln_mix_b, v_ln_mlp_g, v_ln_mlp_b):
    args = dict(locals())
    W = {n: args[n] for n in WEIGHTS}
    M = {n: args["m_" + n] for n in WEIGHTS}
    V = {n: args["v_" + n] for n in WEIGHTS}
    T = x.shape[1]
    x2d = x.reshape(T, D_MODEL)
    tgt = loss_target.reshape(T, D_MODEL)
    H = MLA_HEADS

    SH = {"mla_w_in": (-1, mla_w_in.shape[-1]), "mla_w_uq": (-1, H * (MLA_NOPE + MLA_ROPE)),
          "mla_w_uk": (-1, H * MLA_NOPE), "mla_w_uv": (-1, H * MLA_V), "mla_w_o": (-1, D_MODEL),
          "kv_w_shared": (-1, kv_w_shared.shape[-1]), "swa_w_q": (-1, swa_w_q.shape[-1]), "swa_w_o": (-1, D_MODEL)}
    slab = lambda d, n: d[n].reshape(SH[n])
    bf = lambda a: a.astype(_MXU_DTYPE)
    gains_slab = lambda d: jnp.pad(jnp.concatenate([d["mla_g_q"], d["mla_g_kv"]], axis=1),
                                   ((0, 7), (0, 128 - d["mla_g_q"].shape[1] - d["mla_g_kv"].shape[1])))
    n_gq, n_gkv = mla_g_q.shape[1], mla_g_kv.shape[1]
    w_in_s, w_uq_s, w_uk_s, gains_all = _exchange(
        [bf(slab(W, "mla_w_in")), bf(slab(W, "mla_w_uq")), bf(slab(W, "mla_w_uk")), gains_slab(W)], [],
        name="gather_mla_in")
    later = [bf(slab(W, n)) for n in ("mla_w_uv", "mla_w_o", "kv_w_shared", "swa_w_q", "swa_w_o")]
    later += [bf(mlp_w_up), bf(mlp_w_down)]
    w_in = w_in_s.reshape(D_MODEL, -1)
    g_q = gains_all[:, 0, :n_gq].reshape(1, MLA_QR)
    g_kv = gains_all[:, 0, n_gq:n_gq + n_gkv].reshape(1, MLA_C)
    w_uq = w_uq_s.reshape(MLA_QR, H, MLA_NOPE + MLA_ROPE)
    w_uq_n = w_uq[:, :, :MLA_NOPE].reshape(MLA_QR, H * MLA_NOPE)
    w_uq_r = w_uq[:, :, MLA_NOPE:].reshape(MLA_QR, H * MLA_ROPE)
    w_uk = w_uk_s.reshape(MLA_C, H, MLA_NOPE).transpose(1, 0, 2)
    w_uk_t = w_uk.transpose(0, 2, 1)
    ln = lambda a, l: a[l].reshape(1, D_MODEL)

    half = MLA_ROPE // 2
    inv = ROPE_THETA ** (-jnp.arange(half, dtype=F32) / half)
    ang = jnp.arange(T, dtype=F32)[:, None] * inv[None, :]
    cos = jnp.tile(jnp.concatenate([jnp.cos(ang), jnp.cos(ang)], -1), (1, H))
    sin = jnp.tile(jnp.concatenate([-jnp.sin(ang), jnp.sin(ang)], -1), (1, H))

    h, kc, kct, qs, qst, cq, qn = _mla_pre_fwd(x2d, w_in, g_q, g_kv, w_uq_n, w_uq_r, w_uk_t, cos, sin)
    olat, lse, (w_uv_s, w_o_s, w_kv_s, w_q_s, w_o2_s, w_up, w_dn) = _mla_attn_fwd(qs, kc, kct, gather=later)
    w_uv = w_uv_s.reshape(MLA_C, H, MLA_V).transpose(1, 0, 2)
    w_o = w_o_s.reshape(H * MLA_V, D_MODEL)
    w_qkv = jnp.concatenate([w_q_s.reshape(D_MODEL, -1), w_kv_s.reshape(D_MODEL, -1)], axis=1)
    w_o2 = w_o2_s.reshape(SWA_QH * SWA_D, D_MODEL)
    o_mla = _mla_uv_fwd(olat, w_uv)
    x1, x1b, xh1, rs1 = _proj_ln_fwd(o_mla, w_o, x2d, ln(ln_mix_g, 0), ln(ln_mix_b, 0), name="mla_out_ln_fwd")
    u0, x2, x2b, xh2, rs2 = _mlp_fwd(x1b, x1, w_up, w_dn, 0, ln(ln_mlp_g, 0), ln(ln_mlp_b, 0))
    onehot = _t5_onehot()
    bias = _mm(rel_bias.T, onehot, name="rel_bias_expand", precision=lax.Precision.HIGHEST, tn=8192).reshape(
        SWA_QH * SWA_BLOCK, 2 * SWA_BLOCK).T
    sink_rows = jnp.repeat(swa_sinks.reshape(SWA_QH), SWA_BLOCK).reshape(1, SWA_QH * SWA_BLOCK)
    qkv = _mm(x2b, w_qkv, name="swa_qkv_fwd", out_dtype=_MXU_DTYPE, tm=1024, tn=512, tk=1024)
    o_swa = _swa_attn_fwd(qkv, bias, sink_rows)
    x3, x3b, xh3, rs3 = _proj_ln_fwd(o_swa, w_o2, x2, ln(ln_mix_g, 1), ln(ln_mix_b, 1), name="swa_out_ln_fwd")
    u1, x4, _, xh4, rs4 = _mlp_fwd(x3b, x3, w_up, w_dn, 1, ln(ln_mlp_g, 1), ln(ln_mlp_b, 1))
    loss_part, dx4 = _loss_head(x4, tgt)
    loss = lax.psum(loss_part[0, 0], AXES)

    nj = w_up.shape[0]
    dx3, du1, dy4b, dg_mlp1, db_mlp1 = _mlp_bwd_dx(dx4, xh4, rs4, ln(ln_mlp_g, 1), u1, w_up, w_dn, 1)
    g_dn1, g_up1 = _mlp_bwd_dw(u1, dy4b, x3b, du1, 1, nj=nj)
    dres3, do_swa, g_o2, dg_mix1, db_mix1 = _proj_ln_bwd(dx3, xh3, rs3, ln(ln_mix_g, 1), o_swa, w_o2,
                                                         name="swa_out_ln_bwd")
    dqkv, dbias, dsink = _swa_attn_bwd(qkv, o_swa, do_swa, bias, sink_rows)
    g_rel = _mm(onehot, dbias.T.reshape(SWA_QH, -1), name="rel_bias_grad", tb=True, precision=lax.Precision.HIGHEST,
                tk=8192)
    head_of_row = (jnp.arange(SWA_QH * SWA_BLOCK)[:, None] // SWA_BLOCK == jnp.arange(SWA_QH)[None, :]).astype(F32)
    g_sinks = _mm(dsink, head_of_row, name="sinks_grad", precision=lax.Precision.HIGHEST, tk=2048)[0:1]
    dx2 = _mm(dqkv, w_qkv, name="swa_qkv_bwd_dx", tb=True, add=dres3, tm=1024, tn=1024, tk=512)
    g_qkv = _mm(x2b, dqkv, name="swa_qkv_bwd_dw", ta=True, tm=1024, tn=512, tk=1024)
    dx1, du0, dy2b, dg_mlp0, db_mlp0 = _mlp_bwd_dx(dx2, xh2, rs2, ln(ln_mlp_g, 0), u0, w_up, w_dn, 0)
    g_dn0, g_up0 = _mlp_bwd_dw(u0, dy2b, x1b, du0, 0, nj=nj)
    dres1, do_mla, g_o, dg_mix0, db_mix0 = _proj_ln_bwd(dx1, xh1, rs1, ln(ln_mix_g, 0), o_mla, w_o,
                                                        name="mla_out_ln_bwd")
    dol, delta, g_uv = _mla_uv_bwd(do_mla, olat, w_uv)
    repl_grads = {
        "swa_sinks": g_sinks, "rel_bias": g_rel,
        "ln_mix_g": jnp.concatenate([dg_mix0, dg_mix1], 0), "ln_mix_b": jnp.concatenate([db_mix0, db_mix1], 0),
        "ln_mlp_g": jnp.concatenate([dg_mlp0, dg_mlp1], 0), "ln_mlp_b": jnp.concatenate([db_mlp0, db_mlp1], 0),
    }
    repl_rows = _round_up(sum(_rows_of(W[n].size) for n in REPL), 8)
    r_part = _pack([repl_grads[n] for n in REPL], repl_rows, F32)
    by_dev = lambda g: g.reshape((N_DEV, g.shape[0] // N_DEV) + g.shape[1:])
    early = [by_dev(g_o2), by_dev(g_qkv), g_up0, g_up1, g_dn0, g_dn1, by_dev(g_o),
             by_dev(g_uv.transpose(1, 0, 2).reshape(MLA_C, H * MLA_V))]
    dqs, dkc, dv, (r_all, p_o2, p_qkv, p_up0, p_up1, p_dn0, p_dn1, p_o, p_uv) = _mla_attn_bwd(
        qs, qst, kc, kct, dol, lse, delta, gather=[r_part], scatter=early)
    grad_x, g_in, g_uq_n, g_uq_r, g_uk, g_gq, g_gkv = _mla_pre_bwd(
        dqs, dkc, dv, h, x2d, dres1, cq, qn, cos, sin, w_in, g_q, g_kv, w_uq_n, w_uq_r, w_uk)
    g_uq = jnp.concatenate([g_uq_n.reshape(MLA_QR, H, MLA_NOPE), g_uq_r.reshape(MLA_QR, H, MLA_ROPE)], -1)
    g_gains = jnp.pad(jnp.concatenate([g_gq.reshape(N_DEV, n_gq), g_gkv.reshape(N_DEV, n_gkv)], axis=1)[:, None, :],
                      ((0, 0), (0, 7), (0, 128 - n_gq - n_gkv)))
    p_in, p_uq, p_uk, p_gains = _exchange(
        [], [by_dev(g_in), by_dev(g_uq.reshape(MLA_QR, -1)), by_dev(g_uk.transpose(1, 0, 2).reshape(MLA_C, -1)),
             g_gains], name="exchange_mla_in_grads")

    res = {}

    def adam(name, parts, names, to_slab, from_slab):
        out = _adamw(parts, to_slab(W), to_slab(M), to_slab(V), name="adamw_" + name)
        for k in range(4):
            for n, a in zip(names, from_slab(out[k])):
                res[(k, n)] = a.reshape(W[n].shape)

    one = lambda n: (lambda d: slab(d, n))
    adam("swa_w_o", p_o2, ["swa_w_o"], one("swa_w_o"), lambda s: [s])
    dq_cols = SWA_QH * SWA_D
    adam("swa_qkv", p_qkv, ["swa_w_q", "kv_w_shared"],
         lambda d: jnp.concatenate([slab(d, "swa_w_q"), slab(d, "kv_w_shared")], axis=1),
         lambda s: [s[:, :dq_cols], s[:, dq_cols:]])
    for name, parts in (("mlp_w_up", (p_up0, p_up1)), ("mlp_w_down", (p_dn0, p_dn1))):
        per_layer = [_adamw(parts[l], W[name][l], M[name][l], V[name][l], name=f"adamw_{name}_{l}") for l in range(DEPTH)]
        for k in range(4):
            res[(k, name)] = jnp.stack([per_layer[l][k] for l in range(DEPTH)], axis=0)
    adam("mla_w_o", p_o, ["mla_w_o"], one("mla_w_o"), lambda s: [s])
    adam("mla_w_uv", p_uv, ["mla_w_uv"], one("mla_w_uv"), lambda s: [s])
    adam("mla_w_in", p_in, ["mla_w_in"], one("mla_w_in"), lambda s: [s])
    adam("mla_w_uq", p_uq, ["mla_w_uq"], one("mla_w_uq"), lambda s: [s])
    adam("mla_w_uk", p_uk, ["mla_w_uk"], one("mla_w_uk"), lambda s: [s])
    adam("mla_gains", p_gains, ["mla_g_q", "mla_g_kv"], gains_slab,
         lambda s: [s[0:1, :n_gq], s[0:1, n_gq:n_gq + n_gkv]])
    adam("replicated", r_all, REPL, lambda d: _pack([d[n] for n in REPL], repl_rows, F32),
         lambda s: _unpack(s, [W[n].shape for n in REPL]))
    return (loss, grad_x.reshape(x.shape), *[res[(k, n)] for k in range(4) for n in WEIGHTS])
```

```python
import functools
import math

import numpy as np
import jax
import jax.numpy as jnp
from jax import lax
from jax.experimental import pallas as pl
from jax.experimental.pallas import tpu as pltpu

F32 = jnp.float32
_MXU_DTYPE = jnp.bfloat16

D_MODEL = 1024
DEPTH = 2
MLA_HEADS = 8
MLA_NOPE = 128
MLA_ROPE = 64
MLA_V = 128
MLA_QR = 384
MLA_C = 256
MLA_DK = 384
MLA_DT = MLA_C + MLA_ROPE
ROPE_THETA = 10000.0
SWA_QH = 16
SWA_KVH = 4
SWA_D = 64
SWA_BLOCK = 128
REL_BUCKETS = 32
REL_MAX_DIST = 128
D_FF = 4096
LN_EPS = 1e-5
RMS_EPS = 1e-6
ALPHA = (2 * DEPTH) ** 0.25
ADAM_LR, ADAM_B1, ADAM_B2, ADAM_EPS, ADAM_WD, ADAM_STEP = 0.001, 0.9, 0.999, 1e-08, 0.01, 10

N_DEV = 8
AXES = ("x", "y", "c")
V7X_VMEM_BYTES = 64 * 1024 * 1024
VMEM_LIMIT = V7X_VMEM_BYTES - 8 * 1024 * 1024
LANES = 1024
ATT_TQ = 512
ATT_TK = 512
ATT_HEAD_GROUP = 1

NT = (((1,), (1,)), ((), ()))
TN = (((0,), (0,)), ((), ()))
S = jax.ShapeDtypeStruct


def _params(*sem, vmem=VMEM_LIMIT):
    return pltpu.CompilerParams(dimension_semantics=sem, vmem_limit_bytes=vmem)


def _dot(a, b, dims=None, precision=None):
    if dims is None:
        return jnp.dot(a, b, preferred_element_type=F32, precision=precision)
    return lax.dot_general(a, b, dims, preferred_element_type=F32, precision=precision)


def _mx(v):
    return v.astype(_MXU_DTYPE)


ROPE_TABLE_W = 128


def _tile_heads(t):
    return jnp.concatenate([t] * (MLA_HEADS * MLA_ROPE // ROPE_TABLE_W), axis=1)


def _swap_halves_64(v):
    return jnp.concatenate([v[:, 32:], v[:, :32]], axis=-1)


def _swap_halves_groups(v):
    n = v.shape[-1]
    lane = lax.broadcasted_iota(jnp.int32, v.shape, 1)
    return jnp.where(lane % 64 < 32, pltpu.roll(v, n - 32, 1), pltpu.roll(v, 32, 1))


def _mm(a, b, *, name, ta=False, tb=False, add=None, out_dtype=F32, tm=512, tn=512, tk=512, precision=None):
    M, K = (a.shape[1], a.shape[0]) if ta else a.shape
    N = b.shape[0] if tb else b.shape[1]
    tm, tn, tk = min(tm, M), min(tn, N), min(tk, K)
    assert M % tm == 0 and N % tn == 0 and K % tk == 0, (M, N, K, tm, tn, tk)
    nk = K // tk
    dims = (((0 if ta else 1,), (1 if tb else 0,)), ((), ()))
    has_add = add is not None

    def body(*refs):
        if has_add:
            a_ref, b_ref, add_ref, o_ref, acc = refs
        else:
            a_ref, b_ref, o_ref, acc = refs
        k = pl.program_id(2)
        av, bv = a_ref[...], b_ref[...]
        if precision is None:
            av, bv = _mx(av), _mx(bv)
        part = _dot(av, bv, dims, precision)
        if nk == 1:
            o_ref[...] = (part + add_ref[...] if has_add else part).astype(out_dtype)
            return

        @pl.when(k == 0)
        def _():
            acc[...] = add_ref[...] if has_add else jnp.zeros_like(acc)

        acc[...] += part

        @pl.when(k == nk - 1)
        def _():
            o_ref[...] = acc[...].astype(out_dtype)

    a_spec = pl.BlockSpec((tk, tm), lambda i, j, k: (k, i)) if ta else pl.BlockSpec((tm, tk), lambda i, j, k: (i, k))
    b_spec = pl.BlockSpec((tn, tk), lambda i, j, k: (j, k)) if tb else pl.BlockSpec((tk, tn), lambda i, j, k: (k, j))
    in_specs = [a_spec, b_spec]
    args = [a, b]
    if has_add:
        in_specs.append(pl.BlockSpec((tm, tn), lambda i, j, k: (i, j)))
        args.append(add)
    return pl.pallas_call(
        body, name=name, grid=(M // tm, N // tn, nk), in_specs=in_specs,
        out_specs=pl.BlockSpec((tm, tn), lambda i, j, k: (i, j)), out_shape=S((M, N), out_dtype),
        scratch_shapes=[pltpu.VMEM((tm, tn), F32)],
        compiler_params=_params("parallel", "parallel", "arbitrary"))(*args)


def _ln_fwd_math(z, g, b):
    mu = jnp.mean(z, axis=-1, keepdims=True)
    zc = z - mu
    var = jnp.mean(zc * zc, axis=-1, keepdims=True)
    rstd = lax.rsqrt(var + LN_EPS)
    xhat = zc * rstd
    return xhat * g + b, xhat, rstd


def _ln_bwd_math(dxo, xhat, rstd, g):
    dxh = dxo * g
    m1 = jnp.mean(dxh, axis=-1, keepdims=True)
    m2 = jnp.mean(dxh * xhat, axis=-1, keepdims=True)
    dz = rstd * (dxh - m1 - xhat * m2)
    dg = jnp.sum(dxo * xhat, axis=0, keepdims=True)
    db = jnp.sum(dxo, axis=0, keepdims=True)
    return dz, dg, db


def _rms_fwd_math(xr, g):
    r = lax.rsqrt(jnp.mean(xr * xr, axis=-1, keepdims=True) + RMS_EPS)
    return xr * r * g


def _rms_bwd_math(dy, xr, g):
    r = lax.rsqrt(jnp.mean(xr * xr, axis=-1, keepdims=True) + RMS_EPS)
    gy = dy * g
    dx = r * gy - xr * (r * r * r) * jnp.mean(gy * xr, axis=-1, keepdims=True)
    dg = jnp.sum(dy * xr * r, axis=0, keepdims=True)
    return dx, dg


def _mla_pre_fwd(x, w_in, g_q, g_kv, w_uq_n, w_uq_r, w_uk_t, cos, sin):
    T = x.shape[0]
    tm = min(ATT_TQ, T)
    nq = T // tm
    H = MLA_HEADS

    tk = min(ATT_TK, T)

    def body(x_ref, win_ref, gq_ref, gkv_ref, wn_ref, wr_ref, wuk_ref, cos_ref, sin_ref,
             h_ref, kc_ref, kct_ref, qs_ref, qst_ref, cq_ref, qn_ref):
        h = _dot(_mx(x_ref[...]), win_ref[...])
        h_ref[...] = h
        cos_v, sin_v = _tile_heads(cos_ref[...]), _tile_heads(sin_ref[...])
        cq = _mx(_rms_fwd_math(h[:, :MLA_QR], gq_ref[...]))
        ckv = _rms_fwd_math(h[:, MLA_QR:MLA_QR + MLA_C], gkv_ref[...])
        krr = h[:, MLA_QR + MLA_C:]
        kr = krr * cos_v[:, :MLA_ROPE] + _swap_halves_64(krr) * sin_v[:, :MLA_ROPE]
        kr_pad = jnp.concatenate([kr, jnp.zeros((tm, MLA_DK - MLA_C - MLA_ROPE), F32)], axis=1)
        kc_ref[:, 0:MLA_C] = _mx(ckv)
        kc_ref[:, MLA_C:] = _mx(kr_pad)
        kct_ref[0:MLA_C, :] = _mx(ckv.T)
        kct_ref[MLA_C:, :] = _mx(kr_pad.T[0:MLA_ROPE, :])
        cq_ref[...] = cq
        qnb = _mx(_dot(cq, wn_ref[...]))
        qn_ref[...] = qnb
        qr = _dot(cq, wr_ref[...])
        qrr = qr * cos_v + _swap_halves_groups(qr) * sin_v
        qrr_t = qrr.T
        for hd in range(H):
            ql = _dot(qnb[:, MLA_NOPE * hd:MLA_NOPE * (hd + 1)], wuk_ref[hd])
            qst_ref[0, 0:MLA_C, tm * hd:tm * (hd + 1)] = _mx(ql.T)
            qst_ref[0, MLA_C:, tm * hd:tm * (hd + 1)] = _mx(qrr_t[MLA_ROPE * hd:MLA_ROPE * (hd + 1), :])
            qs_ref[0, hd, :, 0:MLA_C] = _mx(ql)
            qs_ref[0, hd, :, MLA_C:MLA_C + MLA_ROPE] = _mx(qrr[:, MLA_ROPE * hd:MLA_ROPE * (hd + 1)])
            qs_ref[0, hd, :, MLA_C + MLA_ROPE:] = jnp.zeros((tm, MLA_DK - MLA_C - MLA_ROPE), _MXU_DTYPE)

    full = lambda shp: pl.BlockSpec(shp, lambda i: (0,) * len(shp))
    rows = lambda n: pl.BlockSpec((tm, n), lambda i: (i, 0))
    n_in = w_in.shape[1]
    return pl.pallas_call(
        body, name="mla_pre_fwd", grid=(nq,),
        in_specs=[rows(D_MODEL), full(w_in.shape), full(g_q.shape), full(g_kv.shape), full(w_uq_n.shape),
                  full(w_uq_r.shape), full(w_uk_t.shape), rows(ROPE_TABLE_W), rows(ROPE_TABLE_W)],
        out_specs=[rows(n_in), rows(MLA_DK),
                   pl.BlockSpec((None, MLA_DT, tm), lambda i: (i * tm // tk, 0, i % (tk // tm))),
                   pl.BlockSpec((1, H, tm, MLA_DK), lambda i: (i, 0, 0, 0)),
                   pl.BlockSpec((1, MLA_DT, H * tm), lambda i: (i, 0, 0)), rows(MLA_QR), rows(H * MLA_NOPE)],
        out_shape=[S((T, n_in), F32), S((T, MLA_DK), _MXU_DTYPE), S((T // tk, MLA_DT, tk), _MXU_DTYPE),
                   S((nq, H, tm, MLA_DK), _MXU_DTYPE), S((nq, MLA_DT, H * tm), _MXU_DTYPE),
                   S((T, MLA_QR), _MXU_DTYPE), S((T, H * MLA_NOPE), _MXU_DTYPE)],
        compiler_params=_params("parallel"))(x, w_in, g_q, g_kv, w_uq_n, w_uq_r, w_uk_t, cos, sin)


def _att_steps(T, tq, tk):
    qi, kj = [], []
    for i in range(T // tq):
        for j in range((i * tq + tq - 1) // tk + 1):
            qi.append(i)
            kj.append(j)
    return jnp.asarray(np.array(qi, np.int32)), jnp.asarray(np.array(kj, np.int32))


def _ride_exchange(st, n_steps, ins, outs, n_gather, sems):
    if not ins:
        return

    @pl.when(st == 0)
    def _():
        for cp in _exchange_copies(ins, outs, n_gather, *sems):
            cp.start()

    @pl.when(st == n_steps - 1)
    def _():
        for cp in _exchange_copies(ins, outs, n_gather, *sems):
            cp.wait()


def _mla_attn_fwd(qs, kc, kct, gather=(), scatter=()):
    nq, H, tq, DK = qs.shape
    T = kc.shape[0]
    tk = min(ATT_TK, T)
    scale = (MLA_NOPE + MLA_ROPE) ** -0.5
    c2 = scale * math.log2(math.e)
    qi, kj = _att_steps(T, tq, tk)
    n_steps = int(qi.shape[0])
    hg = ATT_HEAD_GROUP
    R = hg * tq
    n_x = len(gather) + len(scatter)

    def body(qi_ref, kj_ref, q_ref, k_ref, kt_ref, *rest):
        x_ins, (o_ref, lse_ref), x_outs = rest[:n_x], rest[n_x:n_x + 2], rest[n_x + 2:2 * n_x + 2]
        m_sc, l_sc, acc_sc = rest[2 * n_x + 2:2 * n_x + 5]
        st = pl.program_id(0)
        _ride_exchange(st, n_steps, x_ins, x_outs, len(gather), rest[2 * n_x + 5:])
        i, j = qi_ref[st], kj_ref[st]
        j_last = (i * tq + tq - 1) // tk

        @pl.when(j == 0)
        def _():
            m_sc[...] = jnp.full_like(m_sc, -jnp.inf)
            l_sc[...] = jnp.zeros_like(l_sc)
            acc_sc[...] = jnp.zeros_like(acc_sc)

        def step(masked):
            k = k_ref[...]
            vt = kt_ref[0:MLA_C, :]
            if masked:
                key = lax.broadcasted_iota(jnp.int32, (tk, R), 0) + j * tk
                qry = lax.broadcasted_iota(jnp.int32, (tk, R), 1) % tq + i * tq
                causal = key <= qry
            n_g = H // hg
            qk = lambda g: _dot(k, q_ref[0, g * hg:(g + 1) * hg].reshape(R, DK), NT)
            def accumulate(g, a, pb):
                cs = slice(g * R, (g + 1) * R)
                acc_sc[:, cs] = a * acc_sc[:, cs] + _dot(vt, pb)

            s_next = qk(0)
            pending = None
            for g in range(n_g):
                cs = slice(g * R, (g + 1) * R)
                s = s_next
                if g + 1 < n_g:
                    s_next = qk(g + 1)
                if pending is not None:
                    accumulate(*pending)
                if masked:
                    s = jnp.where(causal, s, -jnp.inf)
                m_prev = m_sc[:, cs]
                m_new = jnp.maximum(m_prev, jnp.max(s, axis=0, keepdims=True))
                a = jnp.exp2((m_prev - m_new) * c2)
                p = jnp.exp2((s - m_new) * c2)
                l_sc[:, cs] = a * l_sc[:, cs] + jnp.sum(p, axis=0, keepdims=True)
                m_sc[:, cs] = m_new
                pending = (g, a, _mx(p))
            accumulate(*pending)

        pl.when(j == j_last)(lambda: step(True))
        pl.when(j != j_last)(lambda: step(False))

        @pl.when(j == j_last)
        def _():
            o_ref[0] = _mx(acc_sc[...] / l_sc[...])
            lse_ref[0] = m_sc[...] * scale + jnp.log(l_sc[...])

    hbm = pl.BlockSpec(memory_space=pl.ANY)
    x_shapes, x_sems = _exchange_shapes(gather, scatter) if n_x else ([], [])
    gs = pltpu.PrefetchScalarGridSpec(
        num_scalar_prefetch=2, grid=(n_steps,),
        in_specs=[pl.BlockSpec((1, H, tq, DK), lambda s, qi, kj: (qi[s], 0, 0, 0)),
                  pl.BlockSpec((tk, DK), lambda s, qi, kj: (kj[s], 0)),
                  pl.BlockSpec((None, MLA_DT, tk), lambda s, qi, kj: (kj[s], 0, 0))] + [hbm] * n_x,
        out_specs=[pl.BlockSpec((1, MLA_C, H * tq), lambda s, qi, kj: (qi[s], 0, 0)),
                   pl.BlockSpec((1, 1, H * tq), lambda s, qi, kj: (qi[s], 0, 0))] + [hbm] * n_x,
        scratch_shapes=[pltpu.VMEM((1, H * tq), F32), pltpu.VMEM((1, H * tq), F32),
                        pltpu.VMEM((MLA_C, H * tq), F32)] + x_sems)
    res = pl.pallas_call(
        body, name="mla_attn_fwd", grid_spec=gs,
        out_shape=[S((nq, MLA_C, H * tq), _MXU_DTYPE), S((nq, 1, H * tq), F32)] + x_shapes,
        compiler_params=_params("arbitrary"))(qi, kj, qs, kc, kct, *gather, *scatter)
    return res[0], res[1], res[2:]


def _mla_attn_bwd(qs, qst, kc, kct, dol, lse, delta, gather=(), scatter=()):
    nq, H, tq, DK = qs.shape
    T = kc.shape[0]
    tk = min(ATT_TK, T)
    scale = (MLA_NOPE + MLA_ROPE) ** -0.5
    log2e = math.log2(math.e)
    qi, kj = _att_steps(T, tq, tk)
    n_steps = int(qi.shape[0])
    hg = ATT_HEAD_GROUP
    R = hg * tq
    n_x = len(gather) + len(scatter)

    def body(qi_ref, kj_ref, q_ref, qt_ref, k_ref, kt_ref, do_ref, lse_ref, dl_ref, *rest):
        x_ins, (dq_ref, dk_ref, dv_ref), x_outs = rest[:n_x], rest[n_x:n_x + 3], rest[n_x + 3:2 * n_x + 3]
        dk_acc, dv_acc, sem = rest[2 * n_x + 3:2 * n_x + 6]
        st = pl.program_id(0)
        _ride_exchange(st, n_steps, x_ins, x_outs, len(gather), rest[2 * n_x + 6:])
        i, j = qi_ref[st], kj_ref[st]
        j_last = (i * tq + tq - 1) // tk

        @pl.when(st == 0)
        def _():
            dk_acc[...] = jnp.zeros_like(dk_acc)
            dv_acc[...] = jnp.zeros_like(dv_acc)

        @pl.when(j == 0)
        def _():
            dq_ref[...] = jnp.zeros_like(dq_ref)

        def step(masked):
            k, kt = k_ref[...], kt_ref[...]
            v = k[:, :MLA_C]
            if masked:
                key = lax.broadcasted_iota(jnp.int32, (tk, R), 0) + j * tk
                qry = lax.broadcasted_iota(jnp.int32, (tk, R), 1) % tq + i * tq
                causal = key <= qry
            dkt_c = jnp.zeros((MLA_DT, tk), F32)
            dvt_c = jnp.zeros((MLA_C, tk), F32)
            n_g = H // hg

            def scores(g):
                q = q_ref[0, g * hg:(g + 1) * hg].reshape(R, DK)
                dot = do_ref[0, :, g * R:(g + 1) * R]
                return dot, _dot(k, q, NT), _dot(v, dot)

            nxt = scores(0)
            for g in range(n_g):
                cs = slice(g * R, (g + 1) * R)
                dot, s, dp = nxt
                if g + 1 < n_g:
                    nxt = scores(g + 1)
                p = jnp.exp2(s * (scale * log2e) - lse_ref[0, :, cs] * log2e)
                if masked:
                    p = jnp.where(causal, p, 0.0)
                dsb = _mx(p * (dp - dl_ref[0, :, cs]))
                dq_ref[0, :, cs] += _dot(kt, dsb)
                dkt_c = dkt_c + _dot(qt_ref[0, :, cs], dsb, NT)
                dvt_c = dvt_c + _dot(dot, _mx(p), NT)
            dk_acc[j] += dkt_c * scale
            dv_acc[j] += dvt_c

        pl.when(j == j_last)(lambda: step(True))
        pl.when(j != j_last)(lambda: step(False))

        @pl.when(j == j_last)
        def _():
            dq_ref[...] = dq_ref[...] * scale

        @pl.when(st == n_steps - 1)
        def _():
            c1 = pltpu.make_async_copy(dk_acc, dk_ref, sem.at[0])
            c2 = pltpu.make_async_copy(dv_acc, dv_ref, sem.at[1])
            c1.start()
            c2.start()
            c1.wait()
            c2.wait()

    cols = lambda n: pl.BlockSpec((1, n, H * tq), lambda s, qi, kj: (qi[s], 0, 0))
    hbm = pl.BlockSpec(memory_space=pl.ANY)
    x_shapes, x_sems = _exchange_shapes(gather, scatter) if n_x else ([], [])
    gs = pltpu.PrefetchScalarGridSpec(
        num_scalar_prefetch=2, grid=(n_steps,),
        in_specs=[pl.BlockSpec((1, H, tq, DK), lambda s, qi, kj: (qi[s], 0, 0, 0)), cols(MLA_DT),
                  pl.BlockSpec((tk, DK), lambda s, qi, kj: (kj[s], 0)),
                  pl.BlockSpec((None, MLA_DT, tk), lambda s, qi, kj: (kj[s], 0, 0)),
                  cols(MLA_C), cols(1), cols(1)] + [hbm] * n_x,
        out_specs=[cols(MLA_DT), hbm, hbm] + [hbm] * n_x,
        scratch_shapes=[pltpu.VMEM((T // tk, MLA_DT, tk), F32), pltpu.VMEM((T // tk, MLA_C, tk), F32),
                        pltpu.SemaphoreType.DMA((2,))] + x_sems)
    res = pl.pallas_call(
        body, name="mla_attn_bwd", grid_spec=gs,
        out_shape=[S((nq, MLA_DT, H * tq), F32), S((T // tk, MLA_DT, tk), F32),
                   S((T // tk, MLA_C, tk), F32)] + x_shapes,
        compiler_params=_params("arbitrary"))(qi, kj, qs, qst, kc, kct, dol, lse, delta, *gather, *scatter)
    return res[0], res[1], res[2], res[3:]


def _mla_uv_fwd(olat, w_uv):
    nq, C, cols = olat.shape
    H = w_uv.shape[0]
    tq = cols // H

    def body(ol_ref, wuv_ref, o_ref):
        for hd in range(H):
            o_ref[:, MLA_V * hd:MLA_V * (hd + 1)] = _mx(_dot(ol_ref[0, :, tq * hd:tq * (hd + 1)], wuv_ref[hd], TN))

    return pl.pallas_call(
        body, name="mla_uv_fwd", grid=(nq,),
        in_specs=[pl.BlockSpec((1, C, cols), lambda i: (i, 0, 0)), pl.BlockSpec(w_uv.shape, lambda i: (0, 0, 0))],
        out_specs=pl.BlockSpec((tq, H * MLA_V), lambda i: (i, 0)), out_shape=S((nq * tq, H * MLA_V), _MXU_DTYPE),
        compiler_params=_params("parallel"))(olat, w_uv)


def _mla_uv_bwd(do, olat, w_uv):
    nq, C, cols = olat.shape
    H = w_uv.shape[0]
    tq = cols // H

    def body(do_ref, ol_ref, wuv_ref, dol_ref, dl_ref, dw_ref):
        @pl.when(pl.program_id(0) == 0)
        def _():
            dw_ref[...] = jnp.zeros_like(dw_ref)

        dov = do_ref[...]
        for hd in range(H):
            cs = slice(tq * hd, tq * (hd + 1))
            doh = _mx(dov[:, MLA_V * hd:MLA_V * (hd + 1)])
            ol = ol_ref[0, :, cs]
            dol = _dot(wuv_ref[hd], doh, NT)
            dol_ref[0, :, cs] = _mx(dol)
            dl_ref[0, :, cs] = jnp.sum(dol * ol.astype(F32), axis=0, keepdims=True)
            dw_ref[hd] += _dot(ol, doh)

    blk = lambda n: pl.BlockSpec((1, n, cols), lambda i: (i, 0, 0))
    return pl.pallas_call(
        body, name="mla_uv_bwd", grid=(nq,),
        in_specs=[pl.BlockSpec((tq, H * MLA_V), lambda i: (i, 0)), blk(C), pl.BlockSpec(w_uv.shape, lambda i: (0, 0, 0))],
        out_specs=[blk(C), blk(1), pl.BlockSpec(w_uv.shape, lambda i: (0, 0, 0))],
        out_shape=[S(olat.shape, _MXU_DTYPE), S((nq, 1, cols), F32), S(w_uv.shape, F32)],
        compiler_params=_params("arbitrary"))(do, olat, w_uv)


def _mla_pre_bwd(dqs, dkc, dv, h, x, dres, cq, qn, cos, sin, w_in, g_q, g_kv, w_uq_n, w_uq_r, w_uk):
    nq, DK, cols = dqs.shape
    H = w_uk.shape[0]
    tm = cols // H
    T = nq * tm
    tk = dv.shape[2]
    n_in = w_in.shape[1]

    def body(dqs_ref, dkc_ref, dv_ref, h_ref, x_ref, dres_ref, cq_ref, qn_ref, cos_ref, sin_ref,
             win_ref, gq_ref, gkv_ref, wn_ref, wr_ref, wuk_ref,
             gx_ref, dwin_ref, dwn_ref, dwr_ref, dwuk_ref, dgq_ref, dgkv_ref, dqn_sc, dqr_sc, dh_sc):
        @pl.when(pl.program_id(0) == 0)
        def _():
            for r in (dwin_ref, dwn_ref, dwr_ref, dwuk_ref, dgq_ref, dgkv_ref):
                r[...] = jnp.zeros_like(r)

        cos_v, sin_v = _tile_heads(cos_ref[...]), _tile_heads(sin_ref[...])
        qnb = qn_ref[...]
        for hd in range(H):
            cs = slice(tm * hd, tm * (hd + 1))
            dql = _mx(dqs_ref[0, 0:MLA_C, cs])
            dqn_sc[:, MLA_NOPE * hd:MLA_NOPE * (hd + 1)] = _dot(dql, wuk_ref[hd], TN)
            dwuk_ref[hd] += _dot(dql, qnb[:, MLA_NOPE * hd:MLA_NOPE * (hd + 1)])
            dqr_sc[MLA_ROPE * hd:MLA_ROPE * (hd + 1), :] = dqs_ref[0, MLA_C:MLA_C + MLA_ROPE, cs]
        dqr = dqr_sc[...].T
        dqrb = _mx(dqr * cos_v + _swap_halves_groups(dqr * sin_v))
        dqnb = _mx(dqn_sc[...])
        cq = cq_ref[...]
        dwn_ref[...] += _dot(cq, dqnb, TN)
        dwr_ref[...] += _dot(cq, dqrb, TN)
        dcq = _dot(dqnb, wn_ref[...], NT) + _dot(dqrb, wr_ref[...], NT)
        hv = h_ref[...]
        dxq, dgq = _rms_bwd_math(dcq, hv[:, :MLA_QR], gq_ref[...])
        dgq_ref[...] += dgq
        dckv = (dkc_ref[0:MLA_C, :] + dv_ref[...]).T
        dxkv, dgkv = _rms_bwd_math(dckv, hv[:, MLA_QR:MLA_QR + MLA_C], gkv_ref[...])
        dgkv_ref[...] += dgkv
        dkr = jnp.concatenate([dkc_ref[MLA_C:, :], jnp.zeros((128 - MLA_ROPE, tm), F32)], axis=0).T[:, :MLA_ROPE]
        dkr_raw = dkr * cos_v[:, :MLA_ROPE] + _swap_halves_64(dkr * sin_v[:, :MLA_ROPE])
        dh_sc[:, 0:MLA_QR] = dxq
        dh_sc[:, MLA_QR:MLA_QR + MLA_C] = dxkv
        dh_sc[:, MLA_QR + MLA_C:] = dkr_raw
        dhb = _mx(dh_sc[...])
        gx_ref[...] = dres_ref[...] + _dot(dhb, win_ref[...], NT)
        dwin_ref[...] += _dot(_mx(x_ref[...]), dhb, TN)

    full = lambda shp: pl.BlockSpec(shp, lambda i: (0,) * len(shp))
    rows = lambda n: pl.BlockSpec((tm, n), lambda i: (i, 0))
    return pl.pallas_call(
        body, name="mla_pre_bwd", grid=(nq,),
        in_specs=[pl.BlockSpec((1, DK, cols), lambda i: (i, 0, 0)),
                  pl.BlockSpec((None, DK, tm), lambda i: (i * tm // tk, 0, i % (tk // tm))),
                  pl.BlockSpec((None, MLA_C, tm), lambda i: (i * tm // tk, 0, i % (tk // tm))), rows(n_in),
                  rows(D_MODEL), rows(D_MODEL), rows(MLA_QR), rows(H * MLA_NOPE), rows(ROPE_TABLE_W), rows(ROPE_TABLE_W),
                  full(w_in.shape), full(g_q.shape), full(g_kv.shape), full(w_uq_n.shape), full(w_uq_r.shape),
                  full(w_uk.shape)],
        out_specs=[rows(D_MODEL), full(w_in.shape), full(w_uq_n.shape), full(w_uq_r.shape), full(w_uk.shape),
                   full(g_q.shape), full(g_kv.shape)],
        out_shape=[S((T, D_MODEL), F32), S(w_in.shape, F32), S(w_uq_n.shape, F32), S(w_uq_r.shape, F32),
                   S(w_uk.shape, F32), S(g_q.shape, F32), S(g_kv.shape, F32)],
        scratch_shapes=[pltpu.VMEM((tm, H * MLA_NOPE), F32), pltpu.VMEM((H * MLA_ROPE, tm), F32),
                        pltpu.VMEM((tm, n_in), F32)],
        compiler_params=_params("arbitrary"))(dqs, dkc, dv, h, x, dres, cq, qn, cos, sin, w_in, g_q, g_kv,
                                              w_uq_n, w_uq_r, w_uk)


def _proj_ln_fwd(a, w, xres, g, b, *, name, tm=512):
    T, K = a.shape
    tm = min(tm, T)

    def body(a_ref, w_ref, x_ref, g_ref, b_ref, xo_ref, xob_ref, xh_ref, rs_ref):
        z = ALPHA * x_ref[...] + _dot(a_ref[...], w_ref[...])
        xo, xhat, rstd = _ln_fwd_math(z, g_ref[...], b_ref[...])
        xo_ref[...] = xo
        xob_ref[...] = _mx(xo)
        xh_ref[...] = xhat
        rs_ref[...] = rstd

    rows = lambda n: pl.BlockSpec((tm, n), lambda i: (i, 0))
    full = lambda shp: pl.BlockSpec(shp, lambda i: (0,) * len(shp))
    return pl.pallas_call(
        body, name=name, grid=(T // tm,),
        in_specs=[rows(K), full(w.shape), rows(D_MODEL), full(g.shape), full(b.shape)],
        out_specs=[rows(D_MODEL), rows(D_MODEL), rows(D_MODEL), rows(1)],
        out_shape=[S((T, D_MODEL), F32), S((T, D_MODEL), _MXU_DTYPE), S((T, D_MODEL), F32), S((T, 1), F32)],
        compiler_params=_params("parallel"))(a, w, xres, g, b)


def _proj_ln_bwd(dxo, xhat, rstd, g, a, w, *, name, tm=512):
    T, K = a.shape
    tm = min(tm, T)

    def body(dxo_ref, xh_ref, rs_ref, g_ref, a_ref, w_ref, dres_ref, da_ref, dw_ref, dg_ref, db_ref):
        @pl.when(pl.program_id(0) == 0)
        def _():
            for r in (dw_ref, dg_ref, db_ref):
                r[...] = jnp.zeros_like(r)

        dz, dg, db = _ln_bwd_math(dxo_ref[...], xh_ref[...], rs_ref[...], g_ref[...])
        dg_ref[...] += dg
        db_ref[...] += db
        dres_ref[...] = ALPHA * dz
        dzb = _mx(dz)
        da_ref[...] = _dot(dzb, w_ref[...], NT)
        dw_ref[...] += _dot(a_ref[...], dzb, TN)

    rows = lambda n: pl.BlockSpec((tm, n), lambda i: (i, 0))
    full = lambda shp: pl.BlockSpec(shp, lambda i: (0,) * len(shp))
    return pl.pallas_call(
        body, name=name, grid=(T // tm,),
        in_specs=[rows(D_MODEL), rows(D_MODEL), rows(1), full(g.shape), rows(K), full(w.shape)],
        out_specs=[rows(D_MODEL), rows(K), full(w.shape), full(g.shape), full(g.shape)],
        out_shape=[S((T, D_MODEL), F32), S((T, K), F32), S(w.shape, F32), S(g.shape, F32), S(g.shape, F32)],
        compiler_params=_params("arbitrary"))(dxo, xhat, rstd, g, a, w)


def _mlp_fwd(xb, xres, w_up, w_dn, layer, g, b, *, tm=1024):
    T = xb.shape[0]
    tm = min(tm, T)
    nj, _, _, fc = w_up.shape

    def body(xb_ref, x_ref, wu_ref, wd_ref, g_ref, b_ref, u_ref, xo_ref, xob_ref, xh_ref, rs_ref, acc):
        j = pl.program_id(1)

        @pl.when(j == 0)
        def _():
            acc[...] = ALPHA * x_ref[...]

        u = _dot(xb_ref[...], wu_ref[...])
        u_ref[...] = _mx(u)
        r = jnp.maximum(u, 0.0)
        acc[...] += _dot(_mx(r * r), wd_ref[...])

        @pl.when(j == nj - 1)
        def _():
            xo, xhat, rstd = _ln_fwd_math(acc[...], g_ref[...], b_ref[...])
            xo_ref[...] = xo
            xob_ref[...] = _mx(xo)
            xh_ref[...] = xhat
            rs_ref[...] = rstd

    rows = lambda n: pl.BlockSpec((tm, n), lambda i, j: (i, 0))
    full = lambda shp: pl.BlockSpec(shp, lambda i, j: (0,) * len(shp))
    return pl.pallas_call(
        body, name=f"mlp_fwd_{layer}", grid=(T // tm, nj),
        in_specs=[rows(D_MODEL), rows(D_MODEL),
                  pl.BlockSpec((None, None, D_MODEL, fc), lambda i, j: (j, layer, 0, 0)),
                  pl.BlockSpec((None, None, fc, D_MODEL), lambda i, j: (j, layer, 0, 0)),
                  full(g.shape), full(b.shape)],
        out_specs=[pl.BlockSpec((tm, fc), lambda i, j: (i, j)), rows(D_MODEL), rows(D_MODEL), rows(D_MODEL), rows(1)],
        out_shape=[S((T, nj * fc), _MXU_DTYPE), S((T, D_MODEL), F32), S((T, D_MODEL), _MXU_DTYPE),
                   S((T, D_MODEL), F32), S((T, 1), F32)],
        scratch_shapes=[pltpu.VMEM((tm, D_MODEL), F32)],
        compiler_params=_params("parallel", "arbitrary"))(xb, xres, w_up, w_dn, g, b)


def _mlp_bwd_dx(dxo, xhat, rstd, g, u, w_up, w_dn, layer, *, tm=1024):
    T = dxo.shape[0]
    tm = min(tm, T)
    nj, _, _, fc = w_up.shape

    def body(dxo_ref, xh_ref, rs_ref, g_ref, u_ref, wu_ref, wd_ref, dx_ref, du_ref, dyb_ref, dg_ref, db_ref, acc, dy_sc):
        i, j = pl.program_id(0), pl.program_id(1)

        @pl.when((i == 0) & (j == 0))
        def _():
            dg_ref[...] = jnp.zeros_like(dg_ref)
            db_ref[...] = jnp.zeros_like(db_ref)

        @pl.when(j == 0)
        def _():
            dz, dg, db = _ln_bwd_math(dxo_ref[...], xh_ref[...], rs_ref[...], g_ref[...])
            dg_ref[...] += dg
            db_ref[...] += db
            acc[...] = ALPHA * dz
            dy_sc[...] = _mx(dz)
            dyb_ref[...] = _mx(dz)

        r = jnp.maximum(u_ref[...].astype(F32), 0.0)
        da = _dot(dy_sc[...], wd_ref[...], NT)
        dub = _mx(da * (2.0 * r))
        du_ref[...] = dub
        acc[...] += _dot(dub, wu_ref[...], NT)

        @pl.when(j == nj - 1)
        def _():
            dx_ref[...] = acc[...]

    rows = lambda n: pl.BlockSpec((tm, n), lambda i, j: (i, 0))
    full = lambda shp: pl.BlockSpec(shp, lambda i, j: (0,) * len(shp))
    return pl.pallas_call(
        body, name=f"mlp_bwd_dx_{layer}", grid=(T // tm, nj),
        in_specs=[rows(D_MODEL), rows(D_MODEL), rows(1), full(g.shape), pl.BlockSpec((tm, fc), lambda i, j: (i, j)),
                  pl.BlockSpec((None, None, D_MODEL, fc), lambda i, j: (j, layer, 0, 0)),
                  pl.BlockSpec((None, None, fc, D_MODEL), lambda i, j: (j, layer, 0, 0))],
        out_specs=[rows(D_MODEL), pl.BlockSpec((tm, fc), lambda i, j: (i, j)), rows(D_MODEL), full(g.shape), full(g.shape)],
        out_shape=[S((T, D_MODEL), F32), S((T, nj * fc), _MXU_DTYPE), S((T, D_MODEL), _MXU_DTYPE),
                   S(g.shape, F32), S(g.shape, F32)],
        scratch_shapes=[pltpu.VMEM((tm, D_MODEL), F32), pltpu.VMEM((tm, D_MODEL), _MXU_DTYPE)],
        compiler_params=_params("arbitrary", "arbitrary"))(dxo, xhat, rstd, g, u, w_up, w_dn)


def _mlp_bwd_dw(u, dyb, xinb, du, layer, *, nj, tm=1024):
    T = u.shape[0]
    tm = min(tm, T)
    fc = u.shape[1] // nj

    def body(u_ref, dy_ref, x_ref, du_ref, gd_ref, gu_ref):
        @pl.when(pl.program_id(1) == 0)
        def _():
            gd_ref[...] = jnp.zeros_like(gd_ref)
            gu_ref[...] = jnp.zeros_like(gu_ref)

        r = jnp.maximum(u_ref[...].astype(F32), 0.0)
        gd_ref[...] += _dot(_mx(r * r), dy_ref[...], TN)
        gu_ref[...] += _dot(x_ref[...], du_ref[...], TN)

    return pl.pallas_call(
        body, name=f"mlp_bwd_dw_{layer}", grid=(nj, T // tm),
        in_specs=[pl.BlockSpec((tm, fc), lambda j, i: (i, j)), pl.BlockSpec((tm, D_MODEL), lambda j, i: (i, 0)),
                  pl.BlockSpec((tm, D_MODEL), lambda j, i: (i, 0)), pl.BlockSpec((tm, fc), lambda j, i: (i, j))],
        out_specs=[pl.BlockSpec((None, fc, D_MODEL), lambda j, i: (j, 0, 0)),
                   pl.BlockSpec((None, D_MODEL, fc), lambda j, i: (j, 0, 0))],
        out_shape=[S((nj, fc, D_MODEL), F32), S((nj, D_MODEL, fc), F32)],
        compiler_params=_params("parallel", "arbitrary"))(u, dyb, xinb, du)


SWA_GROUP = SWA_QH // SWA_KVH
SWA_ROWS = SWA_GROUP * SWA_BLOCK


def _swa_heads(a, kh):
    return jnp.concatenate([a[:, SWA_D * (kh * SWA_GROUP + g):SWA_D * (kh * SWA_GROUP + g + 1)]
                            for g in range(SWA_GROUP)], axis=0)


def _swa_operands(q, kvp, kvc, kh):
    dkv = SWA_KVH * SWA_D
    qg = _swa_heads(q, kh)
    kb = jnp.concatenate([kvp[:, SWA_D * kh:SWA_D * (kh + 1)], kvc[:, SWA_D * kh:SWA_D * (kh + 1)]], axis=0)
    vb = jnp.concatenate([kvp[:, dkv + SWA_D * kh:dkv + SWA_D * (kh + 1)],
                          kvc[:, dkv + SWA_D * kh:dkv + SWA_D * (kh + 1)]], axis=0)
    return qg, kb, vb, _dot(kb, qg, NT)


def _swa_softmax(s_raw, bias_ref, sink_ref, n, kh):
    cols = slice(kh * SWA_ROWS, (kh + 1) * SWA_ROWS)
    s = s_raw * (SWA_D ** -0.5) + bias_ref[jnp.minimum(n, 1), :, cols]
    sink = sink_ref[:, cols]
    m = jnp.maximum(jnp.max(s, axis=0, keepdims=True), sink)
    p, ps = jnp.exp(s - m), jnp.exp(sink - m)
    inv = 1.0 / (jnp.sum(p, axis=0, keepdims=True) + ps)
    return p * inv, ps * inv


def _swa_attn_fwd(qkv, bias, sinks):
    T = qkv.shape[0]
    blk = SWA_BLOCK
    nb = T // blk
    dq, dkv = SWA_QH * SWA_D, SWA_KVH * SWA_D

    def body(q_ref, kvp_ref, kvc_ref, bias_ref, sink_ref, o_ref):
        n = pl.program_id(0)
        q, kvp, kvc = q_ref[...], kvp_ref[...], kvc_ref[...]
        nxt = _swa_operands(q, kvp, kvc, 0)
        for kh in range(SWA_KVH):
            _, _, vb, s_raw = nxt
            if kh + 1 < SWA_KVH:
                nxt = _swa_operands(q, kvp, kvc, kh + 1)
            p, _ = _swa_softmax(s_raw, bias_ref, sink_ref, n, kh)
            og = _mx(_dot(_mx(p), vb, TN))
            for g in range(SWA_GROUP):
                hd = kh * SWA_GROUP + g
                o_ref[:, SWA_D * hd:SWA_D * (hd + 1)] = og[blk * g:blk * (g + 1), :]

    return pl.pallas_call(
        body, name="swa_attn_fwd", grid=(nb,),
        in_specs=[pl.BlockSpec((blk, dq), lambda n: (n, 0)),
                  pl.BlockSpec((blk, 2 * dkv), lambda n: (jnp.maximum(n - 1, 0), dq // (2 * dkv))),
                  pl.BlockSpec((blk, 2 * dkv), lambda n: (n, dq // (2 * dkv))),
                  pl.BlockSpec(bias.shape, lambda n: (0, 0, 0)), pl.BlockSpec(sinks.shape, lambda n: (0, 0))],
        out_specs=pl.BlockSpec((blk, dq), lambda n: (n, 0)), out_shape=S((T, dq), _MXU_DTYPE),
        compiler_params=_params("parallel"))(qkv, qkv, qkv, bias, sinks)


def _swa_attn_bwd(qkv, ob, do, bias, sinks):
    T = qkv.shape[0]
    blk = SWA_BLOCK
    nb = T // blk
    dq, dkv = SWA_QH * SWA_D, SWA_KVH * SWA_D

    def body(q_ref, kvp_ref, kvc_ref, o_ref, do_ref, bias_ref, sink_ref, dqkv_ref, dbias_ref, dsink_ref, carry):
        st = pl.program_id(0)
        n = nb - 1 - st

        @pl.when(st == 0)
        def _():
            carry[...] = jnp.zeros_like(carry)
            dbias_ref[...] = jnp.zeros_like(dbias_ref)
            dsink_ref[...] = jnp.zeros_like(dsink_ref)

        q, kvp, kvc = q_ref[...], kvp_ref[...], kvc_ref[...]
        ov, dov = o_ref[...], do_ref[...]
        ones = jnp.ones((8, SWA_D), F32)
        def operands(kh):
            qg, kb, vb, s_raw = _swa_operands(q, kvp, kvc, kh)
            dog = _swa_heads(dov, kh)
            dl = _dot(ones, dog * _swa_heads(ov, kh).astype(F32), NT, lax.Precision.HIGHEST)[0:1]
            dogb = _mx(dog)
            return qg, kb, s_raw, dl, dogb, _dot(vb, dogb, NT)

        nxt = operands(0)
        for kh in range(SWA_KVH):
            cols = slice(kh * SWA_ROWS, (kh + 1) * SWA_ROWS)
            qg, kb, s_raw, dl, dogb, dp = nxt
            if kh + 1 < SWA_KVH:
                nxt = operands(kh + 1)
            p, ps = _swa_softmax(s_raw, bias_ref, sink_ref, n, kh)
            ds = p * (dp - dl)
            dbias_ref[:, cols] += ds
            dsink_ref[0:1, cols] += -ps * dl
            dsb = _mx(ds * (SWA_D ** -0.5))
            dqg = _mx(_dot(dsb, kb, TN))
            for g in range(SWA_GROUP):
                hd = kh * SWA_GROUP + g
                dqkv_ref[:, SWA_D * hd:SWA_D * (hd + 1)] = dqg[blk * g:blk * (g + 1), :]
            dkb = _dot(dsb, qg)
            dvb = _dot(_mx(p), dogb)
            ko, vo = SWA_D * kh, dkv + SWA_D * kh
            dqkv_ref[:, dq + ko:dq + ko + SWA_D] = _mx(dkb[blk:, :] + carry[:, ko:ko + SWA_D])
            dqkv_ref[:, dq + vo:dq + vo + SWA_D] = _mx(dvb[blk:, :] + carry[:, vo:vo + SWA_D])
            carry[:, ko:ko + SWA_D] = dkb[:blk, :]
            carry[:, vo:vo + SWA_D] = dvb[:blk, :]

    rev = lambda s: nb - 1 - s
    return pl.pallas_call(
        body, name="swa_attn_bwd", grid=(nb,),
        in_specs=[pl.BlockSpec((blk, dq), lambda s: (rev(s), 0)),
                  pl.BlockSpec((blk, 2 * dkv), lambda s: (jnp.maximum(rev(s) - 1, 0), dq // (2 * dkv))),
                  pl.BlockSpec((blk, 2 * dkv), lambda s: (rev(s), dq // (2 * dkv))),
                  pl.BlockSpec((blk, dq), lambda s: (rev(s), 0)), pl.BlockSpec((blk, dq), lambda s: (rev(s), 0)),
                  pl.BlockSpec(bias.shape, lambda s: (0, 0, 0)), pl.BlockSpec(sinks.shape, lambda s: (0, 0))],
        out_specs=[pl.BlockSpec((blk, dq + 2 * dkv), lambda s: (rev(s), 0)),
                   pl.BlockSpec(bias.shape[1:], lambda s: (0, 0)), pl.BlockSpec((8, sinks.shape[1]), lambda s: (0, 0))],
        out_shape=[S((T, dq + 2 * dkv), _MXU_DTYPE), S(bias.shape[1:], F32), S((8, sinks.shape[1]), F32)],
        scratch_shapes=[pltpu.VMEM((blk, 2 * dkv), F32)],
        compiler_params=_params("arbitrary"))(qkv, qkv, qkv, ob, do, bias, sinks)


def _t5_onehot():
    i = jnp.arange(SWA_BLOCK)
    j = jnp.arange(2 * SWA_BLOCK)
    n = jnp.maximum(i[:, None] + SWA_BLOCK - j[None, :], 0)
    max_exact = REL_BUCKETS // 2
    nf = jnp.maximum(n, 1).astype(F32)
    large = max_exact + (jnp.log(nf / max_exact) / math.log(REL_MAX_DIST / max_exact)
                         * (REL_BUCKETS - max_exact)).astype(jnp.int32)
    large = jnp.minimum(large, REL_BUCKETS - 1)
    bucket = jnp.where(n < max_exact, n, large).reshape(-1)
    return (bucket[None, :] == jnp.arange(REL_BUCKETS)[:, None]).astype(F32)


def _loss_head(y, target, *, tm=1024):
    T, D = y.shape
    tm = min(tm, T)

    def body(y_ref, t_ref, loss_ref, dy_ref):
        @pl.when(pl.program_id(0) == 0)
        def _():
            loss_ref[...] = jnp.zeros_like(loss_ref)

        d = y_ref[...] - t_ref[...]
        dy_ref[...] = d * (1.0 / D)
        rs = jnp.sum(d * d, axis=1, keepdims=True)
        loss_ref[...] += (0.5 / D) * jnp.sum(rs, axis=0, keepdims=True)

    rows = pl.BlockSpec((tm, D), lambda i: (i, 0))
    return pl.pallas_call(
        body, name="loss_head", grid=(T // tm,), in_specs=[rows, rows],
        out_specs=[pl.BlockSpec((1, 1), lambda i: (0, 0)), rows], out_shape=[S((1, 1), F32), S((T, D), F32)],
        compiler_params=_params("arbitrary"))(y, target)


def _exchange_copies(ins, outs, n_gather, send_sems, recv_sems, loc_sems):
    mx, my, mc = lax.axis_index("x"), lax.axis_index("y"), lax.axis_index("c")
    me = 4 * mx + 2 * my + mc
    copies = []
    for a in range(len(ins)):
        src = ins[a] if a < n_gather else ins[a].at[me]
        copies.append(pltpu.make_async_copy(src, outs[a].at[me], loc_sems.at[a]))
    for k in range(1, N_DEV):
        px, py, pc = mx ^ ((k >> 2) & 1), my ^ ((k >> 1) & 1), mc ^ (k & 1)
        peer = 4 * px + 2 * py + pc
        for a in range(len(ins)):
            src = ins[a] if a < n_gather else ins[a].at[peer]
            copies.append(pltpu.make_async_remote_copy(
                src_ref=src, dst_ref=outs[a].at[me], send_sem=send_sems.at[a, k - 1],
                recv_sem=recv_sems.at[a, k - 1], device_id=(px, py, pc), device_id_type=pl.DeviceIdType.MESH))
    return copies


def _exchange_shapes(gather, scatter):
    n_arr = len(gather) + len(scatter)
    out_shape = [S((N_DEV,) + tuple(g.shape), g.dtype) for g in gather] + [S(s.shape, s.dtype) for s in scatter]
    sems = [pltpu.SemaphoreType.DMA((n_arr, N_DEV - 1)), pltpu.SemaphoreType.DMA((n_arr, N_DEV - 1)),
            pltpu.SemaphoreType.DMA((n_arr,))]
    return out_shape, sems


def _exchange(gather, scatter, *, name):
    n_g = len(gather)
    n_arr = n_g + len(scatter)

    def body(*refs):
        copies = _exchange_copies(refs[:n_arr], refs[n_arr:2 * n_arr], n_g, *refs[2 * n_arr:])
        for cp in copies:
            cp.start()
        for cp in copies:
            cp.wait()

    hbm = pl.BlockSpec(memory_space=pl.ANY)
    out_shape, sems = _exchange_shapes(gather, scatter)
    return pl.pallas_call(
        body, name=name, in_specs=[hbm] * n_arr, out_specs=[hbm] * n_arr, out_shape=out_shape,
        scratch_shapes=sems)(*gather, *scatter)


def _adamw(parts, w, m, v, *, name, tr=256):
    R, C = w.shape
    tr = min(tr, R)
    assert R % tr == 0

    def body(p_ref, w_ref, m_ref, v_ref, g_ref, d_ref, nm_ref, nv_ref):
        g = p_ref[0]
        for k in range(1, N_DEV):
            g = g + p_ref[k]
        g_ref[...] = g
        m_new = ADAM_B1 * m_ref[...] + (1.0 - ADAM_B1) * g
        v_new = ADAM_B2 * v_ref[...] + (1.0 - ADAM_B2) * (g * g)
        m_hat = m_new / (1.0 - ADAM_B1 ** ADAM_STEP)
        v_hat = v_new / (1.0 - ADAM_B2 ** ADAM_STEP)
        d_ref[...] = -ADAM_LR * (m_hat / (jnp.sqrt(v_hat) + ADAM_EPS) + ADAM_WD * w_ref[...])
        nm_ref[...] = m_new
        nv_ref[...] = v_new

    rows = pl.BlockSpec((tr, C), lambda i: (i, 0))
    return pl.pallas_call(
        body, name=name, grid=(R // tr,),
        in_specs=[pl.BlockSpec((N_DEV, tr, C), lambda i: (0, i, 0)), rows, rows, rows],
        out_specs=[rows] * 4, out_shape=[S((R, C), F32)] * 4,
        compiler_params=_params("parallel"))(parts, w, m, v)


def _rows_of(n):
    return -(-n // LANES)


def _pack(pieces, total_rows, dtype, lead=()):
    out = []
    for p in pieces:
        flat = p.reshape(lead + (-1,)).astype(dtype)
        n = flat.shape[-1]
        pad = _rows_of(n) * LANES - n
        if pad:
            flat = jnp.pad(flat, [(0, 0)] * len(lead) + [(0, pad)])
        out.append(flat.reshape(lead + (-1, LANES)))
    used = sum(o.shape[-2] for o in out)
    if total_rows > used:
        out.append(jnp.zeros(lead + (total_rows - used, LANES), dtype))
    return jnp.concatenate(out, axis=len(lead))


def _unpack(buf, shapes, lead=()):
    res, r0 = [], 0
    for shp in shapes:
        n = int(np.prod(shp))
        nr = _rows_of(n)
        piece = lax.slice_in_dim(buf, r0, r0 + nr, axis=len(lead)).reshape(lead + (nr * LANES,))
        res.append(lax.slice_in_dim(piece, 0, n, axis=len(lead)).reshape(lead + tuple(shp)))
        r0 += nr
    return res


def _round_up(n, m):
    return -(-n // m) * m


BIG = ["mla_w_in", "mla_w_uq", "mla_w_uk", "mla_w_uv", "mla_w_o", "kv_w_shared", "swa_w_q", "swa_w_o",
       "mlp_w_up", "mlp_w_down"]
GAINS = ["mla_g_q", "mla_g_kv"]
SHARDED = BIG + GAINS
REPL = ["swa_sinks", "rel_bias", "ln_mix_g", "ln_mix_b", "ln_mlp_g", "ln_mlp_b"]
WEIGHTS = ["mla_w_in", "mla_g_q", "mla_g_kv", "mla_w_uq", "mla_w_uk", "mla_w_uv", "mla_w_o", "kv_w_shared",
           "swa_w_q", "swa_sinks", "swa_w_o", "rel_bias", "mlp_w_up", "mlp_w_down", "ln_mix_g", "ln_mix_b",
           "ln_mlp_g", "ln_mlp_b"]


def kernel(x, mla_w_in, mla_g_q, mla_g_kv, mla_w_uq, mla_w_uk, mla_w_uv, mla_w_o, kv_w_shared, swa_w_q, swa_sinks, swa_w_o, rel_bias, mlp_w_up, mlp_w_down, ln_mix_g, ln_mix_b, ln_mlp_g, ln_mlp_b, loss_target, m_mla_w_in, m_mla_g_q, m_mla_g_kv, m_mla_w_uq, m_mla_w_uk, m_mla_w_uv, m_mla_w_o, m_kv_w_shared, m_swa_w_q, m_swa_sinks, m_swa_w_o, m_rel_bias, m_mlp_w_up, m_mlp_w_down, m_ln_mix_g, m_ln_mix_b, m_ln_mlp_g, m_ln_mlp_b, v_mla_w_in, v_mla_g_q, v_mla_g_kv, v_mla_w_uq, v_mla_w_uk, v_mla_w_uv, v_mla_w_o, v_kv_w_shared, v_swa_w_q, v_swa_sinks, v_swa_w_o, v_rel_bias, v_mlp_w_up, v_mlp_w_down, v_ln_mix_g, v_ln_mix_b, v_ln_mlp_g, v_ln_mlp_b):
    args = dict(locals())
    W = {n: args[n] for n in WEIGHTS}
    M = {n: args["m_" + n] for n in WEIGHTS}
    V = {n: args["v_" + n] for n in WEIGHTS}
    T = x.shape[1]
    x2d = x.reshape(T, D_MODEL)
    tgt = loss_target.reshape(T, D_MODEL)
    H = MLA_HEADS

    SH = {"mla_w_in": (-1, mla_w_in.shape[-1]), "mla_w_uq": (-1, H * (MLA_NOPE + MLA_ROPE)),
          "mla_w_uk": (-1, H * MLA_NOPE), "mla_w_uv": (-1, H * MLA_V), "mla_w_o": (-1, D_MODEL),
          "kv_w_shared": (-1, kv_w_shared.shape[-1]), "swa_w_q": (-1, swa_w_q.shape[-1]), "swa_w_o": (-1, D_MODEL)}
    slab = lambda d, n: d[n].reshape(SH[n])
    bf = lambda a: a.astype(_MXU_DTYPE)
    gains_slab = lambda d: jnp.pad(jnp.concatenate([d["mla_g_q"], d["mla_g_kv"]], axis=1),
                                   ((0, 7), (0, 128 - d["mla_g_q"].shape[1] - d["mla_g_kv"].shape[1])))
    n_gq, n_gkv = mla_g_q.shape[1], mla_g_kv.shape[1]
    w_in_s, w_uq_s, w_uk_s, gains_all = _exchange(
        [bf(slab(W, "mla_w_in")), bf(slab(W, "mla_w_uq")), bf(slab(W, "mla_w_uk")), gains_slab(W)], [],
        name="gather_mla_in")
    later = [bf(slab(W, n)) for n in ("mla_w_uv", "mla_w_o", "kv_w_shared", "swa_w_q", "swa_w_o")]
    later += [bf(mlp_w_up), bf(mlp_w_down)]
    w_in = w_in_s.reshape(D_MODEL, -1)
    g_q = gains_all[:, 0, :n_gq].reshape(1, MLA_QR)
    g_kv = gains_all[:, 0, n_gq:n_gq + n_gkv].reshape(1, MLA_C)
    w_uq = w_uq_s.reshape(MLA_QR, H, MLA_NOPE + MLA_ROPE)
    w_uq_n = w_uq[:, :, :MLA_NOPE].reshape(MLA_QR, H * MLA_NOPE)
    w_uq_r = w_uq[:, :, MLA_NOPE:].reshape(MLA_QR, H * MLA_ROPE)
    w_uk = w_uk_s.reshape(MLA_C, H, MLA_NOPE).transpose(1, 0, 2)
    w_uk_t = w_uk.transpose(0, 2, 1)
    ln = lambda a, l: a[l].reshape(1, D_MODEL)

    half = MLA_ROPE // 2
    inv = ROPE_THETA ** (-jnp.arange(half, dtype=F32) / half)
    ang = jnp.arange(T, dtype=F32)[:, None] * inv[None, :]
    cos = jnp.tile(jnp.concatenate([jnp.cos(ang), jnp.cos(ang)], -1), (1, ROPE_TABLE_W // MLA_ROPE))
    sin = jnp.tile(jnp.concatenate([-jnp.sin(ang), jnp.sin(ang)], -1), (1, ROPE_TABLE_W // MLA_ROPE))

    h, kc, kct, qs, qst, cq, qn = _mla_pre_fwd(x2d, w_in, g_q, g_kv, w_uq_n, w_uq_r, w_uk_t, cos, sin)
    olat, lse, (w_uv_s, w_o_s, w_kv_s, w_q_s, w_o2_s, w_up, w_dn) = _mla_attn_fwd(qs, kc, kct, gather=later)
    w_uv = w_uv_s.reshape(MLA_C, H, MLA_V).transpose(1, 0, 2)
    w_o = w_o_s.reshape(H * MLA_V, D_MODEL)
    w_qkv = jnp.concatenate([w_q_s.reshape(D_MODEL, -1), w_kv_s.reshape(D_MODEL, -1)], axis=1)
    w_o2 = w_o2_s.reshape(SWA_QH * SWA_D, D_MODEL)
    o_mla = _mla_uv_fwd(olat, w_uv)
    x1, x1b, xh1, rs1 = _proj_ln_fwd(o_mla, w_o, x2d, ln(ln_mix_g, 0), ln(ln_mix_b, 0), name="mla_out_ln_fwd")
    u0, x2, x2b, xh2, rs2 = _mlp_fwd(x1b, x1, w_up, w_dn, 0, ln(ln_mlp_g, 0), ln(ln_mlp_b, 0))
    onehot = _t5_onehot()
    bias = _mm(rel_bias.T, onehot, name="rel_bias_expand", precision=lax.Precision.HIGHEST, tn=8192).reshape(
        SWA_QH * SWA_BLOCK, 2 * SWA_BLOCK).T
    key = jnp.arange(2 * SWA_BLOCK)[:, None]
    qry = jnp.arange(SWA_QH * SWA_BLOCK)[None, :] % SWA_BLOCK
    in_window = (key > qry) & (key <= qry + SWA_BLOCK)
    bias = jnp.stack([jnp.where(in_window & (key >= SWA_BLOCK), bias, -jnp.inf), jnp.where(in_window, bias, -jnp.inf)])
    sink_rows = jnp.repeat(swa_sinks.reshape(SWA_QH), SWA_BLOCK).reshape(1, SWA_QH * SWA_BLOCK)
    qkv = _mm(x2b, w_qkv, name="swa_qkv_fwd", out_dtype=_MXU_DTYPE, tm=1024, tn=512, tk=1024)
    o_swa = _swa_attn_fwd(qkv, bias, sink_rows)
    x3, x3b, xh3, rs3 = _proj_ln_fwd(o_swa, w_o2, x2, ln(ln_mix_g, 1), ln(ln_mix_b, 1), name="swa_out_ln_fwd")
    u1, x4, _, xh4, rs4 = _mlp_fwd(x3b, x3, w_up, w_dn, 1, ln(ln_mlp_g, 1), ln(ln_mlp_b, 1))
    loss_part, dx4 = _loss_head(x4, tgt)
    loss = lax.psum(loss_part[0, 0], AXES)

    nj = w_up.shape[0]
    dx3, du1, dy4b, dg_mlp1, db_mlp1 = _mlp_bwd_dx(dx4, xh4, rs4, ln(ln_mlp_g, 1), u1, w_up, w_dn, 1)
    g_dn1, g_up1 = _mlp_bwd_dw(u1, dy4b, x3b, du1, 1, nj=nj)
    dres3, do_swa, g_o2, dg_mix1, db_mix1 = _proj_ln_bwd(dx3, xh3, rs3, ln(ln_mix_g, 1), o_swa, w_o2,
                                                         name="swa_out_ln_bwd")
    dqkv, dbias, dsink = _swa_attn_bwd(qkv, o_swa, do_swa, bias, sink_rows)
    g_rel = _mm(onehot, dbias.T.reshape(SWA_QH, -1), name="rel_bias_grad", tb=True, precision=lax.Precision.HIGHEST,
                tk=8192)
    head_of_row = (jnp.arange(SWA_QH * SWA_BLOCK)[:, None] // SWA_BLOCK == jnp.arange(SWA_QH)[None, :]).astype(F32)
    g_sinks = _mm(dsink, head_of_row, name="sinks_grad", precision=lax.Precision.HIGHEST, tk=2048)[0:1]
    dx2 = _mm(dqkv, w_qkv, name="swa_qkv_bwd_dx", tb=True, add=dres3, tm=1024, tn=1024, tk=1536)
    g_qkv = _mm(x2b, dqkv, name="swa_qkv_bwd_dw", ta=True, tm=1024, tn=512, tk=1024)
    dx1, du0, dy2b, dg_mlp0, db_mlp0 = _mlp_bwd_dx(dx2, xh2, rs2, ln(ln_mlp_g, 0), u0, w_up, w_dn, 0)
    g_dn0, g_up0 = _mlp_bwd_dw(u0, dy2b, x1b, du0, 0, nj=nj)
    dres1, do_mla, g_o, dg_mix0, db_mix0 = _proj_ln_bwd(dx1, xh1, rs1, ln(ln_mix_g, 0), o_mla, w_o,
                                                        name="mla_out_ln_bwd")
    dol, delta, g_uv = _mla_uv_bwd(do_mla, olat, w_uv)
    repl_grads = {
        "swa_sinks": g_sinks, "rel_bias": g_rel,
        "ln_mix_g": jnp.concatenate([dg_mix0, dg_mix1], 0), "ln_mix_b": jnp.concatenate([db_mix0, db_mix1], 0),
        "ln_mlp_g": jnp.concatenate([dg_mlp0, dg_mlp1], 0), "ln_mlp_b": jnp.concatenate([db_mlp0, db_mlp1], 0),
    }
    repl_rows = _round_up(sum(_rows_of(W[n].size) for n in REPL), 8)
    r_part = _pack([repl_grads[n] for n in REPL], repl_rows, F32)
    by_dev = lambda g: g.reshape((N_DEV, g.shape[0] // N_DEV) + g.shape[1:])
    early = [by_dev(g_o2), by_dev(g_qkv), g_up0, g_up1, g_dn0, g_dn1, by_dev(g_o),
             by_dev(g_uv.transpose(1, 0, 2).reshape(MLA_C, H * MLA_V))]
    dqs, dkc, dv, (r_all, p_o2, p_qkv, p_up0, p_up1, p_dn0, p_dn1, p_o, p_uv) = _mla_attn_bwd(
        qs, qst, kc, kct, dol, lse, delta, gather=[r_part], scatter=early)
    grad_x, g_in, g_uq_n, g_uq_r, g_uk, g_gq, g_gkv = _mla_pre_bwd(
        dqs, dkc, dv, h, x2d, dres1, cq, qn, cos, sin, w_in, g_q, g_kv, w_uq_n, w_uq_r, w_uk)
    g_uq = jnp.concatenate([g_uq_n.reshape(MLA_QR, H, MLA_NOPE), g_uq_r.reshape(MLA_QR, H, MLA_ROPE)], -1)
    g_gains = jnp.pad(jnp.concatenate([g_gq.reshape(N_DEV, n_gq), g_gkv.reshape(N_DEV, n_gkv)], axis=1)[:, None, :],
                      ((0, 0), (0, 7), (0, 128 - n_gq - n_gkv)))
    p_in, p_uq, p_uk, p_gains = _exchange(
        [], [by_dev(g_in), by_dev(g_uq.reshape(MLA_QR, -1)), by_dev(g_uk.transpose(1, 0, 2).reshape(MLA_C, -1)),
             g_gains], name="exchange_mla_in_grads")

    res = {}

    def adam(name, parts, names, to_slab, from_slab):
        out = _adamw(parts, to_slab(W), to_slab(M), to_slab(V), name="adamw_" + name)
        for k in range(4):
            for n, a in zip(names, from_slab(out[k])):
                res[(k, n)] = a.reshape(W[n].shape)

    one = lambda n: (lambda d: slab(d, n))
    adam("swa_w_o", p_o2, ["swa_w_o"], one("swa_w_o"), lambda s: [s])
    dq_cols = SWA_QH * SWA_D
    adam("swa_qkv", p_qkv, ["swa_w_q", "kv_w_shared"],
         lambda d: jnp.concatenate([slab(d, "swa_w_q"), slab(d, "kv_w_shared")], axis=1),
         lambda s: [s[:, :dq_cols], s[:, dq_cols:]])
    for name, parts in (("mlp_w_up", (p_up0, p_up1)), ("mlp_w_down", (p_dn0, p_dn1))):
        per_layer = [_adamw(parts[l], W[name][l], M[name][l], V[name][l], name=f"adamw_{name}_{l}") for l in range(DEPTH)]
        for k in range(4):
            res[(k, name)] = jnp.stack([per_layer[l][k] for l in range(DEPTH)], axis=0)
    adam("mla_w_o", p_o, ["mla_w_o"], one("mla_w_o"), lambda s: [s])
    adam("mla_w_uv", p_uv, ["mla_w_uv"], one("mla_w_uv"), lambda s: [s])
    adam("mla_w_in", p_in, ["mla_w_in"], one("mla_w_in"), lambda s: [s])
    adam("mla_w_uq", p_uq, ["mla_w_uq"], one("mla_w_uq"), lambda s: [s])
    adam("mla_w_uk", p_uk, ["mla_w_uk"], one("mla_w_uk"), lambda s: [s])
    adam("mla_gains", p_gains, ["mla_g_q", "mla_g_kv"], gains_slab,
         lambda s: [s[0:1, :n_gq], s[0:1, n_gq:n_gq + n_gkv]])
    adam("replicated", r_all, REPL, lambda d: _pack([d[n] for n in REPL], repl_rows, F32),
         lambda s: _unpack(s, [W[n].shape for n in REPL]))
    return (loss, grad_x.reshape(x.shape), *[res[(k, n)] for k in range(4) for n in WEIGHTS])
```

```python
import math

import numpy as np
import jax
import jax.numpy as jnp
from jax import lax
from jax.experimental import pallas as pl
from jax.experimental.pallas import tpu as pltpu

F32 = jnp.float32
_MXU_DTYPE = jnp.bfloat16

D_MODEL = 1024
DEPTH = 2
MLA_HEADS = 8
MLA_NOPE = 128
MLA_ROPE = 64
MLA_V = 128
MLA_QR = 384
MLA_C = 256
MLA_DK = 384
MLA_DT = MLA_C + MLA_ROPE
ROPE_THETA = 10000.0
SWA_QH = 16
SWA_KVH = 4
SWA_D = 64
SWA_BLOCK = 128
REL_BUCKETS = 32
REL_MAX_DIST = 128
D_FF = 4096
LN_EPS = 1e-5
RMS_EPS = 1e-6
ALPHA = (2 * DEPTH) ** 0.25
ADAM_LR, ADAM_B1, ADAM_B2, ADAM_EPS, ADAM_WD, ADAM_STEP = 0.001, 0.9, 0.999, 1e-08, 0.01, 10

N_DEV = 8
AXES = ("x", "y", "c")
V7X_VMEM_BYTES = 64 * 1024 * 1024
VMEM_LIMIT = V7X_VMEM_BYTES - 8 * 1024 * 1024
LANES = 1024
ATT_TQ = 512
ATT_TK = 512
ATT_HEAD_GROUP = 1
ATT_FWD_HEAD_GROUP = 2

NT = (((1,), (1,)), ((), ()))
TN = (((0,), (0,)), ((), ()))
S = jax.ShapeDtypeStruct


def _params(*sem, vmem=VMEM_LIMIT):
    return pltpu.CompilerParams(dimension_semantics=sem, vmem_limit_bytes=vmem)


def _dot(a, b, dims=None, precision=None):
    if dims is None:
        return jnp.dot(a, b, preferred_element_type=F32, precision=precision)
    return lax.dot_general(a, b, dims, preferred_element_type=F32, precision=precision)


def _mx(v):
    return v.astype(_MXU_DTYPE)


ROPE_TABLE_W = 128


def _tile_heads(t):
    return jnp.concatenate([t] * (MLA_HEADS * MLA_ROPE // ROPE_TABLE_W), axis=1)


def _swap_halves_64(v):
    return jnp.concatenate([v[:, 32:], v[:, :32]], axis=-1)


def _swap_halves_groups(v):
    n = v.shape[-1]
    lane = lax.broadcasted_iota(jnp.int32, v.shape, 1)
    return jnp.where(lane % 64 < 32, pltpu.roll(v, n - 32, 1), pltpu.roll(v, 32, 1))


def _mm(a, b, *, name, ta=False, tb=False, add=None, out_dtype=F32, tm=512, tn=512, tk=512, precision=None):
    M, K = (a.shape[1], a.shape[0]) if ta else a.shape
    N = b.shape[0] if tb else b.shape[1]
    tm, tn, tk = min(tm, M), min(tn, N), min(tk, K)
    assert M % tm == 0 and N % tn == 0 and K % tk == 0, (M, N, K, tm, tn, tk)
    nk = K // tk
    dims = (((0 if ta else 1,), (1 if tb else 0,)), ((), ()))
    has_add = add is not None

    def body(*refs):
        if has_add:
            a_ref, b_ref, add_ref, o_ref, acc = refs
        else:
            a_ref, b_ref, o_ref, acc = refs
        k = pl.program_id(2)
        av, bv = a_ref[...], b_ref[...]
        if precision is None:
            av, bv = _mx(av), _mx(bv)
        part = _dot(av, bv, dims, precision)
        if nk == 1:
            o_ref[...] = (part + add_ref[...] if has_add else part).astype(out_dtype)
            return

        @pl.when(k == 0)
        def _():
            acc[...] = add_ref[...] if has_add else jnp.zeros_like(acc)

        acc[...] += part

        @pl.when(k == nk - 1)
        def _():
            o_ref[...] = acc[...].astype(out_dtype)

    a_spec = pl.BlockSpec((tk, tm), lambda i, j, k: (k, i)) if ta else pl.BlockSpec((tm, tk), lambda i, j, k: (i, k))
    b_spec = pl.BlockSpec((tn, tk), lambda i, j, k: (j, k)) if tb else pl.BlockSpec((tk, tn), lambda i, j, k: (k, j))
    in_specs = [a_spec, b_spec]
    args = [a, b]
    if has_add:
        in_specs.append(pl.BlockSpec((tm, tn), lambda i, j, k: (i, j)))
        args.append(add)
    return pl.pallas_call(
        body, name=name, grid=(M // tm, N // tn, nk), in_specs=in_specs,
        out_specs=pl.BlockSpec((tm, tn), lambda i, j, k: (i, j)), out_shape=S((M, N), out_dtype),
        scratch_shapes=[pltpu.VMEM((tm, tn), F32)],
        compiler_params=_params("parallel", "parallel", "arbitrary"))(*args)


def _ln_fwd_math(z, g, b):
    mu = jnp.mean(z, axis=-1, keepdims=True)
    zc = z - mu
    var = jnp.mean(zc * zc, axis=-1, keepdims=True)
    rstd = lax.rsqrt(var + LN_EPS)
    xhat = zc * rstd
    return xhat * g + b, xhat, rstd


def _ln_bwd_math(dxo, xhat, rstd, g):
    dxh = dxo * g
    m1 = jnp.mean(dxh, axis=-1, keepdims=True)
    m2 = jnp.mean(dxh * xhat, axis=-1, keepdims=True)
    dz = rstd * (dxh - m1 - xhat * m2)
    dg = jnp.sum(dxo * xhat, axis=0, keepdims=True)
    db = jnp.sum(dxo, axis=0, keepdims=True)
    return dz, dg, db


def _rms_fwd_math(xr, g):
    r = lax.rsqrt(jnp.mean(xr * xr, axis=-1, keepdims=True) + RMS_EPS)
    return xr * r * g


def _rms_bwd_math(dy, xr, g):
    r = lax.rsqrt(jnp.mean(xr * xr, axis=-1, keepdims=True) + RMS_EPS)
    gy = dy * g
    dx = r * gy - xr * (r * r * r) * jnp.mean(gy * xr, axis=-1, keepdims=True)
    dg = jnp.sum(dy * xr * r, axis=0, keepdims=True)
    return dx, dg


def _mla_pre_fwd(x, w_in, g_q, g_kv, w_uq_n, w_uq_r, w_uk_t, cos, sin):
    T = x.shape[0]
    tm = min(ATT_TQ, T)
    nq = T // tm
    H = MLA_HEADS

    tk = min(ATT_TK, T)

    def body(x_ref, win_ref, gq_ref, gkv_ref, wn_ref, wr_ref, wuk_ref, cos_ref, sin_ref,
             h_ref, kc_ref, kct_ref, qs_ref, qst_ref, cq_ref, qn_ref):
        h = _dot(_mx(x_ref[...]), win_ref[...])
        h_ref[...] = h
        cos_v, sin_v = _tile_heads(cos_ref[...]), _tile_heads(sin_ref[...])
        cq = _mx(_rms_fwd_math(h[:, :MLA_QR], gq_ref[...]))
        ckv = _rms_fwd_math(h[:, MLA_QR:MLA_QR + MLA_C], gkv_ref[...])
        krr = h[:, MLA_QR + MLA_C:]
        kr = krr * cos_v[:, :MLA_ROPE] + _swap_halves_64(krr) * sin_v[:, :MLA_ROPE]
        kr_pad = jnp.concatenate([kr, jnp.zeros((tm, MLA_DK - MLA_C - MLA_ROPE), F32)], axis=1)
        kc_ref[:, 0:MLA_C] = _mx(ckv)
        kc_ref[:, MLA_C:] = _mx(kr_pad)
        kct_ref[0:MLA_C, :] = _mx(ckv.T)
        kct_ref[MLA_C:, :] = _mx(kr_pad.T[0:MLA_ROPE, :])
        cq_ref[...] = cq
        qnb = _mx(_dot(cq, wn_ref[...]))
        qn_ref[...] = qnb
        qr = _dot(cq, wr_ref[...])
        qrr = qr * cos_v + _swap_halves_groups(qr) * sin_v
        qrr_t = qrr.T
        for hd in range(H):
            ql = _dot(qnb[:, MLA_NOPE * hd:MLA_NOPE * (hd + 1)], wuk_ref[hd])
            qst_ref[0, 0:MLA_C, tm * hd:tm * (hd + 1)] = _mx(ql.T)
            qst_ref[0, MLA_C:, tm * hd:tm * (hd + 1)] = _mx(qrr_t[MLA_ROPE * hd:MLA_ROPE * (hd + 1), :])
            qs_ref[0, hd, :, 0:MLA_C] = _mx(ql)
            qs_ref[0, hd, :, MLA_C:MLA_C + MLA_ROPE] = _mx(qrr[:, MLA_ROPE * hd:MLA_ROPE * (hd + 1)])
            qs_ref[0, hd, :, MLA_C + MLA_ROPE:] = jnp.zeros((tm, MLA_DK - MLA_C - MLA_ROPE), _MXU_DTYPE)

    full = lambda shp: pl.BlockSpec(shp, lambda i: (0,) * len(shp))
    rows = lambda n: pl.BlockSpec((tm, n), lambda i: (i, 0))
    n_in = w_in.shape[1]
    return pl.pallas_call(
        body, name="mla_pre_fwd", grid=(nq,),
        in_specs=[rows(D_MODEL), full(w_in.shape), full(g_q.shape), full(g_kv.shape), full(w_uq_n.shape),
                  full(w_uq_r.shape), full(w_uk_t.shape), rows(ROPE_TABLE_W), rows(ROPE_TABLE_W)],
        out_specs=[rows(n_in), rows(MLA_DK),
                   pl.BlockSpec((None, MLA_DT, tm), lambda i: (i * tm // tk, 0, i % (tk // tm))),
                   pl.BlockSpec((1, H, tm, MLA_DK), lambda i: (i, 0, 0, 0)),
                   pl.BlockSpec((1, MLA_DT, H * tm), lambda i: (i, 0, 0)), rows(MLA_QR), rows(H * MLA_NOPE)],
        out_shape=[S((T, n_in), F32), S((T, MLA_DK), _MXU_DTYPE), S((T // tk, MLA_DT, tk), _MXU_DTYPE),
                   S((nq, H, tm, MLA_DK), _MXU_DTYPE), S((nq, MLA_DT, H * tm), _MXU_DTYPE),
                   S((T, MLA_QR), _MXU_DTYPE), S((T, H * MLA_NOPE), _MXU_DTYPE)],
        compiler_params=_params("parallel"))(x, w_in, g_q, g_kv, w_uq_n, w_uq_r, w_uk_t, cos, sin)


def _att_steps(T, tq, tk):
    qi, kj = [], []
    for i in range(T // tq):
        for j in range((i * tq + tq - 1) // tk + 1):
            qi.append(i)
            kj.append(j)
    return jnp.asarray(np.array(qi, np.int32)), jnp.asarray(np.array(kj, np.int32))


def _ride_exchange(st, n_steps, ins, outs, n_gather, sems):
    if not ins:
        return

    @pl.when(st == 0)
    def _():
        for cp in _exchange_copies(ins, outs, n_gather, *sems):
            cp.start()

    @pl.when(st == n_steps - 1)
    def _():
        for cp in _exchange_copies(ins, outs, n_gather, *sems):
            cp.wait()


def _mla_attn_fwd(qs, kc, kct, gather=(), scatter=()):
    nq, H, tq, DK = qs.shape
    T = kc.shape[0]
    tk = min(ATT_TK, T)
    scale = (MLA_NOPE + MLA_ROPE) ** -0.5
    c2 = scale * math.log2(math.e)
    qi, kj = _att_steps(T, tq, tk)
    n_steps = int(qi.shape[0])
    hg = ATT_FWD_HEAD_GROUP
    R = hg * tq
    n_x = len(gather) + len(scatter)

    def body(qi_ref, kj_ref, q_ref, k_ref, kt_ref, *rest):
        x_ins, (o_ref, lse_ref), x_outs = rest[:n_x], rest[n_x:n_x + 2], rest[n_x + 2:2 * n_x + 2]
        m_sc, l_sc, acc_sc = rest[2 * n_x + 2:2 * n_x + 5]
        st = pl.program_id(0)
        _ride_exchange(st, n_steps, x_ins, x_outs, len(gather), rest[2 * n_x + 5:])
        i, j = qi_ref[st], kj_ref[st]
        j_last = (i * tq + tq - 1) // tk

        @pl.when(j == 0)
        def _():
            m_sc[...] = jnp.full_like(m_sc, -jnp.inf)
            l_sc[...] = jnp.zeros_like(l_sc)
            acc_sc[...] = jnp.zeros_like(acc_sc)

        def step(masked):
            k = k_ref[...]
            vt = kt_ref[0:MLA_C, :]
            if masked:
                key = lax.broadcasted_iota(jnp.int32, (tk, R), 0) + j * tk
                qry = lax.broadcasted_iota(jnp.int32, (tk, R), 1) % tq + i * tq
                causal = key <= qry
            n_g = H // hg
            qk = lambda g: _dot(k, q_ref[0, g * hg:(g + 1) * hg].reshape(R, DK), NT)
            def accumulate(g, a, pb):
                cs = slice(g * R, (g + 1) * R)
                acc_sc[:, cs] = a * acc_sc[:, cs] + _dot(vt, pb)

            s_next = qk(0)
            pending = None
            for g in range(n_g):
                cs = slice(g * R, (g + 1) * R)
                s = s_next
                if g + 1 < n_g:
                    s_next = qk(g + 1)
                if pending is not None:
                    accumulate(*pending)
                if masked:
                    s = jnp.where(causal, s, -jnp.inf)
                m_prev = m_sc[:, cs]
                m_new = jnp.maximum(m_prev, jnp.max(s, axis=0, keepdims=True))
                a = jnp.exp2((m_prev - m_new) * c2)
                p = jnp.exp2((s - m_new) * c2)
                l_sc[:, cs] = a * l_sc[:, cs] + jnp.sum(p, axis=0, keepdims=True)
                m_sc[:, cs] = m_new
                pending = (g, a, _mx(p))
            accumulate(*pending)

        pl.when(j == j_last)(lambda: step(True))
        pl.when(j != j_last)(lambda: step(False))

        @pl.when(j == j_last)
        def _():
            o_ref[0] = _mx(acc_sc[...] / l_sc[...])
            lse_ref[0] = m_sc[...] * scale + jnp.log(l_sc[...])

    hbm = pl.BlockSpec(memory_space=pl.ANY)
    x_shapes, x_sems = _exchange_shapes(gather, scatter) if n_x else ([], [])
    gs = pltpu.PrefetchScalarGridSpec(
        num_scalar_prefetch=2, grid=(n_steps,),
        in_specs=[pl.BlockSpec((1, H, tq, DK), lambda s, qi, kj: (qi[s], 0, 0, 0)),
                  pl.BlockSpec((tk, DK), lambda s, qi, kj: (kj[s], 0)),
                  pl.BlockSpec((None, MLA_DT, tk), lambda s, qi, kj: (kj[s], 0, 0))] + [hbm] * n_x,
        out_specs=[pl.BlockSpec((1, MLA_C, H * tq), lambda s, qi, kj: (qi[s], 0, 0)),
                   pl.BlockSpec((1, 1, H * tq), lambda s, qi, kj: (qi[s], 0, 0))] + [hbm] * n_x,
        scratch_shapes=[pltpu.VMEM((1, H * tq), F32), pltpu.VMEM((1, H * tq), F32),
                        pltpu.VMEM((MLA_C, H * tq), F32)] + x_sems)
    res = pl.pallas_call(
        body, name="mla_attn_fwd", grid_spec=gs,
        out_shape=[S((nq, MLA_C, H * tq), _MXU_DTYPE), S((nq, 1, H * tq), F32)] + x_shapes,
        compiler_params=_params("arbitrary"))(qi, kj, qs, kc, kct, *gather, *scatter)
    return res[0], res[1], res[2:]


def _mla_attn_bwd(qs, qst, kc, kct, dol, lse, delta, gather=(), scatter=()):
    nq, H, tq, DK = qs.shape
    T = kc.shape[0]
    tk = min(ATT_TK, T)
    scale = (MLA_NOPE + MLA_ROPE) ** -0.5
    log2e = math.log2(math.e)
    qi, kj = _att_steps(T, tq, tk)
    n_steps = int(qi.shape[0])
    hg = ATT_HEAD_GROUP
    R = hg * tq
    n_x = len(gather) + len(scatter)

    def body(qi_ref, kj_ref, q_ref, qt_ref, k_ref, kt_ref, do_ref, lse_ref, dl_ref, *rest):
        x_ins, (dq_ref, dk_ref, dv_ref), x_outs = rest[:n_x], rest[n_x:n_x + 3], rest[n_x + 3:2 * n_x + 3]
        dk_acc, dv_acc, sem = rest[2 * n_x + 3:2 * n_x + 6]
        st = pl.program_id(0)
        _ride_exchange(st, n_steps, x_ins, x_outs, len(gather), rest[2 * n_x + 6:])
        i, j = qi_ref[st], kj_ref[st]
        j_last = (i * tq + tq - 1) // tk

        @pl.when(st == 0)
        def _():
            dk_acc[...] = jnp.zeros_like(dk_acc)
            dv_acc[...] = jnp.zeros_like(dv_acc)

        @pl.when(j == 0)
        def _():
            dq_ref[...] = jnp.zeros_like(dq_ref)

        def step(masked):
            k, kt = k_ref[...], kt_ref[...]
            v = k[:, :MLA_C]
            if masked:
                key = lax.broadcasted_iota(jnp.int32, (tk, R), 0) + j * tk
                qry = lax.broadcasted_iota(jnp.int32, (tk, R), 1) % tq + i * tq
                causal = key <= qry
            dkt_c = jnp.zeros((MLA_DT, tk), F32)
            dvt_c = jnp.zeros((MLA_C, tk), F32)
            n_g = H // hg

            def scores(g):
                q = q_ref[0, g * hg:(g + 1) * hg].reshape(R, DK)
                dot = do_ref[0, :, g * R:(g + 1) * R]
                return dot, _dot(k, q, NT), _dot(v, dot)

            nxt = scores(0)
            for g in range(n_g):
                cs = slice(g * R, (g + 1) * R)
                dot, s, dp = nxt
                if g + 1 < n_g:
                    nxt = scores(g + 1)
                p = jnp.exp2(s * (scale * log2e) - lse_ref[0, :, cs] * log2e)
                if masked:
                    p = jnp.where(causal, p, 0.0)
                dsb = _mx(p * (dp - dl_ref[0, :, cs]))
                dq_ref[0, :, cs] += _dot(kt, dsb)
                dkt_c = dkt_c + _dot(qt_ref[0, :, cs], dsb, NT)
                dvt_c = dvt_c + _dot(dot, _mx(p), NT)
            dk_acc[j] += dkt_c * scale
            dv_acc[j] += dvt_c

        pl.when(j == j_last)(lambda: step(True))
        pl.when(j != j_last)(lambda: step(False))

        @pl.when(j == j_last)
        def _():
            dq_ref[...] = dq_ref[...] * scale

        @pl.when(st == n_steps - 1)
        def _():
            c1 = pltpu.make_async_copy(dk_acc, dk_ref, sem.at[0])
            c2 = pltpu.make_async_copy(dv_acc, dv_ref, sem.at[1])
            c1.start()
            c2.start()
            c1.wait()
            c2.wait()

    cols = lambda n: pl.BlockSpec((1, n, H * tq), lambda s, qi, kj: (qi[s], 0, 0))
    hbm = pl.BlockSpec(memory_space=pl.ANY)
    x_shapes, x_sems = _exchange_shapes(gather, scatter) if n_x else ([], [])
    gs = pltpu.PrefetchScalarGridSpec(
        num_scalar_prefetch=2, grid=(n_steps,),
        in_specs=[pl.BlockSpec((1, H, tq, DK), lambda s, qi, kj: (qi[s], 0, 0, 0)), cols(MLA_DT),
                  pl.BlockSpec((tk, DK), lambda s, qi, kj: (kj[s], 0)),
                  pl.BlockSpec((None, MLA_DT, tk), lambda s, qi, kj: (kj[s], 0, 0)),
                  cols(MLA_C), cols(1), cols(1)] + [hbm] * n_x,
        out_specs=[cols(MLA_DT), hbm, hbm] + [hbm] * n_x,
        scratch_shapes=[pltpu.VMEM((T // tk, MLA_DT, tk), F32), pltpu.VMEM((T // tk, MLA_C, tk), F32),
                        pltpu.SemaphoreType.DMA((2,))] + x_sems)
    res = pl.pallas_call(
        body, name="mla_attn_bwd", grid_spec=gs,
        out_shape=[S((nq, MLA_DT, H * tq), F32), S((T // tk, MLA_DT, tk), F32),
                   S((T // tk, MLA_C, tk), F32)] + x_shapes,
        compiler_params=_params("arbitrary"))(qi, kj, qs, qst, kc, kct, dol, lse, delta, *gather, *scatter)
    return res[0], res[1], res[2], res[3:]


def _mla_uv_fwd(olat, w_uv):
    nq, C, cols = olat.shape
    H = w_uv.shape[0]
    tq = cols // H

    def body(ol_ref, wuv_ref, o_ref):
        for hd in range(H):
            o_ref[:, MLA_V * hd:MLA_V * (hd + 1)] = _mx(_dot(ol_ref[0, :, tq * hd:tq * (hd + 1)], wuv_ref[hd], TN))

    return pl.pallas_call(
        body, name="mla_uv_fwd", grid=(nq,),
        in_specs=[pl.BlockSpec((1, C, cols), lambda i: (i, 0, 0)), pl.BlockSpec(w_uv.shape, lambda i: (0, 0, 0))],
        out_specs=pl.BlockSpec((tq, H * MLA_V), lambda i: (i, 0)), out_shape=S((nq * tq, H * MLA_V), _MXU_DTYPE),
        compiler_params=_params("parallel"))(olat, w_uv)


def _mla_uv_bwd(do, olat, w_uv):
    nq, C, cols = olat.shape
    H = w_uv.shape[0]
    tq = cols // H

    def body(do_ref, ol_ref, wuv_ref, dol_ref, dl_ref, dw_ref):
        @pl.when(pl.program_id(0) == 0)
        def _():
            dw_ref[...] = jnp.zeros_like(dw_ref)

        dov = do_ref[...]
        for hd in range(H):
            cs = slice(tq * hd, tq * (hd + 1))
            doh = _mx(dov[:, MLA_V * hd:MLA_V * (hd + 1)])
            ol = ol_ref[0, :, cs]
            dol = _dot(wuv_ref[hd], doh, NT)
            dol_ref[0, :, cs] = _mx(dol)
            dl_ref[0, :, cs] = jnp.sum(dol * ol.astype(F32), axis=0, keepdims=True)
            dw_ref[hd] += _dot(ol, doh)

    blk = lambda n: pl.BlockSpec((1, n, cols), lambda i: (i, 0, 0))
    return pl.pallas_call(
        body, name="mla_uv_bwd", grid=(nq,),
        in_specs=[pl.BlockSpec((tq, H * MLA_V), lambda i: (i, 0)), blk(C), pl.BlockSpec(w_uv.shape, lambda i: (0, 0, 0))],
        out_specs=[blk(C), blk(1), pl.BlockSpec(w_uv.shape, lambda i: (0, 0, 0))],
        out_shape=[S(olat.shape, _MXU_DTYPE), S((nq, 1, cols), F32), S(w_uv.shape, F32)],
        compiler_params=_params("arbitrary"))(do, olat, w_uv)


def _mla_pre_bwd(dqs, dkc, dv, h, x, dres, cq, qn, cos, sin, w_in, g_q, g_kv, w_uq_n, w_uq_r, w_uk):
    nq, DK, cols = dqs.shape
    H = w_uk.shape[0]
    tm = cols // H
    T = nq * tm
    tk = dv.shape[2]
    n_in = w_in.shape[1]

    def body(dqs_ref, dkc_ref, dv_ref, h_ref, x_ref, dres_ref, cq_ref, qn_ref, cos_ref, sin_ref,
             win_ref, gq_ref, gkv_ref, wn_ref, wr_ref, wuk_ref,
             gx_ref, dwin_ref, dwn_ref, dwr_ref, dwuk_ref, dgq_ref, dgkv_ref, dqn_sc, dqr_sc, dh_sc):
        @pl.when(pl.program_id(0) == 0)
        def _():
            for r in (dwin_ref, dwn_ref, dwr_ref, dwuk_ref, dgq_ref, dgkv_ref):
                r[...] = jnp.zeros_like(r)

        cos_v, sin_v = _tile_heads(cos_ref[...]), _tile_heads(sin_ref[...])
        qnb = qn_ref[...]
        for hd in range(H):
            cs = slice(tm * hd, tm * (hd + 1))
            dql = _mx(dqs_ref[0, 0:MLA_C, cs])
            dqn_sc[:, MLA_NOPE * hd:MLA_NOPE * (hd + 1)] = _dot(dql, wuk_ref[hd], TN)
            dwuk_ref[hd] += _dot(dql, qnb[:, MLA_NOPE * hd:MLA_NOPE * (hd + 1)])
            dqr_sc[MLA_ROPE * hd:MLA_ROPE * (hd + 1), :] = dqs_ref[0, MLA_C:MLA_C + MLA_ROPE, cs]
        dqr = dqr_sc[...].T
        dqrb = _mx(dqr * cos_v + _swap_halves_groups(dqr * sin_v))
        dqnb = _mx(dqn_sc[...])
        cq = cq_ref[...]
        dwn_ref[...] += _dot(cq, dqnb, TN)
        dwr_ref[...] += _dot(cq, dqrb, TN)
        dcq = _dot(dqnb, wn_ref[...], NT) + _dot(dqrb, wr_ref[...], NT)
        hv = h_ref[...]
        dxq, dgq = _rms_bwd_math(dcq, hv[:, :MLA_QR], gq_ref[...])
        dgq_ref[...] += dgq
        dckv = (dkc_ref[0:MLA_C, :] + dv_ref[...]).T
        dxkv, dgkv = _rms_bwd_math(dckv, hv[:, MLA_QR:MLA_QR + MLA_C], gkv_ref[...])
        dgkv_ref[...] += dgkv
        dkr = jnp.concatenate([dkc_ref[MLA_C:, :], jnp.zeros((128 - MLA_ROPE, tm), F32)], axis=0).T[:, :MLA_ROPE]
        dkr_raw = dkr * cos_v[:, :MLA_ROPE] + _swap_halves_64(dkr * sin_v[:, :MLA_ROPE])
        dh_sc[:, 0:MLA_QR] = dxq
        dh_sc[:, MLA_QR:MLA_QR + MLA_C] = dxkv
        dh_sc[:, MLA_QR + MLA_C:] = dkr_raw
        dhb = _mx(dh_sc[...])
        gx_ref[...] = dres_ref[...] + _dot(dhb, win_ref[...], NT)
        dwin_ref[...] += _dot(_mx(x_ref[...]), dhb, TN)

    full = lambda shp: pl.BlockSpec(shp, lambda i: (0,) * len(shp))
    rows = lambda n: pl.BlockSpec((tm, n), lambda i: (i, 0))
    return pl.pallas_call(
        body, name="mla_pre_bwd", grid=(nq,),
        in_specs=[pl.BlockSpec((1, DK, cols), lambda i: (i, 0, 0)),
                  pl.BlockSpec((None, DK, tm), lambda i: (i * tm // tk, 0, i % (tk // tm))),
                  pl.BlockSpec((None, MLA_C, tm), lambda i: (i * tm // tk, 0, i % (tk // tm))), rows(n_in),
                  rows(D_MODEL), rows(D_MODEL), rows(MLA_QR), rows(H * MLA_NOPE), rows(ROPE_TABLE_W), rows(ROPE_TABLE_W),
                  full(w_in.shape), full(g_q.shape), full(g_kv.shape), full(w_uq_n.shape), full(w_uq_r.shape),
                  full(w_uk.shape)],
        out_specs=[rows(D_MODEL), full(w_in.shape), full(w_uq_n.shape), full(w_uq_r.shape), full(w_uk.shape),
                   full(g_q.shape), full(g_kv.shape)],
        out_shape=[S((T, D_MODEL), F32), S(w_in.shape, F32), S(w_uq_n.shape, F32), S(w_uq_r.shape, F32),
                   S(w_uk.shape, F32), S(g_q.shape, F32), S(g_kv.shape, F32)],
        scratch_shapes=[pltpu.VMEM((tm, H * MLA_NOPE), F32), pltpu.VMEM((H * MLA_ROPE, tm), F32),
                        pltpu.VMEM((tm, n_in), F32)],
        compiler_params=_params("arbitrary"))(dqs, dkc, dv, h, x, dres, cq, qn, cos, sin, w_in, g_q, g_kv,
                                              w_uq_n, w_uq_r, w_uk)


def _proj_ln_fwd(a, w, xres, g, b, *, name, tm=512):
    T, K = a.shape
    tm = min(tm, T)

    def body(a_ref, w_ref, x_ref, g_ref, b_ref, xo_ref, xob_ref, xh_ref, rs_ref):
        z = ALPHA * x_ref[...] + _dot(a_ref[...], w_ref[...])
        xo, xhat, rstd = _ln_fwd_math(z, g_ref[...], b_ref[...])
        xo_ref[...] = xo
        xob_ref[...] = _mx(xo)
        xh_ref[...] = xhat
        rs_ref[...] = rstd

    rows = lambda n: pl.BlockSpec((tm, n), lambda i: (i, 0))
    full = lambda shp: pl.BlockSpec(shp, lambda i: (0,) * len(shp))
    return pl.pallas_call(
        body, name=name, grid=(T // tm,),
        in_specs=[rows(K), full(w.shape), rows(D_MODEL), full(g.shape), full(b.shape)],
        out_specs=[rows(D_MODEL), rows(D_MODEL), rows(D_MODEL), rows(1)],
        out_shape=[S((T, D_MODEL), F32), S((T, D_MODEL), _MXU_DTYPE), S((T, D_MODEL), F32), S((T, 1), F32)],
        compiler_params=_params("parallel"))(a, w, xres, g, b)


def _proj_ln_bwd(dxo, xhat, rstd, g, a, w, *, name, tm=512):
    T, K = a.shape
    tm = min(tm, T)

    def body(dxo_ref, xh_ref, rs_ref, g_ref, a_ref, w_ref, dres_ref, da_ref, dw_ref, dg_ref, db_ref):
        @pl.when(pl.program_id(0) == 0)
        def _():
            for r in (dw_ref, dg_ref, db_ref):
                r[...] = jnp.zeros_like(r)

        dz, dg, db = _ln_bwd_math(dxo_ref[...], xh_ref[...], rs_ref[...], g_ref[...])
        dg_ref[...] += dg
        db_ref[...] += db
        dres_ref[...] = ALPHA * dz
        dzb = _mx(dz)
        da_ref[...] = _dot(dzb, w_ref[...], NT)
        dw_ref[...] += _dot(a_ref[...], dzb, TN)

    rows = lambda n: pl.BlockSpec((tm, n), lambda i: (i, 0))
    full = lambda shp: pl.BlockSpec(shp, lambda i: (0,) * len(shp))
    return pl.pallas_call(
        body, name=name, grid=(T // tm,),
        in_specs=[rows(D_MODEL), rows(D_MODEL), rows(1), full(g.shape), rows(K), full(w.shape)],
        out_specs=[rows(D_MODEL), rows(K), full(w.shape), full(g.shape), full(g.shape)],
        out_shape=[S((T, D_MODEL), F32), S((T, K), F32), S(w.shape, F32), S(g.shape, F32), S(g.shape, F32)],
        compiler_params=_params("arbitrary"))(dxo, xhat, rstd, g, a, w)


def _mlp_fwd(xb, xres, w_up, w_dn, layer, g, b, *, tm=1024):
    T = xb.shape[0]
    tm = min(tm, T)
    nj, _, _, fc = w_up.shape

    def body(xb_ref, x_ref, wu_ref, wd_ref, g_ref, b_ref, u_ref, xo_ref, xob_ref, xh_ref, rs_ref, acc):
        j = pl.program_id(1)

        @pl.when(j == 0)
        def _():
            acc[...] = ALPHA * x_ref[...]

        u = _dot(xb_ref[...], wu_ref[...])
        u_ref[...] = _mx(u)
        r = jnp.maximum(u, 0.0)
        acc[...] += _dot(_mx(r * r), wd_ref[...])

        @pl.when(j == nj - 1)
        def _():
            xo, xhat, rstd = _ln_fwd_math(acc[...], g_ref[...], b_ref[...])
            xo_ref[...] = xo
            xob_ref[...] = _mx(xo)
            xh_ref[...] = xhat
            rs_ref[...] = rstd

    rows = lambda n: pl.BlockSpec((tm, n), lambda i, j: (i, 0))
    full = lambda shp: pl.BlockSpec(shp, lambda i, j: (0,) * len(shp))
    return pl.pallas_call(
        body, name=f"mlp_fwd_{layer}", grid=(T // tm, nj),
        in_specs=[rows(D_MODEL), rows(D_MODEL),
                  pl.BlockSpec((None, None, D_MODEL, fc), lambda i, j: (j, layer, 0, 0)),
                  pl.BlockSpec((None, None, fc, D_MODEL), lambda i, j: (j, layer, 0, 0)),
                  full(g.shape), full(b.shape)],
        out_specs=[pl.BlockSpec((tm, fc), lambda i, j: (i, j)), rows(D_MODEL), rows(D_MODEL), rows(D_MODEL), rows(1)],
        out_shape=[S((T, nj * fc), _MXU_DTYPE), S((T, D_MODEL), F32), S((T, D_MODEL), _MXU_DTYPE),
                   S((T, D_MODEL), F32), S((T, 1), F32)],
        scratch_shapes=[pltpu.VMEM((tm, D_MODEL), F32)],
        compiler_params=_params("parallel", "arbitrary"))(xb, xres, w_up, w_dn, g, b)


def _mlp_bwd_dx(dxo, xhat, rstd, g, u, w_up, w_dn, layer, *, tm=1024):
    T = dxo.shape[0]
    tm = min(tm, T)
    nj, _, _, fc = w_up.shape

    def body(dxo_ref, xh_ref, rs_ref, g_ref, u_ref, wu_ref, wd_ref, dx_ref, du_ref, dyb_ref, dg_ref, db_ref, acc, dy_sc):
        i, j = pl.program_id(0), pl.program_id(1)

        @pl.when((i == 0) & (j == 0))
        def _():
            dg_ref[...] = jnp.zeros_like(dg_ref)
            db_ref[...] = jnp.zeros_like(db_ref)

        @pl.when(j == 0)
        def _():
            dz, dg, db = _ln_bwd_math(dxo_ref[...], xh_ref[...], rs_ref[...], g_ref[...])
            dg_ref[...] += dg
            db_ref[...] += db
            acc[...] = ALPHA * dz
            dy_sc[...] = _mx(dz)
            dyb_ref[...] = _mx(dz)

        r = jnp.maximum(u_ref[...].astype(F32), 0.0)
        da = _dot(dy_sc[...], wd_ref[...], NT)
        dub = _mx(da * (2.0 * r))
        du_ref[...] = dub
        acc[...] += _dot(dub, wu_ref[...], NT)

        @pl.when(j == nj - 1)
        def _():
            dx_ref[...] = acc[...]

    rows = lambda n: pl.BlockSpec((tm, n), lambda i, j: (i, 0))
    full = lambda shp: pl.BlockSpec(shp, lambda i, j: (0,) * len(shp))
    return pl.pallas_call(
        body, name=f"mlp_bwd_dx_{layer}", grid=(T // tm, nj),
        in_specs=[rows(D_MODEL), rows(D_MODEL), rows(1), full(g.shape), pl.BlockSpec((tm, fc), lambda i, j: (i, j)),
                  pl.BlockSpec((None, None, D_MODEL, fc), lambda i, j: (j, layer, 0, 0)),
                  pl.BlockSpec((None, None, fc, D_MODEL), lambda i, j: (j, layer, 0, 0))],
        out_specs=[rows(D_MODEL), pl.BlockSpec((tm, fc), lambda i, j: (i, j)), rows(D_MODEL), full(g.shape), full(g.shape)],
        out_shape=[S((T, D_MODEL), F32), S((T, nj * fc), _MXU_DTYPE), S((T, D_MODEL), _MXU_DTYPE),
                   S(g.shape, F32), S(g.shape, F32)],
        scratch_shapes=[pltpu.VMEM((tm, D_MODEL), F32), pltpu.VMEM((tm, D_MODEL), _MXU_DTYPE)],
        compiler_params=_params("arbitrary", "arbitrary"))(dxo, xhat, rstd, g, u, w_up, w_dn)


def _mlp_bwd_dw(u, dyb, xinb, du, layer, *, nj, other_layers=None, tm=1024):
    T = u.shape[0]
    tm = min(tm, T)
    fc = u.shape[1] // nj

    def body(u_ref, dy_ref, x_ref, du_ref, *rest):
        gd_ref, gu_ref = rest[-2:]

        @pl.when(pl.program_id(1) == 0)
        def _():
            gd_ref[...] = jnp.zeros_like(gd_ref)
            gu_ref[...] = jnp.zeros_like(gu_ref)

        r = jnp.maximum(u_ref[...].astype(F32), 0.0)
        gd_ref[...] += _dot(_mx(r * r), dy_ref[...], TN)
        gu_ref[...] += _dot(x_ref[...], du_ref[...], TN)

    in_specs = [pl.BlockSpec((tm, fc), lambda j, i: (i, j)), pl.BlockSpec((tm, D_MODEL), lambda j, i: (i, 0)),
                pl.BlockSpec((tm, D_MODEL), lambda j, i: (i, 0)), pl.BlockSpec((tm, fc), lambda j, i: (i, j))]
    args, aliases = [u, dyb, xinb, du], {}
    if other_layers is not None:
        in_specs += [pl.BlockSpec(memory_space=pl.ANY)] * 2
        args += list(other_layers)
        aliases = {4: 0, 5: 1}
    return pl.pallas_call(
        body, name=f"mlp_bwd_dw_{layer}", grid=(nj, T // tm), in_specs=in_specs,
        out_specs=[pl.BlockSpec((None, None, fc, D_MODEL), lambda j, i: (j, layer, 0, 0)),
                   pl.BlockSpec((None, None, D_MODEL, fc), lambda j, i: (j, layer, 0, 0))],
        out_shape=[S((nj, DEPTH, fc, D_MODEL), F32), S((nj, DEPTH, D_MODEL, fc), F32)],
        input_output_aliases=aliases,
        compiler_params=_params("parallel", "arbitrary"))(*args)


SWA_GROUP = SWA_QH // SWA_KVH
SWA_ROWS = SWA_GROUP * SWA_BLOCK


def _swa_heads(a, kh):
    return jnp.concatenate([a[:, SWA_D * (kh * SWA_GROUP + g):SWA_D * (kh * SWA_GROUP + g + 1)]
                            for g in range(SWA_GROUP)], axis=0)


def _swa_operands(q, kvp, kvc, kh):
    dkv = SWA_KVH * SWA_D
    qg = _swa_heads(q, kh)
    kb = jnp.concatenate([kvp[:, SWA_D * kh:SWA_D * (kh + 1)], kvc[:, SWA_D * kh:SWA_D * (kh + 1)]], axis=0)
    vb = jnp.concatenate([kvp[:, dkv + SWA_D * kh:dkv + SWA_D * (kh + 1)],
                          kvc[:, dkv + SWA_D * kh:dkv + SWA_D * (kh + 1)]], axis=0)
    return qg, kb, vb, _dot(kb, qg, NT)


def _swa_softmax(s_raw, bias_ref, sink_ref, n, kh):
    cols = slice(kh * SWA_ROWS, (kh + 1) * SWA_ROWS)
    s = s_raw * (SWA_D ** -0.5) + bias_ref[jnp.minimum(n, 1), :, cols]
    sink = sink_ref[:, cols]
    m = jnp.maximum(jnp.max(s, axis=0, keepdims=True), sink)
    p, ps = jnp.exp(s - m), jnp.exp(sink - m)
    inv = 1.0 / (jnp.sum(p, axis=0, keepdims=True) + ps)
    return p * inv, ps * inv


def _swa_attn_fwd(qkv, bias, sinks):
    T = qkv.shape[0]
    blk = SWA_BLOCK
    nb = T // blk
    dq, dkv = SWA_QH * SWA_D, SWA_KVH * SWA_D

    def body(q_ref, kvp_ref, kvc_ref, bias_ref, sink_ref, o_ref):
        n = pl.program_id(0)
        q, kvp, kvc = q_ref[...], kvp_ref[...], kvc_ref[...]
        nxt = _swa_operands(q, kvp, kvc, 0)
        for kh in range(SWA_KVH):
            _, _, vb, s_raw = nxt
            if kh + 1 < SWA_KVH:
                nxt = _swa_operands(q, kvp, kvc, kh + 1)
            p, _ = _swa_softmax(s_raw, bias_ref, sink_ref, n, kh)
            og = _mx(_dot(_mx(p), vb, TN))
            for g in range(SWA_GROUP):
                hd = kh * SWA_GROUP + g
                o_ref[:, SWA_D * hd:SWA_D * (hd + 1)] = og[blk * g:blk * (g + 1), :]

    return pl.pallas_call(
        body, name="swa_attn_fwd", grid=(nb,),
        in_specs=[pl.BlockSpec((blk, dq), lambda n: (n, 0)),
                  pl.BlockSpec((blk, 2 * dkv), lambda n: (jnp.maximum(n - 1, 0), dq // (2 * dkv))),
                  pl.BlockSpec((blk, 2 * dkv), lambda n: (n, dq // (2 * dkv))),
                  pl.BlockSpec(bias.shape, lambda n: (0, 0, 0)), pl.BlockSpec(sinks.shape, lambda n: (0, 0))],
        out_specs=pl.BlockSpec((blk, dq), lambda n: (n, 0)), out_shape=S((T, dq), _MXU_DTYPE),
        compiler_params=_params("parallel"))(qkv, qkv, qkv, bias, sinks)


def _swa_attn_bwd(qkv, ob, do, bias, sinks):
    T = qkv.shape[0]
    blk = SWA_BLOCK
    nb = T // blk
    dq, dkv = SWA_QH * SWA_D, SWA_KVH * SWA_D

    def body(q_ref, kvp_ref, kvc_ref, o_ref, do_ref, bias_ref, sink_ref, dqkv_ref, dbias_ref, dsink_ref, carry):
        st = pl.program_id(0)
        n = nb - 1 - st

        @pl.when(st == 0)
        def _():
            carry[...] = jnp.zeros_like(carry)
            dbias_ref[...] = jnp.zeros_like(dbias_ref)
            dsink_ref[...] = jnp.zeros_like(dsink_ref)

        q, kvp, kvc = q_ref[...], kvp_ref[...], kvc_ref[...]
        ov, dov = o_ref[...], do_ref[...]
        ones = jnp.ones((8, SWA_D), F32)
        def operands(kh):
            qg, kb, vb, s_raw = _swa_operands(q, kvp, kvc, kh)
            dog = _swa_heads(dov, kh)
            dl = _dot(ones, dog * _swa_heads(ov, kh).astype(F32), NT, lax.Precision.HIGHEST)[0:1]
            dogb = _mx(dog)
            return qg, kb, s_raw, dl, dogb, _dot(vb, dogb, NT)

        nxt = operands(0)
        for kh in range(SWA_KVH):
            cols = slice(kh * SWA_ROWS, (kh + 1) * SWA_ROWS)
            qg, kb, s_raw, dl, dogb, dp = nxt
            if kh + 1 < SWA_KVH:
                nxt = operands(kh + 1)
            p, ps = _swa_softmax(s_raw, bias_ref, sink_ref, n, kh)
            ds = p * (dp - dl)
            dbias_ref[:, cols] += ds
            dsink_ref[0:1, cols] += -ps * dl
            dsb = _mx(ds * (SWA_D ** -0.5))
            dqg = _mx(_dot(dsb, kb, TN))
            for g in range(SWA_GROUP):
                hd = kh * SWA_GROUP + g
                dqkv_ref[:, SWA_D * hd:SWA_D * (hd + 1)] = dqg[blk * g:blk * (g + 1), :]
            dkb = _dot(dsb, qg)
            dvb = _dot(_mx(p), dogb)
            ko, vo = SWA_D * kh, dkv + SWA_D * kh
            dqkv_ref[:, dq + ko:dq + ko + SWA_D] = _mx(dkb[blk:, :] + carry[:, ko:ko + SWA_D])
            dqkv_ref[:, dq + vo:dq + vo + SWA_D] = _mx(dvb[blk:, :] + carry[:, vo:vo + SWA_D])
            carry[:, ko:ko + SWA_D] = dkb[:blk, :]
            carry[:, vo:vo + SWA_D] = dvb[:blk, :]

    rev = lambda s: nb - 1 - s
    return pl.pallas_call(
        body, name="swa_attn_bwd", grid=(nb,),
        in_specs=[pl.BlockSpec((blk, dq), lambda s: (rev(s), 0)),
                  pl.BlockSpec((blk, 2 * dkv), lambda s: (jnp.maximum(rev(s) - 1, 0), dq // (2 * dkv))),
                  pl.BlockSpec((blk, 2 * dkv), lambda s: (rev(s), dq // (2 * dkv))),
                  pl.BlockSpec((blk, dq), lambda s: (rev(s), 0)), pl.BlockSpec((blk, dq), lambda s: (rev(s), 0)),
                  pl.BlockSpec(bias.shape, lambda s: (0, 0, 0)), pl.BlockSpec(sinks.shape, lambda s: (0, 0))],
        out_specs=[pl.BlockSpec((blk, dq + 2 * dkv), lambda s: (rev(s), 0)),
                   pl.BlockSpec(bias.shape[1:], lambda s: (0, 0)), pl.BlockSpec((8, sinks.shape[1]), lambda s: (0, 0))],
        out_shape=[S((T, dq + 2 * dkv), _MXU_DTYPE), S(bias.shape[1:], F32), S((8, sinks.shape[1]), F32)],
        scratch_shapes=[pltpu.VMEM((blk, 2 * dkv), F32)],
        compiler_params=_params("arbitrary"))(qkv, qkv, qkv, ob, do, bias, sinks)


def _t5_onehot():
    i = jnp.arange(SWA_BLOCK)
    j = jnp.arange(2 * SWA_BLOCK)
    n = jnp.maximum(i[:, None] + SWA_BLOCK - j[None, :], 0)
    max_exact = REL_BUCKETS // 2
    nf = jnp.maximum(n, 1).astype(F32)
    large = max_exact + (jnp.log(nf / max_exact) / math.log(REL_MAX_DIST / max_exact)
                         * (REL_BUCKETS - max_exact)).astype(jnp.int32)
    large = jnp.minimum(large, REL_BUCKETS - 1)
    bucket = jnp.where(n < max_exact, n, large).reshape(-1)
    return (bucket[None, :] == jnp.arange(REL_BUCKETS)[:, None]).astype(F32)


def _loss_head(y, target, *, tm=1024):
    T, D = y.shape
    tm = min(tm, T)

    def body(y_ref, t_ref, loss_ref, dy_ref):
        @pl.when(pl.program_id(0) == 0)
        def _():
            loss_ref[...] = jnp.zeros_like(loss_ref)

        d = y_ref[...] - t_ref[...]
        dy_ref[...] = d * (1.0 / D)
        rs = jnp.sum(d * d, axis=1, keepdims=True)
        loss_ref[...] += (0.5 / D) * jnp.sum(rs, axis=0, keepdims=True)

    rows = pl.BlockSpec((tm, D), lambda i: (i, 0))
    return pl.pallas_call(
        body, name="loss_head", grid=(T // tm,), in_specs=[rows, rows],
        out_specs=[pl.BlockSpec((1, 1), lambda i: (0, 0)), rows], out_shape=[S((1, 1), F32), S((T, D), F32)],
        compiler_params=_params("arbitrary"))(y, target)


def _exchange_copies(ins, outs, n_gather, send_sems, recv_sems, loc_sems):
    mx, my, mc = lax.axis_index("x"), lax.axis_index("y"), lax.axis_index("c")
    me = 4 * mx + 2 * my + mc
    copies = []
    for a in range(len(ins)):
        src = ins[a] if a < n_gather else ins[a].at[me]
        copies.append(pltpu.make_async_copy(src, outs[a].at[me], loc_sems.at[a]))
    for k in range(1, N_DEV):
        px, py, pc = mx ^ ((k >> 2) & 1), my ^ ((k >> 1) & 1), mc ^ (k & 1)
        peer = 4 * px + 2 * py + pc
        for a in range(len(ins)):
            src = ins[a] if a < n_gather else ins[a].at[peer]
            copies.append(pltpu.make_async_remote_copy(
                src_ref=src, dst_ref=outs[a].at[me], send_sem=send_sems.at[a, k - 1],
                recv_sem=recv_sems.at[a, k - 1], device_id=(px, py, pc), device_id_type=pl.DeviceIdType.MESH))
    return copies


def _exchange_shapes(gather, scatter):
    n_arr = len(gather) + len(scatter)
    out_shape = [S((N_DEV,) + tuple(g.shape), g.dtype) for g in gather] + [S(s.shape, s.dtype) for s in scatter]
    sems = [pltpu.SemaphoreType.DMA((n_arr, N_DEV - 1)), pltpu.SemaphoreType.DMA((n_arr, N_DEV - 1)),
            pltpu.SemaphoreType.DMA((n_arr,))]
    return out_shape, sems


def _exchange(gather, scatter, *, name):
    n_g = len(gather)
    n_arr = n_g + len(scatter)

    def body(*refs):
        copies = _exchange_copies(refs[:n_arr], refs[n_arr:2 * n_arr], n_g, *refs[2 * n_arr:])
        for cp in copies:
            cp.start()
        for cp in copies:
            cp.wait()

    hbm = pl.BlockSpec(memory_space=pl.ANY)
    out_shape, sems = _exchange_shapes(gather, scatter)
    return pl.pallas_call(
        body, name=name, in_specs=[hbm] * n_arr, out_specs=[hbm] * n_arr, out_shape=out_shape,
        scratch_shapes=sems)(*gather, *scatter)


def _adamw(parts, w, m, v, *, name, tr=256):
    R, C = w.shape
    tr = min(tr, R)
    assert R % tr == 0

    def body(p_ref, w_ref, m_ref, v_ref, g_ref, d_ref, nm_ref, nv_ref):
        g = p_ref[0]
        for k in range(1, N_DEV):
            g = g + p_ref[k]
        g_ref[...] = g
        d_ref[...], nm_ref[...], nv_ref[...] = _adamw_math(g, w_ref[...], m_ref[...], v_ref[...])

    rows = pl.BlockSpec((tr, C), lambda i: (i, 0))
    return pl.pallas_call(
        body, name=name, grid=(R // tr,),
        in_specs=[pl.BlockSpec((N_DEV, tr, C), lambda i: (0, i, 0)), rows, rows, rows],
        out_specs=[rows] * 4, out_shape=[S((R, C), F32)] * 4,
        compiler_params=_params("parallel"))(parts, w, m, v)


def _adamw_math(g, w, m, v):
    m_new = ADAM_B1 * m + (1.0 - ADAM_B1) * g
    v_new = ADAM_B2 * v + (1.0 - ADAM_B2) * (g * g)
    m_hat = m_new / (1.0 - ADAM_B1 ** ADAM_STEP)
    v_hat = v_new / (1.0 - ADAM_B2 ** ADAM_STEP)
    return -ADAM_LR * (m_hat / (jnp.sqrt(v_hat) + ADAM_EPS) + ADAM_WD * w), m_new, v_new


SMALL_ROWS = 48
REPL = {"ln_mix_g": (slice(0, 2), slice(None)), "ln_mix_b": (slice(2, 4), slice(None)),
        "ln_mlp_g": (slice(4, 6), slice(None)), "ln_mlp_b": (slice(6, 8), slice(None)),
        "swa_sinks": (slice(8, 9), slice(0, SWA_QH)), "rel_bias": (slice(16, 16 + REL_BUCKETS), slice(0, SWA_QH))}
GAINS = {"mla_g_q": (slice(0, 1), slice(0, MLA_QR // N_DEV)),
         "mla_g_kv": (slice(0, 1), slice(MLA_QR // N_DEV, (MLA_QR + MLA_C) // N_DEV))}


def _adamw_small(r_all, p_gains, W, M, V):
    names = list(REPL) + list(GAINS)

    def body(r_ref, pg_ref, *refs):
        ins, outs = refs[:3 * len(names)], refs[3 * len(names):]
        r_sum, g_sum = r_ref[0], pg_ref[0]
        for k in range(1, N_DEV):
            r_sum, g_sum = r_sum + r_ref[k], g_sum + pg_ref[k]
        for i, n in enumerate(names):
            g = r_sum[REPL[n]] if n in REPL else g_sum[GAINS[n]]
            w_ref, m_ref, v_ref = ins[3 * i:3 * i + 3]
            g_ref, d_ref, nm_ref, nv_ref = outs[4 * i:4 * i + 4]
            g_ref[...] = g
            d_ref[...], nm_ref[...], nv_ref[...] = _adamw_math(g, w_ref[...], m_ref[...], v_ref[...])

    flat_in = [d[n] for n in names for d in (W, M, V)]
    res = pl.pallas_call(body, name="adamw_small", out_shape=[S(W[n].shape, F32) for n in names for _ in range(4)],
                         compiler_params=_params())(r_all, p_gains, *flat_in)
    return {(k, n): res[4 * i + k] for i, n in enumerate(names) for k in range(4)}


WEIGHTS = ["mla_w_in", "mla_g_q", "mla_g_kv", "mla_w_uq", "mla_w_uk", "mla_w_uv", "mla_w_o", "kv_w_shared",
           "swa_w_q", "swa_sinks", "swa_w_o", "rel_bias", "mlp_w_up", "mlp_w_down", "ln_mix_g", "ln_mix_b",
           "ln_mlp_g", "ln_mlp_b"]


def kernel(x, mla_w_in, mla_g_q, mla_g_kv, mla_w_uq, mla_w_uk, mla_w_uv, mla_w_o, kv_w_shared, swa_w_q, swa_sinks, swa_w_o, rel_bias, mlp_w_up, mlp_w_down, ln_mix_g, ln_mix_b, ln_mlp_g, ln_mlp_b, loss_target, m_mla_w_in, m_mla_g_q, m_mla_g_kv, m_mla_w_uq, m_mla_w_uk, m_mla_w_uv, m_mla_w_o, m_kv_w_shared, m_swa_w_q, m_swa_sinks, m_swa_w_o, m_rel_bias, m_mlp_w_up, m_mlp_w_down, m_ln_mix_g, m_ln_mix_b, m_ln_mlp_g, m_ln_mlp_b, v_mla_w_in, v_mla_g_q, v_mla_g_kv, v_mla_w_uq, v_mla_w_uk, v_mla_w_uv, v_mla_w_o, v_kv_w_shared, v_swa_w_q, v_swa_sinks, v_swa_w_o, v_rel_bias, v_mlp_w_up, v_mlp_w_down, v_ln_mix_g, v_ln_mix_b, v_ln_mlp_g, v_ln_mlp_b):
    args = dict(locals())
    W = {n: args[n] for n in WEIGHTS}
    M = {n: args["m_" + n] for n in WEIGHTS}
    V = {n: args["v_" + n] for n in WEIGHTS}
    T = x.shape[1]
    x2d = x.reshape(T, D_MODEL)
    tgt = loss_target.reshape(T, D_MODEL)
    H = MLA_HEADS

    SH = {"mla_w_in": (-1, mla_w_in.shape[-1]), "mla_w_uq": (-1, H * (MLA_NOPE + MLA_ROPE)),
          "mla_w_uk": (-1, H * MLA_NOPE), "mla_w_uv": (-1, H * MLA_V), "mla_w_o": (-1, D_MODEL),
          "kv_w_shared": (-1, kv_w_shared.shape[-1]), "swa_w_q": (-1, swa_w_q.shape[-1]), "swa_w_o": (-1, D_MODEL)}
    slab = lambda d, n: d[n].reshape(SH[n])
    bf = lambda a: a.astype(_MXU_DTYPE)
    gains_slab = lambda d: jnp.pad(jnp.concatenate([d["mla_g_q"], d["mla_g_kv"]], axis=1),
                                   ((0, 7), (0, 128 - d["mla_g_q"].shape[1] - d["mla_g_kv"].shape[1])))
    n_gq, n_gkv = mla_g_q.shape[1], mla_g_kv.shape[1]
    w_in_s, w_uq_s, w_uk_s, gains_all = _exchange(
        [bf(slab(W, "mla_w_in")), bf(slab(W, "mla_w_uq")), bf(slab(W, "mla_w_uk")), gains_slab(W)], [],
        name="gather_mla_in")
    later = [bf(slab(W, n)) for n in ("mla_w_uv", "mla_w_o", "kv_w_shared", "swa_w_q", "swa_w_o")]
    later += [bf(mlp_w_up), bf(mlp_w_down)]
    w_in = w_in_s.reshape(D_MODEL, -1)
    g_q = gains_all[:, 0, :n_gq].reshape(1, MLA_QR)
    g_kv = gains_all[:, 0, n_gq:n_gq + n_gkv].reshape(1, MLA_C)
    w_uq = w_uq_s.reshape(MLA_QR, H, MLA_NOPE + MLA_ROPE)
    w_uq_n = w_uq[:, :, :MLA_NOPE].reshape(MLA_QR, H * MLA_NOPE)
    w_uq_r = w_uq[:, :, MLA_NOPE:].reshape(MLA_QR, H * MLA_ROPE)
    w_uk = w_uk_s.reshape(MLA_C, H, MLA_NOPE).transpose(1, 0, 2)
    w_uk_t = w_uk.transpose(0, 2, 1)
    ln = lambda a, l: a[l].reshape(1, D_MODEL)

    half = MLA_ROPE // 2
    inv = ROPE_THETA ** (-jnp.arange(half, dtype=F32) / half)
    ang = jnp.arange(T, dtype=F32)[:, None] * inv[None, :]
    cos = jnp.tile(jnp.concatenate([jnp.cos(ang), jnp.cos(ang)], -1), (1, ROPE_TABLE_W // MLA_ROPE))
    sin = jnp.tile(jnp.concatenate([-jnp.sin(ang), jnp.sin(ang)], -1), (1, ROPE_TABLE_W // MLA_ROPE))

    h, kc, kct, qs, qst, cq, qn = _mla_pre_fwd(x2d, w_in, g_q, g_kv, w_uq_n, w_uq_r, w_uk_t, cos, sin)
    olat, lse, (w_uv_s, w_o_s, w_kv_s, w_q_s, w_o2_s, w_up, w_dn) = _mla_attn_fwd(qs, kc, kct, gather=later)
    w_uv = w_uv_s.reshape(MLA_C, H, MLA_V).transpose(1, 0, 2)
    w_o = w_o_s.reshape(H * MLA_V, D_MODEL)
    w_qkv = jnp.concatenate([w_q_s.reshape(D_MODEL, -1), w_kv_s.reshape(D_MODEL, -1)], axis=1)
    w_o2 = w_o2_s.reshape(SWA_QH * SWA_D, D_MODEL)
    o_mla = _mla_uv_fwd(olat, w_uv)
    x1, x1b, xh1, rs1 = _proj_ln_fwd(o_mla, w_o, x2d, ln(ln_mix_g, 0), ln(ln_mix_b, 0), name="mla_out_ln_fwd")
    u0, x2, x2b, xh2, rs2 = _mlp_fwd(x1b, x1, w_up, w_dn, 0, ln(ln_mlp_g, 0), ln(ln_mlp_b, 0))
    onehot = _t5_onehot()
    bias = _mm(rel_bias.T, onehot, name="rel_bias_expand", precision=lax.Precision.HIGHEST, tn=8192).reshape(
        SWA_QH * SWA_BLOCK, 2 * SWA_BLOCK).T
    key = jnp.arange(2 * SWA_BLOCK)[:, None]
    qry = jnp.arange(SWA_QH * SWA_BLOCK)[None, :] % SWA_BLOCK
    in_window = (key > qry) & (key <= qry + SWA_BLOCK)
    bias = jnp.stack([jnp.where(in_window & (key >= SWA_BLOCK), bias, -jnp.inf), jnp.where(in_window, bias, -jnp.inf)])
    sink_rows = jnp.repeat(swa_sinks.reshape(SWA_QH), SWA_BLOCK).reshape(1, SWA_QH * SWA_BLOCK)
    qkv = _mm(x2b, w_qkv, name="swa_qkv_fwd", out_dtype=_MXU_DTYPE, tm=1024, tn=512, tk=1024)
    o_swa = _swa_attn_fwd(qkv, bias, sink_rows)
    x3, x3b, xh3, rs3 = _proj_ln_fwd(o_swa, w_o2, x2, ln(ln_mix_g, 1), ln(ln_mix_b, 1), name="swa_out_ln_fwd")
    u1, x4, _, xh4, rs4 = _mlp_fwd(x3b, x3, w_up, w_dn, 1, ln(ln_mlp_g, 1), ln(ln_mlp_b, 1))
    loss_part, dx4 = _loss_head(x4, tgt)
    loss = lax.psum(loss_part[0, 0], AXES)

    nj = w_up.shape[0]
    dx3, du1, dy4b, dg_mlp1, db_mlp1 = _mlp_bwd_dx(dx4, xh4, rs4, ln(ln_mlp_g, 1), u1, w_up, w_dn, 1)
    g_dn_last, g_up_last = _mlp_bwd_dw(u1, dy4b, x3b, du1, 1, nj=nj)
    dres3, do_swa, g_o2, dg_mix1, db_mix1 = _proj_ln_bwd(dx3, xh3, rs3, ln(ln_mix_g, 1), o_swa, w_o2,
                                                         name="swa_out_ln_bwd")
    dqkv, dbias, dsink = _swa_attn_bwd(qkv, o_swa, do_swa, bias, sink_rows)
    g_rel = _mm(onehot, dbias.T.reshape(SWA_QH, -1), name="rel_bias_grad", tb=True, precision=lax.Precision.HIGHEST,
                tk=8192)
    head_of_row = (jnp.arange(SWA_QH * SWA_BLOCK)[:, None] // SWA_BLOCK == jnp.arange(SWA_QH)[None, :]).astype(F32)
    g_sinks = _mm(dsink, head_of_row, name="sinks_grad", precision=lax.Precision.HIGHEST, tk=2048)[0:1]
    dx2 = _mm(dqkv, w_qkv, name="swa_qkv_bwd_dx", tb=True, add=dres3, tm=1024, tn=1024, tk=1536)
    g_qkv = _mm(x2b, dqkv, name="swa_qkv_bwd_dw", ta=True, tm=1024, tn=512, tk=1024)
    dx1, du0, dy2b, dg_mlp0, db_mlp0 = _mlp_bwd_dx(dx2, xh2, rs2, ln(ln_mlp_g, 0), u0, w_up, w_dn, 0)
    g_dn, g_up = _mlp_bwd_dw(u0, dy2b, x1b, du0, 0, nj=nj, other_layers=(g_dn_last, g_up_last))
    dres1, do_mla, g_o, dg_mix0, db_mix0 = _proj_ln_bwd(dx1, xh1, rs1, ln(ln_mix_g, 0), o_mla, w_o,
                                                        name="mla_out_ln_bwd")
    dol, delta, g_uv = _mla_uv_bwd(do_mla, olat, w_uv)
    wide = lambda a, rows: jnp.pad(a, ((0, rows - a.shape[0]), (0, LANES - a.shape[1])))
    r_part = jnp.concatenate([dg_mix0, dg_mix1, db_mix0, db_mix1, dg_mlp0, dg_mlp1, db_mlp0, db_mlp1,
                              wide(g_sinks, 8), wide(g_rel, SMALL_ROWS - 16)], axis=0)
    by_dev = lambda g: g.reshape((N_DEV, g.shape[0] // N_DEV) + g.shape[1:])
    early = [by_dev(g_o2), by_dev(g_qkv), g_up, g_dn, by_dev(g_o),
             by_dev(g_uv.transpose(1, 0, 2).reshape(MLA_C, H * MLA_V))]
    dqs, dkc, dv, (r_all, p_o2, p_qkv, p_up, p_dn, p_o, p_uv) = _mla_attn_bwd(
        qs, qst, kc, kct, dol, lse, delta, gather=[r_part], scatter=early)
    grad_x, g_in, g_uq_n, g_uq_r, g_uk, g_gq, g_gkv = _mla_pre_bwd(
        dqs, dkc, dv, h, x2d, dres1, cq, qn, cos, sin, w_in, g_q, g_kv, w_uq_n, w_uq_r, w_uk)
    g_uq = jnp.concatenate([g_uq_n.reshape(MLA_QR, H, MLA_NOPE), g_uq_r.reshape(MLA_QR, H, MLA_ROPE)], -1)
    g_gains = jnp.pad(jnp.concatenate([g_gq.reshape(N_DEV, n_gq), g_gkv.reshape(N_DEV, n_gkv)], axis=1)[:, None, :],
                      ((0, 0), (0, 7), (0, 128 - n_gq - n_gkv)))
    p_in, p_uq, p_uk, p_gains = _exchange(
        [], [by_dev(g_in), by_dev(g_uq.reshape(MLA_QR, -1)), by_dev(g_uk.transpose(1, 0, 2).reshape(MLA_C, -1)),
             g_gains], name="exchange_mla_in_grads")

    res = {}

    def adam(name, parts, names, to_slab, from_slab):
        out = _adamw(parts, to_slab(W), to_slab(M), to_slab(V), name="adamw_" + name)
        for k in range(4):
            for n, a in zip(names, from_slab(out[k])):
                res[(k, n)] = a.reshape(W[n].shape)

    one = lambda n: (lambda d: slab(d, n))
    adam("swa_w_o", p_o2, ["swa_w_o"], one("swa_w_o"), lambda s: [s])
    dq_cols = SWA_QH * SWA_D
    adam("swa_qkv", p_qkv, ["swa_w_q", "kv_w_shared"],
         lambda d: jnp.concatenate([slab(d, "swa_w_q"), slab(d, "kv_w_shared")], axis=1),
         lambda s: [s[:, :dq_cols], s[:, dq_cols:]])
    layers_as_rows = lambda a: a.reshape((-1,) + a.shape[-1:])
    adam("mlp_w_up", p_up.reshape(N_DEV, -1, p_up.shape[-1]), ["mlp_w_up"],
         lambda d: layers_as_rows(d["mlp_w_up"]), lambda s: [s])
    adam("mlp_w_down", p_dn.reshape(N_DEV, -1, p_dn.shape[-1]), ["mlp_w_down"],
         lambda d: layers_as_rows(d["mlp_w_down"]), lambda s: [s])
    adam("mla_w_o", p_o, ["mla_w_o"], one("mla_w_o"), lambda s: [s])
    adam("mla_w_uv", p_uv, ["mla_w_uv"], one("mla_w_uv"), lambda s: [s])
    adam("mla_w_in", p_in, ["mla_w_in"], one("mla_w_in"), lambda s: [s])
    adam("mla_w_uq", p_uq, ["mla_w_uq"], one("mla_w_uq"), lambda s: [s])
    adam("mla_w_uk", p_uk, ["mla_w_uk"], one("mla_w_uk"), lambda s: [s])
    res.update(_adamw_small(r_all, p_gains, W, M, V))
    return (loss, grad_x.reshape(x.shape), *[res[(k, n)] for k in range(4) for n in WEIGHTS])
```

```python
import math

import numpy as np
import jax
import jax.numpy as jnp
from jax import lax
from jax.experimental import pallas as pl
from jax.experimental.pallas import tpu as pltpu

F32 = jnp.float32
_MXU_DTYPE = jnp.bfloat16

D_MODEL = 1024
DEPTH = 2
MLA_HEADS = 8
MLA_NOPE = 128
MLA_ROPE = 64
MLA_V = 128
MLA_QR = 384
MLA_C = 256
MLA_DK = 384
MLA_DT = MLA_C + MLA_ROPE
ROPE_THETA = 10000.0
SWA_QH = 16
SWA_KVH = 4
SWA_D = 64
SWA_BLOCK = 128
REL_BUCKETS = 32
REL_MAX_DIST = 128
D_FF = 4096
LN_EPS = 1e-5
RMS_EPS = 1e-6
ALPHA = (2 * DEPTH) ** 0.25
ADAM_LR, ADAM_B1, ADAM_B2, ADAM_EPS, ADAM_WD, ADAM_STEP = 0.001, 0.9, 0.999, 1e-08, 0.01, 10

N_DEV = 8
AXES = ("x", "y", "c")
V7X_VMEM_BYTES = 64 * 1024 * 1024
VMEM_LIMIT = V7X_VMEM_BYTES - 8 * 1024 * 1024
LANES = 1024
ATT_TQ = 512
ATT_TK = 512
ATT_HEAD_GROUP = 1
ATT_FWD_HEAD_GROUP = 2

NT = (((1,), (1,)), ((), ()))
TN = (((0,), (0,)), ((), ()))
S = jax.ShapeDtypeStruct


def _params(*sem, vmem=VMEM_LIMIT):
    return pltpu.CompilerParams(dimension_semantics=sem, vmem_limit_bytes=vmem)


def _dot(a, b, dims=None, precision=None):
    if dims is None:
        return jnp.dot(a, b, preferred_element_type=F32, precision=precision)
    return lax.dot_general(a, b, dims, preferred_element_type=F32, precision=precision)


def _mx(v):
    return v.astype(_MXU_DTYPE)


ROPE_TABLE_W = 128


def _tile_heads(t):
    return jnp.concatenate([t] * (MLA_HEADS * MLA_ROPE // ROPE_TABLE_W), axis=1)


def _swap_halves_64(v):
    return jnp.concatenate([v[:, 32:], v[:, :32]], axis=-1)


def _swap_halves_groups(v):
    n = v.shape[-1]
    lane = lax.broadcasted_iota(jnp.int32, v.shape, 1)
    return jnp.where(lane % 64 < 32, pltpu.roll(v, n - 32, 1), pltpu.roll(v, 32, 1))


def _mm(a, b, *, name, ta=False, tb=False, add=None, out_dtype=F32, tm=512, tn=512, tk=512, precision=None):
    M, K = (a.shape[1], a.shape[0]) if ta else a.shape
    N = b.shape[0] if tb else b.shape[1]
    tm, tn, tk = min(tm, M), min(tn, N), min(tk, K)
    assert M % tm == 0 and N % tn == 0 and K % tk == 0, (M, N, K, tm, tn, tk)
    nk = K // tk
    dims = (((0 if ta else 1,), (1 if tb else 0,)), ((), ()))
    has_add = add is not None

    def body(*refs):
        if has_add:
            a_ref, b_ref, add_ref, o_ref, acc = refs
        else:
            a_ref, b_ref, o_ref, acc = refs
        k = pl.program_id(2)
        av, bv = a_ref[...], b_ref[...]
        if precision is None:
            av, bv = _mx(av), _mx(bv)
        part = _dot(av, bv, dims, precision)
        if nk == 1:
            o_ref[...] = (part + add_ref[...] if has_add else part).astype(out_dtype)
            return

        @pl.when(k == 0)
        def _():
            acc[...] = add_ref[...] if has_add else jnp.zeros_like(acc)

        acc[...] += part

        @pl.when(k == nk - 1)
        def _():
            o_ref[...] = acc[...].astype(out_dtype)

    a_spec = pl.BlockSpec((tk, tm), lambda i, j, k: (k, i)) if ta else pl.BlockSpec((tm, tk), lambda i, j, k: (i, k))
    b_spec = pl.BlockSpec((tn, tk), lambda i, j, k: (j, k)) if tb else pl.BlockSpec((tk, tn), lambda i, j, k: (k, j))
    in_specs = [a_spec, b_spec]
    args = [a, b]
    if has_add:
        in_specs.append(pl.BlockSpec((tm, tn), lambda i, j, k: (i, j)))
        args.append(add)
    return pl.pallas_call(
        body, name=name, grid=(M // tm, N // tn, nk), in_specs=in_specs,
        out_specs=pl.BlockSpec((tm, tn), lambda i, j, k: (i, j)), out_shape=S((M, N), out_dtype),
        scratch_shapes=[pltpu.VMEM((tm, tn), F32)],
        compiler_params=_params("parallel", "parallel", "arbitrary"))(*args)


def _ln_fwd_math(z, g, b):
    mu = jnp.mean(z, axis=-1, keepdims=True)
    zc = z - mu
    var = jnp.mean(zc * zc, axis=-1, keepdims=True)
    rstd = lax.rsqrt(var + LN_EPS)
    xhat = zc * rstd
    return xhat * g + b, xhat, rstd


def _ln_bwd_math(dxo, xhat, rstd, g):
    dxh = dxo * g
    m1 = jnp.mean(dxh, axis=-1, keepdims=True)
    m2 = jnp.mean(dxh * xhat, axis=-1, keepdims=True)
    dz = rstd * (dxh - m1 - xhat * m2)
    dg = jnp.sum(dxo * xhat, axis=0, keepdims=True)
    db = jnp.sum(dxo, axis=0, keepdims=True)
    return dz, dg, db


def _rms_fwd_math(xr, g):
    r = lax.rsqrt(jnp.mean(xr * xr, axis=-1, keepdims=True) + RMS_EPS)
    return xr * r * g


def _rms_bwd_math(dy, xr, g):
    r = lax.rsqrt(jnp.mean(xr * xr, axis=-1, keepdims=True) + RMS_EPS)
    gy = dy * g
    dx = r * gy - xr * (r * r * r) * jnp.mean(gy * xr, axis=-1, keepdims=True)
    dg = jnp.sum(dy * xr * r, axis=0, keepdims=True)
    return dx, dg


def _mla_pre_fwd(x, w_in, g_q, g_kv, w_uq_n, w_uq_r, w_uk_t, cos, sin):
    T = x.shape[0]
    tm = min(ATT_TQ, T)
    nq = T // tm
    H = MLA_HEADS

    tk = min(ATT_TK, T)

    def body(x_ref, win_ref, gq_ref, gkv_ref, wn_ref, wr_ref, wuk_ref, cos_ref, sin_ref,
             h_ref, kc_ref, kct_ref, qs_ref, qst_ref, cq_ref, qn_ref):
        h = _dot(_mx(x_ref[...]), win_ref[...])
        h_ref[...] = h
        cos_v, sin_v = _tile_heads(cos_ref[...]), _tile_heads(sin_ref[...])
        cq = _mx(_rms_fwd_math(h[:, :MLA_QR], gq_ref[...]))
        ckv = _rms_fwd_math(h[:, MLA_QR:MLA_QR + MLA_C], gkv_ref[...])
        krr = h[:, MLA_QR + MLA_C:]
        kr = krr * cos_v[:, :MLA_ROPE] + _swap_halves_64(krr) * sin_v[:, :MLA_ROPE]
        kr_pad = jnp.concatenate([kr, jnp.zeros((tm, MLA_DK - MLA_C - MLA_ROPE), F32)], axis=1)
        kc_ref[:, 0:MLA_C] = _mx(ckv)
        kc_ref[:, MLA_C:] = _mx(kr_pad)
        kct_ref[0:MLA_C, :] = _mx(ckv.T)
        kct_ref[MLA_C:, :] = _mx(kr_pad.T[0:MLA_ROPE, :])
        cq_ref[...] = cq
        qnb = _mx(_dot(cq, wn_ref[...]))
        qn_ref[...] = qnb
        qr = _dot(cq, wr_ref[...])
        qrr = qr * cos_v + _swap_halves_groups(qr) * sin_v
        qrr_t = qrr.T
        for hd in range(H):
            ql = _dot(qnb[:, MLA_NOPE * hd:MLA_NOPE * (hd + 1)], wuk_ref[hd])
            qst_ref[0, 0:MLA_C, tm * hd:tm * (hd + 1)] = _mx(ql.T)
            qst_ref[0, MLA_C:, tm * hd:tm * (hd + 1)] = _mx(qrr_t[MLA_ROPE * hd:MLA_ROPE * (hd + 1), :])
            qs_ref[0, hd, :, 0:MLA_C] = _mx(ql)
            qs_ref[0, hd, :, MLA_C:MLA_C + MLA_ROPE] = _mx(qrr[:, MLA_ROPE * hd:MLA_ROPE * (hd + 1)])
            qs_ref[0, hd, :, MLA_C + MLA_ROPE:] = jnp.zeros((tm, MLA_DK - MLA_C - MLA_ROPE), _MXU_DTYPE)

    full = lambda shp: pl.BlockSpec(shp, lambda i: (0,) * len(shp))
    rows = lambda n: pl.BlockSpec((tm, n), lambda i: (i, 0))
    n_in = w_in.shape[1]
    return pl.pallas_call(
        body, name="mla_pre_fwd", grid=(nq,),
        in_specs=[rows(D_MODEL), full(w_in.shape), full(g_q.shape), full(g_kv.shape), full(w_uq_n.shape),
                  full(w_uq_r.shape), full(w_uk_t.shape), rows(ROPE_TABLE_W), rows(ROPE_TABLE_W)],
        out_specs=[rows(n_in), rows(MLA_DK),
                   pl.BlockSpec((None, MLA_DT, tm), lambda i: (i * tm // tk, 0, i % (tk // tm))),
                   pl.BlockSpec((1, H, tm, MLA_DK), lambda i: (i, 0, 0, 0)),
                   pl.BlockSpec((1, MLA_DT, H * tm), lambda i: (i, 0, 0)), rows(MLA_QR), rows(H * MLA_NOPE)],
        out_shape=[S((T, n_in), F32), S((T, MLA_DK), _MXU_DTYPE), S((T // tk, MLA_DT, tk), _MXU_DTYPE),
                   S((nq, H, tm, MLA_DK), _MXU_DTYPE), S((nq, MLA_DT, H * tm), _MXU_DTYPE),
                   S((T, MLA_QR), _MXU_DTYPE), S((T, H * MLA_NOPE), _MXU_DTYPE)],
        compiler_params=_params("parallel"))(x, w_in, g_q, g_kv, w_uq_n, w_uq_r, w_uk_t, cos, sin)


def _att_steps(T, tq, tk):
    qi, kj = [], []
    for i in range(T // tq):
        for j in range((i * tq + tq - 1) // tk + 1):
            qi.append(i)
            kj.append(j)
    return jnp.asarray(np.array(qi, np.int32)), jnp.asarray(np.array(kj, np.int32))


def _ride_exchange(st, n_steps, ins, outs, n_gather, sems):
    if not ins:
        return

    @pl.when(st == 0)
    def _():
        for cp in _exchange_copies(ins, outs, n_gather, *sems):
            cp.start()

    @pl.when(st == n_steps - 1)
    def _():
        for cp in _exchange_copies(ins, outs, n_gather, *sems):
            cp.wait()


def _mla_attn_fwd(qs, kc, kct, gather=(), scatter=()):
    nq, H, tq, DK = qs.shape
    T = kc.shape[0]
    tk = min(ATT_TK, T)
    scale = (MLA_NOPE + MLA_ROPE) ** -0.5
    c2 = scale * math.log2(math.e)
    qi, kj = _att_steps(T, tq, tk)
    n_steps = int(qi.shape[0])
    hg = ATT_FWD_HEAD_GROUP
    R = hg * tq
    n_x = len(gather) + len(scatter)

    def body(qi_ref, kj_ref, q_ref, k_ref, kt_ref, *rest):
        x_ins, (o_ref, lse_ref), x_outs = rest[:n_x], rest[n_x:n_x + 2], rest[n_x + 2:2 * n_x + 2]
        m_sc, l_sc, acc_sc = rest[2 * n_x + 2:2 * n_x + 5]
        st = pl.program_id(0)
        _ride_exchange(st, n_steps, x_ins, x_outs, len(gather), rest[2 * n_x + 5:])
        i, j = qi_ref[st], kj_ref[st]
        j_last = (i * tq + tq - 1) // tk

        @pl.when(j == 0)
        def _():
            m_sc[...] = jnp.full_like(m_sc, -jnp.inf)
            l_sc[...] = jnp.zeros_like(l_sc)
            acc_sc[...] = jnp.zeros_like(acc_sc)

        def step(masked):
            k = k_ref[...]
            vt = kt_ref[0:MLA_C, :]
            if masked:
                key = lax.broadcasted_iota(jnp.int32, (tk, R), 0) + j * tk
                qry = lax.broadcasted_iota(jnp.int32, (tk, R), 1) % tq + i * tq
                causal = key <= qry
            n_g = H // hg
            qk = lambda g: _dot(k, q_ref[0, g * hg:(g + 1) * hg].reshape(R, DK), NT)
            def accumulate(g, a, pb):
                cs = slice(g * R, (g + 1) * R)
                acc_sc[:, cs] = a * acc_sc[:, cs] + _dot(vt, pb)

            s_next = qk(0)
            pending = None
            for g in range(n_g):
                cs = slice(g * R, (g + 1) * R)
                s = s_next
                if g + 1 < n_g:
                    s_next = qk(g + 1)
                if pending is not None:
                    accumulate(*pending)
                if masked:
                    s = jnp.where(causal, s, -jnp.inf)
                m_prev = m_sc[:, cs]
                m_new = jnp.maximum(m_prev, jnp.max(s, axis=0, keepdims=True))
                a = jnp.exp2((m_prev - m_new) * c2)
                p = jnp.exp2((s - m_new) * c2)
                l_sc[:, cs] = a * l_sc[:, cs] + jnp.sum(p, axis=0, keepdims=True)
                m_sc[:, cs] = m_new
                pending = (g, a, _mx(p))
            accumulate(*pending)

        pl.when(j == j_last)(lambda: step(True))
        pl.when(j != j_last)(lambda: step(False))

        @pl.when(j == j_last)
        def _():
            o_ref[0] = _mx(acc_sc[...] / l_sc[...])
            lse_ref[0] = m_sc[...] * scale + jnp.log(l_sc[...])

    hbm = pl.BlockSpec(memory_space=pl.ANY)
    x_shapes, x_sems = _exchange_shapes(gather, scatter) if n_x else ([], [])
    gs = pltpu.PrefetchScalarGridSpec(
        num_scalar_prefetch=2, grid=(n_steps,),
        in_specs=[pl.BlockSpec((1, H, tq, DK), lambda s, qi, kj: (qi[s], 0, 0, 0)),
                  pl.BlockSpec((tk, DK), lambda s, qi, kj: (kj[s], 0)),
                  pl.BlockSpec((None, MLA_DT, tk), lambda s, qi, kj: (kj[s], 0, 0))] + [hbm] * n_x,
        out_specs=[pl.BlockSpec((1, MLA_C, H * tq), lambda s, qi, kj: (qi[s], 0, 0)),
                   pl.BlockSpec((1, 1, H * tq), lambda s, qi, kj: (qi[s], 0, 0))] + [hbm] * n_x,
        scratch_shapes=[pltpu.VMEM((1, H * tq), F32), pltpu.VMEM((1, H * tq), F32),
                        pltpu.VMEM((MLA_C, H * tq), F32)] + x_sems)
    res = pl.pallas_call(
        body, name="mla_attn_fwd", grid_spec=gs,
        out_shape=[S((nq, MLA_C, H * tq), _MXU_DTYPE), S((nq, 1, H * tq), F32)] + x_shapes,
        compiler_params=_params("arbitrary"))(qi, kj, qs, kc, kct, *gather, *scatter)
    return res[0], res[1], res[2:]


def _mla_attn_bwd(qs, qst, kc, kct, dol, lse, delta, gather=(), scatter=()):
    nq, H, tq, DK = qs.shape
    T = kc.shape[0]
    tk = min(ATT_TK, T)
    scale = (MLA_NOPE + MLA_ROPE) ** -0.5
    log2e = math.log2(math.e)
    qi, kj = _att_steps(T, tq, tk)
    n_steps = int(qi.shape[0])
    hg = ATT_HEAD_GROUP
    R = hg * tq
    n_x = len(gather) + len(scatter)

    def body(qi_ref, kj_ref, q_ref, qt_ref, k_ref, kt_ref, do_ref, lse_ref, dl_ref, *rest):
        x_ins, (dq_ref, dk_ref, dv_ref), x_outs = rest[:n_x], rest[n_x:n_x + 3], rest[n_x + 3:2 * n_x + 3]
        dk_acc, dv_acc, sem = rest[2 * n_x + 3:2 * n_x + 6]
        st = pl.program_id(0)
        _ride_exchange(st, n_steps, x_ins, x_outs, len(gather), rest[2 * n_x + 6:])
        i, j = qi_ref[st], kj_ref[st]
        j_last = (i * tq + tq - 1) // tk

        @pl.when(st == 0)
        def _():
            dk_acc[...] = jnp.zeros_like(dk_acc)
            dv_acc[...] = jnp.zeros_like(dv_acc)

        @pl.when(j == 0)
        def _():
            dq_ref[...] = jnp.zeros_like(dq_ref)

        def step(masked):
            k, kt = k_ref[...], kt_ref[...]
            v = k[:, :MLA_C]
            if masked:
                key = lax.broadcasted_iota(jnp.int32, (tk, R), 0) + j * tk
                qry = lax.broadcasted_iota(jnp.int32, (tk, R), 1) % tq + i * tq
                causal = key <= qry
            dkt_c = jnp.zeros((MLA_DT, tk), F32)
            dvt_c = jnp.zeros((MLA_C, tk), F32)
            n_g = H // hg

            def scores(g):
                q = q_ref[0, g * hg:(g + 1) * hg].reshape(R, DK)
                dot = do_ref[0, :, g * R:(g + 1) * R]
                return dot, _dot(k, q, NT), _dot(v, dot)

            nxt = scores(0)
            for g in range(n_g):
                cs = slice(g * R, (g + 1) * R)
                dot, s, dp = nxt
                if g + 1 < n_g:
                    nxt = scores(g + 1)
                p = jnp.exp2(s * (scale * log2e) - lse_ref[0, :, cs] * log2e)
                if masked:
                    p = jnp.where(causal, p, 0.0)
                dsb = _mx(p * (dp - dl_ref[0, :, cs]))
                dq_ref[0, :, cs] += _dot(kt, dsb)
                dkt_c = dkt_c + _dot(qt_ref[0, :, cs], dsb, NT)
                dvt_c = dvt_c + _dot(dot, _mx(p), NT)
            dk_acc[j] += dkt_c * scale
            dv_acc[j] += dvt_c

        pl.when(j == j_last)(lambda: step(True))
        pl.when(j != j_last)(lambda: step(False))

        @pl.when(j == j_last)
        def _():
            dq_ref[...] = dq_ref[...] * scale

        @pl.when(st == n_steps - 1)
        def _():
            c1 = pltpu.make_async_copy(dk_acc, dk_ref, sem.at[0])
            c2 = pltpu.make_async_copy(dv_acc, dv_ref, sem.at[1])
            c1.start()
            c2.start()
            c1.wait()
            c2.wait()

    cols = lambda n: pl.BlockSpec((1, n, H * tq), lambda s, qi, kj: (qi[s], 0, 0))
    hbm = pl.BlockSpec(memory_space=pl.ANY)
    x_shapes, x_sems = _exchange_shapes(gather, scatter) if n_x else ([], [])
    gs = pltpu.PrefetchScalarGridSpec(
        num_scalar_prefetch=2, grid=(n_steps,),
        in_specs=[pl.BlockSpec((1, H, tq, DK), lambda s, qi, kj: (qi[s], 0, 0, 0)), cols(MLA_DT),
                  pl.BlockSpec((tk, DK), lambda s, qi, kj: (kj[s], 0)),
                  pl.BlockSpec((None, MLA_DT, tk), lambda s, qi, kj: (kj[s], 0, 0)),
                  cols(MLA_C), cols(1), cols(1)] + [hbm] * n_x,
        out_specs=[cols(MLA_DT), hbm, hbm] + [hbm] * n_x,
        scratch_shapes=[pltpu.VMEM((T // tk, MLA_DT, tk), F32), pltpu.VMEM((T // tk, MLA_C, tk), F32),
                        pltpu.SemaphoreType.DMA((2,))] + x_sems)
    res = pl.pallas_call(
        body, name="mla_attn_bwd", grid_spec=gs,
        out_shape=[S((nq, MLA_DT, H * tq), F32), S((T // tk, MLA_DT, tk), F32),
                   S((T // tk, MLA_C, tk), F32)] + x_shapes,
        compiler_params=_params("arbitrary"))(qi, kj, qs, qst, kc, kct, dol, lse, delta, *gather, *scatter)
    return res[0], res[1], res[2], res[3:]


def _mla_uv_fwd(olat, w_uv):
    nq, C, cols = olat.shape
    H = w_uv.shape[0]
    tq = cols // H

    def body(ol_ref, wuv_ref, o_ref):
        for hd in range(H):
            o_ref[:, MLA_V * hd:MLA_V * (hd + 1)] = _mx(_dot(ol_ref[0, :, tq * hd:tq * (hd + 1)], wuv_ref[hd], TN))

    return pl.pallas_call(
        body, name="mla_uv_fwd", grid=(nq,),
        in_specs=[pl.BlockSpec((1, C, cols), lambda i: (i, 0, 0)), pl.BlockSpec(w_uv.shape, lambda i: (0, 0, 0))],
        out_specs=pl.BlockSpec((tq, H * MLA_V), lambda i: (i, 0)), out_shape=S((nq * tq, H * MLA_V), _MXU_DTYPE),
        compiler_params=_params("parallel"))(olat, w_uv)


def _mla_uv_bwd(do, olat, w_uv):
    nq, C, cols = olat.shape
    H = w_uv.shape[0]
    tq = cols // H

    def body(do_ref, ol_ref, wuv_ref, dol_ref, dl_ref, dw_ref):
        @pl.when(pl.program_id(0) == 0)
        def _():
            dw_ref[...] = jnp.zeros_like(dw_ref)

        dov = do_ref[...]
        for hd in range(H):
            cs = slice(tq * hd, tq * (hd + 1))
            doh = _mx(dov[:, MLA_V * hd:MLA_V * (hd + 1)])
            ol = ol_ref[0, :, cs]
            dol = _dot(wuv_ref[hd], doh, NT)
            dol_ref[0, :, cs] = _mx(dol)
            dl_ref[0, :, cs] = jnp.sum(dol * ol.astype(F32), axis=0, keepdims=True)
            dw_ref[hd] += _dot(ol, doh)

    blk = lambda n: pl.BlockSpec((1, n, cols), lambda i: (i, 0, 0))
    return pl.pallas_call(
        body, name="mla_uv_bwd", grid=(nq,),
        in_specs=[pl.BlockSpec((tq, H * MLA_V), lambda i: (i, 0)), blk(C), pl.BlockSpec(w_uv.shape, lambda i: (0, 0, 0))],
        out_specs=[blk(C), blk(1), pl.BlockSpec(w_uv.shape, lambda i: (0, 0, 0))],
        out_shape=[S(olat.shape, _MXU_DTYPE), S((nq, 1, cols), F32), S(w_uv.shape, F32)],
        compiler_params=_params("arbitrary"))(do, olat, w_uv)


def _mla_pre_bwd(dqs, dkc, dv, h, x, dres, cq, qn, cos, sin, w_in, g_q, g_kv, w_uq_n, w_uq_r, w_uk):
    nq, DK, cols = dqs.shape
    H = w_uk.shape[0]
    tm = cols // H
    T = nq * tm
    tk = dv.shape[2]
    n_in = w_in.shape[1]

    def body(dqs_ref, dkc_ref, dv_ref, h_ref, x_ref, dres_ref, cq_ref, qn_ref, cos_ref, sin_ref,
             win_ref, gq_ref, gkv_ref, wn_ref, wr_ref, wuk_ref,
             gx_ref, dwin_ref, dwn_ref, dwr_ref, dwuk_ref, dgq_ref, dgkv_ref, dqn_sc, dqr_sc, dh_sc):
        @pl.when(pl.program_id(0) == 0)
        def _():
            for r in (dwin_ref, dwn_ref, dwr_ref, dwuk_ref, dgq_ref, dgkv_ref):
                r[...] = jnp.zeros_like(r)

        cos_v, sin_v = _tile_heads(cos_ref[...]), _tile_heads(sin_ref[...])
        qnb = qn_ref[...]
        for hd in range(H):
            cs = slice(tm * hd, tm * (hd + 1))
            dql = _mx(dqs_ref[0, 0:MLA_C, cs])
            dqn_sc[:, MLA_NOPE * hd:MLA_NOPE * (hd + 1)] = _dot(dql, wuk_ref[hd], TN)
            dwuk_ref[hd] += _dot(dql, qnb[:, MLA_NOPE * hd:MLA_NOPE * (hd + 1)])
            dqr_sc[MLA_ROPE * hd:MLA_ROPE * (hd + 1), :] = dqs_ref[0, MLA_C:MLA_C + MLA_ROPE, cs]
        dqr = dqr_sc[...].T
        dqrb = _mx(dqr * cos_v + _swap_halves_groups(dqr * sin_v))
        dqnb = _mx(dqn_sc[...])
        cq = cq_ref[...]
        dwn_ref[...] += _dot(cq, dqnb, TN)
        dwr_ref[...] += _dot(cq, dqrb, TN)
        dcq = _dot(dqnb, wn_ref[...], NT) + _dot(dqrb, wr_ref[...], NT)
        hv = h_ref[...]
        dxq, dgq = _rms_bwd_math(dcq, hv[:, :MLA_QR], gq_ref[...])
        dgq_ref[...] += dgq
        dckv = (dkc_ref[0:MLA_C, :] + dv_ref[...]).T
        dxkv, dgkv = _rms_bwd_math(dckv, hv[:, MLA_QR:MLA_QR + MLA_C], gkv_ref[...])
        dgkv_ref[...] += dgkv
        dkr = jnp.concatenate([dkc_ref[MLA_C:, :], jnp.zeros((128 - MLA_ROPE, tm), F32)], axis=0).T[:, :MLA_ROPE]
        dkr_raw = dkr * cos_v[:, :MLA_ROPE] + _swap_halves_64(dkr * sin_v[:, :MLA_ROPE])
        dh_sc[:, 0:MLA_QR] = dxq
        dh_sc[:, MLA_QR:MLA_QR + MLA_C] = dxkv
        dh_sc[:, MLA_QR + MLA_C:] = dkr_raw
        dhb = _mx(dh_sc[...])
        gx_ref[...] = dres_ref[...] + _dot(dhb, win_ref[...], NT)
        dwin_ref[...] += _dot(_mx(x_ref[...]), dhb, TN)

    full = lambda shp: pl.BlockSpec(shp, lambda i: (0,) * len(shp))
    rows = lambda n: pl.BlockSpec((tm, n), lambda i: (i, 0))
    return pl.pallas_call(
        body, name="mla_pre_bwd", grid=(nq,),
        in_specs=[pl.BlockSpec((1, DK, cols), lambda i: (i, 0, 0)),
                  pl.BlockSpec((None, DK, tm), lambda i: (i * tm // tk, 0, i % (tk // tm))),
                  pl.BlockSpec((None, MLA_C, tm), lambda i: (i * tm // tk, 0, i % (tk // tm))), rows(n_in),
                  rows(D_MODEL), rows(D_MODEL), rows(MLA_QR), rows(H * MLA_NOPE), rows(ROPE_TABLE_W), rows(ROPE_TABLE_W),
                  full(w_in.shape), full(g_q.shape), full(g_kv.shape), full(w_uq_n.shape), full(w_uq_r.shape),
                  full(w_uk.shape)],
        out_specs=[rows(D_MODEL), full(w_in.shape), full(w_uq_n.shape), full(w_uq_r.shape), full(w_uk.shape),
                   full(g_q.shape), full(g_kv.shape)],
        out_shape=[S((T, D_MODEL), F32), S(w_in.shape, F32), S(w_uq_n.shape, F32), S(w_uq_r.shape, F32),
                   S(w_uk.shape, F32), S(g_q.shape, F32), S(g_kv.shape, F32)],
        scratch_shapes=[pltpu.VMEM((tm, H * MLA_NOPE), F32), pltpu.VMEM((H * MLA_ROPE, tm), F32),
                        pltpu.VMEM((tm, n_in), F32)],
        compiler_params=_params("arbitrary"))(dqs, dkc, dv, h, x, dres, cq, qn, cos, sin, w_in, g_q, g_kv,
                                              w_uq_n, w_uq_r, w_uk)


def _proj_ln_fwd(a, w, xres, res_gb, g, b, *, name, tm=512):
    T, K = a.shape
    tm = min(tm, T)
    gp, bp = res_gb if res_gb is not None else (None, None)

    def body(a_ref, w_ref, x_ref, *rest):
        if res_gb is not None:
            x = x_ref[...] * rest[0][...] + rest[1][...]
            rest = rest[2:]
        else:
            x = x_ref[...]
        g_ref, b_ref, xob_ref, xt_ref, xh_ref, rs_ref = rest
        z = ALPHA * x + _dot(a_ref[...], w_ref[...])
        xo, xhat, rstd = _ln_fwd_math(z, g_ref[...], b_ref[...])
        xob_ref[...] = _mx(xo)
        xt_ref[...] = _mx(xo.T)
        xh_ref[...] = xhat
        rs_ref[...] = rstd

    rows = lambda n: pl.BlockSpec((tm, n), lambda i: (i, 0))
    full = lambda shp: pl.BlockSpec(shp, lambda i: (0,) * len(shp))
    extra = [gp, bp] if res_gb is not None else []
    return pl.pallas_call(
        body, name=name, grid=(T // tm,),
        in_specs=[rows(K), full(w.shape), rows(D_MODEL)] + [full(e.shape) for e in extra] + [full(g.shape), full(b.shape)],
        out_specs=[rows(D_MODEL), pl.BlockSpec((D_MODEL, tm), lambda i: (0, i)), rows(D_MODEL), rows(1)],
        out_shape=[S((T, D_MODEL), _MXU_DTYPE), S((D_MODEL, T), _MXU_DTYPE), S((T, D_MODEL), F32), S((T, 1), F32)],
        compiler_params=_params("parallel"))(a, w, xres, *extra, g, b)


def _proj_ln_bwd(dxo, xhat, rstd, g, a, w, *, name, tm=512):
    T, K = a.shape
    tm = min(tm, T)

    def body(dxo_ref, xh_ref, rs_ref, g_ref, a_ref, w_ref, dres_ref, da_ref, dw_ref, dg_ref, db_ref):
        @pl.when(pl.program_id(0) == 0)
        def _():
            for r in (dw_ref, dg_ref, db_ref):
                r[...] = jnp.zeros_like(r)

        dz, dg, db = _ln_bwd_math(dxo_ref[...], xh_ref[...], rs_ref[...], g_ref[...])
        dg_ref[...] += dg
        db_ref[...] += db
        dres_ref[...] = ALPHA * dz
        dzb = _mx(dz)
        da_ref[...] = _dot(dzb, w_ref[...], NT)
        dw_ref[...] += _dot(a_ref[...], dzb, TN)

    rows = lambda n: pl.BlockSpec((tm, n), lambda i: (i, 0))
    full = lambda shp: pl.BlockSpec(shp, lambda i: (0,) * len(shp))
    return pl.pallas_call(
        body, name=name, grid=(T // tm,),
        in_specs=[rows(D_MODEL), rows(D_MODEL), rows(1), full(g.shape), rows(K), full(w.shape)],
        out_specs=[rows(D_MODEL), rows(K), full(w.shape), full(g.shape), full(g.shape)],
        out_shape=[S((T, D_MODEL), F32), S((T, K), F32), S(w.shape, F32), S(g.shape, F32), S(g.shape, F32)],
        compiler_params=_params("arbitrary"))(dxo, xhat, rstd, g, a, w)


def _mlp_fwd(xb, xh_in, g_in, b_in, w_up, w_dn, layer, g, b, *, tm=1024):
    T = xb.shape[0]
    tm = min(tm, T)
    nj, _, _, fc = w_up.shape

    def body(xb_ref, xh_ref_in, gi_ref, bi_ref, wu_ref, wd_ref, g_ref, b_ref, u_ref, ut_ref, xob_ref, xh_ref, rs_ref, acc):
        j = pl.program_id(1)

        @pl.when(j == 0)
        def _():
            acc[...] = ALPHA * (xh_ref_in[...] * gi_ref[...] + bi_ref[...])

        u = _dot(xb_ref[...], wu_ref[...])
        u_ref[...] = _mx(u)
        ut_ref[...] = _mx(u.T)
        r = jnp.maximum(u, 0.0)
        acc[...] += _dot(_mx(r * r), wd_ref[...])

        @pl.when(j == nj - 1)
        def _():
            xo, xhat, rstd = _ln_fwd_math(acc[...], g_ref[...], b_ref[...])
            xob_ref[...] = _mx(xo)
            xh_ref[...] = xhat
            rs_ref[...] = rstd

    rows = lambda n: pl.BlockSpec((tm, n), lambda i, j: (i, 0))
    full = lambda shp: pl.BlockSpec(shp, lambda i, j: (0,) * len(shp))
    return pl.pallas_call(
        body, name=f"mlp_fwd_{layer}", grid=(T // tm, nj),
        in_specs=[rows(D_MODEL), rows(D_MODEL), full(g_in.shape), full(b_in.shape),
                  pl.BlockSpec((None, None, D_MODEL, fc), lambda i, j: (j, layer, 0, 0)),
                  pl.BlockSpec((None, None, fc, D_MODEL), lambda i, j: (j, layer, 0, 0)),
                  full(g.shape), full(b.shape)],
        out_specs=[pl.BlockSpec((tm, fc), lambda i, j: (i, j)), pl.BlockSpec((fc, tm), lambda i, j: (j, i)),
                   rows(D_MODEL), rows(D_MODEL), rows(1)],
        out_shape=[S((T, nj * fc), _MXU_DTYPE), S((nj * fc, T), _MXU_DTYPE), S((T, D_MODEL), _MXU_DTYPE),
                   S((T, D_MODEL), F32), S((T, 1), F32)],
        scratch_shapes=[pltpu.VMEM((tm, D_MODEL), F32)],
        compiler_params=_params("parallel", "arbitrary"))(xb, xh_in, g_in, b_in, w_up, w_dn, g, b)


def _mlp_bwd_dx(dxo, xhat, rstd, g, u, w_up, w_dn, layer, *, tm=1024):
    T = dxo.shape[0]
    tm = min(tm, T)
    nj, _, _, fc = w_up.shape

    def body(dxo_ref, xh_ref, rs_ref, g_ref, u_ref, wu_ref, wd_ref, dx_ref, du_ref, dyb_ref, dg_ref, db_ref, acc, dy_sc):
        i, j = pl.program_id(0), pl.program_id(1)

        @pl.when((i == 0) & (j == 0))
        def _():
            dg_ref[...] = jnp.zeros_like(dg_ref)
            db_ref[...] = jnp.zeros_like(db_ref)

        @pl.when(j == 0)
        def _():
            dz, dg, db = _ln_bwd_math(dxo_ref[...], xh_ref[...], rs_ref[...], g_ref[...])
            dg_ref[...] += dg
            db_ref[...] += db
            acc[...] = ALPHA * dz
            dy_sc[...] = _mx(dz)
            dyb_ref[...] = _mx(dz)

        r = jnp.maximum(u_ref[...].astype(F32), 0.0)
        da = _dot(dy_sc[...], wd_ref[...], NT)
        dub = _mx(da * (2.0 * r))
        du_ref[...] = dub
        acc[...] += _dot(dub, wu_ref[...], NT)

        @pl.when(j == nj - 1)
        def _():
            dx_ref[...] = acc[...]

    rows = lambda n: pl.BlockSpec((tm, n), lambda i, j: (i, 0))
    full = lambda shp: pl.BlockSpec(shp, lambda i, j: (0,) * len(shp))
    return pl.pallas_call(
        body, name=f"mlp_bwd_dx_{layer}", grid=(T // tm, nj),
        in_specs=[rows(D_MODEL), rows(D_MODEL), rows(1), full(g.shape), pl.BlockSpec((tm, fc), lambda i, j: (i, j)),
                  pl.BlockSpec((None, None, D_MODEL, fc), lambda i, j: (j, layer, 0, 0)),
                  pl.BlockSpec((None, None, fc, D_MODEL), lambda i, j: (j, layer, 0, 0))],
        out_specs=[rows(D_MODEL), pl.BlockSpec((tm, fc), lambda i, j: (i, j)), rows(D_MODEL), full(g.shape), full(g.shape)],
        out_shape=[S((T, D_MODEL), F32), S((T, nj * fc), _MXU_DTYPE), S((T, D_MODEL), _MXU_DTYPE),
                   S(g.shape, F32), S(g.shape, F32)],
        scratch_shapes=[pltpu.VMEM((tm, D_MODEL), F32), pltpu.VMEM((tm, D_MODEL), _MXU_DTYPE)],
        compiler_params=_params("arbitrary", "arbitrary"))(dxo, xhat, rstd, g, u, w_up, w_dn)


def _mlp_bwd_dw(ut, dyb, xt, du, layer, *, nj, other_layers=None, tm=1024):
    T = ut.shape[1]
    tm = min(tm, T)
    fc = ut.shape[0] // nj

    def body(ut_ref, dy_ref, xt_ref, du_ref, *rest):
        gd_ref, gu_ref = rest[-2:]

        @pl.when(pl.program_id(1) == 0)
        def _():
            gd_ref[...] = jnp.zeros_like(gd_ref)
            gu_ref[...] = jnp.zeros_like(gu_ref)

        r = jnp.maximum(ut_ref[...].astype(F32), 0.0)
        gd_ref[...] += _dot(_mx(r * r), dy_ref[...])
        gu_ref[...] += _dot(xt_ref[...], du_ref[...])

    in_specs = [pl.BlockSpec((fc, tm), lambda j, i: (j, i)), pl.BlockSpec((tm, D_MODEL), lambda j, i: (i, 0)),
                pl.BlockSpec((D_MODEL, tm), lambda j, i: (0, i)), pl.BlockSpec((tm, fc), lambda j, i: (i, j))]
    args, aliases = [ut, dyb, xt, du], {}
    if other_layers is not None:
        in_specs += [pl.BlockSpec(memory_space=pl.ANY)] * 2
        args += list(other_layers)
        aliases = {4: 0, 5: 1}
    return pl.pallas_call(
        body, name=f"mlp_bwd_dw_{layer}", grid=(nj, T // tm), in_specs=in_specs,
        out_specs=[pl.BlockSpec((None, None, fc, D_MODEL), lambda j, i: (j, layer, 0, 0)),
                   pl.BlockSpec((None, None, D_MODEL, fc), lambda j, i: (j, layer, 0, 0))],
        out_shape=[S((nj, DEPTH, fc, D_MODEL), F32), S((nj, DEPTH, D_MODEL, fc), F32)],
        input_output_aliases=aliases,
        compiler_params=_params("parallel", "arbitrary"))(*args)


SWA_GROUP = SWA_QH // SWA_KVH
SWA_ROWS = SWA_GROUP * SWA_BLOCK


def _swa_heads(a, kh):
    return jnp.concatenate([a[:, SWA_D * (kh * SWA_GROUP + g):SWA_D * (kh * SWA_GROUP + g + 1)]
                            for g in range(SWA_GROUP)], axis=0)


def _swa_operands(q, kvp, kvc, kh):
    dkv = SWA_KVH * SWA_D
    qg = _swa_heads(q, kh)
    kb = jnp.concatenate([kvp[:, SWA_D * kh:SWA_D * (kh + 1)], kvc[:, SWA_D * kh:SWA_D * (kh + 1)]], axis=0)
    vb = jnp.concatenate([kvp[:, dkv + SWA_D * kh:dkv + SWA_D * (kh + 1)],
                          kvc[:, dkv + SWA_D * kh:dkv + SWA_D * (kh + 1)]], axis=0)
    return qg, kb, vb, _dot(kb, qg, NT)


def _swa_softmax(s_raw, bias_ref, sink_ref, n, kh):
    cols = slice(kh * SWA_ROWS, (kh + 1) * SWA_ROWS)
    s = s_raw * (SWA_D ** -0.5) + bias_ref[jnp.minimum(n, 1), :, cols]
    sink = sink_ref[:, cols]
    m = jnp.maximum(jnp.max(s, axis=0, keepdims=True), sink)
    p, ps = jnp.exp(s - m), jnp.exp(sink - m)
    inv = 1.0 / (jnp.sum(p, axis=0, keepdims=True) + ps)
    return p * inv, ps * inv


def _swa_attn_fwd(qkv, bias, sinks):
    T = qkv.shape[0]
    blk = SWA_BLOCK
    nb = T // blk
    dq, dkv = SWA_QH * SWA_D, SWA_KVH * SWA_D

    def body(q_ref, kvp_ref, kvc_ref, bias_ref, sink_ref, o_ref):
        n = pl.program_id(0)
        q, kvp, kvc = q_ref[...], kvp_ref[...], kvc_ref[...]
        nxt = _swa_operands(q, kvp, kvc, 0)
        for kh in range(SWA_KVH):
            _, _, vb, s_raw = nxt
            if kh + 1 < SWA_KVH:
                nxt = _swa_operands(q, kvp, kvc, kh + 1)
            p, _ = _swa_softmax(s_raw, bias_ref, sink_ref, n, kh)
            og = _mx(_dot(_mx(p), vb, TN))
            for g in range(SWA_GROUP):
                hd = kh * SWA_GROUP + g
                o_ref[:, SWA_D * hd:SWA_D * (hd + 1)] = og[blk * g:blk * (g + 1), :]

    return pl.pallas_call(
        body, name="swa_attn_fwd", grid=(nb,),
        in_specs=[pl.BlockSpec((blk, dq), lambda n: (n, 0)),
                  pl.BlockSpec((blk, 2 * dkv), lambda n: (jnp.maximum(n - 1, 0), dq // (2 * dkv))),
                  pl.BlockSpec((blk, 2 * dkv), lambda n: (n, dq // (2 * dkv))),
                  pl.BlockSpec(bias.shape, lambda n: (0, 0, 0)), pl.BlockSpec(sinks.shape, lambda n: (0, 0))],
        out_specs=pl.BlockSpec((blk, dq), lambda n: (n, 0)), out_shape=S((T, dq), _MXU_DTYPE),
        compiler_params=_params("parallel"))(qkv, qkv, qkv, bias, sinks)


def _swa_attn_bwd(qkv, ob, do, bias, sinks):
    T = qkv.shape[0]
    blk = SWA_BLOCK
    nb = T // blk
    dq, dkv = SWA_QH * SWA_D, SWA_KVH * SWA_D

    def body(q_ref, kvp_ref, kvc_ref, o_ref, do_ref, bias_ref, sink_ref, dqkv_ref, dbias_ref, dsink_ref, carry):
        st = pl.program_id(0)
        n = nb - 1 - st

        @pl.when(st == 0)
        def _():
            carry[...] = jnp.zeros_like(carry)
            dbias_ref[...] = jnp.zeros_like(dbias_ref)
            dsink_ref[...] = jnp.zeros_like(dsink_ref)

        q, kvp, kvc = q_ref[...], kvp_ref[...], kvc_ref[...]
        ov, dov = o_ref[...], do_ref[...]
        ones = jnp.ones((8, SWA_D), F32)
        def operands(kh):
            qg, kb, vb, s_raw = _swa_operands(q, kvp, kvc, kh)
            dog = _swa_heads(dov, kh)
            dl = _dot(ones, dog * _swa_heads(ov, kh).astype(F32), NT, lax.Precision.HIGHEST)[0:1]
            dogb = _mx(dog)
            return qg, kb, s_raw, dl, dogb, _dot(vb, dogb, NT)

        nxt = operands(0)
        for kh in range(SWA_KVH):
            cols = slice(kh * SWA_ROWS, (kh + 1) * SWA_ROWS)
            qg, kb, s_raw, dl, dogb, dp = nxt
            if kh + 1 < SWA_KVH:
                nxt = operands(kh + 1)
            p, ps = _swa_softmax(s_raw, bias_ref, sink_ref, n, kh)
            ds = p * (dp - dl)
            dbias_ref[:, cols] += ds
            dsink_ref[0:1, cols] += -ps * dl
            dsb = _mx(ds * (SWA_D ** -0.5))
            dqg = _mx(_dot(dsb, kb, TN))
            for g in range(SWA_GROUP):
                hd = kh * SWA_GROUP + g
                dqkv_ref[:, SWA_D * hd:SWA_D * (hd + 1)] = dqg[blk * g:blk * (g + 1), :]
            dkb = _dot(dsb, qg)
            dvb = _dot(_mx(p), dogb)
            ko, vo = SWA_D * kh, dkv + SWA_D * kh
            dqkv_ref[:, dq + ko:dq + ko + SWA_D] = _mx(dkb[blk:, :] + carry[:, ko:ko + SWA_D])
            dqkv_ref[:, dq + vo:dq + vo + SWA_D] = _mx(dvb[blk:, :] + carry[:, vo:vo + SWA_D])
            carry[:, ko:ko + SWA_D] = dkb[:blk, :]
            carry[:, vo:vo + SWA_D] = dvb[:blk, :]

    rev = lambda s: nb - 1 - s
    return pl.pallas_call(
        body, name="swa_attn_bwd", grid=(nb,),
        in_specs=[pl.BlockSpec((blk, dq), lambda s: (rev(s), 0)),
                  pl.BlockSpec((blk, 2 * dkv), lambda s: (jnp.maximum(rev(s) - 1, 0), dq // (2 * dkv))),
                  pl.BlockSpec((blk, 2 * dkv), lambda s: (rev(s), dq // (2 * dkv))),
                  pl.BlockSpec((blk, dq), lambda s: (rev(s), 0)), pl.BlockSpec((blk, dq), lambda s: (rev(s), 0)),
                  pl.BlockSpec(bias.shape, lambda s: (0, 0, 0)), pl.BlockSpec(sinks.shape, lambda s: (0, 0))],
        out_specs=[pl.BlockSpec((blk, dq + 2 * dkv), lambda s: (rev(s), 0)),
                   pl.BlockSpec(bias.shape[1:], lambda s: (0, 0)), pl.BlockSpec((8, sinks.shape[1]), lambda s: (0, 0))],
        out_shape=[S((T, dq + 2 * dkv), _MXU_DTYPE), S(bias.shape[1:], F32), S((8, sinks.shape[1]), F32)],
        scratch_shapes=[pltpu.VMEM((blk, 2 * dkv), F32)],
        compiler_params=_params("arbitrary"))(qkv, qkv, qkv, ob, do, bias, sinks)


def _t5_onehot():
    i = jnp.arange(SWA_BLOCK)
    j = jnp.arange(2 * SWA_BLOCK)
    n = jnp.maximum(i[:, None] + SWA_BLOCK - j[None, :], 0)
    max_exact = REL_BUCKETS // 2
    nf = jnp.maximum(n, 1).astype(F32)
    large = max_exact + (jnp.log(nf / max_exact) / math.log(REL_MAX_DIST / max_exact)
                         * (REL_BUCKETS - max_exact)).astype(jnp.int32)
    large = jnp.minimum(large, REL_BUCKETS - 1)
    bucket = jnp.where(n < max_exact, n, large).reshape(-1)
    return (bucket[None, :] == jnp.arange(REL_BUCKETS)[:, None]).astype(F32)


def _loss_head(yh, g, b, target, *, tm=1024):
    T, D = yh.shape
    tm = min(tm, T)

    def body(y_ref, g_ref, b_ref, t_ref, loss_ref, dy_ref):
        @pl.when(pl.program_id(0) == 0)
        def _():
            loss_ref[...] = jnp.zeros_like(loss_ref)

        d = (y_ref[...] * g_ref[...] + b_ref[...]) - t_ref[...]
        dy_ref[...] = d * (1.0 / D)
        rs = jnp.sum(d * d, axis=1, keepdims=True)
        loss_ref[...] += (0.5 / D) * jnp.sum(rs, axis=0, keepdims=True)

    rows = pl.BlockSpec((tm, D), lambda i: (i, 0))
    vec = pl.BlockSpec(g.shape, lambda i: (0, 0))
    return pl.pallas_call(
        body, name="loss_head", grid=(T // tm,), in_specs=[rows, vec, vec, rows],
        out_specs=[pl.BlockSpec((1, 1), lambda i: (0, 0)), rows], out_shape=[S((1, 1), F32), S((T, D), F32)],
        compiler_params=_params("arbitrary"))(yh, g, b, target)


def _exchange_copies(ins, outs, n_gather, send_sems, recv_sems, loc_sems):
    mx, my, mc = lax.axis_index("x"), lax.axis_index("y"), lax.axis_index("c")
    me = 4 * mx + 2 * my + mc
    copies = []
    for a in range(len(ins)):
        src = ins[a] if a < n_gather else ins[a].at[me]
        copies.append(pltpu.make_async_copy(src, outs[a].at[me], loc_sems.at[a]))
    for k in range(1, N_DEV):
        px, py, pc = mx ^ ((k >> 2) & 1), my ^ ((k >> 1) & 1), mc ^ (k & 1)
        peer = 4 * px + 2 * py + pc
        for a in range(len(ins)):
            src = ins[a] if a < n_gather else ins[a].at[peer]
            copies.append(pltpu.make_async_remote_copy(
                src_ref=src, dst_ref=outs[a].at[me], send_sem=send_sems.at[a, k - 1],
                recv_sem=recv_sems.at[a, k - 1], device_id=(px, py, pc), device_id_type=pl.DeviceIdType.MESH))
    return copies


def _exchange_shapes(gather, scatter):
    n_arr = len(gather) + len(scatter)
    out_shape = [S((N_DEV,) + tuple(g.shape), g.dtype) for g in gather] + [S(s.shape, s.dtype) for s in scatter]
    sems = [pltpu.SemaphoreType.DMA((n_arr, N_DEV - 1)), pltpu.SemaphoreType.DMA((n_arr, N_DEV - 1)),
            pltpu.SemaphoreType.DMA((n_arr,))]
    return out_shape, sems


def _exchange(gather, scatter, *, name):
    n_g = len(gather)
    n_arr = n_g + len(scatter)

    def body(*refs):
        copies = _exchange_copies(refs[:n_arr], refs[n_arr:2 * n_arr], n_g, *refs[2 * n_arr:])
        for cp in copies:
            cp.start()
        for cp in copies:
            cp.wait()

    hbm = pl.BlockSpec(memory_space=pl.ANY)
    out_shape, sems = _exchange_shapes(gather, scatter)
    return pl.pallas_call(
        body, name=name, in_specs=[hbm] * n_arr, out_specs=[hbm] * n_arr, out_shape=out_shape,
        scratch_shapes=sems)(*gather, *scatter)


def _adamw(parts, w, m, v, *, name, tr=256):
    R, C = w.shape
    tr = min(tr, R)
    assert R % tr == 0

    def body(p_ref, w_ref, m_ref, v_ref, g_ref, d_ref, nm_ref, nv_ref):
        g = p_ref[0]
        for k in range(1, N_DEV):
            g = g + p_ref[k]
        g_ref[...] = g
        d_ref[...], nm_ref[...], nv_ref[...] = _adamw_math(g, w_ref[...], m_ref[...], v_ref[...])

    rows = pl.BlockSpec((tr, C), lambda i: (i, 0))
    return pl.pallas_call(
        body, name=name, grid=(R // tr,),
        in_specs=[pl.BlockSpec((N_DEV, tr, C), lambda i: (0, i, 0)), rows, rows, rows],
        out_specs=[rows] * 4, out_shape=[S((R, C), F32)] * 4,
        compiler_params=_params("parallel"))(parts, w, m, v)


def _adamw_math(g, w, m, v):
    m_new = ADAM_B1 * m + (1.0 - ADAM_B1) * g
    v_new = ADAM_B2 * v + (1.0 - ADAM_B2) * (g * g)
    m_hat = m_new / (1.0 - ADAM_B1 ** ADAM_STEP)
    v_hat = v_new / (1.0 - ADAM_B2 ** ADAM_STEP)
    return -ADAM_LR * (m_hat / (jnp.sqrt(v_hat) + ADAM_EPS) + ADAM_WD * w), m_new, v_new


SMALL_ROWS = 48
REPL = {"ln_mix_g": (slice(0, 2), slice(None)), "ln_mix_b": (slice(2, 4), slice(None)),
        "ln_mlp_g": (slice(4, 6), slice(None)), "ln_mlp_b": (slice(6, 8), slice(None)),
        "swa_sinks": (slice(8, 9), slice(0, SWA_QH)), "rel_bias": (slice(16, 16 + REL_BUCKETS), slice(0, SWA_QH))}
GAINS = {"mla_g_q": (slice(0, 1), slice(0, MLA_QR // N_DEV)),
         "mla_g_kv": (slice(0, 1), slice(MLA_QR // N_DEV, (MLA_QR + MLA_C) // N_DEV))}


def _adamw_small(r_all, p_gains, W, M, V):
    names = list(REPL) + list(GAINS)

    def body(r_ref, pg_ref, *refs):
        ins, outs = refs[:3 * len(names)], refs[3 * len(names):]
        r_sum, g_sum = r_ref[0], pg_ref[0]
        for k in range(1, N_DEV):
            r_sum, g_sum = r_sum + r_ref[k], g_sum + pg_ref[k]
        for i, n in enumerate(names):
            g = r_sum[REPL[n]] if n in REPL else g_sum[GAINS[n]]
            w_ref, m_ref, v_ref = ins[3 * i:3 * i + 3]
            g_ref, d_ref, nm_ref, nv_ref = outs[4 * i:4 * i + 4]
            g_ref[...] = g
            d_ref[...], nm_ref[...], nv_ref[...] = _adamw_math(g, w_ref[...], m_ref[...], v_ref[...])

    flat_in = [d[n] for n in names for d in (W, M, V)]
    res = pl.pallas_call(body, name="adamw_small", out_shape=[S(W[n].shape, F32) for n in names for _ in range(4)],
                         compiler_params=_params())(r_all, p_gains, *flat_in)
    return {(k, n): res[4 * i + k] for i, n in enumerate(names) for k in range(4)}


WEIGHTS = ["mla_w_in", "mla_g_q", "mla_g_kv", "mla_w_uq", "mla_w_uk", "mla_w_uv", "mla_w_o", "kv_w_shared",
           "swa_w_q", "swa_sinks", "swa_w_o", "rel_bias", "mlp_w_up", "mlp_w_down", "ln_mix_g", "ln_mix_b",
           "ln_mlp_g", "ln_mlp_b"]


def kernel(x, mla_w_in, mla_g_q, mla_g_kv, mla_w_uq, mla_w_uk, mla_w_uv, mla_w_o, kv_w_shared, swa_w_q, swa_sinks, swa_w_o, rel_bias, mlp_w_up, mlp_w_down, ln_mix_g, ln_mix_b, ln_mlp_g, ln_mlp_b, loss_target, m_mla_w_in, m_mla_g_q, m_mla_g_kv, m_mla_w_uq, m_mla_w_uk, m_mla_w_uv, m_mla_w_o, m_kv_w_shared, m_swa_w_q, m_swa_sinks, m_swa_w_o, m_rel_bias, m_mlp_w_up, m_mlp_w_down, m_ln_mix_g, m_ln_mix_b, m_ln_mlp_g, m_ln_mlp_b, v_mla_w_in, v_mla_g_q, v_mla_g_kv, v_mla_w_uq, v_mla_w_uk, v_mla_w_uv, v_mla_w_o, v_kv_w_shared, v_swa_w_q, v_swa_sinks, v_swa_w_o, v_rel_bias, v_mlp_w_up, v_mlp_w_down, v_ln_mix_g, v_ln_mix_b, v_ln_mlp_g, v_ln_mlp_b):
    args = dict(locals())
    W = {n: args[n] for n in WEIGHTS}
    M = {n: args["m_" + n] for n in WEIGHTS}
    V = {n: args["v_" + n] for n in WEIGHTS}
    T = x.shape[1]
    x2d = x.reshape(T, D_MODEL)
    tgt = loss_target.reshape(T, D_MODEL)
    H = MLA_HEADS

    SH = {"mla_w_in": (-1, mla_w_in.shape[-1]), "mla_w_uq": (-1, H * (MLA_NOPE + MLA_ROPE)),
          "mla_w_uk": (-1, H * MLA_NOPE), "mla_w_uv": (-1, H * MLA_V), "mla_w_o": (-1, D_MODEL),
          "kv_w_shared": (-1, kv_w_shared.shape[-1]), "swa_w_q": (-1, swa_w_q.shape[-1]), "swa_w_o": (-1, D_MODEL)}
    slab = lambda d, n: d[n].reshape(SH[n])
    bf = lambda a: a.astype(_MXU_DTYPE)
    gains_slab = lambda d: jnp.pad(jnp.concatenate([d["mla_g_q"], d["mla_g_kv"]], axis=1),
                                   ((0, 7), (0, 128 - d["mla_g_q"].shape[1] - d["mla_g_kv"].shape[1])))
    n_gq, n_gkv = mla_g_q.shape[1], mla_g_kv.shape[1]
    w_in_s, w_uq_s, w_uk_s, gains_all = _exchange(
        [bf(slab(W, "mla_w_in")), bf(slab(W, "mla_w_uq")), bf(slab(W, "mla_w_uk")), gains_slab(W)], [],
        name="gather_mla_in")
    later = [bf(slab(W, n)) for n in ("mla_w_uv", "mla_w_o", "kv_w_shared", "swa_w_q", "swa_w_o")]
    later += [bf(mlp_w_up), bf(mlp_w_down)]
    w_in = w_in_s.reshape(D_MODEL, -1)
    g_q = gains_all[:, 0, :n_gq].reshape(1, MLA_QR)
    g_kv = gains_all[:, 0, n_gq:n_gq + n_gkv].reshape(1, MLA_C)
    w_uq = w_uq_s.reshape(MLA_QR, H, MLA_NOPE + MLA_ROPE)
    w_uq_n = w_uq[:, :, :MLA_NOPE].reshape(MLA_QR, H * MLA_NOPE)
    w_uq_r = w_uq[:, :, MLA_NOPE:].reshape(MLA_QR, H * MLA_ROPE)
    w_uk = w_uk_s.reshape(MLA_C, H, MLA_NOPE).transpose(1, 0, 2)
    w_uk_t = w_uk.transpose(0, 2, 1)
    ln = lambda a, l: a[l].reshape(1, D_MODEL)

    half = MLA_ROPE // 2
    inv = ROPE_THETA ** (-jnp.arange(half, dtype=F32) / half)
    ang = jnp.arange(T, dtype=F32)[:, None] * inv[None, :]
    cos = jnp.tile(jnp.concatenate([jnp.cos(ang), jnp.cos(ang)], -1), (1, ROPE_TABLE_W // MLA_ROPE))
    sin = jnp.tile(jnp.concatenate([-jnp.sin(ang), jnp.sin(ang)], -1), (1, ROPE_TABLE_W // MLA_ROPE))

    h, kc, kct, qs, qst, cq, qn = _mla_pre_fwd(x2d, w_in, g_q, g_kv, w_uq_n, w_uq_r, w_uk_t, cos, sin)
    olat, lse, (w_uv_s, w_o_s, w_kv_s, w_q_s, w_o2_s, w_up, w_dn) = _mla_attn_fwd(qs, kc, kct, gather=later)
    w_uv = w_uv_s.reshape(MLA_C, H, MLA_V).transpose(1, 0, 2)
    w_o = w_o_s.reshape(H * MLA_V, D_MODEL)
    w_qkv = jnp.concatenate([w_q_s.reshape(D_MODEL, -1), w_kv_s.reshape(D_MODEL, -1)], axis=1)
    w_o2 = w_o2_s.reshape(SWA_QH * SWA_D, D_MODEL)
    o_mla = _mla_uv_fwd(olat, w_uv)
    mix0, mlp0 = (ln(ln_mix_g, 0), ln(ln_mix_b, 0)), (ln(ln_mlp_g, 0), ln(ln_mlp_b, 0))
    mix1, mlp1 = (ln(ln_mix_g, 1), ln(ln_mix_b, 1)), (ln(ln_mlp_g, 1), ln(ln_mlp_b, 1))
    x1b, x1t, xh1, rs1 = _proj_ln_fwd(o_mla, w_o, x2d, None, *mix0, name="mla_out_ln_fwd")
    u0, u0t, x2b, xh2, rs2 = _mlp_fwd(x1b, xh1, *mix0, w_up, w_dn, 0, *mlp0)
    onehot = _t5_onehot()
    bias = _mm(rel_bias.T, onehot, name="rel_bias_expand", precision=lax.Precision.HIGHEST, tn=8192).reshape(
        SWA_QH * SWA_BLOCK, 2 * SWA_BLOCK).T
    key = jnp.arange(2 * SWA_BLOCK)[:, None]
    qry = jnp.arange(SWA_QH * SWA_BLOCK)[None, :] % SWA_BLOCK
    in_window = (key > qry) & (key <= qry + SWA_BLOCK)
    bias = jnp.stack([jnp.where(in_window & (key >= SWA_BLOCK), bias, -jnp.inf), jnp.where(in_window, bias, -jnp.inf)])
    sink_rows = jnp.repeat(swa_sinks.reshape(SWA_QH), SWA_BLOCK).reshape(1, SWA_QH * SWA_BLOCK)
    qkv = _mm(x2b, w_qkv, name="swa_qkv_fwd", out_dtype=_MXU_DTYPE, tm=1024, tn=512, tk=1024)
    o_swa = _swa_attn_fwd(qkv, bias, sink_rows)
    x3b, x3t, xh3, rs3 = _proj_ln_fwd(o_swa, w_o2, xh2, mlp0, *mix1, name="swa_out_ln_fwd")
    u1, u1t, _, xh4, rs4 = _mlp_fwd(x3b, xh3, *mix1, w_up, w_dn, 1, *mlp1)
    loss_part, dx4 = _loss_head(xh4, *mlp1, tgt)
    loss = lax.psum(loss_part[0, 0], AXES)

    nj = w_up.shape[0]
    dx3, du1, dy4b, dg_mlp1, db_mlp1 = _mlp_bwd_dx(dx4, xh4, rs4, ln(ln_mlp_g, 1), u1, w_up, w_dn, 1)
    g_dn_last, g_up_last = _mlp_bwd_dw(u1t, dy4b, x3t, du1, 1, nj=nj)
    dres3, do_swa, g_o2, dg_mix1, db_mix1 = _proj_ln_bwd(dx3, xh3, rs3, ln(ln_mix_g, 1), o_swa, w_o2,
                                                         name="swa_out_ln_bwd")
    dqkv, dbias, dsink = _swa_attn_bwd(qkv, o_swa, do_swa, bias, sink_rows)
    g_rel = _mm(onehot, dbias.T.reshape(SWA_QH, -1), name="rel_bias_grad", tb=True, precision=lax.Precision.HIGHEST,
                tk=8192)
    head_of_row = (jnp.arange(SWA_QH * SWA_BLOCK)[:, None] // SWA_BLOCK == jnp.arange(SWA_QH)[None, :]).astype(F32)
    g_sinks = _mm(dsink, head_of_row, name="sinks_grad", precision=lax.Precision.HIGHEST, tk=2048)[0:1]
    dx2 = _mm(dqkv, w_qkv, name="swa_qkv_bwd_dx", tb=True, add=dres3, tm=1024, tn=1024, tk=1536)
    g_qkv = _mm(x2b, dqkv, name="swa_qkv_bwd_dw", ta=True, tm=1024, tn=512, tk=1024)
    dx1, du0, dy2b, dg_mlp0, db_mlp0 = _mlp_bwd_dx(dx2, xh2, rs2, ln(ln_mlp_g, 0), u0, w_up, w_dn, 0)
    g_dn, g_up = _mlp_bwd_dw(u0t, dy2b, x1t, du0, 0, nj=nj, other_layers=(g_dn_last, g_up_last))
    dres1, do_mla, g_o, dg_mix0, db_mix0 = _proj_ln_bwd(dx1, xh1, rs1, ln(ln_mix_g, 0), o_mla, w_o,
                                                        name="mla_out_ln_bwd")
    dol, delta, g_uv = _mla_uv_bwd(do_mla, olat, w_uv)
    wide = lambda a, rows: jnp.pad(a, ((0, rows - a.shape[0]), (0, LANES - a.shape[1])))
    r_part = jnp.concatenate([dg_mix0, dg_mix1, db_mix0, db_mix1, dg_mlp0, dg_mlp1, db_mlp0, db_mlp1,
                              wide(g_sinks, 8), wide(g_rel, SMALL_ROWS - 16)], axis=0)
    by_dev = lambda g: g.reshape((N_DEV, g.shape[0] // N_DEV) + g.shape[1:])
    early = [by_dev(g_o2), by_dev(g_qkv), g_up, g_dn, by_dev(g_o),
             by_dev(g_uv.transpose(1, 0, 2).reshape(MLA_C, H * MLA_V))]
    dqs, dkc, dv, (r_all, p_o2, p_qkv, p_up, p_dn, p_o, p_uv) = _mla_attn_bwd(
        qs, qst, kc, kct, dol, lse, delta, gather=[r_part], scatter=early)
    grad_x, g_in, g_uq_n, g_uq_r, g_uk, g_gq, g_gkv = _mla_pre_bwd(
        dqs, dkc, dv, h, x2d, dres1, cq, qn, cos, sin, w_in, g_q, g_kv, w_uq_n, w_uq_r, w_uk)
    g_uq = jnp.concatenate([g_uq_n.reshape(MLA_QR, H, MLA_NOPE), g_uq_r.reshape(MLA_QR, H, MLA_ROPE)], -1)
    g_gains = jnp.pad(jnp.concatenate([g_gq.reshape(N_DEV, n_gq), g_gkv.reshape(N_DEV, n_gkv)], axis=1)[:, None, :],
                      ((0, 0), (0, 7), (0, 128 - n_gq - n_gkv)))
    p_in, p_uq, p_uk, p_gains = _exchange(
        [], [by_dev(g_in), by_dev(g_uq.reshape(MLA_QR, -1)), by_dev(g_uk.transpose(1, 0, 2).reshape(MLA_C, -1)),
             g_gains], name="exchange_mla_in_grads")

    res = {}

    def adam(name, parts, names, to_slab, from_slab):
        out = _adamw(parts, to_slab(W), to_slab(M), to_slab(V), name="adamw_" + name)
        for k in range(4):
            for n, a in zip(names, from_slab(out[k])):
                res[(k, n)] = a.reshape(W[n].shape)

    one = lambda n: (lambda d: slab(d, n))
    adam("swa_w_o", p_o2, ["swa_w_o"], one("swa_w_o"), lambda s: [s])
    dq_cols = SWA_QH * SWA_D
    adam("swa_qkv", p_qkv, ["swa_w_q", "kv_w_shared"],
         lambda d: jnp.concatenate([slab(d, "swa_w_q"), slab(d, "kv_w_shared")], axis=1),
         lambda s: [s[:, :dq_cols], s[:, dq_cols:]])
    layers_as_rows = lambda a: a.reshape((-1,) + a.shape[-1:])
    adam("mlp_w_up", p_up.reshape(N_DEV, -1, p_up.shape[-1]), ["mlp_w_up"],
         lambda d: layers_as_rows(d["mlp_w_up"]), lambda s: [s])
    adam("mlp_w_down", p_dn.reshape(N_DEV, -1, p_dn.shape[-1]), ["mlp_w_down"],
         lambda d: layers_as_rows(d["mlp_w_down"]), lambda s: [s])
    adam("mla_w_o", p_o, ["mla_w_o"], one("mla_w_o"), lambda s: [s])
    adam("mla_w_uv", p_uv, ["mla_w_uv"], one("mla_w_uv"), lambda s: [s])
    adam("mla_w_in", p_in, ["mla_w_in"], one("mla_w_in"), lambda s: [s])
    adam("mla_w_uq", p_uq, ["mla_w_uq"], one("mla_w_uq"), lambda s: [s])
    adam("mla_w_uk", p_uk, ["mla_w_uk"], one("mla_w_uk"), lambda s: [s])
    res.update(_adamw_small(r_all, p_gains, W, M, V))
    return (loss, grad_x.reshape(x.shape), *[res[(k, n)] for k in range(4) for n in WEIGHTS])
```

```python
import math

import numpy as np
import jax
import jax.numpy as jnp
from jax import lax
from jax.experimental import pallas as pl
from jax.experimental.pallas import tpu as pltpu

F32 = jnp.float32
_MXU_DTYPE = jnp.bfloat16

D_MODEL = 1024
DEPTH = 2
MLA_HEADS = 8
MLA_NOPE = 128
MLA_ROPE = 64
MLA_V = 128
MLA_QR = 384
MLA_C = 256
MLA_DK = 384
MLA_DT = MLA_C + MLA_ROPE
ROPE_THETA = 10000.0
SWA_QH = 16
SWA_KVH = 4
SWA_D = 64
SWA_BLOCK = 128
REL_BUCKETS = 32
REL_MAX_DIST = 128
D_FF = 4096
LN_EPS = 1e-5
RMS_EPS = 1e-6
ALPHA = (2 * DEPTH) ** 0.25
ADAM_LR, ADAM_B1, ADAM_B2, ADAM_EPS, ADAM_WD, ADAM_STEP = 0.001, 0.9, 0.999, 1e-08, 0.01, 10

N_DEV = 8
AXES = ("x", "y", "c")
V7X_VMEM_BYTES = 64 * 1024 * 1024
VMEM_LIMIT = V7X_VMEM_BYTES - 8 * 1024 * 1024
LANES = 1024
ATT_TQ = 512
ATT_TK = 512
ATT_HEAD_GROUP = 1
ATT_FWD_HEAD_GROUP = 2

NT = (((1,), (1,)), ((), ()))
TN = (((0,), (0,)), ((), ()))
S = jax.ShapeDtypeStruct


def _params(*sem, vmem=VMEM_LIMIT):
    return pltpu.CompilerParams(dimension_semantics=sem, vmem_limit_bytes=vmem)


def _dot(a, b, dims=None, precision=None):
    if dims is None:
        return jnp.dot(a, b, preferred_element_type=F32, precision=precision)
    return lax.dot_general(a, b, dims, preferred_element_type=F32, precision=precision)


def _mx(v):
    return v.astype(_MXU_DTYPE)


ROPE_TABLE_W = 128


def _tile_heads(t):
    return jnp.concatenate([t] * (MLA_HEADS * MLA_ROPE // ROPE_TABLE_W), axis=1)


def _swap_halves_64(v):
    return jnp.concatenate([v[:, 32:], v[:, :32]], axis=-1)


def _swap_halves_groups(v):
    n = v.shape[-1]
    lane = lax.broadcasted_iota(jnp.int32, v.shape, 1)
    return jnp.where(lane % 64 < 32, pltpu.roll(v, n - 32, 1), pltpu.roll(v, 32, 1))


def _mm(a, b, *, name, ta=False, tb=False, add=None, out_dtype=F32, tm=512, tn=512, tk=512, precision=None):
    M, K = (a.shape[1], a.shape[0]) if ta else a.shape
    N = b.shape[0] if tb else b.shape[1]
    tm, tn, tk = min(tm, M), min(tn, N), min(tk, K)
    assert M % tm == 0 and N % tn == 0 and K % tk == 0, (M, N, K, tm, tn, tk)
    nk = K // tk
    dims = (((0 if ta else 1,), (1 if tb else 0,)), ((), ()))
    has_add = add is not None

    def body(*refs):
        if has_add:
            a_ref, b_ref, add_ref, o_ref, acc = refs
        else:
            a_ref, b_ref, o_ref, acc = refs
        k = pl.program_id(2)
        av, bv = a_ref[...], b_ref[...]
        if precision is None:
            av, bv = _mx(av), _mx(bv)
        part = _dot(av, bv, dims, precision)
        if nk == 1:
            o_ref[...] = (part + add_ref[...] if has_add else part).astype(out_dtype)
            return

        @pl.when(k == 0)
        def _():
            acc[...] = add_ref[...] if has_add else jnp.zeros_like(acc)

        acc[...] += part

        @pl.when(k == nk - 1)
        def _():
            o_ref[...] = acc[...].astype(out_dtype)

    a_spec = pl.BlockSpec((tk, tm), lambda i, j, k: (k, i)) if ta else pl.BlockSpec((tm, tk), lambda i, j, k: (i, k))
    b_spec = pl.BlockSpec((tn, tk), lambda i, j, k: (j, k)) if tb else pl.BlockSpec((tk, tn), lambda i, j, k: (k, j))
    in_specs = [a_spec, b_spec]
    args = [a, b]
    if has_add:
        in_specs.append(pl.BlockSpec((tm, tn), lambda i, j, k: (i, j)))
        args.append(add)
    return pl.pallas_call(
        body, name=name, grid=(M // tm, N // tn, nk), in_specs=in_specs,
        out_specs=pl.BlockSpec((tm, tn), lambda i, j, k: (i, j)), out_shape=S((M, N), out_dtype),
        scratch_shapes=[pltpu.VMEM((tm, tn), F32)],
        compiler_params=_params("parallel", "parallel", "arbitrary"))(*args)


def _ln_fwd_math(z, g, b):
    mu = jnp.mean(z, axis=-1, keepdims=True)
    zc = z - mu
    var = jnp.mean(zc * zc, axis=-1, keepdims=True)
    rstd = lax.rsqrt(var + LN_EPS)
    xhat = zc * rstd
    return xhat * g + b, xhat, rstd


def _ln_bwd_math(dxo, xhat, rstd, g):
    dxh = dxo * g
    m1 = jnp.mean(dxh, axis=-1, keepdims=True)
    m2 = jnp.mean(dxh * xhat, axis=-1, keepdims=True)
    dz = rstd * (dxh - m1 - xhat * m2)
    dg = jnp.sum(dxo * xhat, axis=0, keepdims=True)
    db = jnp.sum(dxo, axis=0, keepdims=True)
    return dz, dg, db


def _rms_fwd_math(xr, g):
    r = lax.rsqrt(jnp.mean(xr * xr, axis=-1, keepdims=True) + RMS_EPS)
    return xr * r * g


def _rms_bwd_math(dy, xr, g):
    r = lax.rsqrt(jnp.mean(xr * xr, axis=-1, keepdims=True) + RMS_EPS)
    gy = dy * g
    dx = r * gy - xr * (r * r * r) * jnp.mean(gy * xr, axis=-1, keepdims=True)
    dg = jnp.sum(dy * xr * r, axis=0, keepdims=True)
    return dx, dg


def _mla_pre_fwd(x, w_in, g_q, g_kv, w_uq_n, w_uq_r, w_uk_t, cos, sin):
    T = x.shape[0]
    tm = min(ATT_TQ, T)
    nq = T // tm
    H = MLA_HEADS

    tk = min(ATT_TK, T)

    def body(x_ref, win_ref, gq_ref, gkv_ref, wn_ref, wr_ref, wuk_ref, cos_ref, sin_ref,
             h_ref, kc_ref, kct_ref, qs_ref, qst_ref, cq_ref, qn_ref):
        h = _dot(_mx(x_ref[...]), win_ref[...])
        h_ref[...] = h
        cos_v, sin_v = _tile_heads(cos_ref[...]), _tile_heads(sin_ref[...])
        cq = _mx(_rms_fwd_math(h[:, :MLA_QR], gq_ref[...]))
        ckv = _rms_fwd_math(h[:, MLA_QR:MLA_QR + MLA_C], gkv_ref[...])
        krr = h[:, MLA_QR + MLA_C:]
        kr = krr * cos_v[:, :MLA_ROPE] + _swap_halves_64(krr) * sin_v[:, :MLA_ROPE]
        kr_pad = jnp.concatenate([kr, jnp.zeros((tm, MLA_DK - MLA_C - MLA_ROPE), F32)], axis=1)
        kc_ref[:, 0:MLA_C] = _mx(ckv)
        kc_ref[:, MLA_C:] = _mx(kr_pad)
        kct_ref[0:MLA_C, :] = _mx(ckv.T)
        kct_ref[MLA_C:, :] = _mx(kr_pad.T[0:MLA_ROPE, :])
        cq_ref[...] = cq
        qnb = _mx(_dot(cq, wn_ref[...]))
        qn_ref[...] = qnb
        qr = _dot(cq, wr_ref[...])
        qrr = qr * cos_v + _swap_halves_groups(qr) * sin_v
        qrr_t = qrr.T
        for hd in range(H):
            ql = _dot(qnb[:, MLA_NOPE * hd:MLA_NOPE * (hd + 1)], wuk_ref[hd])
            qst_ref[0, 0:MLA_C, tm * hd:tm * (hd + 1)] = _mx(ql.T)
            qst_ref[0, MLA_C:, tm * hd:tm * (hd + 1)] = _mx(qrr_t[MLA_ROPE * hd:MLA_ROPE * (hd + 1), :])
            qs_ref[0, hd, :, 0:MLA_C] = _mx(ql)
            qs_ref[0, hd, :, MLA_C:MLA_C + MLA_ROPE] = _mx(qrr[:, MLA_ROPE * hd:MLA_ROPE * (hd + 1)])
            qs_ref[0, hd, :, MLA_C + MLA_ROPE:] = jnp.zeros((tm, MLA_DK - MLA_C - MLA_ROPE), _MXU_DTYPE)

    full = lambda shp: pl.BlockSpec(shp, lambda i: (0,) * len(shp))
    rows = lambda n: pl.BlockSpec((tm, n), lambda i: (i, 0))
    n_in = w_in.shape[1]
    return pl.pallas_call(
        body, name="mla_pre_fwd", grid=(nq,),
        in_specs=[rows(D_MODEL), full(w_in.shape), full(g_q.shape), full(g_kv.shape), full(w_uq_n.shape),
                  full(w_uq_r.shape), full(w_uk_t.shape), rows(ROPE_TABLE_W), rows(ROPE_TABLE_W)],
        out_specs=[rows(n_in), rows(MLA_DK),
                   pl.BlockSpec((None, MLA_DT, tm), lambda i: (i * tm // tk, 0, i % (tk // tm))),
                   pl.BlockSpec((1, H, tm, MLA_DK), lambda i: (i, 0, 0, 0)),
                   pl.BlockSpec((1, MLA_DT, H * tm), lambda i: (i, 0, 0)), rows(MLA_QR), rows(H * MLA_NOPE)],
        out_shape=[S((T, n_in), F32), S((T, MLA_DK), _MXU_DTYPE), S((T // tk, MLA_DT, tk), _MXU_DTYPE),
                   S((nq, H, tm, MLA_DK), _MXU_DTYPE), S((nq, MLA_DT, H * tm), _MXU_DTYPE),
                   S((T, MLA_QR), _MXU_DTYPE), S((T, H * MLA_NOPE), _MXU_DTYPE)],
        compiler_params=_params("parallel"))(x, w_in, g_q, g_kv, w_uq_n, w_uq_r, w_uk_t, cos, sin)


def _att_steps(T, tq, tk):
    qi, kj = [], []
    for i in range(T // tq):
        for j in range((i * tq + tq - 1) // tk + 1):
            qi.append(i)
            kj.append(j)
    return jnp.asarray(np.array(qi, np.int32)), jnp.asarray(np.array(kj, np.int32))


def _ride_exchange(st, n_steps, ins, outs, n_gather, sems):
    if not ins:
        return

    @pl.when(st == 0)
    def _():
        for cp in _exchange_copies(ins, outs, n_gather, *sems):
            cp.start()

    @pl.when(st == n_steps - 1)
    def _():
        for cp in _exchange_copies(ins, outs, n_gather, *sems):
            cp.wait()


def _mla_attn_fwd(qs, kc, kct, gather=(), scatter=()):
    nq, H, tq, DK = qs.shape
    T = kc.shape[0]
    tk = min(ATT_TK, T)
    scale = (MLA_NOPE + MLA_ROPE) ** -0.5
    c2 = scale * math.log2(math.e)
    qi, kj = _att_steps(T, tq, tk)
    n_steps = int(qi.shape[0])
    hg = ATT_FWD_HEAD_GROUP
    R = hg * tq
    n_x = len(gather) + len(scatter)

    def body(qi_ref, kj_ref, q_ref, k_ref, kt_ref, *rest):
        x_ins, (o_ref, lse_ref), x_outs = rest[:n_x], rest[n_x:n_x + 2], rest[n_x + 2:2 * n_x + 2]
        m_sc, l_sc, acc_sc = rest[2 * n_x + 2:2 * n_x + 5]
        st = pl.program_id(0)
        _ride_exchange(st, n_steps, x_ins, x_outs, len(gather), rest[2 * n_x + 5:])
        i, j = qi_ref[st], kj_ref[st]
        j_last = (i * tq + tq - 1) // tk

        @pl.when(j == 0)
        def _():
            m_sc[...] = jnp.full_like(m_sc, -jnp.inf)
            l_sc[...] = jnp.zeros_like(l_sc)
            acc_sc[...] = jnp.zeros_like(acc_sc)

        def step(masked):
            k = k_ref[...]
            vt = kt_ref[0:MLA_C, :]
            if masked:
                key = lax.broadcasted_iota(jnp.int32, (tk, R), 0) + j * tk
                qry = lax.broadcasted_iota(jnp.int32, (tk, R), 1) % tq + i * tq
                causal = key <= qry
            n_g = H // hg
            qk = lambda g: _dot(k, q_ref[0, g * hg:(g + 1) * hg].reshape(R, DK), NT)
            def accumulate(g, a, pb):
                cs = slice(g * R, (g + 1) * R)
                acc_sc[:, cs] = a * acc_sc[:, cs] + _dot(vt, pb)

            s_next = qk(0)
            pending = None
            for g in range(n_g):
                cs = slice(g * R, (g + 1) * R)
                s = s_next
                if g + 1 < n_g:
                    s_next = qk(g + 1)
                if pending is not None:
                    accumulate(*pending)
                if masked:
                    s = jnp.where(causal, s, -jnp.inf)
                m_prev = m_sc[:, cs]
                m_new = jnp.maximum(m_prev, jnp.max(s, axis=0, keepdims=True))
                a = jnp.exp2((m_prev - m_new) * c2)
                p = jnp.exp2((s - m_new) * c2)
                l_sc[:, cs] = a * l_sc[:, cs] + jnp.sum(p, axis=0, keepdims=True)
                m_sc[:, cs] = m_new
                pending = (g, a, _mx(p))
            accumulate(*pending)

        pl.when(j == j_last)(lambda: step(True))
        pl.when(j != j_last)(lambda: step(False))

        @pl.when(j == j_last)
        def _():
            o_ref[0] = _mx(acc_sc[...] / l_sc[...])
            lse_ref[0] = m_sc[...] * scale + jnp.log(l_sc[...])

    hbm = pl.BlockSpec(memory_space=pl.ANY)
    x_shapes, x_sems = _exchange_shapes(gather, scatter) if n_x else ([], [])
    gs = pltpu.PrefetchScalarGridSpec(
        num_scalar_prefetch=2, grid=(n_steps,),
        in_specs=[pl.BlockSpec((1, H, tq, DK), lambda s, qi, kj: (qi[s], 0, 0, 0)),
                  pl.BlockSpec((tk, DK), lambda s, qi, kj: (kj[s], 0)),
                  pl.BlockSpec((None, MLA_DT, tk), lambda s, qi, kj: (kj[s], 0, 0))] + [hbm] * n_x,
        out_specs=[pl.BlockSpec((1, MLA_C, H * tq), lambda s, qi, kj: (qi[s], 0, 0)),
                   pl.BlockSpec((1, 1, H * tq), lambda s, qi, kj: (qi[s], 0, 0))] + [hbm] * n_x,
        scratch_shapes=[pltpu.VMEM((1, H * tq), F32), pltpu.VMEM((1, H * tq), F32),
                        pltpu.VMEM((MLA_C, H * tq), F32)] + x_sems)
    res = pl.pallas_call(
        body, name="mla_attn_fwd", grid_spec=gs,
        out_shape=[S((nq, MLA_C, H * tq), _MXU_DTYPE), S((nq, 1, H * tq), F32)] + x_shapes,
        compiler_params=_params("arbitrary"))(qi, kj, qs, kc, kct, *gather, *scatter)
    return res[0], res[1], res[2:]


def _mla_attn_bwd(qs, qst, kc, kct, dol, lse, delta, gather=(), scatter=()):
    nq, H, tq, DK = qs.shape
    T = kc.shape[0]
    tk = min(ATT_TK, T)
    scale = (MLA_NOPE + MLA_ROPE) ** -0.5
    log2e = math.log2(math.e)
    qi, kj = _att_steps(T, tq, tk)
    n_steps = int(qi.shape[0])
    hg = ATT_HEAD_GROUP
    R = hg * tq
    n_x = len(gather) + len(scatter)

    def body(qi_ref, kj_ref, q_ref, qt_ref, k_ref, kt_ref, do_ref, lse_ref, dl_ref, *rest):
        x_ins, (dq_ref, dk_ref, dv_ref), x_outs = rest[:n_x], rest[n_x:n_x + 3], rest[n_x + 3:2 * n_x + 3]
        dk_acc, dv_acc, sem = rest[2 * n_x + 3:2 * n_x + 6]
        st = pl.program_id(0)
        _ride_exchange(st, n_steps, x_ins, x_outs, len(gather), rest[2 * n_x + 6:])
        i, j = qi_ref[st], kj_ref[st]
        j_last = (i * tq + tq - 1) // tk

        @pl.when(st == 0)
        def _():
            dk_acc[...] = jnp.zeros_like(dk_acc)
            dv_acc[...] = jnp.zeros_like(dv_acc)

        @pl.when(j == 0)
        def _():
            dq_ref[...] = jnp.zeros_like(dq_ref)

        def step(masked):
            k, kt = k_ref[...], kt_ref[...]
            v = k[:, :MLA_C]
            if masked:
                key = lax.broadcasted_iota(jnp.int32, (tk, R), 0) + j * tk
                qry = lax.broadcasted_iota(jnp.int32, (tk, R), 1) % tq + i * tq
                causal = key <= qry
            dkt_c = jnp.zeros((MLA_DT, tk), F32)
            dvt_c = jnp.zeros((MLA_C, tk), F32)
            n_g = H // hg

            def scores(g):
                q = q_ref[0, g * hg:(g + 1) * hg].reshape(R, DK)
                dot = do_ref[0, :, g * R:(g + 1) * R]
                return dot, _dot(k, q, NT), _dot(v, dot)

            nxt = scores(0)
            for g in range(n_g):
                cs = slice(g * R, (g + 1) * R)
                dot, s, dp = nxt
                if g + 1 < n_g:
                    nxt = scores(g + 1)
                p = jnp.exp2(s * (scale * log2e) - lse_ref[0, :, cs] * log2e)
                if masked:
                    p = jnp.where(causal, p, 0.0)
                dsb = _mx(p * (dp - dl_ref[0, :, cs]))
                dq_ref[0, :, cs] += _dot(kt, dsb)
                dkt_c = dkt_c + _dot(qt_ref[0, :, cs], dsb, NT)
                dvt_c = dvt_c + _dot(dot, _mx(p), NT)
            dk_acc[j] += dkt_c * scale
            dv_acc[j] += dvt_c

        pl.when(j == j_last)(lambda: step(True))
        pl.when(j != j_last)(lambda: step(False))

        @pl.when(j == j_last)
        def _():
            dq_ref[...] = dq_ref[...] * scale

        @pl.when(st == n_steps - 1)
        def _():
            c1 = pltpu.make_async_copy(dk_acc, dk_ref, sem.at[0])
            c2 = pltpu.make_async_copy(dv_acc, dv_ref, sem.at[1])
            c1.start()
            c2.start()
            c1.wait()
            c2.wait()

    cols = lambda n: pl.BlockSpec((1, n, H * tq), lambda s, qi, kj: (qi[s], 0, 0))
    hbm = pl.BlockSpec(memory_space=pl.ANY)
    x_shapes, x_sems = _exchange_shapes(gather, scatter) if n_x else ([], [])
    gs = pltpu.PrefetchScalarGridSpec(
        num_scalar_prefetch=2, grid=(n_steps,),
        in_specs=[pl.BlockSpec((1, H, tq, DK), lambda s, qi, kj: (qi[s], 0, 0, 0)), cols(MLA_DT),
                  pl.BlockSpec((tk, DK), lambda s, qi, kj: (kj[s], 0)),
                  pl.BlockSpec((None, MLA_DT, tk), lambda s, qi, kj: (kj[s], 0, 0)),
                  cols(MLA_C), cols(1), cols(1)] + [hbm] * n_x,
        out_specs=[cols(MLA_DT), hbm, hbm] + [hbm] * n_x,
        scratch_shapes=[pltpu.VMEM((T // tk, MLA_DT, tk), F32), pltpu.VMEM((T // tk, MLA_C, tk), F32),
                        pltpu.SemaphoreType.DMA((2,))] + x_sems)
    res = pl.pallas_call(
        body, name="mla_attn_bwd", grid_spec=gs,
        out_shape=[S((nq, MLA_DT, H * tq), F32), S((T // tk, MLA_DT, tk), F32),
                   S((T // tk, MLA_C, tk), F32)] + x_shapes,
        compiler_params=_params("arbitrary"))(qi, kj, qs, qst, kc, kct, dol, lse, delta, *gather, *scatter)
    return res[0], res[1], res[2], res[3:]


def _mla_uv_fwd(olat, w_uv):
    nq, C, cols = olat.shape
    H = w_uv.shape[0]
    tq = cols // H

    def body(ol_ref, wuv_ref, o_ref):
        for hd in range(H):
            o_ref[:, MLA_V * hd:MLA_V * (hd + 1)] = _mx(_dot(ol_ref[0, :, tq * hd:tq * (hd + 1)], wuv_ref[hd], TN))

    return pl.pallas_call(
        body, name="mla_uv_fwd", grid=(nq,),
        in_specs=[pl.BlockSpec((1, C, cols), lambda i: (i, 0, 0)), pl.BlockSpec(w_uv.shape, lambda i: (0, 0, 0))],
        out_specs=pl.BlockSpec((tq, H * MLA_V), lambda i: (i, 0)), out_shape=S((nq * tq, H * MLA_V), _MXU_DTYPE),
        compiler_params=_params("parallel"))(olat, w_uv)


def _mla_uv_bwd(do, olat, w_uv):
    nq, C, cols = olat.shape
    H = w_uv.shape[0]
    tq = cols // H

    def body(do_ref, ol_ref, wuv_ref, dol_ref, dl_ref, dw_ref):
        @pl.when(pl.program_id(0) == 0)
        def _():
            dw_ref[...] = jnp.zeros_like(dw_ref)

        dov = do_ref[...]
        for hd in range(H):
            cs = slice(tq * hd, tq * (hd + 1))
            doh = _mx(dov[:, MLA_V * hd:MLA_V * (hd + 1)])
            ol = ol_ref[0, :, cs]
            dol = _dot(wuv_ref[hd], doh, NT)
            dol_ref[0, :, cs] = _mx(dol)
            dl_ref[0, :, cs] = jnp.sum(dol * ol.astype(F32), axis=0, keepdims=True)
            dw_ref[hd] += _dot(ol, doh)

    blk = lambda n: pl.BlockSpec((1, n, cols), lambda i: (i, 0, 0))
    return pl.pallas_call(
        body, name="mla_uv_bwd", grid=(nq,),
        in_specs=[pl.BlockSpec((tq, H * MLA_V), lambda i: (i, 0)), blk(C), pl.BlockSpec(w_uv.shape, lambda i: (0, 0, 0))],
        out_specs=[blk(C), blk(1), pl.BlockSpec(w_uv.shape, lambda i: (0, 0, 0))],
        out_shape=[S(olat.shape, _MXU_DTYPE), S((nq, 1, cols), F32), S(w_uv.shape, F32)],
        compiler_params=_params("arbitrary"))(do, olat, w_uv)


def _mla_pre_bwd(dqs, dkc, dv, h, x, dres, cq, qn, cos, sin, w_in, g_q, g_kv, w_uq_n, w_uq_r, w_uk):
    nq, DK, cols = dqs.shape
    H = w_uk.shape[0]
    tm = cols // H
    T = nq * tm
    tk = dv.shape[2]
    n_in = w_in.shape[1]

    def body(dqs_ref, dkc_ref, dv_ref, h_ref, x_ref, dres_ref, cq_ref, qn_ref, cos_ref, sin_ref,
             win_ref, gq_ref, gkv_ref, wn_ref, wr_ref, wuk_ref,
             gx_ref, dwin_ref, dwn_ref, dwr_ref, dwuk_ref, dgq_ref, dgkv_ref, dqn_sc, dqr_sc, dh_sc):
        @pl.when(pl.program_id(0) == 0)
        def _():
            for r in (dwin_ref, dwn_ref, dwr_ref, dwuk_ref, dgq_ref, dgkv_ref):
                r[...] = jnp.zeros_like(r)

        cos_v, sin_v = _tile_heads(cos_ref[...]), _tile_heads(sin_ref[...])
        qnb = qn_ref[...]
        for hd in range(H):
            cs = slice(tm * hd, tm * (hd + 1))
            dql = _mx(dqs_ref[0, 0:MLA_C, cs])
            dqn_sc[:, MLA_NOPE * hd:MLA_NOPE * (hd + 1)] = _dot(dql, wuk_ref[hd], TN)
            dwuk_ref[hd] += _dot(dql, qnb[:, MLA_NOPE * hd:MLA_NOPE * (hd + 1)])
            dqr_sc[MLA_ROPE * hd:MLA_ROPE * (hd + 1), :] = dqs_ref[0, MLA_C:MLA_C + MLA_ROPE, cs]
        dqr = dqr_sc[...].T
        dqrb = _mx(dqr * cos_v + _swap_halves_groups(dqr * sin_v))
        dqnb = _mx(dqn_sc[...])
        cq = cq_ref[...]
        dwn_ref[...] += _dot(cq, dqnb, TN)
        dwr_ref[...] += _dot(cq, dqrb, TN)
        dcq = _dot(dqnb, wn_ref[...], NT) + _dot(dqrb, wr_ref[...], NT)
        hv = h_ref[...]
        dxq, dgq = _rms_bwd_math(dcq, hv[:, :MLA_QR], gq_ref[...])
        dgq_ref[...] += dgq
        dckv = (dkc_ref[0:MLA_C, :] + dv_ref[...]).T
        dxkv, dgkv = _rms_bwd_math(dckv, hv[:, MLA_QR:MLA_QR + MLA_C], gkv_ref[...])
        dgkv_ref[...] += dgkv
        dkr = jnp.concatenate([dkc_ref[MLA_C:, :], jnp.zeros((128 - MLA_ROPE, tm), F32)], axis=0).T[:, :MLA_ROPE]
        dkr_raw = dkr * cos_v[:, :MLA_ROPE] + _swap_halves_64(dkr * sin_v[:, :MLA_ROPE])
        dh_sc[:, 0:MLA_QR] = dxq
        dh_sc[:, MLA_QR:MLA_QR + MLA_C] = dxkv
        dh_sc[:, MLA_QR + MLA_C:] = dkr_raw
        dhb = _mx(dh_sc[...])
        gx_ref[...] = dres_ref[...] + _dot(dhb, win_ref[...], NT)
        dwin_ref[...] += _dot(_mx(x_ref[...]), dhb, TN)

    full = lambda shp: pl.BlockSpec(shp, lambda i: (0,) * len(shp))
    rows = lambda n: pl.BlockSpec((tm, n), lambda i: (i, 0))
    return pl.pallas_call(
        body, name="mla_pre_bwd", grid=(nq,),
        in_specs=[pl.BlockSpec((1, DK, cols), lambda i: (i, 0, 0)),
                  pl.BlockSpec((None, DK, tm), lambda i: (i * tm // tk, 0, i % (tk // tm))),
                  pl.BlockSpec((None, MLA_C, tm), lambda i: (i * tm // tk, 0, i % (tk // tm))), rows(n_in),
                  rows(D_MODEL), rows(D_MODEL), rows(MLA_QR), rows(H * MLA_NOPE), rows(ROPE_TABLE_W), rows(ROPE_TABLE_W),
                  full(w_in.shape), full(g_q.shape), full(g_kv.shape), full(w_uq_n.shape), full(w_uq_r.shape),
                  full(w_uk.shape)],
        out_specs=[rows(D_MODEL), full(w_in.shape), full(w_uq_n.shape), full(w_uq_r.shape), full(w_uk.shape),
                   full(g_q.shape), full(g_kv.shape)],
        out_shape=[S((T, D_MODEL), F32), S(w_in.shape, F32), S(w_uq_n.shape, F32), S(w_uq_r.shape, F32),
                   S(w_uk.shape, F32), S(g_q.shape, F32), S(g_kv.shape, F32)],
        scratch_shapes=[pltpu.VMEM((tm, H * MLA_NOPE), F32), pltpu.VMEM((H * MLA_ROPE, tm), F32),
                        pltpu.VMEM((tm, n_in), F32)],
        compiler_params=_params("arbitrary"))(dqs, dkc, dv, h, x, dres, cq, qn, cos, sin, w_in, g_q, g_kv,
                                              w_uq_n, w_uq_r, w_uk)


def _proj_ln_fwd(a, w, xres, res_gb, g, b, *, name, tm=512):
    T, K = a.shape
    tm = min(tm, T)
    gp, bp = res_gb if res_gb is not None else (None, None)

    def body(a_ref, w_ref, x_ref, *rest):
        if res_gb is not None:
            x = x_ref[...] * rest[0][...] + rest[1][...]
            rest = rest[2:]
        else:
            x = x_ref[...]
        g_ref, b_ref, xob_ref, xt_ref, xh_ref, rs_ref = rest
        z = ALPHA * x + _dot(a_ref[...], w_ref[...])
        xo, xhat, rstd = _ln_fwd_math(z, g_ref[...], b_ref[...])
        xob_ref[...] = _mx(xo)
        xt_ref[...] = _mx(xo.T)
        xh_ref[...] = xhat
        rs_ref[...] = rstd

    rows = lambda n: pl.BlockSpec((tm, n), lambda i: (i, 0))
    full = lambda shp: pl.BlockSpec(shp, lambda i: (0,) * len(shp))
    extra = [gp, bp] if res_gb is not None else []
    return pl.pallas_call(
        body, name=name, grid=(T // tm,),
        in_specs=[rows(K), full(w.shape), rows(D_MODEL)] + [full(e.shape) for e in extra] + [full(g.shape), full(b.shape)],
        out_specs=[rows(D_MODEL), pl.BlockSpec((D_MODEL, tm), lambda i: (0, i)), rows(D_MODEL), rows(1)],
        out_shape=[S((T, D_MODEL), _MXU_DTYPE), S((D_MODEL, T), _MXU_DTYPE), S((T, D_MODEL), F32), S((T, 1), F32)],
        compiler_params=_params("parallel"))(a, w, xres, *extra, g, b)


def _proj_ln_bwd(dxo, xhat, rstd, g, a, w, *, name, tm=512):
    T, K = a.shape
    tm = min(tm, T)

    def body(dxo_ref, xh_ref, rs_ref, g_ref, a_ref, w_ref, dres_ref, da_ref, dw_ref, dg_ref, db_ref):
        @pl.when(pl.program_id(0) == 0)
        def _():
            for r in (dw_ref, dg_ref, db_ref):
                r[...] = jnp.zeros_like(r)

        dz, dg, db = _ln_bwd_math(dxo_ref[...], xh_ref[...], rs_ref[...], g_ref[...])
        dg_ref[...] += dg
        db_ref[...] += db
        dres_ref[...] = ALPHA * dz
        dzb = _mx(dz)
        da_ref[...] = _dot(dzb, w_ref[...], NT)
        dw_ref[...] += _dot(a_ref[...], dzb, TN)

    rows = lambda n: pl.BlockSpec((tm, n), lambda i: (i, 0))
    full = lambda shp: pl.BlockSpec(shp, lambda i: (0,) * len(shp))
    return pl.pallas_call(
        body, name=name, grid=(T // tm,),
        in_specs=[rows(D_MODEL), rows(D_MODEL), rows(1), full(g.shape), rows(K), full(w.shape)],
        out_specs=[rows(D_MODEL), rows(K), full(w.shape), full(g.shape), full(g.shape)],
        out_shape=[S((T, D_MODEL), F32), S((T, K), F32), S(w.shape, F32), S(g.shape, F32), S(g.shape, F32)],
        compiler_params=_params("arbitrary"))(dxo, xhat, rstd, g, a, w)


def _mlp_fwd(xb, xh_in, g_in, b_in, w_up, w_dn, layer, g, b, *, tm=1024):
    T = xb.shape[0]
    tm = min(tm, T)
    nj, _, _, fc = w_up.shape

    def body(xb_ref, xh_ref_in, gi_ref, bi_ref, wu_ref, wd_ref, g_ref, b_ref, u_ref, ut_ref, xob_ref, xh_ref, rs_ref, acc):
        j = pl.program_id(1)

        @pl.when(j == 0)
        def _():
            acc[...] = ALPHA * (xh_ref_in[...] * gi_ref[...] + bi_ref[...])

        u = _dot(xb_ref[...], wu_ref[...])
        u_ref[...] = _mx(u)
        ut_ref[...] = _mx(u.T)
        r = jnp.maximum(u, 0.0)
        acc[...] += _dot(_mx(r * r), wd_ref[...])

        @pl.when(j == nj - 1)
        def _():
            xo, xhat, rstd = _ln_fwd_math(acc[...], g_ref[...], b_ref[...])
            xob_ref[...] = _mx(xo)
            xh_ref[...] = xhat
            rs_ref[...] = rstd

    rows = lambda n: pl.BlockSpec((tm, n), lambda i, j: (i, 0))
    full = lambda shp: pl.BlockSpec(shp, lambda i, j: (0,) * len(shp))
    return pl.pallas_call(
        body, name=f"mlp_fwd_{layer}", grid=(T // tm, nj),
        in_specs=[rows(D_MODEL), rows(D_MODEL), full(g_in.shape), full(b_in.shape),
                  pl.BlockSpec((None, None, D_MODEL, fc), lambda i, j: (j, layer, 0, 0)),
                  pl.BlockSpec((None, None, fc, D_MODEL), lambda i, j: (j, layer, 0, 0)),
                  full(g.shape), full(b.shape)],
        out_specs=[pl.BlockSpec((tm, fc), lambda i, j: (i, j)), pl.BlockSpec((fc, tm), lambda i, j: (j, i)),
                   rows(D_MODEL), rows(D_MODEL), rows(1)],
        out_shape=[S((T, nj * fc), _MXU_DTYPE), S((nj * fc, T), _MXU_DTYPE), S((T, D_MODEL), _MXU_DTYPE),
                   S((T, D_MODEL), F32), S((T, 1), F32)],
        scratch_shapes=[pltpu.VMEM((tm, D_MODEL), F32)],
        compiler_params=_params("parallel", "arbitrary"))(xb, xh_in, g_in, b_in, w_up, w_dn, g, b)


def _mlp_bwd_dx(dxo, xhat, rstd, g, u, w_up, w_dn, layer, *, tm=1024):
    T = dxo.shape[0]
    tm = min(tm, T)
    nj, _, _, fc = w_up.shape

    def body(dxo_ref, xh_ref, rs_ref, g_ref, u_ref, wu_ref, wd_ref, dx_ref, du_ref, dyb_ref, dg_ref, db_ref, acc, dy_sc):
        i, j = pl.program_id(0), pl.program_id(1)

        @pl.when((i == 0) & (j == 0))
        def _():
            dg_ref[...] = jnp.zeros_like(dg_ref)
            db_ref[...] = jnp.zeros_like(db_ref)

        @pl.when(j == 0)
        def _():
            dz, dg, db = _ln_bwd_math(dxo_ref[...], xh_ref[...], rs_ref[...], g_ref[...])
            dg_ref[...] += dg
            db_ref[...] += db
            acc[...] = ALPHA * dz
            dy_sc[...] = _mx(dz)
            dyb_ref[...] = _mx(dz)

        r = jnp.maximum(u_ref[...].astype(F32), 0.0)
        da = _dot(dy_sc[...], wd_ref[...], NT)
        dub = _mx(da * (2.0 * r))
        du_ref[...] = dub
        acc[...] += _dot(dub, wu_ref[...], NT)

        @pl.when(j == nj - 1)
        def _():
            dx_ref[...] = acc[...]

    rows = lambda n: pl.BlockSpec((tm, n), lambda i, j: (i, 0))
    full = lambda shp: pl.BlockSpec(shp, lambda i, j: (0,) * len(shp))
    return pl.pallas_call(
        body, name=f"mlp_bwd_dx_{layer}", grid=(T // tm, nj),
        in_specs=[rows(D_MODEL), rows(D_MODEL), rows(1), full(g.shape), pl.BlockSpec((tm, fc), lambda i, j: (i, j)),
                  pl.BlockSpec((None, None, D_MODEL, fc), lambda i, j: (j, layer, 0, 0)),
                  pl.BlockSpec((None, None, fc, D_MODEL), lambda i, j: (j, layer, 0, 0))],
        out_specs=[rows(D_MODEL), pl.BlockSpec((tm, fc), lambda i, j: (i, j)), rows(D_MODEL), full(g.shape), full(g.shape)],
        out_shape=[S((T, D_MODEL), F32), S((T, nj * fc), _MXU_DTYPE), S((T, D_MODEL), _MXU_DTYPE),
                   S(g.shape, F32), S(g.shape, F32)],
        scratch_shapes=[pltpu.VMEM((tm, D_MODEL), F32), pltpu.VMEM((tm, D_MODEL), _MXU_DTYPE)],
        compiler_params=_params("arbitrary", "arbitrary"))(dxo, xhat, rstd, g, u, w_up, w_dn)


def _mlp_bwd_dw(ut, dyb, xt, du, layer, *, nj, other_layers=None, tm=1024):
    T = ut.shape[1]
    tm = min(tm, T)
    fc = ut.shape[0] // nj
    ni = T // tm

    def body(ut_ref, dy_ref, xt_ref, du_ref, *rest):
        gd_ref, gu_ref, gd_acc, gu_acc, sem = rest[-5:]
        i, j = pl.program_id(0), pl.program_id(1)

        @pl.when(i == 0)
        def _():
            gd_acc[j] = jnp.zeros((fc, D_MODEL), F32)
            gu_acc[j] = jnp.zeros((D_MODEL, fc), F32)

        r = jnp.maximum(ut_ref[...].astype(F32), 0.0)
        gd_acc[j] += _dot(_mx(r * r), dy_ref[...])
        gu_acc[j] += _dot(xt_ref[...], du_ref[...])

        @pl.when((i == ni - 1) & (j == nj - 1))
        def _():
            c1 = pltpu.make_async_copy(gd_acc, gd_ref.at[:, layer], sem.at[0])
            c2 = pltpu.make_async_copy(gu_acc, gu_ref.at[:, layer], sem.at[1])
            c1.start()
            c2.start()
            c1.wait()
            c2.wait()

    hbm = pl.BlockSpec(memory_space=pl.ANY)
    in_specs = [pl.BlockSpec((fc, tm), lambda i, j: (j, i)), pl.BlockSpec((tm, D_MODEL), lambda i, j: (i, 0)),
                pl.BlockSpec((D_MODEL, tm), lambda i, j: (0, i)), pl.BlockSpec((tm, fc), lambda i, j: (i, j))]
    args, aliases = [ut, dyb, xt, du], {}
    if other_layers is not None:
        in_specs += [hbm] * 2
        args += list(other_layers)
        aliases = {4: 0, 5: 1}
    return pl.pallas_call(
        body, name=f"mlp_bwd_dw_{layer}", grid=(ni, nj), in_specs=in_specs, out_specs=[hbm, hbm],
        out_shape=[S((nj, DEPTH, fc, D_MODEL), F32), S((nj, DEPTH, D_MODEL, fc), F32)],
        scratch_shapes=[pltpu.VMEM((nj, fc, D_MODEL), F32), pltpu.VMEM((nj, D_MODEL, fc), F32),
                        pltpu.SemaphoreType.DMA((2,))],
        input_output_aliases=aliases,
        compiler_params=_params("arbitrary", "arbitrary"))(*args)


SWA_GROUP = SWA_QH // SWA_KVH
SWA_ROWS = SWA_GROUP * SWA_BLOCK


def _swa_heads(a, kh):
    return jnp.concatenate([a[:, SWA_D * (kh * SWA_GROUP + g):SWA_D * (kh * SWA_GROUP + g + 1)]
                            for g in range(SWA_GROUP)], axis=0)


def _swa_operands(q, kvp, kvc, kh):
    dkv = SWA_KVH * SWA_D
    qg = _swa_heads(q, kh)
    kb = jnp.concatenate([kvp[:, SWA_D * kh:SWA_D * (kh + 1)], kvc[:, SWA_D * kh:SWA_D * (kh + 1)]], axis=0)
    vb = jnp.concatenate([kvp[:, dkv + SWA_D * kh:dkv + SWA_D * (kh + 1)],
                          kvc[:, dkv + SWA_D * kh:dkv + SWA_D * (kh + 1)]], axis=0)
    return qg, kb, vb, _dot(kb, qg, NT)


def _swa_softmax(s_raw, bias_ref, sink_ref, n, kh):
    cols = slice(kh * SWA_ROWS, (kh + 1) * SWA_ROWS)
    s = s_raw * (SWA_D ** -0.5) + bias_ref[jnp.minimum(n, 1), :, cols]
    sink = sink_ref[:, cols]
    m = jnp.maximum(jnp.max(s, axis=0, keepdims=True), sink)
    p, ps = jnp.exp(s - m), jnp.exp(sink - m)
    inv = 1.0 / (jnp.sum(p, axis=0, keepdims=True) + ps)
    return p * inv, ps * inv


def _swa_attn_fwd(qkv, bias, sinks):
    T = qkv.shape[0]
    blk = SWA_BLOCK
    nb = T // blk
    dq, dkv = SWA_QH * SWA_D, SWA_KVH * SWA_D

    def body(q_ref, kvp_ref, kvc_ref, bias_ref, sink_ref, o_ref):
        n = pl.program_id(0)
        q, kvp, kvc = q_ref[...], kvp_ref[...], kvc_ref[...]
        nxt = _swa_operands(q, kvp, kvc, 0)
        for kh in range(SWA_KVH):
            _, _, vb, s_raw = nxt
            if kh + 1 < SWA_KVH:
                nxt = _swa_operands(q, kvp, kvc, kh + 1)
            p, _ = _swa_softmax(s_raw, bias_ref, sink_ref, n, kh)
            og = _mx(_dot(_mx(p), vb, TN))
            for g in range(SWA_GROUP):
                hd = kh * SWA_GROUP + g
                o_ref[:, SWA_D * hd:SWA_D * (hd + 1)] = og[blk * g:blk * (g + 1), :]

    return pl.pallas_call(
        body, name="swa_attn_fwd", grid=(nb,),
        in_specs=[pl.BlockSpec((blk, dq), lambda n: (n, 0)),
                  pl.BlockSpec((blk, 2 * dkv), lambda n: (jnp.maximum(n - 1, 0), dq // (2 * dkv))),
                  pl.BlockSpec((blk, 2 * dkv), lambda n: (n, dq // (2 * dkv))),
                  pl.BlockSpec(bias.shape, lambda n: (0, 0, 0)), pl.BlockSpec(sinks.shape, lambda n: (0, 0))],
        out_specs=pl.BlockSpec((blk, dq), lambda n: (n, 0)), out_shape=S((T, dq), _MXU_DTYPE),
        compiler_params=_params("parallel"))(qkv, qkv, qkv, bias, sinks)


def _swa_attn_bwd(qkv, ob, do, bias, sinks):
    T = qkv.shape[0]
    blk = SWA_BLOCK
    nb = T // blk
    dq, dkv = SWA_QH * SWA_D, SWA_KVH * SWA_D

    def body(q_ref, kvp_ref, kvc_ref, o_ref, do_ref, bias_ref, sink_ref, dqkv_ref, dbias_ref, dsink_ref, carry):
        st = pl.program_id(0)
        n = nb - 1 - st

        @pl.when(st == 0)
        def _():
            carry[...] = jnp.zeros_like(carry)
            dbias_ref[...] = jnp.zeros_like(dbias_ref)
            dsink_ref[...] = jnp.zeros_like(dsink_ref)

        q, kvp, kvc = q_ref[...], kvp_ref[...], kvc_ref[...]
        ov, dov = o_ref[...], do_ref[...]
        ones = jnp.ones((8, SWA_D), F32)
        def operands(kh):
            qg, kb, vb, s_raw = _swa_operands(q, kvp, kvc, kh)
            dog = _swa_heads(dov, kh)
            dl = _dot(ones, dog * _swa_heads(ov, kh).astype(F32), NT, lax.Precision.HIGHEST)[0:1]
            dogb = _mx(dog)
            return qg, kb, s_raw, dl, dogb, _dot(vb, dogb, NT)

        nxt = operands(0)
        for kh in range(SWA_KVH):
            cols = slice(kh * SWA_ROWS, (kh + 1) * SWA_ROWS)
            qg, kb, s_raw, dl, dogb, dp = nxt
            if kh + 1 < SWA_KVH:
                nxt = operands(kh + 1)
            p, ps = _swa_softmax(s_raw, bias_ref, sink_ref, n, kh)
            ds = p * (dp - dl)
            dbias_ref[:, cols] += ds
            dsink_ref[0:1, cols] += -ps * dl
            dsb = _mx(ds * (SWA_D ** -0.5))
            dqg = _mx(_dot(dsb, kb, TN))
            for g in range(SWA_GROUP):
                hd = kh * SWA_GROUP + g
                dqkv_ref[:, SWA_D * hd:SWA_D * (hd + 1)] = dqg[blk * g:blk * (g + 1), :]
            dkb = _dot(dsb, qg)
            dvb = _dot(_mx(p), dogb)
            ko, vo = SWA_D * kh, dkv + SWA_D * kh
            dqkv_ref[:, dq + ko:dq + ko + SWA_D] = _mx(dkb[blk:, :] + carry[:, ko:ko + SWA_D])
            dqkv_ref[:, dq + vo:dq + vo + SWA_D] = _mx(dvb[blk:, :] + carry[:, vo:vo + SWA_D])
            carry[:, ko:ko + SWA_D] = dkb[:blk, :]
            carry[:, vo:vo + SWA_D] = dvb[:blk, :]

    rev = lambda s: nb - 1 - s
    return pl.pallas_call(
        body, name="swa_attn_bwd", grid=(nb,),
        in_specs=[pl.BlockSpec((blk, dq), lambda s: (rev(s), 0)),
                  pl.BlockSpec((blk, 2 * dkv), lambda s: (jnp.maximum(rev(s) - 1, 0), dq // (2 * dkv))),
                  pl.BlockSpec((blk, 2 * dkv), lambda s: (rev(s), dq // (2 * dkv))),
                  pl.BlockSpec((blk, dq), lambda s: (rev(s), 0)), pl.BlockSpec((blk, dq), lambda s: (rev(s), 0)),
                  pl.BlockSpec(bias.shape, lambda s: (0, 0, 0)), pl.BlockSpec(sinks.shape, lambda s: (0, 0))],
        out_specs=[pl.BlockSpec((blk, dq + 2 * dkv), lambda s: (rev(s), 0)),
                   pl.BlockSpec(bias.shape[1:], lambda s: (0, 0)), pl.BlockSpec((8, sinks.shape[1]), lambda s: (0, 0))],
        out_shape=[S((T, dq + 2 * dkv), _MXU_DTYPE), S(bias.shape[1:], F32), S((8, sinks.shape[1]), F32)],
        scratch_shapes=[pltpu.VMEM((blk, 2 * dkv), F32)],
        compiler_params=_params("arbitrary"))(qkv, qkv, qkv, ob, do, bias, sinks)


def _t5_onehot():
    i = jnp.arange(SWA_BLOCK)
    j = jnp.arange(2 * SWA_BLOCK)
    n = jnp.maximum(i[:, None] + SWA_BLOCK - j[None, :], 0)
    max_exact = REL_BUCKETS // 2
    nf = jnp.maximum(n, 1).astype(F32)
    large = max_exact + (jnp.log(nf / max_exact) / math.log(REL_MAX_DIST / max_exact)
                         * (REL_BUCKETS - max_exact)).astype(jnp.int32)
    large = jnp.minimum(large, REL_BUCKETS - 1)
    bucket = jnp.where(n < max_exact, n, large).reshape(-1)
    return (bucket[None, :] == jnp.arange(REL_BUCKETS)[:, None]).astype(F32)


def _loss_head(yh, g, b, target, *, tm=1024):
    T, D = yh.shape
    tm = min(tm, T)

    def body(y_ref, g_ref, b_ref, t_ref, loss_ref, dy_ref):
        @pl.when(pl.program_id(0) == 0)
        def _():
            loss_ref[...] = jnp.zeros_like(loss_ref)

        d = (y_ref[...] * g_ref[...] + b_ref[...]) - t_ref[...]
        dy_ref[...] = d * (1.0 / D)
        rs = jnp.sum(d * d, axis=1, keepdims=True)
        loss_ref[...] += (0.5 / D) * jnp.sum(rs, axis=0, keepdims=True)

    rows = pl.BlockSpec((tm, D), lambda i: (i, 0))
    vec = pl.BlockSpec(g.shape, lambda i: (0, 0))
    return pl.pallas_call(
        body, name="loss_head", grid=(T // tm,), in_specs=[rows, vec, vec, rows],
        out_specs=[pl.BlockSpec((1, 1), lambda i: (0, 0)), rows], out_shape=[S((1, 1), F32), S((T, D), F32)],
        compiler_params=_params("arbitrary"))(yh, g, b, target)


def _exchange_copies(ins, outs, n_gather, send_sems, recv_sems, loc_sems):
    mx, my, mc = lax.axis_index("x"), lax.axis_index("y"), lax.axis_index("c")
    me = 4 * mx + 2 * my + mc
    copies = []
    for a in range(len(ins)):
        src = ins[a] if a < n_gather else ins[a].at[me]
        copies.append(pltpu.make_async_copy(src, outs[a].at[me], loc_sems.at[a]))
    for k in range(1, N_DEV):
        px, py, pc = mx ^ ((k >> 2) & 1), my ^ ((k >> 1) & 1), mc ^ (k & 1)
        peer = 4 * px + 2 * py + pc
        for a in range(len(ins)):
            src = ins[a] if a < n_gather else ins[a].at[peer]
            copies.append(pltpu.make_async_remote_copy(
                src_ref=src, dst_ref=outs[a].at[me], send_sem=send_sems.at[a, k - 1],
                recv_sem=recv_sems.at[a, k - 1], device_id=(px, py, pc), device_id_type=pl.DeviceIdType.MESH))
    return copies


def _exchange_shapes(gather, scatter):
    n_arr = len(gather) + len(scatter)
    out_shape = [S((N_DEV,) + tuple(g.shape), g.dtype) for g in gather] + [S(s.shape, s.dtype) for s in scatter]
    sems = [pltpu.SemaphoreType.DMA((n_arr, N_DEV - 1)), pltpu.SemaphoreType.DMA((n_arr, N_DEV - 1)),
            pltpu.SemaphoreType.DMA((n_arr,))]
    return out_shape, sems


def _exchange(gather, scatter, *, name):
    n_g = len(gather)
    n_arr = n_g + len(scatter)

    def body(*refs):
        copies = _exchange_copies(refs[:n_arr], refs[n_arr:2 * n_arr], n_g, *refs[2 * n_arr:])
        for cp in copies:
            cp.start()
        for cp in copies:
            cp.wait()

    hbm = pl.BlockSpec(memory_space=pl.ANY)
    out_shape, sems = _exchange_shapes(gather, scatter)
    return pl.pallas_call(
        body, name=name, in_specs=[hbm] * n_arr, out_specs=[hbm] * n_arr, out_shape=out_shape,
        scratch_shapes=sems)(*gather, *scatter)


def _adamw(parts, w, m, v, *, name, tr=256):
    R, C = w.shape
    tr = min(tr, R)
    assert R % tr == 0

    def body(p_ref, w_ref, m_ref, v_ref, g_ref, d_ref, nm_ref, nv_ref):
        g = p_ref[0].astype(F32)
        for k in range(1, N_DEV):
            g = g + p_ref[k].astype(F32)
        g_ref[...] = g
        d_ref[...], nm_ref[...], nv_ref[...] = _adamw_math(g, w_ref[...], m_ref[...], v_ref[...])

    rows = pl.BlockSpec((tr, C), lambda i: (i, 0))
    return pl.pallas_call(
        body, name=name, grid=(R // tr,),
        in_specs=[pl.BlockSpec((N_DEV, tr, C), lambda i: (0, i, 0)), rows, rows, rows],
        out_specs=[rows] * 4, out_shape=[S((R, C), F32)] * 4,
        compiler_params=_params("parallel"))(parts, w, m, v)


def _adamw_math(g, w, m, v):
    m_new = ADAM_B1 * m + (1.0 - ADAM_B1) * g
    v_new = ADAM_B2 * v + (1.0 - ADAM_B2) * (g * g)
    m_hat = m_new / (1.0 - ADAM_B1 ** ADAM_STEP)
    v_hat = v_new / (1.0 - ADAM_B2 ** ADAM_STEP)
    return -ADAM_LR * (m_hat / (jnp.sqrt(v_hat) + ADAM_EPS) + ADAM_WD * w), m_new, v_new


SMALL_ROWS = 48
REPL = {"ln_mix_g": (slice(0, 2), slice(None)), "ln_mix_b": (slice(2, 4), slice(None)),
        "ln_mlp_g": (slice(4, 6), slice(None)), "ln_mlp_b": (slice(6, 8), slice(None)),
        "swa_sinks": (slice(8, 9), slice(0, SWA_QH)), "rel_bias": (slice(16, 16 + REL_BUCKETS), slice(0, SWA_QH))}
GAINS = {"mla_g_q": (slice(0, 1), slice(0, MLA_QR // N_DEV)),
         "mla_g_kv": (slice(0, 1), slice(MLA_QR // N_DEV, (MLA_QR + MLA_C) // N_DEV))}


LOSS_AT = (slice(8, 9), slice(SWA_QH, SWA_QH + 1))


def _adamw_small(r_all, p_gains, W, M, V):
    names = list(REPL) + list(GAINS)

    def body(r_ref, pg_ref, *refs):
        ins, outs, loss_ref = refs[:3 * len(names)], refs[3 * len(names):-1], refs[-1]
        r_sum, g_sum = r_ref[0], pg_ref[0]
        for k in range(1, N_DEV):
            r_sum, g_sum = r_sum + r_ref[k], g_sum + pg_ref[k]
        loss_ref[...] = r_sum[LOSS_AT]
        for i, n in enumerate(names):
            g = r_sum[REPL[n]] if n in REPL else g_sum[GAINS[n]]
            w_ref, m_ref, v_ref = ins[3 * i:3 * i + 3]
            g_ref, d_ref, nm_ref, nv_ref = outs[4 * i:4 * i + 4]
            g_ref[...] = g
            d_ref[...], nm_ref[...], nv_ref[...] = _adamw_math(g, w_ref[...], m_ref[...], v_ref[...])

    flat_in = [d[n] for n in names for d in (W, M, V)]
    res = pl.pallas_call(body, name="adamw_small",
                         out_shape=[S(W[n].shape, F32) for n in names for _ in range(4)] + [S((1, 1), F32)],
                         compiler_params=_params())(r_all, p_gains, *flat_in)
    return {(k, n): res[4 * i + k] for i, n in enumerate(names) for k in range(4)}, res[-1]


WEIGHTS = ["mla_w_in", "mla_g_q", "mla_g_kv", "mla_w_uq", "mla_w_uk", "mla_w_uv", "mla_w_o", "kv_w_shared",
           "swa_w_q", "swa_sinks", "swa_w_o", "rel_bias", "mlp_w_up", "mlp_w_down", "ln_mix_g", "ln_mix_b",
           "ln_mlp_g", "ln_mlp_b"]


def kernel(x, mla_w_in, mla_g_q, mla_g_kv, mla_w_uq, mla_w_uk, mla_w_uv, mla_w_o, kv_w_shared, swa_w_q, swa_sinks, swa_w_o, rel_bias, mlp_w_up, mlp_w_down, ln_mix_g, ln_mix_b, ln_mlp_g, ln_mlp_b, loss_target, m_mla_w_in, m_mla_g_q, m_mla_g_kv, m_mla_w_uq, m_mla_w_uk, m_mla_w_uv, m_mla_w_o, m_kv_w_shared, m_swa_w_q, m_swa_sinks, m_swa_w_o, m_rel_bias, m_mlp_w_up, m_mlp_w_down, m_ln_mix_g, m_ln_mix_b, m_ln_mlp_g, m_ln_mlp_b, v_mla_w_in, v_mla_g_q, v_mla_g_kv, v_mla_w_uq, v_mla_w_uk, v_mla_w_uv, v_mla_w_o, v_kv_w_shared, v_swa_w_q, v_swa_sinks, v_swa_w_o, v_rel_bias, v_mlp_w_up, v_mlp_w_down, v_ln_mix_g, v_ln_mix_b, v_ln_mlp_g, v_ln_mlp_b):
    args = dict(locals())
    W = {n: args[n] for n in WEIGHTS}
    M = {n: args["m_" + n] for n in WEIGHTS}
    V = {n: args["v_" + n] for n in WEIGHTS}
    T = x.shape[1]
    x2d = x.reshape(T, D_MODEL)
    tgt = loss_target.reshape(T, D_MODEL)
    H = MLA_HEADS

    SH = {"mla_w_in": (-1, mla_w_in.shape[-1]), "mla_w_uq": (-1, H * (MLA_NOPE + MLA_ROPE)),
          "mla_w_uk": (-1, H * MLA_NOPE), "mla_w_uv": (-1, H * MLA_V), "mla_w_o": (-1, D_MODEL),
          "kv_w_shared": (-1, kv_w_shared.shape[-1]), "swa_w_q": (-1, swa_w_q.shape[-1]), "swa_w_o": (-1, D_MODEL)}
    slab = lambda d, n: d[n].reshape(SH[n])
    bf = lambda a: a.astype(_MXU_DTYPE)
    gains_slab = lambda d: jnp.pad(jnp.concatenate([d["mla_g_q"], d["mla_g_kv"]], axis=1),
                                   ((0, 7), (0, 128 - d["mla_g_q"].shape[1] - d["mla_g_kv"].shape[1])))
    n_gq, n_gkv = mla_g_q.shape[1], mla_g_kv.shape[1]
    w_in_s, w_uq_s, w_uk_s, gains_all = _exchange(
        [bf(slab(W, "mla_w_in")), bf(slab(W, "mla_w_uq")), bf(slab(W, "mla_w_uk")), gains_slab(W)], [],
        name="gather_mla_in")
    later = [bf(slab(W, n)) for n in ("mla_w_uv", "mla_w_o", "kv_w_shared", "swa_w_q", "swa_w_o")]
    later += [bf(mlp_w_up), bf(mlp_w_down)]
    w_in = w_in_s.reshape(D_MODEL, -1)
    g_q = gains_all[:, 0, :n_gq].reshape(1, MLA_QR)
    g_kv = gains_all[:, 0, n_gq:n_gq + n_gkv].reshape(1, MLA_C)
    w_uq = w_uq_s.reshape(MLA_QR, H, MLA_NOPE + MLA_ROPE)
    w_uq_n = w_uq[:, :, :MLA_NOPE].reshape(MLA_QR, H * MLA_NOPE)
    w_uq_r = w_uq[:, :, MLA_NOPE:].reshape(MLA_QR, H * MLA_ROPE)
    w_uk = w_uk_s.reshape(MLA_C, H, MLA_NOPE).transpose(1, 0, 2)
    w_uk_t = w_uk.transpose(0, 2, 1)
    ln = lambda a, l: a[l].reshape(1, D_MODEL)

    half = MLA_ROPE // 2
    inv = ROPE_THETA ** (-jnp.arange(half, dtype=F32) / half)
    ang = jnp.arange(T, dtype=F32)[:, None] * inv[None, :]
    cos = jnp.tile(jnp.concatenate([jnp.cos(ang), jnp.cos(ang)], -1), (1, ROPE_TABLE_W // MLA_ROPE))
    sin = jnp.tile(jnp.concatenate([-jnp.sin(ang), jnp.sin(ang)], -1), (1, ROPE_TABLE_W // MLA_ROPE))

    h, kc, kct, qs, qst, cq, qn = _mla_pre_fwd(x2d, w_in, g_q, g_kv, w_uq_n, w_uq_r, w_uk_t, cos, sin)
    olat, lse, (w_uv_s, w_o_s, w_kv_s, w_q_s, w_o2_s, w_up, w_dn) = _mla_attn_fwd(qs, kc, kct, gather=later)
    w_uv = w_uv_s.reshape(MLA_C, H, MLA_V).transpose(1, 0, 2)
    w_o = w_o_s.reshape(H * MLA_V, D_MODEL)
    w_qkv = jnp.concatenate([w_q_s.reshape(D_MODEL, -1), w_kv_s.reshape(D_MODEL, -1)], axis=1)
    w_o2 = w_o2_s.reshape(SWA_QH * SWA_D, D_MODEL)
    o_mla = _mla_uv_fwd(olat, w_uv)
    mix0, mlp0 = (ln(ln_mix_g, 0), ln(ln_mix_b, 0)), (ln(ln_mlp_g, 0), ln(ln_mlp_b, 0))
    mix1, mlp1 = (ln(ln_mix_g, 1), ln(ln_mix_b, 1)), (ln(ln_mlp_g, 1), ln(ln_mlp_b, 1))
    x1b, x1t, xh1, rs1 = _proj_ln_fwd(o_mla, w_o, x2d, None, *mix0, name="mla_out_ln_fwd")
    u0, u0t, x2b, xh2, rs2 = _mlp_fwd(x1b, xh1, *mix0, w_up, w_dn, 0, *mlp0)
    onehot = _t5_onehot()
    bias = _mm(rel_bias.T, onehot, name="rel_bias_expand", precision=lax.Precision.HIGHEST, tn=8192).reshape(
        SWA_QH * SWA_BLOCK, 2 * SWA_BLOCK).T
    key = jnp.arange(2 * SWA_BLOCK)[:, None]
    qry = jnp.arange(SWA_QH * SWA_BLOCK)[None, :] % SWA_BLOCK
    in_window = (key > qry) & (key <= qry + SWA_BLOCK)
    bias = jnp.stack([jnp.where(in_window & (key >= SWA_BLOCK), bias, -jnp.inf), jnp.where(in_window, bias, -jnp.inf)])
    sink_rows = jnp.repeat(swa_sinks.reshape(SWA_QH), SWA_BLOCK).reshape(1, SWA_QH * SWA_BLOCK)
    qkv = _mm(x2b, w_qkv, name="swa_qkv_fwd", out_dtype=_MXU_DTYPE, tm=1024, tn=512, tk=1024)
    o_swa = _swa_attn_fwd(qkv, bias, sink_rows)
    x3b, x3t, xh3, rs3 = _proj_ln_fwd(o_swa, w_o2, xh2, mlp0, *mix1, name="swa_out_ln_fwd")
    u1, u1t, _, xh4, rs4 = _mlp_fwd(x3b, xh3, *mix1, w_up, w_dn, 1, *mlp1)
    loss_part, dx4 = _loss_head(xh4, *mlp1, tgt)

    nj = w_up.shape[0]
    dx3, du1, dy4b, dg_mlp1, db_mlp1 = _mlp_bwd_dx(dx4, xh4, rs4, ln(ln_mlp_g, 1), u1, w_up, w_dn, 1)
    g_dn_last, g_up_last = _mlp_bwd_dw(u1t, dy4b, x3t, du1, 1, nj=nj)
    dres3, do_swa, g_o2, dg_mix1, db_mix1 = _proj_ln_bwd(dx3, xh3, rs3, ln(ln_mix_g, 1), o_swa, w_o2,
                                                         name="swa_out_ln_bwd")
    dqkv, dbias, dsink = _swa_attn_bwd(qkv, o_swa, do_swa, bias, sink_rows)
    g_rel = _mm(onehot, dbias.T.reshape(SWA_QH, -1), name="rel_bias_grad", tb=True, precision=lax.Precision.HIGHEST,
                tk=8192)
    head_of_row = (jnp.arange(SWA_QH * SWA_BLOCK)[:, None] // SWA_BLOCK == jnp.arange(SWA_QH)[None, :]).astype(F32)
    g_sinks = _mm(dsink, head_of_row, name="sinks_grad", precision=lax.Precision.HIGHEST, tk=2048)[0:1]
    dx2 = _mm(dqkv, w_qkv, name="swa_qkv_bwd_dx", tb=True, add=dres3, tm=1024, tn=1024, tk=1536)
    g_qkv = _mm(x2b, dqkv, name="swa_qkv_bwd_dw", ta=True, tm=1024, tn=512, tk=1024)
    dx1, du0, dy2b, dg_mlp0, db_mlp0 = _mlp_bwd_dx(dx2, xh2, rs2, ln(ln_mlp_g, 0), u0, w_up, w_dn, 0)
    g_dn, g_up = _mlp_bwd_dw(u0t, dy2b, x1t, du0, 0, nj=nj, other_layers=(g_dn_last, g_up_last))
    dres1, do_mla, g_o, dg_mix0, db_mix0 = _proj_ln_bwd(dx1, xh1, rs1, ln(ln_mix_g, 0), o_mla, w_o,
                                                        name="mla_out_ln_bwd")
    dol, delta, g_uv = _mla_uv_bwd(do_mla, olat, w_uv)
    wide = lambda a, rows: jnp.pad(a, ((0, rows - a.shape[0]), (0, LANES - a.shape[1])))
    r_part = jnp.concatenate([dg_mix0, dg_mix1, db_mix0, db_mix1, dg_mlp0, dg_mlp1, db_mlp0, db_mlp1,
                              wide(jnp.concatenate([g_sinks, loss_part], axis=1), 8), wide(g_rel, SMALL_ROWS - 16)],
                             axis=0)
    by_dev = lambda g: g.reshape((N_DEV, g.shape[0] // N_DEV) + g.shape[1:])
    early = [by_dev(g_o2), by_dev(g_qkv), g_up, g_dn, by_dev(g_o),
             by_dev(g_uv.transpose(1, 0, 2).reshape(MLA_C, H * MLA_V))]
    dqs, dkc, dv, (r_all, p_o2, p_qkv, p_up, p_dn, p_o, p_uv) = _mla_attn_bwd(
        qs, qst, kc, kct, dol, lse, delta, gather=[r_part], scatter=early)
    grad_x, g_in, g_uq_n, g_uq_r, g_uk, g_gq, g_gkv = _mla_pre_bwd(
        dqs, dkc, dv, h, x2d, dres1, cq, qn, cos, sin, w_in, g_q, g_kv, w_uq_n, w_uq_r, w_uk)
    g_uq = jnp.concatenate([g_uq_n.reshape(MLA_QR, H, MLA_NOPE), g_uq_r.reshape(MLA_QR, H, MLA_ROPE)], -1)
    g_gains = jnp.pad(jnp.concatenate([g_gq.reshape(N_DEV, n_gq), g_gkv.reshape(N_DEV, n_gkv)], axis=1)[:, None, :],
                      ((0, 0), (0, 7), (0, 128 - n_gq - n_gkv)))
    p_in, p_uq, p_uk, p_gains = _exchange(
        [], [bf(by_dev(g_in)), bf(by_dev(g_uq.reshape(MLA_QR, -1))),
             bf(by_dev(g_uk.transpose(1, 0, 2).reshape(MLA_C, -1))), g_gains], name="exchange_mla_in_grads")

    res = {}

    def adam(name, parts, names, to_slab, from_slab):
        out = _adamw(parts, to_slab(W), to_slab(M), to_slab(V), name="adamw_" + name)
        for k in range(4):
            for n, a in zip(names, from_slab(out[k])):
                res[(k, n)] = a.reshape(W[n].shape)

    one = lambda n: (lambda d: slab(d, n))
    adam("swa_w_o", p_o2, ["swa_w_o"], one("swa_w_o"), lambda s: [s])
    dq_cols = SWA_QH * SWA_D
    adam("swa_qkv", p_qkv, ["swa_w_q", "kv_w_shared"],
         lambda d: jnp.concatenate([slab(d, "swa_w_q"), slab(d, "kv_w_shared")], axis=1),
         lambda s: [s[:, :dq_cols], s[:, dq_cols:]])
    layers_as_rows = lambda a: a.reshape((-1,) + a.shape[-1:])
    adam("mlp_w_up", p_up.reshape(N_DEV, -1, p_up.shape[-1]), ["mlp_w_up"],
         lambda d: layers_as_rows(d["mlp_w_up"]), lambda s: [s])
    adam("mlp_w_down", p_dn.reshape(N_DEV, -1, p_dn.shape[-1]), ["mlp_w_down"],
         lambda d: layers_as_rows(d["mlp_w_down"]), lambda s: [s])
    adam("mla_w_o", p_o, ["mla_w_o"], one("mla_w_o"), lambda s: [s])
    adam("mla_w_uv", p_uv, ["mla_w_uv"], one("mla_w_uv"), lambda s: [s])
    adam("mla_w_in", p_in, ["mla_w_in"], one("mla_w_in"), lambda s: [s])
    adam("mla_w_uq", p_uq, ["mla_w_uq"], one("mla_w_uq"), lambda s: [s])
    adam("mla_w_uk", p_uk, ["mla_w_uk"], one("mla_w_uk"), lambda s: [s])
    small, loss = _adamw_small(r_all, p_gains, W, M, V)
    res.update(small)
    loss = loss.reshape(())
    return (loss, grad_x.reshape(x.shape), *[res[(k, n)] for k in range(4) for n in WEIGHTS])
```

```python
import math

import numpy as np
import jax
import jax.numpy as jnp
from jax import lax
from jax.experimental import pallas as pl
from jax.experimental.pallas import tpu as pltpu

F32 = jnp.float32
_MXU_DTYPE = jnp.bfloat16

D_MODEL = 1024
DEPTH = 2
MLA_HEADS = 8
MLA_NOPE = 128
MLA_ROPE = 64
MLA_V = 128
MLA_QR = 384
MLA_C = 256
MLA_DK = 384
MLA_DT = MLA_C + MLA_ROPE
ROPE_THETA = 10000.0
SWA_QH = 16
SWA_KVH = 4
SWA_D = 64
SWA_BLOCK = 128
REL_BUCKETS = 32
REL_MAX_DIST = 128
D_FF = 4096
LN_EPS = 1e-5
RMS_EPS = 1e-6
ALPHA = (2 * DEPTH) ** 0.25
ADAM_LR, ADAM_B1, ADAM_B2, ADAM_EPS, ADAM_WD, ADAM_STEP = 0.001, 0.9, 0.999, 1e-08, 0.01, 10

N_DEV = 8
AXES = ("x", "y", "c")
V7X_VMEM_BYTES = 64 * 1024 * 1024
VMEM_LIMIT = V7X_VMEM_BYTES - 8 * 1024 * 1024
LANES = 1024
ATT_TQ = 512
ATT_TK = 512
ATT_HEAD_GROUP = 1
ATT_FWD_HEAD_GROUP = 2

NT = (((1,), (1,)), ((), ()))
TN = (((0,), (0,)), ((), ()))
S = jax.ShapeDtypeStruct


def _params(*sem, vmem=VMEM_LIMIT):
    return pltpu.CompilerParams(dimension_semantics=sem, vmem_limit_bytes=vmem)


def _dot(a, b, dims=None, precision=None):
    if dims is None:
        return jnp.dot(a, b, preferred_element_type=F32, precision=precision)
    return lax.dot_general(a, b, dims, preferred_element_type=F32, precision=precision)


def _mx(v):
    return v.astype(_MXU_DTYPE)


ROPE_TABLE_W = 128


def _tile_heads(t):
    return jnp.concatenate([t] * (MLA_HEADS * MLA_ROPE // ROPE_TABLE_W), axis=1)


def _swap_halves_64(v):
    return jnp.concatenate([v[:, 32:], v[:, :32]], axis=-1)


def _swap_halves_groups(v):
    n = v.shape[-1]
    lane = lax.broadcasted_iota(jnp.int32, v.shape, 1)
    return jnp.where(lane % 64 < 32, pltpu.roll(v, n - 32, 1), pltpu.roll(v, 32, 1))


def _mm(a, b, *, name, ta=False, tb=False, add=None, out_dtype=F32, tm=512, tn=512, tk=512, precision=None):
    M, K = (a.shape[1], a.shape[0]) if ta else a.shape
    N = b.shape[0] if tb else b.shape[1]
    tm, tn, tk = min(tm, M), min(tn, N), min(tk, K)
    assert M % tm == 0 and N % tn == 0 and K % tk == 0, (M, N, K, tm, tn, tk)
    nk = K // tk
    dims = (((0 if ta else 1,), (1 if tb else 0,)), ((), ()))
    has_add = add is not None

    def body(*refs):
        if has_add:
            a_ref, b_ref, add_ref, o_ref, acc = refs
        else:
            a_ref, b_ref, o_ref, acc = refs
        k = pl.program_id(2)
        av, bv = a_ref[...], b_ref[...]
        if precision is None:
            av, bv = _mx(av), _mx(bv)
        part = _dot(av, bv, dims, precision)
        if nk == 1:
            o_ref[...] = (part + add_ref[...] if has_add else part).astype(out_dtype)
            return

        @pl.when(k == 0)
        def _():
            acc[...] = add_ref[...] if has_add else jnp.zeros_like(acc)

        acc[...] += part

        @pl.when(k == nk - 1)
        def _():
            o_ref[...] = acc[...].astype(out_dtype)

    a_spec = pl.BlockSpec((tk, tm), lambda i, j, k: (k, i)) if ta else pl.BlockSpec((tm, tk), lambda i, j, k: (i, k))
    b_spec = pl.BlockSpec((tn, tk), lambda i, j, k: (j, k)) if tb else pl.BlockSpec((tk, tn), lambda i, j, k: (k, j))
    in_specs = [a_spec, b_spec]
    args = [a, b]
    if has_add:
        in_specs.append(pl.BlockSpec((tm, tn), lambda i, j, k: (i, j)))
        args.append(add)
    return pl.pallas_call(
        body, name=name, grid=(M // tm, N // tn, nk), in_specs=in_specs,
        out_specs=pl.BlockSpec((tm, tn), lambda i, j, k: (i, j)), out_shape=S((M, N), out_dtype),
        scratch_shapes=[pltpu.VMEM((tm, tn), F32)],
        compiler_params=_params("parallel", "parallel", "arbitrary"))(*args)


def _ln_fwd_math(z, g, b):
    mu = jnp.mean(z, axis=-1, keepdims=True)
    zc = z - mu
    var = jnp.mean(zc * zc, axis=-1, keepdims=True)
    rstd = lax.rsqrt(var + LN_EPS)
    xhat = zc * rstd
    return xhat * g + b, xhat, rstd


def _ln_bwd_math(dxo, xhat, rstd, g):
    dxh = dxo * g
    m1 = jnp.mean(dxh, axis=-1, keepdims=True)
    m2 = jnp.mean(dxh * xhat, axis=-1, keepdims=True)
    dz = rstd * (dxh - m1 - xhat * m2)
    dg = jnp.sum(dxo * xhat, axis=0, keepdims=True)
    db = jnp.sum(dxo, axis=0, keepdims=True)
    return dz, dg, db


def _rms_fwd_math(xr, g):
    r = lax.rsqrt(jnp.mean(xr * xr, axis=-1, keepdims=True) + RMS_EPS)
    return xr * r * g


def _rms_bwd_math(dy, xr, g):
    r = lax.rsqrt(jnp.mean(xr * xr, axis=-1, keepdims=True) + RMS_EPS)
    gy = dy * g
    dx = r * gy - xr * (r * r * r) * jnp.mean(gy * xr, axis=-1, keepdims=True)
    dg = jnp.sum(dy * xr * r, axis=0, keepdims=True)
    return dx, dg


def _mla_pre_fwd(x, w_in, g_q, g_kv, w_uq_n, w_uq_r, w_uk_t, cos, sin):
    T = x.shape[0]
    tm = min(ATT_TQ, T)
    nq = T // tm
    H = MLA_HEADS

    tk = min(ATT_TK, T)

    def body(x_ref, win_ref, gq_ref, gkv_ref, wn_ref, wr_ref, wuk_ref, cos_ref, sin_ref,
             h_ref, kc_ref, kct_ref, qs_ref, qst_ref, cq_ref, qn_ref):
        h = _dot(_mx(x_ref[...]), win_ref[...])
        h_ref[...] = h
        cos_v, sin_v = _tile_heads(cos_ref[...]), _tile_heads(sin_ref[...])
        cq = _mx(_rms_fwd_math(h[:, :MLA_QR], gq_ref[...]))
        ckv = _rms_fwd_math(h[:, MLA_QR:MLA_QR + MLA_C], gkv_ref[...])
        krr = h[:, MLA_QR + MLA_C:]
        kr = krr * cos_v[:, :MLA_ROPE] + _swap_halves_64(krr) * sin_v[:, :MLA_ROPE]
        kr_pad = jnp.concatenate([kr, jnp.zeros((tm, MLA_DK - MLA_C - MLA_ROPE), F32)], axis=1)
        kc_ref[:, 0:MLA_C] = _mx(ckv)
        kc_ref[:, MLA_C:] = _mx(kr_pad)
        kct_ref[0:MLA_C, :] = _mx(ckv.T)
        kct_ref[MLA_C:, :] = _mx(kr_pad.T[0:MLA_ROPE, :])
        cq_ref[...] = cq
        qnb = _mx(_dot(cq, wn_ref[...]))
        qn_ref[...] = qnb
        qr = _dot(cq, wr_ref[...])
        qrr = qr * cos_v + _swap_halves_groups(qr) * sin_v
        qrr_t = qrr.T
        for hd in range(H):
            ql = _dot(qnb[:, MLA_NOPE * hd:MLA_NOPE * (hd + 1)], wuk_ref[hd])
            qst_ref[0, 0:MLA_C, tm * hd:tm * (hd + 1)] = _mx(ql.T)
            qst_ref[0, MLA_C:, tm * hd:tm * (hd + 1)] = _mx(qrr_t[MLA_ROPE * hd:MLA_ROPE * (hd + 1), :])
            qs_ref[0, hd, :, 0:MLA_C] = _mx(ql)
            qs_ref[0, hd, :, MLA_C:MLA_C + MLA_ROPE] = _mx(qrr[:, MLA_ROPE * hd:MLA_ROPE * (hd + 1)])
            qs_ref[0, hd, :, MLA_C + MLA_ROPE:] = jnp.zeros((tm, MLA_DK - MLA_C - MLA_ROPE), _MXU_DTYPE)

    full = lambda shp: pl.BlockSpec(shp, lambda i: (0,) * len(shp))
    rows = lambda n: pl.BlockSpec((tm, n), lambda i: (i, 0))
    n_in = w_in.shape[1]
    return pl.pallas_call(
        body, name="mla_pre_fwd", grid=(nq,),
        in_specs=[rows(D_MODEL), full(w_in.shape), full(g_q.shape), full(g_kv.shape), full(w_uq_n.shape),
                  full(w_uq_r.shape), full(w_uk_t.shape), rows(ROPE_TABLE_W), rows(ROPE_TABLE_W)],
        out_specs=[rows(n_in), rows(MLA_DK),
                   pl.BlockSpec((None, MLA_DT, tm), lambda i: (i * tm // tk, 0, i % (tk // tm))),
                   pl.BlockSpec((1, H, tm, MLA_DK), lambda i: (i, 0, 0, 0)),
                   pl.BlockSpec((1, MLA_DT, H * tm), lambda i: (i, 0, 0)), rows(MLA_QR), rows(H * MLA_NOPE)],
        out_shape=[S((T, n_in), F32), S((T, MLA_DK), _MXU_DTYPE), S((T // tk, MLA_DT, tk), _MXU_DTYPE),
                   S((nq, H, tm, MLA_DK), _MXU_DTYPE), S((nq, MLA_DT, H * tm), _MXU_DTYPE),
                   S((T, MLA_QR), _MXU_DTYPE), S((T, H * MLA_NOPE), _MXU_DTYPE)],
        compiler_params=_params("parallel"))(x, w_in, g_q, g_kv, w_uq_n, w_uq_r, w_uk_t, cos, sin)


def _att_steps(T, tq, tk):
    qi, kj = [], []
    for i in range(T // tq):
        for j in range((i * tq + tq - 1) // tk + 1):
            qi.append(i)
            kj.append(j)
    return jnp.asarray(np.array(qi, np.int32)), jnp.asarray(np.array(kj, np.int32))


def _ride_exchange(st, n_steps, ins, outs, n_gather, sems):
    if not ins:
        return

    @pl.when(st == 0)
    def _():
        for cp in _exchange_copies(ins, outs, n_gather, *sems):
            cp.start()

    @pl.when(st == n_steps - 1)
    def _():
        for cp in _exchange_copies(ins, outs, n_gather, *sems):
            cp.wait()


def _mla_attn_fwd(qs, kc, kct, gather=(), scatter=()):
    nq, H, tq, DK = qs.shape
    T = kc.shape[0]
    tk = min(ATT_TK, T)
    scale = (MLA_NOPE + MLA_ROPE) ** -0.5
    c2 = scale * math.log2(math.e)
    qi, kj = _att_steps(T, tq, tk)
    n_steps = int(qi.shape[0])
    hg = ATT_FWD_HEAD_GROUP
    R = hg * tq
    n_x = len(gather) + len(scatter)

    def body(qi_ref, kj_ref, q_ref, k_ref, kt_ref, *rest):
        x_ins, (o_ref, lse_ref), x_outs = rest[:n_x], rest[n_x:n_x + 2], rest[n_x + 2:2 * n_x + 2]
        m_sc, l_sc, acc_sc = rest[2 * n_x + 2:2 * n_x + 5]
        st = pl.program_id(0)
        _ride_exchange(st, n_steps, x_ins, x_outs, len(gather), rest[2 * n_x + 5:])
        i, j = qi_ref[st], kj_ref[st]
        j_last = (i * tq + tq - 1) // tk

        @pl.when(j == 0)
        def _():
            m_sc[...] = jnp.full_like(m_sc, -jnp.inf)
            l_sc[...] = jnp.zeros_like(l_sc)
            acc_sc[...] = jnp.zeros_like(acc_sc)

        def step(masked):
            k = k_ref[...]
            vt = kt_ref[0:MLA_C, :]
            if masked:
                key = lax.broadcasted_iota(jnp.int32, (tk, R), 0) + j * tk
                qry = lax.broadcasted_iota(jnp.int32, (tk, R), 1) % tq + i * tq
                causal = key <= qry
            n_g = H // hg
            qk = lambda g: _dot(k, q_ref[0, g * hg:(g + 1) * hg].reshape(R, DK), NT)
            def accumulate(g, a, pb):
                cs = slice(g * R, (g + 1) * R)
                acc_sc[:, cs] = a * acc_sc[:, cs] + _dot(vt, pb)

            s_next = qk(0)
            pending = None
            for g in range(n_g):
                cs = slice(g * R, (g + 1) * R)
                s = s_next
                if g + 1 < n_g:
                    s_next = qk(g + 1)
                if pending is not None:
                    accumulate(*pending)
                if masked:
                    s = jnp.where(causal, s, -jnp.inf)
                m_prev = m_sc[:, cs]
                m_new = jnp.maximum(m_prev, jnp.max(s, axis=0, keepdims=True))
                a = jnp.exp2((m_prev - m_new) * c2)
                p = jnp.exp2((s - m_new) * c2)
                l_sc[:, cs] = a * l_sc[:, cs] + jnp.sum(p, axis=0, keepdims=True)
                m_sc[:, cs] = m_new
                pending = (g, a, _mx(p))
            accumulate(*pending)

        pl.when(j == j_last)(lambda: step(True))
        pl.when(j != j_last)(lambda: step(False))

        @pl.when(j == j_last)
        def _():
            o_ref[0] = _mx(acc_sc[...] / l_sc[...])
            lse_ref[0] = m_sc[...] * scale + jnp.log(l_sc[...])

    hbm = pl.BlockSpec(memory_space=pl.ANY)
    x_shapes, x_sems = _exchange_shapes(gather, scatter) if n_x else ([], [])
    gs = pltpu.PrefetchScalarGridSpec(
        num_scalar_prefetch=2, grid=(n_steps,),
        in_specs=[pl.BlockSpec((1, H, tq, DK), lambda s, qi, kj: (qi[s], 0, 0, 0)),
                  pl.BlockSpec((tk, DK), lambda s, qi, kj: (kj[s], 0)),
                  pl.BlockSpec((None, MLA_DT, tk), lambda s, qi, kj: (kj[s], 0, 0))] + [hbm] * n_x,
        out_specs=[pl.BlockSpec((1, MLA_C, H * tq), lambda s, qi, kj: (qi[s], 0, 0)),
                   pl.BlockSpec((1, 1, H * tq), lambda s, qi, kj: (qi[s], 0, 0))] + [hbm] * n_x,
        scratch_shapes=[pltpu.VMEM((1, H * tq), F32), pltpu.VMEM((1, H * tq), F32),
                        pltpu.VMEM((MLA_C, H * tq), F32)] + x_sems)
    res = pl.pallas_call(
        body, name="mla_attn_fwd", grid_spec=gs,
        out_shape=[S((nq, MLA_C, H * tq), _MXU_DTYPE), S((nq, 1, H * tq), F32)] + x_shapes,
        compiler_params=_params("arbitrary"))(qi, kj, qs, kc, kct, *gather, *scatter)
    return res[0], res[1], res[2:]


def _mla_attn_bwd(qs, qst, kc, kct, dol, lse, delta, gather=(), scatter=()):
    nq, H, tq, DK = qs.shape
    T = kc.shape[0]
    tk = min(ATT_TK, T)
    scale = (MLA_NOPE + MLA_ROPE) ** -0.5
    log2e = math.log2(math.e)
    qi, kj = _att_steps(T, tq, tk)
    n_steps = int(qi.shape[0])
    hg = ATT_HEAD_GROUP
    R = hg * tq
    n_x = len(gather) + len(scatter)

    def body(qi_ref, kj_ref, q_ref, qt_ref, k_ref, kt_ref, do_ref, lse_ref, dl_ref, *rest):
        x_ins, (dq_ref, dk_ref, dv_ref), x_outs = rest[:n_x], rest[n_x:n_x + 3], rest[n_x + 3:2 * n_x + 3]
        dk_acc, dv_acc, sem = rest[2 * n_x + 3:2 * n_x + 6]
        st = pl.program_id(0)
        _ride_exchange(st, n_steps, x_ins, x_outs, len(gather), rest[2 * n_x + 6:])
        i, j = qi_ref[st], kj_ref[st]
        j_last = (i * tq + tq - 1) // tk

        @pl.when(st == 0)
        def _():
            dk_acc[...] = jnp.zeros_like(dk_acc)
            dv_acc[...] = jnp.zeros_like(dv_acc)

        @pl.when(j == 0)
        def _():
            dq_ref[...] = jnp.zeros_like(dq_ref)

        def step(masked):
            k, kt = k_ref[...], kt_ref[...]
            v = k[:, :MLA_C]
            if masked:
                key = lax.broadcasted_iota(jnp.int32, (tk, R), 0) + j * tk
                qry = lax.broadcasted_iota(jnp.int32, (tk, R), 1) % tq + i * tq
                causal = key <= qry
            dkt_c = jnp.zeros((MLA_DT, tk), F32)
            dvt_c = jnp.zeros((MLA_C, tk), F32)
            n_g = H // hg

            def scores(g):
                q = q_ref[0, g * hg:(g + 1) * hg].reshape(R, DK)
                dot = do_ref[0, :, g * R:(g + 1) * R]
                return dot, _dot(k, q, NT), _dot(v, dot)

            nxt = scores(0)
            for g in range(n_g):
                cs = slice(g * R, (g + 1) * R)
                dot, s, dp = nxt
                if g + 1 < n_g:
                    nxt = scores(g + 1)
                p = jnp.exp2(s * (scale * log2e) - lse_ref[0, :, cs] * log2e)
                if masked:
                    p = jnp.where(causal, p, 0.0)
                dsb = _mx(p * (dp - dl_ref[0, :, cs]))
                dq_ref[0, :, cs] += _dot(kt, dsb)
                dkt_c = dkt_c + _dot(qt_ref[0, :, cs], dsb, NT)
                dvt_c = dvt_c + _dot(dot, _mx(p), NT)
            dk_acc[j] += dkt_c * scale
            dv_acc[j] += dvt_c

        pl.when(j == j_last)(lambda: step(True))
        pl.when(j != j_last)(lambda: step(False))

        @pl.when(j == j_last)
        def _():
            dq_ref[...] = dq_ref[...] * scale

        @pl.when(st == n_steps - 1)
        def _():
            c1 = pltpu.make_async_copy(dk_acc, dk_ref, sem.at[0])
            c2 = pltpu.make_async_copy(dv_acc, dv_ref, sem.at[1])
            c1.start()
            c2.start()
            c1.wait()
            c2.wait()

    cols = lambda n: pl.BlockSpec((1, n, H * tq), lambda s, qi, kj: (qi[s], 0, 0))
    hbm = pl.BlockSpec(memory_space=pl.ANY)
    x_shapes, x_sems = _exchange_shapes(gather, scatter) if n_x else ([], [])
    gs = pltpu.PrefetchScalarGridSpec(
        num_scalar_prefetch=2, grid=(n_steps,),
        in_specs=[pl.BlockSpec((1, H, tq, DK), lambda s, qi, kj: (qi[s], 0, 0, 0)), cols(MLA_DT),
                  pl.BlockSpec((tk, DK), lambda s, qi, kj: (kj[s], 0)),
                  pl.BlockSpec((None, MLA_DT, tk), lambda s, qi, kj: (kj[s], 0, 0)),
                  cols(MLA_C), cols(1), cols(1)] + [hbm] * n_x,
        out_specs=[cols(MLA_DT), hbm, hbm] + [hbm] * n_x,
        scratch_shapes=[pltpu.VMEM((T // tk, MLA_DT, tk), F32), pltpu.VMEM((T // tk, MLA_C, tk), F32),
                        pltpu.SemaphoreType.DMA((2,))] + x_sems)
    res = pl.pallas_call(
        body, name="mla_attn_bwd", grid_spec=gs,
        out_shape=[S((nq, MLA_DT, H * tq), F32), S((T // tk, MLA_DT, tk), F32),
                   S((T // tk, MLA_C, tk), F32)] + x_shapes,
        compiler_params=_params("arbitrary"))(qi, kj, qs, qst, kc, kct, dol, lse, delta, *gather, *scatter)
    return res[0], res[1], res[2], res[3:]


def _mla_uv_fwd(olat, w_uv):
    nq, C, cols = olat.shape
    H = w_uv.shape[0]
    tq = cols // H

    def body(ol_ref, wuv_ref, o_ref):
        for hd in range(H):
            o_ref[:, MLA_V * hd:MLA_V * (hd + 1)] = _mx(_dot(ol_ref[0, :, tq * hd:tq * (hd + 1)], wuv_ref[hd], TN))

    return pl.pallas_call(
        body, name="mla_uv_fwd", grid=(nq,),
        in_specs=[pl.BlockSpec((1, C, cols), lambda i: (i, 0, 0)), pl.BlockSpec(w_uv.shape, lambda i: (0, 0, 0))],
        out_specs=pl.BlockSpec((tq, H * MLA_V), lambda i: (i, 0)), out_shape=S((nq * tq, H * MLA_V), _MXU_DTYPE),
        compiler_params=_params("parallel"))(olat, w_uv)


def _mla_uv_bwd(do, olat, w_uv):
    nq, C, cols = olat.shape
    H = w_uv.shape[0]
    tq = cols // H

    def body(do_ref, ol_ref, wuv_ref, dol_ref, dl_ref, dw_ref):
        @pl.when(pl.program_id(0) == 0)
        def _():
            dw_ref[...] = jnp.zeros_like(dw_ref)

        dov = do_ref[...]
        for hd in range(H):
            cs = slice(tq * hd, tq * (hd + 1))
            doh = _mx(dov[:, MLA_V * hd:MLA_V * (hd + 1)])
            ol = ol_ref[0, :, cs]
            dol = _dot(wuv_ref[hd], doh, NT)
            dol_ref[0, :, cs] = _mx(dol)
            dl_ref[0, :, cs] = jnp.sum(dol * ol.astype(F32), axis=0, keepdims=True)
            dw_ref[hd] += _dot(ol, doh)

    blk = lambda n: pl.BlockSpec((1, n, cols), lambda i: (i, 0, 0))
    return pl.pallas_call(
        body, name="mla_uv_bwd", grid=(nq,),
        in_specs=[pl.BlockSpec((tq, H * MLA_V), lambda i: (i, 0)), blk(C), pl.BlockSpec(w_uv.shape, lambda i: (0, 0, 0))],
        out_specs=[blk(C), blk(1), pl.BlockSpec(w_uv.shape, lambda i: (0, 0, 0))],
        out_shape=[S(olat.shape, _MXU_DTYPE), S((nq, 1, cols), F32), S(w_uv.shape, F32)],
        compiler_params=_params("arbitrary"))(do, olat, w_uv)


def _mla_pre_bwd(dqs, dkc, dv, h, x, dres, cq, qn, cos, sin, w_in, g_q, g_kv, w_uq_n, w_uq_r, w_uk):
    nq, DK, cols = dqs.shape
    H = w_uk.shape[0]
    tm = cols // H
    T = nq * tm
    tk = dv.shape[2]
    n_in = w_in.shape[1]

    def body(dqs_ref, dkc_ref, dv_ref, h_ref, x_ref, dres_ref, cq_ref, qn_ref, cos_ref, sin_ref,
             win_ref, gq_ref, gkv_ref, wn_ref, wr_ref, wuk_ref,
             gx_ref, dwin_ref, dwn_ref, dwr_ref, dwuk_ref, dgq_ref, dgkv_ref, dqn_sc, dqr_sc, dh_sc):
        @pl.when(pl.program_id(0) == 0)
        def _():
            for r in (dwin_ref, dwn_ref, dwr_ref, dwuk_ref, dgq_ref, dgkv_ref):
                r[...] = jnp.zeros_like(r)

        cos_v, sin_v = _tile_heads(cos_ref[...]), _tile_heads(sin_ref[...])
        qnb = qn_ref[...]
        for hd in range(H):
            cs = slice(tm * hd, tm * (hd + 1))
            dql = _mx(dqs_ref[0, 0:MLA_C, cs])
            dqn_sc[:, MLA_NOPE * hd:MLA_NOPE * (hd + 1)] = _dot(dql, wuk_ref[hd], TN)
            dwuk_ref[hd] += _dot(dql, qnb[:, MLA_NOPE * hd:MLA_NOPE * (hd + 1)])
            dqr_sc[MLA_ROPE * hd:MLA_ROPE * (hd + 1), :] = dqs_ref[0, MLA_C:MLA_C + MLA_ROPE, cs]
        dqr = dqr_sc[...].T
        dqrb = _mx(dqr * cos_v + _swap_halves_groups(dqr * sin_v))
        dqnb = _mx(dqn_sc[...])
        cq = cq_ref[...]
        dwn_ref[...] += _dot(cq, dqnb, TN)
        dwr_ref[...] += _dot(cq, dqrb, TN)
        dcq = _dot(dqnb, wn_ref[...], NT) + _dot(dqrb, wr_ref[...], NT)
        hv = h_ref[...]
        dxq, dgq = _rms_bwd_math(dcq, hv[:, :MLA_QR], gq_ref[...])
        dgq_ref[...] += dgq
        dckv = (dkc_ref[0:MLA_C, :] + dv_ref[...]).T
        dxkv, dgkv = _rms_bwd_math(dckv, hv[:, MLA_QR:MLA_QR + MLA_C], gkv_ref[...])
        dgkv_ref[...] += dgkv
        dkr = jnp.concatenate([dkc_ref[MLA_C:, :], jnp.zeros((128 - MLA_ROPE, tm), F32)], axis=0).T[:, :MLA_ROPE]
        dkr_raw = dkr * cos_v[:, :MLA_ROPE] + _swap_halves_64(dkr * sin_v[:, :MLA_ROPE])
        dh_sc[:, 0:MLA_QR] = dxq
        dh_sc[:, MLA_QR:MLA_QR + MLA_C] = dxkv
        dh_sc[:, MLA_QR + MLA_C:] = dkr_raw
        dhb = _mx(dh_sc[...])
        gx_ref[...] = dres_ref[...] + _dot(dhb, win_ref[...], NT)
        dwin_ref[...] += _dot(_mx(x_ref[...]), dhb, TN)

    full = lambda shp: pl.BlockSpec(shp, lambda i: (0,) * len(shp))
    rows = lambda n: pl.BlockSpec((tm, n), lambda i: (i, 0))
    return pl.pallas_call(
        body, name="mla_pre_bwd", grid=(nq,),
        in_specs=[pl.BlockSpec((1, DK, cols), lambda i: (i, 0, 0)),
                  pl.BlockSpec((None, DK, tm), lambda i: (i * tm // tk, 0, i % (tk // tm))),
                  pl.BlockSpec((None, MLA_C, tm), lambda i: (i * tm // tk, 0, i % (tk // tm))), rows(n_in),
                  rows(D_MODEL), rows(D_MODEL), rows(MLA_QR), rows(H * MLA_NOPE), rows(ROPE_TABLE_W), rows(ROPE_TABLE_W),
                  full(w_in.shape), full(g_q.shape), full(g_kv.shape), full(w_uq_n.shape), full(w_uq_r.shape),
                  full(w_uk.shape)],
        out_specs=[rows(D_MODEL), full(w_in.shape), full(w_uq_n.shape), full(w_uq_r.shape), full(w_uk.shape),
                   full(g_q.shape), full(g_kv.shape)],
        out_shape=[S((T, D_MODEL), F32), S(w_in.shape, F32), S(w_uq_n.shape, F32), S(w_uq_r.shape, F32),
                   S(w_uk.shape, F32), S(g_q.shape, F32), S(g_kv.shape, F32)],
        scratch_shapes=[pltpu.VMEM((tm, H * MLA_NOPE), F32), pltpu.VMEM((H * MLA_ROPE, tm), F32),
                        pltpu.VMEM((tm, n_in), F32)],
        compiler_params=_params("arbitrary"))(dqs, dkc, dv, h, x, dres, cq, qn, cos, sin, w_in, g_q, g_kv,
                                              w_uq_n, w_uq_r, w_uk)


def _proj_ln_fwd(a, w, xres, res_gb, g, b, *, name, tm=512):
    T, K = a.shape
    tm = min(tm, T)
    gp, bp = res_gb if res_gb is not None else (None, None)

    def body(a_ref, w_ref, x_ref, *rest):
        if res_gb is not None:
            x = x_ref[...] * rest[0][...] + rest[1][...]
            rest = rest[2:]
        else:
            x = x_ref[...]
        g_ref, b_ref, xob_ref, xt_ref, xh_ref, rs_ref = rest
        z = ALPHA * x + _dot(a_ref[...], w_ref[...])
        xo, xhat, rstd = _ln_fwd_math(z, g_ref[...], b_ref[...])
        xob_ref[...] = _mx(xo)
        xt_ref[...] = _mx(xo.T)
        xh_ref[...] = xhat
        rs_ref[...] = rstd

    rows = lambda n: pl.BlockSpec((tm, n), lambda i: (i, 0))
    full = lambda shp: pl.BlockSpec(shp, lambda i: (0,) * len(shp))
    extra = [gp, bp] if res_gb is not None else []
    return pl.pallas_call(
        body, name=name, grid=(T // tm,),
        in_specs=[rows(K), full(w.shape), rows(D_MODEL)] + [full(e.shape) for e in extra] + [full(g.shape), full(b.shape)],
        out_specs=[rows(D_MODEL), pl.BlockSpec((D_MODEL, tm), lambda i: (0, i)), rows(D_MODEL), rows(1)],
        out_shape=[S((T, D_MODEL), _MXU_DTYPE), S((D_MODEL, T), _MXU_DTYPE), S((T, D_MODEL), F32), S((T, 1), F32)],
        compiler_params=_params("parallel"))(a, w, xres, *extra, g, b)


def _proj_ln_bwd(dxo, xhat, rstd, g, a, w, *, name, da_dtype=F32, tm=512):
    T, K = a.shape
    tm = min(tm, T)

    def body(dxo_ref, xh_ref, rs_ref, g_ref, a_ref, w_ref, dres_ref, da_ref, dw_ref, dg_ref, db_ref):
        @pl.when(pl.program_id(0) == 0)
        def _():
            for r in (dw_ref, dg_ref, db_ref):
                r[...] = jnp.zeros_like(r)

        dz, dg, db = _ln_bwd_math(dxo_ref[...], xh_ref[...], rs_ref[...], g_ref[...])
        dg_ref[...] += dg
        db_ref[...] += db
        dres_ref[...] = ALPHA * dz
        dzb = _mx(dz)
        da_ref[...] = _dot(dzb, w_ref[...], NT).astype(da_dtype)
        dw_ref[...] += _dot(a_ref[...], dzb, TN)

    rows = lambda n: pl.BlockSpec((tm, n), lambda i: (i, 0))
    full = lambda shp: pl.BlockSpec(shp, lambda i: (0,) * len(shp))
    return pl.pallas_call(
        body, name=name, grid=(T // tm,),
        in_specs=[rows(D_MODEL), rows(D_MODEL), rows(1), full(g.shape), rows(K), full(w.shape)],
        out_specs=[rows(D_MODEL), rows(K), full(w.shape), full(g.shape), full(g.shape)],
        out_shape=[S((T, D_MODEL), F32), S((T, K), da_dtype), S(w.shape, F32), S(g.shape, F32), S(g.shape, F32)],
        compiler_params=_params("arbitrary"))(dxo, xhat, rstd, g, a, w)


MLP_FWD_CHUNKS = 2


def _mlp_fwd(xb, xh_in, g_in, b_in, w_up, w_dn, layer, g, b, target=None, *, tm=1024):
    T, D = xb.shape
    with_loss = target is not None
    tm = min(tm // 2 if with_loss else tm, T)
    nj, _, _, fc = w_up.shape
    cps = MLP_FWD_CHUNKS
    ns = nj // cps

    def body(xb_ref, xh_ref_in, gi_ref, bi_ref, wu_ref, wd_ref, g_ref, b_ref, *rest):
        if with_loss:
            t_ref, u_ref, ut_ref, xh_ref, rs_ref, loss_ref, dy_ref, acc = rest
        else:
            u_ref, ut_ref, xh_ref, rs_ref, xob_ref, acc = rest
        i, j = pl.program_id(0), pl.program_id(1)

        @pl.when(j == 0)
        def _():
            acc[...] = ALPHA * (xh_ref_in[...] * gi_ref[...] + bi_ref[...])

        if with_loss:
            @pl.when((i == 0) & (j == 0))
            def _():
                loss_ref[...] = jnp.zeros_like(loss_ref)

        xb_v = xb_ref[...]
        for c in range(cps):
            u = _dot(xb_v, wu_ref[c])
            u_ref[:, fc * c:fc * (c + 1)] = _mx(u)
            ut_ref[fc * c:fc * (c + 1), :] = _mx(u.T)
            r = jnp.maximum(u, 0.0)
            acc[...] += _dot(_mx(r * r), wd_ref[c])

        @pl.when(j == ns - 1)
        def _():
            xo, xhat, rstd = _ln_fwd_math(acc[...], g_ref[...], b_ref[...])
            xh_ref[...] = xhat
            rs_ref[...] = rstd
            if with_loss:
                d = xo - t_ref[...]
                dy_ref[...] = d * (1.0 / D)
                loss_ref[...] += (0.5 / D) * jnp.sum(jnp.sum(d * d, axis=1, keepdims=True), axis=0, keepdims=True)
            else:
                xob_ref[...] = _mx(xo)

    rows = lambda n: pl.BlockSpec((tm, n), lambda i, j: (i, 0))
    full = lambda shp: pl.BlockSpec(shp, lambda i, j: (0,) * len(shp))
    in_specs = [rows(D), rows(D), full(g_in.shape), full(b_in.shape),
                pl.BlockSpec((cps, None, D, fc), lambda i, j: (j, layer, 0, 0)),
                pl.BlockSpec((cps, None, fc, D), lambda i, j: (j, layer, 0, 0)), full(g.shape), full(b.shape)]
    out_specs = [pl.BlockSpec((tm, cps * fc), lambda i, j: (i, j)), pl.BlockSpec((cps * fc, tm), lambda i, j: (j, i)),
                 rows(D), rows(1)]
    out_shape = [S((T, nj * fc), _MXU_DTYPE), S((nj * fc, T), _MXU_DTYPE), S((T, D), F32), S((T, 1), F32)]
    args = [xb, xh_in, g_in, b_in, w_up, w_dn, g, b]
    if with_loss:
        in_specs.append(rows(D))
        args.append(target)
        out_specs += [pl.BlockSpec((1, 1), lambda i, j: (0, 0)), rows(D)]
        out_shape += [S((1, 1), F32), S((T, D), F32)]
    else:
        out_specs.append(rows(D))
        out_shape.append(S((T, D), _MXU_DTYPE))
    return pl.pallas_call(
        body, name=f"mlp_fwd_{layer}", grid=(T // tm, ns), in_specs=in_specs, out_specs=out_specs, out_shape=out_shape,
        scratch_shapes=[pltpu.VMEM((tm, D), F32)],
        compiler_params=_params("arbitrary", "arbitrary"))(*args)


def _mlp_bwd_dx(dxo, xhat, rstd, g, u, w_up, w_dn, layer, *, tm=1024):
    T = dxo.shape[0]
    tm = min(tm, T)
    nj, _, _, fc = w_up.shape

    def body(dxo_ref, xh_ref, rs_ref, g_ref, u_ref, wu_ref, wd_ref, dx_ref, du_ref, dyb_ref, dg_ref, db_ref, acc, dy_sc):
        i, j = pl.program_id(0), pl.program_id(1)

        @pl.when((i == 0) & (j == 0))
        def _():
            dg_ref[...] = jnp.zeros_like(dg_ref)
            db_ref[...] = jnp.zeros_like(db_ref)

        @pl.when(j == 0)
        def _():
            dz, dg, db = _ln_bwd_math(dxo_ref[...], xh_ref[...], rs_ref[...], g_ref[...])
            dg_ref[...] += dg
            db_ref[...] += db
            acc[...] = ALPHA * dz
            dy_sc[...] = _mx(dz)
            dyb_ref[...] = _mx(dz)

        r = jnp.maximum(u_ref[...].astype(F32), 0.0)
        da = _dot(dy_sc[...], wd_ref[...], NT)
        dub = _mx(da * (2.0 * r))
        du_ref[...] = dub
        acc[...] += _dot(dub, wu_ref[...], NT)

        @pl.when(j == nj - 1)
        def _():
            dx_ref[...] = acc[...]

    rows = lambda n: pl.BlockSpec((tm, n), lambda i, j: (i, 0))
    full = lambda shp: pl.BlockSpec(shp, lambda i, j: (0,) * len(shp))
    return pl.pallas_call(
        body, name=f"mlp_bwd_dx_{layer}", grid=(T // tm, nj),
        in_specs=[rows(D_MODEL), rows(D_MODEL), rows(1), full(g.shape), pl.BlockSpec((tm, fc), lambda i, j: (i, j)),
                  pl.BlockSpec((None, None, D_MODEL, fc), lambda i, j: (j, layer, 0, 0)),
                  pl.BlockSpec((None, None, fc, D_MODEL), lambda i, j: (j, layer, 0, 0))],
        out_specs=[rows(D_MODEL), pl.BlockSpec((tm, fc), lambda i, j: (i, j)), rows(D_MODEL), full(g.shape), full(g.shape)],
        out_shape=[S((T, D_MODEL), F32), S((T, nj * fc), _MXU_DTYPE), S((T, D_MODEL), _MXU_DTYPE),
                   S(g.shape, F32), S(g.shape, F32)],
        scratch_shapes=[pltpu.VMEM((tm, D_MODEL), F32), pltpu.VMEM((tm, D_MODEL), _MXU_DTYPE)],
        compiler_params=_params("arbitrary", "arbitrary"))(dxo, xhat, rstd, g, u, w_up, w_dn)


def _mlp_bwd_dw(ut, dyb, xt, du, layer, *, nj, other_layers=None, tm=1024):
    T = ut.shape[1]
    tm = min(tm, T)
    fc = ut.shape[0] // nj
    ni = T // tm

    def body(ut_ref, dy_ref, xt_ref, du_ref, *rest):
        gd_ref, gu_ref, gd_acc, gu_acc, sem = rest[-5:]
        i, j = pl.program_id(0), pl.program_id(1)

        @pl.when(i == 0)
        def _():
            gd_acc[j] = jnp.zeros((fc, D_MODEL), F32)
            gu_acc[j] = jnp.zeros((D_MODEL, fc), F32)

        r = jnp.maximum(ut_ref[...].astype(F32), 0.0)
        gd_acc[j] += _dot(_mx(r * r), dy_ref[...])
        gu_acc[j] += _dot(xt_ref[...], du_ref[...])

        @pl.when((i == ni - 1) & (j == nj - 1))
        def _():
            c1 = pltpu.make_async_copy(gd_acc, gd_ref.at[:, layer], sem.at[0])
            c2 = pltpu.make_async_copy(gu_acc, gu_ref.at[:, layer], sem.at[1])
            c1.start()
            c2.start()
            c1.wait()
            c2.wait()

    hbm = pl.BlockSpec(memory_space=pl.ANY)
    in_specs = [pl.BlockSpec((fc, tm), lambda i, j: (j, i)), pl.BlockSpec((tm, D_MODEL), lambda i, j: (i, 0)),
                pl.BlockSpec((D_MODEL, tm), lambda i, j: (0, i)), pl.BlockSpec((tm, fc), lambda i, j: (i, j))]
    args, aliases = [ut, dyb, xt, du], {}
    if other_layers is not None:
        in_specs += [hbm] * 2
        args += list(other_layers)
        aliases = {4: 0, 5: 1}
    return pl.pallas_call(
        body, name=f"mlp_bwd_dw_{layer}", grid=(ni, nj), in_specs=in_specs, out_specs=[hbm, hbm],
        out_shape=[S((nj, DEPTH, fc, D_MODEL), F32), S((nj, DEPTH, D_MODEL, fc), F32)],
        scratch_shapes=[pltpu.VMEM((nj, fc, D_MODEL), F32), pltpu.VMEM((nj, D_MODEL, fc), F32),
                        pltpu.SemaphoreType.DMA((2,))],
        input_output_aliases=aliases,
        compiler_params=_params("arbitrary", "arbitrary"))(*args)


SWA_GROUP = SWA_QH // SWA_KVH
SWA_ROWS = SWA_GROUP * SWA_BLOCK


def _swa_heads(a, kh):
    return jnp.concatenate([a[:, SWA_D * (kh * SWA_GROUP + g):SWA_D * (kh * SWA_GROUP + g + 1)]
                            for g in range(SWA_GROUP)], axis=0)


def _swa_operands(q, kvp, kvc, kh):
    dkv = SWA_KVH * SWA_D
    qg = _swa_heads(q, kh)
    kb = jnp.concatenate([kvp[:, SWA_D * kh:SWA_D * (kh + 1)], kvc[:, SWA_D * kh:SWA_D * (kh + 1)]], axis=0)
    vb = jnp.concatenate([kvp[:, dkv + SWA_D * kh:dkv + SWA_D * (kh + 1)],
                          kvc[:, dkv + SWA_D * kh:dkv + SWA_D * (kh + 1)]], axis=0)
    return qg, kb, vb, _dot(kb, qg, NT)


def _swa_softmax(s_raw, bias_ref, sink_ref, n, kh):
    cols = slice(kh * SWA_ROWS, (kh + 1) * SWA_ROWS)
    s = s_raw * (SWA_D ** -0.5) + bias_ref[jnp.minimum(n, 1), :, cols]
    sink = sink_ref[:, cols]
    m = jnp.maximum(jnp.max(s, axis=0, keepdims=True), sink)
    p, ps = jnp.exp(s - m), jnp.exp(sink - m)
    inv = 1.0 / (jnp.sum(p, axis=0, keepdims=True) + ps)
    return p * inv, ps * inv


def _swa_attn_fwd(qkv, bias, sinks):
    T = qkv.shape[0]
    blk = SWA_BLOCK
    nb = T // blk
    dq, dkv = SWA_QH * SWA_D, SWA_KVH * SWA_D

    def body(q_ref, kvp_ref, kvc_ref, bias_ref, sink_ref, o_ref):
        n = pl.program_id(0)
        q, kvp, kvc = q_ref[...], kvp_ref[...], kvc_ref[...]
        nxt = _swa_operands(q, kvp, kvc, 0)
        for kh in range(SWA_KVH):
            _, _, vb, s_raw = nxt
            if kh + 1 < SWA_KVH:
                nxt = _swa_operands(q, kvp, kvc, kh + 1)
            p, _ = _swa_softmax(s_raw, bias_ref, sink_ref, n, kh)
            og = _mx(_dot(_mx(p), vb, TN))
            for g in range(SWA_GROUP):
                hd = kh * SWA_GROUP + g
                o_ref[:, SWA_D * hd:SWA_D * (hd + 1)] = og[blk * g:blk * (g + 1), :]

    return pl.pallas_call(
        body, name="swa_attn_fwd", grid=(nb,),
        in_specs=[pl.BlockSpec((blk, dq), lambda n: (n, 0)),
                  pl.BlockSpec((blk, 2 * dkv), lambda n: (jnp.maximum(n - 1, 0), dq // (2 * dkv))),
                  pl.BlockSpec((blk, 2 * dkv), lambda n: (n, dq // (2 * dkv))),
                  pl.BlockSpec(bias.shape, lambda n: (0, 0, 0)), pl.BlockSpec(sinks.shape, lambda n: (0, 0))],
        out_specs=pl.BlockSpec((blk, dq), lambda n: (n, 0)), out_shape=S((T, dq), _MXU_DTYPE),
        compiler_params=_params("parallel"))(qkv, qkv, qkv, bias, sinks)


def _swa_attn_bwd(qkv, ob, do, bias, sinks):
    T = qkv.shape[0]
    blk = SWA_BLOCK
    nb = T // blk
    dq, dkv = SWA_QH * SWA_D, SWA_KVH * SWA_D

    def body(q_ref, kvp_ref, kvc_ref, o_ref, do_ref, bias_ref, sink_ref, dqkv_ref, dbias_ref, dsink_ref, carry):
        st = pl.program_id(0)
        n = nb - 1 - st

        @pl.when(st == 0)
        def _():
            carry[...] = jnp.zeros_like(carry)
            dbias_ref[...] = jnp.zeros_like(dbias_ref)
            dsink_ref[...] = jnp.zeros_like(dsink_ref)

        q, kvp, kvc = q_ref[...], kvp_ref[...], kvc_ref[...]
        ov, dov = o_ref[...], do_ref[...]
        ones = jnp.ones((8, SWA_D), F32)
        def operands(kh):
            qg, kb, vb, s_raw = _swa_operands(q, kvp, kvc, kh)
            dog = _swa_heads(dov, kh)
            dl = _dot(ones, dog * _swa_heads(ov, kh).astype(F32), NT, lax.Precision.HIGHEST)[0:1]
            dogb = _mx(dog)
            return qg, kb, s_raw, dl, dogb, _dot(vb, dogb, NT)

        nxt = operands(0)
        for kh in range(SWA_KVH):
            cols = slice(kh * SWA_ROWS, (kh + 1) * SWA_ROWS)
            qg, kb, s_raw, dl, dogb, dp = nxt
            if kh + 1 < SWA_KVH:
                nxt = operands(kh + 1)
            p, ps = _swa_softmax(s_raw, bias_ref, sink_ref, n, kh)
            ds = p * (dp - dl)
            dbias_ref[:, cols] += ds
            dsink_ref[0:1, cols] += -ps * dl
            dsb = _mx(ds * (SWA_D ** -0.5))
            dqg = _mx(_dot(dsb, kb, TN))
            for g in range(SWA_GROUP):
                hd = kh * SWA_GROUP + g
                dqkv_ref[:, SWA_D * hd:SWA_D * (hd + 1)] = dqg[blk * g:blk * (g + 1), :]
            dkb = _dot(dsb, qg)
            dvb = _dot(_mx(p), dogb)
            ko, vo = SWA_D * kh, dkv + SWA_D * kh
            dqkv_ref[:, dq + ko:dq + ko + SWA_D] = _mx(dkb[blk:, :] + carry[:, ko:ko + SWA_D])
            dqkv_ref[:, dq + vo:dq + vo + SWA_D] = _mx(dvb[blk:, :] + carry[:, vo:vo + SWA_D])
            carry[:, ko:ko + SWA_D] = dkb[:blk, :]
            carry[:, vo:vo + SWA_D] = dvb[:blk, :]

    rev = lambda s: nb - 1 - s
    return pl.pallas_call(
        body, name="swa_attn_bwd", grid=(nb,),
        in_specs=[pl.BlockSpec((blk, dq), lambda s: (rev(s), 0)),
                  pl.BlockSpec((blk, 2 * dkv), lambda s: (jnp.maximum(rev(s) - 1, 0), dq // (2 * dkv))),
                  pl.BlockSpec((blk, 2 * dkv), lambda s: (rev(s), dq // (2 * dkv))),
                  pl.BlockSpec((blk, dq), lambda s: (rev(s), 0)), pl.BlockSpec((blk, dq), lambda s: (rev(s), 0)),
                  pl.BlockSpec(bias.shape, lambda s: (0, 0, 0)), pl.BlockSpec(sinks.shape, lambda s: (0, 0))],
        out_specs=[pl.BlockSpec((blk, dq + 2 * dkv), lambda s: (rev(s), 0)),
                   pl.BlockSpec(bias.shape[1:], lambda s: (0, 0)), pl.BlockSpec((8, sinks.shape[1]), lambda s: (0, 0))],
        out_shape=[S((T, dq + 2 * dkv), _MXU_DTYPE), S(bias.shape[1:], F32), S((8, sinks.shape[1]), F32)],
        scratch_shapes=[pltpu.VMEM((blk, 2 * dkv), F32)],
        compiler_params=_params("arbitrary"))(qkv, qkv, qkv, ob, do, bias, sinks)


def _t5_onehot():
    i = jnp.arange(SWA_BLOCK)
    j = jnp.arange(2 * SWA_BLOCK)
    n = jnp.maximum(i[:, None] + SWA_BLOCK - j[None, :], 0)
    max_exact = REL_BUCKETS // 2
    nf = jnp.maximum(n, 1).astype(F32)
    large = max_exact + (jnp.log(nf / max_exact) / math.log(REL_MAX_DIST / max_exact)
                         * (REL_BUCKETS - max_exact)).astype(jnp.int32)
    large = jnp.minimum(large, REL_BUCKETS - 1)
    bucket = jnp.where(n < max_exact, n, large).reshape(-1)
    return (bucket[None, :] == jnp.arange(REL_BUCKETS)[:, None]).astype(F32)


def _exchange_copies(ins, outs, n_gather, send_sems, recv_sems, loc_sems):
    mx, my, mc = lax.axis_index("x"), lax.axis_index("y"), lax.axis_index("c")
    me = 4 * mx + 2 * my + mc
    copies = []
    for a in range(len(ins)):
        src = ins[a] if a < n_gather else ins[a].at[me]
        copies.append(pltpu.make_async_copy(src, outs[a].at[me], loc_sems.at[a]))
    for k in range(1, N_DEV):
        px, py, pc = mx ^ ((k >> 2) & 1), my ^ ((k >> 1) & 1), mc ^ (k & 1)
        peer = 4 * px + 2 * py + pc
        for a in range(len(ins)):
            src = ins[a] if a < n_gather else ins[a].at[peer]
            copies.append(pltpu.make_async_remote_copy(
                src_ref=src, dst_ref=outs[a].at[me], send_sem=send_sems.at[a, k - 1],
                recv_sem=recv_sems.at[a, k - 1], device_id=(px, py, pc), device_id_type=pl.DeviceIdType.MESH))
    return copies


def _exchange_shapes(gather, scatter):
    n_arr = len(gather) + len(scatter)
    out_shape = [S((N_DEV,) + tuple(g.shape), g.dtype) for g in gather] + [S(s.shape, s.dtype) for s in scatter]
    sems = [pltpu.SemaphoreType.DMA((n_arr, N_DEV - 1)), pltpu.SemaphoreType.DMA((n_arr, N_DEV - 1)),
            pltpu.SemaphoreType.DMA((n_arr,))]
    return out_shape, sems


def _exchange(gather, scatter, *, name):
    n_g = len(gather)
    n_arr = n_g + len(scatter)

    def body(*refs):
        copies = _exchange_copies(refs[:n_arr], refs[n_arr:2 * n_arr], n_g, *refs[2 * n_arr:])
        for cp in copies:
            cp.start()
        for cp in copies:
            cp.wait()

    hbm = pl.BlockSpec(memory_space=pl.ANY)
    out_shape, sems = _exchange_shapes(gather, scatter)
    return pl.pallas_call(
        body, name=name, in_specs=[hbm] * n_arr, out_specs=[hbm] * n_arr, out_shape=out_shape,
        scratch_shapes=sems)(*gather, *scatter)


def _adamw(parts, w, m, v, *, name, tr=256):
    R, C = w.shape
    tr = min(tr, R)
    assert R % tr == 0

    def body(p_ref, w_ref, m_ref, v_ref, g_ref, d_ref, nm_ref, nv_ref):
        g = p_ref[0].astype(F32)
        for k in range(1, N_DEV):
            g = g + p_ref[k].astype(F32)
        g_ref[...] = g
        d_ref[...], nm_ref[...], nv_ref[...] = _adamw_math(g, w_ref[...], m_ref[...], v_ref[...])

    rows = pl.BlockSpec((tr, C), lambda i: (i, 0))
    return pl.pallas_call(
        body, name=name, grid=(R // tr,),
        in_specs=[pl.BlockSpec((N_DEV, tr, C), lambda i: (0, i, 0)), rows, rows, rows],
        out_specs=[rows] * 4, out_shape=[S((R, C), F32)] * 4,
        compiler_params=_params("parallel"))(parts, w, m, v)


def _adamw_math(g, w, m, v):
    m_new = ADAM_B1 * m + (1.0 - ADAM_B1) * g
    v_new = ADAM_B2 * v + (1.0 - ADAM_B2) * (g * g)
    m_hat = m_new / (1.0 - ADAM_B1 ** ADAM_STEP)
    v_hat = v_new / (1.0 - ADAM_B2 ** ADAM_STEP)
    return -ADAM_LR * (m_hat / (jnp.sqrt(v_hat) + ADAM_EPS) + ADAM_WD * w), m_new, v_new


SMALL_ROWS = 48
REPL = {"ln_mix_g": (slice(0, 2), slice(None)), "ln_mix_b": (slice(2, 4), slice(None)),
        "ln_mlp_g": (slice(4, 6), slice(None)), "ln_mlp_b": (slice(6, 8), slice(None)),
        "swa_sinks": (slice(8, 9), slice(0, SWA_QH)), "rel_bias": (slice(16, 16 + REL_BUCKETS), slice(0, SWA_QH))}
GAINS = {"mla_g_q": (slice(0, 1), slice(0, MLA_QR // N_DEV)),
         "mla_g_kv": (slice(0, 1), slice(MLA_QR // N_DEV, (MLA_QR + MLA_C) // N_DEV))}


LOSS_AT = (slice(8, 9), slice(SWA_QH, SWA_QH + 1))


def _adamw_small(r_all, p_gains, W, M, V):
    names = list(REPL) + list(GAINS)

    def body(r_ref, pg_ref, *refs):
        ins, outs, loss_ref = refs[:3 * len(names)], refs[3 * len(names):-1], refs[-1]
        r_sum, g_sum = r_ref[0], pg_ref[0]
        for k in range(1, N_DEV):
            r_sum, g_sum = r_sum + r_ref[k], g_sum + pg_ref[k]
        loss_ref[...] = r_sum[LOSS_AT]
        for i, n in enumerate(names):
            g = r_sum[REPL[n]] if n in REPL else g_sum[GAINS[n]]
            w_ref, m_ref, v_ref = ins[3 * i:3 * i + 3]
            g_ref, d_ref, nm_ref, nv_ref = outs[4 * i:4 * i + 4]
            g_ref[...] = g
            d_ref[...], nm_ref[...], nv_ref[...] = _adamw_math(g, w_ref[...], m_ref[...], v_ref[...])

    flat_in = [d[n] for n in names for d in (W, M, V)]
    res = pl.pallas_call(body, name="adamw_small",
                         out_shape=[S(W[n].shape, F32) for n in names for _ in range(4)] + [S((1, 1), F32)],
                         compiler_params=_params())(r_all, p_gains, *flat_in)
    return {(k, n): res[4 * i + k] for i, n in enumerate(names) for k in range(4)}, res[-1]


WEIGHTS = ["mla_w_in", "mla_g_q", "mla_g_kv", "mla_w_uq", "mla_w_uk", "mla_w_uv", "mla_w_o", "kv_w_shared",
           "swa_w_q", "swa_sinks", "swa_w_o", "rel_bias", "mlp_w_up", "mlp_w_down", "ln_mix_g", "ln_mix_b",
           "ln_mlp_g", "ln_mlp_b"]


def kernel(x, mla_w_in, mla_g_q, mla_g_kv, mla_w_uq, mla_w_uk, mla_w_uv, mla_w_o, kv_w_shared, swa_w_q, swa_sinks, swa_w_o, rel_bias, mlp_w_up, mlp_w_down, ln_mix_g, ln_mix_b, ln_mlp_g, ln_mlp_b, loss_target, m_mla_w_in, m_mla_g_q, m_mla_g_kv, m_mla_w_uq, m_mla_w_uk, m_mla_w_uv, m_mla_w_o, m_kv_w_shared, m_swa_w_q, m_swa_sinks, m_swa_w_o, m_rel_bias, m_mlp_w_up, m_mlp_w_down, m_ln_mix_g, m_ln_mix_b, m_ln_mlp_g, m_ln_mlp_b, v_mla_w_in, v_mla_g_q, v_mla_g_kv, v_mla_w_uq, v_mla_w_uk, v_mla_w_uv, v_mla_w_o, v_kv_w_shared, v_swa_w_q, v_swa_sinks, v_swa_w_o, v_rel_bias, v_mlp_w_up, v_mlp_w_down, v_ln_mix_g, v_ln_mix_b, v_ln_mlp_g, v_ln_mlp_b):
    args = dict(locals())
    W = {n: args[n] for n in WEIGHTS}
    M = {n: args["m_" + n] for n in WEIGHTS}
    V = {n: args["v_" + n] for n in WEIGHTS}
    T = x.shape[1]
    x2d = x.reshape(T, D_MODEL)
    tgt = loss_target.reshape(T, D_MODEL)
    H = MLA_HEADS

    SH = {"mla_w_in": (-1, mla_w_in.shape[-1]), "mla_w_uq": (-1, H * (MLA_NOPE + MLA_ROPE)),
          "mla_w_uk": (-1, H * MLA_NOPE), "mla_w_uv": (-1, H * MLA_V), "mla_w_o": (-1, D_MODEL),
          "kv_w_shared": (-1, kv_w_shared.shape[-1]), "swa_w_q": (-1, swa_w_q.shape[-1]), "swa_w_o": (-1, D_MODEL)}
    slab = lambda d, n: d[n].reshape(SH[n])
    bf = lambda a: a.astype(_MXU_DTYPE)
    gains_slab = lambda d: jnp.pad(jnp.concatenate([d["mla_g_q"], d["mla_g_kv"]], axis=1),
                                   ((0, 7), (0, 128 - d["mla_g_q"].shape[1] - d["mla_g_kv"].shape[1])))
    n_gq, n_gkv = mla_g_q.shape[1], mla_g_kv.shape[1]
    w_in_s, w_uq_s, w_uk_s, gains_all = _exchange(
        [bf(slab(W, "mla_w_in")), bf(slab(W, "mla_w_uq")), bf(slab(W, "mla_w_uk")), gains_slab(W)], [],
        name="gather_mla_in")
    later = [bf(slab(W, n)) for n in ("mla_w_uv", "mla_w_o", "kv_w_shared", "swa_w_q", "swa_w_o")]
    later += [bf(mlp_w_up), bf(mlp_w_down)]
    w_in = w_in_s.reshape(D_MODEL, -1)
    g_q = gains_all[:, 0, :n_gq].reshape(1, MLA_QR)
    g_kv = gains_all[:, 0, n_gq:n_gq + n_gkv].reshape(1, MLA_C)
    w_uq = w_uq_s.reshape(MLA_QR, H, MLA_NOPE + MLA_ROPE)
    w_uq_n = w_uq[:, :, :MLA_NOPE].reshape(MLA_QR, H * MLA_NOPE)
    w_uq_r = w_uq[:, :, MLA_NOPE:].reshape(MLA_QR, H * MLA_ROPE)
    w_uk = w_uk_s.reshape(MLA_C, H, MLA_NOPE).transpose(1, 0, 2)
    w_uk_t = w_uk.transpose(0, 2, 1)
    ln = lambda a, l: a[l].reshape(1, D_MODEL)

    half = MLA_ROPE // 2
    inv = ROPE_THETA ** (-jnp.arange(half, dtype=F32) / half)
    ang = jnp.arange(T, dtype=F32)[:, None] * inv[None, :]
    cos = jnp.tile(jnp.concatenate([jnp.cos(ang), jnp.cos(ang)], -1), (1, ROPE_TABLE_W // MLA_ROPE))
    sin = jnp.tile(jnp.concatenate([-jnp.sin(ang), jnp.sin(ang)], -1), (1, ROPE_TABLE_W // MLA_ROPE))

    h, kc, kct, qs, qst, cq, qn = _mla_pre_fwd(x2d, w_in, g_q, g_kv, w_uq_n, w_uq_r, w_uk_t, cos, sin)
    olat, lse, (w_uv_s, w_o_s, w_kv_s, w_q_s, w_o2_s, w_up, w_dn) = _mla_attn_fwd(qs, kc, kct, gather=later)
    w_uv = w_uv_s.reshape(MLA_C, H, MLA_V).transpose(1, 0, 2)
    w_o = w_o_s.reshape(H * MLA_V, D_MODEL)
    w_qkv = jnp.concatenate([w_q_s.reshape(D_MODEL, -1), w_kv_s.reshape(D_MODEL, -1)], axis=1)
    w_o2 = w_o2_s.reshape(SWA_QH * SWA_D, D_MODEL)
    o_mla = _mla_uv_fwd(olat, w_uv)
    mix0, mlp0 = (ln(ln_mix_g, 0), ln(ln_mix_b, 0)), (ln(ln_mlp_g, 0), ln(ln_mlp_b, 0))
    mix1, mlp1 = (ln(ln_mix_g, 1), ln(ln_mix_b, 1)), (ln(ln_mlp_g, 1), ln(ln_mlp_b, 1))
    x1b, x1t, xh1, rs1 = _proj_ln_fwd(o_mla, w_o, x2d, None, *mix0, name="mla_out_ln_fwd")
    u0, u0t, xh2, rs2, x2b = _mlp_fwd(x1b, xh1, *mix0, w_up, w_dn, 0, *mlp0)
    onehot = _t5_onehot()
    bias = _mm(rel_bias.T, onehot, name="rel_bias_expand", precision=lax.Precision.HIGHEST, tn=8192).reshape(
        SWA_QH * SWA_BLOCK, 2 * SWA_BLOCK).T
    key = jnp.arange(2 * SWA_BLOCK)[:, None]
    qry = jnp.arange(SWA_QH * SWA_BLOCK)[None, :] % SWA_BLOCK
    in_window = (key > qry) & (key <= qry + SWA_BLOCK)
    bias = jnp.stack([jnp.where(in_window & (key >= SWA_BLOCK), bias, -jnp.inf), jnp.where(in_window, bias, -jnp.inf)])
    sink_rows = jnp.repeat(swa_sinks.reshape(SWA_QH), SWA_BLOCK).reshape(1, SWA_QH * SWA_BLOCK)
    qkv = _mm(x2b, w_qkv, name="swa_qkv_fwd", out_dtype=_MXU_DTYPE, tm=1024, tn=512, tk=1024)
    o_swa = _swa_attn_fwd(qkv, bias, sink_rows)
    x3b, x3t, xh3, rs3 = _proj_ln_fwd(o_swa, w_o2, xh2, mlp0, *mix1, name="swa_out_ln_fwd")
    u1, u1t, xh4, rs4, loss_part, dx4 = _mlp_fwd(x3b, xh3, *mix1, w_up, w_dn, 1, *mlp1, tgt)

    nj = w_up.shape[0]
    dx3, du1, dy4b, dg_mlp1, db_mlp1 = _mlp_bwd_dx(dx4, xh4, rs4, ln(ln_mlp_g, 1), u1, w_up, w_dn, 1)
    g_dn_last, g_up_last = _mlp_bwd_dw(u1t, dy4b, x3t, du1, 1, nj=nj)
    dres3, do_swa, g_o2, dg_mix1, db_mix1 = _proj_ln_bwd(dx3, xh3, rs3, ln(ln_mix_g, 1), o_swa, w_o2,
                                                         name="swa_out_ln_bwd")
    dqkv, dbias, dsink = _swa_attn_bwd(qkv, o_swa, do_swa, bias, sink_rows)
    g_rel = _mm(onehot, dbias.T.reshape(SWA_QH, -1), name="rel_bias_grad", tb=True, precision=lax.Precision.HIGHEST,
                tk=8192)
    head_of_row = (jnp.arange(SWA_QH * SWA_BLOCK)[:, None] // SWA_BLOCK == jnp.arange(SWA_QH)[None, :]).astype(F32)
    g_sinks = _mm(dsink, head_of_row, name="sinks_grad", precision=lax.Precision.HIGHEST, tk=2048)[0:1]
    dx2 = _mm(dqkv, w_qkv, name="swa_qkv_bwd_dx", tb=True, add=dres3, tm=1024, tn=1024, tk=1536)
    g_qkv = _mm(x2b, dqkv, name="swa_qkv_bwd_dw", ta=True, tm=1024, tn=512, tk=1024)
    dx1, du0, dy2b, dg_mlp0, db_mlp0 = _mlp_bwd_dx(dx2, xh2, rs2, ln(ln_mlp_g, 0), u0, w_up, w_dn, 0)
    g_dn, g_up = _mlp_bwd_dw(u0t, dy2b, x1t, du0, 0, nj=nj, other_layers=(g_dn_last, g_up_last))
    dres1, do_mla, g_o, dg_mix0, db_mix0 = _proj_ln_bwd(dx1, xh1, rs1, ln(ln_mix_g, 0), o_mla, w_o,
                                                        name="mla_out_ln_bwd", da_dtype=_MXU_DTYPE)
    dol, delta, g_uv = _mla_uv_bwd(do_mla, olat, w_uv)
    wide = lambda a, rows: jnp.pad(a, ((0, rows - a.shape[0]), (0, LANES - a.shape[1])))
    r_part = jnp.concatenate([dg_mix0, dg_mix1, db_mix0, db_mix1, dg_mlp0, dg_mlp1, db_mlp0, db_mlp1,
                              wide(jnp.concatenate([g_sinks, loss_part], axis=1), 8), wide(g_rel, SMALL_ROWS - 16)],
                             axis=0)
    by_dev = lambda g: g.reshape((N_DEV, g.shape[0] // N_DEV) + g.shape[1:])
    early = [by_dev(g_o2), by_dev(g_qkv), g_up, g_dn, by_dev(g_o),
             by_dev(g_uv.transpose(1, 0, 2).reshape(MLA_C, H * MLA_V))]
    dqs, dkc, dv, (r_all, p_o2, p_qkv, p_up, p_dn, p_o, p_uv) = _mla_attn_bwd(
        qs, qst, kc, kct, dol, lse, delta, gather=[r_part], scatter=early)
    grad_x, g_in, g_uq_n, g_uq_r, g_uk, g_gq, g_gkv = _mla_pre_bwd(
        dqs, dkc, dv, h, x2d, dres1, cq, qn, cos, sin, w_in, g_q, g_kv, w_uq_n, w_uq_r, w_uk)
    g_uq = jnp.concatenate([g_uq_n.reshape(MLA_QR, H, MLA_NOPE), g_uq_r.reshape(MLA_QR, H, MLA_ROPE)], -1)
    g_gains = jnp.pad(jnp.concatenate([g_gq.reshape(N_DEV, n_gq), g_gkv.reshape(N_DEV, n_gkv)], axis=1)[:, None, :],
                      ((0, 0), (0, 7), (0, 128 - n_gq - n_gkv)))
    p_in, p_uq, p_uk, p_gains = _exchange(
        [], [bf(by_dev(g_in)), bf(by_dev(g_uq.reshape(MLA_QR, -1))),
             bf(by_dev(g_uk.transpose(1, 0, 2).reshape(MLA_C, -1))), g_gains], name="exchange_mla_in_grads")

    res = {}

    def adam(name, parts, names, to_slab, from_slab):
        out = _adamw(parts, to_slab(W), to_slab(M), to_slab(V), name="adamw_" + name)
        for k in range(4):
            for n, a in zip(names, from_slab(out[k])):
                res[(k, n)] = a.reshape(W[n].shape)

    one = lambda n: (lambda d: slab(d, n))
    adam("swa_w_o", p_o2, ["swa_w_o"], one("swa_w_o"), lambda s: [s])
    dq_cols = SWA_QH * SWA_D
    adam("swa_qkv", p_qkv, ["swa_w_q", "kv_w_shared"],
         lambda d: jnp.concatenate([slab(d, "swa_w_q"), slab(d, "kv_w_shared")], axis=1),
         lambda s: [s[:, :dq_cols], s[:, dq_cols:]])
    layers_as_rows = lambda a: a.reshape((-1,) + a.shape[-1:])
    adam("mlp_w_up", p_up.reshape(N_DEV, -1, p_up.shape[-1]), ["mlp_w_up"],
         lambda d: layers_as_rows(d["mlp_w_up"]), lambda s: [s])
    adam("mlp_w_down", p_dn.reshape(N_DEV, -1, p_dn.shape[-1]), ["mlp_w_down"],
         lambda d: layers_as_rows(d["mlp_w_down"]), lambda s: [s])
    adam("mla_w_o", p_o, ["mla_w_o"], one("mla_w_o"), lambda s: [s])
    adam("mla_w_uv", p_uv, ["mla_w_uv"], one("mla_w_uv"), lambda s: [s])
    adam("mla_w_in", p_in, ["mla_w_in"], one("mla_w_in"), lambda s: [s])
    adam("mla_w_uq", p_uq, ["mla_w_uq"], one("mla_w_uq"), lambda s: [s])
    adam("mla_w_uk", p_uk, ["mla_w_uk"], one("mla_w_uk"), lambda s: [s])
    small, loss = _adamw_small(r_all, p_gains, W, M, V)
    res.update(small)
    loss = loss.reshape(())
    return (loss, grad_x.reshape(x.shape), *[res[(k, n)] for k in range(4) for n in WEIGHTS])
```

```python
import math

import numpy as np
import jax
import jax.numpy as jnp
from jax import lax
from jax.experimental import pallas as pl
from jax.experimental.pallas import tpu as pltpu

F32 = jnp.float32
_MXU_DTYPE = jnp.bfloat16

D_MODEL = 1024
DEPTH = 2
MLA_HEADS = 8
MLA_NOPE = 128
MLA_ROPE = 64
MLA_V = 128
MLA_QR = 384
MLA_C = 256
MLA_DK = 384
MLA_DT = MLA_C + MLA_ROPE
ROPE_THETA = 10000.0
SWA_QH = 16
SWA_KVH = 4
SWA_D = 64
SWA_BLOCK = 128
REL_BUCKETS = 32
REL_MAX_DIST = 128
D_FF = 4096
LN_EPS = 1e-5
RMS_EPS = 1e-6
ALPHA = (2 * DEPTH) ** 0.25
ADAM_LR, ADAM_B1, ADAM_B2, ADAM_EPS, ADAM_WD, ADAM_STEP = 0.001, 0.9, 0.999, 1e-08, 0.01, 10

N_DEV = 8
AXES = ("x", "y", "c")
V7X_VMEM_BYTES = 64 * 1024 * 1024
VMEM_LIMIT = V7X_VMEM_BYTES - 8 * 1024 * 1024
LANES = 1024
ATT_TQ = 512
ATT_TK = 512
ATT_HEAD_GROUP = 1
ATT_FWD_HEAD_GROUP = 2

NT = (((1,), (1,)), ((), ()))
TN = (((0,), (0,)), ((), ()))
S = jax.ShapeDtypeStruct


def _params(*sem, vmem=VMEM_LIMIT):
    return pltpu.CompilerParams(dimension_semantics=sem, vmem_limit_bytes=vmem)


def _dot(a, b, dims=None, precision=None):
    if dims is None:
        return jnp.dot(a, b, preferred_element_type=F32, precision=precision)
    return lax.dot_general(a, b, dims, preferred_element_type=F32, precision=precision)


def _mx(v):
    return v.astype(_MXU_DTYPE)


ROPE_TABLE_W = 128


def _tile_heads(t):
    return jnp.concatenate([t] * (MLA_HEADS * MLA_ROPE // ROPE_TABLE_W), axis=1)


def _swap_halves_64(v):
    return jnp.concatenate([v[:, 32:], v[:, :32]], axis=-1)


def _swap_halves_groups(v):
    n = v.shape[-1]
    lane = lax.broadcasted_iota(jnp.int32, v.shape, 1)
    return jnp.where(lane % 64 < 32, pltpu.roll(v, n - 32, 1), pltpu.roll(v, 32, 1))


def _mm(a, b, *, name, ta=False, tb=False, add=None, out_dtype=F32, tm=512, tn=512, tk=512, precision=None):
    M, K = (a.shape[1], a.shape[0]) if ta else a.shape
    N = b.shape[0] if tb else b.shape[1]
    tm, tn, tk = min(tm, M), min(tn, N), min(tk, K)
    assert M % tm == 0 and N % tn == 0 and K % tk == 0, (M, N, K, tm, tn, tk)
    nk = K // tk
    dims = (((0 if ta else 1,), (1 if tb else 0,)), ((), ()))
    has_add = add is not None

    def body(*refs):
        if has_add:
            a_ref, b_ref, add_ref, o_ref, acc = refs
        else:
            a_ref, b_ref, o_ref, acc = refs
        k = pl.program_id(2)
        av, bv = a_ref[...], b_ref[...]
        if precision is None:
            av, bv = _mx(av), _mx(bv)
        part = _dot(av, bv, dims, precision)
        if nk == 1:
            o_ref[...] = (part + add_ref[...] if has_add else part).astype(out_dtype)
            return

        @pl.when(k == 0)
        def _():
            acc[...] = add_ref[...] if has_add else jnp.zeros_like(acc)

        acc[...] += part

        @pl.when(k == nk - 1)
        def _():
            o_ref[...] = acc[...].astype(out_dtype)

    a_spec = pl.BlockSpec((tk, tm), lambda i, j, k: (k, i)) if ta else pl.BlockSpec((tm, tk), lambda i, j, k: (i, k))
    b_spec = pl.BlockSpec((tn, tk), lambda i, j, k: (j, k)) if tb else pl.BlockSpec((tk, tn), lambda i, j, k: (k, j))
    in_specs = [a_spec, b_spec]
    args = [a, b]
    if has_add:
        in_specs.append(pl.BlockSpec((tm, tn), lambda i, j, k: (i, j)))
        args.append(add)
    return pl.pallas_call(
        body, name=name, grid=(M // tm, N // tn, nk), in_specs=in_specs,
        out_specs=pl.BlockSpec((tm, tn), lambda i, j, k: (i, j)), out_shape=S((M, N), out_dtype),
        scratch_shapes=[pltpu.VMEM((tm, tn), F32)],
        compiler_params=_params("parallel", "parallel", "arbitrary"))(*args)


def _ln_fwd_math(z, g, b):
    mu = jnp.mean(z, axis=-1, keepdims=True)
    zc = z - mu
    var = jnp.mean(zc * zc, axis=-1, keepdims=True)
    rstd = lax.rsqrt(var + LN_EPS)
    xhat = zc * rstd
    return xhat * g + b, xhat, rstd


def _ln_bwd_math(dxo, xhat, rstd, g):
    dxh = dxo * g
    m1 = jnp.mean(dxh, axis=-1, keepdims=True)
    m2 = jnp.mean(dxh * xhat, axis=-1, keepdims=True)
    dz = rstd * (dxh - m1 - xhat * m2)
    dg = jnp.sum(dxo * xhat, axis=0, keepdims=True)
    db = jnp.sum(dxo, axis=0, keepdims=True)
    return dz, dg, db


def _rms_fwd_math(xr, g):
    r = lax.rsqrt(jnp.mean(xr * xr, axis=-1, keepdims=True) + RMS_EPS)
    return xr * r * g


def _rms_bwd_math(dy, xr, g):
    r = lax.rsqrt(jnp.mean(xr * xr, axis=-1, keepdims=True) + RMS_EPS)
    gy = dy * g
    dx = r * gy - xr * (r * r * r) * jnp.mean(gy * xr, axis=-1, keepdims=True)
    dg = jnp.sum(dy * xr * r, axis=0, keepdims=True)
    return dx, dg


def _mla_pre_fwd(x, w_in, g_q, g_kv, w_uq_n, w_uq_r, w_uk_t, cos, sin):
    T = x.shape[0]
    tm = min(ATT_TQ, T)
    nq = T // tm
    H = MLA_HEADS

    tk = min(ATT_TK, T)

    def body(x_ref, win_ref, gq_ref, gkv_ref, wn_ref, wr_ref, wuk_ref, cos_ref, sin_ref,
             h_ref, kc_ref, kct_ref, qs_ref, qst_ref, cq_ref, qn_ref):
        h = _dot(_mx(x_ref[...]), win_ref[...])
        h_ref[...] = h
        cos_v, sin_v = _tile_heads(cos_ref[...]), _tile_heads(sin_ref[...])
        cq = _mx(_rms_fwd_math(h[:, :MLA_QR], gq_ref[...]))
        ckv = _rms_fwd_math(h[:, MLA_QR:MLA_QR + MLA_C], gkv_ref[...])
        krr = h[:, MLA_QR + MLA_C:]
        kr = krr * cos_v[:, :MLA_ROPE] + _swap_halves_64(krr) * sin_v[:, :MLA_ROPE]
        kr_pad = jnp.concatenate([kr, jnp.zeros((tm, MLA_DK - MLA_C - MLA_ROPE), F32)], axis=1)
        kc_ref[:, 0:MLA_C] = _mx(ckv)
        kc_ref[:, MLA_C:] = _mx(kr_pad)
        kct_ref[0:MLA_C, :] = _mx(ckv.T)
        kct_ref[MLA_C:, :] = _mx(kr_pad.T[0:MLA_ROPE, :])
        cq_ref[...] = cq
        qnb = _mx(_dot(cq, wn_ref[...]))
        qn_ref[...] = qnb
        qr = _dot(cq, wr_ref[...])
        qrr = qr * cos_v + _swap_halves_groups(qr) * sin_v
        qrr_t = qrr.T
        for hd in range(H):
            ql = _dot(qnb[:, MLA_NOPE * hd:MLA_NOPE * (hd + 1)], wuk_ref[hd])
            qst_ref[0, 0:MLA_C, tm * hd:tm * (hd + 1)] = _mx(ql.T)
            qst_ref[0, MLA_C:, tm * hd:tm * (hd + 1)] = _mx(qrr_t[MLA_ROPE * hd:MLA_ROPE * (hd + 1), :])
            qs_ref[0, hd, :, 0:MLA_C] = _mx(ql)
            qs_ref[0, hd, :, MLA_C:MLA_C + MLA_ROPE] = _mx(qrr[:, MLA_ROPE * hd:MLA_ROPE * (hd + 1)])
            qs_ref[0, hd, :, MLA_C + MLA_ROPE:] = jnp.zeros((tm, MLA_DK - MLA_C - MLA_ROPE), _MXU_DTYPE)

    full = lambda shp: pl.BlockSpec(shp, lambda i: (0,) * len(shp))
    rows = lambda n: pl.BlockSpec((tm, n), lambda i: (i, 0))
    n_in = w_in.shape[1]
    return pl.pallas_call(
        body, name="mla_pre_fwd", grid=(nq,),
        in_specs=[rows(D_MODEL), full(w_in.shape), full(g_q.shape), full(g_kv.shape), full(w_uq_n.shape),
                  full(w_uq_r.shape), full(w_uk_t.shape), rows(ROPE_TABLE_W), rows(ROPE_TABLE_W)],
        out_specs=[rows(n_in), rows(MLA_DK),
                   pl.BlockSpec((None, MLA_DT, tm), lambda i: (i * tm // tk, 0, i % (tk // tm))),
                   pl.BlockSpec((1, H, tm, MLA_DK), lambda i: (i, 0, 0, 0)),
                   pl.BlockSpec((1, MLA_DT, H * tm), lambda i: (i, 0, 0)), rows(MLA_QR), rows(H * MLA_NOPE)],
        out_shape=[S((T, n_in), F32), S((T, MLA_DK), _MXU_DTYPE), S((T // tk, MLA_DT, tk), _MXU_DTYPE),
                   S((nq, H, tm, MLA_DK), _MXU_DTYPE), S((nq, MLA_DT, H * tm), _MXU_DTYPE),
                   S((T, MLA_QR), _MXU_DTYPE), S((T, H * MLA_NOPE), _MXU_DTYPE)],
        compiler_params=_params("parallel"))(x, w_in, g_q, g_kv, w_uq_n, w_uq_r, w_uk_t, cos, sin)


def _att_steps(T, tq, tk):
    qi, kj = [], []
    for i in range(T // tq):
        for j in range((i * tq + tq - 1) // tk + 1):
            qi.append(i)
            kj.append(j)
    return jnp.asarray(np.array(qi, np.int32)), jnp.asarray(np.array(kj, np.int32))


def _ride_exchange(st, n_steps, ins, outs, n_gather, sems):
    if not ins:
        return

    @pl.when(st == 0)
    def _():
        for cp in _exchange_copies(ins, outs, n_gather, *sems):
            cp.start()

    @pl.when(st == n_steps - 1)
    def _():
        for cp in _exchange_copies(ins, outs, n_gather, *sems):
            cp.wait()


def _mla_attn_fwd(qs, kc, kct, gather=(), scatter=()):
    nq, H, tq, DK = qs.shape
    T = kc.shape[0]
    tk = min(ATT_TK, T)
    scale = (MLA_NOPE + MLA_ROPE) ** -0.5
    c2 = scale * math.log2(math.e)
    qi, kj = _att_steps(T, tq, tk)
    n_steps = int(qi.shape[0])
    hg = ATT_FWD_HEAD_GROUP
    R = hg * tq
    n_x = len(gather) + len(scatter)

    def body(qi_ref, kj_ref, q_ref, k_ref, kt_ref, *rest):
        x_ins, (o_ref, lse_ref), x_outs = rest[:n_x], rest[n_x:n_x + 2], rest[n_x + 2:2 * n_x + 2]
        m_sc, l_sc, acc_sc = rest[2 * n_x + 2:2 * n_x + 5]
        st = pl.program_id(0)
        _ride_exchange(st, n_steps, x_ins, x_outs, len(gather), rest[2 * n_x + 5:])
        i, j = qi_ref[st], kj_ref[st]
        j_last = (i * tq + tq - 1) // tk

        @pl.when(j == 0)
        def _():
            m_sc[...] = jnp.full_like(m_sc, -jnp.inf)
            l_sc[...] = jnp.zeros_like(l_sc)
            acc_sc[...] = jnp.zeros_like(acc_sc)

        def step(masked):
            k = k_ref[...]
            vt = kt_ref[0:MLA_C, :]
            if masked:
                key = lax.broadcasted_iota(jnp.int32, (tk, R), 0) + j * tk
                qry = lax.broadcasted_iota(jnp.int32, (tk, R), 1) % tq + i * tq
                causal = key <= qry
            n_g = H // hg
            qk = lambda g: _dot(k, q_ref[0, g * hg:(g + 1) * hg].reshape(R, DK), NT)
            def accumulate(g, a, pb):
                cs = slice(g * R, (g + 1) * R)
                acc_sc[:, cs] = a * acc_sc[:, cs] + _dot(vt, pb)

            s_next = qk(0)
            pending = None
            for g in range(n_g):
                cs = slice(g * R, (g + 1) * R)
                s = s_next
                if g + 1 < n_g:
                    s_next = qk(g + 1)
                if pending is not None:
                    accumulate(*pending)
                if masked:
                    s = jnp.where(causal, s, -jnp.inf)
                m_prev = m_sc[:, cs]
                m_new = jnp.maximum(m_prev, jnp.max(s, axis=0, keepdims=True))
                a = jnp.exp2((m_prev - m_new) * c2)
                p = jnp.exp2((s - m_new) * c2)
                l_sc[:, cs] = a * l_sc[:, cs] + jnp.sum(p, axis=0, keepdims=True)
                m_sc[:, cs] = m_new
                pending = (g, a, _mx(p))
            accumulate(*pending)

        pl.when(j == j_last)(lambda: step(True))
        pl.when(j != j_last)(lambda: step(False))

        @pl.when(j == j_last)
        def _():
            o_ref[0] = _mx(acc_sc[...] / l_sc[...])
            lse_ref[0] = m_sc[...] * scale + jnp.log(l_sc[...])

    hbm = pl.BlockSpec(memory_space=pl.ANY)
    x_shapes, x_sems = _exchange_shapes(gather, scatter) if n_x else ([], [])
    gs = pltpu.PrefetchScalarGridSpec(
        num_scalar_prefetch=2, grid=(n_steps,),
        in_specs=[pl.BlockSpec((1, H, tq, DK), lambda s, qi, kj: (qi[s], 0, 0, 0)),
                  pl.BlockSpec((tk, DK), lambda s, qi, kj: (kj[s], 0)),
                  pl.BlockSpec((None, MLA_DT, tk), lambda s, qi, kj: (kj[s], 0, 0))] + [hbm] * n_x,
        out_specs=[pl.BlockSpec((1, MLA_C, H * tq), lambda s, qi, kj: (qi[s], 0, 0)),
                   pl.BlockSpec((1, 1, H * tq), lambda s, qi, kj: (qi[s], 0, 0))] + [hbm] * n_x,
        scratch_shapes=[pltpu.VMEM((1, H * tq), F32), pltpu.VMEM((1, H * tq), F32),
                        pltpu.VMEM((MLA_C, H * tq), F32)] + x_sems)
    res = pl.pallas_call(
        body, name="mla_attn_fwd", grid_spec=gs,
        out_shape=[S((nq, MLA_C, H * tq), _MXU_DTYPE), S((nq, 1, H * tq), F32)] + x_shapes,
        compiler_params=_params("arbitrary"))(qi, kj, qs, kc, kct, *gather, *scatter)
    return res[0], res[1], res[2:]


def _mla_attn_bwd(qs, qst, kc, kct, dol, lse, delta, gather=(), scatter=()):
    nq, H, tq, DK = qs.shape
    T = kc.shape[0]
    tk = min(ATT_TK, T)
    scale = (MLA_NOPE + MLA_ROPE) ** -0.5
    log2e = math.log2(math.e)
    qi, kj = _att_steps(T, tq, tk)
    n_steps = int(qi.shape[0])
    hg = ATT_HEAD_GROUP
    R = hg * tq
    n_x = len(gather) + len(scatter)

    def body(qi_ref, kj_ref, q_ref, qt_ref, k_ref, kt_ref, do_ref, lse_ref, dl_ref, *rest):
        x_ins, (dq_ref, dk_ref, dv_ref), x_outs = rest[:n_x], rest[n_x:n_x + 3], rest[n_x + 3:2 * n_x + 3]
        dk_acc, dv_acc, sem = rest[2 * n_x + 3:2 * n_x + 6]
        st = pl.program_id(0)
        _ride_exchange(st, n_steps, x_ins, x_outs, len(gather), rest[2 * n_x + 6:])
        i, j = qi_ref[st], kj_ref[st]
        j_last = (i * tq + tq - 1) // tk

        @pl.when(st == 0)
        def _():
            dk_acc[...] = jnp.zeros_like(dk_acc)
            dv_acc[...] = jnp.zeros_like(dv_acc)

        @pl.when(j == 0)
        def _():
            dq_ref[...] = jnp.zeros_like(dq_ref)

        def step(masked):
            k, kt = k_ref[...], kt_ref[...]
            v = k[:, :MLA_C]
            if masked:
                key = lax.broadcasted_iota(jnp.int32, (tk, R), 0) + j * tk
                qry = lax.broadcasted_iota(jnp.int32, (tk, R), 1) % tq + i * tq
                causal = key <= qry
            dkt_c = jnp.zeros((MLA_DT, tk), F32)
            dvt_c = jnp.zeros((MLA_C, tk), F32)
            n_g = H // hg

            def scores(g):
                q = q_ref[0, g * hg:(g + 1) * hg].reshape(R, DK)
                dot = do_ref[0, :, g * R:(g + 1) * R]
                return dot, _dot(k, q, NT), _dot(v, dot)

            nxt = scores(0)
            for g in range(n_g):
                cs = slice(g * R, (g + 1) * R)
                dot, s, dp = nxt
                if g + 1 < n_g:
                    nxt = scores(g + 1)
                p = jnp.exp2(s * (scale * log2e) - lse_ref[0, :, cs] * log2e)
                if masked:
                    p = jnp.where(causal, p, 0.0)
                dsb = _mx(p * (dp - dl_ref[0, :, cs]))
                dq_ref[0, :, cs] += _dot(kt, dsb)
                dkt_c = dkt_c + _dot(qt_ref[0, :, cs], dsb, NT)
                dvt_c = dvt_c + _dot(dot, _mx(p), NT)
            dk_acc[j] += dkt_c * scale
            dv_acc[j] += dvt_c

        pl.when(j == j_last)(lambda: step(True))
        pl.when(j != j_last)(lambda: step(False))

        @pl.when(j == j_last)
        def _():
            dq_ref[...] = dq_ref[...] * scale

        @pl.when(st == n_steps - 1)
        def _():
            c1 = pltpu.make_async_copy(dk_acc, dk_ref, sem.at[0])
            c2 = pltpu.make_async_copy(dv_acc, dv_ref, sem.at[1])
            c1.start()
            c2.start()
            c1.wait()
            c2.wait()

    cols = lambda n: pl.BlockSpec((1, n, H * tq), lambda s, qi, kj: (qi[s], 0, 0))
    hbm = pl.BlockSpec(memory_space=pl.ANY)
    x_shapes, x_sems = _exchange_shapes(gather, scatter) if n_x else ([], [])
    gs = pltpu.PrefetchScalarGridSpec(
        num_scalar_prefetch=2, grid=(n_steps,),
        in_specs=[pl.BlockSpec((1, H, tq, DK), lambda s, qi, kj: (qi[s], 0, 0, 0)), cols(MLA_DT),
                  pl.BlockSpec((tk, DK), lambda s, qi, kj: (kj[s], 0)),
                  pl.BlockSpec((None, MLA_DT, tk), lambda s, qi, kj: (kj[s], 0, 0)),
                  cols(MLA_C), cols(1), cols(1)] + [hbm] * n_x,
        out_specs=[cols(MLA_DT), hbm, hbm] + [hbm] * n_x,
        scratch_shapes=[pltpu.VMEM((T // tk, MLA_DT, tk), F32), pltpu.VMEM((T // tk, MLA_C, tk), F32),
                        pltpu.SemaphoreType.DMA((2,))] + x_sems)
    res = pl.pallas_call(
        body, name="mla_attn_bwd", grid_spec=gs,
        out_shape=[S((nq, MLA_DT, H * tq), F32), S((T // tk, MLA_DT, tk), F32),
                   S((T // tk, MLA_C, tk), F32)] + x_shapes,
        compiler_params=_params("arbitrary"))(qi, kj, qs, qst, kc, kct, dol, lse, delta, *gather, *scatter)
    return res[0], res[1], res[2], res[3:]


def _mla_uv_fwd(olat, w_uv):
    nq, C, cols = olat.shape
    H = w_uv.shape[0]
    tq = cols // H

    def body(ol_ref, wuv_ref, o_ref):
        for hd in range(H):
            o_ref[:, MLA_V * hd:MLA_V * (hd + 1)] = _mx(_dot(ol_ref[0, :, tq * hd:tq * (hd + 1)], wuv_ref[hd], TN))

    return pl.pallas_call(
        body, name="mla_uv_fwd", grid=(nq,),
        in_specs=[pl.BlockSpec((1, C, cols), lambda i: (i, 0, 0)), pl.BlockSpec(w_uv.shape, lambda i: (0, 0, 0))],
        out_specs=pl.BlockSpec((tq, H * MLA_V), lambda i: (i, 0)), out_shape=S((nq * tq, H * MLA_V), _MXU_DTYPE),
        compiler_params=_params("parallel"))(olat, w_uv)


def _mla_uv_bwd(do, olat, w_uv):
    nq, C, cols = olat.shape
    H = w_uv.shape[0]
    tq = cols // H

    def body(do_ref, ol_ref, wuv_ref, dol_ref, dl_ref, dw_ref):
        @pl.when(pl.program_id(0) == 0)
        def _():
            dw_ref[...] = jnp.zeros_like(dw_ref)

        dov = do_ref[...]
        for hd in range(H):
            cs = slice(tq * hd, tq * (hd + 1))
            doh = _mx(dov[:, MLA_V * hd:MLA_V * (hd + 1)])
            ol = ol_ref[0, :, cs]
            dol = _dot(wuv_ref[hd], doh, NT)
            dol_ref[0, :, cs] = _mx(dol)
            dl_ref[0, :, cs] = jnp.sum(dol * ol.astype(F32), axis=0, keepdims=True)
            dw_ref[hd] += _dot(ol, doh)

    blk = lambda n: pl.BlockSpec((1, n, cols), lambda i: (i, 0, 0))
    return pl.pallas_call(
        body, name="mla_uv_bwd", grid=(nq,),
        in_specs=[pl.BlockSpec((tq, H * MLA_V), lambda i: (i, 0)), blk(C), pl.BlockSpec(w_uv.shape, lambda i: (0, 0, 0))],
        out_specs=[blk(C), blk(1), pl.BlockSpec(w_uv.shape, lambda i: (0, 0, 0))],
        out_shape=[S(olat.shape, _MXU_DTYPE), S((nq, 1, cols), F32), S(w_uv.shape, F32)],
        compiler_params=_params("arbitrary"))(do, olat, w_uv)


def _mla_pre_bwd(dqs, dkc, dv, h, x, dres, cq, qn, cos, sin, w_in, g_q, g_kv, w_uq_n, w_uq_r, w_uk):
    nq, DK, cols = dqs.shape
    H = w_uk.shape[0]
    tm = cols // H
    T = nq * tm
    tk = dv.shape[2]
    n_in = w_in.shape[1]

    def body(dqs_ref, dkc_ref, dv_ref, h_ref, x_ref, dres_ref, cq_ref, qn_ref, cos_ref, sin_ref,
             win_ref, gq_ref, gkv_ref, wn_ref, wr_ref, wuk_ref,
             gx_ref, dwin_ref, dwn_ref, dwr_ref, dwuk_ref, dgq_ref, dgkv_ref, dqn_sc, dqr_sc, dh_sc):
        @pl.when(pl.program_id(0) == 0)
        def _():
            for r in (dwin_ref, dwn_ref, dwr_ref, dwuk_ref, dgq_ref, dgkv_ref):
                r[...] = jnp.zeros_like(r)

        cos_v, sin_v = _tile_heads(cos_ref[...]), _tile_heads(sin_ref[...])
        qnb = qn_ref[...]
        for hd in range(H):
            cs = slice(tm * hd, tm * (hd + 1))
            dql = _mx(dqs_ref[0, 0:MLA_C, cs])
            dqn_sc[:, MLA_NOPE * hd:MLA_NOPE * (hd + 1)] = _dot(dql, wuk_ref[hd], TN)
            dwuk_ref[hd] += _dot(dql, qnb[:, MLA_NOPE * hd:MLA_NOPE * (hd + 1)])
            dqr_sc[MLA_ROPE * hd:MLA_ROPE * (hd + 1), :] = dqs_ref[0, MLA_C:MLA_C + MLA_ROPE, cs]
        dqr = dqr_sc[...].T
        dqrb = _mx(dqr * cos_v + _swap_halves_groups(dqr * sin_v))
        dqnb = _mx(dqn_sc[...])
        cq = cq_ref[...]
        dwn_ref[...] += _dot(cq, dqnb, TN)
        dwr_ref[...] += _dot(cq, dqrb, TN)
        dcq = _dot(dqnb, wn_ref[...], NT) + _dot(dqrb, wr_ref[...], NT)
        hv = h_ref[...]
        dxq, dgq = _rms_bwd_math(dcq, hv[:, :MLA_QR], gq_ref[...])
        dgq_ref[...] += dgq
        dckv = (dkc_ref[0:MLA_C, :] + dv_ref[...]).T
        dxkv, dgkv = _rms_bwd_math(dckv, hv[:, MLA_QR:MLA_QR + MLA_C], gkv_ref[...])
        dgkv_ref[...] += dgkv
        dkr = jnp.concatenate([dkc_ref[MLA_C:, :], jnp.zeros((128 - MLA_ROPE, tm), F32)], axis=0).T[:, :MLA_ROPE]
        dkr_raw = dkr * cos_v[:, :MLA_ROPE] + _swap_halves_64(dkr * sin_v[:, :MLA_ROPE])
        dh_sc[:, 0:MLA_QR] = dxq
        dh_sc[:, MLA_QR:MLA_QR + MLA_C] = dxkv
        dh_sc[:, MLA_QR + MLA_C:] = dkr_raw
        dhb = _mx(dh_sc[...])
        gx_ref[...] = dres_ref[...] + _dot(dhb, win_ref[...], NT)
        dwin_ref[...] += _dot(_mx(x_ref[...]), dhb, TN)

    full = lambda shp: pl.BlockSpec(shp, lambda i: (0,) * len(shp))
    rows = lambda n: pl.BlockSpec((tm, n), lambda i: (i, 0))
    return pl.pallas_call(
        body, name="mla_pre_bwd", grid=(nq,),
        in_specs=[pl.BlockSpec((1, DK, cols), lambda i: (i, 0, 0)),
                  pl.BlockSpec((None, DK, tm), lambda i: (i * tm // tk, 0, i % (tk // tm))),
                  pl.BlockSpec((None, MLA_C, tm), lambda i: (i * tm // tk, 0, i % (tk // tm))), rows(n_in),
                  rows(D_MODEL), rows(D_MODEL), rows(MLA_QR), rows(H * MLA_NOPE), rows(ROPE_TABLE_W), rows(ROPE_TABLE_W),
                  full(w_in.shape), full(g_q.shape), full(g_kv.shape), full(w_uq_n.shape), full(w_uq_r.shape),
                  full(w_uk.shape)],
        out_specs=[rows(D_MODEL), full(w_in.shape), full(w_uq_n.shape), full(w_uq_r.shape), full(w_uk.shape),
                   full(g_q.shape), full(g_kv.shape)],
        out_shape=[S((T, D_MODEL), F32), S(w_in.shape, F32), S(w_uq_n.shape, F32), S(w_uq_r.shape, F32),
                   S(w_uk.shape, F32), S(g_q.shape, F32), S(g_kv.shape, F32)],
        scratch_shapes=[pltpu.VMEM((tm, H * MLA_NOPE), F32), pltpu.VMEM((H * MLA_ROPE, tm), F32),
                        pltpu.VMEM((tm, n_in), F32)],
        compiler_params=_params("arbitrary"))(dqs, dkc, dv, h, x, dres, cq, qn, cos, sin, w_in, g_q, g_kv,
                                              w_uq_n, w_uq_r, w_uk)


def _proj_ln_fwd(a, w, xres, res_gb, g, b, *, name, tm=512):
    T, K = a.shape
    tm = min(tm, T)
    gp, bp = res_gb if res_gb is not None else (None, None)

    def body(a_ref, w_ref, x_ref, *rest):
        if res_gb is not None:
            x = x_ref[...] * rest[0][...] + rest[1][...]
            rest = rest[2:]
        else:
            x = x_ref[...]
        g_ref, b_ref, xob_ref, xt_ref, xh_ref, rs_ref = rest
        z = ALPHA * x + _dot(a_ref[...], w_ref[...])
        xo, xhat, rstd = _ln_fwd_math(z, g_ref[...], b_ref[...])
        xob_ref[...] = _mx(xo)
        xt_ref[...] = _mx(xo.T)
        xh_ref[...] = xhat
        rs_ref[...] = rstd

    rows = lambda n: pl.BlockSpec((tm, n), lambda i: (i, 0))
    full = lambda shp: pl.BlockSpec(shp, lambda i: (0,) * len(shp))
    extra = [gp, bp] if res_gb is not None else []
    return pl.pallas_call(
        body, name=name, grid=(T // tm,),
        in_specs=[rows(K), full(w.shape), rows(D_MODEL)] + [full(e.shape) for e in extra] + [full(g.shape), full(b.shape)],
        out_specs=[rows(D_MODEL), pl.BlockSpec((D_MODEL, tm), lambda i: (0, i)), rows(D_MODEL), rows(1)],
        out_shape=[S((T, D_MODEL), _MXU_DTYPE), S((D_MODEL, T), _MXU_DTYPE), S((T, D_MODEL), F32), S((T, 1), F32)],
        compiler_params=_params("parallel"))(a, w, xres, *extra, g, b)


def _proj_ln_bwd(dxo, xhat, rstd, g, a, w, *, name, da_dtype=F32, tm=512):
    T, K = a.shape
    tm = min(tm, T)

    def body(dxo_ref, xh_ref, rs_ref, g_ref, a_ref, w_ref, dres_ref, da_ref, dw_ref, dg_ref, db_ref):
        @pl.when(pl.program_id(0) == 0)
        def _():
            for r in (dw_ref, dg_ref, db_ref):
                r[...] = jnp.zeros_like(r)

        dz, dg, db = _ln_bwd_math(dxo_ref[...], xh_ref[...], rs_ref[...], g_ref[...])
        dg_ref[...] += dg
        db_ref[...] += db
        dres_ref[...] = ALPHA * dz
        dzb = _mx(dz)
        da_ref[...] = _dot(dzb, w_ref[...], NT).astype(da_dtype)
        dw_ref[...] += _dot(a_ref[...], dzb, TN)

    rows = lambda n: pl.BlockSpec((tm, n), lambda i: (i, 0))
    full = lambda shp: pl.BlockSpec(shp, lambda i: (0,) * len(shp))
    return pl.pallas_call(
        body, name=name, grid=(T // tm,),
        in_specs=[rows(D_MODEL), rows(D_MODEL), rows(1), full(g.shape), rows(K), full(w.shape)],
        out_specs=[rows(D_MODEL), rows(K), full(w.shape), full(g.shape), full(g.shape)],
        out_shape=[S((T, D_MODEL), F32), S((T, K), da_dtype), S(w.shape, F32), S(g.shape, F32), S(g.shape, F32)],
        compiler_params=_params("arbitrary"))(dxo, xhat, rstd, g, a, w)


MLP_FWD_CHUNKS = 2


def _mlp_fwd(xb, xh_in, g_in, b_in, w_up, w_dn, layer, g, b, target=None, *, tm=1024):
    T, D = xb.shape
    with_loss = target is not None
    tm = min(tm // 2 if with_loss else tm, T)
    nj, _, _, fc = w_up.shape
    cps = MLP_FWD_CHUNKS
    ns = nj // cps

    def body(xb_ref, xh_ref_in, gi_ref, bi_ref, wu_ref, wd_ref, g_ref, b_ref, *rest):
        if with_loss:
            t_ref, u_ref, ut_ref, xh_ref, rs_ref, loss_ref, dy_ref, acc = rest
        else:
            u_ref, ut_ref, xh_ref, rs_ref, xob_ref, acc = rest
        i, j = pl.program_id(0), pl.program_id(1)

        @pl.when(j == 0)
        def _():
            acc[...] = ALPHA * (xh_ref_in[...] * gi_ref[...] + bi_ref[...])

        if with_loss:
            @pl.when((i == 0) & (j == 0))
            def _():
                loss_ref[...] = jnp.zeros_like(loss_ref)

        xb_v = xb_ref[...]
        for c in range(cps):
            u = _dot(xb_v, wu_ref[c])
            u_ref[:, fc * c:fc * (c + 1)] = _mx(u)
            ut_ref[fc * c:fc * (c + 1), :] = _mx(u.T)
            r = jnp.maximum(u, 0.0)
            acc[...] += _dot(_mx(r * r), wd_ref[c])

        @pl.when(j == ns - 1)
        def _():
            xo, xhat, rstd = _ln_fwd_math(acc[...], g_ref[...], b_ref[...])
            xh_ref[...] = xhat
            rs_ref[...] = rstd
            if with_loss:
                d = xo - t_ref[...]
                dy_ref[...] = d * (1.0 / D)
                loss_ref[...] += (0.5 / D) * jnp.sum(jnp.sum(d * d, axis=1, keepdims=True), axis=0, keepdims=True)
            else:
                xob_ref[...] = _mx(xo)

    rows = lambda n: pl.BlockSpec((tm, n), lambda i, j: (i, 0))
    full = lambda shp: pl.BlockSpec(shp, lambda i, j: (0,) * len(shp))
    in_specs = [rows(D), rows(D), full(g_in.shape), full(b_in.shape),
                pl.BlockSpec((cps, None, D, fc), lambda i, j: (j, layer, 0, 0)),
                pl.BlockSpec((cps, None, fc, D), lambda i, j: (j, layer, 0, 0)), full(g.shape), full(b.shape)]
    out_specs = [pl.BlockSpec((tm, cps * fc), lambda i, j: (i, j)), pl.BlockSpec((cps * fc, tm), lambda i, j: (j, i)),
                 rows(D), rows(1)]
    out_shape = [S((T, nj * fc), _MXU_DTYPE), S((nj * fc, T), _MXU_DTYPE), S((T, D), F32), S((T, 1), F32)]
    args = [xb, xh_in, g_in, b_in, w_up, w_dn, g, b]
    if with_loss:
        in_specs.append(rows(D))
        args.append(target)
        out_specs += [pl.BlockSpec((1, 1), lambda i, j: (0, 0)), rows(D)]
        out_shape += [S((1, 1), F32), S((T, D), F32)]
    else:
        out_specs.append(rows(D))
        out_shape.append(S((T, D), _MXU_DTYPE))
    return pl.pallas_call(
        body, name=f"mlp_fwd_{layer}", grid=(T // tm, ns), in_specs=in_specs, out_specs=out_specs, out_shape=out_shape,
        scratch_shapes=[pltpu.VMEM((tm, D), F32)],
        compiler_params=_params("arbitrary", "arbitrary"))(*args)


def _mlp_bwd_dx(dxo, xhat, rstd, g, u, w_up, w_dn, layer, *, tm=1024):
    T = dxo.shape[0]
    tm = min(tm, T)
    nj, _, _, fc = w_up.shape

    def body(dxo_ref, xh_ref, rs_ref, g_ref, u_ref, wu_ref, wd_ref, dx_ref, du_ref, dyb_ref, dg_ref, db_ref, acc, dy_sc):
        i, j = pl.program_id(0), pl.program_id(1)

        @pl.when((i == 0) & (j == 0))
        def _():
            dg_ref[...] = jnp.zeros_like(dg_ref)
            db_ref[...] = jnp.zeros_like(db_ref)

        @pl.when(j == 0)
        def _():
            dz, dg, db = _ln_bwd_math(dxo_ref[...], xh_ref[...], rs_ref[...], g_ref[...])
            dg_ref[...] += dg
            db_ref[...] += db
            acc[...] = ALPHA * dz
            dy_sc[...] = _mx(dz)
            dyb_ref[...] = _mx(dz)

        r = jnp.maximum(u_ref[...].astype(F32), 0.0)
        da = _dot(dy_sc[...], wd_ref[...], NT)
        dub = _mx(da * (2.0 * r))
        du_ref[...] = dub
        acc[...] += _dot(dub, wu_ref[...], NT)

        @pl.when(j == nj - 1)
        def _():
            dx_ref[...] = acc[...]

    rows = lambda n: pl.BlockSpec((tm, n), lambda i, j: (i, 0))
    full = lambda shp: pl.BlockSpec(shp, lambda i, j: (0,) * len(shp))
    return pl.pallas_call(
        body, name=f"mlp_bwd_dx_{layer}", grid=(T // tm, nj),
        in_specs=[rows(D_MODEL), rows(D_MODEL), rows(1), full(g.shape), pl.BlockSpec((tm, fc), lambda i, j: (i, j)),
                  pl.BlockSpec((None, None, D_MODEL, fc), lambda i, j: (j, layer, 0, 0)),
                  pl.BlockSpec((None, None, fc, D_MODEL), lambda i, j: (j, layer, 0, 0))],
        out_specs=[rows(D_MODEL), pl.BlockSpec((tm, fc), lambda i, j: (i, j)), rows(D_MODEL), full(g.shape), full(g.shape)],
        out_shape=[S((T, D_MODEL), F32), S((T, nj * fc), _MXU_DTYPE), S((T, D_MODEL), _MXU_DTYPE),
                   S(g.shape, F32), S(g.shape, F32)],
        scratch_shapes=[pltpu.VMEM((tm, D_MODEL), F32), pltpu.VMEM((tm, D_MODEL), _MXU_DTYPE)],
        compiler_params=_params("arbitrary", "arbitrary"))(dxo, xhat, rstd, g, u, w_up, w_dn)


def _mlp_bwd_dw(ut, dyb, xt, du, layer, *, nj, other_layers=None, tm=1024):
    T = ut.shape[1]
    tm = min(tm, T)
    fc = ut.shape[0] // nj
    ni = T // tm

    def body(ut_ref, dy_ref, xt_ref, du_ref, *rest):
        gd_ref, gu_ref, gd_acc, gu_acc, sem = rest[-5:]
        i, j = pl.program_id(0), pl.program_id(1)

        @pl.when(i == 0)
        def _():
            gd_acc[j] = jnp.zeros((fc, D_MODEL), F32)
            gu_acc[j] = jnp.zeros((D_MODEL, fc), F32)

        r = jnp.maximum(ut_ref[...].astype(F32), 0.0)
        gd_acc[j] += _dot(_mx(r * r), dy_ref[...])
        gu_acc[j] += _dot(xt_ref[...], du_ref[...])

        @pl.when((i == ni - 1) & (j == nj - 1))
        def _():
            c1 = pltpu.make_async_copy(gd_acc, gd_ref.at[:, layer], sem.at[0])
            c2 = pltpu.make_async_copy(gu_acc, gu_ref.at[:, layer], sem.at[1])
            c1.start()
            c2.start()
            c1.wait()
            c2.wait()

    hbm = pl.BlockSpec(memory_space=pl.ANY)
    in_specs = [pl.BlockSpec((fc, tm), lambda i, j: (j, i)), pl.BlockSpec((tm, D_MODEL), lambda i, j: (i, 0)),
                pl.BlockSpec((D_MODEL, tm), lambda i, j: (0, i)), pl.BlockSpec((tm, fc), lambda i, j: (i, j))]
    args, aliases = [ut, dyb, xt, du], {}
    if other_layers is not None:
        in_specs += [hbm] * 2
        args += list(other_layers)
        aliases = {4: 0, 5: 1}
    return pl.pallas_call(
        body, name=f"mlp_bwd_dw_{layer}", grid=(ni, nj), in_specs=in_specs, out_specs=[hbm, hbm],
        out_shape=[S((nj, DEPTH, fc, D_MODEL), F32), S((nj, DEPTH, D_MODEL, fc), F32)],
        scratch_shapes=[pltpu.VMEM((nj, fc, D_MODEL), F32), pltpu.VMEM((nj, D_MODEL, fc), F32),
                        pltpu.SemaphoreType.DMA((2,))],
        input_output_aliases=aliases,
        compiler_params=_params("arbitrary", "arbitrary"))(*args)


SWA_GROUP = SWA_QH // SWA_KVH
SWA_ROWS = SWA_GROUP * SWA_BLOCK


def _swa_heads(a, kh):
    return jnp.concatenate([a[:, SWA_D * (kh * SWA_GROUP + g):SWA_D * (kh * SWA_GROUP + g + 1)]
                            for g in range(SWA_GROUP)], axis=0)


def _swa_operands(q, kvp, kvc, kh):
    dkv = SWA_KVH * SWA_D
    qg = _swa_heads(q, kh)
    kb = jnp.concatenate([kvp[:, SWA_D * kh:SWA_D * (kh + 1)], kvc[:, SWA_D * kh:SWA_D * (kh + 1)]], axis=0)
    vb = jnp.concatenate([kvp[:, dkv + SWA_D * kh:dkv + SWA_D * (kh + 1)],
                          kvc[:, dkv + SWA_D * kh:dkv + SWA_D * (kh + 1)]], axis=0)
    return qg, kb, vb, _dot(kb, qg, NT)


def _swa_softmax(s_raw, bias_ref, sink_ref, n, kh):
    cols = slice(kh * SWA_ROWS, (kh + 1) * SWA_ROWS)
    s = s_raw * (SWA_D ** -0.5) + bias_ref[jnp.minimum(n, 1), :, cols]
    sink = sink_ref[:, cols]
    m = jnp.maximum(jnp.max(s, axis=0, keepdims=True), sink)
    p, ps = jnp.exp(s - m), jnp.exp(sink - m)
    inv = 1.0 / (jnp.sum(p, axis=0, keepdims=True) + ps)
    return p * inv, ps * inv


def _swa_attn_fwd(qkv, bias, sinks):
    T = qkv.shape[0]
    blk = SWA_BLOCK
    nb = T // blk
    dq, dkv = SWA_QH * SWA_D, SWA_KVH * SWA_D

    def body(q_ref, kvp_ref, kvc_ref, bias_ref, sink_ref, o_ref):
        n = pl.program_id(0)
        q, kvp, kvc = q_ref[...], kvp_ref[...], kvc_ref[...]
        nxt = _swa_operands(q, kvp, kvc, 0)
        for kh in range(SWA_KVH):
            _, _, vb, s_raw = nxt
            if kh + 1 < SWA_KVH:
                nxt = _swa_operands(q, kvp, kvc, kh + 1)
            p, _ = _swa_softmax(s_raw, bias_ref, sink_ref, n, kh)
            og = _mx(_dot(_mx(p), vb, TN))
            for g in range(SWA_GROUP):
                hd = kh * SWA_GROUP + g
                o_ref[:, SWA_D * hd:SWA_D * (hd + 1)] = og[blk * g:blk * (g + 1), :]

    return pl.pallas_call(
        body, name="swa_attn_fwd", grid=(nb,),
        in_specs=[pl.BlockSpec((blk, dq), lambda n: (n, 0)),
                  pl.BlockSpec((blk, 2 * dkv), lambda n: (jnp.maximum(n - 1, 0), dq // (2 * dkv))),
                  pl.BlockSpec((blk, 2 * dkv), lambda n: (n, dq // (2 * dkv))),
                  pl.BlockSpec(bias.shape, lambda n: (0, 0, 0)), pl.BlockSpec(sinks.shape, lambda n: (0, 0))],
        out_specs=pl.BlockSpec((blk, dq), lambda n: (n, 0)), out_shape=S((T, dq), _MXU_DTYPE),
        compiler_params=_params("parallel"))(qkv, qkv, qkv, bias, sinks)


def _swa_attn_bwd(qkv, ob, do, bias, sinks):
    T = qkv.shape[0]
    blk = SWA_BLOCK
    nb = T // blk
    dq, dkv = SWA_QH * SWA_D, SWA_KVH * SWA_D

    def body(q_ref, kvp_ref, kvc_ref, o_ref, do_ref, bias_ref, sink_ref, dqkv_ref, dbias_ref, dsink_ref, carry):
        st = pl.program_id(0)
        n = nb - 1 - st

        @pl.when(st == 0)
        def _():
            carry[...] = jnp.zeros_like(carry)
            dbias_ref[...] = jnp.zeros_like(dbias_ref)
            dsink_ref[...] = jnp.zeros_like(dsink_ref)

        q, kvp, kvc = q_ref[...], kvp_ref[...], kvc_ref[...]
        ov, dov = o_ref[...], do_ref[...]
        ones = jnp.ones((8, SWA_D), F32)
        def operands(kh):
            qg, kb, vb, s_raw = _swa_operands(q, kvp, kvc, kh)
            dog = _swa_heads(dov, kh)
            dl = _dot(ones, dog * _swa_heads(ov, kh).astype(F32), NT, lax.Precision.HIGHEST)[0:1]
            dogb = _mx(dog)
            return qg, kb, s_raw, dl, dogb, _dot(vb, dogb, NT)

        nxt = operands(0)
        for kh in range(SWA_KVH):
            cols = slice(kh * SWA_ROWS, (kh + 1) * SWA_ROWS)
            qg, kb, s_raw, dl, dogb, dp = nxt
            if kh + 1 < SWA_KVH:
                nxt = operands(kh + 1)
            p, ps = _swa_softmax(s_raw, bias_ref, sink_ref, n, kh)
            ds = p * (dp - dl)
            dbias_ref[:, cols] += ds
            dsink_ref[0:1, cols] += -ps * dl
            dsb = _mx(ds * (SWA_D ** -0.5))
            dqg = _mx(_dot(dsb, kb, TN))
            for g in range(SWA_GROUP):
                hd = kh * SWA_GROUP + g
                dqkv_ref[:, SWA_D * hd:SWA_D * (hd + 1)] = dqg[blk * g:blk * (g + 1), :]
            dkb = _dot(dsb, qg)
            dvb = _dot(_mx(p), dogb)
            ko, vo = SWA_D * kh, dkv + SWA_D * kh
            dqkv_ref[:, dq + ko:dq + ko + SWA_D] = _mx(dkb[blk:, :] + carry[:, ko:ko + SWA_D])
            dqkv_ref[:, dq + vo:dq + vo + SWA_D] = _mx(dvb[blk:, :] + carry[:, vo:vo + SWA_D])
            carry[:, ko:ko + SWA_D] = dkb[:blk, :]
            carry[:, vo:vo + SWA_D] = dvb[:blk, :]

    rev = lambda s: nb - 1 - s
    return pl.pallas_call(
        body, name="swa_attn_bwd", grid=(nb,),
        in_specs=[pl.BlockSpec((blk, dq), lambda s: (rev(s), 0)),
                  pl.BlockSpec((blk, 2 * dkv), lambda s: (jnp.maximum(rev(s) - 1, 0), dq // (2 * dkv))),
                  pl.BlockSpec((blk, 2 * dkv), lambda s: (rev(s), dq // (2 * dkv))),
                  pl.BlockSpec((blk, dq), lambda s: (rev(s), 0)), pl.BlockSpec((blk, dq), lambda s: (rev(s), 0)),
                  pl.BlockSpec(bias.shape, lambda s: (0, 0, 0)), pl.BlockSpec(sinks.shape, lambda s: (0, 0))],
        out_specs=[pl.BlockSpec((blk, dq + 2 * dkv), lambda s: (rev(s), 0)),
                   pl.BlockSpec(bias.shape[1:], lambda s: (0, 0)), pl.BlockSpec((8, sinks.shape[1]), lambda s: (0, 0))],
        out_shape=[S((T, dq + 2 * dkv), _MXU_DTYPE), S(bias.shape[1:], F32), S((8, sinks.shape[1]), F32)],
        scratch_shapes=[pltpu.VMEM((blk, 2 * dkv), F32)],
        compiler_params=_params("arbitrary"))(qkv, qkv, qkv, ob, do, bias, sinks)


def _t5_onehot():
    i = jnp.arange(SWA_BLOCK)
    j = jnp.arange(2 * SWA_BLOCK)
    n = jnp.maximum(i[:, None] + SWA_BLOCK - j[None, :], 0)
    max_exact = REL_BUCKETS // 2
    nf = jnp.maximum(n, 1).astype(F32)
    large = max_exact + (jnp.log(nf / max_exact) / math.log(REL_MAX_DIST / max_exact)
                         * (REL_BUCKETS - max_exact)).astype(jnp.int32)
    large = jnp.minimum(large, REL_BUCKETS - 1)
    bucket = jnp.where(n < max_exact, n, large).reshape(-1)
    return (bucket[None, :] == jnp.arange(REL_BUCKETS)[:, None]).astype(F32)


def _exchange_copies(ins, outs, n_gather, send_sems, recv_sems, loc_sems):
    mx, my, mc = lax.axis_index("x"), lax.axis_index("y"), lax.axis_index("c")
    me = 4 * mx + 2 * my + mc
    copies = []
    for a in range(len(ins)):
        src = ins[a] if a < n_gather else ins[a].at[me]
        copies.append(pltpu.make_async_copy(src, outs[a].at[me], loc_sems.at[a]))
    for k in range(1, N_DEV):
        px, py, pc = mx ^ ((k >> 2) & 1), my ^ ((k >> 1) & 1), mc ^ (k & 1)
        peer = 4 * px + 2 * py + pc
        for a in range(len(ins)):
            src = ins[a] if a < n_gather else ins[a].at[peer]
            copies.append(pltpu.make_async_remote_copy(
                src_ref=src, dst_ref=outs[a].at[me], send_sem=send_sems.at[a, k - 1],
                recv_sem=recv_sems.at[a, k - 1], device_id=(px, py, pc), device_id_type=pl.DeviceIdType.MESH))
    return copies


def _exchange_shapes(gather, scatter):
    n_arr = len(gather) + len(scatter)
    out_shape = [S((N_DEV,) + tuple(g.shape), g.dtype) for g in gather] + [S(s.shape, s.dtype) for s in scatter]
    sems = [pltpu.SemaphoreType.DMA((n_arr, N_DEV - 1)), pltpu.SemaphoreType.DMA((n_arr, N_DEV - 1)),
            pltpu.SemaphoreType.DMA((n_arr,))]
    return out_shape, sems


def _exchange(gather, scatter, *, name):
    n_g = len(gather)
    n_arr = n_g + len(scatter)

    def body(*refs):
        copies = _exchange_copies(refs[:n_arr], refs[n_arr:2 * n_arr], n_g, *refs[2 * n_arr:])
        for cp in copies:
            cp.start()
        for cp in copies:
            cp.wait()

    hbm = pl.BlockSpec(memory_space=pl.ANY)
    out_shape, sems = _exchange_shapes(gather, scatter)
    return pl.pallas_call(
        body, name=name, in_specs=[hbm] * n_arr, out_specs=[hbm] * n_arr, out_shape=out_shape,
        scratch_shapes=sems)(*gather, *scatter)


def _gather_two_level(arrays, *, name):
    n = len(arrays)

    def body(*refs):
        ins, outs = refs[:n], refs[n:2 * n]
        send_sems, recv_sems, loc_sems = refs[2 * n:]
        x, y, c = lax.axis_index("x"), lax.axis_index("y"), lax.axis_index("c")
        me, sibling = (x, y, c), (x, y, 1 - c)
        chips = [(1 - x, y), (x, 1 - y), (1 - x, 1 - y)]

        def copy(a, k, block, to, src=None):
            dst = outs[a].at[4 * block[0] + 2 * block[1] + block[2]]
            return pltpu.make_async_remote_copy(
                src_ref=dst if src is None else src, dst_ref=dst, send_sem=send_sems.at[a, k],
                recv_sem=recv_sems.at[a, k], device_id=to, device_id_type=pl.DeviceIdType.MESH)

        mine = [pltpu.make_async_copy(ins[a], outs[a].at[4 * x + 2 * y + c], loc_sems.at[a]) for a in range(n)]
        first = [copy(a, 0, me, sibling, src=ins[a]) for a in range(n)]
        first += [copy(a, 1 + j, me, (*chip, c), src=ins[a]) for j, chip in enumerate(chips) for a in range(n)]
        for cp in mine + first:
            cp.start()
        passed = []
        for j, chip in enumerate(chips):
            for a in range(n):
                copy(a, 1 + j, (*chip, c), me).wait_recv()
                passed.append(copy(a, 4 + j, (*chip, c), sibling))
                passed[-1].start()
        for a in range(n):
            copy(a, 0, sibling, me).wait_recv()
        for j, chip in enumerate(chips):
            for a in range(n):
                copy(a, 4 + j, (*chip, 1 - c), me).wait_recv()
        for cp in first + passed:
            cp.wait_send()
        for cp in mine:
            cp.wait()

    hbm = pl.BlockSpec(memory_space=pl.ANY)
    out_shape, sems = _exchange_shapes(arrays, [])
    return pl.pallas_call(
        body, name=name, in_specs=[hbm] * n, out_specs=[hbm] * n, out_shape=out_shape, scratch_shapes=sems)(*arrays)


def _adamw(parts, w, m, v, *, name, tr=256):
    R, C = w.shape
    tr = min(tr, R)
    assert R % tr == 0

    def body(p_ref, w_ref, m_ref, v_ref, g_ref, d_ref, nm_ref, nv_ref):
        g = p_ref[0].astype(F32)
        for k in range(1, N_DEV):
            g = g + p_ref[k].astype(F32)
        g_ref[...] = g
        d_ref[...], nm_ref[...], nv_ref[...] = _adamw_math(g, w_ref[...], m_ref[...], v_ref[...])

    rows = pl.BlockSpec((tr, C), lambda i: (i, 0))
    return pl.pallas_call(
        body, name=name, grid=(R // tr,),
        in_specs=[pl.BlockSpec((N_DEV, tr, C), lambda i: (0, i, 0)), rows, rows, rows],
        out_specs=[rows] * 4, out_shape=[S((R, C), F32)] * 4,
        compiler_params=_params("parallel"))(parts, w, m, v)


def _adamw_math(g, w, m, v):
    m_new = ADAM_B1 * m + (1.0 - ADAM_B1) * g
    v_new = ADAM_B2 * v + (1.0 - ADAM_B2) * (g * g)
    m_hat = m_new / (1.0 - ADAM_B1 ** ADAM_STEP)
    v_hat = v_new / (1.0 - ADAM_B2 ** ADAM_STEP)
    return -ADAM_LR * (m_hat / (jnp.sqrt(v_hat) + ADAM_EPS) + ADAM_WD * w), m_new, v_new


SMALL_ROWS = 48
REPL = {"ln_mix_g": (slice(0, 2), slice(None)), "ln_mix_b": (slice(2, 4), slice(None)),
        "ln_mlp_g": (slice(4, 6), slice(None)), "ln_mlp_b": (slice(6, 8), slice(None)),
        "swa_sinks": (slice(8, 9), slice(0, SWA_QH)), "rel_bias": (slice(16, 16 + REL_BUCKETS), slice(0, SWA_QH))}
GAINS = {"mla_g_q": (slice(0, 1), slice(0, MLA_QR // N_DEV)),
         "mla_g_kv": (slice(0, 1), slice(MLA_QR // N_DEV, (MLA_QR + MLA_C) // N_DEV))}


LOSS_AT = (slice(8, 9), slice(SWA_QH, SWA_QH + 1))


def _adamw_small(r_all, p_gains, W, M, V):
    names = list(REPL) + list(GAINS)

    def body(r_ref, pg_ref, *refs):
        ins, outs, loss_ref = refs[:3 * len(names)], refs[3 * len(names):-1], refs[-1]
        r_sum, g_sum = r_ref[0], pg_ref[0]
        for k in range(1, N_DEV):
            r_sum, g_sum = r_sum + r_ref[k], g_sum + pg_ref[k]
        loss_ref[...] = r_sum[LOSS_AT]
        for i, n in enumerate(names):
            g = r_sum[REPL[n]] if n in REPL else g_sum[GAINS[n]]
            w_ref, m_ref, v_ref = ins[3 * i:3 * i + 3]
            g_ref, d_ref, nm_ref, nv_ref = outs[4 * i:4 * i + 4]
            g_ref[...] = g
            d_ref[...], nm_ref[...], nv_ref[...] = _adamw_math(g, w_ref[...], m_ref[...], v_ref[...])

    flat_in = [d[n] for n in names for d in (W, M, V)]
    res = pl.pallas_call(body, name="adamw_small",
                         out_shape=[S(W[n].shape, F32) for n in names for _ in range(4)] + [S((1, 1), F32)],
                         compiler_params=_params())(r_all, p_gains, *flat_in)
    return {(k, n): res[4 * i + k] for i, n in enumerate(names) for k in range(4)}, res[-1]


WEIGHTS = ["mla_w_in", "mla_g_q", "mla_g_kv", "mla_w_uq", "mla_w_uk", "mla_w_uv", "mla_w_o", "kv_w_shared",
           "swa_w_q", "swa_sinks", "swa_w_o", "rel_bias", "mlp_w_up", "mlp_w_down", "ln_mix_g", "ln_mix_b",
           "ln_mlp_g", "ln_mlp_b"]


def kernel(x, mla_w_in, mla_g_q, mla_g_kv, mla_w_uq, mla_w_uk, mla_w_uv, mla_w_o, kv_w_shared, swa_w_q, swa_sinks, swa_w_o, rel_bias, mlp_w_up, mlp_w_down, ln_mix_g, ln_mix_b, ln_mlp_g, ln_mlp_b, loss_target, m_mla_w_in, m_mla_g_q, m_mla_g_kv, m_mla_w_uq, m_mla_w_uk, m_mla_w_uv, m_mla_w_o, m_kv_w_shared, m_swa_w_q, m_swa_sinks, m_swa_w_o, m_rel_bias, m_mlp_w_up, m_mlp_w_down, m_ln_mix_g, m_ln_mix_b, m_ln_mlp_g, m_ln_mlp_b, v_mla_w_in, v_mla_g_q, v_mla_g_kv, v_mla_w_uq, v_mla_w_uk, v_mla_w_uv, v_mla_w_o, v_kv_w_shared, v_swa_w_q, v_swa_sinks, v_swa_w_o, v_rel_bias, v_mlp_w_up, v_mlp_w_down, v_ln_mix_g, v_ln_mix_b, v_ln_mlp_g, v_ln_mlp_b):
    args = dict(locals())
    W = {n: args[n] for n in WEIGHTS}
    M = {n: args["m_" + n] for n in WEIGHTS}
    V = {n: args["v_" + n] for n in WEIGHTS}
    T = x.shape[1]
    x2d = x.reshape(T, D_MODEL)
    tgt = loss_target.reshape(T, D_MODEL)
    H = MLA_HEADS

    SH = {"mla_w_in": (-1, mla_w_in.shape[-1]), "mla_w_uq": (-1, H * (MLA_NOPE + MLA_ROPE)),
          "mla_w_uk": (-1, H * MLA_NOPE), "mla_w_uv": (-1, H * MLA_V), "mla_w_o": (-1, D_MODEL),
          "kv_w_shared": (-1, kv_w_shared.shape[-1]), "swa_w_q": (-1, swa_w_q.shape[-1]), "swa_w_o": (-1, D_MODEL)}
    slab = lambda d, n: d[n].reshape(SH[n])
    bf = lambda a: a.astype(_MXU_DTYPE)
    gains_slab = lambda d: jnp.pad(jnp.concatenate([d["mla_g_q"], d["mla_g_kv"]], axis=1),
                                   ((0, 7), (0, 128 - d["mla_g_q"].shape[1] - d["mla_g_kv"].shape[1])))
    n_gq, n_gkv = mla_g_q.shape[1], mla_g_kv.shape[1]
    w_in_s, w_uq_s, w_uk_s, gains_all = _gather_two_level(
        [bf(slab(W, "mla_w_in")), bf(slab(W, "mla_w_uq")), bf(slab(W, "mla_w_uk")), gains_slab(W)],
        name="gather_mla_in")
    later = [bf(slab(W, n)) for n in ("mla_w_uv", "mla_w_o", "kv_w_shared", "swa_w_q", "swa_w_o")]
    later += [bf(mlp_w_up), bf(mlp_w_down)]
    w_in = w_in_s.reshape(D_MODEL, -1)
    g_q = gains_all[:, 0, :n_gq].reshape(1, MLA_QR)
    g_kv = gains_all[:, 0, n_gq:n_gq + n_gkv].reshape(1, MLA_C)
    w_uq = w_uq_s.reshape(MLA_QR, H, MLA_NOPE + MLA_ROPE)
    w_uq_n = w_uq[:, :, :MLA_NOPE].reshape(MLA_QR, H * MLA_NOPE)
    w_uq_r = w_uq[:, :, MLA_NOPE:].reshape(MLA_QR, H * MLA_ROPE)
    w_uk = w_uk_s.reshape(MLA_C, H, MLA_NOPE).transpose(1, 0, 2)
    w_uk_t = w_uk.transpose(0, 2, 1)
    ln = lambda a, l: a[l].reshape(1, D_MODEL)

    half = MLA_ROPE // 2
    inv = ROPE_THETA ** (-jnp.arange(half, dtype=F32) / half)
    ang = jnp.arange(T, dtype=F32)[:, None] * inv[None, :]
    cos = jnp.tile(jnp.concatenate([jnp.cos(ang), jnp.cos(ang)], -1), (1, ROPE_TABLE_W // MLA_ROPE))
    sin = jnp.tile(jnp.concatenate([-jnp.sin(ang), jnp.sin(ang)], -1), (1, ROPE_TABLE_W // MLA_ROPE))

    h, kc, kct, qs, qst, cq, qn = _mla_pre_fwd(x2d, w_in, g_q, g_kv, w_uq_n, w_uq_r, w_uk_t, cos, sin)
    olat, lse, (w_uv_s, w_o_s, w_kv_s, w_q_s, w_o2_s, w_up, w_dn) = _mla_attn_fwd(qs, kc, kct, gather=later)
    w_uv = w_uv_s.reshape(MLA_C, H, MLA_V).transpose(1, 0, 2)
    w_o = w_o_s.reshape(H * MLA_V, D_MODEL)
    w_qkv = jnp.concatenate([w_q_s.reshape(D_MODEL, -1), w_kv_s.reshape(D_MODEL, -1)], axis=1)
    w_o2 = w_o2_s.reshape(SWA_QH * SWA_D, D_MODEL)
    o_mla = _mla_uv_fwd(olat, w_uv)
    mix0, mlp0 = (ln(ln_mix_g, 0), ln(ln_mix_b, 0)), (ln(ln_mlp_g, 0), ln(ln_mlp_b, 0))
    mix1, mlp1 = (ln(ln_mix_g, 1), ln(ln_mix_b, 1)), (ln(ln_mlp_g, 1), ln(ln_mlp_b, 1))
    x1b, x1t, xh1, rs1 = _proj_ln_fwd(o_mla, w_o, x2d, None, *mix0, name="mla_out_ln_fwd")
    u0, u0t, xh2, rs2, x2b = _mlp_fwd(x1b, xh1, *mix0, w_up, w_dn, 0, *mlp0)
    onehot = _t5_onehot()
    bias = _mm(rel_bias.T, onehot, name="rel_bias_expand", precision=lax.Precision.HIGHEST, tn=8192).reshape(
        SWA_QH * SWA_BLOCK, 2 * SWA_BLOCK).T
    key = jnp.arange(2 * SWA_BLOCK)[:, None]
    qry = jnp.arange(SWA_QH * SWA_BLOCK)[None, :] % SWA_BLOCK
    in_window = (key > qry) & (key <= qry + SWA_BLOCK)
    bias = jnp.stack([jnp.where(in_window & (key >= SWA_BLOCK), bias, -jnp.inf), jnp.where(in_window, bias, -jnp.inf)])
    sink_rows = jnp.repeat(swa_sinks.reshape(SWA_QH), SWA_BLOCK).reshape(1, SWA_QH * SWA_BLOCK)
    qkv = _mm(x2b, w_qkv, name="swa_qkv_fwd", out_dtype=_MXU_DTYPE, tm=1024, tn=512, tk=1024)
    o_swa = _swa_attn_fwd(qkv, bias, sink_rows)
    x3b, x3t, xh3, rs3 = _proj_ln_fwd(o_swa, w_o2, xh2, mlp0, *mix1, name="swa_out_ln_fwd")
    u1, u1t, xh4, rs4, loss_part, dx4 = _mlp_fwd(x3b, xh3, *mix1, w_up, w_dn, 1, *mlp1, tgt)

    nj = w_up.shape[0]
    dx3, du1, dy4b, dg_mlp1, db_mlp1 = _mlp_bwd_dx(dx4, xh4, rs4, ln(ln_mlp_g, 1), u1, w_up, w_dn, 1)
    g_dn_last, g_up_last = _mlp_bwd_dw(u1t, dy4b, x3t, du1, 1, nj=nj)
    dres3, do_swa, g_o2, dg_mix1, db_mix1 = _proj_ln_bwd(dx3, xh3, rs3, ln(ln_mix_g, 1), o_swa, w_o2,
                                                         name="swa_out_ln_bwd")
    dqkv, dbias, dsink = _swa_attn_bwd(qkv, o_swa, do_swa, bias, sink_rows)
    g_rel = _mm(onehot, dbias.T.reshape(SWA_QH, -1), name="rel_bias_grad", tb=True, precision=lax.Precision.HIGHEST,
                tk=8192)
    head_of_row = (jnp.arange(SWA_QH * SWA_BLOCK)[:, None] // SWA_BLOCK == jnp.arange(SWA_QH)[None, :]).astype(F32)
    g_sinks = _mm(dsink, head_of_row, name="sinks_grad", precision=lax.Precision.HIGHEST, tk=2048)[0:1]
    dx2 = _mm(dqkv, w_qkv, name="swa_qkv_bwd_dx", tb=True, add=dres3, tm=1024, tn=1024, tk=1536)
    g_qkv = _mm(x2b, dqkv, name="swa_qkv_bwd_dw", ta=True, tm=1024, tn=512, tk=1024)
    dx1, du0, dy2b, dg_mlp0, db_mlp0 = _mlp_bwd_dx(dx2, xh2, rs2, ln(ln_mlp_g, 0), u0, w_up, w_dn, 0)
    g_dn, g_up = _mlp_bwd_dw(u0t, dy2b, x1t, du0, 0, nj=nj, other_layers=(g_dn_last, g_up_last))
    dres1, do_mla, g_o, dg_mix0, db_mix0 = _proj_ln_bwd(dx1, xh1, rs1, ln(ln_mix_g, 0), o_mla, w_o,
                                                        name="mla_out_ln_bwd", da_dtype=_MXU_DTYPE)
    dol, delta, g_uv = _mla_uv_bwd(do_mla, olat, w_uv)
    wide = lambda a, rows: jnp.pad(a, ((0, rows - a.shape[0]), (0, LANES - a.shape[1])))
    r_part = jnp.concatenate([dg_mix0, dg_mix1, db_mix0, db_mix1, dg_mlp0, dg_mlp1, db_mlp0, db_mlp1,
                              wide(jnp.concatenate([g_sinks, loss_part], axis=1), 8), wide(g_rel, SMALL_ROWS - 16)],
                             axis=0)
    by_dev = lambda g: g.reshape((N_DEV, g.shape[0] // N_DEV) + g.shape[1:])
    early = [by_dev(g_o2), by_dev(g_qkv), g_up, g_dn, by_dev(g_o),
             by_dev(g_uv.transpose(1, 0, 2).reshape(MLA_C, H * MLA_V))]
    dqs, dkc, dv, (r_all, p_o2, p_qkv, p_up, p_dn, p_o, p_uv) = _mla_attn_bwd(
        qs, qst, kc, kct, dol, lse, delta, gather=[r_part], scatter=early)
    grad_x, g_in, g_uq_n, g_uq_r, g_uk, g_gq, g_gkv = _mla_pre_bwd(
        dqs, dkc, dv, h, x2d, dres1, cq, qn, cos, sin, w_in, g_q, g_kv, w_uq_n, w_uq_r, w_uk)
    g_uq = jnp.concatenate([g_uq_n.reshape(MLA_QR, H, MLA_NOPE), g_uq_r.reshape(MLA_QR, H, MLA_ROPE)], -1)
    g_gains = jnp.pad(jnp.concatenate([g_gq.reshape(N_DEV, n_gq), g_gkv.reshape(N_DEV, n_gkv)], axis=1)[:, None, :],
                      ((0, 0), (0, 7), (0, 128 - n_gq - n_gkv)))
    p_in, p_uq, p_uk, p_gains = _exchange(
        [], [bf(by_dev(g_in)), bf(by_dev(g_uq.reshape(MLA_QR, -1))),
             bf(by_dev(g_uk.transpose(1, 0, 2).reshape(MLA_C, -1))), g_gains], name="exchange_mla_in_grads")

    res = {}

    def adam(name, parts, names, to_slab, from_slab):
        out = _adamw(parts, to_slab(W), to_slab(M), to_slab(V), name="adamw_" + name)
        for k in range(4):
            for n, a in zip(names, from_slab(out[k])):
                res[(k, n)] = a.reshape(W[n].shape)

    one = lambda n: (lambda d: slab(d, n))
    adam("swa_w_o", p_o2, ["swa_w_o"], one("swa_w_o"), lambda s: [s])
    dq_cols = SWA_QH * SWA_D
    adam("swa_qkv", p_qkv, ["swa_w_q", "kv_w_shared"],
         lambda d: jnp.concatenate([slab(d, "swa_w_q"), slab(d, "kv_w_shared")], axis=1),
         lambda s: [s[:, :dq_cols], s[:, dq_cols:]])
    layers_as_rows = lambda a: a.reshape((-1,) + a.shape[-1:])
    adam("mlp_w_up", p_up.reshape(N_DEV, -1, p_up.shape[-1]), ["mlp_w_up"],
         lambda d: layers_as_rows(d["mlp_w_up"]), lambda s: [s])
    adam("mlp_w_down", p_dn.reshape(N_DEV, -1, p_dn.shape[-1]), ["mlp_w_down"],
         lambda d: layers_as_rows(d["mlp_w_down"]), lambda s: [s])
    adam("mla_w_o", p_o, ["mla_w_o"], one("mla_w_o"), lambda s: [s])
    adam("mla_w_uv", p_uv, ["mla_w_uv"], one("mla_w_uv"), lambda s: [s])
    adam("mla_w_in", p_in, ["mla_w_in"], one("mla_w_in"), lambda s: [s])
    adam("mla_w_uq", p_uq, ["mla_w_uq"], one("mla_w_uq"), lambda s: [s])
    adam("mla_w_uk", p_uk, ["mla_w_uk"], one("mla_w_uk"), lambda s: [s])
    small, loss = _adamw_small(r_all, p_gains, W, M, V)
    res.update(small)
    loss = loss.reshape(())
    return (loss, grad_x.reshape(x.shape), *[res[(k, n)] for k in range(4) for n in WEIGHTS])
```

```python
import math

import numpy as np
import jax
import jax.numpy as jnp
from jax import lax
from jax.experimental import pallas as pl
from jax.experimental.pallas import tpu as pltpu

F32 = jnp.float32
_MXU_DTYPE = jnp.bfloat16

D_MODEL = 1024
DEPTH = 2
MLA_HEADS = 8
MLA_NOPE = 128
MLA_ROPE = 64
MLA_V = 128
MLA_QR = 384
MLA_C = 256
MLA_DK = 384
MLA_DT = MLA_C + MLA_ROPE
ROPE_THETA = 10000.0
SWA_QH = 16
SWA_KVH = 4
SWA_D = 64
SWA_BLOCK = 128
REL_BUCKETS = 32
REL_MAX_DIST = 128
D_FF = 4096
LN_EPS = 1e-5
RMS_EPS = 1e-6
ALPHA = (2 * DEPTH) ** 0.25
ADAM_LR, ADAM_B1, ADAM_B2, ADAM_EPS, ADAM_WD, ADAM_STEP = 0.001, 0.9, 0.999, 1e-08, 0.01, 10

N_DEV = 8
AXES = ("x", "y", "c")
V7X_VMEM_BYTES = 64 * 1024 * 1024
VMEM_LIMIT = V7X_VMEM_BYTES - 8 * 1024 * 1024
LANES = 1024
ATT_TQ = 512
ATT_TK = 512
ATT_HEAD_GROUP = 1
ATT_FWD_HEAD_GROUP = 2

NT = (((1,), (1,)), ((), ()))
TN = (((0,), (0,)), ((), ()))
S = jax.ShapeDtypeStruct


def _params(*sem, vmem=VMEM_LIMIT):
    return pltpu.CompilerParams(dimension_semantics=sem, vmem_limit_bytes=vmem)


def _dot(a, b, dims=None, precision=None):
    if dims is None:
        return jnp.dot(a, b, preferred_element_type=F32, precision=precision)
    return lax.dot_general(a, b, dims, preferred_element_type=F32, precision=precision)


def _mx(v):
    return v.astype(_MXU_DTYPE)


ROPE_TABLE_W = 128


def _tile_heads(t):
    return jnp.concatenate([t] * (MLA_HEADS * MLA_ROPE // ROPE_TABLE_W), axis=1)


def _swap_halves_64(v):
    return jnp.concatenate([v[:, 32:], v[:, :32]], axis=-1)


def _swap_halves_groups(v):
    n = v.shape[-1]
    lane = lax.broadcasted_iota(jnp.int32, v.shape, 1)
    return jnp.where(lane % 64 < 32, pltpu.roll(v, n - 32, 1), pltpu.roll(v, 32, 1))


def _mm(a, b, *, name, ta=False, tb=False, add=None, out_dtype=F32, tm=512, tn=512, tk=512, precision=None):
    M, K = (a.shape[1], a.shape[0]) if ta else a.shape
    N = b.shape[0] if tb else b.shape[1]
    tm, tn, tk = min(tm, M), min(tn, N), min(tk, K)
    assert M % tm == 0 and N % tn == 0 and K % tk == 0, (M, N, K, tm, tn, tk)
    nk = K // tk
    dims = (((0 if ta else 1,), (1 if tb else 0,)), ((), ()))
    has_add = add is not None

    def body(*refs):
        if has_add:
            a_ref, b_ref, add_ref, o_ref, acc = refs
        else:
            a_ref, b_ref, o_ref, acc = refs
        k = pl.program_id(2)
        av, bv = a_ref[...], b_ref[...]
        if precision is None:
            av, bv = _mx(av), _mx(bv)
        part = _dot(av, bv, dims, precision)
        if nk == 1:
            o_ref[...] = (part + add_ref[...] if has_add else part).astype(out_dtype)
            return

        @pl.when(k == 0)
        def _():
            acc[...] = add_ref[...] if has_add else jnp.zeros_like(acc)

        acc[...] += part

        @pl.when(k == nk - 1)
        def _():
            o_ref[...] = acc[...].astype(out_dtype)

    a_spec = pl.BlockSpec((tk, tm), lambda i, j, k: (k, i)) if ta else pl.BlockSpec((tm, tk), lambda i, j, k: (i, k))
    b_spec = pl.BlockSpec((tn, tk), lambda i, j, k: (j, k)) if tb else pl.BlockSpec((tk, tn), lambda i, j, k: (k, j))
    in_specs = [a_spec, b_spec]
    args = [a, b]
    if has_add:
        in_specs.append(pl.BlockSpec((tm, tn), lambda i, j, k: (i, j)))
        args.append(add)
    return pl.pallas_call(
        body, name=name, grid=(M // tm, N // tn, nk), in_specs=in_specs,
        out_specs=pl.BlockSpec((tm, tn), lambda i, j, k: (i, j)), out_shape=S((M, N), out_dtype),
        scratch_shapes=[pltpu.VMEM((tm, tn), F32)],
        compiler_params=_params("parallel", "parallel", "arbitrary"))(*args)


def _ln_fwd_math(z, g, b):
    mu = jnp.mean(z, axis=-1, keepdims=True)
    zc = z - mu
    var = jnp.mean(zc * zc, axis=-1, keepdims=True)
    rstd = lax.rsqrt(var + LN_EPS)
    xhat = zc * rstd
    return xhat * g + b, xhat, rstd


def _ln_bwd_math(dxo, xhat, rstd, g):
    dxh = dxo * g
    m1 = jnp.mean(dxh, axis=-1, keepdims=True)
    m2 = jnp.mean(dxh * xhat, axis=-1, keepdims=True)
    dz = rstd * (dxh - m1 - xhat * m2)
    dg = jnp.sum(dxo * xhat, axis=0, keepdims=True)
    db = jnp.sum(dxo, axis=0, keepdims=True)
    return dz, dg, db


def _rms_fwd_math(xr, g):
    r = lax.rsqrt(jnp.mean(xr * xr, axis=-1, keepdims=True) + RMS_EPS)
    return xr * r * g


def _rms_bwd_math(dy, xr, g):
    r = lax.rsqrt(jnp.mean(xr * xr, axis=-1, keepdims=True) + RMS_EPS)
    gy = dy * g
    dx = r * gy - xr * (r * r * r) * jnp.mean(gy * xr, axis=-1, keepdims=True)
    dg = jnp.sum(dy * xr * r, axis=0, keepdims=True)
    return dx, dg


def _mla_pre_fwd(x, w_in, g_q, g_kv, w_uq_n, w_uq_r, w_uk_t, cos, sin):
    T = x.shape[0]
    tm = min(ATT_TQ, T)
    nq = T // tm
    H = MLA_HEADS

    tk = min(ATT_TK, T)

    def body(x_ref, win_ref, gq_ref, gkv_ref, wn_ref, wr_ref, wuk_ref, cos_ref, sin_ref,
             h_ref, kc_ref, kct_ref, qs_ref, qst_ref, cq_ref, qn_ref):
        h = _dot(_mx(x_ref[...]), win_ref[...])
        h_ref[...] = h
        cos_v, sin_v = _tile_heads(cos_ref[...]), _tile_heads(sin_ref[...])
        cq = _mx(_rms_fwd_math(h[:, :MLA_QR], gq_ref[...]))
        ckv = _rms_fwd_math(h[:, MLA_QR:MLA_QR + MLA_C], gkv_ref[...])
        krr = h[:, MLA_QR + MLA_C:]
        kr = krr * cos_v[:, :MLA_ROPE] + _swap_halves_64(krr) * sin_v[:, :MLA_ROPE]
        kr_pad = jnp.concatenate([kr, jnp.zeros((tm, MLA_DK - MLA_C - MLA_ROPE), F32)], axis=1)
        kc_ref[:, 0:MLA_C] = _mx(ckv)
        kc_ref[:, MLA_C:] = _mx(kr_pad)
        kct_ref[0:MLA_C, :] = _mx(ckv.T)
        kct_ref[MLA_C:, :] = _mx(kr_pad.T[0:MLA_ROPE, :])
        cq_ref[...] = cq
        qnb = _mx(_dot(cq, wn_ref[...]))
        qn_ref[...] = qnb
        qr = _dot(cq, wr_ref[...])
        qrr = qr * cos_v + _swap_halves_groups(qr) * sin_v
        qrr_t = qrr.T
        for hd in range(H):
            ql = _dot(qnb[:, MLA_NOPE * hd:MLA_NOPE * (hd + 1)], wuk_ref[hd])
            qst_ref[0, 0:MLA_C, tm * hd:tm * (hd + 1)] = _mx(ql.T)
            qst_ref[0, MLA_C:, tm * hd:tm * (hd + 1)] = _mx(qrr_t[MLA_ROPE * hd:MLA_ROPE * (hd + 1), :])
            qs_ref[0, hd, :, 0:MLA_C] = _mx(ql)
            qs_ref[0, hd, :, MLA_C:MLA_C + MLA_ROPE] = _mx(qrr[:, MLA_ROPE * hd:MLA_ROPE * (hd + 1)])
            qs_ref[0, hd, :, MLA_C + MLA_ROPE:] = jnp.zeros((tm, MLA_DK - MLA_C - MLA_ROPE), _MXU_DTYPE)

    full = lambda shp: pl.BlockSpec(shp, lambda i: (0,) * len(shp))
    rows = lambda n: pl.BlockSpec((tm, n), lambda i: (i, 0))
    n_in = w_in.shape[1]
    return pl.pallas_call(
        body, name="mla_pre_fwd", grid=(nq,),
        in_specs=[rows(D_MODEL), full(w_in.shape), full(g_q.shape), full(g_kv.shape), full(w_uq_n.shape),
                  full(w_uq_r.shape), full(w_uk_t.shape), rows(ROPE_TABLE_W), rows(ROPE_TABLE_W)],
        out_specs=[rows(n_in), rows(MLA_DK),
                   pl.BlockSpec((None, MLA_DT, tm), lambda i: (i * tm // tk, 0, i % (tk // tm))),
                   pl.BlockSpec((1, H, tm, MLA_DK), lambda i: (i, 0, 0, 0)),
                   pl.BlockSpec((1, MLA_DT, H * tm), lambda i: (i, 0, 0)), rows(MLA_QR), rows(H * MLA_NOPE)],
        out_shape=[S((T, n_in), F32), S((T, MLA_DK), _MXU_DTYPE), S((T // tk, MLA_DT, tk), _MXU_DTYPE),
                   S((nq, H, tm, MLA_DK), _MXU_DTYPE), S((nq, MLA_DT, H * tm), _MXU_DTYPE),
                   S((T, MLA_QR), _MXU_DTYPE), S((T, H * MLA_NOPE), _MXU_DTYPE)],
        compiler_params=_params("parallel"))(x, w_in, g_q, g_kv, w_uq_n, w_uq_r, w_uk_t, cos, sin)


def _att_steps(T, tq, tk):
    qi, kj = [], []
    for i in range(T // tq):
        for j in range((i * tq + tq - 1) // tk + 1):
            qi.append(i)
            kj.append(j)
    return jnp.asarray(np.array(qi, np.int32)), jnp.asarray(np.array(kj, np.int32))


def _ride_exchange(st, n_steps, ins, outs, n_gather, sems):
    if not ins:
        return

    @pl.when(st == 0)
    def _():
        for cp in _exchange_copies(ins, outs, n_gather, *sems):
            cp.start()

    @pl.when(st == n_steps - 1)
    def _():
        for cp in _exchange_copies(ins, outs, n_gather, *sems):
            cp.wait()


def _mla_attn_fwd(qs, kc, kct, gather=(), scatter=()):
    nq, H, tq, DK = qs.shape
    T = kc.shape[0]
    tk = min(ATT_TK, T)
    scale = (MLA_NOPE + MLA_ROPE) ** -0.5
    c2 = scale * math.log2(math.e)
    qi, kj = _att_steps(T, tq, tk)
    n_steps = int(qi.shape[0])
    hg = ATT_FWD_HEAD_GROUP
    R = hg * tq
    n_x = len(gather) + len(scatter)

    def body(qi_ref, kj_ref, q_ref, k_ref, kt_ref, *rest):
        x_ins, (o_ref, lse_ref), x_outs = rest[:n_x], rest[n_x:n_x + 2], rest[n_x + 2:2 * n_x + 2]
        m_sc, l_sc, acc_sc = rest[2 * n_x + 2:2 * n_x + 5]
        st = pl.program_id(0)
        _ride_exchange(st, n_steps, x_ins, x_outs, len(gather), rest[2 * n_x + 5:])
        i, j = qi_ref[st], kj_ref[st]
        j_last = (i * tq + tq - 1) // tk

        @pl.when(j == 0)
        def _():
            m_sc[...] = jnp.full_like(m_sc, -jnp.inf)
            l_sc[...] = jnp.zeros_like(l_sc)
            acc_sc[...] = jnp.zeros_like(acc_sc)

        def step(masked):
            k = k_ref[...]
            vt = kt_ref[0:MLA_C, :]
            if masked:
                key = lax.broadcasted_iota(jnp.int32, (tk, R), 0) + j * tk
                qry = lax.broadcasted_iota(jnp.int32, (tk, R), 1) % tq + i * tq
                causal = key <= qry
            n_g = H // hg
            qk = lambda g: _dot(k, q_ref[0, g * hg:(g + 1) * hg].reshape(R, DK), NT)
            def accumulate(g, a, pb):
                cs = slice(g * R, (g + 1) * R)
                acc_sc[:, cs] = a * acc_sc[:, cs] + _dot(vt, pb)

            s_next = qk(0)
            pending = None
            for g in range(n_g):
                cs = slice(g * R, (g + 1) * R)
                s = s_next
                if g + 1 < n_g:
                    s_next = qk(g + 1)
                if pending is not None:
                    accumulate(*pending)
                if masked:
                    s = jnp.where(causal, s, -jnp.inf)
                m_prev = m_sc[:, cs]
                m_new = jnp.maximum(m_prev, jnp.max(s, axis=0, keepdims=True))
                a = jnp.exp2((m_prev - m_new) * c2)
                p = jnp.exp2((s - m_new) * c2)
                l_sc[:, cs] = a * l_sc[:, cs] + jnp.sum(p, axis=0, keepdims=True)
                m_sc[:, cs] = m_new
                pending = (g, a, _mx(p))
            accumulate(*pending)

        pl.when(j == j_last)(lambda: step(True))
        pl.when(j != j_last)(lambda: step(False))

        @pl.when(j == j_last)
        def _():
            o_ref[0] = _mx(acc_sc[...] / l_sc[...])
            lse_ref[0] = m_sc[...] * scale + jnp.log(l_sc[...])

    hbm = pl.BlockSpec(memory_space=pl.ANY)
    x_shapes, x_sems = _exchange_shapes(gather, scatter) if n_x else ([], [])
    gs = pltpu.PrefetchScalarGridSpec(
        num_scalar_prefetch=2, grid=(n_steps,),
        in_specs=[pl.BlockSpec((1, H, tq, DK), lambda s, qi, kj: (qi[s], 0, 0, 0)),
                  pl.BlockSpec((tk, DK), lambda s, qi, kj: (kj[s], 0)),
                  pl.BlockSpec((None, MLA_DT, tk), lambda s, qi, kj: (kj[s], 0, 0))] + [hbm] * n_x,
        out_specs=[pl.BlockSpec((1, MLA_C, H * tq), lambda s, qi, kj: (qi[s], 0, 0)),
                   pl.BlockSpec((1, 1, H * tq), lambda s, qi, kj: (qi[s], 0, 0))] + [hbm] * n_x,
        scratch_shapes=[pltpu.VMEM((1, H * tq), F32), pltpu.VMEM((1, H * tq), F32),
                        pltpu.VMEM((MLA_C, H * tq), F32)] + x_sems)
    res = pl.pallas_call(
        body, name="mla_attn_fwd", grid_spec=gs,
        out_shape=[S((nq, MLA_C, H * tq), _MXU_DTYPE), S((nq, 1, H * tq), F32)] + x_shapes,
        compiler_params=_params("arbitrary"))(qi, kj, qs, kc, kct, *gather, *scatter)
    return res[0], res[1], res[2:]


def _mla_attn_bwd(qs, qst, kc, kct, dol, lse, delta, gather=(), scatter=()):
    nq, H, tq, DK = qs.shape
    T = kc.shape[0]
    tk = min(ATT_TK, T)
    scale = (MLA_NOPE + MLA_ROPE) ** -0.5
    log2e = math.log2(math.e)
    qi, kj = _att_steps(T, tq, tk)
    n_steps = int(qi.shape[0])
    hg = ATT_HEAD_GROUP
    R = hg * tq
    n_x = len(gather) + len(scatter)

    def body(qi_ref, kj_ref, q_ref, qt_ref, k_ref, kt_ref, do_ref, lse_ref, dl_ref, *rest):
        x_ins, (dq_ref, dk_ref, dv_ref), x_outs = rest[:n_x], rest[n_x:n_x + 3], rest[n_x + 3:2 * n_x + 3]
        dk_acc, dv_acc, sem = rest[2 * n_x + 3:2 * n_x + 6]
        st = pl.program_id(0)
        _ride_exchange(st, n_steps, x_ins, x_outs, len(gather), rest[2 * n_x + 6:])
        i, j = qi_ref[st], kj_ref[st]
        j_last = (i * tq + tq - 1) // tk

        @pl.when(st == 0)
        def _():
            dk_acc[...] = jnp.zeros_like(dk_acc)
            dv_acc[...] = jnp.zeros_like(dv_acc)

        @pl.when(j == 0)
        def _():
            dq_ref[...] = jnp.zeros_like(dq_ref)

        def step(masked):
            k, kt = k_ref[...], kt_ref[...]
            v = k[:, :MLA_C]
            if masked:
                key = lax.broadcasted_iota(jnp.int32, (tk, R), 0) + j * tk
                qry = lax.broadcasted_iota(jnp.int32, (tk, R), 1) % tq + i * tq
                causal = key <= qry
            dkt_c = jnp.zeros((MLA_DT, tk), F32)
            dvt_c = jnp.zeros((MLA_C, tk), F32)
            n_g = H // hg

            def scores(g):
                q = q_ref[0, g * hg:(g + 1) * hg].reshape(R, DK)
                dot = do_ref[0, :, g * R:(g + 1) * R]
                return dot, _dot(k, q, NT), _dot(v, dot)

            nxt = scores(0)
            for g in range(n_g):
                cs = slice(g * R, (g + 1) * R)
                dot, s, dp = nxt
                if g + 1 < n_g:
                    nxt = scores(g + 1)
                p = jnp.exp2(s * (scale * log2e) - lse_ref[0, :, cs] * log2e)
                if masked:
                    p = jnp.where(causal, p, 0.0)
                dsb = _mx(p * (dp - dl_ref[0, :, cs]))
                dq_ref[0, :, cs] += _dot(kt, dsb)
                dkt_c = dkt_c + _dot(qt_ref[0, :, cs], dsb, NT)
                dvt_c = dvt_c + _dot(dot, _mx(p), NT)
            dk_acc[j] += dkt_c * scale
            dv_acc[j] += dvt_c

        pl.when(j == j_last)(lambda: step(True))
        pl.when(j != j_last)(lambda: step(False))

        @pl.when(j == j_last)
        def _():
            dq_ref[...] = dq_ref[...] * scale

        @pl.when(st == n_steps - 1)
        def _():
            c1 = pltpu.make_async_copy(dk_acc, dk_ref, sem.at[0])
            c2 = pltpu.make_async_copy(dv_acc, dv_ref, sem.at[1])
            c1.start()
            c2.start()
            c1.wait()
            c2.wait()

    cols = lambda n: pl.BlockSpec((1, n, H * tq), lambda s, qi, kj: (qi[s], 0, 0))
    hbm = pl.BlockSpec(memory_space=pl.ANY)
    x_shapes, x_sems = _exchange_shapes(gather, scatter) if n_x else ([], [])
    gs = pltpu.PrefetchScalarGridSpec(
        num_scalar_prefetch=2, grid=(n_steps,),
        in_specs=[pl.BlockSpec((1, H, tq, DK), lambda s, qi, kj: (qi[s], 0, 0, 0)), cols(MLA_DT),
                  pl.BlockSpec((tk, DK), lambda s, qi, kj: (kj[s], 0)),
                  pl.BlockSpec((None, MLA_DT, tk), lambda s, qi, kj: (kj[s], 0, 0)),
                  cols(MLA_C), cols(1), cols(1)] + [hbm] * n_x,
        out_specs=[cols(MLA_DT), hbm, hbm] + [hbm] * n_x,
        scratch_shapes=[pltpu.VMEM((T // tk, MLA_DT, tk), F32), pltpu.VMEM((T // tk, MLA_C, tk), F32),
                        pltpu.SemaphoreType.DMA((2,))] + x_sems)
    res = pl.pallas_call(
        body, name="mla_attn_bwd", grid_spec=gs,
        out_shape=[S((nq, MLA_DT, H * tq), F32), S((T // tk, MLA_DT, tk), F32),
                   S((T // tk, MLA_C, tk), F32)] + x_shapes,
        compiler_params=_params("arbitrary"))(qi, kj, qs, qst, kc, kct, dol, lse, delta, *gather, *scatter)
    return res[0], res[1], res[2], res[3:]


def _mla_uv_fwd(olat, w_uv):
    nq, C, cols = olat.shape
    H = w_uv.shape[0]
    tq = cols // H

    def body(ol_ref, wuv_ref, o_ref):
        for hd in range(H):
            o_ref[:, MLA_V * hd:MLA_V * (hd + 1)] = _mx(_dot(ol_ref[0, :, tq * hd:tq * (hd + 1)], wuv_ref[hd], TN))

    return pl.pallas_call(
        body, name="mla_uv_fwd", grid=(nq,),
        in_specs=[pl.BlockSpec((1, C, cols), lambda i: (i, 0, 0)), pl.BlockSpec(w_uv.shape, lambda i: (0, 0, 0))],
        out_specs=pl.BlockSpec((tq, H * MLA_V), lambda i: (i, 0)), out_shape=S((nq * tq, H * MLA_V), _MXU_DTYPE),
        compiler_params=_params("parallel"))(olat, w_uv)


def _mla_uv_bwd(do, olat, w_uv):
    nq, C, cols = olat.shape
    H = w_uv.shape[0]
    tq = cols // H

    def body(do_ref, ol_ref, wuv_ref, dol_ref, dl_ref, dw_ref):
        @pl.when(pl.program_id(0) == 0)
        def _():
            dw_ref[...] = jnp.zeros_like(dw_ref)

        dov = do_ref[...]
        for hd in range(H):
            cs = slice(tq * hd, tq * (hd + 1))
            doh = _mx(dov[:, MLA_V * hd:MLA_V * (hd + 1)])
            ol = ol_ref[0, :, cs]
            dol = _dot(wuv_ref[hd], doh, NT)
            dol_ref[0, :, cs] = _mx(dol)
            dl_ref[0, :, cs] = jnp.sum(dol * ol.astype(F32), axis=0, keepdims=True)
            dw_ref[hd] += _dot(ol, doh)

    blk = lambda n: pl.BlockSpec((1, n, cols), lambda i: (i, 0, 0))
    return pl.pallas_call(
        body, name="mla_uv_bwd", grid=(nq,),
        in_specs=[pl.BlockSpec((tq, H * MLA_V), lambda i: (i, 0)), blk(C), pl.BlockSpec(w_uv.shape, lambda i: (0, 0, 0))],
        out_specs=[blk(C), blk(1), pl.BlockSpec(w_uv.shape, lambda i: (0, 0, 0))],
        out_shape=[S(olat.shape, _MXU_DTYPE), S((nq, 1, cols), F32), S(w_uv.shape, F32)],
        compiler_params=_params("arbitrary"))(do, olat, w_uv)


def _mla_pre_bwd(dqs, dkc, dv, h, x, dres, cq, qn, cos, sin, w_in, g_q, g_kv, w_uq_n, w_uq_r, w_uk):
    nq, DK, cols = dqs.shape
    H = w_uk.shape[0]
    tm = cols // H
    T = nq * tm
    tk = dv.shape[2]
    n_in = w_in.shape[1]

    def body(dqs_ref, dkc_ref, dv_ref, h_ref, x_ref, dres_ref, cq_ref, qn_ref, cos_ref, sin_ref,
             win_ref, gq_ref, gkv_ref, wn_ref, wr_ref, wuk_ref,
             gx_ref, dwin_ref, dwn_ref, dwr_ref, dwuk_ref, dgq_ref, dgkv_ref, dqn_sc, dqr_sc, dh_sc):
        @pl.when(pl.program_id(0) == 0)
        def _():
            for r in (dwin_ref, dwn_ref, dwr_ref, dwuk_ref, dgq_ref, dgkv_ref):
                r[...] = jnp.zeros_like(r)

        cos_v, sin_v = _tile_heads(cos_ref[...]), _tile_heads(sin_ref[...])
        qnb = qn_ref[...]
        for hd in range(H):
            cs = slice(tm * hd, tm * (hd + 1))
            dql = _mx(dqs_ref[0, 0:MLA_C, cs])
            dqn_sc[:, MLA_NOPE * hd:MLA_NOPE * (hd + 1)] = _dot(dql, wuk_ref[hd], TN)
            dwuk_ref[hd] += _dot(dql, qnb[:, MLA_NOPE * hd:MLA_NOPE * (hd + 1)])
            dqr_sc[MLA_ROPE * hd:MLA_ROPE * (hd + 1), :] = dqs_ref[0, MLA_C:MLA_C + MLA_ROPE, cs]
        dqr = dqr_sc[...].T
        dqrb = _mx(dqr * cos_v + _swap_halves_groups(dqr * sin_v))
        dqnb = _mx(dqn_sc[...])
        cq = cq_ref[...]
        dwn_ref[...] += _dot(cq, dqnb, TN)
        dwr_ref[...] += _dot(cq, dqrb, TN)
        dcq = _dot(dqnb, wn_ref[...], NT) + _dot(dqrb, wr_ref[...], NT)
        hv = h_ref[...]
        dxq, dgq = _rms_bwd_math(dcq, hv[:, :MLA_QR], gq_ref[...])
        dgq_ref[...] += dgq
        dckv = (dkc_ref[0:MLA_C, :] + dv_ref[...]).T
        dxkv, dgkv = _rms_bwd_math(dckv, hv[:, MLA_QR:MLA_QR + MLA_C], gkv_ref[...])
        dgkv_ref[...] += dgkv
        dkr = jnp.concatenate([dkc_ref[MLA_C:, :], jnp.zeros((128 - MLA_ROPE, tm), F32)], axis=0).T[:, :MLA_ROPE]
        dkr_raw = dkr * cos_v[:, :MLA_ROPE] + _swap_halves_64(dkr * sin_v[:, :MLA_ROPE])
        dh_sc[:, 0:MLA_QR] = dxq
        dh_sc[:, MLA_QR:MLA_QR + MLA_C] = dxkv
        dh_sc[:, MLA_QR + MLA_C:] = dkr_raw
        dhb = _mx(dh_sc[...])
        gx_ref[...] = dres_ref[...] + _dot(dhb, win_ref[...], NT)
        dwin_ref[...] += _dot(_mx(x_ref[...]), dhb, TN)

    full = lambda shp: pl.BlockSpec(shp, lambda i: (0,) * len(shp))
    rows = lambda n: pl.BlockSpec((tm, n), lambda i: (i, 0))
    return pl.pallas_call(
        body, name="mla_pre_bwd", grid=(nq,),
        in_specs=[pl.BlockSpec((1, DK, cols), lambda i: (i, 0, 0)),
                  pl.BlockSpec((None, DK, tm), lambda i: (i * tm // tk, 0, i % (tk // tm))),
                  pl.BlockSpec((None, MLA_C, tm), lambda i: (i * tm // tk, 0, i % (tk // tm))), rows(n_in),
                  rows(D_MODEL), rows(D_MODEL), rows(MLA_QR), rows(H * MLA_NOPE), rows(ROPE_TABLE_W), rows(ROPE_TABLE_W),
                  full(w_in.shape), full(g_q.shape), full(g_kv.shape), full(w_uq_n.shape), full(w_uq_r.shape),
                  full(w_uk.shape)],
        out_specs=[rows(D_MODEL), full(w_in.shape), full(w_uq_n.shape), full(w_uq_r.shape), full(w_uk.shape),
                   full(g_q.shape), full(g_kv.shape)],
        out_shape=[S((T, D_MODEL), F32), S(w_in.shape, F32), S(w_uq_n.shape, F32), S(w_uq_r.shape, F32),
                   S(w_uk.shape, F32), S(g_q.shape, F32), S(g_kv.shape, F32)],
        scratch_shapes=[pltpu.VMEM((tm, H * MLA_NOPE), F32), pltpu.VMEM((H * MLA_ROPE, tm), F32),
                        pltpu.VMEM((tm, n_in), F32)],
        compiler_params=_params("arbitrary"))(dqs, dkc, dv, h, x, dres, cq, qn, cos, sin, w_in, g_q, g_kv,
                                              w_uq_n, w_uq_r, w_uk)


def _proj_ln_fwd(a, w, xres, res_gb, g, b, *, name, tm=512):
    T, K = a.shape
    tm = min(tm, T)
    gp, bp = res_gb if res_gb is not None else (None, None)

    def body(a_ref, w_ref, x_ref, *rest):
        if res_gb is not None:
            x = x_ref[...] * rest[0][...] + rest[1][...]
            rest = rest[2:]
        else:
            x = x_ref[...]
        g_ref, b_ref, xob_ref, xt_ref, xh_ref, rs_ref = rest
        z = ALPHA * x + _dot(a_ref[...], w_ref[...])
        xo, xhat, rstd = _ln_fwd_math(z, g_ref[...], b_ref[...])
        xob_ref[...] = _mx(xo)
        xt_ref[...] = _mx(xo.T)
        xh_ref[...] = xhat
        rs_ref[...] = rstd

    rows = lambda n: pl.BlockSpec((tm, n), lambda i: (i, 0))
    full = lambda shp: pl.BlockSpec(shp, lambda i: (0,) * len(shp))
    extra = [gp, bp] if res_gb is not None else []
    return pl.pallas_call(
        body, name=name, grid=(T // tm,),
        in_specs=[rows(K), full(w.shape), rows(D_MODEL)] + [full(e.shape) for e in extra] + [full(g.shape), full(b.shape)],
        out_specs=[rows(D_MODEL), pl.BlockSpec((D_MODEL, tm), lambda i: (0, i)), rows(D_MODEL), rows(1)],
        out_shape=[S((T, D_MODEL), _MXU_DTYPE), S((D_MODEL, T), _MXU_DTYPE), S((T, D_MODEL), F32), S((T, 1), F32)],
        compiler_params=_params("parallel"))(a, w, xres, *extra, g, b)


def _proj_ln_bwd(dxo, xhat, rstd, g, a, w, *, name, da_dtype=F32, tm=512):
    T, K = a.shape
    tm = min(tm, T)

    def body(dxo_ref, xh_ref, rs_ref, g_ref, a_ref, w_ref, dres_ref, da_ref, dw_ref, dg_ref, db_ref):
        @pl.when(pl.program_id(0) == 0)
        def _():
            for r in (dw_ref, dg_ref, db_ref):
                r[...] = jnp.zeros_like(r)

        dz, dg, db = _ln_bwd_math(dxo_ref[...], xh_ref[...], rs_ref[...], g_ref[...])
        dg_ref[...] += dg
        db_ref[...] += db
        dres_ref[...] = ALPHA * dz
        dzb = _mx(dz)
        da_ref[...] = _dot(dzb, w_ref[...], NT).astype(da_dtype)
        dw_ref[...] += _dot(a_ref[...], dzb, TN)

    rows = lambda n: pl.BlockSpec((tm, n), lambda i: (i, 0))
    full = lambda shp: pl.BlockSpec(shp, lambda i: (0,) * len(shp))
    return pl.pallas_call(
        body, name=name, grid=(T // tm,),
        in_specs=[rows(D_MODEL), rows(D_MODEL), rows(1), full(g.shape), rows(K), full(w.shape)],
        out_specs=[rows(D_MODEL), rows(K), full(w.shape), full(g.shape), full(g.shape)],
        out_shape=[S((T, D_MODEL), F32), S((T, K), da_dtype), S(w.shape, F32), S(g.shape, F32), S(g.shape, F32)],
        compiler_params=_params("arbitrary"))(dxo, xhat, rstd, g, a, w)


MLP_FWD_CHUNKS = 2


def _mlp_fwd(xb, xh_in, g_in, b_in, w_up, w_dn, layer, g, b, target=None, *, tm=1024):
    T, D = xb.shape
    with_loss = target is not None
    tm = min(tm // 2 if with_loss else tm, T)
    nj, _, _, fc = w_up.shape
    cps = MLP_FWD_CHUNKS * (2 if with_loss else 1)
    ns = nj // cps

    def body(xb_ref, xh_ref_in, gi_ref, bi_ref, wu_ref, wd_ref, g_ref, b_ref, *rest):
        if with_loss:
            t_ref, u_ref, ut_ref, xh_ref, rs_ref, loss_ref, dy_ref, acc = rest
        else:
            u_ref, ut_ref, xh_ref, rs_ref, xob_ref, acc = rest
        i, j = pl.program_id(0), pl.program_id(1)

        @pl.when(j == 0)
        def _():
            acc[...] = ALPHA * (xh_ref_in[...] * gi_ref[...] + bi_ref[...])

        if with_loss:
            @pl.when((i == 0) & (j == 0))
            def _():
                loss_ref[...] = jnp.zeros_like(loss_ref)

        xb_v = xb_ref[...]
        for c in range(cps):
            u = _dot(xb_v, wu_ref[c])
            u_ref[:, fc * c:fc * (c + 1)] = _mx(u)
            ut_ref[fc * c:fc * (c + 1), :] = _mx(u.T)
            r = jnp.maximum(u, 0.0)
            acc[...] += _dot(_mx(r * r), wd_ref[c])

        @pl.when(j == ns - 1)
        def _():
            xo, xhat, rstd = _ln_fwd_math(acc[...], g_ref[...], b_ref[...])
            xh_ref[...] = xhat
            rs_ref[...] = rstd
            if with_loss:
                d = xo - t_ref[...]
                dy_ref[...] = d * (1.0 / D)
                loss_ref[...] += (0.5 / D) * jnp.sum(jnp.sum(d * d, axis=1, keepdims=True), axis=0, keepdims=True)
            else:
                xob_ref[...] = _mx(xo)

    rows = lambda n: pl.BlockSpec((tm, n), lambda i, j: (i, 0))
    full = lambda shp: pl.BlockSpec(shp, lambda i, j: (0,) * len(shp))
    in_specs = [rows(D), rows(D), full(g_in.shape), full(b_in.shape),
                pl.BlockSpec((cps, None, D, fc), lambda i, j: (j, layer, 0, 0)),
                pl.BlockSpec((cps, None, fc, D), lambda i, j: (j, layer, 0, 0)), full(g.shape), full(b.shape)]
    out_specs = [pl.BlockSpec((tm, cps * fc), lambda i, j: (i, j)), pl.BlockSpec((cps * fc, tm), lambda i, j: (j, i)),
                 rows(D), rows(1)]
    out_shape = [S((T, nj * fc), _MXU_DTYPE), S((nj * fc, T), _MXU_DTYPE), S((T, D), F32), S((T, 1), F32)]
    args = [xb, xh_in, g_in, b_in, w_up, w_dn, g, b]
    if with_loss:
        in_specs.append(rows(D))
        args.append(target)
        out_specs += [pl.BlockSpec((1, 1), lambda i, j: (0, 0)), rows(D)]
        out_shape += [S((1, 1), F32), S((T, D), F32)]
    else:
        out_specs.append(rows(D))
        out_shape.append(S((T, D), _MXU_DTYPE))
    return pl.pallas_call(
        body, name=f"mlp_fwd_{layer}", grid=(T // tm, ns), in_specs=in_specs, out_specs=out_specs, out_shape=out_shape,
        scratch_shapes=[pltpu.VMEM((tm, D), F32)],
        compiler_params=_params("arbitrary", "arbitrary"))(*args)


MLP_BWD_DX_CHUNKS = 4


def _mlp_bwd_dx(dxo, xhat, rstd, g, u, w_up, w_dn, layer, *, tm=512):
    T = dxo.shape[0]
    tm = min(tm, T)
    nj, _, _, fc = w_up.shape
    cps = MLP_BWD_DX_CHUNKS
    ns = nj // cps

    def body(dxo_ref, xh_ref, rs_ref, g_ref, u_ref, wu_ref, wd_ref, dx_ref, du_ref, dyb_ref, dg_ref, db_ref, acc, dy_sc):
        i, j = pl.program_id(0), pl.program_id(1)

        @pl.when((i == 0) & (j == 0))
        def _():
            dg_ref[...] = jnp.zeros_like(dg_ref)
            db_ref[...] = jnp.zeros_like(db_ref)

        @pl.when(j == 0)
        def _():
            dz, dg, db = _ln_bwd_math(dxo_ref[...], xh_ref[...], rs_ref[...], g_ref[...])
            dg_ref[...] += dg
            db_ref[...] += db
            acc[...] = ALPHA * dz
            dy_sc[...] = _mx(dz)
            dyb_ref[...] = _mx(dz)

        dyb = dy_sc[...]
        for c in range(cps):
            cs = slice(fc * c, fc * (c + 1))
            r = jnp.maximum(u_ref[:, cs].astype(F32), 0.0)
            da = _dot(dyb, wd_ref[c], NT)
            dub = _mx(da * (2.0 * r))
            du_ref[:, cs] = dub
            acc[...] += _dot(dub, wu_ref[c], NT)

        @pl.when(j == ns - 1)
        def _():
            dx_ref[...] = acc[...]

    rows = lambda n: pl.BlockSpec((tm, n), lambda i, j: (i, 0))
    full = lambda shp: pl.BlockSpec(shp, lambda i, j: (0,) * len(shp))
    return pl.pallas_call(
        body, name=f"mlp_bwd_dx_{layer}", grid=(T // tm, ns),
        in_specs=[rows(D_MODEL), rows(D_MODEL), rows(1), full(g.shape),
                  pl.BlockSpec((tm, cps * fc), lambda i, j: (i, j)),
                  pl.BlockSpec((cps, None, D_MODEL, fc), lambda i, j: (j, layer, 0, 0)),
                  pl.BlockSpec((cps, None, fc, D_MODEL), lambda i, j: (j, layer, 0, 0))],
        out_specs=[rows(D_MODEL), pl.BlockSpec((tm, cps * fc), lambda i, j: (i, j)), rows(D_MODEL), full(g.shape),
                   full(g.shape)],
        out_shape=[S((T, D_MODEL), F32), S((T, nj * fc), _MXU_DTYPE), S((T, D_MODEL), _MXU_DTYPE),
                   S(g.shape, F32), S(g.shape, F32)],
        scratch_shapes=[pltpu.VMEM((tm, D_MODEL), F32), pltpu.VMEM((tm, D_MODEL), _MXU_DTYPE)],
        compiler_params=_params("arbitrary", "arbitrary"))(dxo, xhat, rstd, g, u, w_up, w_dn)


def _mlp_bwd_dw(ut, dyb, xt, du, layer, *, nj, other_layers=None, tm=1024):
    T = ut.shape[1]
    tm = min(tm, T)
    fc = ut.shape[0] // nj
    ni = T // tm

    def body(ut_ref, dy_ref, xt_ref, du_ref, *rest):
        gd_ref, gu_ref, gd_acc, gu_acc, sem = rest[-5:]
        i, j = pl.program_id(0), pl.program_id(1)

        @pl.when(i == 0)
        def _():
            gd_acc[j] = jnp.zeros((fc, D_MODEL), F32)
            gu_acc[j] = jnp.zeros((D_MODEL, fc), F32)

        r = jnp.maximum(ut_ref[...].astype(F32), 0.0)
        gd_acc[j] += _dot(_mx(r * r), dy_ref[...])
        gu_acc[j] += _dot(xt_ref[...], du_ref[...])

        @pl.when((i == ni - 1) & (j == nj - 1))
        def _():
            c1 = pltpu.make_async_copy(gd_acc, gd_ref.at[:, layer], sem.at[0])
            c2 = pltpu.make_async_copy(gu_acc, gu_ref.at[:, layer], sem.at[1])
            c1.start()
            c2.start()
            c1.wait()
            c2.wait()

    hbm = pl.BlockSpec(memory_space=pl.ANY)
    in_specs = [pl.BlockSpec((fc, tm), lambda i, j: (j, i)), pl.BlockSpec((tm, D_MODEL), lambda i, j: (i, 0)),
                pl.BlockSpec((D_MODEL, tm), lambda i, j: (0, i)), pl.BlockSpec((tm, fc), lambda i, j: (i, j))]
    args, aliases = [ut, dyb, xt, du], {}
    if other_layers is not None:
        in_specs += [hbm] * 2
        args += list(other_layers)
        aliases = {4: 0, 5: 1}
    return pl.pallas_call(
        body, name=f"mlp_bwd_dw_{layer}", grid=(ni, nj), in_specs=in_specs, out_specs=[hbm, hbm],
        out_shape=[S((nj, DEPTH, fc, D_MODEL), F32), S((nj, DEPTH, D_MODEL, fc), F32)],
        scratch_shapes=[pltpu.VMEM((nj, fc, D_MODEL), F32), pltpu.VMEM((nj, D_MODEL, fc), F32),
                        pltpu.SemaphoreType.DMA((2,))],
        input_output_aliases=aliases,
        compiler_params=_params("arbitrary", "arbitrary"))(*args)


SWA_GROUP = SWA_QH // SWA_KVH
SWA_ROWS = SWA_GROUP * SWA_BLOCK


def _swa_heads(a, kh):
    return jnp.concatenate([a[:, SWA_D * (kh * SWA_GROUP + g):SWA_D * (kh * SWA_GROUP + g + 1)]
                            for g in range(SWA_GROUP)], axis=0)


def _swa_operands(q, kvp, kvc, kh):
    dkv = SWA_KVH * SWA_D
    qg = _swa_heads(q, kh)
    kb = jnp.concatenate([kvp[:, SWA_D * kh:SWA_D * (kh + 1)], kvc[:, SWA_D * kh:SWA_D * (kh + 1)]], axis=0)
    vb = jnp.concatenate([kvp[:, dkv + SWA_D * kh:dkv + SWA_D * (kh + 1)],
                          kvc[:, dkv + SWA_D * kh:dkv + SWA_D * (kh + 1)]], axis=0)
    return qg, kb, vb, _dot(kb, qg, NT)


def _swa_softmax(s_raw, bias_ref, sink_ref, n, kh):
    cols = slice(kh * SWA_ROWS, (kh + 1) * SWA_ROWS)
    s = s_raw * (SWA_D ** -0.5) + bias_ref[jnp.minimum(n, 1), :, cols]
    sink = sink_ref[:, cols]
    m = jnp.maximum(jnp.max(s, axis=0, keepdims=True), sink)
    p, ps = jnp.exp(s - m), jnp.exp(sink - m)
    inv = 1.0 / (jnp.sum(p, axis=0, keepdims=True) + ps)
    return p * inv, ps * inv


def _swa_attn_fwd(qkv, bias, sinks):
    T = qkv.shape[0]
    blk = SWA_BLOCK
    nb = T // blk
    dq, dkv = SWA_QH * SWA_D, SWA_KVH * SWA_D

    def body(q_ref, kvp_ref, kvc_ref, bias_ref, sink_ref, o_ref):
        n = pl.program_id(0)
        q, kvp, kvc = q_ref[...], kvp_ref[...], kvc_ref[...]
        nxt = _swa_operands(q, kvp, kvc, 0)
        for kh in range(SWA_KVH):
            _, _, vb, s_raw = nxt
            if kh + 1 < SWA_KVH:
                nxt = _swa_operands(q, kvp, kvc, kh + 1)
            p, _ = _swa_softmax(s_raw, bias_ref, sink_ref, n, kh)
            og = _mx(_dot(_mx(p), vb, TN))
            for g in range(SWA_GROUP):
                hd = kh * SWA_GROUP + g
                o_ref[:, SWA_D * hd:SWA_D * (hd + 1)] = og[blk * g:blk * (g + 1), :]

    return pl.pallas_call(
        body, name="swa_attn_fwd", grid=(nb,),
        in_specs=[pl.BlockSpec((blk, dq), lambda n: (n, 0)),
                  pl.BlockSpec((blk, 2 * dkv), lambda n: (jnp.maximum(n - 1, 0), dq // (2 * dkv))),
                  pl.BlockSpec((blk, 2 * dkv), lambda n: (n, dq // (2 * dkv))),
                  pl.BlockSpec(bias.shape, lambda n: (0, 0, 0)), pl.BlockSpec(sinks.shape, lambda n: (0, 0))],
        out_specs=pl.BlockSpec((blk, dq), lambda n: (n, 0)), out_shape=S((T, dq), _MXU_DTYPE),
        compiler_params=_params("parallel"))(qkv, qkv, qkv, bias, sinks)


def _swa_attn_bwd(qkv, ob, do, bias, sinks):
    T = qkv.shape[0]
    blk = SWA_BLOCK
    nb = T // blk
    dq, dkv = SWA_QH * SWA_D, SWA_KVH * SWA_D

    def body(q_ref, kvp_ref, kvc_ref, o_ref, do_ref, bias_ref, sink_ref, dqkv_ref, dbias_ref, dsink_ref, carry):
        st = pl.program_id(0)
        n = nb - 1 - st

        @pl.when(st == 0)
        def _():
            carry[...] = jnp.zeros_like(carry)
            dbias_ref[...] = jnp.zeros_like(dbias_ref)
            dsink_ref[...] = jnp.zeros_like(dsink_ref)

        q, kvp, kvc = q_ref[...], kvp_ref[...], kvc_ref[...]
        ov, dov = o_ref[...], do_ref[...]
        ones = jnp.ones((8, SWA_D), F32)
        def operands(kh):
            qg, kb, vb, s_raw = _swa_operands(q, kvp, kvc, kh)
            dog = _swa_heads(dov, kh)
            dl = _dot(ones, dog * _swa_heads(ov, kh).astype(F32), NT, lax.Precision.HIGHEST)[0:1]
            dogb = _mx(dog)
            return qg, kb, s_raw, dl, dogb, _dot(vb, dogb, NT)

        nxt = operands(0)
        for kh in range(SWA_KVH):
            cols = slice(kh * SWA_ROWS, (kh + 1) * SWA_ROWS)
            qg, kb, s_raw, dl, dogb, dp = nxt
            if kh + 1 < SWA_KVH:
                nxt = operands(kh + 1)
            p, ps = _swa_softmax(s_raw, bias_ref, sink_ref, n, kh)
            ds = p * (dp - dl)
            dbias_ref[:, cols] += ds
            dsink_ref[0:1, cols] += -ps * dl
            dsb = _mx(ds * (SWA_D ** -0.5))
            dqg = _mx(_dot(dsb, kb, TN))
            for g in range(SWA_GROUP):
                hd = kh * SWA_GROUP + g
                dqkv_ref[:, SWA_D * hd:SWA_D * (hd + 1)] = dqg[blk * g:blk * (g + 1), :]
            dkb = _dot(dsb, qg)
            dvb = _dot(_mx(p), dogb)
            ko, vo = SWA_D * kh, dkv + SWA_D * kh
            dqkv_ref[:, dq + ko:dq + ko + SWA_D] = _mx(dkb[blk:, :] + carry[:, ko:ko + SWA_D])
            dqkv_ref[:, dq + vo:dq + vo + SWA_D] = _mx(dvb[blk:, :] + carry[:, vo:vo + SWA_D])
            carry[:, ko:ko + SWA_D] = dkb[:blk, :]
            carry[:, vo:vo + SWA_D] = dvb[:blk, :]

    rev = lambda s: nb - 1 - s
    return pl.pallas_call(
        body, name="swa_attn_bwd", grid=(nb,),
        in_specs=[pl.BlockSpec((blk, dq), lambda s: (rev(s), 0)),
                  pl.BlockSpec((blk, 2 * dkv), lambda s: (jnp.maximum(rev(s) - 1, 0), dq // (2 * dkv))),
                  pl.BlockSpec((blk, 2 * dkv), lambda s: (rev(s), dq // (2 * dkv))),
                  pl.BlockSpec((blk, dq), lambda s: (rev(s), 0)), pl.BlockSpec((blk, dq), lambda s: (rev(s), 0)),
                  pl.BlockSpec(bias.shape, lambda s: (0, 0, 0)), pl.BlockSpec(sinks.shape, lambda s: (0, 0))],
        out_specs=[pl.BlockSpec((blk, dq + 2 * dkv), lambda s: (rev(s), 0)),
                   pl.BlockSpec(bias.shape[1:], lambda s: (0, 0)), pl.BlockSpec((8, sinks.shape[1]), lambda s: (0, 0))],
        out_shape=[S((T, dq + 2 * dkv), _MXU_DTYPE), S(bias.shape[1:], F32), S((8, sinks.shape[1]), F32)],
        scratch_shapes=[pltpu.VMEM((blk, 2 * dkv), F32)],
        compiler_params=_params("arbitrary"))(qkv, qkv, qkv, ob, do, bias, sinks)


def _t5_onehot():
    i = jnp.arange(SWA_BLOCK)
    j = jnp.arange(2 * SWA_BLOCK)
    n = jnp.maximum(i[:, None] + SWA_BLOCK - j[None, :], 0)
    max_exact = REL_BUCKETS // 2
    nf = jnp.maximum(n, 1).astype(F32)
    large = max_exact + (jnp.log(nf / max_exact) / math.log(REL_MAX_DIST / max_exact)
                         * (REL_BUCKETS - max_exact)).astype(jnp.int32)
    large = jnp.minimum(large, REL_BUCKETS - 1)
    bucket = jnp.where(n < max_exact, n, large).reshape(-1)
    return (bucket[None, :] == jnp.arange(REL_BUCKETS)[:, None]).astype(F32)


def _exchange_copies(ins, outs, n_gather, send_sems, recv_sems, loc_sems):
    mx, my, mc = lax.axis_index("x"), lax.axis_index("y"), lax.axis_index("c")
    me = 4 * mx + 2 * my + mc
    copies = []
    for a in range(len(ins)):
        src = ins[a] if a < n_gather else ins[a].at[me]
        copies.append(pltpu.make_async_copy(src, outs[a].at[me], loc_sems.at[a]))
    for k in range(1, N_DEV):
        px, py, pc = mx ^ ((k >> 2) & 1), my ^ ((k >> 1) & 1), mc ^ (k & 1)
        peer = 4 * px + 2 * py + pc
        for a in range(len(ins)):
            src = ins[a] if a < n_gather else ins[a].at[peer]
            copies.append(pltpu.make_async_remote_copy(
                src_ref=src, dst_ref=outs[a].at[me], send_sem=send_sems.at[a, k - 1],
                recv_sem=recv_sems.at[a, k - 1], device_id=(px, py, pc), device_id_type=pl.DeviceIdType.MESH))
    return copies


def _exchange_shapes(gather, scatter):
    n_arr = len(gather) + len(scatter)
    out_shape = [S((N_DEV,) + tuple(g.shape), g.dtype) for g in gather] + [S(s.shape, s.dtype) for s in scatter]
    sems = [pltpu.SemaphoreType.DMA((n_arr, N_DEV - 1)), pltpu.SemaphoreType.DMA((n_arr, N_DEV - 1)),
            pltpu.SemaphoreType.DMA((n_arr,))]
    return out_shape, sems


def _exchange(gather, scatter, *, name):
    n_g = len(gather)
    n_arr = n_g + len(scatter)

    def body(*refs):
        copies = _exchange_copies(refs[:n_arr], refs[n_arr:2 * n_arr], n_g, *refs[2 * n_arr:])
        for cp in copies:
            cp.start()
        for cp in copies:
            cp.wait()

    hbm = pl.BlockSpec(memory_space=pl.ANY)
    out_shape, sems = _exchange_shapes(gather, scatter)
    return pl.pallas_call(
        body, name=name, in_specs=[hbm] * n_arr, out_specs=[hbm] * n_arr, out_shape=out_shape,
        scratch_shapes=sems)(*gather, *scatter)


def _gather_two_level(arrays, *, name):
    n = len(arrays)

    def body(*refs):
        ins, outs = refs[:n], refs[n:2 * n]
        send_sems, recv_sems, loc_sems = refs[2 * n:]
        x, y, c = lax.axis_index("x"), lax.axis_index("y"), lax.axis_index("c")
        me, sibling = (x, y, c), (x, y, 1 - c)
        chips = [(1 - x, y), (x, 1 - y), (1 - x, 1 - y)]

        def copy(a, k, block, to, src=None):
            dst = outs[a].at[4 * block[0] + 2 * block[1] + block[2]]
            return pltpu.make_async_remote_copy(
                src_ref=dst if src is None else src, dst_ref=dst, send_sem=send_sems.at[a, k],
                recv_sem=recv_sems.at[a, k], device_id=to, device_id_type=pl.DeviceIdType.MESH)

        mine = [pltpu.make_async_copy(ins[a], outs[a].at[4 * x + 2 * y + c], loc_sems.at[a]) for a in range(n)]
        first = [copy(a, 0, me, sibling, src=ins[a]) for a in range(n)]
        first += [copy(a, 1 + j, me, (*chip, c), src=ins[a]) for j, chip in enumerate(chips) for a in range(n)]
        for cp in mine + first:
            cp.start()
        passed = []
        for j, chip in enumerate(chips):
            for a in range(n):
                copy(a, 1 + j, (*chip, c), me).wait_recv()
                passed.append(copy(a, 4 + j, (*chip, c), sibling))
                passed[-1].start()
        for a in range(n):
            copy(a, 0, sibling, me).wait_recv()
        for j, chip in enumerate(chips):
            for a in range(n):
                copy(a, 4 + j, (*chip, 1 - c), me).wait_recv()
        for cp in first + passed:
            cp.wait_send()
        for cp in mine:
            cp.wait()

    hbm = pl.BlockSpec(memory_space=pl.ANY)
    out_shape, sems = _exchange_shapes(arrays, [])
    return pl.pallas_call(
        body, name=name, in_specs=[hbm] * n, out_specs=[hbm] * n, out_shape=out_shape, scratch_shapes=sems)(*arrays)


def _adamw(parts, w, m, v, *, name, tr=256):
    R, C = w.shape
    tr = min(tr, R)
    assert R % tr == 0

    def body(p_ref, w_ref, m_ref, v_ref, g_ref, d_ref, nm_ref, nv_ref):
        g = p_ref[0].astype(F32)
        for k in range(1, N_DEV):
            g = g + p_ref[k].astype(F32)
        g_ref[...] = g
        d_ref[...], nm_ref[...], nv_ref[...] = _adamw_math(g, w_ref[...], m_ref[...], v_ref[...])

    rows = pl.BlockSpec((tr, C), lambda i: (i, 0))
    return pl.pallas_call(
        body, name=name, grid=(R // tr,),
        in_specs=[pl.BlockSpec((N_DEV, tr, C), lambda i: (0, i, 0)), rows, rows, rows],
        out_specs=[rows] * 4, out_shape=[S((R, C), F32)] * 4,
        compiler_params=_params("parallel"))(parts, w, m, v)


def _adamw_math(g, w, m, v):
    m_new = ADAM_B1 * m + (1.0 - ADAM_B1) * g
    v_new = ADAM_B2 * v + (1.0 - ADAM_B2) * (g * g)
    m_hat = m_new / (1.0 - ADAM_B1 ** ADAM_STEP)
    v_hat = v_new / (1.0 - ADAM_B2 ** ADAM_STEP)
    return -ADAM_LR * (m_hat / (jnp.sqrt(v_hat) + ADAM_EPS) + ADAM_WD * w), m_new, v_new


SMALL_ROWS = 48
REPL = {"ln_mix_g": (slice(0, 2), slice(None)), "ln_mix_b": (slice(2, 4), slice(None)),
        "ln_mlp_g": (slice(4, 6), slice(None)), "ln_mlp_b": (slice(6, 8), slice(None)),
        "swa_sinks": (slice(8, 9), slice(0, SWA_QH)), "rel_bias": (slice(16, 16 + REL_BUCKETS), slice(0, SWA_QH))}
GAINS = {"mla_g_q": (slice(0, 1), slice(0, MLA_QR // N_DEV)),
         "mla_g_kv": (slice(0, 1), slice(MLA_QR // N_DEV, (MLA_QR + MLA_C) // N_DEV))}


LOSS_AT = (slice(8, 9), slice(SWA_QH, SWA_QH + 1))


def _adamw_small(r_all, p_gains, W, M, V):
    names = list(REPL) + list(GAINS)

    def body(r_ref, pg_ref, *refs):
        ins, outs, loss_ref = refs[:3 * len(names)], refs[3 * len(names):-1], refs[-1]
        r_sum, g_sum = r_ref[0], pg_ref[0]
        for k in range(1, N_DEV):
            r_sum, g_sum = r_sum + r_ref[k], g_sum + pg_ref[k]
        loss_ref[...] = r_sum[LOSS_AT]
        for i, n in enumerate(names):
            g = r_sum[REPL[n]] if n in REPL else g_sum[GAINS[n]]
            w_ref, m_ref, v_ref = ins[3 * i:3 * i + 3]
            g_ref, d_ref, nm_ref, nv_ref = outs[4 * i:4 * i + 4]
            g_ref[...] = g
            d_ref[...], nm_ref[...], nv_ref[...] = _adamw_math(g, w_ref[...], m_ref[...], v_ref[...])

    flat_in = [d[n] for n in names for d in (W, M, V)]
    res = pl.pallas_call(body, name="adamw_small",
                         out_shape=[S(W[n].shape, F32) for n in names for _ in range(4)] + [S((1, 1), F32)],
                         compiler_params=_params())(r_all, p_gains, *flat_in)
    return {(k, n): res[4 * i + k] for i, n in enumerate(names) for k in range(4)}, res[-1]


WEIGHTS = ["mla_w_in", "mla_g_q", "mla_g_kv", "mla_w_uq", "mla_w_uk", "mla_w_uv", "mla_w_o", "kv_w_shared",
           "swa_w_q", "swa_sinks", "swa_w_o", "rel_bias", "mlp_w_up", "mlp_w_down", "ln_mix_g", "ln_mix_b",
           "ln_mlp_g", "ln_mlp_b"]


def kernel(x, mla_w_in, mla_g_q, mla_g_kv, mla_w_uq, mla_w_uk, mla_w_uv, mla_w_o, kv_w_shared, swa_w_q, swa_sinks, swa_w_o, rel_bias, mlp_w_up, mlp_w_down, ln_mix_g, ln_mix_b, ln_mlp_g, ln_mlp_b, loss_target, m_mla_w_in, m_mla_g_q, m_mla_g_kv, m_mla_w_uq, m_mla_w_uk, m_mla_w_uv, m_mla_w_o, m_kv_w_shared, m_swa_w_q, m_swa_sinks, m_swa_w_o, m_rel_bias, m_mlp_w_up, m_mlp_w_down, m_ln_mix_g, m_ln_mix_b, m_ln_mlp_g, m_ln_mlp_b, v_mla_w_in, v_mla_g_q, v_mla_g_kv, v_mla_w_uq, v_mla_w_uk, v_mla_w_uv, v_mla_w_o, v_kv_w_shared, v_swa_w_q, v_swa_sinks, v_swa_w_o, v_rel_bias, v_mlp_w_up, v_mlp_w_down, v_ln_mix_g, v_ln_mix_b, v_ln_mlp_g, v_ln_mlp_b):
    args = dict(locals())
    W = {n: args[n] for n in WEIGHTS}
    M = {n: args["m_" + n] for n in WEIGHTS}
    V = {n: args["v_" + n] for n in WEIGHTS}
    T = x.shape[1]
    x2d = x.reshape(T, D_MODEL)
    tgt = loss_target.reshape(T, D_MODEL)
    H = MLA_HEADS

    SH = {"mla_w_in": (-1, mla_w_in.shape[-1]), "mla_w_uq": (-1, H * (MLA_NOPE + MLA_ROPE)),
          "mla_w_uk": (-1, H * MLA_NOPE), "mla_w_uv": (-1, H * MLA_V), "mla_w_o": (-1, D_MODEL),
          "kv_w_shared": (-1, kv_w_shared.shape[-1]), "swa_w_q": (-1, swa_w_q.shape[-1]), "swa_w_o": (-1, D_MODEL)}
    slab = lambda d, n: d[n].reshape(SH[n])
    bf = lambda a: a.astype(_MXU_DTYPE)
    gains_slab = lambda d: jnp.pad(jnp.concatenate([d["mla_g_q"], d["mla_g_kv"]], axis=1),
                                   ((0, 7), (0, 128 - d["mla_g_q"].shape[1] - d["mla_g_kv"].shape[1])))
    n_gq, n_gkv = mla_g_q.shape[1], mla_g_kv.shape[1]
    w_in_s, w_uq_s, w_uk_s, gains_all = _gather_two_level(
        [bf(slab(W, "mla_w_in")), bf(slab(W, "mla_w_uq")), bf(slab(W, "mla_w_uk")), gains_slab(W)],
        name="gather_mla_in")
    later = [bf(slab(W, n)) for n in ("mla_w_uv", "mla_w_o", "kv_w_shared", "swa_w_q", "swa_w_o")]
    later += [bf(mlp_w_up), bf(mlp_w_down)]
    w_in = w_in_s.reshape(D_MODEL, -1)
    g_q = gains_all[:, 0, :n_gq].reshape(1, MLA_QR)
    g_kv = gains_all[:, 0, n_gq:n_gq + n_gkv].reshape(1, MLA_C)
    w_uq = w_uq_s.reshape(MLA_QR, H, MLA_NOPE + MLA_ROPE)
    w_uq_n = w_uq[:, :, :MLA_NOPE].reshape(MLA_QR, H * MLA_NOPE)
    w_uq_r = w_uq[:, :, MLA_NOPE:].reshape(MLA_QR, H * MLA_ROPE)
    w_uk = w_uk_s.reshape(MLA_C, H, MLA_NOPE).transpose(1, 0, 2)
    w_uk_t = w_uk.transpose(0, 2, 1)
    ln = lambda a, l: a[l].reshape(1, D_MODEL)

    half = MLA_ROPE // 2
    inv = ROPE_THETA ** (-jnp.arange(half, dtype=F32) / half)
    ang = jnp.arange(T, dtype=F32)[:, None] * inv[None, :]
    cos = jnp.tile(jnp.concatenate([jnp.cos(ang), jnp.cos(ang)], -1), (1, ROPE_TABLE_W // MLA_ROPE))
    sin = jnp.tile(jnp.concatenate([-jnp.sin(ang), jnp.sin(ang)], -1), (1, ROPE_TABLE_W // MLA_ROPE))

    h, kc, kct, qs, qst, cq, qn = _mla_pre_fwd(x2d, w_in, g_q, g_kv, w_uq_n, w_uq_r, w_uk_t, cos, sin)
    olat, lse, (w_uv_s, w_o_s, w_kv_s, w_q_s, w_o2_s, w_up, w_dn) = _mla_attn_fwd(qs, kc, kct, gather=later)
    w_uv = w_uv_s.reshape(MLA_C, H, MLA_V).transpose(1, 0, 2)
    w_o = w_o_s.reshape(H * MLA_V, D_MODEL)
    w_qkv = jnp.concatenate([w_q_s.reshape(D_MODEL, -1), w_kv_s.reshape(D_MODEL, -1)], axis=1)
    w_o2 = w_o2_s.reshape(SWA_QH * SWA_D, D_MODEL)
    o_mla = _mla_uv_fwd(olat, w_uv)
    mix0, mlp0 = (ln(ln_mix_g, 0), ln(ln_mix_b, 0)), (ln(ln_mlp_g, 0), ln(ln_mlp_b, 0))
    mix1, mlp1 = (ln(ln_mix_g, 1), ln(ln_mix_b, 1)), (ln(ln_mlp_g, 1), ln(ln_mlp_b, 1))
    x1b, x1t, xh1, rs1 = _proj_ln_fwd(o_mla, w_o, x2d, None, *mix0, name="mla_out_ln_fwd")
    u0, u0t, xh2, rs2, x2b = _mlp_fwd(x1b, xh1, *mix0, w_up, w_dn, 0, *mlp0)
    onehot = _t5_onehot()
    bias = _mm(rel_bias.T, onehot, name="rel_bias_expand", precision=lax.Precision.HIGHEST, tn=8192).reshape(
        SWA_QH * SWA_BLOCK, 2 * SWA_BLOCK).T
    key = jnp.arange(2 * SWA_BLOCK)[:, None]
    qry = jnp.arange(SWA_QH * SWA_BLOCK)[None, :] % SWA_BLOCK
    in_window = (key > qry) & (key <= qry + SWA_BLOCK)
    bias = jnp.stack([jnp.where(in_window & (key >= SWA_BLOCK), bias, -jnp.inf), jnp.where(in_window, bias, -jnp.inf)])
    sink_rows = jnp.repeat(swa_sinks.reshape(SWA_QH), SWA_BLOCK).reshape(1, SWA_QH * SWA_BLOCK)
    qkv = _mm(x2b, w_qkv, name="swa_qkv_fwd", out_dtype=_MXU_DTYPE, tm=1024, tn=512, tk=1024)
    o_swa = _swa_attn_fwd(qkv, bias, sink_rows)
    x3b, x3t, xh3, rs3 = _proj_ln_fwd(o_swa, w_o2, xh2, mlp0, *mix1, name="swa_out_ln_fwd")
    u1, u1t, xh4, rs4, loss_part, dx4 = _mlp_fwd(x3b, xh3, *mix1, w_up, w_dn, 1, *mlp1, tgt)

    nj = w_up.shape[0]
    dx3, du1, dy4b, dg_mlp1, db_mlp1 = _mlp_bwd_dx(dx4, xh4, rs4, ln(ln_mlp_g, 1), u1, w_up, w_dn, 1)
    g_dn_last, g_up_last = _mlp_bwd_dw(u1t, dy4b, x3t, du1, 1, nj=nj)
    dres3, do_swa, g_o2, dg_mix1, db_mix1 = _proj_ln_bwd(dx3, xh3, rs3, ln(ln_mix_g, 1), o_swa, w_o2,
                                                         name="swa_out_ln_bwd")
    dqkv, dbias, dsink = _swa_attn_bwd(qkv, o_swa, do_swa, bias, sink_rows)
    g_rel = _mm(onehot, dbias.T.reshape(SWA_QH, -1), name="rel_bias_grad", tb=True, precision=lax.Precision.HIGHEST,
                tk=8192)
    head_of_row = (jnp.arange(SWA_QH * SWA_BLOCK)[:, None] // SWA_BLOCK == jnp.arange(SWA_QH)[None, :]).astype(F32)
    g_sinks = _mm(dsink, head_of_row, name="sinks_grad", precision=lax.Precision.HIGHEST, tk=2048)[0:1]
    dx2 = _mm(dqkv, w_qkv, name="swa_qkv_bwd_dx", tb=True, add=dres3, tm=1024, tn=1024, tk=1536)
    g_qkv = _mm(x2b, dqkv, name="swa_qkv_bwd_dw", ta=True, tm=1024, tn=512, tk=1024)
    dx1, du0, dy2b, dg_mlp0, db_mlp0 = _mlp_bwd_dx(dx2, xh2, rs2, ln(ln_mlp_g, 0), u0, w_up, w_dn, 0)
    g_dn, g_up = _mlp_bwd_dw(u0t, dy2b, x1t, du0, 0, nj=nj, other_layers=(g_dn_last, g_up_last))
    dres1, do_mla, g_o, dg_mix0, db_mix0 = _proj_ln_bwd(dx1, xh1, rs1, ln(ln_mix_g, 0), o_mla, w_o,
                                                        name="mla_out_ln_bwd", da_dtype=_MXU_DTYPE)
    dol, delta, g_uv = _mla_uv_bwd(do_mla, olat, w_uv)
    wide = lambda a, rows: jnp.pad(a, ((0, rows - a.shape[0]), (0, LANES - a.shape[1])))
    r_part = jnp.concatenate([dg_mix0, dg_mix1, db_mix0, db_mix1, dg_mlp0, dg_mlp1, db_mlp0, db_mlp1,
                              wide(jnp.concatenate([g_sinks, loss_part], axis=1), 8), wide(g_rel, SMALL_ROWS - 16)],
                             axis=0)
    by_dev = lambda g: g.reshape((N_DEV, g.shape[0] // N_DEV) + g.shape[1:])
    early = [by_dev(g_o2), by_dev(g_qkv), g_up, g_dn, by_dev(g_o),
             by_dev(g_uv.transpose(1, 0, 2).reshape(MLA_C, H * MLA_V))]
    dqs, dkc, dv, (r_all, p_o2, p_qkv, p_up, p_dn, p_o, p_uv) = _mla_attn_bwd(
        qs, qst, kc, kct, dol, lse, delta, gather=[r_part], scatter=early)
    grad_x, g_in, g_uq_n, g_uq_r, g_uk, g_gq, g_gkv = _mla_pre_bwd(
        dqs, dkc, dv, h, x2d, dres1, cq, qn, cos, sin, w_in, g_q, g_kv, w_uq_n, w_uq_r, w_uk)
    g_uq = jnp.concatenate([g_uq_n.reshape(MLA_QR, H, MLA_NOPE), g_uq_r.reshape(MLA_QR, H, MLA_ROPE)], -1)
    g_gains = jnp.pad(jnp.concatenate([g_gq.reshape(N_DEV, n_gq), g_gkv.reshape(N_DEV, n_gkv)], axis=1)[:, None, :],
                      ((0, 0), (0, 7), (0, 128 - n_gq - n_gkv)))
    p_in, p_uq, p_uk, p_gains = _exchange(
        [], [bf(by_dev(g_in)), bf(by_dev(g_uq.reshape(MLA_QR, -1))),
             bf(by_dev(g_uk.transpose(1, 0, 2).reshape(MLA_C, -1))), g_gains], name="exchange_mla_in_grads")

    res = {}

    def adam(name, parts, names, to_slab, from_slab):
        out = _adamw(parts, to_slab(W), to_slab(M), to_slab(V), name="adamw_" + name)
        for k in range(4):
            for n, a in zip(names, from_slab(out[k])):
                res[(k, n)] = a.reshape(W[n].shape)

    one = lambda n: (lambda d: slab(d, n))
    adam("swa_w_o", p_o2, ["swa_w_o"], one("swa_w_o"), lambda s: [s])
    dq_cols = SWA_QH * SWA_D
    adam("swa_qkv", p_qkv, ["swa_w_q", "kv_w_shared"],
         lambda d: jnp.concatenate([slab(d, "swa_w_q"), slab(d, "kv_w_shared")], axis=1),
         lambda s: [s[:, :dq_cols], s[:, dq_cols:]])
    layers_as_rows = lambda a: a.reshape((-1,) + a.shape[-1:])
    adam("mlp_w_up", p_up.reshape(N_DEV, -1, p_up.shape[-1]), ["mlp_w_up"],
         lambda d: layers_as_rows(d["mlp_w_up"]), lambda s: [s])
    adam("mlp_w_down", p_dn.reshape(N_DEV, -1, p_dn.shape[-1]), ["mlp_w_down"],
         lambda d: layers_as_rows(d["mlp_w_down"]), lambda s: [s])
    adam("mla_w_o", p_o, ["mla_w_o"], one("mla_w_o"), lambda s: [s])
    adam("mla_w_uv", p_uv, ["mla_w_uv"], one("mla_w_uv"), lambda s: [s])
    adam("mla_w_in", p_in, ["mla_w_in"], one("mla_w_in"), lambda s: [s])
    adam("mla_w_uq", p_uq, ["mla_w_uq"], one("mla_w_uq"), lambda s: [s])
    adam("mla_w_uk", p_uk, ["mla_w_uk"], one("mla_w_uk"), lambda s: [s])
    small, loss = _adamw_small(r_all, p_gains, W, M, V)
    res.update(small)
    loss = loss.reshape(())
    return (loss, grad_x.reshape(x.shape), *[res[(k, n)] for k in range(4) for n in WEIGHTS])
```

```python
import math

import numpy as np
import jax
import jax.numpy as jnp
from jax import lax
from jax.experimental import pallas as pl
from jax.experimental.pallas import tpu as pltpu

F32 = jnp.float32
_MXU_DTYPE = jnp.bfloat16

D_MODEL = 1024
DEPTH = 2
MLA_HEADS = 8
MLA_NOPE = 128
MLA_ROPE = 64
MLA_V = 128
MLA_QR = 384
MLA_C = 256
MLA_DK = 384
MLA_DT = MLA_C + MLA_ROPE
ROPE_THETA = 10000.0
SWA_QH = 16
SWA_KVH = 4
SWA_D = 64
SWA_BLOCK = 128
REL_BUCKETS = 32
REL_MAX_DIST = 128
D_FF = 4096
LN_EPS = 1e-5
RMS_EPS = 1e-6
ALPHA = (2 * DEPTH) ** 0.25
ADAM_LR, ADAM_B1, ADAM_B2, ADAM_EPS, ADAM_WD, ADAM_STEP = 0.001, 0.9, 0.999, 1e-08, 0.01, 10

N_DEV = 8
AXES = ("x", "y", "c")
V7X_VMEM_BYTES = 64 * 1024 * 1024
VMEM_LIMIT = V7X_VMEM_BYTES - 8 * 1024 * 1024
LANES = 1024
ATT_TQ = 512
ATT_TK = 512
ATT_HEAD_GROUP = 1
ATT_FWD_HEAD_GROUP = 2

NT = (((1,), (1,)), ((), ()))
TN = (((0,), (0,)), ((), ()))
S = jax.ShapeDtypeStruct


def _params(*sem, vmem=VMEM_LIMIT):
    return pltpu.CompilerParams(dimension_semantics=sem, vmem_limit_bytes=vmem)


def _dot(a, b, dims=None, precision=None):
    if dims is None:
        return jnp.dot(a, b, preferred_element_type=F32, precision=precision)
    return lax.dot_general(a, b, dims, preferred_element_type=F32, precision=precision)


def _mx(v):
    return v.astype(_MXU_DTYPE)


ROPE_TABLE_W = 128


def _tile_heads(t):
    return jnp.concatenate([t] * (MLA_HEADS * MLA_ROPE // ROPE_TABLE_W), axis=1)


def _swap_halves_64(v):
    return jnp.concatenate([v[:, 32:], v[:, :32]], axis=-1)


def _swap_halves_groups(v):
    n = v.shape[-1]
    lane = lax.broadcasted_iota(jnp.int32, v.shape, 1)
    return jnp.where(lane % 64 < 32, pltpu.roll(v, n - 32, 1), pltpu.roll(v, 32, 1))


def _mm(a, b, *, name, ta=False, tb=False, add=None, out_dtype=F32, tm=512, tn=512, tk=512, precision=None):
    M, K = (a.shape[1], a.shape[0]) if ta else a.shape
    N = b.shape[0] if tb else b.shape[1]
    tm, tn, tk = min(tm, M), min(tn, N), min(tk, K)
    assert M % tm == 0 and N % tn == 0 and K % tk == 0, (M, N, K, tm, tn, tk)
    nk = K // tk
    dims = (((0 if ta else 1,), (1 if tb else 0,)), ((), ()))
    has_add = add is not None

    def body(*refs):
        if has_add:
            a_ref, b_ref, add_ref, o_ref, acc = refs
        else:
            a_ref, b_ref, o_ref, acc = refs
        k = pl.program_id(2)
        av, bv = a_ref[...], b_ref[...]
        if precision is None:
            av, bv = _mx(av), _mx(bv)
        part = _dot(av, bv, dims, precision)
        if nk == 1:
            o_ref[...] = (part + add_ref[...] if has_add else part).astype(out_dtype)
            return

        @pl.when(k == 0)
        def _():
            acc[...] = add_ref[...] if has_add else jnp.zeros_like(acc)

        acc[...] += part

        @pl.when(k == nk - 1)
        def _():
            o_ref[...] = acc[...].astype(out_dtype)

    a_spec = pl.BlockSpec((tk, tm), lambda i, j, k: (k, i)) if ta else pl.BlockSpec((tm, tk), lambda i, j, k: (i, k))
    b_spec = pl.BlockSpec((tn, tk), lambda i, j, k: (j, k)) if tb else pl.BlockSpec((tk, tn), lambda i, j, k: (k, j))
    in_specs = [a_spec, b_spec]
    args = [a, b]
    if has_add:
        in_specs.append(pl.BlockSpec((tm, tn), lambda i, j, k: (i, j)))
        args.append(add)
    return pl.pallas_call(
        body, name=name, grid=(M // tm, N // tn, nk), in_specs=in_specs,
        out_specs=pl.BlockSpec((tm, tn), lambda i, j, k: (i, j)), out_shape=S((M, N), out_dtype),
        scratch_shapes=[pltpu.VMEM((tm, tn), F32)],
        compiler_params=_params("parallel", "parallel", "arbitrary"))(*args)


def _ln_fwd_math(z, g, b):
    mu = jnp.mean(z, axis=-1, keepdims=True)
    zc = z - mu
    var = jnp.mean(zc * zc, axis=-1, keepdims=True)
    rstd = lax.rsqrt(var + LN_EPS)
    xhat = zc * rstd
    return xhat * g + b, xhat, rstd


def _ln_bwd_math(dxo, xhat, rstd, g):
    dxh = dxo * g
    m1 = jnp.mean(dxh, axis=-1, keepdims=True)
    m2 = jnp.mean(dxh * xhat, axis=-1, keepdims=True)
    dz = rstd * (dxh - m1 - xhat * m2)
    dg = jnp.sum(dxo * xhat, axis=0, keepdims=True)
    db = jnp.sum(dxo, axis=0, keepdims=True)
    return dz, dg, db


def _rms_fwd_math(xr, g):
    r = lax.rsqrt(jnp.mean(xr * xr, axis=-1, keepdims=True) + RMS_EPS)
    return xr * r * g


def _rms_bwd_math(dy, xr, g):
    r = lax.rsqrt(jnp.mean(xr * xr, axis=-1, keepdims=True) + RMS_EPS)
    gy = dy * g
    dx = r * gy - xr * (r * r * r) * jnp.mean(gy * xr, axis=-1, keepdims=True)
    dg = jnp.sum(dy * xr * r, axis=0, keepdims=True)
    return dx, dg


def _mla_pre_fwd(x, w_in, g_q, g_kv, w_uq_n, w_uq_r, w_uk_t, cos, sin):
    T = x.shape[0]
    tm = min(ATT_TQ, T)
    nq = T // tm
    H = MLA_HEADS

    tk = min(ATT_TK, T)

    def body(x_ref, win_ref, gq_ref, gkv_ref, wn_ref, wr_ref, wuk_ref, cos_ref, sin_ref,
             h_ref, kc_ref, kct_ref, qs_ref, qst_ref, cq_ref, qn_ref):
        h = _dot(_mx(x_ref[...]), win_ref[...])
        h_ref[...] = h
        cos_v, sin_v = _tile_heads(cos_ref[...]), _tile_heads(sin_ref[...])
        cq = _mx(_rms_fwd_math(h[:, :MLA_QR], gq_ref[...]))
        ckv = _rms_fwd_math(h[:, MLA_QR:MLA_QR + MLA_C], gkv_ref[...])
        krr = h[:, MLA_QR + MLA_C:]
        kr = krr * cos_v[:, :MLA_ROPE] + _swap_halves_64(krr) * sin_v[:, :MLA_ROPE]
        kr_pad = jnp.concatenate([kr, jnp.zeros((tm, MLA_DK - MLA_C - MLA_ROPE), F32)], axis=1)
        kc_ref[:, 0:MLA_C] = _mx(ckv)
        kc_ref[:, MLA_C:] = _mx(kr_pad)
        kct_ref[0:MLA_C, :] = _mx(ckv.T)
        kct_ref[MLA_C:, :] = _mx(kr_pad.T[0:MLA_ROPE, :])
        cq_ref[...] = cq
        qnb = _mx(_dot(cq, wn_ref[...]))
        qn_ref[...] = qnb
        qr = _dot(cq, wr_ref[...])
        qrr = qr * cos_v + _swap_halves_groups(qr) * sin_v
        qrr_t = qrr.T
        for hd in range(H):
            ql = _dot(qnb[:, MLA_NOPE * hd:MLA_NOPE * (hd + 1)], wuk_ref[hd])
            qst_ref[0, 0:MLA_C, tm * hd:tm * (hd + 1)] = _mx(ql.T)
            qst_ref[0, MLA_C:, tm * hd:tm * (hd + 1)] = _mx(qrr_t[MLA_ROPE * hd:MLA_ROPE * (hd + 1), :])
            qs_ref[0, hd, :, 0:MLA_C] = _mx(ql)
            qs_ref[0, hd, :, MLA_C:MLA_C + MLA_ROPE] = _mx(qrr[:, MLA_ROPE * hd:MLA_ROPE * (hd + 1)])
            qs_ref[0, hd, :, MLA_C + MLA_ROPE:] = jnp.zeros((tm, MLA_DK - MLA_C - MLA_ROPE), _MXU_DTYPE)

    full = lambda shp: pl.BlockSpec(shp, lambda i: (0,) * len(shp))
    rows = lambda n: pl.BlockSpec((tm, n), lambda i: (i, 0))
    n_in = w_in.shape[1]
    return pl.pallas_call(
        body, name="mla_pre_fwd", grid=(nq,),
        in_specs=[rows(D_MODEL), full(w_in.shape), full(g_q.shape), full(g_kv.shape), full(w_uq_n.shape),
                  full(w_uq_r.shape), full(w_uk_t.shape), rows(ROPE_TABLE_W), rows(ROPE_TABLE_W)],
        out_specs=[rows(n_in), rows(MLA_DK),
                   pl.BlockSpec((None, MLA_DT, tm), lambda i: (i * tm // tk, 0, i % (tk // tm))),
                   pl.BlockSpec((1, H, tm, MLA_DK), lambda i: (i, 0, 0, 0)),
                   pl.BlockSpec((1, MLA_DT, H * tm), lambda i: (i, 0, 0)), rows(MLA_QR), rows(H * MLA_NOPE)],
        out_shape=[S((T, n_in), F32), S((T, MLA_DK), _MXU_DTYPE), S((T // tk, MLA_DT, tk), _MXU_DTYPE),
                   S((nq, H, tm, MLA_DK), _MXU_DTYPE), S((nq, MLA_DT, H * tm), _MXU_DTYPE),
                   S((T, MLA_QR), _MXU_DTYPE), S((T, H * MLA_NOPE), _MXU_DTYPE)],
        compiler_params=_params("parallel"))(x, w_in, g_q, g_kv, w_uq_n, w_uq_r, w_uk_t, cos, sin)


def _att_steps(T, tq, tk):
    qi, kj = [], []
    for i in range(T // tq):
        for j in range((i * tq + tq - 1) // tk + 1):
            qi.append(i)
            kj.append(j)
    return jnp.asarray(np.array(qi, np.int32)), jnp.asarray(np.array(kj, np.int32))


def _ride_exchange(st, n_steps, ins, outs, n_gather, sems):
    if not ins:
        return

    @pl.when(st == 0)
    def _():
        for cp in _exchange_copies(ins, outs, n_gather, *sems):
            cp.start()

    @pl.when(st == n_steps - 1)
    def _():
        for cp in _exchange_copies(ins, outs, n_gather, *sems):
            cp.wait()


def _mla_attn_fwd(qs, kc, kct, gather=(), scatter=()):
    nq, H, tq, DK = qs.shape
    T = kc.shape[0]
    tk = min(ATT_TK, T)
    scale = (MLA_NOPE + MLA_ROPE) ** -0.5
    c2 = scale * math.log2(math.e)
    qi, kj = _att_steps(T, tq, tk)
    n_steps = int(qi.shape[0])
    hg = ATT_FWD_HEAD_GROUP
    R = hg * tq
    n_x = len(gather) + len(scatter)

    def body(qi_ref, kj_ref, q_ref, k_ref, kt_ref, *rest):
        x_ins, (o_ref, lse_ref), x_outs = rest[:n_x], rest[n_x:n_x + 2], rest[n_x + 2:2 * n_x + 2]
        m_sc, l_sc, acc_sc = rest[2 * n_x + 2:2 * n_x + 5]
        st = pl.program_id(0)
        _ride_exchange(st, n_steps, x_ins, x_outs, len(gather), rest[2 * n_x + 5:])
        i, j = qi_ref[st], kj_ref[st]
        j_last = (i * tq + tq - 1) // tk

        @pl.when(j == 0)
        def _():
            m_sc[...] = jnp.full_like(m_sc, -jnp.inf)
            l_sc[...] = jnp.zeros_like(l_sc)
            acc_sc[...] = jnp.zeros_like(acc_sc)

        def step(masked):
            k = k_ref[...]
            vt = kt_ref[0:MLA_C, :]
            if masked:
                key = lax.broadcasted_iota(jnp.int32, (tk, R), 0) + j * tk
                qry = lax.broadcasted_iota(jnp.int32, (tk, R), 1) % tq + i * tq
                causal = key <= qry
            n_g = H // hg
            qk = lambda g: _dot(k, q_ref[0, g * hg:(g + 1) * hg].reshape(R, DK), NT)
            def accumulate(g, a, pb):
                cs = slice(g * R, (g + 1) * R)
                acc_sc[:, cs] = a * acc_sc[:, cs] + _dot(vt, pb)

            s_next = qk(0)
            pending = None
            for g in range(n_g):
                cs = slice(g * R, (g + 1) * R)
                s = s_next
                if g + 1 < n_g:
                    s_next = qk(g + 1)
                if pending is not None:
                    accumulate(*pending)
                if masked:
                    s = jnp.where(causal, s, -jnp.inf)
                m_prev = m_sc[:, cs]
                m_new = jnp.maximum(m_prev, jnp.max(s, axis=0, keepdims=True))
                a = jnp.exp2((m_prev - m_new) * c2)
                p = jnp.exp2((s - m_new) * c2)
                l_sc[:, cs] = a * l_sc[:, cs] + jnp.sum(p, axis=0, keepdims=True)
                m_sc[:, cs] = m_new
                pending = (g, a, _mx(p))
            accumulate(*pending)

        pl.when(j == j_last)(lambda: step(True))
        pl.when(j != j_last)(lambda: step(False))

        @pl.when(j == j_last)
        def _():
            o_ref[0] = _mx(acc_sc[...] / l_sc[...])
            lse_ref[0] = m_sc[...] * scale + jnp.log(l_sc[...])

    hbm = pl.BlockSpec(memory_space=pl.ANY)
    x_shapes, x_sems = _exchange_shapes(gather, scatter) if n_x else ([], [])
    gs = pltpu.PrefetchScalarGridSpec(
        num_scalar_prefetch=2, grid=(n_steps,),
        in_specs=[pl.BlockSpec((1, H, tq, DK), lambda s, qi, kj: (qi[s], 0, 0, 0)),
                  pl.BlockSpec((tk, DK), lambda s, qi, kj: (kj[s], 0)),
                  pl.BlockSpec((None, MLA_DT, tk), lambda s, qi, kj: (kj[s], 0, 0))] + [hbm] * n_x,
        out_specs=[pl.BlockSpec((1, MLA_C, H * tq), lambda s, qi, kj: (qi[s], 0, 0)),
                   pl.BlockSpec((1, 1, H * tq), lambda s, qi, kj: (qi[s], 0, 0))] + [hbm] * n_x,
        scratch_shapes=[pltpu.VMEM((1, H * tq), F32), pltpu.VMEM((1, H * tq), F32),
                        pltpu.VMEM((MLA_C, H * tq), F32)] + x_sems)
    res = pl.pallas_call(
        body, name="mla_attn_fwd", grid_spec=gs,
        out_shape=[S((nq, MLA_C, H * tq), _MXU_DTYPE), S((nq, 1, H * tq), F32)] + x_shapes,
        compiler_params=_params("arbitrary"))(qi, kj, qs, kc, kct, *gather, *scatter)
    return res[0], res[1], res[2:]


def _mla_attn_bwd(qs, qst, kc, kct, dol, lse, delta, gather=(), scatter=()):
    nq, H, tq, DK = qs.shape
    T = kc.shape[0]
    tk = min(ATT_TK, T)
    scale = (MLA_NOPE + MLA_ROPE) ** -0.5
    log2e = math.log2(math.e)
    qi, kj = _att_steps(T, tq, tk)
    n_steps = int(qi.shape[0])
    hg = ATT_HEAD_GROUP
    R = hg * tq
    n_x = len(gather) + len(scatter)

    def body(qi_ref, kj_ref, q_ref, qt_ref, k_ref, kt_ref, do_ref, lse_ref, dl_ref, *rest):
        x_ins, (dq_ref, dk_ref, dv_ref), x_outs = rest[:n_x], rest[n_x:n_x + 3], rest[n_x + 3:2 * n_x + 3]
        dk_acc, dv_acc, sem = rest[2 * n_x + 3:2 * n_x + 6]
        st = pl.program_id(0)
        _ride_exchange(st, n_steps, x_ins, x_outs, len(gather), rest[2 * n_x + 6:])
        i, j = qi_ref[st], kj_ref[st]
        j_last = (i * tq + tq - 1) // tk

        @pl.when(st == 0)
        def _():
            dk_acc[...] = jnp.zeros_like(dk_acc)
            dv_acc[...] = jnp.zeros_like(dv_acc)

        @pl.when(j == 0)
        def _():
            dq_ref[...] = jnp.zeros_like(dq_ref)

        def step(masked):
            k, kt = k_ref[...], kt_ref[...]
            v = k[:, :MLA_C]
            if masked:
                key = lax.broadcasted_iota(jnp.int32, (tk, R), 0) + j * tk
                qry = lax.broadcasted_iota(jnp.int32, (tk, R), 1) % tq + i * tq
                causal = key <= qry
            dkt_c = jnp.zeros((MLA_DT, tk), F32)
            dvt_c = jnp.zeros((MLA_C, tk), F32)
            n_g = H // hg

            def scores(g):
                q = q_ref[0, g * hg:(g + 1) * hg].reshape(R, DK)
                dot = do_ref[0, :, g * R:(g + 1) * R]
                return dot, _dot(k, q, NT), _dot(v, dot)

            nxt = scores(0)
            for g in range(n_g):
                cs = slice(g * R, (g + 1) * R)
                dot, s, dp = nxt
                if g + 1 < n_g:
                    nxt = scores(g + 1)
                p = jnp.exp2(s * (scale * log2e) - lse_ref[0, :, cs] * log2e)
                if masked:
                    p = jnp.where(causal, p, 0.0)
                dsb = _mx(p * (dp - dl_ref[0, :, cs]))
                dq_ref[0, :, cs] += _dot(kt, dsb)
                dkt_c = dkt_c + _dot(qt_ref[0, :, cs], dsb, NT)
                dvt_c = dvt_c + _dot(dot, _mx(p), NT)
            dk_acc[j] += dkt_c * scale
            dv_acc[j] += dvt_c

        pl.when(j == j_last)(lambda: step(True))
        pl.when(j != j_last)(lambda: step(False))

        @pl.when(j == j_last)
        def _():
            dq_ref[...] = dq_ref[...] * scale

        @pl.when(st == n_steps - 1)
        def _():
            c1 = pltpu.make_async_copy(dk_acc, dk_ref, sem.at[0])
            c2 = pltpu.make_async_copy(dv_acc, dv_ref, sem.at[1])
            c1.start()
            c2.start()
            c1.wait()
            c2.wait()

    cols = lambda n: pl.BlockSpec((1, n, H * tq), lambda s, qi, kj: (qi[s], 0, 0))
    hbm = pl.BlockSpec(memory_space=pl.ANY)
    x_shapes, x_sems = _exchange_shapes(gather, scatter) if n_x else ([], [])
    gs = pltpu.PrefetchScalarGridSpec(
        num_scalar_prefetch=2, grid=(n_steps,),
        in_specs=[pl.BlockSpec((1, H, tq, DK), lambda s, qi, kj: (qi[s], 0, 0, 0)), cols(MLA_DT),
                  pl.BlockSpec((tk, DK), lambda s, qi, kj: (kj[s], 0)),
                  pl.BlockSpec((None, MLA_DT, tk), lambda s, qi, kj: (kj[s], 0, 0)),
                  cols(MLA_C), cols(1), cols(1)] + [hbm] * n_x,
        out_specs=[cols(MLA_DT), hbm, hbm] + [hbm] * n_x,
        scratch_shapes=[pltpu.VMEM((T // tk, MLA_DT, tk), F32), pltpu.VMEM((T // tk, MLA_C, tk), F32),
                        pltpu.SemaphoreType.DMA((2,))] + x_sems)
    res = pl.pallas_call(
        body, name="mla_attn_bwd", grid_spec=gs,
        out_shape=[S((nq, MLA_DT, H * tq), F32), S((T // tk, MLA_DT, tk), F32),
                   S((T // tk, MLA_C, tk), F32)] + x_shapes,
        compiler_params=_params("arbitrary"))(qi, kj, qs, qst, kc, kct, dol, lse, delta, *gather, *scatter)
    return res[0], res[1], res[2], res[3:]


def _mla_uv_fwd(olat, w_uv):
    nq, C, cols = olat.shape
    H = w_uv.shape[0]
    tq = cols // H

    def body(ol_ref, wuv_ref, o_ref):
        for hd in range(H):
            o_ref[:, MLA_V * hd:MLA_V * (hd + 1)] = _mx(_dot(ol_ref[0, :, tq * hd:tq * (hd + 1)], wuv_ref[hd], TN))

    return pl.pallas_call(
        body, name="mla_uv_fwd", grid=(nq,),
        in_specs=[pl.BlockSpec((1, C, cols), lambda i: (i, 0, 0)), pl.BlockSpec(w_uv.shape, lambda i: (0, 0, 0))],
        out_specs=pl.BlockSpec((tq, H * MLA_V), lambda i: (i, 0)), out_shape=S((nq * tq, H * MLA_V), _MXU_DTYPE),
        compiler_params=_params("parallel"))(olat, w_uv)


def _mla_uv_bwd(do, olat, w_uv):
    nq, C, cols = olat.shape
    H = w_uv.shape[0]
    tq = cols // H

    def body(do_ref, ol_ref, wuv_ref, dol_ref, dl_ref, dw_ref):
        @pl.when(pl.program_id(0) == 0)
        def _():
            dw_ref[...] = jnp.zeros_like(dw_ref)

        dov = do_ref[...]
        for hd in range(H):
            cs = slice(tq * hd, tq * (hd + 1))
            doh = _mx(dov[:, MLA_V * hd:MLA_V * (hd + 1)])
            ol = ol_ref[0, :, cs]
            dol = _dot(wuv_ref[hd], doh, NT)
            dol_ref[0, :, cs] = _mx(dol)
            dl_ref[0, :, cs] = jnp.sum(dol * ol.astype(F32), axis=0, keepdims=True)
            dw_ref[hd] += _dot(ol, doh)

    blk = lambda n: pl.BlockSpec((1, n, cols), lambda i: (i, 0, 0))
    return pl.pallas_call(
        body, name="mla_uv_bwd", grid=(nq,),
        in_specs=[pl.BlockSpec((tq, H * MLA_V), lambda i: (i, 0)), blk(C), pl.BlockSpec(w_uv.shape, lambda i: (0, 0, 0))],
        out_specs=[blk(C), blk(1), pl.BlockSpec(w_uv.shape, lambda i: (0, 0, 0))],
        out_shape=[S(olat.shape, _MXU_DTYPE), S((nq, 1, cols), F32), S(w_uv.shape, F32)],
        compiler_params=_params("arbitrary"))(do, olat, w_uv)


def _mla_pre_bwd(dqs, dkc, dv, h, x, dres, cq, qn, cos, sin, w_in, g_q, g_kv, w_uq_n, w_uq_r, w_uk):
    nq, DK, cols = dqs.shape
    H = w_uk.shape[0]
    tm = cols // H
    T = nq * tm
    tk = dv.shape[2]
    n_in = w_in.shape[1]

    def body(dqs_ref, dkc_ref, dv_ref, h_ref, x_ref, dres_ref, cq_ref, qn_ref, cos_ref, sin_ref,
             win_ref, gq_ref, gkv_ref, wn_ref, wr_ref, wuk_ref,
             gx_ref, dwin_ref, dwn_ref, dwr_ref, dwuk_ref, dgq_ref, dgkv_ref, dqn_sc, dqr_sc, dh_sc):
        @pl.when(pl.program_id(0) == 0)
        def _():
            for r in (dwin_ref, dwn_ref, dwr_ref, dwuk_ref, dgq_ref, dgkv_ref):
                r[...] = jnp.zeros_like(r)

        cos_v, sin_v = _tile_heads(cos_ref[...]), _tile_heads(sin_ref[...])
        qnb = qn_ref[...]
        for hd in range(H):
            cs = slice(tm * hd, tm * (hd + 1))
            dql = _mx(dqs_ref[0, 0:MLA_C, cs])
            dqn_sc[:, MLA_NOPE * hd:MLA_NOPE * (hd + 1)] = _dot(dql, wuk_ref[hd], TN)
            dwuk_ref[hd] += _dot(dql, qnb[:, MLA_NOPE * hd:MLA_NOPE * (hd + 1)])
            dqr_sc[MLA_ROPE * hd:MLA_ROPE * (hd + 1), :] = dqs_ref[0, MLA_C:MLA_C + MLA_ROPE, cs]
        dqr = dqr_sc[...].T
        dqrb = _mx(dqr * cos_v + _swap_halves_groups(dqr * sin_v))
        dqnb = _mx(dqn_sc[...])
        cq = cq_ref[...]
        dwn_ref[...] += _dot(cq, dqnb, TN)
        dwr_ref[...] += _dot(cq, dqrb, TN)
        dcq = _dot(dqnb, wn_ref[...], NT) + _dot(dqrb, wr_ref[...], NT)
        hv = h_ref[...]
        dxq, dgq = _rms_bwd_math(dcq, hv[:, :MLA_QR], gq_ref[...])
        dgq_ref[...] += dgq
        dckv = (dkc_ref[0:MLA_C, :] + dv_ref[...]).T
        dxkv, dgkv = _rms_bwd_math(dckv, hv[:, MLA_QR:MLA_QR + MLA_C], gkv_ref[...])
        dgkv_ref[...] += dgkv
        dkr = jnp.concatenate([dkc_ref[MLA_C:, :], jnp.zeros((128 - MLA_ROPE, tm), F32)], axis=0).T[:, :MLA_ROPE]
        dkr_raw = dkr * cos_v[:, :MLA_ROPE] + _swap_halves_64(dkr * sin_v[:, :MLA_ROPE])
        dh_sc[:, 0:MLA_QR] = dxq
        dh_sc[:, MLA_QR:MLA_QR + MLA_C] = dxkv
        dh_sc[:, MLA_QR + MLA_C:] = dkr_raw
        dhb = _mx(dh_sc[...])
        gx_ref[...] = dres_ref[...] + _dot(dhb, win_ref[...], NT)
        dwin_ref[...] += _dot(_mx(x_ref[...]), dhb, TN)

    full = lambda shp: pl.BlockSpec(shp, lambda i: (0,) * len(shp))
    rows = lambda n: pl.BlockSpec((tm, n), lambda i: (i, 0))
    return pl.pallas_call(
        body, name="mla_pre_bwd", grid=(nq,),
        in_specs=[pl.BlockSpec((1, DK, cols), lambda i: (i, 0, 0)),
                  pl.BlockSpec((None, DK, tm), lambda i: (i * tm // tk, 0, i % (tk // tm))),
                  pl.BlockSpec((None, MLA_C, tm), lambda i: (i * tm // tk, 0, i % (tk // tm))), rows(n_in),
                  rows(D_MODEL), rows(D_MODEL), rows(MLA_QR), rows(H * MLA_NOPE), rows(ROPE_TABLE_W), rows(ROPE_TABLE_W),
                  full(w_in.shape), full(g_q.shape), full(g_kv.shape), full(w_uq_n.shape), full(w_uq_r.shape),
                  full(w_uk.shape)],
        out_specs=[rows(D_MODEL), full(w_in.shape), full(w_uq_n.shape), full(w_uq_r.shape), full(w_uk.shape),
                   full(g_q.shape), full(g_kv.shape)],
        out_shape=[S((T, D_MODEL), F32), S(w_in.shape, F32), S(w_uq_n.shape, F32), S(w_uq_r.shape, F32),
                   S(w_uk.shape, F32), S(g_q.shape, F32), S(g_kv.shape, F32)],
        scratch_shapes=[pltpu.VMEM((tm, H * MLA_NOPE), F32), pltpu.VMEM((H * MLA_ROPE, tm), F32),
                        pltpu.VMEM((tm, n_in), F32)],
        compiler_params=_params("arbitrary"))(dqs, dkc, dv, h, x, dres, cq, qn, cos, sin, w_in, g_q, g_kv,
                                              w_uq_n, w_uq_r, w_uk)


def _proj_ln_fwd(a, w, xres, res_gb, g, b, *, name, tm=512):
    T, K = a.shape
    tm = min(tm, T)
    gp, bp = res_gb if res_gb is not None else (None, None)

    def body(a_ref, w_ref, x_ref, *rest):
        if res_gb is not None:
            x = x_ref[...] * rest[0][...] + rest[1][...]
            rest = rest[2:]
        else:
            x = x_ref[...]
        g_ref, b_ref, xob_ref, xt_ref, xh_ref, rs_ref = rest
        z = ALPHA * x + _dot(a_ref[...], w_ref[...])
        xo, xhat, rstd = _ln_fwd_math(z, g_ref[...], b_ref[...])
        xob_ref[...] = _mx(xo)
        xt_ref[...] = _mx(xo.T)
        xh_ref[...] = xhat
        rs_ref[...] = rstd

    rows = lambda n: pl.BlockSpec((tm, n), lambda i: (i, 0))
    full = lambda shp: pl.BlockSpec(shp, lambda i: (0,) * len(shp))
    extra = [gp, bp] if res_gb is not None else []
    return pl.pallas_call(
        body, name=name, grid=(T // tm,),
        in_specs=[rows(K), full(w.shape), rows(D_MODEL)] + [full(e.shape) for e in extra] + [full(g.shape), full(b.shape)],
        out_specs=[rows(D_MODEL), pl.BlockSpec((D_MODEL, tm), lambda i: (0, i)), rows(D_MODEL), rows(1)],
        out_shape=[S((T, D_MODEL), _MXU_DTYPE), S((D_MODEL, T), _MXU_DTYPE), S((T, D_MODEL), F32), S((T, 1), F32)],
        compiler_params=_params("parallel"))(a, w, xres, *extra, g, b)


def _proj_ln_bwd(dxo, xhat, rstd, g, a, w, *, name, da_dtype=F32, tm=512):
    T, K = a.shape
    tm = min(tm, T)

    def body(dxo_ref, xh_ref, rs_ref, g_ref, a_ref, w_ref, dres_ref, da_ref, dw_ref, dg_ref, db_ref):
        @pl.when(pl.program_id(0) == 0)
        def _():
            for r in (dw_ref, dg_ref, db_ref):
                r[...] = jnp.zeros_like(r)

        dz, dg, db = _ln_bwd_math(dxo_ref[...], xh_ref[...], rs_ref[...], g_ref[...])
        dg_ref[...] += dg
        db_ref[...] += db
        dres_ref[...] = ALPHA * dz
        dzb = _mx(dz)
        da_ref[...] = _dot(dzb, w_ref[...], NT).astype(da_dtype)
        dw_ref[...] += _dot(a_ref[...], dzb, TN)

    rows = lambda n: pl.BlockSpec((tm, n), lambda i: (i, 0))
    full = lambda shp: pl.BlockSpec(shp, lambda i: (0,) * len(shp))
    return pl.pallas_call(
        body, name=name, grid=(T // tm,),
        in_specs=[rows(D_MODEL), rows(D_MODEL), rows(1), full(g.shape), rows(K), full(w.shape)],
        out_specs=[rows(D_MODEL), rows(K), full(w.shape), full(g.shape), full(g.shape)],
        out_shape=[S((T, D_MODEL), F32), S((T, K), da_dtype), S(w.shape, F32), S(g.shape, F32), S(g.shape, F32)],
        compiler_params=_params("arbitrary"))(dxo, xhat, rstd, g, a, w)


MLP_FWD_CHUNKS = 2


def _mlp_fwd(xb, xh_in, g_in, b_in, w_up, w_dn, layer, g, b, target=None, *, tm=1024):
    T, D = xb.shape
    with_loss = target is not None
    tm = min(tm // 2 if with_loss else tm, T)
    nj, _, _, fc = w_up.shape
    cps = MLP_FWD_CHUNKS * (2 if with_loss else 1)
    ns = nj // cps

    def body(xb_ref, xh_ref_in, gi_ref, bi_ref, wu_ref, wd_ref, g_ref, b_ref, *rest):
        if with_loss:
            t_ref, u_ref, ut_ref, xh_ref, rs_ref, loss_ref, dy_ref, acc = rest
        else:
            u_ref, ut_ref, xh_ref, rs_ref, xob_ref, acc = rest
        i, j = pl.program_id(0), pl.program_id(1)

        @pl.when(j == 0)
        def _():
            acc[...] = ALPHA * (xh_ref_in[...] * gi_ref[...] + bi_ref[...])

        if with_loss:
            @pl.when((i == 0) & (j == 0))
            def _():
                loss_ref[...] = jnp.zeros_like(loss_ref)

        xb_v = xb_ref[...]
        for c in range(cps):
            u = _dot(xb_v, wu_ref[c])
            u_ref[:, fc * c:fc * (c + 1)] = _mx(u)
            ut_ref[fc * c:fc * (c + 1), :] = _mx(u.T)
            r = jnp.maximum(u, 0.0)
            acc[...] += _dot(_mx(r * r), wd_ref[c])

        @pl.when(j == ns - 1)
        def _():
            xo, xhat, rstd = _ln_fwd_math(acc[...], g_ref[...], b_ref[...])
            xh_ref[...] = xhat
            rs_ref[...] = rstd
            if with_loss:
                d = xo - t_ref[...]
                dy_ref[...] = d * (1.0 / D)
                loss_ref[...] += (0.5 / D) * jnp.sum(jnp.sum(d * d, axis=1, keepdims=True), axis=0, keepdims=True)
            else:
                xob_ref[...] = _mx(xo)

    rows = lambda n: pl.BlockSpec((tm, n), lambda i, j: (i, 0))
    full = lambda shp: pl.BlockSpec(shp, lambda i, j: (0,) * len(shp))
    in_specs = [rows(D), rows(D), full(g_in.shape), full(b_in.shape),
                pl.BlockSpec((cps, None, D, fc), lambda i, j: (j, layer, 0, 0)),
                pl.BlockSpec((cps, None, fc, D), lambda i, j: (j, layer, 0, 0)), full(g.shape), full(b.shape)]
    out_specs = [pl.BlockSpec((tm, cps * fc), lambda i, j: (i, j)), pl.BlockSpec((cps * fc, tm), lambda i, j: (j, i)),
                 rows(D), rows(1)]
    out_shape = [S((T, nj * fc), _MXU_DTYPE), S((nj * fc, T), _MXU_DTYPE), S((T, D), F32), S((T, 1), F32)]
    args = [xb, xh_in, g_in, b_in, w_up, w_dn, g, b]
    if with_loss:
        in_specs.append(rows(D))
        args.append(target)
        out_specs += [pl.BlockSpec((1, 1), lambda i, j: (0, 0)), rows(D)]
        out_shape += [S((1, 1), F32), S((T, D), F32)]
    else:
        out_specs.append(rows(D))
        out_shape.append(S((T, D), _MXU_DTYPE))
    return pl.pallas_call(
        body, name=f"mlp_fwd_{layer}", grid=(T // tm, ns), in_specs=in_specs, out_specs=out_specs, out_shape=out_shape,
        scratch_shapes=[pltpu.VMEM((tm, D), F32)],
        compiler_params=_params("arbitrary", "arbitrary"))(*args)


MLP_BWD_DX_CHUNKS = 4


def _mlp_bwd_dx(dxo, xhat, rstd, g, u, w_up, w_dn, layer, *, tm=512):
    T = dxo.shape[0]
    tm = min(tm, T)
    nj, _, _, fc = w_up.shape
    cps = MLP_BWD_DX_CHUNKS
    ns = nj // cps

    def body(dxo_ref, xh_ref, rs_ref, g_ref, u_ref, wu_ref, wd_ref, dx_ref, du_ref, dyb_ref, dg_ref, db_ref, acc, dy_sc):
        i, j = pl.program_id(0), pl.program_id(1)

        @pl.when((i == 0) & (j == 0))
        def _():
            dg_ref[...] = jnp.zeros_like(dg_ref)
            db_ref[...] = jnp.zeros_like(db_ref)

        @pl.when(j == 0)
        def _():
            dz, dg, db = _ln_bwd_math(dxo_ref[...], xh_ref[...], rs_ref[...], g_ref[...])
            dg_ref[...] += dg
            db_ref[...] += db
            acc[...] = ALPHA * dz
            dy_sc[...] = _mx(dz)
            dyb_ref[...] = _mx(dz)

        dyb = dy_sc[...]
        for c in range(cps):
            cs = slice(fc * c, fc * (c + 1))
            r = jnp.maximum(u_ref[:, cs].astype(F32), 0.0)
            da = _dot(dyb, wd_ref[c], NT)
            dub = _mx(da * (2.0 * r))
            du_ref[:, cs] = dub
            acc[...] += _dot(dub, wu_ref[c], NT)

        @pl.when(j == ns - 1)
        def _():
            dx_ref[...] = acc[...]

    rows = lambda n: pl.BlockSpec((tm, n), lambda i, j: (i, 0))
    full = lambda shp: pl.BlockSpec(shp, lambda i, j: (0,) * len(shp))
    return pl.pallas_call(
        body, name=f"mlp_bwd_dx_{layer}", grid=(T // tm, ns),
        in_specs=[rows(D_MODEL), rows(D_MODEL), rows(1), full(g.shape),
                  pl.BlockSpec((tm, cps * fc), lambda i, j: (i, j)),
                  pl.BlockSpec((cps, None, D_MODEL, fc), lambda i, j: (j, layer, 0, 0)),
                  pl.BlockSpec((cps, None, fc, D_MODEL), lambda i, j: (j, layer, 0, 0))],
        out_specs=[rows(D_MODEL), pl.BlockSpec((tm, cps * fc), lambda i, j: (i, j)), rows(D_MODEL), full(g.shape),
                   full(g.shape)],
        out_shape=[S((T, D_MODEL), F32), S((T, nj * fc), _MXU_DTYPE), S((T, D_MODEL), _MXU_DTYPE),
                   S(g.shape, F32), S(g.shape, F32)],
        scratch_shapes=[pltpu.VMEM((tm, D_MODEL), F32), pltpu.VMEM((tm, D_MODEL), _MXU_DTYPE)],
        compiler_params=_params("arbitrary", "arbitrary"))(dxo, xhat, rstd, g, u, w_up, w_dn)


def _mlp_bwd_dw(ut, dyb, xt, du, layer, *, nj, other_layers=None, tm=512):
    T = ut.shape[1]
    tm = min(tm, T)
    fc = ut.shape[0] // nj
    ni = T // tm
    cps = MLP_BWD_DX_CHUNKS
    ns = nj // cps

    def body(ut_ref, dy_ref, xt_ref, du_ref, *rest):
        gd_ref, gu_ref, gd_acc, gu_acc, sem = rest[-5:]
        i, j = pl.program_id(0), pl.program_id(1)

        @pl.when(i == 0)
        def _():
            for c in range(cps):
                gd_acc[j * cps + c] = jnp.zeros((fc, D_MODEL), F32)
                gu_acc[j * cps + c] = jnp.zeros((D_MODEL, fc), F32)

        dy, xtv = dy_ref[...], xt_ref[...]
        for c in range(cps):
            cs = slice(fc * c, fc * (c + 1))
            r = jnp.maximum(ut_ref[cs, :].astype(F32), 0.0)
            gd_acc[j * cps + c] += _dot(_mx(r * r), dy)
            gu_acc[j * cps + c] += _dot(xtv, du_ref[:, cs])

        @pl.when((i == ni - 1) & (j == ns - 1))
        def _():
            c1 = pltpu.make_async_copy(gd_acc, gd_ref.at[:, layer], sem.at[0])
            c2 = pltpu.make_async_copy(gu_acc, gu_ref.at[:, layer], sem.at[1])
            c1.start()
            c2.start()
            c1.wait()
            c2.wait()

    hbm = pl.BlockSpec(memory_space=pl.ANY)
    in_specs = [pl.BlockSpec((cps * fc, tm), lambda i, j: (j, i)), pl.BlockSpec((tm, D_MODEL), lambda i, j: (i, 0)),
                pl.BlockSpec((D_MODEL, tm), lambda i, j: (0, i)), pl.BlockSpec((tm, cps * fc), lambda i, j: (i, j))]
    args, aliases = [ut, dyb, xt, du], {}
    if other_layers is not None:
        in_specs += [hbm] * 2
        args += list(other_layers)
        aliases = {4: 0, 5: 1}
    return pl.pallas_call(
        body, name=f"mlp_bwd_dw_{layer}", grid=(ni, ns), in_specs=in_specs, out_specs=[hbm, hbm],
        out_shape=[S((nj, DEPTH, fc, D_MODEL), F32), S((nj, DEPTH, D_MODEL, fc), F32)],
        scratch_shapes=[pltpu.VMEM((nj, fc, D_MODEL), F32), pltpu.VMEM((nj, D_MODEL, fc), F32),
                        pltpu.SemaphoreType.DMA((2,))],
        input_output_aliases=aliases,
        compiler_params=_params("arbitrary", "arbitrary"))(*args)


SWA_GROUP = SWA_QH // SWA_KVH
SWA_ROWS = SWA_GROUP * SWA_BLOCK


def _swa_heads(a, kh):
    return jnp.concatenate([a[:, SWA_D * (kh * SWA_GROUP + g):SWA_D * (kh * SWA_GROUP + g + 1)]
                            for g in range(SWA_GROUP)], axis=0)


def _swa_operands(q, kvp, kvc, kh):
    dkv = SWA_KVH * SWA_D
    qg = _swa_heads(q, kh)
    kb = jnp.concatenate([kvp[:, SWA_D * kh:SWA_D * (kh + 1)], kvc[:, SWA_D * kh:SWA_D * (kh + 1)]], axis=0)
    vb = jnp.concatenate([kvp[:, dkv + SWA_D * kh:dkv + SWA_D * (kh + 1)],
                          kvc[:, dkv + SWA_D * kh:dkv + SWA_D * (kh + 1)]], axis=0)
    return qg, kb, vb, _dot(kb, qg, NT)


def _swa_softmax(s_raw, bias_ref, sink_ref, n, kh):
    cols = slice(kh * SWA_ROWS, (kh + 1) * SWA_ROWS)
    s = s_raw * (SWA_D ** -0.5) + bias_ref[jnp.minimum(n, 1), :, cols]
    sink = sink_ref[:, cols]
    m = jnp.maximum(jnp.max(s, axis=0, keepdims=True), sink)
    p, ps = jnp.exp(s - m), jnp.exp(sink - m)
    inv = 1.0 / (jnp.sum(p, axis=0, keepdims=True) + ps)
    return p * inv, ps * inv


def _swa_attn_fwd(qkv, bias, sinks):
    T = qkv.shape[0]
    blk = SWA_BLOCK
    nb = T // blk
    dq, dkv = SWA_QH * SWA_D, SWA_KVH * SWA_D

    def body(q_ref, kvp_ref, kvc_ref, bias_ref, sink_ref, o_ref):
        n = pl.program_id(0)
        q, kvp, kvc = q_ref[...], kvp_ref[...], kvc_ref[...]
        nxt = _swa_operands(q, kvp, kvc, 0)
        for kh in range(SWA_KVH):
            _, _, vb, s_raw = nxt
            if kh + 1 < SWA_KVH:
                nxt = _swa_operands(q, kvp, kvc, kh + 1)
            p, _ = _swa_softmax(s_raw, bias_ref, sink_ref, n, kh)
            og = _mx(_dot(_mx(p), vb, TN))
            for g in range(SWA_GROUP):
                hd = kh * SWA_GROUP + g
                o_ref[:, SWA_D * hd:SWA_D * (hd + 1)] = og[blk * g:blk * (g + 1), :]

    return pl.pallas_call(
        body, name="swa_attn_fwd", grid=(nb,),
        in_specs=[pl.BlockSpec((blk, dq), lambda n: (n, 0)),
                  pl.BlockSpec((blk, 2 * dkv), lambda n: (jnp.maximum(n - 1, 0), dq // (2 * dkv))),
                  pl.BlockSpec((blk, 2 * dkv), lambda n: (n, dq // (2 * dkv))),
                  pl.BlockSpec(bias.shape, lambda n: (0, 0, 0)), pl.BlockSpec(sinks.shape, lambda n: (0, 0))],
        out_specs=pl.BlockSpec((blk, dq), lambda n: (n, 0)), out_shape=S((T, dq), _MXU_DTYPE),
        compiler_params=_params("parallel"))(qkv, qkv, qkv, bias, sinks)


def _swa_attn_bwd(qkv, ob, do, bias, sinks):
    T = qkv.shape[0]
    blk = SWA_BLOCK
    nb = T // blk
    dq, dkv = SWA_QH * SWA_D, SWA_KVH * SWA_D

    def body(q_ref, kvp_ref, kvc_ref, o_ref, do_ref, bias_ref, sink_ref, dqkv_ref, dbias_ref, dsink_ref, carry):
        st = pl.program_id(0)
        n = nb - 1 - st

        @pl.when(st == 0)
        def _():
            carry[...] = jnp.zeros_like(carry)
            dbias_ref[...] = jnp.zeros_like(dbias_ref)
            dsink_ref[...] = jnp.zeros_like(dsink_ref)

        q, kvp, kvc = q_ref[...], kvp_ref[...], kvc_ref[...]
        ov, dov = o_ref[...], do_ref[...]
        ones = jnp.ones((8, SWA_D), F32)
        def operands(kh):
            qg, kb, vb, s_raw = _swa_operands(q, kvp, kvc, kh)
            dog = _swa_heads(dov, kh)
            dl = _dot(ones, dog * _swa_heads(ov, kh).astype(F32), NT, lax.Precision.HIGHEST)[0:1]
            dogb = _mx(dog)
            return qg, kb, s_raw, dl, dogb, _dot(vb, dogb, NT)

        nxt = operands(0)
        for kh in range(SWA_KVH):
            cols = slice(kh * SWA_ROWS, (kh + 1) * SWA_ROWS)
            qg, kb, s_raw, dl, dogb, dp = nxt
            if kh + 1 < SWA_KVH:
                nxt = operands(kh + 1)
            p, ps = _swa_softmax(s_raw, bias_ref, sink_ref, n, kh)
            ds = p * (dp - dl)
            dbias_ref[:, cols] += ds
            dsink_ref[0:1, cols] += -ps * dl
            dsb = _mx(ds * (SWA_D ** -0.5))
            dqg = _mx(_dot(dsb, kb, TN))
            for g in range(SWA_GROUP):
                hd = kh * SWA_GROUP + g
                dqkv_ref[:, SWA_D * hd:SWA_D * (hd + 1)] = dqg[blk * g:blk * (g + 1), :]
            dkb = _dot(dsb, qg)
            dvb = _dot(_mx(p), dogb)
            ko, vo = SWA_D * kh, dkv + SWA_D * kh
            dqkv_ref[:, dq + ko:dq + ko + SWA_D] = _mx(dkb[blk:, :] + carry[:, ko:ko + SWA_D])
            dqkv_ref[:, dq + vo:dq + vo + SWA_D] = _mx(dvb[blk:, :] + carry[:, vo:vo + SWA_D])
            carry[:, ko:ko + SWA_D] = dkb[:blk, :]
            carry[:, vo:vo + SWA_D] = dvb[:blk, :]

    rev = lambda s: nb - 1 - s
    return pl.pallas_call(
        body, name="swa_attn_bwd", grid=(nb,),
        in_specs=[pl.BlockSpec((blk, dq), lambda s: (rev(s), 0)),
                  pl.BlockSpec((blk, 2 * dkv), lambda s: (jnp.maximum(rev(s) - 1, 0), dq // (2 * dkv))),
                  pl.BlockSpec((blk, 2 * dkv), lambda s: (rev(s), dq // (2 * dkv))),
                  pl.BlockSpec((blk, dq), lambda s: (rev(s), 0)), pl.BlockSpec((blk, dq), lambda s: (rev(s), 0)),
                  pl.BlockSpec(bias.shape, lambda s: (0, 0, 0)), pl.BlockSpec(sinks.shape, lambda s: (0, 0))],
        out_specs=[pl.BlockSpec((blk, dq + 2 * dkv), lambda s: (rev(s), 0)),
                   pl.BlockSpec(bias.shape[1:], lambda s: (0, 0)), pl.BlockSpec((8, sinks.shape[1]), lambda s: (0, 0))],
        out_shape=[S((T, dq + 2 * dkv), _MXU_DTYPE), S(bias.shape[1:], F32), S((8, sinks.shape[1]), F32)],
        scratch_shapes=[pltpu.VMEM((blk, 2 * dkv), F32)],
        compiler_params=_params("arbitrary"))(qkv, qkv, qkv, ob, do, bias, sinks)


def _t5_onehot():
    i = jnp.arange(SWA_BLOCK)
    j = jnp.arange(2 * SWA_BLOCK)
    n = jnp.maximum(i[:, None] + SWA_BLOCK - j[None, :], 0)
    max_exact = REL_BUCKETS // 2
    nf = jnp.maximum(n, 1).astype(F32)
    large = max_exact + (jnp.log(nf / max_exact) / math.log(REL_MAX_DIST / max_exact)
                         * (REL_BUCKETS - max_exact)).astype(jnp.int32)
    large = jnp.minimum(large, REL_BUCKETS - 1)
    bucket = jnp.where(n < max_exact, n, large).reshape(-1)
    return (bucket[None, :] == jnp.arange(REL_BUCKETS)[:, None]).astype(F32)


def _exchange_copies(ins, outs, n_gather, send_sems, recv_sems, loc_sems):
    mx, my, mc = lax.axis_index("x"), lax.axis_index("y"), lax.axis_index("c")
    me = 4 * mx + 2 * my + mc
    copies = []
    for a in range(len(ins)):
        src = ins[a] if a < n_gather else ins[a].at[me]
        copies.append(pltpu.make_async_copy(src, outs[a].at[me], loc_sems.at[a]))
    for k in range(1, N_DEV):
        px, py, pc = mx ^ ((k >> 2) & 1), my ^ ((k >> 1) & 1), mc ^ (k & 1)
        peer = 4 * px + 2 * py + pc
        for a in range(len(ins)):
            src = ins[a] if a < n_gather else ins[a].at[peer]
            copies.append(pltpu.make_async_remote_copy(
                src_ref=src, dst_ref=outs[a].at[me], send_sem=send_sems.at[a, k - 1],
                recv_sem=recv_sems.at[a, k - 1], device_id=(px, py, pc), device_id_type=pl.DeviceIdType.MESH))
    return copies


def _exchange_shapes(gather, scatter):
    n_arr = len(gather) + len(scatter)
    out_shape = [S((N_DEV,) + tuple(g.shape), g.dtype) for g in gather] + [S(s.shape, s.dtype) for s in scatter]
    sems = [pltpu.SemaphoreType.DMA((n_arr, N_DEV - 1)), pltpu.SemaphoreType.DMA((n_arr, N_DEV - 1)),
            pltpu.SemaphoreType.DMA((n_arr,))]
    return out_shape, sems


def _exchange(gather, scatter, *, name):
    n_g = len(gather)
    n_arr = n_g + len(scatter)

    def body(*refs):
        copies = _exchange_copies(refs[:n_arr], refs[n_arr:2 * n_arr], n_g, *refs[2 * n_arr:])
        for cp in copies:
            cp.start()
        for cp in copies:
            cp.wait()

    hbm = pl.BlockSpec(memory_space=pl.ANY)
    out_shape, sems = _exchange_shapes(gather, scatter)
    return pl.pallas_call(
        body, name=name, in_specs=[hbm] * n_arr, out_specs=[hbm] * n_arr, out_shape=out_shape,
        scratch_shapes=sems)(*gather, *scatter)


def _gather_two_level(arrays, *, name):
    n = len(arrays)

    def body(*refs):
        ins, outs = refs[:n], refs[n:2 * n]
        send_sems, recv_sems, loc_sems = refs[2 * n:]
        x, y, c = lax.axis_index("x"), lax.axis_index("y"), lax.axis_index("c")
        me, sibling = (x, y, c), (x, y, 1 - c)
        chips = [(1 - x, y), (x, 1 - y), (1 - x, 1 - y)]

        def copy(a, k, block, to, src=None):
            dst = outs[a].at[4 * block[0] + 2 * block[1] + block[2]]
            return pltpu.make_async_remote_copy(
                src_ref=dst if src is None else src, dst_ref=dst, send_sem=send_sems.at[a, k],
                recv_sem=recv_sems.at[a, k], device_id=to, device_id_type=pl.DeviceIdType.MESH)

        mine = [pltpu.make_async_copy(ins[a], outs[a].at[4 * x + 2 * y + c], loc_sems.at[a]) for a in range(n)]
        first = [copy(a, 0, me, sibling, src=ins[a]) for a in range(n)]
        first += [copy(a, 1 + j, me, (*chip, c), src=ins[a]) for j, chip in enumerate(chips) for a in range(n)]
        for cp in mine + first:
            cp.start()
        passed = []
        for j, chip in enumerate(chips):
            for a in range(n):
                copy(a, 1 + j, (*chip, c), me).wait_recv()
                passed.append(copy(a, 4 + j, (*chip, c), sibling))
                passed[-1].start()
        for a in range(n):
            copy(a, 0, sibling, me).wait_recv()
        for j, chip in enumerate(chips):
            for a in range(n):
                copy(a, 4 + j, (*chip, 1 - c), me).wait_recv()
        for cp in first + passed:
            cp.wait_send()
        for cp in mine:
            cp.wait()

    hbm = pl.BlockSpec(memory_space=pl.ANY)
    out_shape, sems = _exchange_shapes(arrays, [])
    return pl.pallas_call(
        body, name=name, in_specs=[hbm] * n, out_specs=[hbm] * n, out_shape=out_shape, scratch_shapes=sems)(*arrays)


def _adamw(parts, w, m, v, *, name, tr=256):
    R, C = w.shape
    tr = min(tr, R)
    assert R % tr == 0

    def body(p_ref, w_ref, m_ref, v_ref, g_ref, d_ref, nm_ref, nv_ref):
        g = p_ref[0].astype(F32)
        for k in range(1, N_DEV):
            g = g + p_ref[k].astype(F32)
        g_ref[...] = g
        d_ref[...], nm_ref[...], nv_ref[...] = _adamw_math(g, w_ref[...], m_ref[...], v_ref[...])

    rows = pl.BlockSpec((tr, C), lambda i: (i, 0))
    return pl.pallas_call(
        body, name=name, grid=(R // tr,),
        in_specs=[pl.BlockSpec((N_DEV, tr, C), lambda i: (0, i, 0)), rows, rows, rows],
        out_specs=[rows] * 4, out_shape=[S((R, C), F32)] * 4,
        compiler_params=_params("parallel"))(parts, w, m, v)


def _adamw_math(g, w, m, v):
    m_new = ADAM_B1 * m + (1.0 - ADAM_B1) * g
    v_new = ADAM_B2 * v + (1.0 - ADAM_B2) * (g * g)
    m_hat = m_new / (1.0 - ADAM_B1 ** ADAM_STEP)
    v_hat = v_new / (1.0 - ADAM_B2 ** ADAM_STEP)
    return -ADAM_LR * (m_hat / (jnp.sqrt(v_hat) + ADAM_EPS) + ADAM_WD * w), m_new, v_new


SMALL_ROWS = 48
REPL = {"ln_mix_g": (slice(0, 2), slice(None)), "ln_mix_b": (slice(2, 4), slice(None)),
        "ln_mlp_g": (slice(4, 6), slice(None)), "ln_mlp_b": (slice(6, 8), slice(None)),
        "swa_sinks": (slice(8, 9), slice(0, SWA_QH)), "rel_bias": (slice(16, 16 + REL_BUCKETS), slice(0, SWA_QH))}
GAINS = {"mla_g_q": (slice(0, 1), slice(0, MLA_QR // N_DEV)),
         "mla_g_kv": (slice(0, 1), slice(MLA_QR // N_DEV, (MLA_QR + MLA_C) // N_DEV))}


LOSS_AT = (slice(8, 9), slice(SWA_QH, SWA_QH + 1))


def _adamw_small(r_all, p_gains, W, M, V):
    names = list(REPL) + list(GAINS)

    def body(r_ref, pg_ref, *refs):
        ins, outs, loss_ref = refs[:3 * len(names)], refs[3 * len(names):-1], refs[-1]
        r_sum, g_sum = r_ref[0], pg_ref[0]
        for k in range(1, N_DEV):
            r_sum, g_sum = r_sum + r_ref[k], g_sum + pg_ref[k]
        loss_ref[...] = r_sum[LOSS_AT]
        for i, n in enumerate(names):
            g = r_sum[REPL[n]] if n in REPL else g_sum[GAINS[n]]
            w_ref, m_ref, v_ref = ins[3 * i:3 * i + 3]
            g_ref, d_ref, nm_ref, nv_ref = outs[4 * i:4 * i + 4]
            g_ref[...] = g
            d_ref[...], nm_ref[...], nv_ref[...] = _adamw_math(g, w_ref[...], m_ref[...], v_ref[...])

    flat_in = [d[n] for n in names for d in (W, M, V)]
    res = pl.pallas_call(body, name="adamw_small",
                         out_shape=[S(W[n].shape, F32) for n in names for _ in range(4)] + [S((1, 1), F32)],
                         compiler_params=_params())(r_all, p_gains, *flat_in)
    return {(k, n): res[4 * i + k] for i, n in enumerate(names) for k in range(4)}, res[-1]


WEIGHTS = ["mla_w_in", "mla_g_q", "mla_g_kv", "mla_w_uq", "mla_w_uk", "mla_w_uv", "mla_w_o", "kv_w_shared",
           "swa_w_q", "swa_sinks", "swa_w_o", "rel_bias", "mlp_w_up", "mlp_w_down", "ln_mix_g", "ln_mix_b",
           "ln_mlp_g", "ln_mlp_b"]


def kernel(x, mla_w_in, mla_g_q, mla_g_kv, mla_w_uq, mla_w_uk, mla_w_uv, mla_w_o, kv_w_shared, swa_w_q, swa_sinks, swa_w_o, rel_bias, mlp_w_up, mlp_w_down, ln_mix_g, ln_mix_b, ln_mlp_g, ln_mlp_b, loss_target, m_mla_w_in, m_mla_g_q, m_mla_g_kv, m_mla_w_uq, m_mla_w_uk, m_mla_w_uv, m_mla_w_o, m_kv_w_shared, m_swa_w_q, m_swa_sinks, m_swa_w_o, m_rel_bias, m_mlp_w_up, m_mlp_w_down, m_ln_mix_g, m_ln_mix_b, m_ln_mlp_g, m_ln_mlp_b, v_mla_w_in, v_mla_g_q, v_mla_g_kv, v_mla_w_uq, v_mla_w_uk, v_mla_w_uv, v_mla_w_o, v_kv_w_shared, v_swa_w_q, v_swa_sinks, v_swa_w_o, v_rel_bias, v_mlp_w_up, v_mlp_w_down, v_ln_mix_g, v_ln_mix_b, v_ln_mlp_g, v_ln_mlp_b):
    args = dict(locals())
    W = {n: args[n] for n in WEIGHTS}
    M = {n: args["m_" + n] for n in WEIGHTS}
    V = {n: args["v_" + n] for n in WEIGHTS}
    T = x.shape[1]
    x2d = x.reshape(T, D_MODEL)
    tgt = loss_target.reshape(T, D_MODEL)
    H = MLA_HEADS

    SH = {"mla_w_in": (-1, mla_w_in.shape[-1]), "mla_w_uq": (-1, H * (MLA_NOPE + MLA_ROPE)),
          "mla_w_uk": (-1, H * MLA_NOPE), "mla_w_uv": (-1, H * MLA_V), "mla_w_o": (-1, D_MODEL),
          "kv_w_shared": (-1, kv_w_shared.shape[-1]), "swa_w_q": (-1, swa_w_q.shape[-1]), "swa_w_o": (-1, D_MODEL)}
    slab = lambda d, n: d[n].reshape(SH[n])
    bf = lambda a: a.astype(_MXU_DTYPE)
    gains_slab = lambda d: jnp.pad(jnp.concatenate([d["mla_g_q"], d["mla_g_kv"]], axis=1),
                                   ((0, 7), (0, 128 - d["mla_g_q"].shape[1] - d["mla_g_kv"].shape[1])))
    n_gq, n_gkv = mla_g_q.shape[1], mla_g_kv.shape[1]
    w_in_s, w_uq_s, w_uk_s, gains_all = _gather_two_level(
        [bf(slab(W, "mla_w_in")), bf(slab(W, "mla_w_uq")), bf(slab(W, "mla_w_uk")), gains_slab(W)],
        name="gather_mla_in")
    later = [bf(slab(W, n)) for n in ("mla_w_uv", "mla_w_o", "kv_w_shared", "swa_w_q", "swa_w_o")]
    later += [bf(mlp_w_up), bf(mlp_w_down)]
    w_in = w_in_s.reshape(D_MODEL, -1)
    g_q = gains_all[:, 0, :n_gq].reshape(1, MLA_QR)
    g_kv = gains_all[:, 0, n_gq:n_gq + n_gkv].reshape(1, MLA_C)
    w_uq = w_uq_s.reshape(MLA_QR, H, MLA_NOPE + MLA_ROPE)
    w_uq_n = w_uq[:, :, :MLA_NOPE].reshape(MLA_QR, H * MLA_NOPE)
    w_uq_r = w_uq[:, :, MLA_NOPE:].reshape(MLA_QR, H * MLA_ROPE)
    w_uk = w_uk_s.reshape(MLA_C, H, MLA_NOPE).transpose(1, 0, 2)
    w_uk_t = w_uk.transpose(0, 2, 1)
    ln = lambda a, l: a[l].reshape(1, D_MODEL)

    half = MLA_ROPE // 2
    inv = ROPE_THETA ** (-jnp.arange(half, dtype=F32) / half)
    ang = jnp.arange(T, dtype=F32)[:, None] * inv[None, :]
    cos = jnp.tile(jnp.concatenate([jnp.cos(ang), jnp.cos(ang)], -1), (1, ROPE_TABLE_W // MLA_ROPE))
    sin = jnp.tile(jnp.concatenate([-jnp.sin(ang), jnp.sin(ang)], -1), (1, ROPE_TABLE_W // MLA_ROPE))

    h, kc, kct, qs, qst, cq, qn = _mla_pre_fwd(x2d, w_in, g_q, g_kv, w_uq_n, w_uq_r, w_uk_t, cos, sin)
    olat, lse, (w_uv_s, w_o_s, w_kv_s, w_q_s, w_o2_s, w_up, w_dn) = _mla_attn_fwd(qs, kc, kct, gather=later)
    w_uv = w_uv_s.reshape(MLA_C, H, MLA_V).transpose(1, 0, 2)
    w_o = w_o_s.reshape(H * MLA_V, D_MODEL)
    w_qkv = jnp.concatenate([w_q_s.reshape(D_MODEL, -1), w_kv_s.reshape(D_MODEL, -1)], axis=1)
    w_o2 = w_o2_s.reshape(SWA_QH * SWA_D, D_MODEL)
    o_mla = _mla_uv_fwd(olat, w_uv)
    mix0, mlp0 = (ln(ln_mix_g, 0), ln(ln_mix_b, 0)), (ln(ln_mlp_g, 0), ln(ln_mlp_b, 0))
    mix1, mlp1 = (ln(ln_mix_g, 1), ln(ln_mix_b, 1)), (ln(ln_mlp_g, 1), ln(ln_mlp_b, 1))
    x1b, x1t, xh1, rs1 = _proj_ln_fwd(o_mla, w_o, x2d, None, *mix0, name="mla_out_ln_fwd")
    u0, u0t, xh2, rs2, x2b = _mlp_fwd(x1b, xh1, *mix0, w_up, w_dn, 0, *mlp0)
    onehot = _t5_onehot()
    bias = _mm(rel_bias.T, onehot, name="rel_bias_expand", precision=lax.Precision.HIGHEST, tn=8192).reshape(
        SWA_QH * SWA_BLOCK, 2 * SWA_BLOCK).T
    key = jnp.arange(2 * SWA_BLOCK)[:, None]
    qry = jnp.arange(SWA_QH * SWA_BLOCK)[None, :] % SWA_BLOCK
    in_window = (key > qry) & (key <= qry + SWA_BLOCK)
    bias = jnp.stack([jnp.where(in_window & (key >= SWA_BLOCK), bias, -jnp.inf), jnp.where(in_window, bias, -jnp.inf)])
    sink_rows = jnp.repeat(swa_sinks.reshape(SWA_QH), SWA_BLOCK).reshape(1, SWA_QH * SWA_BLOCK)
    qkv = _mm(x2b, w_qkv, name="swa_qkv_fwd", out_dtype=_MXU_DTYPE, tm=1024, tn=512, tk=1024)
    o_swa = _swa_attn_fwd(qkv, bias, sink_rows)
    x3b, x3t, xh3, rs3 = _proj_ln_fwd(o_swa, w_o2, xh2, mlp0, *mix1, name="swa_out_ln_fwd")
    u1, u1t, xh4, rs4, loss_part, dx4 = _mlp_fwd(x3b, xh3, *mix1, w_up, w_dn, 1, *mlp1, tgt)

    nj = w_up.shape[0]
    dx3, du1, dy4b, dg_mlp1, db_mlp1 = _mlp_bwd_dx(dx4, xh4, rs4, ln(ln_mlp_g, 1), u1, w_up, w_dn, 1)
    g_dn_last, g_up_last = _mlp_bwd_dw(u1t, dy4b, x3t, du1, 1, nj=nj)
    dres3, do_swa, g_o2, dg_mix1, db_mix1 = _proj_ln_bwd(dx3, xh3, rs3, ln(ln_mix_g, 1), o_swa, w_o2,
                                                         name="swa_out_ln_bwd")
    dqkv, dbias, dsink = _swa_attn_bwd(qkv, o_swa, do_swa, bias, sink_rows)
    g_rel = _mm(onehot, dbias.T.reshape(SWA_QH, -1), name="rel_bias_grad", tb=True, precision=lax.Precision.HIGHEST,
                tk=8192)
    head_of_row = (jnp.arange(SWA_QH * SWA_BLOCK)[:, None] // SWA_BLOCK == jnp.arange(SWA_QH)[None, :]).astype(F32)
    g_sinks = _mm(dsink, head_of_row, name="sinks_grad", precision=lax.Precision.HIGHEST, tk=2048)[0:1]
    dx2 = _mm(dqkv, w_qkv, name="swa_qkv_bwd_dx", tb=True, add=dres3, tm=1024, tn=1024, tk=1536)
    g_qkv = _mm(x2b, dqkv, name="swa_qkv_bwd_dw", ta=True, tm=1024, tn=512, tk=1024)
    dx1, du0, dy2b, dg_mlp0, db_mlp0 = _mlp_bwd_dx(dx2, xh2, rs2, ln(ln_mlp_g, 0), u0, w_up, w_dn, 0)
    g_dn, g_up = _mlp_bwd_dw(u0t, dy2b, x1t, du0, 0, nj=nj, other_layers=(g_dn_last, g_up_last))
    dres1, do_mla, g_o, dg_mix0, db_mix0 = _proj_ln_bwd(dx1, xh1, rs1, ln(ln_mix_g, 0), o_mla, w_o,
                                                        name="mla_out_ln_bwd", da_dtype=_MXU_DTYPE)
    dol, delta, g_uv = _mla_uv_bwd(do_mla, olat, w_uv)
    wide = lambda a, rows: jnp.pad(a, ((0, rows - a.shape[0]), (0, LANES - a.shape[1])))
    r_part = jnp.concatenate([dg_mix0, dg_mix1, db_mix0, db_mix1, dg_mlp0, dg_mlp1, db_mlp0, db_mlp1,
                              wide(jnp.concatenate([g_sinks, loss_part], axis=1), 8), wide(g_rel, SMALL_ROWS - 16)],
                             axis=0)
    by_dev = lambda g: g.reshape((N_DEV, g.shape[0] // N_DEV) + g.shape[1:])
    early = [by_dev(g_o2), by_dev(g_qkv), g_up, g_dn, by_dev(g_o),
             by_dev(g_uv.transpose(1, 0, 2).reshape(MLA_C, H * MLA_V))]
    dqs, dkc, dv, (r_all, p_o2, p_qkv, p_up, p_dn, p_o, p_uv) = _mla_attn_bwd(
        qs, qst, kc, kct, dol, lse, delta, gather=[r_part], scatter=early)
    grad_x, g_in, g_uq_n, g_uq_r, g_uk, g_gq, g_gkv = _mla_pre_bwd(
        dqs, dkc, dv, h, x2d, dres1, cq, qn, cos, sin, w_in, g_q, g_kv, w_uq_n, w_uq_r, w_uk)
    g_uq = jnp.concatenate([g_uq_n.reshape(MLA_QR, H, MLA_NOPE), g_uq_r.reshape(MLA_QR, H, MLA_ROPE)], -1)
    g_gains = jnp.pad(jnp.concatenate([g_gq.reshape(N_DEV, n_gq), g_gkv.reshape(N_DEV, n_gkv)], axis=1)[:, None, :],
                      ((0, 0), (0, 7), (0, 128 - n_gq - n_gkv)))
    p_in, p_uq, p_uk, p_gains = _exchange(
        [], [bf(by_dev(g_in)), bf(by_dev(g_uq.reshape(MLA_QR, -1))),
             bf(by_dev(g_uk.transpose(1, 0, 2).reshape(MLA_C, -1))), g_gains], name="exchange_mla_in_grads")

    res = {}

    def adam(name, parts, names, to_slab, from_slab):
        out = _adamw(parts, to_slab(W), to_slab(M), to_slab(V), name="adamw_" + name)
        for k in range(4):
            for n, a in zip(names, from_slab(out[k])):
                res[(k, n)] = a.reshape(W[n].shape)

    one = lambda n: (lambda d: slab(d, n))
    adam("swa_w_o", p_o2, ["swa_w_o"], one("swa_w_o"), lambda s: [s])
    dq_cols = SWA_QH * SWA_D
    adam("swa_qkv", p_qkv, ["swa_w_q", "kv_w_shared"],
         lambda d: jnp.concatenate([slab(d, "swa_w_q"), slab(d, "kv_w_shared")], axis=1),
         lambda s: [s[:, :dq_cols], s[:, dq_cols:]])
    layers_as_rows = lambda a: a.reshape((-1,) + a.shape[-1:])
    adam("mlp_w_up", p_up.reshape(N_DEV, -1, p_up.shape[-1]), ["mlp_w_up"],
         lambda d: layers_as_rows(d["mlp_w_up"]), lambda s: [s])
    adam("mlp_w_down", p_dn.reshape(N_DEV, -1, p_dn.shape[-1]), ["mlp_w_down"],
         lambda d: layers_as_rows(d["mlp_w_down"]), lambda s: [s])
    adam("mla_w_o", p_o, ["mla_w_o"], one("mla_w_o"), lambda s: [s])
    adam("mla_w_uv", p_uv, ["mla_w_uv"], one("mla_w_uv"), lambda s: [s])
    adam("mla_w_in", p_in, ["mla_w_in"], one("mla_w_in"), lambda s: [s])
    adam("mla_w_uq", p_uq, ["mla_w_uq"], one("mla_w_uq"), lambda s: [s])
    adam("mla_w_uk", p_uk, ["mla_w_uk"], one("mla_w_uk"), lambda s: [s])
    small, loss = _adamw_small(r_all, p_gains, W, M, V)
    res.update(small)
    loss = loss.reshape(())
    return (loss, grad_x.reshape(x.shape), *[res[(k, n)] for k in range(4) for n in WEIGHTS])
```
